```python
import math
import jax, jax.numpy as jnp
from jax import lax
import numpy as np

D_MODEL = 1024
BATCH = 8
SEQ = 4096
DEPTH = 4

N_A = DEPTH // 2
N_B = DEPTH - N_A
POOL_WINDOWS = (2, 4, 8, 16)
N_POOL_GROUPS = len(POOL_WINDOWS)
GROUP_CH = D_MODEL // N_POOL_GROUPS
HEAD_DIM = 64
N_Q_HEADS = D_MODEL // HEAD_DIM
N_KV_HEADS = 4
Q_PER_KV = N_Q_HEADS // N_KV_HEADS
WINDOW = 128
BLOCK = 128
ROPE_THETA = 10000.0
ATTN_SCALE = 1.0 / math.sqrt(HEAD_DIM)
NEG_INF = -1e30
D_FF = 2816
CONV_WIDTH = 3
RMS_EPS = 1e-6

kernel_name = "yoco_pool_swa_sink_hybrid"


def rms_norm(x, g):
    xf = x.astype(jnp.float32)
    y = xf * lax.rsqrt(jnp.mean(xf * xf, axis=-1, keepdims=True) + RMS_EPS)
    return (y * g.astype(jnp.float32)).astype(x.dtype)


def pool_mixer(h, w_pool, scale):
    B, S, D = h.shape
    hf = h.astype(jnp.float32)
    csum = jnp.concatenate([jnp.zeros((B, 1, D), jnp.float32), jnp.cumsum(hf, axis=1)], axis=1)
    t = jnp.arange(1, S + 1)
    diffs = []
    for gi, w in enumerate(POOL_WINDOWS):
        sl = slice(gi * GROUP_CH, (gi + 1) * GROUP_CH)
        lo = jnp.maximum(t - w, 0)
        cnt = jnp.minimum(t, w).astype(jnp.float32)
        mean = (csum[:, 1:, sl] - csum[:, lo, sl]) / cnt[None, :, None]
        diffs.append(mean - hf[..., sl])
    d = jnp.stack(diffs, axis=2).astype(h.dtype)
    y = jnp.einsum('bsgc,gcd->bsgd', d, w_pool).reshape(B, S, D)
    return y * scale


def conv_glu_ffn(h, w_in, conv_w, conv_b, w_out):
    S = h.shape[1]
    u = h @ w_in
    up = jnp.pad(u, ((0, 0), (CONV_WIDTH - 1, 0), (0, 0)))
    u = sum(conv_w[k] * up[:, k:k + S] for k in range(CONV_WIDTH)) + conv_b
    gate, val = jnp.split(u, 2, axis=-1)
    return (jax.nn.gelu(gate, approximate=True) * val) @ w_out


def rope(x, cos, sin):
    xf = x.astype(jnp.float32)
    x1, x2 = jnp.split(xf, 2, axis=-1)
    return jnp.concatenate([x1 * cos - x2 * sin, x2 * cos + x1 * sin], axis=-1).astype(x.dtype)


def rope_tables(positions):
    inv_freq = 1.0 / (ROPE_THETA ** (jnp.arange(0, HEAD_DIM, 2, dtype=jnp.float32) / HEAD_DIM))
    ang = positions.astype(jnp.float32)[..., None] * inv_freq
    return jnp.cos(ang)[:, :, None, :], jnp.sin(ang)[:, :, None, :]


def band_blocks(t):
    B, S = t.shape[:2]
    nb = S // BLOCK
    tb = t.reshape(B, nb, BLOCK, N_KV_HEADS, HEAD_DIM)
    prev = jnp.pad(tb, ((0, 0), (1, 0), (0, 0), (0, 0), (0, 0)))[:, :-1]
    return jnp.concatenate([prev, tb], axis=2).astype(jnp.float32)


def swa_sink_attention(q, kk, vv, sinks):
    B, S = q.shape[:2]
    nb = S // BLOCK
    qb = q.reshape(B, nb, BLOCK, N_KV_HEADS, Q_PER_KV, HEAD_DIM).astype(jnp.float32) * ATTN_SCALE
    s = jnp.einsum('bnqhgd,bnkhd->bnhgqk', qb, kk)
    qi = jnp.arange(BLOCK)[:, None]
    kj = jnp.arange(2 * BLOCK)[None, :]
    rel = BLOCK + qi - kj
    blk = jnp.arange(nb)[:, None, None]
    valid = (rel >= 0) & (rel < WINDOW) & (blk * BLOCK + kj - BLOCK >= 0)
    s = jnp.where(valid[None, :, None, None], s, NEG_INF)
    sink = sinks.astype(jnp.float32).reshape(N_KV_HEADS, Q_PER_KV)[None, None, :, :, None, None]
    m = jnp.maximum(jnp.max(s, axis=-1, keepdims=True), sink)
    p = jnp.exp(s - m)
    denom = jnp.sum(p, axis=-1) + jnp.exp(sink - m)[..., 0]
    o = jnp.einsum('bnhgqk,bnkhd->bnhgqd', p, vv) / denom[..., None]
    o = o.transpose(0, 1, 4, 2, 3, 5).reshape(B, S, N_Q_HEADS * HEAD_DIM)
    return o.astype(q.dtype)


def _fwd_setup_inputs(seed: int = 0) -> dict:
    key = jax.random.key(seed)
    ks = jax.random.split(key, 20)
    f32 = jnp.float32
    D, F = D_MODEL, D_FF
    HQD, HKVD = N_Q_HEADS * HEAD_DIM, N_KV_HEADS * HEAD_DIM

    def gain(k, shape):
        return 1.0 + 0.05 * jax.random.normal(k, shape, f32)

    x = jax.random.normal(ks[0], (BATCH, SEQ, D), f32)
    positions = jnp.broadcast_to(jnp.arange(SEQ, dtype=jnp.int32)[None, :], (BATCH, SEQ))
    return {
        "x": x,
        "positions": positions,
        "mix_pre_g": gain(ks[1], (DEPTH, D)),
        "mix_post_g": gain(ks[2], (DEPTH, D)),
        "pool_w": jax.random.normal(ks[3], (N_A, N_POOL_GROUPS, GROUP_CH, GROUP_CH), f32) * GROUP_CH ** -0.5,
        "pool_scale": 1.0 + 0.1 * jax.random.normal(ks[4], (N_A, D), f32),
        "kv_norm_g": gain(ks[5], (D,)),
        "w_kv": jax.random.normal(ks[6], (D, 2 * HKVD), f32) * D ** -0.5,
        "w_q": jax.random.normal(ks[7], (N_B, D, HQD), f32) * D ** -0.5,
        "w_o": jax.random.normal(ks[8], (N_B, HQD, D), f32) * HQD ** -0.5,
        "sinks": jax.random.normal(ks[9], (N_B, N_Q_HEADS), f32),
        "ffn_pre_g": gain(ks[10], (DEPTH, D)),
        "ffn_post_g": gain(ks[11], (DEPTH, D)),
        "ffn_w_in": jax.random.normal(ks[12], (DEPTH, D, 2 * F), f32) * D ** -0.5,
        "ffn_conv_w": jax.random.normal(ks[13], (DEPTH, CONV_WIDTH, 2 * F), f32) * CONV_WIDTH ** -0.5,
        "ffn_conv_b": 0.01 * jax.random.normal(ks[14], (DEPTH, 2 * F), f32),
        "ffn_w_out": jax.random.normal(ks[15], (DEPTH, F, D), f32) * F ** -0.5,
    }


def _fwd_reference(x, positions, mix_pre_g, mix_post_g, pool_w, pool_scale, kv_norm_g, w_kv,
              w_q, w_o, sinks, ffn_pre_g, ffn_post_g, ffn_w_in, ffn_conv_w, ffn_conv_b, ffn_w_out):
    B, S, D = x.shape
    cos, sin = rope_tables(positions)
    kk = vv = None
    for layer in range(DEPTH):
        h = rms_norm(x, mix_pre_g[layer])
        if layer < N_A:
            m = pool_mixer(h, pool_w[layer], pool_scale[layer])
        else:
            if layer == N_A:
                hkv = rms_norm(x, kv_norm_g)
                kv = (hkv @ w_kv).reshape(B, S, 2, N_KV_HEADS, HEAD_DIM)
                k_shared = rope(kv[:, :, 0], cos, sin)
                kk, vv = band_blocks(k_shared), band_blocks(kv[:, :, 1])
            j = layer - N_A
            q = rope((h @ w_q[j]).reshape(B, S, N_Q_HEADS, HEAD_DIM), cos, sin)
            m = swa_sink_attention(q, kk, vv, sinks[j]) @ w_o[j]
        x = x + rms_norm(m, mix_post_g[layer])
        f = conv_glu_ffn(rms_norm(x, ffn_pre_g[layer]), ffn_w_in[layer], ffn_conv_w[layer],
                         ffn_conv_b[layer], ffn_w_out[layer])
        x = x + rms_norm(f, ffn_post_g[layer])
    return x


import jax as _jax
import jax.numpy as _jnp

TWIN_FORMAT = 'train_step'
FWD_PARAMS = ['x', 'positions', 'mix_pre_g', 'mix_post_g', 'pool_w', 'pool_scale', 'kv_norm_g', 'w_kv', 'w_q', 'w_o', 'sinks', 'ffn_pre_g', 'ffn_post_g', 'ffn_w_in', 'ffn_conv_w', 'ffn_conv_b', 'ffn_w_out']
TWIN_WEIGHTS = ['mix_pre_g', 'mix_post_g', 'pool_w', 'pool_scale', 'kv_norm_g', 'w_kv', 'w_q', 'w_o', 'sinks', 'ffn_pre_g', 'ffn_post_g', 'ffn_w_in', 'ffn_conv_w', 'ffn_conv_b', 'ffn_w_out']
TWIN_DIFF_INPUT = 'x'
TWIN_INPUTS = ['x', 'positions', 'mix_pre_g', 'mix_post_g', 'pool_w', 'pool_scale', 'kv_norm_g', 'w_kv', 'w_q', 'w_o', 'sinks', 'ffn_pre_g', 'ffn_post_g', 'ffn_w_in', 'ffn_conv_w', 'ffn_conv_b', 'ffn_w_out', 'loss_target', 'm_mix_pre_g', 'm_mix_post_g', 'm_pool_w', 'm_pool_scale', 'm_kv_norm_g', 'm_w_kv', 'm_w_q', 'm_w_o', 'm_sinks', 'm_ffn_pre_g', 'm_ffn_post_g', 'm_ffn_w_in', 'm_ffn_conv_w', 'm_ffn_conv_b', 'm_ffn_w_out', 'v_mix_pre_g', 'v_mix_post_g', 'v_pool_w', 'v_pool_scale', 'v_kv_norm_g', 'v_w_kv', 'v_w_q', 'v_w_o', 'v_sinks', 'v_ffn_pre_g', 'v_ffn_post_g', 'v_ffn_w_in', 'v_ffn_conv_w', 'v_ffn_conv_b', 'v_ffn_w_out']
TWIN_OUTPUTS = ['loss', 'grad_x', 'grad_mix_pre_g', 'grad_mix_post_g', 'grad_pool_w', 'grad_pool_scale', 'grad_kv_norm_g', 'grad_w_kv', 'grad_w_q', 'grad_w_o', 'grad_sinks', 'grad_ffn_pre_g', 'grad_ffn_post_g', 'grad_ffn_w_in', 'grad_ffn_conv_w', 'grad_ffn_conv_b', 'grad_ffn_w_out', 'delta_mix_pre_g', 'delta_mix_post_g', 'delta_pool_w', 'delta_pool_scale', 'delta_kv_norm_g', 'delta_w_kv', 'delta_w_q', 'delta_w_o', 'delta_sinks', 'delta_ffn_pre_g', 'delta_ffn_post_g', 'delta_ffn_w_in', 'delta_ffn_conv_w', 'delta_ffn_conv_b', 'delta_ffn_w_out', 'new_m_mix_pre_g', 'new_m_mix_post_g', 'new_m_pool_w', 'new_m_pool_scale', 'new_m_kv_norm_g', 'new_m_w_kv', 'new_m_w_q', 'new_m_w_o', 'new_m_sinks', 'new_m_ffn_pre_g', 'new_m_ffn_post_g', 'new_m_ffn_w_in', 'new_m_ffn_conv_w', 'new_m_ffn_conv_b', 'new_m_ffn_w_out', 'new_v_mix_pre_g', 'new_v_mix_post_g', 'new_v_pool_w', 'new_v_pool_scale', 'new_v_kv_norm_g', 'new_v_w_kv', 'new_v_w_q', 'new_v_w_o', 'new_v_sinks', 'new_v_ffn_pre_g', 'new_v_ffn_post_g', 'new_v_ffn_w_in', 'new_v_ffn_conv_w', 'new_v_ffn_conv_b', 'new_v_ffn_w_out']
TWIN_LEAF_KINDS = {'loss': 'loss', 'grad_x': 'grad_x', 'grad_mix_pre_g': 'grad_w', 'grad_mix_post_g': 'grad_w', 'grad_pool_w': 'grad_w', 'grad_pool_scale': 'grad_w', 'grad_kv_norm_g': 'grad_w', 'grad_w_kv': 'grad_w', 'grad_w_q': 'grad_w', 'grad_w_o': 'grad_w', 'grad_sinks': 'grad_w', 'grad_ffn_pre_g': 'grad_w', 'grad_ffn_post_g': 'grad_w', 'grad_ffn_w_in': 'grad_w', 'grad_ffn_conv_w': 'grad_w', 'grad_ffn_conv_b': 'grad_w', 'grad_ffn_w_out': 'grad_w', 'delta_mix_pre_g': 'delta_w', 'delta_mix_post_g': 'delta_w', 'delta_pool_w': 'delta_w', 'delta_pool_scale': 'delta_w', 'delta_kv_norm_g': 'delta_w', 'delta_w_kv': 'delta_w', 'delta_w_q': 'delta_w', 'delta_w_o': 'delta_w', 'delta_sinks': 'delta_w', 'delta_ffn_pre_g': 'delta_w', 'delta_ffn_post_g': 'delta_w', 'delta_ffn_w_in': 'delta_w', 'delta_ffn_conv_w': 'delta_w', 'delta_ffn_conv_b': 'delta_w', 'delta_ffn_w_out': 'delta_w', 'new_m_mix_pre_g': 'new_m', 'new_m_mix_post_g': 'new_m', 'new_m_pool_w': 'new_m', 'new_m_pool_scale': 'new_m', 'new_m_kv_norm_g': 'new_m', 'new_m_w_kv': 'new_m', 'new_m_w_q': 'new_m', 'new_m_w_o': 'new_m', 'new_m_sinks': 'new_m', 'new_m_ffn_pre_g': 'new_m', 'new_m_ffn_post_g': 'new_m', 'new_m_ffn_w_in': 'new_m', 'new_m_ffn_conv_w': 'new_m', 'new_m_ffn_conv_b': 'new_m', 'new_m_ffn_w_out': 'new_m', 'new_v_mix_pre_g': 'new_v', 'new_v_mix_post_g': 'new_v', 'new_v_pool_w': 'new_v', 'new_v_pool_scale': 'new_v', 'new_v_kv_norm_g': 'new_v', 'new_v_w_kv': 'new_v', 'new_v_w_q': 'new_v', 'new_v_w_o': 'new_v', 'new_v_sinks': 'new_v', 'new_v_ffn_pre_g': 'new_v', 'new_v_ffn_post_g': 'new_v', 'new_v_ffn_w_in': 'new_v', 'new_v_ffn_conv_w': 'new_v', 'new_v_ffn_conv_b': 'new_v', 'new_v_ffn_w_out': 'new_v'}


def _forward(args):
    return _fwd_reference(*[args[k] for k in FWD_PARAMS])


def _output_shape():
    out = _jax.eval_shape(lambda: _forward(_fwd_setup_inputs(0)))
    return out.shape, out.dtype

N_MICROBATCH = 1
ADAM_LR = 0.001
ADAM_B1 = 0.9
ADAM_B2 = 0.999
ADAM_EPS = 1e-08
ADAM_WD = 0.01
ADAM_STEP = 10
PER_EXAMPLE_BATCH_AXIS = {'x': 0, 'positions': 0, 'loss_target': 0}
SHARED_INPUTS = []
_WEIGHT_DTYPES = {'mix_pre_g': _jnp.float32, 'mix_post_g': _jnp.float32, 'pool_w': _jnp.float32, 'pool_scale': _jnp.float32, 'kv_norm_g': _jnp.float32, 'w_kv': _jnp.float32, 'w_q': _jnp.float32, 'w_o': _jnp.float32, 'sinks': _jnp.float32, 'ffn_pre_g': _jnp.float32, 'ffn_post_g': _jnp.float32, 'ffn_w_in': _jnp.float32, 'ffn_conv_w': _jnp.float32, 'ffn_conv_b': _jnp.float32, 'ffn_w_out': _jnp.float32}
MOMENT_SCALE = {'mix_pre_g': 2.190792e+00, 'mix_post_g': 3.228388e+01, 'pool_w': 2.742379e+00, 'pool_scale': 3.854729e+00, 'kv_norm_g': 1.812671e+00, 'w_kv': 2.467736e+00, 'w_q': 8.264120e-01, 'w_o': 9.744228e-01, 'sinks': 2.432731e-01, 'ffn_pre_g': 1.588200e+00, 'ffn_post_g': 3.185568e+01, 'ffn_w_in': 6.749360e-01, 'ffn_conv_w': 6.798541e-01, 'ffn_conv_b': 2.407919e+00, 'ffn_w_out': 1.181231e+00}


def _to_microbatches(a, axis):
    t = _jnp.moveaxis(a, axis, 0)
    t = t.reshape((N_MICROBATCH, t.shape[0] // N_MICROBATCH) + t.shape[1:])
    return _jnp.moveaxis(t, 1, axis + 1)


def setup_inputs(seed: int = 0) -> dict:
    inp = _fwd_setup_inputs(seed)
    key = _jax.random.fold_in(_jax.random.key(seed), 7919)
    shape, _ = _output_shape()
    out = dict(inp)
    out["loss_target"] = _jax.random.normal(_jax.random.fold_in(key, 0), shape, _jnp.float32)
    for i, name in enumerate(TWIN_WEIGHTS):
        w = inp[name].astype(_jnp.float32)
        if MOMENT_SCALE is None:
            s = _jnp.sqrt(_jnp.mean(_jnp.square(w)) + 1e-30)
        else:
            s = MOMENT_SCALE[name]
        km, kv = _jax.random.split(_jax.random.fold_in(key, i + 1))
        out[name] = w
        out["m_" + name] = s * _jax.random.normal(km, w.shape, _jnp.float32)
        out["v_" + name] = (s * s) * _jax.random.uniform(kv, w.shape, _jnp.float32, 0.5, 1.5)
    if N_MICROBATCH > 1:
        for name, axis in PER_EXAMPLE_BATCH_AXIS.items():
            out[name] = _to_microbatches(out[name], axis)
    return {'x': out['x'], 'positions': out['positions'], 'mix_pre_g': out['mix_pre_g'], 'mix_post_g': out['mix_post_g'], 'pool_w': out['pool_w'], 'pool_scale': out['pool_scale'], 'kv_norm_g': out['kv_norm_g'], 'w_kv': out['w_kv'], 'w_q': out['w_q'], 'w_o': out['w_o'], 'sinks': out['sinks'], 'ffn_pre_g': out['ffn_pre_g'], 'ffn_post_g': out['ffn_post_g'], 'ffn_w_in': out['ffn_w_in'], 'ffn_conv_w': out['ffn_conv_w'], 'ffn_conv_b': out['ffn_conv_b'], 'ffn_w_out': out['ffn_w_out'], 'loss_target': out['loss_target'], 'm_mix_pre_g': out['m_mix_pre_g'], 'm_mix_post_g': out['m_mix_post_g'], 'm_pool_w': out['m_pool_w'], 'm_pool_scale': out['m_pool_scale'], 'm_kv_norm_g': out['m_kv_norm_g'], 'm_w_kv': out['m_w_kv'], 'm_w_q': out['m_w_q'], 'm_w_o': out['m_w_o'], 'm_sinks': out['m_sinks'], 'm_ffn_pre_g': out['m_ffn_pre_g'], 'm_ffn_post_g': out['m_ffn_post_g'], 'm_ffn_w_in': out['m_ffn_w_in'], 'm_ffn_conv_w': out['m_ffn_conv_w'], 'm_ffn_conv_b': out['m_ffn_conv_b'], 'm_ffn_w_out': out['m_ffn_w_out'], 'v_mix_pre_g': out['v_mix_pre_g'], 'v_mix_post_g': out['v_mix_post_g'], 'v_pool_w': out['v_pool_w'], 'v_pool_scale': out['v_pool_scale'], 'v_kv_norm_g': out['v_kv_norm_g'], 'v_w_kv': out['v_w_kv'], 'v_w_q': out['v_w_q'], 'v_w_o': out['v_w_o'], 'v_sinks': out['v_sinks'], 'v_ffn_pre_g': out['v_ffn_pre_g'], 'v_ffn_post_g': out['v_ffn_post_g'], 'v_ffn_w_in': out['v_ffn_w_in'], 'v_ffn_conv_w': out['v_ffn_conv_w'], 'v_ffn_conv_b': out['v_ffn_conv_b'], 'v_ffn_w_out': out['v_ffn_w_out']}


def _loss(weights, diff, rest, loss_target):
    with _jax.named_scope("forward"):
        args = {**rest, TWIN_DIFF_INPUT: diff, **{k: w.astype(_WEIGHT_DTYPES[k]) for k, w in weights.items()}}
        y = _forward(args)
    with _jax.named_scope("loss_head"):
        err = _jnp.square(y.astype(_jnp.float32) - loss_target)
        return 0.5 * _jnp.sum(_jnp.mean(err, axis=-1)) if err.ndim else 0.5 * err


def _adamw(w, g, m, v):
    m = ADAM_B1 * m + (1.0 - ADAM_B1) * g
    v = ADAM_B2 * v + (1.0 - ADAM_B2) * _jnp.square(g)
    m_hat = m / (1.0 - ADAM_B1 ** ADAM_STEP)
    v_hat = v / (1.0 - ADAM_B2 ** ADAM_STEP)
    delta = -ADAM_LR * (m_hat / (_jnp.sqrt(v_hat) + ADAM_EPS) + ADAM_WD * w)
    return delta, m, v


def reference(x, positions, mix_pre_g, mix_post_g, pool_w, pool_scale, kv_norm_g, w_kv, w_q, w_o, sinks, ffn_pre_g, ffn_post_g, ffn_w_in, ffn_conv_w, ffn_conv_b, ffn_w_out, loss_target, m_mix_pre_g, m_mix_post_g, m_pool_w, m_pool_scale, m_kv_norm_g, m_w_kv, m_w_q, m_w_o, m_sinks, m_ffn_pre_g, m_ffn_post_g, m_ffn_w_in, m_ffn_conv_w, m_ffn_conv_b, m_ffn_w_out, v_mix_pre_g, v_mix_post_g, v_pool_w, v_pool_scale, v_kv_norm_g, v_w_kv, v_w_q, v_w_o, v_sinks, v_ffn_pre_g, v_ffn_post_g, v_ffn_w_in, v_ffn_conv_w, v_ffn_conv_b, v_ffn_w_out):
    given = dict(x=x, positions=positions, mix_pre_g=mix_pre_g, mix_post_g=mix_post_g, pool_w=pool_w, pool_scale=pool_scale, kv_norm_g=kv_norm_g, w_kv=w_kv, w_q=w_q, w_o=w_o, sinks=sinks, ffn_pre_g=ffn_pre_g, ffn_post_g=ffn_post_g, ffn_w_in=ffn_w_in, ffn_conv_w=ffn_conv_w, ffn_conv_b=ffn_conv_b, ffn_w_out=ffn_w_out, loss_target=loss_target, m_mix_pre_g=m_mix_pre_g, m_mix_post_g=m_mix_post_g, m_pool_w=m_pool_w, m_pool_scale=m_pool_scale, m_kv_norm_g=m_kv_norm_g, m_w_kv=m_w_kv, m_w_q=m_w_q, m_w_o=m_w_o, m_sinks=m_sinks, m_ffn_pre_g=m_ffn_pre_g, m_ffn_post_g=m_ffn_post_g, m_ffn_w_in=m_ffn_w_in, m_ffn_conv_w=m_ffn_conv_w, m_ffn_conv_b=m_ffn_conv_b, m_ffn_w_out=m_ffn_w_out, v_mix_pre_g=v_mix_pre_g, v_mix_post_g=v_mix_post_g, v_pool_w=v_pool_w, v_pool_scale=v_pool_scale, v_kv_norm_g=v_kv_norm_g, v_w_kv=v_w_kv, v_w_q=v_w_q, v_w_o=v_w_o, v_sinks=v_sinks, v_ffn_pre_g=v_ffn_pre_g, v_ffn_post_g=v_ffn_post_g, v_ffn_w_in=v_ffn_w_in, v_ffn_conv_w=v_ffn_conv_w, v_ffn_conv_b=v_ffn_conv_b, v_ffn_w_out=v_ffn_w_out)
    weights = {n: given[n] for n in TWIN_WEIGHTS}
    shared = {n: given[n] for n in SHARED_INPUTS}
    per_example = {n: given[n] for n in ['x', 'positions']}
    grad_fn = _jax.value_and_grad(_loss, argnums=(0, 1))

    def one_microbatch(ex, loss_target):
        ex = dict(ex)
        diff = ex.pop(TWIN_DIFF_INPUT)
        return grad_fn(weights, diff, {**shared, **ex}, loss_target)

    if N_MICROBATCH == 1:
        loss, (grad_w, grad_x) = one_microbatch(per_example, given["loss_target"])
    else:
        def body(carry, xs):
            loss_sum, grad_sum = carry
            l_k, (gw_k, gx_k) = one_microbatch(xs[0], xs[1])
            with _jax.named_scope("update"):
                return (loss_sum + l_k, _jax.tree.map(_jnp.add, grad_sum, gw_k)), gx_k

        init = (_jnp.zeros((), _jnp.float32), _jax.tree.map(_jnp.zeros_like, weights))
        (loss, grad_w), grad_x = _jax.lax.scan(body, init, (per_example, given["loss_target"]))
    with _jax.named_scope("update"):
        delta_w, new_m, new_v = {}, {}, {}
        for n in TWIN_WEIGHTS:
            delta_w[n], new_m[n], new_v[n] = _adamw(weights[n], grad_w[n], given["m_" + n], given["v_" + n])
    return (loss, grad_x, *[grad_w[n] for n in TWIN_WEIGHTS], *[delta_w[n] for n in TWIN_WEIGHTS],
            *[new_m[n] for n in TWIN_WEIGHTS], *[new_v[n] for n in TWIN_WEIGHTS])
```

```python
import functools
import math

import jax
import jax.numpy as jnp
from jax import lax
from jax.experimental import pallas as pl
from jax.experimental.pallas import tpu as pltpu

F32 = jnp.float32
_MXU_DTYPE = jnp.bfloat16
_ACT_DTYPE = jnp.bfloat16
_WIRE_DTYPE = jnp.bfloat16

N_DEV = 8
POOL_WINDOWS = (2, 4, 8, 16)
POOL_HALO = 16
HEAD_DIM = 64
N_KV_HEADS = 4
WINDOW = 128
BLOCK = 128
LANES = 128
ROPE_THETA = 10000.0
ATTN_SCALE = 1.0 / math.sqrt(HEAD_DIM)
NEG_INF = -1e30
RMS_EPS = 1e-6
CONV_HALO = 8
ADAM_LR = 0.001
ADAM_B1 = 0.9
ADAM_B2 = 0.999
ADAM_EPS = 1e-08
ADAM_WD = 0.01
ADAM_STEP = 10
VMEM_LIMIT = 56 * 1024 * 1024
MESH = pl.DeviceIdType.MESH
AXES = ("x", "y", "c")


def _params(n_axes=1, vmem=VMEM_LIMIT):
    return pltpu.CompilerParams(dimension_semantics=("arbitrary",) * n_axes, vmem_limit_bytes=vmem)


def _resident(shape, index):
    return pl.BlockSpec(shape, lambda *_: index, pipeline_mode=pl.Buffered(1))


def _const(shape, index=None):
    index = (0,) * len(shape) if index is None else index
    return pl.BlockSpec(shape, lambda *_: index)


def _rows(tm, cols):
    return pl.BlockSpec((tm, cols), lambda i: (i, 0))


def _row_tile(t):
    for tm in (256, 128, 64, 32, 16, 8):
        if t % tm == 0:
            return tm
    raise ValueError(f"sequence length {t} is not a multiple of 8")


def _mm(a, b):
    return jnp.dot(a.astype(_MXU_DTYPE), b.astype(_MXU_DTYPE), preferred_element_type=F32)


def _mm_tb(a, b):
    return lax.dot_general(a.astype(_MXU_DTYPE), b.astype(_MXU_DTYPE), (((1,), (1,)), ((), ())),
                           preferred_element_type=F32)


def _mm_ta(a, b):
    return lax.dot_general(a.astype(_MXU_DTYPE), b.astype(_MXU_DTYPE), (((0,), (0,)), ((), ())),
                           preferred_element_type=F32)


def _rms_r(x):
    return lax.rsqrt(jnp.mean(x * x, axis=-1, keepdims=True) + RMS_EPS)


def _rms_fwd(x, g):
    return (x * _rms_r(x)) * g


def _rms_bwd(x, g, dy):
    r = _rms_r(x)
    xh = x * r
    dg = jnp.sum(dy * xh, axis=0, keepdims=True)
    dxh = dy * g
    dx = r * (dxh - xh * jnp.mean(dxh * xh, axis=-1, keepdims=True))
    return dx, dg


_GELU_C = math.sqrt(2.0 / math.pi)


def _gelu_parts(z):
    z2 = z * z
    t = jnp.tanh(_GELU_C * (z + 0.044715 * (z2 * z)))
    cdf = 0.5 * (1.0 + t)
    dz = cdf + z * (0.5 * (1.0 - t * t)) * (_GELU_C * (1.0 + (3 * 0.044715) * z2))
    return cdf, dz


def _lane_iota(shape):
    return lax.broadcasted_iota(jnp.int32, shape, len(shape) - 1)


def _rope_partner(xb):
    first = (_lane_iota(xb.shape) & 32) == 0
    return jnp.where(first, pltpu.roll(xb, LANES - 32, 1), pltpu.roll(xb, 32, 1))


def _rope_fwd(xb, cos, ssin):
    return xb * cos + _rope_partner(xb) * ssin


def _rope_bwd(dyb, cos, ssin):
    return dyb * cos - _rope_partner(dyb) * ssin


def _low_half(shape):
    return (_lane_iota(shape) & 64) == 0


def _pool_fwd(x, pre_g, post_g, w, scale, layer, name):
    t, d_model = x.shape
    tm = _row_tile(t)
    n_groups, gc = w.shape[1], w.shape[2]

    def body(x_ref, pre_ref, post_ref, w_ref, sc_ref, xo_ref, d_ref, yu_ref, hbuf):
        i = pl.program_id(0)

        @pl.when(i == 0)
        def _():
            hbuf[pl.ds(0, POOL_HALO), :] = jnp.zeros((POOL_HALO, d_model), F32)

        xv = x_ref[...]
        hbuf[pl.ds(POOL_HALO, tm), :] = _rms_fwd(xv, pre_ref[...])
        tok = i * tm + lax.broadcasted_iota(jnp.int32, (tm, 1), 0)
        yus = []
        for gi, wnd in enumerate(POOL_WINDOWS):
            cols = pl.ds(gi * gc, gc)
            h = hbuf[pl.ds(POOL_HALO, tm), cols]
            acc = h
            for k in range(1, wnd):
                acc = acc + hbuf[pl.ds(POOL_HALO - k, tm), cols]
            cnt = jnp.minimum(tok + 1, wnd).astype(F32)
            dg = acc / cnt - h
            d_ref[:, cols] = dg.astype(d_ref.dtype)
            yus.append(_mm(dg, w_ref[gi]))
        hbuf[pl.ds(0, POOL_HALO), :] = hbuf[pl.ds(tm, POOL_HALO), :]
        yu = jnp.concatenate(yus, axis=1)
        yu_ref[...] = yu
        xo_ref[...] = xv + _rms_fwd(yu * sc_ref[...], post_ref[...])

    return pl.pallas_call(
        body, name=name, grid=(t // tm,),
        out_shape=(jax.ShapeDtypeStruct((t, d_model), F32), jax.ShapeDtypeStruct((t, d_model), _ACT_DTYPE),
                   jax.ShapeDtypeStruct((t, d_model), F32)),
        in_specs=[_rows(tm, d_model), _const((None, 1, d_model), (layer, 0, 0)), _const((None, 1, d_model), (layer, 0, 0)),
                  _const((None, n_groups, gc, gc), (layer, 0, 0, 0)), _const((None, 1, d_model), (layer, 0, 0))],
        out_specs=(_rows(tm, d_model), _rows(tm, d_model), _rows(tm, d_model)),
        scratch_shapes=[pltpu.VMEM((POOL_HALO + tm, d_model), F32)],
        compiler_params=_params(),
    )(x, pre_g, post_g, w, scale)


def _pool_bwd(dx, x, pre_g, post_g, d, yu, w, scale, layer, name):
    t, d_model = x.shape
    tm = _row_tile(t)
    nt = t // tm
    n_groups, gc = w.shape[1], w.shape[2]
    rev = lambda i: (nt - 1 - i, 0)
    rows = pl.BlockSpec((tm, d_model), rev)

    def body(dx_ref, x_ref, pre_ref, post_ref, d_ref, yu_ref, w_ref, sc_ref,
             dxi_ref, dyu_ref, dsc_ref, dpre_ref, dpost_ref, zbuf):
        i = pl.program_id(0)

        @pl.when(i == 0)
        def _():
            zbuf[pl.ds(tm, POOL_HALO), :] = jnp.zeros((POOL_HALO, d_model), F32)
            dsc_ref[...] = jnp.zeros_like(dsc_ref)
            dpre_ref[...] = jnp.zeros_like(dpre_ref)
            dpost_ref[...] = jnp.zeros_like(dpost_ref)

        dxo = dx_ref[...]
        yuv = yu_ref[...]
        sc = sc_ref[...]
        dm, dpost = _rms_bwd(yuv * sc, post_ref[...], dxo)
        dpost_ref[...] += dpost
        dsc_ref[...] += jnp.sum(dm * yuv, axis=0, keepdims=True)
        dyu = dm * sc
        dyu_ref[...] = dyu.astype(dyu_ref.dtype)
        tok = (nt - 1 - i) * tm + lax.broadcasted_iota(jnp.int32, (tm, 1), 0)
        dds = []
        for gi, wnd in enumerate(POOL_WINDOWS):
            cols = pl.ds(gi * gc, gc)
            dd = _mm_tb(dyu[:, gi * gc:(gi + 1) * gc], w_ref[gi])
            cnt = jnp.minimum(tok + 1, wnd).astype(F32)
            zbuf[pl.ds(0, tm), cols] = dd / cnt
            dds.append(dd)
        dhs = []
        for gi, wnd in enumerate(POOL_WINDOWS):
            cols = pl.ds(gi * gc, gc)
            acc = zbuf[pl.ds(0, tm), cols]
            for k in range(1, wnd):
                acc = acc + zbuf[pl.ds(k, tm), cols]
            dhs.append(acc - dds[gi])
        zbuf[pl.ds(tm, POOL_HALO), :] = zbuf[pl.ds(0, POOL_HALO), :]
        dh = jnp.concatenate(dhs, axis=1)
        dxp, dpre = _rms_bwd(x_ref[...], pre_ref[...], dh)
        dpre_ref[...] += dpre
        dxi_ref[...] = dxo + dxp

    vec = jax.ShapeDtypeStruct((1, d_model), F32)
    return pl.pallas_call(
        body, name=name, grid=(nt,),
        out_shape=(jax.ShapeDtypeStruct((t, d_model), F32), jax.ShapeDtypeStruct((t, d_model), _ACT_DTYPE), vec, vec, vec),
        in_specs=[rows, rows, _const((None, 1, d_model), (layer, 0, 0)), _const((None, 1, d_model), (layer, 0, 0)), rows, rows,
                  _const((None, n_groups, gc, gc), (layer, 0, 0, 0)), _const((None, 1, d_model), (layer, 0, 0))],
        out_specs=(rows, rows, _const((1, d_model)), _const((1, d_model)), _const((1, d_model))),
        scratch_shapes=[pltpu.VMEM((tm + POOL_HALO, d_model), F32)],
        compiler_params=_params(),
    )(dx, x, pre_g, post_g, d, yu, w, scale)


def _conv_taps(cw_ref, s):
    return [cw_ref[k, pl.ds(s, 1), :] for k in range(3)]


def _ffn_fwd(x, pre_g, post_g, w_in, conv_w, conv_b, w_out, layer, name):
    t, d_model = x.shape
    tm = _row_tile(t)
    fs = w_in.shape[3]
    half = N_DEV // 2

    def body(x_ref, pre_ref, post_ref, win_ref, cw_ref, cb_ref, wout_ref, xo_ref, u_ref, f_ref, carry, ubuf):
        i = pl.program_id(0)

        @pl.when(i == 0)
        def _():
            carry[...] = jnp.zeros_like(carry)

        xv = x_ref[...]
        hf = _rms_fwd(xv, pre_ref[...]).astype(_MXU_DTYPE)
        f = jnp.zeros((tm, d_model), F32)
        for b in range(half):
            ucs = []
            for j, s in enumerate((b, b + half)):
                u = jnp.dot(hf, win_ref[s], preferred_element_type=F32)
                u_ref[s] = u
                ubuf[j, pl.ds(0, CONV_HALO), :] = carry[s]
                ubuf[j, pl.ds(CONV_HALO, tm), :] = u
                carry[s] = ubuf[j, pl.ds(tm, CONV_HALO), :]
                w0, w1, w2 = _conv_taps(cw_ref, s)
                ucs.append(((w0 * ubuf[j, pl.ds(CONV_HALO - 2, tm), :] + w1 * ubuf[j, pl.ds(CONV_HALO - 1, tm), :])
                            + w2 * u) + cb_ref[pl.ds(s, 1), :])
            gate, val = ucs
            cdf, _ = _gelu_parts(gate)
            f = f + _mm((gate * cdf) * val, wout_ref[pl.ds(b * fs, fs), :])
        f_ref[...] = f
        xo_ref[...] = xv + _rms_fwd(f, post_ref[...])

    return pl.pallas_call(
        body, name=name, grid=(t // tm,),
        out_shape=(jax.ShapeDtypeStruct((t, d_model), F32), jax.ShapeDtypeStruct((N_DEV, t, fs), F32),
                   jax.ShapeDtypeStruct((t, d_model), F32)),
        in_specs=[_rows(tm, d_model), _const((None, 1, d_model), (layer, 0, 0)), _const((None, 1, d_model), (layer, 0, 0)),
                  _resident((None, N_DEV, d_model, fs), (layer, 0, 0, 0)), _const((None, 3, N_DEV, fs), (layer, 0, 0, 0)),
                  _const((None, N_DEV, fs), (layer, 0, 0)), _resident((None, half * fs, d_model), (layer, 0, 0))],
        out_specs=(_rows(tm, d_model), pl.BlockSpec((N_DEV, tm, fs), lambda i: (0, i, 0)), _rows(tm, d_model)),
        scratch_shapes=[pltpu.VMEM((N_DEV, CONV_HALO, fs), F32), pltpu.VMEM((2, CONV_HALO + tm, fs), F32)],
        compiler_params=_params(),
    )(x, pre_g, post_g, w_in, conv_w, conv_b, w_out)


def _ffn_bwd_out(dx, f, post_g, u, conv_w, conv_b, w_out, layer, name):
    t, d_model = dx.shape
    tm = _row_tile(t)
    fs = u.shape[2]
    half = N_DEV // 2
    h8 = tm // CONV_HALO

    def body(dx_ref, f_ref, post_ref, u_ref, up_ref, cw_ref, cb_ref, wout_ref,
             duc_ref, g_ref, df_ref, dcw_ref, dcb_ref, dpost_ref, ubuf):
        i = pl.program_id(0)

        @pl.when(i == 0)
        def _():
            dcw_ref[...] = jnp.zeros_like(dcw_ref)
            dcb_ref[...] = jnp.zeros_like(dcb_ref)
            dpost_ref[...] = jnp.zeros_like(dpost_ref)

        df, dpost = _rms_bwd(f_ref[...], post_ref[...], dx_ref[...])
        dpost_ref[...] += dpost
        dfm = df.astype(_MXU_DTYPE)
        df_ref[...] = dfm
        for b in range(half):
            ucs, taps = [], []
            for j, s in enumerate((b, b + half)):
                u0 = u_ref[s]
                ubuf[j, pl.ds(0, CONV_HALO), :] = jnp.where(i == 0, 0.0, up_ref[s])
                ubuf[j, pl.ds(CONV_HALO, tm), :] = u0
                u1 = ubuf[j, pl.ds(CONV_HALO - 1, tm), :]
                u2 = ubuf[j, pl.ds(CONV_HALO - 2, tm), :]
                w0, w1, w2 = _conv_taps(cw_ref, s)
                ucs.append(((w0 * u2 + w1 * u1) + w2 * u0) + cb_ref[pl.ds(s, 1), :])
                taps.append((u2, u1, u0))
            gate, val = ucs
            cdf, dgelu = _gelu_parts(gate)
            ge = gate * cdf
            g_ref[b] = (ge * val).astype(g_ref.dtype)
            dg = _mm_tb(dfm, wout_ref[pl.ds(b * fs, fs), :])
            for j, (s, dd) in enumerate(((b, dg * val * dgelu), (b + half, dg * ge))):
                duc_ref[s] = dd
                dcb_ref[pl.ds(s, 1), :] += jnp.sum(dd, axis=0, keepdims=True)
                for k in range(3):
                    dcw_ref[k, pl.ds(s, 1), :] += jnp.sum(dd * taps[j][k], axis=0, keepdims=True)

    tile3 = pl.BlockSpec((N_DEV, tm, fs), lambda i: (0, i, 0))
    return pl.pallas_call(
        body, name=name, grid=(t // tm,),
        out_shape=(jax.ShapeDtypeStruct((N_DEV, t, fs), F32), jax.ShapeDtypeStruct((half, t, fs), _ACT_DTYPE),
                   jax.ShapeDtypeStruct((t, d_model), _ACT_DTYPE), jax.ShapeDtypeStruct((3, N_DEV, fs), F32),
                   jax.ShapeDtypeStruct((N_DEV, fs), F32), jax.ShapeDtypeStruct((1, d_model), F32)),
        in_specs=[_rows(tm, d_model), _rows(tm, d_model), _const((None, 1, d_model), (layer, 0, 0)), tile3,
                  pl.BlockSpec((N_DEV, CONV_HALO, fs), lambda i: (0, jnp.maximum(i * h8 - 1, 0), 0)),
                  _const((None, 3, N_DEV, fs), (layer, 0, 0, 0)), _const((None, N_DEV, fs), (layer, 0, 0)),
                  _resident((None, half * fs, d_model), (layer, 0, 0))],
        out_specs=(tile3, pl.BlockSpec((half, tm, fs), lambda i: (0, i, 0)), _rows(tm, d_model),
                   _const((3, N_DEV, fs)), _const((N_DEV, fs)), _const((1, d_model))),
        scratch_shapes=[pltpu.VMEM((2, CONV_HALO + tm, fs), F32)],
        compiler_params=_params(),
    )(dx, f, post_g, u, u, conv_w, conv_b, w_out)


def _ffn_bwd_in(dx, x, pre_g, duc, conv_w, w_in, layer, name):
    t, d_model = dx.shape
    tm = _row_tile(t)
    nt = t // tm
    fs = duc.shape[2]
    h8 = tm // CONV_HALO

    def body(dx_ref, x_ref, pre_ref, duc_ref, dn_ref, cw_ref, win_ref, dxi_ref, du_ref, hf_ref, dpre_ref, dbuf):
        i = pl.program_id(0)

        @pl.when(i == 0)
        def _():
            dpre_ref[...] = jnp.zeros_like(dpre_ref)

        xv = x_ref[...]
        pre = pre_ref[...]
        hf_ref[...] = _rms_fwd(xv, pre).astype(hf_ref.dtype)
        dhf = jnp.zeros((tm, d_model), F32)
        for s in range(N_DEV):
            d0 = duc_ref[s]
            dbuf[pl.ds(0, tm), :] = d0
            dbuf[pl.ds(tm, CONV_HALO), :] = jnp.where(i == nt - 1, 0.0, dn_ref[s])
            w0, w1, w2 = _conv_taps(cw_ref, s)
            du = (w2 * d0 + w1 * dbuf[pl.ds(1, tm), :] + w0 * dbuf[pl.ds(2, tm), :]).astype(_MXU_DTYPE)
            du_ref[s] = du
            dhf = dhf + _mm_tb(du, win_ref[s])
        dxp, dpre = _rms_bwd(xv, pre, dhf)
        dpre_ref[...] += dpre
        dxi_ref[...] = dx_ref[...] + dxp

    tile3 = pl.BlockSpec((N_DEV, tm, fs), lambda i: (0, i, 0))
    return pl.pallas_call(
        body, name=name, grid=(nt,),
        out_shape=(jax.ShapeDtypeStruct((t, d_model), F32), jax.ShapeDtypeStruct((N_DEV, t, fs), _ACT_DTYPE),
                   jax.ShapeDtypeStruct((t, d_model), _ACT_DTYPE), jax.ShapeDtypeStruct((1, d_model), F32)),
        in_specs=[_rows(tm, d_model), _rows(tm, d_model), _const((None, 1, d_model), (layer, 0, 0)), tile3,
                  pl.BlockSpec((N_DEV, CONV_HALO, fs), lambda i: (0, jnp.minimum((i + 1) * h8, t // CONV_HALO - 1), 0)),
                  _const((None, 3, N_DEV, fs), (layer, 0, 0, 0)), _resident((None, N_DEV, d_model, fs), (layer, 0, 0, 0))],
        out_specs=(_rows(tm, d_model), tile3, _rows(tm, d_model), _const((1, d_model))),
        scratch_shapes=[pltpu.VMEM((tm + CONV_HALO, fs), F32)],
        compiler_params=_params(),
    )(dx, x, pre_g, duc, duc, conv_w, w_in)


def _tn_matmul(a, b, a_spec, b_spec, n_out, m, n, tk, name, out_dtype):
    nk = a.shape[-2] // tk

    def body(a_ref, b_ref, o_ref, acc):
        k = pl.program_id(1)

        @pl.when(k == 0)
        def _():
            acc[...] = jnp.zeros_like(acc)

        acc[...] += _mm_ta(a_ref[...], b_ref[...])

        @pl.when(k == nk - 1)
        def _():
            o_ref[...] = acc[...].astype(o_ref.dtype)

    return pl.pallas_call(
        body, name=name, grid=(n_out, nk),
        out_shape=jax.ShapeDtypeStruct((n_out, m, n), out_dtype),
        in_specs=[a_spec, b_spec],
        out_specs=pl.BlockSpec((None, m, n), lambda c, k: (c, 0, 0)),
        scratch_shapes=[pltpu.VMEM((m, n), F32)],
        compiler_params=_params(2),
    )(a, b)


def _k_tile(t):
    for tk in (512, 256, 128, 64, 32, 16):
        if t % tk == 0:
            return tk
    raise ValueError(f"sequence length {t} is not a multiple of 16")


def _kv_fwd(x, kv_g, w_kv, cos, ssin, name):
    t, d_model = x.shape
    tm = _row_tile(t)
    kvd = w_kv.shape[1] // 2
    pairs = kvd // LANES

    def body(x_ref, g_ref, w_ref, cos_ref, sin_ref, k_ref, v_ref):
        kv = _mm(_rms_fwd(x_ref[...], g_ref[...]), w_ref[...])
        low = _low_half((tm, LANES))
        for j in range(pairs):
            kb = _rope_fwd(kv[:, j * LANES:(j + 1) * LANES], cos_ref[...], sin_ref[...])
            vb = kv[:, kvd + j * LANES:kvd + (j + 1) * LANES]
            for blk, ref in ((kb, k_ref), (vb, v_ref)):
                sw = pltpu.roll(blk, 64, 1)
                ref[2 * j] = jnp.where(low, blk, sw).astype(ref.dtype)
                ref[2 * j + 1] = jnp.where(low, sw, blk).astype(ref.dtype)

    heads = jax.ShapeDtypeStruct((N_KV_HEADS, t, LANES), _ACT_DTYPE)
    hspec = pl.BlockSpec((N_KV_HEADS, tm, LANES), lambda i: (0, i, 0))
    return pl.pallas_call(
        body, name=name, grid=(t // tm,), out_shape=(heads, heads),
        in_specs=[_rows(tm, d_model), _const((1, d_model)), _const(w_kv.shape), _rows(tm, LANES), _rows(tm, LANES)],
        out_specs=(hspec, hspec), compiler_params=_params(),
    )(x, kv_g, w_kv, cos, ssin)


def _kv_bwd(dx, x, kv_g, w_kv, cos, ssin, dks, dvs, name):
    t, d_model = x.shape
    tm = _row_tile(t)
    kvd = w_kv.shape[1] // 2
    pairs = kvd // LANES
    n_users = len(dks)

    def body(dx_ref, x_ref, g_ref, w_ref, cos_ref, sin_ref, *refs):
        dk_refs, dv_refs = refs[:n_users], refs[n_users:2 * n_users]
        dxi_ref, h_ref, dkv_ref, dg_ref = refs[2 * n_users:]
        dk_ref = functools.reduce(lambda a, b: a + b, [r[...] for r in dk_refs])
        dv_ref = functools.reduce(lambda a, b: a + b, [r[...] for r in dv_refs])
        i = pl.program_id(0)

        @pl.when(i == 0)
        def _():
            dg_ref[...] = jnp.zeros_like(dg_ref)

        xv = x_ref[...]
        g = g_ref[...]
        h_ref[...] = _rms_fwd(xv, g).astype(h_ref.dtype)
        low = _low_half((tm, LANES))
        dks, dvs = [], []
        for j in range(pairs):
            dkb = jnp.where(low, dk_ref[2 * j], dk_ref[2 * j + 1])
            dks.append(_rope_bwd(dkb, cos_ref[...], sin_ref[...]))
            dvs.append(jnp.where(low, dv_ref[2 * j], dv_ref[2 * j + 1]))
        dkv = jnp.concatenate(dks + dvs, axis=1).astype(_MXU_DTYPE)
        dkv_ref[...] = dkv
        dxp, dg = _rms_bwd(xv, g, _mm_tb(dkv, w_ref[...]))
        dg_ref[...] += dg
        dxi_ref[...] = dx_ref[...] + dxp

    hspec = pl.BlockSpec((N_KV_HEADS, tm, LANES), lambda i: (0, i, 0))
    return pl.pallas_call(
        body, name=name, grid=(t // tm,),
        out_shape=(jax.ShapeDtypeStruct((t, d_model), F32), jax.ShapeDtypeStruct((t, d_model), _ACT_DTYPE),
                   jax.ShapeDtypeStruct((t, 2 * kvd), _ACT_DTYPE), jax.ShapeDtypeStruct((1, d_model), F32)),
        in_specs=[_rows(tm, d_model), _rows(tm, d_model), _const((1, d_model)), _const(w_kv.shape),
                  _rows(tm, LANES), _rows(tm, LANES)] + [hspec] * (2 * n_users),
        out_specs=(_rows(tm, d_model), _rows(tm, d_model), _rows(tm, 2 * kvd), _const((1, d_model))),
        compiler_params=_params(),
    )(dx, x, kv_g, w_kv, cos, ssin, *dks, *dvs)


def _q_fwd(x, pre_g, w_q, cos, ssin, layer, j, name):
    t, d_model = x.shape
    tm = _row_tile(t)

    def body(x_ref, g_ref, w_ref, cos_ref, sin_ref, q_ref):
        q = _mm(_rms_fwd(x_ref[...], g_ref[...]), w_ref[...])
        for p in range(d_model // LANES):
            cols = slice(p * LANES, (p + 1) * LANES)
            q_ref[:, cols] = (_rope_fwd(q[:, cols], cos_ref[...], sin_ref[...]) * ATTN_SCALE).astype(q_ref.dtype)

    return pl.pallas_call(
        body, name=name, grid=(t // tm,), out_shape=jax.ShapeDtypeStruct((t, d_model), _ACT_DTYPE),
        in_specs=[_rows(tm, d_model), _const((None, 1, d_model), (layer, 0, 0)), _const((None, d_model, d_model), (j, 0, 0)),
                  _rows(tm, LANES), _rows(tm, LANES)],
        out_specs=_rows(tm, d_model), compiler_params=_params(),
    )(x, pre_g, w_q, cos, ssin)


def _q_bwd(dx, dqs, x, pre_g, w_q, cos, ssin, layer, j, name):
    t, d_model = x.shape
    tm = _row_tile(t)

    def body(dx_ref, dq_ref, x_ref, g_ref, w_ref, cos_ref, sin_ref, dxi_ref, h_ref, dqo_ref, dg_ref):
        i = pl.program_id(0)

        @pl.when(i == 0)
        def _():
            dg_ref[...] = jnp.zeros_like(dg_ref)

        xv = x_ref[...]
        g = g_ref[...]
        h_ref[...] = _rms_fwd(xv, g).astype(h_ref.dtype)
        parts = []
        for p in range(d_model // LANES):
            cols = slice(p * LANES, (p + 1) * LANES)
            parts.append(_rope_bwd(dq_ref[:, cols] * ATTN_SCALE, cos_ref[...], sin_ref[...]))
        dq = jnp.concatenate(parts, axis=1).astype(_MXU_DTYPE)
        dqo_ref[...] = dq
        dxp, dg = _rms_bwd(xv, g, _mm_tb(dq, w_ref[...]))
        dg_ref[...] += dg
        dxi_ref[...] = dx_ref[...] + dxp

    act = jax.ShapeDtypeStruct((t, d_model), _ACT_DTYPE)
    return pl.pallas_call(
        body, name=name, grid=(t // tm,),
        out_shape=(jax.ShapeDtypeStruct((t, d_model), F32), act, act, jax.ShapeDtypeStruct((1, d_model), F32)),
        in_specs=[_rows(tm, d_model), _rows(tm, d_model), _rows(tm, d_model), _const((None, 1, d_model), (layer, 0, 0)),
                  _const((None, d_model, d_model), (j, 0, 0)), _rows(tm, LANES), _rows(tm, LANES)],
        out_specs=(_rows(tm, d_model), _rows(tm, d_model), _rows(tm, d_model), _const((1, d_model))),
        compiler_params=_params(),
    )(dx, dqs, x, pre_g, w_q, cos, ssin)


def _attn_scores(q_pair, k2, n, sink_a, sink_b):
    low = _low_half(q_pair.shape)
    zero = jnp.zeros_like(q_pair)
    qst = jnp.concatenate([jnp.where(low, q_pair, zero), jnp.where(low, zero, q_pair)], axis=0)
    s = _mm_tb(qst, k2)
    row = lax.broadcasted_iota(jnp.int32, s.shape, 0)
    col = lax.broadcasted_iota(jnp.int32, s.shape, 1)
    rel = BLOCK + (row & (BLOCK - 1)) - col
    valid = (rel >= 0) & (rel < WINDOW) & (n * BLOCK + col - BLOCK >= 0)
    s = jnp.where(valid, s, NEG_INF)
    rows1 = lax.broadcasted_iota(jnp.int32, (2 * BLOCK, 1), 0)
    sink = jnp.where(rows1 < BLOCK, sink_a, sink_b)
    return qst, s, sink


def _attn_fwd(qs, kdup, vdup, sinks, j, name):
    t, d_model = qs.shape
    nb = t // BLOCK
    n_pairs = d_model // LANES
    per_group = n_pairs // N_KV_HEADS

    def body(sink_ref, q_ref, kp_ref, ko_ref, vp_ref, vo_ref, o_ref, lse_ref):
        n = pl.program_id(0)
        low = _low_half((BLOCK, LANES))
        lane = _lane_iota((BLOCK, LANES))
        lse = jnp.zeros((BLOCK, LANES), F32)
        for p in range(n_pairs):
            hk = p // per_group
            k2 = jnp.concatenate([kp_ref[hk], ko_ref[hk]], axis=0)
            v2 = jnp.concatenate([vp_ref[hk], vo_ref[hk]], axis=0)
            _, s, sink = _attn_scores(q_ref[:, p * LANES:(p + 1) * LANES], k2, n, sink_ref[j, 2 * p], sink_ref[j, 2 * p + 1])
            m = jnp.maximum(jnp.max(s, axis=-1, keepdims=True), sink)
            pe = jnp.exp(s - m)
            denom = jnp.sum(pe, axis=-1, keepdims=True) + jnp.exp(sink - m)
            o2 = _mm(pe, v2) / denom
            o_ref[:, p * LANES:(p + 1) * LANES] = jnp.where(low, o2[:BLOCK], o2[BLOCK:]).astype(o_ref.dtype)
            l2 = m + jnp.log(denom)
            lse = jnp.where(lane == 2 * p, l2[:BLOCK], lse)
            lse = jnp.where(lane == 2 * p + 1, l2[BLOCK:], lse)
        lse_ref[...] = lse

    prev = pl.BlockSpec((N_KV_HEADS, BLOCK, LANES), lambda n: (0, jnp.maximum(n - 1, 0), 0))
    own = pl.BlockSpec((N_KV_HEADS, BLOCK, LANES), lambda n: (0, n, 0))
    return pl.pallas_call(
        body, name=name, grid=(nb,),
        out_shape=(jax.ShapeDtypeStruct((t, d_model), _ACT_DTYPE), jax.ShapeDtypeStruct((t, LANES), F32)),
        in_specs=[pl.BlockSpec(memory_space=pltpu.SMEM), _rows(BLOCK, d_model), prev, own, prev, own],
        out_specs=(_rows(BLOCK, d_model), _rows(BLOCK, LANES)), compiler_params=_params(),
    )(sinks, qs, kdup, kdup, vdup, vdup)


def _attn_bwd(qs, kdup, vdup, sinks, lse, do, j, name):
    t, d_model = qs.shape
    nb = t // BLOCK
    n_pairs = d_model // LANES
    per_group = n_pairs // N_KV_HEADS
    rev = lambda n: nb - 1 - n

    def body(sink_ref, q_ref, kp_ref, ko_ref, vp_ref, vo_ref, lse_ref, do_ref, dq_ref, dk_ref, dv_ref, ds_ref, ck, cv):
        i = pl.program_id(0)
        n = nb - 1 - i

        @pl.when(i == 0)
        def _():
            ck[...] = jnp.zeros_like(ck)
            cv[...] = jnp.zeros_like(cv)
            ds_ref[...] = jnp.zeros_like(ds_ref)

        low = _low_half((BLOCK, LANES))
        lane = _lane_iota((BLOCK, LANES))
        lane1 = _lane_iota((1, LANES))
        lsev = lse_ref[...]
        dsink = jnp.zeros((1, LANES), F32)
        dk2 = [jnp.zeros((2 * BLOCK, LANES), F32) for _ in range(N_KV_HEADS)]
        dv2 = [jnp.zeros((2 * BLOCK, LANES), F32) for _ in range(N_KV_HEADS)]
        for p in range(n_pairs):
            hk = p // per_group
            k2 = jnp.concatenate([kp_ref[hk], ko_ref[hk]], axis=0)
            v2 = jnp.concatenate([vp_ref[hk], vo_ref[hk]], axis=0)
            qst, s, sink = _attn_scores(q_ref[:, p * LANES:(p + 1) * LANES], k2, n, sink_ref[j, 2 * p], sink_ref[j, 2 * p + 1])
            l2 = jnp.concatenate([jnp.sum(jnp.where(lane == 2 * p, lsev, 0.0), axis=-1, keepdims=True),
                                  jnp.sum(jnp.where(lane == 2 * p + 1, lsev, 0.0), axis=-1, keepdims=True)], axis=0)
            pn = jnp.exp(s - l2)
            dop = do_ref[:, p * LANES:(p + 1) * LANES]
            zero = jnp.zeros_like(dop)
            dost = jnp.concatenate([jnp.where(low, dop, zero), jnp.where(low, zero, dop)], axis=0)
            dp = _mm_tb(dost, v2)
            dr = jnp.sum(pn * dp, axis=-1, keepdims=True)
            dsm = (pn * (dp - dr)).astype(_MXU_DTYPE)
            dsk = -jnp.exp(sink - l2) * dr
            dsink = dsink + jnp.where(lane1 == 2 * p, jnp.sum(dsk[:BLOCK]), 0.0) + jnp.where(lane1 == 2 * p + 1, jnp.sum(dsk[BLOCK:]), 0.0)
            dq2 = _mm(dsm, k2)
            dq_ref[:, p * LANES:(p + 1) * LANES] = jnp.where(low, dq2[:BLOCK], dq2[BLOCK:])
            dk2[hk] = dk2[hk] + _mm_ta(dsm, qst)
            dv2[hk] = dv2[hk] + _mm_ta(pn, dost)
        ds_ref[...] += dsink
        for hk in range(N_KV_HEADS):
            for acc, carry, ref in ((dk2[hk], ck, dk_ref), (dv2[hk], cv, dv_ref)):
                folded = acc + pltpu.roll(acc, 64, 1)
                ref[hk] = folded[BLOCK:] + carry[hk]
                carry[hk] = folded[:BLOCK]

    prev = pl.BlockSpec((N_KV_HEADS, BLOCK, LANES), lambda n: (0, jnp.maximum(rev(n) - 1, 0), 0))
    own = pl.BlockSpec((N_KV_HEADS, BLOCK, LANES), lambda n: (0, rev(n), 0))
    rows = lambda cols: pl.BlockSpec((BLOCK, cols), lambda n: (rev(n), 0))
    heads = jax.ShapeDtypeStruct((N_KV_HEADS, t, LANES), F32)
    return pl.pallas_call(
        body, name=name, grid=(nb,),
        out_shape=(jax.ShapeDtypeStruct((t, d_model), F32), heads, heads, jax.ShapeDtypeStruct((1, LANES), F32)),
        in_specs=[pl.BlockSpec(memory_space=pltpu.SMEM), rows(d_model), prev, own, prev, own, rows(LANES), rows(d_model)],
        out_specs=(rows(d_model), own, own, _const((1, LANES))),
        scratch_shapes=[pltpu.VMEM((N_KV_HEADS, BLOCK, LANES), F32), pltpu.VMEM((N_KV_HEADS, BLOCK, LANES), F32)],
        compiler_params=_params(),
    )(sinks, qs, kdup, kdup, vdup, vdup, lse, do)


def _oproj_fwd(x, o, w_o, post_g, layer, j, name):
    t, d_model = x.shape
    tm = _row_tile(t)

    def body(x_ref, o_ref, w_ref, g_ref, xo_ref, mo_ref):
        mo = _mm(o_ref[...], w_ref[...])
        mo_ref[...] = mo
        xo_ref[...] = x_ref[...] + _rms_fwd(mo, g_ref[...])

    full = jax.ShapeDtypeStruct((t, d_model), F32)
    return pl.pallas_call(
        body, name=name, grid=(t // tm,), out_shape=(full, full),
        in_specs=[_rows(tm, d_model), _rows(tm, d_model), _const((None, d_model, d_model), (j, 0, 0)),
                  _const((None, 1, d_model), (layer, 0, 0))],
        out_specs=(_rows(tm, d_model), _rows(tm, d_model)), compiler_params=_params(),
    )(x, o, w_o, post_g)


def _oproj_bwd(dx, mo, w_o, post_g, layer, j, name):
    t, d_model = dx.shape
    tm = _row_tile(t)

    def body(dx_ref, mo_ref, w_ref, g_ref, do_ref, dmo_ref, dg_ref):
        i = pl.program_id(0)

        @pl.when(i == 0)
        def _():
            dg_ref[...] = jnp.zeros_like(dg_ref)

        dmo, dg = _rms_bwd(mo_ref[...], g_ref[...], dx_ref[...])
        dg_ref[...] += dg
        dmo = dmo.astype(_MXU_DTYPE)
        dmo_ref[...] = dmo
        do_ref[...] = _mm_tb(dmo, w_ref[...]).astype(do_ref.dtype)

    act = jax.ShapeDtypeStruct((t, d_model), _ACT_DTYPE)
    return pl.pallas_call(
        body, name=name, grid=(t // tm,), out_shape=(act, act, jax.ShapeDtypeStruct((1, d_model), F32)),
        in_specs=[_rows(tm, d_model), _rows(tm, d_model), _const((None, d_model, d_model), (j, 0, 0)),
                  _const((None, 1, d_model), (layer, 0, 0))],
        out_specs=(_rows(tm, d_model), _rows(tm, d_model), _const((1, d_model))), compiler_params=_params(),
    )(dx, mo, w_o, post_g)


def _loss_grad(y, target, name):
    t, d_model = y.shape
    tm = _row_tile(t)

    def body(y_ref, t_ref, dy_ref, loss_ref):
        i = pl.program_id(0)

        @pl.when(i == 0)
        def _():
            loss_ref[...] = jnp.zeros_like(loss_ref)

        err = y_ref[...] - t_ref[...]
        dy_ref[...] = err / d_model
        loss_ref[...] += 0.5 * jnp.sum(jnp.mean(err * err, axis=-1, keepdims=True), axis=0, keepdims=True)

    return pl.pallas_call(
        body, name=name, grid=(t // tm,),
        out_shape=(jax.ShapeDtypeStruct((t, d_model), F32), jax.ShapeDtypeStruct((1, 1), F32)),
        in_specs=[_rows(tm, d_model), _rows(tm, d_model)], out_specs=(_rows(tm, d_model), _const((1, 1))),
        compiler_params=_params(),
    )(y, target)


def _mesh_position():
    return lax.axis_index("x"), lax.axis_index("y"), lax.axis_index("c")


def _block_of(px, py, pc):
    return 4 * px + 2 * py + pc


def _at_block(ref, axis, block):
    return ref.at[(slice(None),) * axis + (block,)]


def _all_gather(shards, axes, name):
    n = len(shards)

    def body(*refs):
        srcs, outs = refs[:n], refs[n:2 * n]
        send_sems, recv_sems, local_sems = refs[2 * n:]
        x, y, c = _mesh_position()
        me, sibling = (x, y, c), (x, y, 1 - c)
        chips = [(1 - x, y), (x, 1 - y), (1 - x, 1 - y)]

        def blk(i, pos):
            return _at_block(outs[i], axes[i], _block_of(*pos))

        def copy(i, k, block, to, src=None):
            return pltpu.make_async_remote_copy(
                src_ref=blk(i, block) if src is None else src, dst_ref=blk(i, block),
                send_sem=send_sems.at[i, k], recv_sem=recv_sems.at[i, k], device_id=to, device_id_type=MESH)

        mine = [pltpu.make_async_copy(srcs[i], blk(i, me), local_sems.at[i]) for i in range(n)]
        for cp in mine:
            cp.start()
        sent = []
        for i in range(n):
            sent += [copy(i, 1 + k, me, (*chip, c), src=srcs[i]) for k, chip in enumerate(chips)]
            sent.append(copy(i, 0, me, sibling, src=srcs[i]))
        for cp in sent:
            cp.start()
        for i in range(n):
            for k, chip in enumerate(chips):
                copy(i, 1 + k, (*chip, c), me).wait_recv()
                passed = copy(i, 4 + k, (*chip, c), sibling)
                passed.start()
                sent.append(passed)
        for i in range(n):
            copy(i, 0, sibling, me).wait_recv()
            for k, chip in enumerate(chips):
                copy(i, 4 + k, (*chip, 1 - c), me).wait_recv()
        for cp in sent:
            cp.wait_send()
        for cp in mine:
            cp.wait()

    hbm = pl.BlockSpec(memory_space=pl.ANY)
    return pl.pallas_call(
        body, name=name,
        out_shape=tuple(jax.ShapeDtypeStruct(s.shape[:a] + (N_DEV,) + s.shape[a:], s.dtype) for s, a in zip(shards, axes)),
        in_specs=[hbm] * n, out_specs=(hbm,) * n,
        scratch_shapes=[pltpu.SemaphoreType.DMA((n, 7)), pltpu.SemaphoreType.DMA((n, 7)), pltpu.SemaphoreType.DMA((n,))],
    )(*shards)


def _reduce_scatter_parts(grads, axes, name):
    n = len(grads)

    def body(*refs):
        srcs, outs = refs[:n], refs[n:2 * n]
        send_sems, recv_sems, local_sems = refs[2 * n:]
        x, y, c = _mesh_position()
        my_block = _block_of(x, y, c)
        peers = []
        for k in range(1, N_DEV):
            fx, fy, fc = (k >> 2) & 1, (k >> 1) & 1, k & 1
            peers.append((1 - x if fx else x, 1 - y if fy else y, 1 - c if fc else c))

        def copy(i, k, peer):
            return pltpu.make_async_remote_copy(
                src_ref=_at_block(srcs[i], axes[i], _block_of(*peer)), dst_ref=outs[i].at[my_block],
                send_sem=send_sems.at[i, k], recv_sem=recv_sems.at[i, k], device_id=peer, device_id_type=MESH)

        mine = [pltpu.make_async_copy(_at_block(srcs[i], axes[i], my_block), outs[i].at[my_block], local_sems.at[i])
                for i in range(n)]
        for cp in mine:
            cp.start()
        sent = [copy(i, k, peer) for i in range(n) for k, peer in enumerate(peers)]
        for cp in sent:
            cp.start()
        for i in range(n):
            for k, peer in enumerate(peers):
                pltpu.make_async_remote_copy(
                    src_ref=_at_block(srcs[i], axes[i], my_block), dst_ref=outs[i].at[_block_of(*peer)],
                    send_sem=send_sems.at[i, k], recv_sem=recv_sems.at[i, k], device_id=peer, device_id_type=MESH).wait_recv()
        for cp in sent:
            cp.wait_send()
        for cp in mine:
            cp.wait()

    hbm = pl.BlockSpec(memory_space=pl.ANY)
    return pl.pallas_call(
        body, name=name,
        out_shape=tuple(jax.ShapeDtypeStruct((N_DEV,) + g.shape[:a] + g.shape[a + 1:], g.dtype) for g, a in zip(grads, axes)),
        in_specs=[hbm] * n, out_specs=(hbm,) * n,
        scratch_shapes=[pltpu.SemaphoreType.DMA((n, 7)), pltpu.SemaphoreType.DMA((n, 7)), pltpu.SemaphoreType.DMA((n,))],
    )(*grads)


def _adamw_math(w, g, m, v):
    m = ADAM_B1 * m + (1.0 - ADAM_B1) * g
    v = ADAM_B2 * v + (1.0 - ADAM_B2) * jnp.square(g)
    m_hat = m / (1.0 - ADAM_B1 ** ADAM_STEP)
    v_hat = v / (1.0 - ADAM_B2 ** ADAM_STEP)
    delta = -ADAM_LR * (m_hat / (jnp.sqrt(v_hat) + ADAM_EPS) + ADAM_WD * w)
    return delta, m, v


def _update_tile(rows):
    if rows <= 1024:
        return rows
    for tr in (512, 384, 352, 256, 176, 128, 64, 32, 16):
        if rows % tr == 0:
            return tr
    raise ValueError(f"{rows} rows do not tile")


def _adamw(parts, w, m, v, name):
    r, c = w.shape
    tr = _update_tile(r)

    def body(p_ref, w_ref, m_ref, v_ref, g_ref, d_ref, mo_ref, vo_ref):
        g = p_ref[0].astype(F32)
        for s in range(1, N_DEV):
            g = g + p_ref[s].astype(F32)
        g_ref[...] = g
        d_ref[...], mo_ref[...], vo_ref[...] = _adamw_math(w_ref[...], g, m_ref[...], v_ref[...])

    out = jax.ShapeDtypeStruct((r, c), F32)
    return pl.pallas_call(
        body, name=name, grid=(r // tr,), out_shape=(out,) * 4,
        in_specs=[pl.BlockSpec((N_DEV, tr, c), lambda i: (0, i, 0))] + [_rows(tr, c)] * 3,
        out_specs=(_rows(tr, c),) * 4, compiler_params=_params(),
    )(parts, w, m, v)


def _sum_parts(parts, name):
    _, r, c = parts.shape
    tr = _update_tile(r)

    def body(p_ref, g_ref):
        g = p_ref[0].astype(F32)
        for s in range(1, N_DEV):
            g = g + p_ref[s].astype(F32)
        g_ref[...] = g

    return pl.pallas_call(
        body, name=name, grid=(r // tr,), out_shape=jax.ShapeDtypeStruct((r, c), F32),
        in_specs=[pl.BlockSpec((N_DEV, tr, c), lambda i: (0, i, 0))], out_specs=_rows(tr, c), compiler_params=_params(),
    )(parts)


def _adamw_plain(g, w, m, v, name):
    r, c = w.shape
    tr = _update_tile(r)

    def body(g_ref, w_ref, m_ref, v_ref, d_ref, mo_ref, vo_ref):
        d_ref[...], mo_ref[...], vo_ref[...] = _adamw_math(w_ref[...], g_ref[...], m_ref[...], v_ref[...])

    out = jax.ShapeDtypeStruct((r, c), F32)
    return pl.pallas_call(
        body, name=name, grid=(r // tr,), out_shape=(out,) * 3, in_specs=[_rows(tr, c)] * 4, out_specs=(_rows(tr, c),) * 3,
        compiler_params=_params(),
    )(g, w, m, v)


def _pack(arrays):
    flat = []
    for a in arrays:
        a = a.reshape(-1)
        flat.append(jnp.pad(a, (0, (-a.shape[0]) % LANES)))
    flat = jnp.concatenate(flat)
    flat = jnp.pad(flat, (0, (-flat.shape[0]) % (16 * LANES)))
    return flat.reshape(-1, LANES)


def _unpack(packed, shapes):
    flat = packed.reshape(-1)
    out, at = [], 0
    for shp in shapes:
        size = math.prod(shp)
        out.append(flat[at:at + size].reshape(shp))
        at += size + (-size) % LANES
    return out


def kernel(x, positions, mix_pre_g, mix_post_g, pool_w, pool_scale, kv_norm_g, w_kv, w_q, w_o, sinks, ffn_pre_g, ffn_post_g, ffn_w_in, ffn_conv_w, ffn_conv_b, ffn_w_out, loss_target, m_mix_pre_g, m_mix_post_g, m_pool_w, m_pool_scale, m_kv_norm_g, m_w_kv, m_w_q, m_w_o, m_sinks, m_ffn_pre_g, m_ffn_post_g, m_ffn_w_in, m_ffn_conv_w, m_ffn_conv_b, m_ffn_w_out, v_mix_pre_g, v_mix_post_g, v_pool_w, v_pool_scale, v_kv_norm_g, v_w_kv, v_w_q, v_w_o, v_sinks, v_ffn_pre_g, v_ffn_post_g, v_ffn_w_in, v_ffn_conv_w, v_ffn_conv_b, v_ffn_w_out):
    depth, d_model = mix_pre_g.shape
    n_a = pool_w.shape[0]
    n_b = w_q.shape[0]
    t = x.shape[1]
    fs = ffn_w_in.shape[2]
    half = N_DEV // 2
    n_heads = d_model // HEAD_DIM
    me = _block_of(*_mesh_position())
    x0 = x.reshape(t, d_model)
    target = loss_target.reshape(t, d_model)

    inv_freq = 1.0 / (ROPE_THETA ** (jnp.arange(0, HEAD_DIM, 2, dtype=F32) / HEAD_DIM))
    ang = positions.reshape(t).astype(F32)[:, None] * inv_freq
    cos, sin = jnp.cos(ang), jnp.sin(ang)
    cos = jnp.tile(cos, (1, 2 * LANES // HEAD_DIM))
    ssin = jnp.tile(jnp.concatenate([-sin, sin], axis=1), (1, LANES // HEAD_DIM))

    wire = lambda a: a.astype(_WIRE_DTYPE)
    pool_w_g, w_kv_g, w_q_g, w_o_g, w_in_g, w_out_g = _all_gather(
        [wire(pool_w), wire(w_kv), wire(w_q), wire(w_o), wire(ffn_w_in), wire(ffn_w_out)], [2, 0, 1, 1, 1, 1], "gather_weights")
    small_shapes = [pool_scale.shape, ffn_conv_w.shape]
    (small_g,) = _all_gather([_pack([pool_scale, ffn_conv_w])], [0], "gather_small")
    small_g = [_unpack(small_g[b], small_shapes) for b in range(N_DEV)]
    pool_scale_f = jnp.concatenate([s[0] for s in small_g], axis=1).reshape(n_a, 1, d_model)
    conv_w_f = jnp.stack([s[1] for s in small_g], axis=2)
    pool_w_f = pool_w_g.reshape(n_a, len(POOL_WINDOWS), d_model // len(POOL_WINDOWS), -1)
    w_kv_f = w_kv_g.reshape(d_model, -1)
    w_q_f = w_q_g.reshape(n_b, d_model, d_model)
    w_o_f = w_o_g.reshape(n_b, d_model, d_model)
    w_out_f = w_out_g.reshape(depth, half * fs, d_model)
    conv_b_f = ffn_conv_b.reshape(depth, N_DEV, fs)
    g3 = lambda a: a.reshape(a.shape[0], 1, a.shape[1])
    mix_pre, mix_post, ffn_pre, ffn_post = g3(mix_pre_g), g3(mix_post_g), g3(ffn_pre_g), g3(ffn_post_g)
    kv_g = kv_norm_g.reshape(1, d_model)

    saved = []
    xc = x0
    kdup = vdup = x_kv = None
    for l in range(depth):
        x_in = xc
        if l < n_a:
            x_mid, dsave, yu = _pool_fwd(x_in, mix_pre, mix_post, pool_w_f, pool_scale_f, l, f"pool_fwd_{l}")
            mixer = (dsave, yu)
        else:
            j = l - n_a
            if j == 0:
                x_kv = x_in
                kdup, vdup = _kv_fwd(x_kv, kv_g, w_kv_f, cos, ssin, "kv_fwd")
            qs = _q_fwd(x_in, mix_pre, w_q_f, cos, ssin, l, j, f"q_fwd_{l}")
            o, lse = _attn_fwd(qs, kdup, vdup, sinks, j, f"attn_fwd_{l}")
            x_mid, mo = _oproj_fwd(x_in, o, w_o_f, mix_post, l, j, f"oproj_fwd_{l}")
            mixer = (qs, o, lse, mo)
        xc, u, f = _ffn_fwd(x_mid, ffn_pre, ffn_post, w_in_g, conv_w_f, conv_b_f, w_out_f, l, f"ffn_fwd_{l}")
        saved.append((x_in, x_mid, u, f, mixer))

    dx, loss_part = _loss_grad(xc, target, "loss")
    loss = lax.psum(loss_part[0, 0], AXES)

    tk = _k_tile(t)
    gw_in, gw_out, gconv_w, gconv_b = [None] * depth, [None] * depth, [None] * depth, [None] * depth
    gmix_pre, gmix_post, gffn_pre, gffn_post = [None] * depth, [None] * depth, [None] * depth, [None] * depth
    gpool_w, gpool_scale = [None] * n_a, [None] * n_a
    gw_q, gw_o, gsinks = [None] * n_b, [None] * n_b, [None] * n_b
    dks, dvs = [], []
    gw_kv = gkv_g = None
    for l in reversed(range(depth)):
        x_in, x_mid, u, f, mixer = saved[l]
        duc, gact, df, gconv_w[l], gconv_b[l], gffn_post[l] = _ffn_bwd_out(
            dx, f, ffn_post, u, conv_w_f, conv_b_f, w_out_f, l, f"ffn_bwd_out_{l}")
        gw_out[l] = _tn_matmul(
            gact, df, pl.BlockSpec((None, tk, fs), lambda c, k: (c, k, 0)), pl.BlockSpec((tk, d_model), lambda c, k: (k, 0)),
            half, fs, d_model, tk, f"grad_w_out_{l}", _WIRE_DTYPE)
        dx, du, hf, gffn_pre[l] = _ffn_bwd_in(dx, x_mid, ffn_pre, duc, conv_w_f, w_in_g, l, f"ffn_bwd_in_{l}")
        gw_in[l] = _tn_matmul(
            hf, du, pl.BlockSpec((tk, d_model), lambda c, k: (k, 0)), pl.BlockSpec((None, tk, fs), lambda c, k: (c, k, 0)),
            N_DEV, d_model, fs, tk, f"grad_w_in_{l}", _WIRE_DTYPE)
        if l < n_a:
            dsave, yu = mixer
            dx, dyu, gpool_scale[l], gmix_pre[l], gmix_post[l] = _pool_bwd(
                dx, x_in, mix_pre, mix_post, dsave, yu, pool_w_f, pool_scale_f, l, f"pool_bwd_{l}")
            gc = d_model // len(POOL_WINDOWS)
            gpool_w[l] = _tn_matmul(
                dsave, dyu, pl.BlockSpec((tk, gc), lambda c, k: (k, c)), pl.BlockSpec((tk, gc), lambda c, k: (k, c)),
                len(POOL_WINDOWS), gc, gc, tk, f"grad_pool_w_{l}", _WIRE_DTYPE)
        else:
            j = l - n_a
            qs, o, lse, mo = mixer
            do, dmo, gmix_post[l] = _oproj_bwd(dx, mo, w_o_f, mix_post, l, j, f"oproj_bwd_{l}")
            shared = lambda cols: pl.BlockSpec((tk, cols), lambda c, k: (k, 0))
            gw_o[j] = _tn_matmul(o, dmo, shared(d_model), shared(d_model), 1, d_model, d_model, tk, f"grad_w_o_{l}", _WIRE_DTYPE)
            dqs, dk, dv, gsinks[j] = _attn_bwd(qs, kdup, vdup, sinks, lse, do, j, f"attn_bwd_{l}")
            dks.append(dk)
            dvs.append(dv)
            dx, hq, dq, gmix_pre[l] = _q_bwd(dx, dqs, x_in, mix_pre, w_q_f, cos, ssin, l, j, f"q_bwd_{l}")
            gw_q[j] = _tn_matmul(hq, dq, shared(d_model), shared(d_model), 1, d_model, d_model, tk, f"grad_w_q_{l}", _WIRE_DTYPE)
            if j == 0:
                dx, hkv, dkv, gkv_g = _kv_bwd(dx, x_kv, kv_g, w_kv_f, cos, ssin, dks, dvs, "kv_bwd")
                gw_kv = _tn_matmul(hkv, dkv, shared(d_model), shared(dkv.shape[1]), 1, d_model, dkv.shape[1], tk,
                                   "grad_w_kv", _WIRE_DTYPE)

    grad_x = dx.reshape(x.shape)

    full_pool_w = jnp.stack(gpool_w).reshape(n_a, len(POOL_WINDOWS), N_DEV, -1, gpool_w[0].shape[-1])
    full_w_kv = gw_kv.reshape(N_DEV, -1, gw_kv.shape[-1])
    full_w_q = jnp.concatenate(gw_q).reshape(n_b, N_DEV, -1, d_model)
    full_w_o = jnp.concatenate(gw_o).reshape(n_b, N_DEV, -1, d_model)
    full_w_in = jnp.stack(gw_in)
    full_w_out = jnp.stack(gw_out).reshape(depth, N_DEV, -1, d_model)
    parts = _reduce_scatter_parts([full_pool_w, full_w_kv, full_w_q, full_w_o, full_w_in, full_w_out],
                                  [2, 0, 1, 1, 1, 1], "scatter_grads")
    big = {}
    for nm, p, w, m, v in (("pool_w", parts[0], pool_w, m_pool_w, v_pool_w), ("w_kv", parts[1], w_kv, m_w_kv, v_w_kv),
                           ("w_q", parts[2], w_q, m_w_q, v_w_q), ("w_o", parts[3], w_o, m_w_o, v_w_o),
                           ("ffn_w_in", parts[4], ffn_w_in, m_ffn_w_in, v_ffn_w_in),
                           ("ffn_w_out", parts[5], ffn_w_out, m_ffn_w_out, v_ffn_w_out)):
        cols = w.shape[-1]
        two = lambda a: a.reshape(-1, cols)
        res = _adamw(p.reshape(N_DEV, -1, cols), two(w), two(m), two(v), f"adamw_{nm}")
        big[nm] = tuple(r.reshape(w.shape) for r in res)

    cat = lambda rows: jnp.concatenate(rows, axis=0)
    small_partials = [cat(gmix_pre), cat(gmix_post), gkv_g.reshape(d_model), cat([s[:, :n_heads] for s in gsinks]),
                      cat(gffn_pre), cat(gffn_post), jnp.stack(gconv_b).reshape(depth, N_DEV * fs),
                      cat(gpool_scale), jnp.stack(gconv_w)]
    small_shapes = [(depth, d_model), (depth, d_model), (d_model,), (n_b, n_heads), (depth, d_model), (depth, d_model),
                    (depth, N_DEV * fs), (n_a, d_model), (depth, 3, N_DEV, fs)]
    (small_parts,) = _all_gather([_pack(small_partials)], [0], "gather_small_grads")
    small_sum = _unpack(_sum_parts(small_parts, "sum_small_grads"), small_shapes)
    g_pool_scale = lax.dynamic_slice_in_dim(small_sum[7], me * pool_scale.shape[1], pool_scale.shape[1], axis=1)
    g_conv_w = lax.dynamic_index_in_dim(small_sum[8], me, axis=2, keepdims=False)
    small = [("mix_pre_g", small_sum[0], mix_pre_g, m_mix_pre_g, v_mix_pre_g),
             ("mix_post_g", small_sum[1], mix_post_g, m_mix_post_g, v_mix_post_g),
             ("kv_norm_g", small_sum[2], kv_norm_g, m_kv_norm_g, v_kv_norm_g),
             ("sinks", small_sum[3], sinks, m_sinks, v_sinks),
             ("ffn_pre_g", small_sum[4], ffn_pre_g, m_ffn_pre_g, v_ffn_pre_g),
             ("ffn_post_g", small_sum[5], ffn_post_g, m_ffn_post_g, v_ffn_post_g),
             ("ffn_conv_b", small_sum[6], ffn_conv_b, m_ffn_conv_b, v_ffn_conv_b),
             ("pool_scale", g_pool_scale, pool_scale, m_pool_scale, v_pool_scale),
             ("ffn_conv_w", g_conv_w, ffn_conv_w, m_ffn_conv_w, v_ffn_conv_w)]
    shapes = [w.shape for _, _, w, _, _ in small]
    upd = _adamw_plain(_pack([g for _, g, _, _, _ in small]), _pack([w for _, _, w, _, _ in small]),
                       _pack([m for _, _, _, m, _ in small]), _pack([v for _, _, _, _, v in small]), "adamw_small")
    upd = [_unpack(a, shapes) for a in upd]
    res = dict(big)
    for i, (nm, g, _, _, _) in enumerate(small):
        res[nm] = (g, upd[0][i], upd[1][i], upd[2][i])

    order = ["mix_pre_g", "mix_post_g", "pool_w", "pool_scale", "kv_norm_g", "w_kv", "w_q", "w_o", "sinks", "ffn_pre_g",
             "ffn_post_g", "ffn_w_in", "ffn_conv_w", "ffn_conv_b", "ffn_w_out"]
    return (loss, grad_x, *[res[nm][0] for nm in order], *[res[nm][1] for nm in order],
            *[res[nm][2] for nm in order], *[res[nm][3] for nm in order])
```

```python
import functools
import math

import jax
import jax.numpy as jnp
from jax import lax
from jax.experimental import pallas as pl
from jax.experimental.pallas import tpu as pltpu

F32 = jnp.float32
_MXU_DTYPE = jnp.bfloat16
_ACT_DTYPE = jnp.bfloat16
_WIRE_DTYPE = jnp.bfloat16
_SAVE_DTYPE = jnp.bfloat16

N_DEV = 8
POOL_WINDOWS = (2, 4, 8, 16)
POOL_HALO = 16
HEAD_DIM = 64
N_KV_HEADS = 4
WINDOW = 128
BLOCK = 128
LANES = 128
ROPE_THETA = 10000.0
ATTN_SCALE = 1.0 / math.sqrt(HEAD_DIM)
NEG_INF = -1e30
RMS_EPS = 1e-6
CONV_HALO = 8
SAVE_HALO = 16
ADAM_LR = 0.001
ADAM_B1 = 0.9
ADAM_B2 = 0.999
ADAM_EPS = 1e-08
ADAM_WD = 0.01
ADAM_STEP = 10
VMEM_LIMIT = 56 * 1024 * 1024
MESH = pl.DeviceIdType.MESH
AXES = ("x", "y", "c")


def _params(n_axes=1, vmem=VMEM_LIMIT):
    return pltpu.CompilerParams(dimension_semantics=("arbitrary",) * n_axes, vmem_limit_bytes=vmem)


def _resident(shape, index):
    return pl.BlockSpec(shape, lambda *_: index, pipeline_mode=pl.Buffered(1))


def _const(shape, index=None):
    index = (0,) * len(shape) if index is None else index
    return pl.BlockSpec(shape, lambda *_: index)


def _rows(tm, cols):
    return pl.BlockSpec((tm, cols), lambda i: (i, 0))


def _row_tile(t):
    for tm in (256, 128, 64, 32, 16, 8):
        if t % tm == 0:
            return tm
    raise ValueError(f"sequence length {t} is not a multiple of 8")


def _mm(a, b):
    return jnp.dot(a.astype(_MXU_DTYPE), b.astype(_MXU_DTYPE), preferred_element_type=F32)


def _mm_tb(a, b):
    return lax.dot_general(a.astype(_MXU_DTYPE), b.astype(_MXU_DTYPE), (((1,), (1,)), ((), ())),
                           preferred_element_type=F32)


def _mm_ta(a, b):
    return lax.dot_general(a.astype(_MXU_DTYPE), b.astype(_MXU_DTYPE), (((0,), (0,)), ((), ())),
                           preferred_element_type=F32)


def _rms_r(x):
    return lax.rsqrt(jnp.mean(x * x, axis=-1, keepdims=True) + RMS_EPS)


def _rms_fwd(x, g):
    return (x * _rms_r(x)) * g


def _rms_bwd(x, g, dy):
    r = _rms_r(x)
    xh = x * r
    dg = jnp.sum(dy * xh, axis=0, keepdims=True)
    dxh = dy * g
    dx = r * (dxh - xh * jnp.mean(dxh * xh, axis=-1, keepdims=True))
    return dx, dg


_GELU_C = math.sqrt(2.0 / math.pi)


def _gelu_parts(z):
    z2 = z * z
    t = jnp.tanh(_GELU_C * (z + 0.044715 * (z2 * z)))
    cdf = 0.5 * (1.0 + t)
    dz = cdf + z * (0.5 * (1.0 - t * t)) * (_GELU_C * (1.0 + (3 * 0.044715) * z2))
    return cdf, dz


def _lane_iota(shape):
    return lax.broadcasted_iota(jnp.int32, shape, len(shape) - 1)


def _rope_partner(xb):
    first = (_lane_iota(xb.shape) & 32) == 0
    return jnp.where(first, pltpu.roll(xb, LANES - 32, 1), pltpu.roll(xb, 32, 1))


def _rope_fwd(xb, cos, ssin):
    return xb * cos + _rope_partner(xb) * ssin


def _rope_bwd(dyb, cos, ssin):
    return dyb * cos - _rope_partner(dyb) * ssin


def _low_half(shape):
    return (_lane_iota(shape) & 64) == 0


def _pool_fwd(x, pre_g, post_g, w, scale, layer, name):
    t, d_model = x.shape
    tm = _row_tile(t)
    n_groups, gc = w.shape[1], w.shape[2]

    def body(x_ref, pre_ref, post_ref, w_ref, sc_ref, xo_ref, d_ref, yu_ref, hbuf):
        i = pl.program_id(0)

        @pl.when(i == 0)
        def _():
            hbuf[pl.ds(0, POOL_HALO), :] = jnp.zeros((POOL_HALO, d_model), F32)

        xv = x_ref[...]
        hbuf[pl.ds(POOL_HALO, tm), :] = _rms_fwd(xv, pre_ref[...])
        tok = i * tm + lax.broadcasted_iota(jnp.int32, (tm, 1), 0)
        yus = []
        for gi, wnd in enumerate(POOL_WINDOWS):
            cols = pl.ds(gi * gc, gc)
            h = hbuf[pl.ds(POOL_HALO, tm), cols]
            acc = h
            for k in range(1, wnd):
                acc = acc + hbuf[pl.ds(POOL_HALO - k, tm), cols]
            cnt = jnp.minimum(tok + 1, wnd).astype(F32)
            dg = acc / cnt - h
            d_ref[:, cols] = dg.astype(d_ref.dtype)
            yus.append(_mm(dg, w_ref[gi]))
        hbuf[pl.ds(0, POOL_HALO), :] = hbuf[pl.ds(tm, POOL_HALO), :]
        yu = jnp.concatenate(yus, axis=1)
        yu_ref[...] = yu
        xo_ref[...] = xv + _rms_fwd(yu * sc_ref[...], post_ref[...])

    return pl.pallas_call(
        body, name=name, grid=(t // tm,),
        out_shape=(jax.ShapeDtypeStruct((t, d_model), F32), jax.ShapeDtypeStruct((t, d_model), _ACT_DTYPE),
                   jax.ShapeDtypeStruct((t, d_model), F32)),
        in_specs=[_rows(tm, d_model), _const((None, 1, d_model), (layer, 0, 0)), _const((None, 1, d_model), (layer, 0, 0)),
                  _const((None, n_groups, gc, gc), (layer, 0, 0, 0)), _const((None, 1, d_model), (layer, 0, 0))],
        out_specs=(_rows(tm, d_model), _rows(tm, d_model), _rows(tm, d_model)),
        scratch_shapes=[pltpu.VMEM((POOL_HALO + tm, d_model), F32)],
        compiler_params=_params(),
    )(x, pre_g, post_g, w, scale)


def _pool_bwd(dx, x, pre_g, post_g, d, yu, w, scale, layer, name):
    t, d_model = x.shape
    tm = _row_tile(t)
    nt = t // tm
    n_groups, gc = w.shape[1], w.shape[2]
    rev = lambda i: (nt - 1 - i, 0)
    rows = pl.BlockSpec((tm, d_model), rev)

    def body(dx_ref, x_ref, pre_ref, post_ref, d_ref, yu_ref, w_ref, sc_ref,
             dxi_ref, dyu_ref, dsc_ref, dpre_ref, dpost_ref, zbuf):
        i = pl.program_id(0)

        @pl.when(i == 0)
        def _():
            zbuf[pl.ds(tm, POOL_HALO), :] = jnp.zeros((POOL_HALO, d_model), F32)
            dsc_ref[...] = jnp.zeros_like(dsc_ref)
            dpre_ref[...] = jnp.zeros_like(dpre_ref)
            dpost_ref[...] = jnp.zeros_like(dpost_ref)

        dxo = dx_ref[...]
        yuv = yu_ref[...]
        sc = sc_ref[...]
        dm, dpost = _rms_bwd(yuv * sc, post_ref[...], dxo)
        dpost_ref[...] += dpost
        dsc_ref[...] += jnp.sum(dm * yuv, axis=0, keepdims=True)
        dyu = dm * sc
        dyu_ref[...] = dyu.astype(dyu_ref.dtype)
        tok = (nt - 1 - i) * tm + lax.broadcasted_iota(jnp.int32, (tm, 1), 0)
        dds = []
        for gi, wnd in enumerate(POOL_WINDOWS):
            cols = pl.ds(gi * gc, gc)
            dd = _mm_tb(dyu[:, gi * gc:(gi + 1) * gc], w_ref[gi])
            cnt = jnp.minimum(tok + 1, wnd).astype(F32)
            zbuf[pl.ds(0, tm), cols] = dd / cnt
            dds.append(dd)
        dhs = []
        for gi, wnd in enumerate(POOL_WINDOWS):
            cols = pl.ds(gi * gc, gc)
            acc = zbuf[pl.ds(0, tm), cols]
            for k in range(1, wnd):
                acc = acc + zbuf[pl.ds(k, tm), cols]
            dhs.append(acc - dds[gi])
        zbuf[pl.ds(tm, POOL_HALO), :] = zbuf[pl.ds(0, POOL_HALO), :]
        dh = jnp.concatenate(dhs, axis=1)
        dxp, dpre = _rms_bwd(x_ref[...], pre_ref[...], dh)
        dpre_ref[...] += dpre
        dxi_ref[...] = dxo + dxp

    vec = jax.ShapeDtypeStruct((1, d_model), F32)
    return pl.pallas_call(
        body, name=name, grid=(nt,),
        out_shape=(jax.ShapeDtypeStruct((t, d_model), F32), jax.ShapeDtypeStruct((t, d_model), _ACT_DTYPE), vec, vec, vec),
        in_specs=[rows, rows, _const((None, 1, d_model), (layer, 0, 0)), _const((None, 1, d_model), (layer, 0, 0)), rows, rows,
                  _const((None, n_groups, gc, gc), (layer, 0, 0, 0)), _const((None, 1, d_model), (layer, 0, 0))],
        out_specs=(rows, rows, _const((1, d_model)), _const((1, d_model)), _const((1, d_model))),
        scratch_shapes=[pltpu.VMEM((tm + POOL_HALO, d_model), F32)],
        compiler_params=_params(),
    )(dx, x, pre_g, post_g, d, yu, w, scale)


def _conv_taps(cw_ref, s):
    return [cw_ref[k, pl.ds(s, 1), :] for k in range(3)]


def _shift_down(v, k, before):
    rolled = pltpu.roll(v, k, 0)
    row = lax.broadcasted_iota(jnp.int32, before.shape, 0)
    head = jnp.where(row < k, pltpu.roll(before, k, 0), rolled[:CONV_HALO])
    return jnp.concatenate([head, rolled[CONV_HALO:]], axis=0)


def _shift_up(v, k, after):
    rows = v.shape[0]
    rolled = pltpu.roll(v, rows - k, 0)
    row = lax.broadcasted_iota(jnp.int32, after.shape, 0)
    tail = jnp.where(row >= CONV_HALO - k, pltpu.roll(after, CONV_HALO - k, 0), rolled[rows - CONV_HALO:])
    return jnp.concatenate([rolled[:rows - CONV_HALO], tail], axis=0)


def _ffn_fwd(x, pre_g, post_g, w_in, conv_w, conv_b, w_out, layer, w_layer, name):
    t, d_model = x.shape
    tm = _row_tile(t)
    fs = w_in.shape[3]
    half = N_DEV // 2

    def body(x_ref, pre_ref, post_ref, win_ref, cw_ref, cb_ref, wout_ref, xo_ref, u_ref, uc_ref, f_ref, carry):
        i = pl.program_id(0)

        @pl.when(i == 0)
        def _():
            carry[...] = jnp.zeros_like(carry)

        xv = x_ref[...]
        hf = _rms_fwd(xv, pre_ref[...]).astype(_MXU_DTYPE)
        f = jnp.zeros((tm, d_model), F32)
        for b in range(half):
            ucs = []
            for s in (b, b + half):
                u = jnp.dot(hf, win_ref[s], preferred_element_type=F32)
                u_ref[s] = u.astype(u_ref.dtype)
                before = carry[s]
                carry[s] = u[tm - CONV_HALO:]
                w0, w1, w2 = _conv_taps(cw_ref, s)
                uc = ((w0 * _shift_down(u, 2, before) + w1 * _shift_down(u, 1, before)) + w2 * u) + cb_ref[pl.ds(s, 1), :]
                uc_ref[s] = uc.astype(uc_ref.dtype)
                ucs.append(uc)
            gate, val = ucs
            cdf, _ = _gelu_parts(gate)
            f = f + _mm((gate * cdf) * val, wout_ref[pl.ds(b * fs, fs), :])
        f_ref[...] = f
        xo_ref[...] = xv + _rms_fwd(f, post_ref[...])

    tile3 = pl.BlockSpec((N_DEV, tm, fs), lambda i: (0, i, 0))
    saved = jax.ShapeDtypeStruct((N_DEV, t, fs), _SAVE_DTYPE)
    return pl.pallas_call(
        body, name=name, grid=(t // tm,),
        out_shape=(jax.ShapeDtypeStruct((t, d_model), F32), saved, saved, jax.ShapeDtypeStruct((t, d_model), F32)),
        in_specs=[_rows(tm, d_model), _const((None, 1, d_model), (layer, 0, 0)), _const((None, 1, d_model), (layer, 0, 0)),
                  _resident((None, N_DEV, d_model, fs), (w_layer, 0, 0, 0)), _const((None, 3, N_DEV, fs), (layer, 0, 0, 0)),
                  _const((None, N_DEV, fs), (layer, 0, 0)), _resident((None, half * fs, d_model), (w_layer, 0, 0))],
        out_specs=(_rows(tm, d_model), tile3, tile3, _rows(tm, d_model)),
        scratch_shapes=[pltpu.VMEM((N_DEV, CONV_HALO, fs), F32)],
        compiler_params=_params(),
    )(x, pre_g, post_g, w_in, conv_w, conv_b, w_out)


def _ffn_bwd_out(dx, f, post_g, uc, conv_b, w_out, layer, w_layer, name):
    t, d_model = dx.shape
    tm = _row_tile(t)
    fs = uc.shape[2]
    half = N_DEV // 2

    def body(dx_ref, f_ref, post_ref, uc_ref, wout_ref, duc_ref, g_ref, df_ref, dcb_ref, dpost_ref):
        i = pl.program_id(0)

        @pl.when(i == 0)
        def _():
            dcb_ref[...] = jnp.zeros_like(dcb_ref)
            dpost_ref[...] = jnp.zeros_like(dpost_ref)

        df, dpost = _rms_bwd(f_ref[...], post_ref[...], dx_ref[...])
        dpost_ref[...] += dpost
        dfm = df.astype(_MXU_DTYPE)
        df_ref[...] = dfm
        for b in range(half):
            gate = uc_ref[b].astype(F32)
            val = uc_ref[b + half].astype(F32)
            cdf, dgelu = _gelu_parts(gate)
            ge = gate * cdf
            g_ref[b] = (ge * val).astype(g_ref.dtype)
            dg = _mm_tb(dfm, wout_ref[pl.ds(b * fs, fs), :])
            for s, dd in ((b, dg * val * dgelu), (b + half, dg * ge)):
                duc_ref[s] = dd.astype(duc_ref.dtype)
                dcb_ref[pl.ds(s, 1), :] += jnp.sum(dd, axis=0, keepdims=True)

    tile3 = pl.BlockSpec((N_DEV, tm, fs), lambda i: (0, i, 0))
    return pl.pallas_call(
        body, name=name, grid=(t // tm,),
        out_shape=(jax.ShapeDtypeStruct((N_DEV, t, fs), _SAVE_DTYPE), jax.ShapeDtypeStruct((half, t, fs), _ACT_DTYPE),
                   jax.ShapeDtypeStruct((t, d_model), _ACT_DTYPE), jax.ShapeDtypeStruct((N_DEV, fs), F32),
                   jax.ShapeDtypeStruct((1, d_model), F32)),
        in_specs=[_rows(tm, d_model), _rows(tm, d_model), _const((None, 1, d_model), (layer, 0, 0)), tile3,
                  _resident((None, half * fs, d_model), (w_layer, 0, 0))],
        out_specs=(tile3, pl.BlockSpec((half, tm, fs), lambda i: (0, i, 0)), _rows(tm, d_model),
                   _const((N_DEV, fs)), _const((1, d_model))),
        compiler_params=_params(),
    )(dx, f, post_g, uc, w_out)


def _ffn_bwd_in(dx, x, pre_g, duc, u, conv_w, w_in, layer, w_layer, name):
    t, d_model = dx.shape
    tm = _row_tile(t)
    nt = t // tm
    fs = duc.shape[2]
    hb = SAVE_HALO
    per_tile = tm // hb

    def body(dx_ref, x_ref, pre_ref, duc_ref, dn_ref, u_ref, cw_ref, win_ref, dxi_ref, du_ref, hf_ref, dcw_ref, dpre_ref):
        i = pl.program_id(0)

        @pl.when(i == 0)
        def _():
            dcw_ref[...] = jnp.zeros_like(dcw_ref)
            dpre_ref[...] = jnp.zeros_like(dpre_ref)

        xv = x_ref[...]
        pre = pre_ref[...]
        hf_ref[...] = _rms_fwd(xv, pre).astype(hf_ref.dtype)
        dhf = jnp.zeros((tm, d_model), F32)
        for s in range(N_DEV):
            d0 = duc_ref[s].astype(F32)
            after = jnp.where(i == nt - 1, 0.0, dn_ref[s].astype(F32)[:CONV_HALO])
            d1 = _shift_up(d0, 1, after)
            d2 = _shift_up(d0, 2, after)
            uv = u_ref[s].astype(F32)
            for k, dk in ((2, d0), (1, d1), (0, d2)):
                dcw_ref[k, pl.ds(s, 1), :] += jnp.sum(dk * uv, axis=0, keepdims=True)
            w0, w1, w2 = _conv_taps(cw_ref, s)
            du = (w2 * d0 + w1 * d1 + w0 * d2).astype(_MXU_DTYPE)
            du_ref[s] = du
            dhf = dhf + _mm_tb(du, win_ref[s])
        dxp, dpre = _rms_bwd(xv, pre, dhf)
        dpre_ref[...] += dpre
        dxi_ref[...] = dx_ref[...] + dxp

    tile3 = pl.BlockSpec((N_DEV, tm, fs), lambda i: (0, i, 0))
    return pl.pallas_call(
        body, name=name, grid=(nt,),
        out_shape=(jax.ShapeDtypeStruct((t, d_model), F32), jax.ShapeDtypeStruct((N_DEV, t, fs), _ACT_DTYPE),
                   jax.ShapeDtypeStruct((t, d_model), _ACT_DTYPE), jax.ShapeDtypeStruct((3, N_DEV, fs), F32),
                   jax.ShapeDtypeStruct((1, d_model), F32)),
        in_specs=[_rows(tm, d_model), _rows(tm, d_model), _const((None, 1, d_model), (layer, 0, 0)), tile3,
                  pl.BlockSpec((N_DEV, hb, fs), lambda i: (0, jnp.minimum((i + 1) * per_tile, t // hb - 1), 0)), tile3,
                  _const((None, 3, N_DEV, fs), (layer, 0, 0, 0)), _resident((None, N_DEV, d_model, fs), (w_layer, 0, 0, 0))],
        out_specs=(_rows(tm, d_model), tile3, _rows(tm, d_model), _const((3, N_DEV, fs)), _const((1, d_model))),
        compiler_params=_params(),
    )(dx, x, pre_g, duc, duc, u, conv_w, w_in)


def _tn_matmul(a, b, a_spec, b_spec, n_out, m, n, name, out_dtype):
    def body(a_ref, b_ref, o_ref):
        o_ref[...] = _mm_ta(a_ref[...], b_ref[...]).astype(o_ref.dtype)

    return pl.pallas_call(
        body, name=name, grid=(n_out,),
        out_shape=jax.ShapeDtypeStruct((n_out, m, n), out_dtype),
        in_specs=[a_spec, b_spec],
        out_specs=pl.BlockSpec((None, m, n), lambda c: (c, 0, 0)),
        compiler_params=_params(),
    )(a, b)


def _kv_fwd(x, kv_g, w_kv, cos, ssin, name):
    t, d_model = x.shape
    tm = _row_tile(t)
    kvd = w_kv.shape[1] // 2
    pairs = kvd // LANES

    def body(x_ref, g_ref, w_ref, cos_ref, sin_ref, k_ref, v_ref):
        kv = _mm(_rms_fwd(x_ref[...], g_ref[...]), w_ref[...])
        low = _low_half((tm, LANES))
        for j in range(pairs):
            kb = _rope_fwd(kv[:, j * LANES:(j + 1) * LANES], cos_ref[...], sin_ref[...])
            vb = kv[:, kvd + j * LANES:kvd + (j + 1) * LANES]
            for blk, ref in ((kb, k_ref), (vb, v_ref)):
                sw = pltpu.roll(blk, 64, 1)
                ref[2 * j] = jnp.where(low, blk, sw).astype(ref.dtype)
                ref[2 * j + 1] = jnp.where(low, sw, blk).astype(ref.dtype)

    heads = jax.ShapeDtypeStruct((N_KV_HEADS, t, LANES), _ACT_DTYPE)
    hspec = pl.BlockSpec((N_KV_HEADS, tm, LANES), lambda i: (0, i, 0))
    return pl.pallas_call(
        body, name=name, grid=(t // tm,), out_shape=(heads, heads),
        in_specs=[_rows(tm, d_model), _const((1, d_model)), _const(w_kv.shape), _rows(tm, LANES), _rows(tm, LANES)],
        out_specs=(hspec, hspec), compiler_params=_params(),
    )(x, kv_g, w_kv, cos, ssin)


def _kv_bwd(dx, x, kv_g, w_kv, cos, ssin, dks, dvs, name):
    t, d_model = x.shape
    tm = _row_tile(t)
    kvd = w_kv.shape[1] // 2
    pairs = kvd // LANES
    n_users = len(dks)

    def body(dx_ref, x_ref, g_ref, w_ref, cos_ref, sin_ref, *refs):
        dk_refs, dv_refs = refs[:n_users], refs[n_users:2 * n_users]
        dxi_ref, h_ref, dkv_ref, dg_ref = refs[2 * n_users:]
        dk_ref = functools.reduce(lambda a, b: a + b, [r[...] for r in dk_refs])
        dv_ref = functools.reduce(lambda a, b: a + b, [r[...] for r in dv_refs])
        i = pl.program_id(0)

        @pl.when(i == 0)
        def _():
            dg_ref[...] = jnp.zeros_like(dg_ref)

        xv = x_ref[...]
        g = g_ref[...]
        h_ref[...] = _rms_fwd(xv, g).astype(h_ref.dtype)
        low = _low_half((tm, LANES))
        dks, dvs = [], []
        for j in range(pairs):
            dkb = jnp.where(low, dk_ref[2 * j], dk_ref[2 * j + 1])
            dks.append(_rope_bwd(dkb, cos_ref[...], sin_ref[...]))
            dvs.append(jnp.where(low, dv_ref[2 * j], dv_ref[2 * j + 1]))
        dkv = jnp.concatenate(dks + dvs, axis=1).astype(_MXU_DTYPE)
        dkv_ref[...] = dkv
        dxp, dg = _rms_bwd(xv, g, _mm_tb(dkv, w_ref[...]))
        dg_ref[...] += dg
        dxi_ref[...] = dx_ref[...] + dxp

    hspec = pl.BlockSpec((N_KV_HEADS, tm, LANES), lambda i: (0, i, 0))
    return pl.pallas_call(
        body, name=name, grid=(t // tm,),
        out_shape=(jax.ShapeDtypeStruct((t, d_model), F32), jax.ShapeDtypeStruct((t, d_model), _ACT_DTYPE),
                   jax.ShapeDtypeStruct((t, 2 * kvd), _ACT_DTYPE), jax.ShapeDtypeStruct((1, d_model), F32)),
        in_specs=[_rows(tm, d_model), _rows(tm, d_model), _const((1, d_model)), _const(w_kv.shape),
                  _rows(tm, LANES), _rows(tm, LANES)] + [hspec] * (2 * n_users),
        out_specs=(_rows(tm, d_model), _rows(tm, d_model), _rows(tm, 2 * kvd), _const((1, d_model))),
        compiler_params=_params(),
    )(dx, x, kv_g, w_kv, cos, ssin, *dks, *dvs)


def _q_fwd(x, pre_g, w_q, cos, ssin, layer, j, name):
    t, d_model = x.shape
    tm = _row_tile(t)

    def body(x_ref, g_ref, w_ref, cos_ref, sin_ref, q_ref):
        q = _mm(_rms_fwd(x_ref[...], g_ref[...]), w_ref[...])
        for p in range(d_model // LANES):
            cols = slice(p * LANES, (p + 1) * LANES)
            q_ref[:, cols] = (_rope_fwd(q[:, cols], cos_ref[...], sin_ref[...]) * ATTN_SCALE).astype(q_ref.dtype)

    return pl.pallas_call(
        body, name=name, grid=(t // tm,), out_shape=jax.ShapeDtypeStruct((t, d_model), _ACT_DTYPE),
        in_specs=[_rows(tm, d_model), _const((None, 1, d_model), (layer, 0, 0)), _const((None, d_model, d_model), (j, 0, 0)),
                  _rows(tm, LANES), _rows(tm, LANES)],
        out_specs=_rows(tm, d_model), compiler_params=_params(),
    )(x, pre_g, w_q, cos, ssin)


def _q_bwd(dx, dqs, x, pre_g, w_q, cos, ssin, layer, j, name):
    t, d_model = x.shape
    tm = _row_tile(t)

    def body(dx_ref, dq_ref, x_ref, g_ref, w_ref, cos_ref, sin_ref, dxi_ref, h_ref, dqo_ref, dg_ref):
        i = pl.program_id(0)

        @pl.when(i == 0)
        def _():
            dg_ref[...] = jnp.zeros_like(dg_ref)

        xv = x_ref[...]
        g = g_ref[...]
        h_ref[...] = _rms_fwd(xv, g).astype(h_ref.dtype)
        parts = []
        for p in range(d_model // LANES):
            cols = slice(p * LANES, (p + 1) * LANES)
            parts.append(_rope_bwd(dq_ref[:, cols] * ATTN_SCALE, cos_ref[...], sin_ref[...]))
        dq = jnp.concatenate(parts, axis=1).astype(_MXU_DTYPE)
        dqo_ref[...] = dq
        dxp, dg = _rms_bwd(xv, g, _mm_tb(dq, w_ref[...]))
        dg_ref[...] += dg
        dxi_ref[...] = dx_ref[...] + dxp

    act = jax.ShapeDtypeStruct((t, d_model), _ACT_DTYPE)
    return pl.pallas_call(
        body, name=name, grid=(t // tm,),
        out_shape=(jax.ShapeDtypeStruct((t, d_model), F32), act, act, jax.ShapeDtypeStruct((1, d_model), F32)),
        in_specs=[_rows(tm, d_model), _rows(tm, d_model), _rows(tm, d_model), _const((None, 1, d_model), (layer, 0, 0)),
                  _const((None, d_model, d_model), (j, 0, 0)), _rows(tm, LANES), _rows(tm, LANES)],
        out_specs=(_rows(tm, d_model), _rows(tm, d_model), _rows(tm, d_model), _const((1, d_model))),
        compiler_params=_params(),
    )(dx, dqs, x, pre_g, w_q, cos, ssin)


def _attn_scores(q_pair, k2, n, sink_a, sink_b):
    low = _low_half(q_pair.shape)
    zero = jnp.zeros_like(q_pair)
    qst = jnp.concatenate([jnp.where(low, q_pair, zero), jnp.where(low, zero, q_pair)], axis=0)
    s = _mm_tb(qst, k2)
    row = lax.broadcasted_iota(jnp.int32, s.shape, 0)
    col = lax.broadcasted_iota(jnp.int32, s.shape, 1)
    rel = BLOCK + (row & (BLOCK - 1)) - col
    valid = (rel >= 0) & (rel < WINDOW) & (n * BLOCK + col - BLOCK >= 0)
    s = jnp.where(valid, s, NEG_INF)
    rows1 = lax.broadcasted_iota(jnp.int32, (2 * BLOCK, 1), 0)
    sink = jnp.where(rows1 < BLOCK, sink_a, sink_b)
    return qst, s, sink


def _attn_fwd(qs, kdup, vdup, sinks, j, name):
    t, d_model = qs.shape
    nb = t // BLOCK
    n_pairs = d_model // LANES
    per_group = n_pairs // N_KV_HEADS

    def body(sink_ref, q_ref, kp_ref, ko_ref, vp_ref, vo_ref, o_ref, lse_ref):
        n = pl.program_id(0)
        low = _low_half((BLOCK, LANES))
        lane = _lane_iota((BLOCK, LANES))
        lse = jnp.zeros((BLOCK, LANES), F32)
        for p in range(n_pairs):
            hk = p // per_group
            k2 = jnp.concatenate([kp_ref[hk], ko_ref[hk]], axis=0)
            v2 = jnp.concatenate([vp_ref[hk], vo_ref[hk]], axis=0)
            _, s, sink = _attn_scores(q_ref[:, p * LANES:(p + 1) * LANES], k2, n, sink_ref[j, 2 * p], sink_ref[j, 2 * p + 1])
            m = jnp.maximum(jnp.max(s, axis=-1, keepdims=True), sink)
            pe = jnp.exp(s - m)
            denom = jnp.sum(pe, axis=-1, keepdims=True) + jnp.exp(sink - m)
            o2 = _mm(pe, v2) / denom
            o_ref[:, p * LANES:(p + 1) * LANES] = jnp.where(low, o2[:BLOCK], o2[BLOCK:]).astype(o_ref.dtype)
            l2 = m + jnp.log(denom)
            lse = jnp.where(lane == 2 * p, l2[:BLOCK], lse)
            lse = jnp.where(lane == 2 * p + 1, l2[BLOCK:], lse)
        lse_ref[...] = lse

    prev = pl.BlockSpec((N_KV_HEADS, BLOCK, LANES), lambda n: (0, jnp.maximum(n - 1, 0), 0))
    own = pl.BlockSpec((N_KV_HEADS, BLOCK, LANES), lambda n: (0, n, 0))
    return pl.pallas_call(
        body, name=name, grid=(nb,),
        out_shape=(jax.ShapeDtypeStruct((t, d_model), _ACT_DTYPE), jax.ShapeDtypeStruct((t, LANES), F32)),
        in_specs=[pl.BlockSpec(memory_space=pltpu.SMEM), _rows(BLOCK, d_model), prev, own, prev, own],
        out_specs=(_rows(BLOCK, d_model), _rows(BLOCK, LANES)), compiler_params=_params(),
    )(sinks, qs, kdup, kdup, vdup, vdup)


def _attn_bwd(qs, kdup, vdup, sinks, lse, do, j, name):
    t, d_model = qs.shape
    nb = t // BLOCK
    n_pairs = d_model // LANES
    per_group = n_pairs // N_KV_HEADS
    rev = lambda n: nb - 1 - n

    def body(sink_ref, q_ref, kp_ref, ko_ref, vp_ref, vo_ref, lse_ref, do_ref, dq_ref, dk_ref, dv_ref, ds_ref, ck, cv):
        i = pl.program_id(0)
        n = nb - 1 - i

        @pl.when(i == 0)
        def _():
            ck[...] = jnp.zeros_like(ck)
            cv[...] = jnp.zeros_like(cv)
            ds_ref[...] = jnp.zeros_like(ds_ref)

        low = _low_half((BLOCK, LANES))
        lane = _lane_iota((BLOCK, LANES))
        lane1 = _lane_iota((1, LANES))
        lsev = lse_ref[...]
        dsink = jnp.zeros((1, LANES), F32)
        dk2 = [jnp.zeros((2 * BLOCK, LANES), F32) for _ in range(N_KV_HEADS)]
        dv2 = [jnp.zeros((2 * BLOCK, LANES), F32) for _ in range(N_KV_HEADS)]
        for p in range(n_pairs):
            hk = p // per_group
            k2 = jnp.concatenate([kp_ref[hk], ko_ref[hk]], axis=0)
            v2 = jnp.concatenate([vp_ref[hk], vo_ref[hk]], axis=0)
            qst, s, sink = _attn_scores(q_ref[:, p * LANES:(p + 1) * LANES], k2, n, sink_ref[j, 2 * p], sink_ref[j, 2 * p + 1])
            l2 = jnp.concatenate([jnp.sum(jnp.where(lane == 2 * p, lsev, 0.0), axis=-1, keepdims=True),
                                  jnp.sum(jnp.where(lane == 2 * p + 1, lsev, 0.0), axis=-1, keepdims=True)], axis=0)
            pn = jnp.exp(s - l2)
            dop = do_ref[:, p * LANES:(p + 1) * LANES]
            zero = jnp.zeros_like(dop)
            dost = jnp.concatenate([jnp.where(low, dop, zero), jnp.where(low, zero, dop)], axis=0)
            dp = _mm_tb(dost, v2)
            dr = jnp.sum(pn * dp, axis=-1, keepdims=True)
            dsm = (pn * (dp - dr)).astype(_MXU_DTYPE)
            dsk = -jnp.exp(sink - l2) * dr
            dsink = dsink + jnp.where(lane1 == 2 * p, jnp.sum(dsk[:BLOCK]), 0.0) + jnp.where(lane1 == 2 * p + 1, jnp.sum(dsk[BLOCK:]), 0.0)
            dq2 = _mm(dsm, k2)
            dq_ref[:, p * LANES:(p + 1) * LANES] = jnp.where(low, dq2[:BLOCK], dq2[BLOCK:])
            dk2[hk] = dk2[hk] + _mm_ta(dsm, qst)
            dv2[hk] = dv2[hk] + _mm_ta(pn, dost)
        ds_ref[...] += dsink
        for hk in range(N_KV_HEADS):
            for acc, carry, ref in ((dk2[hk], ck, dk_ref), (dv2[hk], cv, dv_ref)):
                folded = acc + pltpu.roll(acc, 64, 1)
                ref[hk] = folded[BLOCK:] + carry[hk]
                carry[hk] = folded[:BLOCK]

    prev = pl.BlockSpec((N_KV_HEADS, BLOCK, LANES), lambda n: (0, jnp.maximum(rev(n) - 1, 0), 0))
    own = pl.BlockSpec((N_KV_HEADS, BLOCK, LANES), lambda n: (0, rev(n), 0))
    rows = lambda cols: pl.BlockSpec((BLOCK, cols), lambda n: (rev(n), 0))
    heads = jax.ShapeDtypeStruct((N_KV_HEADS, t, LANES), F32)
    return pl.pallas_call(
        body, name=name, grid=(nb,),
        out_shape=(jax.ShapeDtypeStruct((t, d_model), F32), heads, heads, jax.ShapeDtypeStruct((1, LANES), F32)),
        in_specs=[pl.BlockSpec(memory_space=pltpu.SMEM), rows(d_model), prev, own, prev, own, rows(LANES), rows(d_model)],
        out_specs=(rows(d_model), own, own, _const((1, LANES))),
        scratch_shapes=[pltpu.VMEM((N_KV_HEADS, BLOCK, LANES), F32), pltpu.VMEM((N_KV_HEADS, BLOCK, LANES), F32)],
        compiler_params=_params(),
    )(sinks, qs, kdup, kdup, vdup, vdup, lse, do)


def _oproj_fwd(x, o, w_o, post_g, layer, j, name):
    t, d_model = x.shape
    tm = _row_tile(t)

    def body(x_ref, o_ref, w_ref, g_ref, xo_ref, mo_ref):
        mo = _mm(o_ref[...], w_ref[...])
        mo_ref[...] = mo
        xo_ref[...] = x_ref[...] + _rms_fwd(mo, g_ref[...])

    full = jax.ShapeDtypeStruct((t, d_model), F32)
    return pl.pallas_call(
        body, name=name, grid=(t // tm,), out_shape=(full, full),
        in_specs=[_rows(tm, d_model), _rows(tm, d_model), _const((None, d_model, d_model), (j, 0, 0)),
                  _const((None, 1, d_model), (layer, 0, 0))],
        out_specs=(_rows(tm, d_model), _rows(tm, d_model)), compiler_params=_params(),
    )(x, o, w_o, post_g)


def _oproj_bwd(dx, mo, w_o, post_g, layer, j, name):
    t, d_model = dx.shape
    tm = _row_tile(t)

    def body(dx_ref, mo_ref, w_ref, g_ref, do_ref, dmo_ref, dg_ref):
        i = pl.program_id(0)

        @pl.when(i == 0)
        def _():
            dg_ref[...] = jnp.zeros_like(dg_ref)

        dmo, dg = _rms_bwd(mo_ref[...], g_ref[...], dx_ref[...])
        dg_ref[...] += dg
        dmo = dmo.astype(_MXU_DTYPE)
        dmo_ref[...] = dmo
        do_ref[...] = _mm_tb(dmo, w_ref[...]).astype(do_ref.dtype)

    act = jax.ShapeDtypeStruct((t, d_model), _ACT_DTYPE)
    return pl.pallas_call(
        body, name=name, grid=(t // tm,), out_shape=(act, act, jax.ShapeDtypeStruct((1, d_model), F32)),
        in_specs=[_rows(tm, d_model), _rows(tm, d_model), _const((None, d_model, d_model), (j, 0, 0)),
                  _const((None, 1, d_model), (layer, 0, 0))],
        out_specs=(_rows(tm, d_model), _rows(tm, d_model), _const((1, d_model))), compiler_params=_params(),
    )(dx, mo, w_o, post_g)


def _loss_grad(y, target, name):
    t, d_model = y.shape
    tm = _row_tile(t)

    def body(y_ref, t_ref, dy_ref, loss_ref):
        i = pl.program_id(0)

        @pl.when(i == 0)
        def _():
            loss_ref[...] = jnp.zeros_like(loss_ref)

        err = y_ref[...] - t_ref[...]
        dy_ref[...] = err / d_model
        loss_ref[...] += 0.5 * jnp.sum(jnp.mean(err * err, axis=-1, keepdims=True), axis=0, keepdims=True)

    return pl.pallas_call(
        body, name=name, grid=(t // tm,),
        out_shape=(jax.ShapeDtypeStruct((t, d_model), F32), jax.ShapeDtypeStruct((1, 1), F32)),
        in_specs=[_rows(tm, d_model), _rows(tm, d_model)], out_specs=(_rows(tm, d_model), _const((1, 1))),
        compiler_params=_params(),
    )(y, target)


def _mesh_position():
    return lax.axis_index("x"), lax.axis_index("y"), lax.axis_index("c")


def _block_of(px, py, pc):
    return 4 * px + 2 * py + pc


def _at_block(ref, axis, block):
    return ref.at[(slice(None),) * axis + (block,)]


def _all_gather(shards, axes, name):
    n = len(shards)

    def body(*refs):
        srcs, outs = refs[:n], refs[n:2 * n]
        send_sems, recv_sems, local_sems = refs[2 * n:]
        x, y, c = _mesh_position()
        me, sibling = (x, y, c), (x, y, 1 - c)
        chips = [(1 - x, y), (x, 1 - y), (1 - x, 1 - y)]

        def blk(i, pos):
            return _at_block(outs[i], axes[i], _block_of(*pos))

        def copy(i, k, block, to, src=None):
            return pltpu.make_async_remote_copy(
                src_ref=blk(i, block) if src is None else src, dst_ref=blk(i, block),
                send_sem=send_sems.at[i, k], recv_sem=recv_sems.at[i, k], device_id=to, device_id_type=MESH)

        mine = [pltpu.make_async_copy(srcs[i], blk(i, me), local_sems.at[i]) for i in range(n)]
        for cp in mine:
            cp.start()
        sent = []
        for i in range(n):
            sent += [copy(i, 1 + k, me, (*chip, c), src=srcs[i]) for k, chip in enumerate(chips)]
            sent.append(copy(i, 0, me, sibling, src=srcs[i]))
        for cp in sent:
            cp.start()
        for i in range(n):
            for k, chip in enumerate(chips):
                copy(i, 1 + k, (*chip, c), me).wait_recv()
                passed = copy(i, 4 + k, (*chip, c), sibling)
                passed.start()
                sent.append(passed)
        for i in range(n):
            copy(i, 0, sibling, me).wait_recv()
            for k, chip in enumerate(chips):
                copy(i, 4 + k, (*chip, 1 - c), me).wait_recv()
        for cp in sent:
            cp.wait_send()
        for cp in mine:
            cp.wait()

    hbm = pl.BlockSpec(memory_space=pl.ANY)
    return pl.pallas_call(
        body, name=name,
        out_shape=tuple(jax.ShapeDtypeStruct(s.shape[:a] + (N_DEV,) + s.shape[a:], s.dtype) for s, a in zip(shards, axes)),
        in_specs=[hbm] * n, out_specs=(hbm,) * n,
        scratch_shapes=[pltpu.SemaphoreType.DMA((n, 7)), pltpu.SemaphoreType.DMA((n, 7)), pltpu.SemaphoreType.DMA((n,))],
    )(*shards)


GATHER, SCATTER = "gather", "scatter"
N_PEERS = N_DEV - 1


def _land_shape(kind, s, axis):
    if kind == GATHER:
        return s.shape[:axis] + (N_DEV,) + s.shape[axis:]
    return (N_DEV,) + s.shape[:axis] + s.shape[axis + 1:]


def _direct_plan(kind, srcs, lands, axes):
    x, y, c = _mesh_position()
    my_block = _block_of(x, y, c)
    peers = []
    for k in range(1, N_DEV):
        fx, fy, fc = (k >> 2) & 1, (k >> 1) & 1, k & 1
        peers.append((1 - x if fx else x, 1 - y if fy else y, 1 - c if fc else c))
    remote, local = [], []
    for src, land, axis in zip(srcs, lands, axes):
        if kind == GATHER:
            mine = _at_block(land, axis, my_block)
            local.append((src, mine))
            remote += [(src, mine, peer, _at_block(land, axis, _block_of(*peer))) for peer in peers]
        else:
            mine = land.at[my_block]
            local.append((_at_block(src, axis, my_block), mine))
            remote += [(_at_block(src, axis, _block_of(*peer)), mine, peer, land.at[_block_of(*peer)]) for peer in peers]
    return remote, local


def _remote(src, dst, send_sems, recv_sems, k, peer):
    return pltpu.make_async_remote_copy(src_ref=src, dst_ref=dst, send_sem=send_sems.at[k], recv_sem=recv_sems.at[k],
                                        device_id=peer, device_id_type=MESH)


def _exchange(kind, arrays, axes, name):
    n = len(arrays)

    def body(*refs):
        srcs, lands = refs[:n], refs[n:2 * n]
        send_sems, recv_sems, local_sems = refs[2 * n:]
        remote, local = _direct_plan(kind, srcs, lands, axes)
        mine = [pltpu.make_async_copy(src, dst, local_sems.at[i]) for i, (src, dst) in enumerate(local)]
        for cp in mine:
            cp.start()
        for k, (src, dst, peer, _) in enumerate(remote):
            _remote(src, dst, send_sems, recv_sems, k, peer).start()
        for k, (src, _, peer, arrives) in enumerate(remote):
            _remote(src, arrives, send_sems, recv_sems, k, peer).wait_recv()
        for k, (src, dst, peer, _) in enumerate(remote):
            _remote(src, dst, send_sems, recv_sems, k, peer).wait_send()
        for cp in mine:
            cp.wait()

    hbm = pl.BlockSpec(memory_space=pl.ANY)
    return pl.pallas_call(
        body, name=name,
        out_shape=tuple(jax.ShapeDtypeStruct(_land_shape(kind, s, a), s.dtype) for s, a in zip(arrays, axes)),
        in_specs=[hbm] * n, out_specs=(hbm,) * n,
        scratch_shapes=[pltpu.SemaphoreType.DMA((n * N_PEERS,)), pltpu.SemaphoreType.DMA((n * N_PEERS,)),
                        pltpu.SemaphoreType.DMA((n,))],
    )(*arrays)


_HBM = pl.BlockSpec(memory_space=pltpu.HBM)
_SEM = pl.BlockSpec(memory_space=pltpu.SEMAPHORE)
_SPLIT = dict(has_side_effects=pltpu.SideEffectType.DATAFLOW_SIDE_EFFECTING)


def _exchange_start(kind, arrays, axes, after, name):
    n = len(arrays)
    lands = [lax.empty(_land_shape(kind, s, a), s.dtype) for s, a in zip(arrays, axes)]

    def body(*refs):
        srcs, land_refs = refs[:n], refs[n:2 * n]
        send_sems, recv_sems = refs[2 * n + 1], refs[2 * n + 2]
        token = refs[-1]
        remote, _ = _direct_plan(kind, srcs, land_refs, axes)
        for k, (src, dst, peer, _) in enumerate(remote):
            _remote(src, dst, send_sems, recv_sems, k, peer).start()
        token[...] = jnp.zeros_like(token)

    both = list(arrays) + lands
    out = pl.pallas_call(
        body, name=name,
        out_shape=(pltpu.SemaphoreType.DMA((n * N_PEERS,)), pltpu.SemaphoreType.DMA((n * N_PEERS,)),
                   *[pltpu.HBM(a.shape, a.dtype) for a in both], jax.ShapeDtypeStruct((8, LANES), F32)),
        in_specs=[_HBM] * (2 * n) + [pl.BlockSpec(memory_space=pl.ANY)],
        out_specs=(_SEM, _SEM, *[_HBM] * (2 * n), pl.BlockSpec(memory_space=pltpu.VMEM)),
        input_output_aliases={i: 2 + i for i in range(2 * n)},
        compiler_params=pltpu.CompilerParams(**_SPLIT),
    )(*[pltpu.with_memory_space_constraint(a, pltpu.HBM) for a in both], after)
    return (kind, axes, n, out[:-1]), out[-1]


def _exchange_wait(handle, after, name):
    kind, axes, n, (send_sems, recv_sems, *thru) = handle

    def body(*refs):
        srcs, land_refs = refs[:n], refs[n:2 * n]
        send_sems, recv_sems = refs[2 * n], refs[2 * n + 1]
        local_sems = refs[-1]
        remote, local = _direct_plan(kind, srcs, land_refs, axes)
        for k, (src, dst, peer, arrives) in enumerate(remote):
            cp = _remote(src, arrives, send_sems, recv_sems, k, peer)
            cp.wait_send()
            cp.wait_recv()
        mine = [pltpu.make_async_copy(src, dst, local_sems.at[i]) for i, (src, dst) in enumerate(local)]
        for cp in mine:
            cp.start()
        for cp in mine:
            cp.wait()

    out = pl.pallas_call(
        body, name=name,
        out_shape=tuple(pltpu.HBM(a.shape, a.dtype) for a in thru),
        in_specs=[_HBM] * (2 * n) + [_SEM, _SEM, pl.BlockSpec(memory_space=pl.ANY)], out_specs=(_HBM,) * (2 * n),
        input_output_aliases={i: i for i in range(2 * n)},
        scratch_shapes=[pltpu.SemaphoreType.DMA((n,))],
        compiler_params=pltpu.CompilerParams(**_SPLIT),
    )(*thru, send_sems, recv_sems, after)
    return out[n:]


def _adamw_math(w, g, m, v):
    m = ADAM_B1 * m + (1.0 - ADAM_B1) * g
    v = ADAM_B2 * v + (1.0 - ADAM_B2) * jnp.square(g)
    m_hat = m / (1.0 - ADAM_B1 ** ADAM_STEP)
    v_hat = v / (1.0 - ADAM_B2 ** ADAM_STEP)
    delta = -ADAM_LR * (m_hat / (jnp.sqrt(v_hat) + ADAM_EPS) + ADAM_WD * w)
    return delta, m, v


def _update_tile(rows):
    if rows <= 512:
        return rows
    for tr in (512, 384, 352, 256, 176, 128, 64, 32, 16):
        if rows % tr == 0:
            return tr
    raise ValueError(f"{rows} rows do not tile")


def _adamw(parts, w, m, v, slab, so_far, name):
    rows, c = w.shape
    r = parts.shape[1]
    tr = _update_tile(r)
    first = slab * (r // tr)
    if so_far is None:
        so_far = tuple(lax.empty((rows, c), F32) for _ in range(4))

    def body(p_ref, w_ref, m_ref, v_ref, *refs):
        g_ref, d_ref, mo_ref, vo_ref = refs[4:]
        g = p_ref[0].astype(F32)
        for s in range(1, N_DEV):
            g = g + p_ref[s].astype(F32)
        g_ref[...] = g
        d_ref[...], mo_ref[...], vo_ref[...] = _adamw_math(w_ref[...], g, m_ref[...], v_ref[...])

    out = jax.ShapeDtypeStruct((rows, c), F32)
    tile = pl.BlockSpec((tr, c), lambda i: (first + i, 0))
    return pl.pallas_call(
        body, name=name, grid=(r // tr,), out_shape=(out,) * 4,
        in_specs=[pl.BlockSpec((N_DEV, tr, c), lambda i: (0, i, 0))] + [tile] * 3 + [pl.BlockSpec(memory_space=pl.ANY)] * 4,
        out_specs=(tile,) * 4, input_output_aliases={4 + k: k for k in range(4)}, compiler_params=_params(),
    )(parts, w, m, v, *so_far)


def _sum_parts(parts, name):
    _, r, c = parts.shape
    tr = _update_tile(r)

    def body(p_ref, g_ref):
        g = p_ref[0].astype(F32)
        for s in range(1, N_DEV):
            g = g + p_ref[s].astype(F32)
        g_ref[...] = g

    return pl.pallas_call(
        body, name=name, grid=(r // tr,), out_shape=jax.ShapeDtypeStruct((r, c), F32),
        in_specs=[pl.BlockSpec((N_DEV, tr, c), lambda i: (0, i, 0))], out_specs=_rows(tr, c), compiler_params=_params(),
    )(parts)


def _adamw_plain(g, w, m, v, name):
    r, c = w.shape
    tr = _update_tile(r)

    def body(g_ref, w_ref, m_ref, v_ref, d_ref, mo_ref, vo_ref):
        d_ref[...], mo_ref[...], vo_ref[...] = _adamw_math(w_ref[...], g_ref[...], m_ref[...], v_ref[...])

    out = jax.ShapeDtypeStruct((r, c), F32)
    return pl.pallas_call(
        body, name=name, grid=(r // tr,), out_shape=(out,) * 3, in_specs=[_rows(tr, c)] * 4, out_specs=(_rows(tr, c),) * 3,
        compiler_params=_params(),
    )(g, w, m, v)


def _pack(arrays):
    flat = []
    for a in arrays:
        a = a.reshape(-1)
        flat.append(jnp.pad(a, (0, (-a.shape[0]) % LANES)))
    flat = jnp.concatenate(flat)
    flat = jnp.pad(flat, (0, (-flat.shape[0]) % (LANES * LANES)))
    return flat.reshape(-1, LANES)


def _unpack(packed, shapes):
    flat = packed.reshape(-1)
    out, at = [], 0
    for shp in shapes:
        size = math.prod(shp)
        out.append(flat[at:at + size].reshape(shp))
        at += size + (-size) % LANES
    return out


def kernel(x, positions, mix_pre_g, mix_post_g, pool_w, pool_scale, kv_norm_g, w_kv, w_q, w_o, sinks, ffn_pre_g, ffn_post_g, ffn_w_in, ffn_conv_w, ffn_conv_b, ffn_w_out, loss_target, m_mix_pre_g, m_mix_post_g, m_pool_w, m_pool_scale, m_kv_norm_g, m_w_kv, m_w_q, m_w_o, m_sinks, m_ffn_pre_g, m_ffn_post_g, m_ffn_w_in, m_ffn_conv_w, m_ffn_conv_b, m_ffn_w_out, v_mix_pre_g, v_mix_post_g, v_pool_w, v_pool_scale, v_kv_norm_g, v_w_kv, v_w_q, v_w_o, v_sinks, v_ffn_pre_g, v_ffn_post_g, v_ffn_w_in, v_ffn_conv_w, v_ffn_conv_b, v_ffn_w_out):
    depth, d_model = mix_pre_g.shape
    n_a = pool_w.shape[0]
    n_b = w_q.shape[0]
    t = x.shape[1]
    fs = ffn_w_in.shape[2]
    half = N_DEV // 2
    n_heads = d_model // HEAD_DIM
    me = _block_of(*_mesh_position())
    x0 = x.reshape(t, d_model)
    target = loss_target.reshape(t, d_model)

    inv_freq = 1.0 / (ROPE_THETA ** (jnp.arange(0, HEAD_DIM, 2, dtype=F32) / HEAD_DIM))
    ang = positions.reshape(t).astype(F32)[:, None] * inv_freq
    cos, sin = jnp.cos(ang), jnp.sin(ang)
    cos = jnp.tile(cos, (1, 2 * LANES // HEAD_DIM))
    ssin = jnp.tile(jnp.concatenate([-sin, sin], axis=1), (1, LANES // HEAD_DIM))

    wire = lambda a: a.astype(_WIRE_DTYPE)
    w_in_b, w_out_b = wire(ffn_w_in), wire(ffn_w_out)
    pool_w_g, w_in_0, w_out_0 = _all_gather([wire(pool_w), w_in_b[:1], w_out_b[:1]], [2, 1, 1], "gather_first")
    small_shapes = [pool_scale.shape, ffn_conv_w.shape]
    (small_g,) = _all_gather([_pack([pool_scale, ffn_conv_w])], [0], "gather_small")
    pending, tokens, after = {}, [], small_g
    for l in range(1, depth):
        if l == n_a:
            pending["attn"], after = _exchange_start(GATHER, [wire(w_kv), wire(w_q), wire(w_o)], [0, 1, 1], after, "gather_start_attn")
            tokens.append(after)
        pending[l], after = _exchange_start(GATHER, [w_in_b[l:l + 1], w_out_b[l:l + 1]], [1, 1], after, f"gather_start_{l}")
        tokens.append(after)
    started = functools.reduce(lambda a, b: a + b, [tk[0, 0] for tk in tokens])
    w_in_l, w_out_l = {0: w_in_0}, {0: w_out_0.reshape(1, half * fs, d_model)}
    small_g = [_unpack(small_g[b], small_shapes) for b in range(N_DEV)]
    pool_scale_f = jnp.concatenate([s[0] for s in small_g], axis=1).reshape(n_a, 1, d_model)
    conv_w_f = jnp.stack([s[1] for s in small_g], axis=2)
    pool_w_f = pool_w_g.reshape(n_a, len(POOL_WINDOWS), d_model // len(POOL_WINDOWS), -1)
    conv_b_f = ffn_conv_b.reshape(depth, N_DEV, fs)
    g3 = lambda a: a.reshape(a.shape[0], 1, a.shape[1])
    mix_pre, mix_post, ffn_pre, ffn_post = g3(mix_pre_g) + started, g3(mix_post_g), g3(ffn_pre_g), g3(ffn_post_g)
    kv_g = kv_norm_g.reshape(1, d_model)
    w_kv_f = w_q_f = w_o_f = None

    saved = []
    xc = x0
    kdup = vdup = x_kv = None
    for l in range(depth):
        x_in = xc
        if l < n_a:
            x_mid, dsave, yu = _pool_fwd(x_in, mix_pre, mix_post, pool_w_f, pool_scale_f, l, f"pool_fwd_{l}")
            mixer = (dsave, yu)
        else:
            j = l - n_a
            if j == 0:
                x_kv = x_in
                w_kv_g, w_q_g, w_o_g = _exchange_wait(pending["attn"], x_in, "gather_wait_attn")
                w_kv_f = w_kv_g.reshape(d_model, -1)
                w_q_f = w_q_g.reshape(n_b, d_model, d_model)
                w_o_f = w_o_g.reshape(n_b, d_model, d_model)
                kdup, vdup = _kv_fwd(x_kv, kv_g, w_kv_f, cos, ssin, "kv_fwd")
            qs = _q_fwd(x_in, mix_pre, w_q_f, cos, ssin, l, j, f"q_fwd_{l}")
            o, lse = _attn_fwd(qs, kdup, vdup, sinks, j, f"attn_fwd_{l}")
            x_mid, mo = _oproj_fwd(x_in, o, w_o_f, mix_post, l, j, f"oproj_fwd_{l}")
            mixer = (qs, o, lse, mo)
        if l > 0:
            w_in_l[l], w_out_g = _exchange_wait(pending[l], x_mid, f"gather_wait_{l}")
            w_out_l[l] = w_out_g.reshape(1, half * fs, d_model)
        xc, u, uc, f = _ffn_fwd(x_mid, ffn_pre, ffn_post, w_in_l[l], conv_w_f, conv_b_f, w_out_l[l], l, 0, f"ffn_fwd_{l}")
        saved.append((x_in, x_mid, u, uc, f, mixer))

    dx, loss_part = _loss_grad(xc, target, "loss")
    loss = lax.psum(loss_part[0, 0], AXES)

    gconv_w, gconv_b = [None] * depth, [None] * depth
    gmix_pre, gmix_post, gffn_pre, gffn_post = [None] * depth, [None] * depth, [None] * depth, [None] * depth
    gpool_scale, gsinks = [None] * n_a, [None] * n_b
    dks, dvs = [], []
    gkv_g = None
    whole = lambda cols: pl.BlockSpec((t, cols), lambda c: (0, 0), pipeline_mode=pl.Buffered(1))
    per_out = lambda cols: pl.BlockSpec((None, t, cols), lambda c: (c, 0, 0))
    by_dev = lambda g: g.reshape(N_DEV, -1, g.shape[-1])
    flying = []
    post = mix_post
    ffn_post_b = ffn_post
    for l in reversed(range(depth)):
        x_in, x_mid, u, uc, f, mixer = saved[l]
        duc, gact, df, gconv_b[l], gffn_post[l] = _ffn_bwd_out(dx, f, ffn_post_b, uc, conv_b_f, w_out_l[l], l, 0, f"ffn_bwd_out_{l}")
        dx, du, hf, gconv_w[l], gffn_pre[l] = _ffn_bwd_in(dx, x_mid, ffn_pre, duc, u, conv_w_f, w_in_l[l], l, 0, f"ffn_bwd_in_{l}")
        going = []
        if l < n_a:
            dsave, yu = mixer
            dx, dyu, gpool_scale[l], gmix_pre[l], gmix_post[l] = _pool_bwd(
                dx, x_in, mix_pre, post, dsave, yu, pool_w_f, pool_scale_f, l, f"pool_bwd_{l}")
            gc = d_model // len(POOL_WINDOWS)
            by_group = pl.BlockSpec((t, gc), lambda c: (0, c))
            gpool = _tn_matmul(dsave, dyu, by_group, by_group, len(POOL_WINDOWS), gc, gc, f"grad_pool_w_{l}", _WIRE_DTYPE)
            going.append(("pool_w", l, gpool.reshape(len(POOL_WINDOWS), N_DEV, -1, gc), 1))
        else:
            j = l - n_a
            qs, o, lse, mo = mixer
            do, dmo, gmix_post[l] = _oproj_bwd(dx, mo, w_o_f, post, l, j, f"oproj_bwd_{l}")
            dqs, dk, dv, gsinks[j] = _attn_bwd(qs, kdup, vdup, sinks, lse, do, j, f"attn_bwd_{l}")
            dks.append(dk)
            dvs.append(dv)
            dx, hq, dq, gmix_pre[l] = _q_bwd(dx, dqs, x_in, mix_pre, w_q_f, cos, ssin, l, j, f"q_bwd_{l}")
            if j == 0:
                dx, hkv, dkv, gkv_g = _kv_bwd(dx, x_kv, kv_g, w_kv_f, cos, ssin, dks, dvs, "kv_bwd")
                gkv = _tn_matmul(hkv, dkv, whole(d_model), whole(dkv.shape[1]), 1, d_model, dkv.shape[1], "grad_w_kv", _WIRE_DTYPE)
                going.append(("w_kv", 0, by_dev(gkv), 0))
            go = _tn_matmul(o, dmo, whole(d_model), whole(d_model), 1, d_model, d_model, f"grad_w_o_{l}", _WIRE_DTYPE)
            gq = _tn_matmul(hq, dq, whole(d_model), whole(d_model), 1, d_model, d_model, f"grad_w_q_{l}", _WIRE_DTYPE)
            going += [("w_o", j, by_dev(go), 0), ("w_q", j, by_dev(gq), 0)]
        gout = _tn_matmul(gact, df, per_out(fs), whole(d_model), half, fs, d_model, f"grad_w_out_{l}", _WIRE_DTYPE)
        gin = _tn_matmul(hf, du, whole(d_model), per_out(fs), N_DEV, d_model, fs, f"grad_w_in_{l}", _WIRE_DTYPE)
        going += [("ffn_w_out", l, by_dev(gout), 0), ("ffn_w_in", l, gin, 0)]
        handle, token = _exchange_start(SCATTER, [g for _, _, g, _ in going], [a for _, _, _, a in going], dx, f"scatter_start_{l}")
        flying.append(([(nm, slab) for nm, slab, _, _ in going], handle))
        ffn_post_b = ffn_post_b + token[0, 0]

    grad_x = dx.reshape(x.shape)

    shard = {"pool_w": (pool_w, m_pool_w, v_pool_w), "w_kv": (w_kv, m_w_kv, v_w_kv), "w_q": (w_q, m_w_q, v_w_q),
             "w_o": (w_o, m_w_o, v_w_o), "ffn_w_in": (ffn_w_in, m_ffn_w_in, v_ffn_w_in),
             "ffn_w_out": (ffn_w_out, m_ffn_w_out, v_ffn_w_out)}
    big = {}
    for names, handle in flying:
        parts = _exchange_wait(handle, token, f"scatter_wait_{names[-1][1]}")
        for (nm, slab), p in zip(names, parts):
            cols = p.shape[-1]
            w2, m2, v2 = (a.reshape(-1, cols) for a in shard[nm])
            big[nm] = _adamw(p.reshape(N_DEV, -1, cols), w2, m2, v2, slab, big.get(nm), f"adamw_{nm}_{slab}")
    big = {nm: tuple(r.reshape(shard[nm][0].shape) for r in res) for nm, res in big.items()}

    cat = lambda rows: jnp.concatenate(rows, axis=0)
    small_partials = [cat(gmix_pre), cat(gmix_post), gkv_g.reshape(d_model), cat([s[:, :n_heads] for s in gsinks]),
                      cat(gffn_pre), cat(gffn_post), jnp.stack(gconv_b).reshape(depth, N_DEV * fs),
                      cat(gpool_scale), jnp.stack(gconv_w)]
    small_shapes = [(depth, d_model), (depth, d_model), (d_model,), (n_b, n_heads), (depth, d_model), (depth, d_model),
                    (depth, N_DEV * fs), (n_a, d_model), (depth, 3, N_DEV, fs)]
    (small_parts,) = _all_gather([_pack(small_partials)], [0], "gather_small_grads")
    small_sum = _unpack(_sum_parts(small_parts, "sum_small_grads"), small_shapes)
    g_pool_scale = lax.dynamic_slice_in_dim(small_sum[7], me * pool_scale.shape[1], pool_scale.shape[1], axis=1)
    g_conv_w = lax.dynamic_index_in_dim(small_sum[8], me, axis=2, keepdims=False)
    small = [("mix_pre_g", small_sum[0], mix_pre_g, m_mix_pre_g, v_mix_pre_g),
             ("mix_post_g", small_sum[1], mix_post_g, m_mix_post_g, v_mix_post_g),
             ("kv_norm_g", small_sum[2], kv_norm_g, m_kv_norm_g, v_kv_norm_g),
             ("sinks", small_sum[3], sinks, m_sinks, v_sinks),
             ("ffn_pre_g", small_sum[4], ffn_pre_g, m_ffn_pre_g, v_ffn_pre_g),
             ("ffn_post_g", small_sum[5], ffn_post_g, m_ffn_post_g, v_ffn_post_g),
             ("ffn_conv_b", small_sum[6], ffn_conv_b, m_ffn_conv_b, v_ffn_conv_b),
             ("pool_scale", g_pool_scale, pool_scale, m_pool_scale, v_pool_scale),
             ("ffn_conv_w", g_conv_w, ffn_conv_w, m_ffn_conv_w, v_ffn_conv_w)]
    shapes = [w.shape for _, _, w, _, _ in small]
    upd = _adamw_plain(_pack([g for _, g, _, _, _ in small]), _pack([w for _, _, w, _, _ in small]),
                       _pack([m for _, _, _, m, _ in small]), _pack([v for _, _, _, _, v in small]), "adamw_small")
    upd = [_unpack(a, shapes) for a in upd]
    res = dict(big)
    for i, (nm, g, _, _, _) in enumerate(small):
        res[nm] = (g, upd[0][i], upd[1][i], upd[2][i])

    order = ["mix_pre_g", "mix_post_g", "pool_w", "pool_scale", "kv_norm_g", "w_kv", "w_q", "w_o", "sinks", "ffn_pre_g",
             "ffn_post_g", "ffn_w_in", "ffn_conv_w", "ffn_conv_b", "ffn_w_out"]
    return (loss, grad_x, *[res[nm][0] for nm in order], *[res[nm][1] for nm in order],
            *[res[nm][2] for nm in order], *[res[nm][3] for nm in order])
```

```python
import functools
import math

import jax
import jax.numpy as jnp
from jax import lax
from jax.experimental import pallas as pl
from jax.experimental.pallas import tpu as pltpu

F32 = jnp.float32
_MXU_DTYPE = jnp.bfloat16
_ACT_DTYPE = jnp.bfloat16
_WIRE_DTYPE = jnp.bfloat16
_SAVE_DTYPE = jnp.bfloat16

N_DEV = 8
POOL_WINDOWS = (2, 4, 8, 16)
POOL_HALO = 16
HEAD_DIM = 64
N_KV_HEADS = 4
WINDOW = 128
BLOCK = 128
LANES = 128
ROPE_THETA = 10000.0
ATTN_SCALE = 1.0 / math.sqrt(HEAD_DIM)
NEG_INF = -1e30
RMS_EPS = 1e-6
CONV_HALO = 8
SAVE_HALO = 16
ADAM_LR = 0.001
ADAM_B1 = 0.9
ADAM_B2 = 0.999
ADAM_EPS = 1e-08
ADAM_WD = 0.01
ADAM_STEP = 10
VMEM_LIMIT = 56 * 1024 * 1024
MESH = pl.DeviceIdType.MESH
AXES = ("x", "y", "c")


def _params(n_axes=1, vmem=VMEM_LIMIT):
    return pltpu.CompilerParams(dimension_semantics=("arbitrary",) * n_axes, vmem_limit_bytes=vmem)


def _resident(shape, index):
    return pl.BlockSpec(shape, lambda *_: index, pipeline_mode=pl.Buffered(1))


def _const(shape, index=None):
    index = (0,) * len(shape) if index is None else index
    return pl.BlockSpec(shape, lambda *_: index)


def _rows(tm, cols):
    return pl.BlockSpec((tm, cols), lambda i: (i, 0))


def _row_tile(t):
    for tm in (256, 128, 64, 32, 16, 8):
        if t % tm == 0:
            return tm
    raise ValueError(f"sequence length {t} is not a multiple of 8")


def _mm(a, b):
    return jnp.dot(a.astype(_MXU_DTYPE), b.astype(_MXU_DTYPE), preferred_element_type=F32)


def _mm_tb(a, b):
    return lax.dot_general(a.astype(_MXU_DTYPE), b.astype(_MXU_DTYPE), (((1,), (1,)), ((), ())),
                           preferred_element_type=F32)


def _mm_ta(a, b):
    return lax.dot_general(a.astype(_MXU_DTYPE), b.astype(_MXU_DTYPE), (((0,), (0,)), ((), ())),
                           preferred_element_type=F32)


def _rms_r(x):
    return lax.rsqrt(jnp.mean(x * x, axis=-1, keepdims=True) + RMS_EPS)


def _rms_fwd(x, g):
    return (x * _rms_r(x)) * g


def _rms_bwd(x, g, dy):
    r = _rms_r(x)
    xh = x * r
    dg = jnp.sum(dy * xh, axis=0, keepdims=True)
    dxh = dy * g
    dx = r * (dxh - xh * jnp.mean(dxh * xh, axis=-1, keepdims=True))
    return dx, dg


_GELU_C = math.sqrt(2.0 / math.pi)


def _gelu_parts(z):
    z2 = z * z
    t = jnp.tanh(_GELU_C * (z + 0.044715 * (z2 * z)))
    cdf = 0.5 * (1.0 + t)
    dz = cdf + z * (0.5 * (1.0 - t * t)) * (_GELU_C * (1.0 + (3 * 0.044715) * z2))
    return cdf, dz


def _lane_iota(shape):
    return lax.broadcasted_iota(jnp.int32, shape, len(shape) - 1)


def _rope_partner(xb):
    first = (_lane_iota(xb.shape) & 32) == 0
    return jnp.where(first, pltpu.roll(xb, LANES - 32, 1), pltpu.roll(xb, 32, 1))


def _rope_fwd(xb, cos, ssin):
    return xb * cos + _rope_partner(xb) * ssin


def _rope_bwd(dyb, cos, ssin):
    return dyb * cos - _rope_partner(dyb) * ssin


def _low_half(shape):
    return (_lane_iota(shape) & 64) == 0


def _pool_fwd(x, pre_g, post_g, w, scale, layer, name):
    t, d_model = x.shape
    tm = _row_tile(t)
    n_groups, gc = w.shape[1], w.shape[2]

    def body(x_ref, pre_ref, post_ref, w_ref, sc_ref, xo_ref, d_ref, yu_ref, hbuf):
        i = pl.program_id(0)

        @pl.when(i == 0)
        def _():
            hbuf[pl.ds(0, POOL_HALO), :] = jnp.zeros((POOL_HALO, d_model), F32)

        xv = x_ref[...]
        hbuf[pl.ds(POOL_HALO, tm), :] = _rms_fwd(xv, pre_ref[...])
        tok = i * tm + lax.broadcasted_iota(jnp.int32, (tm, 1), 0)
        yus = []
        for gi, wnd in enumerate(POOL_WINDOWS):
            cols = pl.ds(gi * gc, gc)
            h = hbuf[pl.ds(POOL_HALO, tm), cols]
            acc = h
            for k in range(1, wnd):
                acc = acc + hbuf[pl.ds(POOL_HALO - k, tm), cols]
            cnt = jnp.minimum(tok + 1, wnd).astype(F32)
            dg = acc / cnt - h
            d_ref[:, cols] = dg.astype(d_ref.dtype)
            yus.append(_mm(dg, w_ref[gi]))
        hbuf[pl.ds(0, POOL_HALO), :] = hbuf[pl.ds(tm, POOL_HALO), :]
        yu = jnp.concatenate(yus, axis=1)
        yu_ref[...] = yu
        xo_ref[...] = xv + _rms_fwd(yu * sc_ref[...], post_ref[...])

    return pl.pallas_call(
        body, name=name, grid=(t // tm,),
        out_shape=(jax.ShapeDtypeStruct((t, d_model), F32), jax.ShapeDtypeStruct((t, d_model), _ACT_DTYPE),
                   jax.ShapeDtypeStruct((t, d_model), F32)),
        in_specs=[_rows(tm, d_model), _const((None, 1, d_model), (layer, 0, 0)), _const((None, 1, d_model), (layer, 0, 0)),
                  _const((None, n_groups, gc, gc), (layer, 0, 0, 0)), _const((None, 1, d_model), (layer, 0, 0))],
        out_specs=(_rows(tm, d_model), _rows(tm, d_model), _rows(tm, d_model)),
        scratch_shapes=[pltpu.VMEM((POOL_HALO + tm, d_model), F32)],
        compiler_params=_params(),
    )(x, pre_g, post_g, w, scale)


def _pool_bwd(dx, x, pre_g, post_g, d, yu, w, scale, layer, name):
    t, d_model = x.shape
    tm = _row_tile(t)
    nt = t // tm
    n_groups, gc = w.shape[1], w.shape[2]
    rev = lambda i: (nt - 1 - i, 0)
    rows = pl.BlockSpec((tm, d_model), rev)

    def body(dx_ref, x_ref, pre_ref, post_ref, d_ref, yu_ref, w_ref, sc_ref,
             dxi_ref, dyu_ref, dsc_ref, dpre_ref, dpost_ref, zbuf):
        i = pl.program_id(0)

        @pl.when(i == 0)
        def _():
            zbuf[pl.ds(tm, POOL_HALO), :] = jnp.zeros((POOL_HALO, d_model), F32)
            dsc_ref[...] = jnp.zeros_like(dsc_ref)
            dpre_ref[...] = jnp.zeros_like(dpre_ref)
            dpost_ref[...] = jnp.zeros_like(dpost_ref)

        dxo = dx_ref[...]
        yuv = yu_ref[...]
        sc = sc_ref[...]
        dm, dpost = _rms_bwd(yuv * sc, post_ref[...], dxo)
        dpost_ref[...] += dpost
        dsc_ref[...] += jnp.sum(dm * yuv, axis=0, keepdims=True)
        dyu = dm * sc
        dyu_ref[...] = dyu.astype(dyu_ref.dtype)
        tok = (nt - 1 - i) * tm + lax.broadcasted_iota(jnp.int32, (tm, 1), 0)
        dds = []
        for gi, wnd in enumerate(POOL_WINDOWS):
            cols = pl.ds(gi * gc, gc)
            dd = _mm_tb(dyu[:, gi * gc:(gi + 1) * gc], w_ref[gi])
            cnt = jnp.minimum(tok + 1, wnd).astype(F32)
            zbuf[pl.ds(0, tm), cols] = dd / cnt
            dds.append(dd)
        dhs = []
        for gi, wnd in enumerate(POOL_WINDOWS):
            cols = pl.ds(gi * gc, gc)
            acc = zbuf[pl.ds(0, tm), cols]
            for k in range(1, wnd):
                acc = acc + zbuf[pl.ds(k, tm), cols]
            dhs.append(acc - dds[gi])
        zbuf[pl.ds(tm, POOL_HALO), :] = zbuf[pl.ds(0, POOL_HALO), :]
        dh = jnp.concatenate(dhs, axis=1)
        dxp, dpre = _rms_bwd(x_ref[...], pre_ref[...], dh)
        dpre_ref[...] += dpre
        dxi_ref[...] = dxo + dxp

    vec = jax.ShapeDtypeStruct((1, d_model), F32)
    return pl.pallas_call(
        body, name=name, grid=(nt,),
        out_shape=(jax.ShapeDtypeStruct((t, d_model), F32), jax.ShapeDtypeStruct((t, d_model), _ACT_DTYPE), vec, vec, vec),
        in_specs=[rows, rows, _const((None, 1, d_model), (layer, 0, 0)), _const((None, 1, d_model), (layer, 0, 0)), rows, rows,
                  _const((None, n_groups, gc, gc), (layer, 0, 0, 0)), _const((None, 1, d_model), (layer, 0, 0))],
        out_specs=(rows, rows, _const((1, d_model)), _const((1, d_model)), _const((1, d_model))),
        scratch_shapes=[pltpu.VMEM((tm + POOL_HALO, d_model), F32)],
        compiler_params=_params(),
    )(dx, x, pre_g, post_g, d, yu, w, scale)


def _conv_taps(cw_ref, s):
    return [cw_ref[k, pl.ds(s, 1), :] for k in range(3)]


def _shift_down(v, k, before):
    rolled = pltpu.roll(v, k, 0)
    row = lax.broadcasted_iota(jnp.int32, before.shape, 0)
    head = jnp.where(row < k, pltpu.roll(before, k, 0), rolled[:CONV_HALO])
    return jnp.concatenate([head, rolled[CONV_HALO:]], axis=0)


def _shift_up(v, k, after):
    rows = v.shape[0]
    rolled = pltpu.roll(v, rows - k, 0)
    row = lax.broadcasted_iota(jnp.int32, after.shape, 0)
    tail = jnp.where(row >= CONV_HALO - k, pltpu.roll(after, CONV_HALO - k, 0), rolled[rows - CONV_HALO:])
    return jnp.concatenate([rolled[:rows - CONV_HALO], tail], axis=0)


def _ffn_fwd(x, pre_g, post_g, w_in, conv_w, conv_b, w_out, layer, w_layer, name):
    t, d_model = x.shape
    tm = _row_tile(t)
    fs = w_in.shape[3]
    half = N_DEV // 2

    def body(x_ref, pre_ref, post_ref, win_ref, cw_ref, cb_ref, wout_ref, xo_ref, u_ref, uc_ref, f_ref, carry):
        i = pl.program_id(0)

        @pl.when(i == 0)
        def _():
            carry[...] = jnp.zeros_like(carry)

        xv = x_ref[...]
        hf = _rms_fwd(xv, pre_ref[...]).astype(_MXU_DTYPE)
        f = jnp.zeros((tm, d_model), F32)
        for b in range(half):
            ucs = []
            for s in (b, b + half):
                u = jnp.dot(hf, win_ref[s], preferred_element_type=F32)
                u_ref[s] = u.astype(u_ref.dtype)
                before = carry[s]
                carry[s] = u[tm - CONV_HALO:]
                w0, w1, w2 = _conv_taps(cw_ref, s)
                uc = ((w0 * _shift_down(u, 2, before) + w1 * _shift_down(u, 1, before)) + w2 * u) + cb_ref[pl.ds(s, 1), :]
                uc_ref[s] = uc.astype(uc_ref.dtype)
                ucs.append(uc)
            gate, val = ucs
            cdf, _ = _gelu_parts(gate)
            f = f + _mm((gate * cdf) * val, wout_ref[pl.ds(b * fs, fs), :])
        f_ref[...] = f
        xo_ref[...] = xv + _rms_fwd(f, post_ref[...])

    tile3 = pl.BlockSpec((N_DEV, tm, fs), lambda i: (0, i, 0))
    saved = jax.ShapeDtypeStruct((N_DEV, t, fs), _SAVE_DTYPE)
    return pl.pallas_call(
        body, name=name, grid=(t // tm,),
        out_shape=(jax.ShapeDtypeStruct((t, d_model), F32), saved, saved, jax.ShapeDtypeStruct((t, d_model), F32)),
        in_specs=[_rows(tm, d_model), _const((None, 1, d_model), (layer, 0, 0)), _const((None, 1, d_model), (layer, 0, 0)),
                  _resident((None, N_DEV, d_model, fs), (w_layer, 0, 0, 0)), _const((None, 3, N_DEV, fs), (layer, 0, 0, 0)),
                  _const((None, N_DEV, fs), (layer, 0, 0)), _resident((None, half * fs, d_model), (w_layer, 0, 0))],
        out_specs=(_rows(tm, d_model), tile3, tile3, _rows(tm, d_model)),
        scratch_shapes=[pltpu.VMEM((N_DEV, CONV_HALO, fs), F32)],
        compiler_params=_params(),
    )(x, pre_g, post_g, w_in, conv_w, conv_b, w_out)


def _ffn_bwd_out(dx, f, post_g, uc, conv_b, w_out, layer, w_layer, name):
    t, d_model = dx.shape
    tm = _row_tile(t)
    fs = uc.shape[2]
    half = N_DEV // 2

    def body(dx_ref, f_ref, post_ref, uc_ref, wout_ref, duc_ref, g_ref, df_ref, dcb_ref, dpost_ref):
        i = pl.program_id(0)

        @pl.when(i == 0)
        def _():
            dcb_ref[...] = jnp.zeros_like(dcb_ref)
            dpost_ref[...] = jnp.zeros_like(dpost_ref)

        df, dpost = _rms_bwd(f_ref[...], post_ref[...], dx_ref[...])
        dpost_ref[...] += dpost
        dfm = df.astype(_MXU_DTYPE)
        df_ref[...] = dfm
        for b in range(half):
            gate = uc_ref[b].astype(F32)
            val = uc_ref[b + half].astype(F32)
            cdf, dgelu = _gelu_parts(gate)
            ge = gate * cdf
            g_ref[b] = (ge * val).astype(g_ref.dtype)
            dg = _mm_tb(dfm, wout_ref[pl.ds(b * fs, fs), :])
            for s, dd in ((b, dg * val * dgelu), (b + half, dg * ge)):
                duc_ref[s] = dd.astype(duc_ref.dtype)
                dcb_ref[pl.ds(s, 1), :] += jnp.sum(dd, axis=0, keepdims=True)

    tile3 = pl.BlockSpec((N_DEV, tm, fs), lambda i: (0, i, 0))
    return pl.pallas_call(
        body, name=name, grid=(t // tm,),
        out_shape=(jax.ShapeDtypeStruct((N_DEV, t, fs), _SAVE_DTYPE), jax.ShapeDtypeStruct((half, t, fs), _ACT_DTYPE),
                   jax.ShapeDtypeStruct((t, d_model), _ACT_DTYPE), jax.ShapeDtypeStruct((N_DEV, fs), F32),
                   jax.ShapeDtypeStruct((1, d_model), F32)),
        in_specs=[_rows(tm, d_model), _rows(tm, d_model), _const((None, 1, d_model), (layer, 0, 0)), tile3,
                  _resident((None, half * fs, d_model), (w_layer, 0, 0))],
        out_specs=(tile3, pl.BlockSpec((half, tm, fs), lambda i: (0, i, 0)), _rows(tm, d_model),
                   _const((N_DEV, fs)), _const((1, d_model))),
        compiler_params=_params(),
    )(dx, f, post_g, uc, w_out)


def _ffn_bwd_in(dx, x, pre_g, duc, u, conv_w, w_in, layer, w_layer, name):
    t, d_model = dx.shape
    tm = _row_tile(t)
    nt = t // tm
    fs = duc.shape[2]
    hb = SAVE_HALO
    per_tile = tm // hb

    def body(dx_ref, x_ref, pre_ref, duc_ref, dn_ref, u_ref, cw_ref, win_ref, dxi_ref, du_ref, hf_ref, dcw_ref, dpre_ref):
        i = pl.program_id(0)

        @pl.when(i == 0)
        def _():
            dcw_ref[...] = jnp.zeros_like(dcw_ref)
            dpre_ref[...] = jnp.zeros_like(dpre_ref)

        xv = x_ref[...]
        pre = pre_ref[...]
        hf_ref[...] = _rms_fwd(xv, pre).astype(hf_ref.dtype)
        dhf = jnp.zeros((tm, d_model), F32)
        for s in range(N_DEV):
            d0 = duc_ref[s].astype(F32)
            after = jnp.where(i == nt - 1, 0.0, dn_ref[s].astype(F32)[:CONV_HALO])
            d1 = _shift_up(d0, 1, after)
            d2 = _shift_up(d0, 2, after)
            uv = u_ref[s].astype(F32)
            for k, dk in ((2, d0), (1, d1), (0, d2)):
                dcw_ref[k, pl.ds(s, 1), :] += jnp.sum(dk * uv, axis=0, keepdims=True)
            w0, w1, w2 = _conv_taps(cw_ref, s)
            du = (w2 * d0 + w1 * d1 + w0 * d2).astype(_MXU_DTYPE)
            du_ref[s] = du
            dhf = dhf + _mm_tb(du, win_ref[s])
        dxp, dpre = _rms_bwd(xv, pre, dhf)
        dpre_ref[...] += dpre
        dxi_ref[...] = dx_ref[...] + dxp

    tile3 = pl.BlockSpec((N_DEV, tm, fs), lambda i: (0, i, 0))
    return pl.pallas_call(
        body, name=name, grid=(nt,),
        out_shape=(jax.ShapeDtypeStruct((t, d_model), F32), jax.ShapeDtypeStruct((N_DEV, t, fs), _ACT_DTYPE),
                   jax.ShapeDtypeStruct((t, d_model), _ACT_DTYPE), jax.ShapeDtypeStruct((3, N_DEV, fs), F32),
                   jax.ShapeDtypeStruct((1, d_model), F32)),
        in_specs=[_rows(tm, d_model), _rows(tm, d_model), _const((None, 1, d_model), (layer, 0, 0)), tile3,
                  pl.BlockSpec((N_DEV, hb, fs), lambda i: (0, jnp.minimum((i + 1) * per_tile, t // hb - 1), 0)), tile3,
                  _const((None, 3, N_DEV, fs), (layer, 0, 0, 0)), _resident((None, N_DEV, d_model, fs), (w_layer, 0, 0, 0))],
        out_specs=(_rows(tm, d_model), tile3, _rows(tm, d_model), _const((3, N_DEV, fs)), _const((1, d_model))),
        compiler_params=_params(),
    )(dx, x, pre_g, duc, duc, u, conv_w, w_in)


def _tn_matmul(a, b, a_spec, b_spec, n_out, m, n, name, out_dtype):
    def body(a_ref, b_ref, o_ref):
        o_ref[...] = _mm_ta(a_ref[...], b_ref[...]).astype(o_ref.dtype)

    return pl.pallas_call(
        body, name=name, grid=(n_out,),
        out_shape=jax.ShapeDtypeStruct((n_out, m, n), out_dtype),
        in_specs=[a_spec, b_spec],
        out_specs=pl.BlockSpec((None, m, n), lambda c: (c, 0, 0)),
        compiler_params=_params(),
    )(a, b)


def _kv_fwd(x, kv_g, w_kv, cos, ssin, name):
    t, d_model = x.shape
    tm = _row_tile(t)
    kvd = w_kv.shape[1] // 2
    pairs = kvd // LANES

    def body(x_ref, g_ref, w_ref, cos_ref, sin_ref, k_ref, v_ref):
        kv = _mm(_rms_fwd(x_ref[...], g_ref[...]), w_ref[...])
        low = _low_half((tm, LANES))
        for j in range(pairs):
            kb = _rope_fwd(kv[:, j * LANES:(j + 1) * LANES], cos_ref[...], sin_ref[...])
            vb = kv[:, kvd + j * LANES:kvd + (j + 1) * LANES]
            for blk, ref in ((kb, k_ref), (vb, v_ref)):
                sw = pltpu.roll(blk, 64, 1)
                ref[2 * j] = jnp.where(low, blk, sw).astype(ref.dtype)
                ref[2 * j + 1] = jnp.where(low, sw, blk).astype(ref.dtype)

    heads = jax.ShapeDtypeStruct((N_KV_HEADS, t, LANES), _ACT_DTYPE)
    hspec = pl.BlockSpec((N_KV_HEADS, tm, LANES), lambda i: (0, i, 0))
    return pl.pallas_call(
        body, name=name, grid=(t // tm,), out_shape=(heads, heads),
        in_specs=[_rows(tm, d_model), _const((1, d_model)), _const(w_kv.shape), _rows(tm, LANES), _rows(tm, LANES)],
        out_specs=(hspec, hspec), compiler_params=_params(),
    )(x, kv_g, w_kv, cos, ssin)


def _kv_bwd(dx, x, kv_g, w_kv, cos, ssin, dks, dvs, name):
    t, d_model = x.shape
    tm = _row_tile(t)
    kvd = w_kv.shape[1] // 2
    pairs = kvd // LANES
    n_users = len(dks)

    def body(dx_ref, x_ref, g_ref, w_ref, cos_ref, sin_ref, *refs):
        dk_refs, dv_refs = refs[:n_users], refs[n_users:2 * n_users]
        dxi_ref, h_ref, dkv_ref, dg_ref = refs[2 * n_users:]
        dk_ref = functools.reduce(lambda a, b: a + b, [r[...] for r in dk_refs])
        dv_ref = functools.reduce(lambda a, b: a + b, [r[...] for r in dv_refs])
        i = pl.program_id(0)

        @pl.when(i == 0)
        def _():
            dg_ref[...] = jnp.zeros_like(dg_ref)

        xv = x_ref[...]
        g = g_ref[...]
        h_ref[...] = _rms_fwd(xv, g).astype(h_ref.dtype)
        low = _low_half((tm, LANES))
        dks, dvs = [], []
        for j in range(pairs):
            dkb = jnp.where(low, dk_ref[2 * j], dk_ref[2 * j + 1])
            dks.append(_rope_bwd(dkb, cos_ref[...], sin_ref[...]))
            dvs.append(jnp.where(low, dv_ref[2 * j], dv_ref[2 * j + 1]))
        dkv = jnp.concatenate(dks + dvs, axis=1).astype(_MXU_DTYPE)
        dkv_ref[...] = dkv
        dxp, dg = _rms_bwd(xv, g, _mm_tb(dkv, w_ref[...]))
        dg_ref[...] += dg
        dxi_ref[...] = dx_ref[...] + dxp

    hspec = pl.BlockSpec((N_KV_HEADS, tm, LANES), lambda i: (0, i, 0))
    return pl.pallas_call(
        body, name=name, grid=(t // tm,),
        out_shape=(jax.ShapeDtypeStruct((t, d_model), F32), jax.ShapeDtypeStruct((t, d_model), _ACT_DTYPE),
                   jax.ShapeDtypeStruct((t, 2 * kvd), _ACT_DTYPE), jax.ShapeDtypeStruct((1, d_model), F32)),
        in_specs=[_rows(tm, d_model), _rows(tm, d_model), _const((1, d_model)), _const(w_kv.shape),
                  _rows(tm, LANES), _rows(tm, LANES)] + [hspec] * (2 * n_users),
        out_specs=(_rows(tm, d_model), _rows(tm, d_model), _rows(tm, 2 * kvd), _const((1, d_model))),
        compiler_params=_params(),
    )(dx, x, kv_g, w_kv, cos, ssin, *dks, *dvs)


def _q_fwd(x, pre_g, w_q, cos, ssin, layer, j, name):
    t, d_model = x.shape
    tm = _row_tile(t)

    def body(x_ref, g_ref, w_ref, cos_ref, sin_ref, q_ref):
        q = _mm(_rms_fwd(x_ref[...], g_ref[...]), w_ref[...])
        for p in range(d_model // LANES):
            cols = slice(p * LANES, (p + 1) * LANES)
            q_ref[:, cols] = (_rope_fwd(q[:, cols], cos_ref[...], sin_ref[...]) * ATTN_SCALE).astype(q_ref.dtype)

    return pl.pallas_call(
        body, name=name, grid=(t // tm,), out_shape=jax.ShapeDtypeStruct((t, d_model), _ACT_DTYPE),
        in_specs=[_rows(tm, d_model), _const((None, 1, d_model), (layer, 0, 0)), _const((None, d_model, d_model), (j, 0, 0)),
                  _rows(tm, LANES), _rows(tm, LANES)],
        out_specs=_rows(tm, d_model), compiler_params=_params(),
    )(x, pre_g, w_q, cos, ssin)


def _q_bwd(dx, dqs, x, pre_g, w_q, cos, ssin, layer, j, name):
    t, d_model = x.shape
    tm = _row_tile(t)

    def body(dx_ref, dq_ref, x_ref, g_ref, w_ref, cos_ref, sin_ref, dxi_ref, h_ref, dqo_ref, dg_ref):
        i = pl.program_id(0)

        @pl.when(i == 0)
        def _():
            dg_ref[...] = jnp.zeros_like(dg_ref)

        xv = x_ref[...]
        g = g_ref[...]
        h_ref[...] = _rms_fwd(xv, g).astype(h_ref.dtype)
        parts = []
        for p in range(d_model // LANES):
            cols = slice(p * LANES, (p + 1) * LANES)
            parts.append(_rope_bwd(dq_ref[:, cols] * ATTN_SCALE, cos_ref[...], sin_ref[...]))
        dq = jnp.concatenate(parts, axis=1).astype(_MXU_DTYPE)
        dqo_ref[...] = dq
        dxp, dg = _rms_bwd(xv, g, _mm_tb(dq, w_ref[...]))
        dg_ref[...] += dg
        dxi_ref[...] = dx_ref[...] + dxp

    act = jax.ShapeDtypeStruct((t, d_model), _ACT_DTYPE)
    return pl.pallas_call(
        body, name=name, grid=(t // tm,),
        out_shape=(jax.ShapeDtypeStruct((t, d_model), F32), act, act, jax.ShapeDtypeStruct((1, d_model), F32)),
        in_specs=[_rows(tm, d_model), _rows(tm, d_model), _rows(tm, d_model), _const((None, 1, d_model), (layer, 0, 0)),
                  _const((None, d_model, d_model), (j, 0, 0)), _rows(tm, LANES), _rows(tm, LANES)],
        out_specs=(_rows(tm, d_model), _rows(tm, d_model), _rows(tm, d_model), _const((1, d_model))),
        compiler_params=_params(),
    )(dx, dqs, x, pre_g, w_q, cos, ssin)


def _attn_scores(q_pair, k2, n, sink_a, sink_b):
    low = _low_half(q_pair.shape)
    zero = jnp.zeros_like(q_pair)
    qst = jnp.concatenate([jnp.where(low, q_pair, zero), jnp.where(low, zero, q_pair)], axis=0)
    s = _mm_tb(qst, k2)
    row = lax.broadcasted_iota(jnp.int32, s.shape, 0)
    col = lax.broadcasted_iota(jnp.int32, s.shape, 1)
    rel = BLOCK + (row & (BLOCK - 1)) - col
    valid = (rel >= 0) & (rel < WINDOW) & (n * BLOCK + col - BLOCK >= 0)
    s = jnp.where(valid, s, NEG_INF)
    rows1 = lax.broadcasted_iota(jnp.int32, (2 * BLOCK, 1), 0)
    sink = jnp.where(rows1 < BLOCK, sink_a, sink_b)
    return qst, s, sink


def _attn_fwd(qs, kdup, vdup, sinks, j, name):
    t, d_model = qs.shape
    nb = t // BLOCK
    n_pairs = d_model // LANES
    per_group = n_pairs // N_KV_HEADS

    def body(sink_ref, q_ref, kp_ref, ko_ref, vp_ref, vo_ref, o_ref, lse_ref):
        n = pl.program_id(0)
        low = _low_half((BLOCK, LANES))
        lane = _lane_iota((BLOCK, LANES))
        lse = jnp.zeros((BLOCK, LANES), F32)
        for p in range(n_pairs):
            hk = p // per_group
            k2 = jnp.concatenate([kp_ref[hk], ko_ref[hk]], axis=0)
            v2 = jnp.concatenate([vp_ref[hk], vo_ref[hk]], axis=0)
            _, s, sink = _attn_scores(q_ref[:, p * LANES:(p + 1) * LANES], k2, n, sink_ref[j, 2 * p], sink_ref[j, 2 * p + 1])
            m = jnp.maximum(jnp.max(s, axis=-1, keepdims=True), sink)
            pe = jnp.exp(s - m)
            denom = jnp.sum(pe, axis=-1, keepdims=True) + jnp.exp(sink - m)
            o2 = _mm(pe, v2) / denom
            o_ref[:, p * LANES:(p + 1) * LANES] = jnp.where(low, o2[:BLOCK], o2[BLOCK:]).astype(o_ref.dtype)
            l2 = m + jnp.log(denom)
            lse = jnp.where(lane == 2 * p, l2[:BLOCK], lse)
            lse = jnp.where(lane == 2 * p + 1, l2[BLOCK:], lse)
        lse_ref[...] = lse

    prev = pl.BlockSpec((N_KV_HEADS, BLOCK, LANES), lambda n: (0, jnp.maximum(n - 1, 0), 0))
    own = pl.BlockSpec((N_KV_HEADS, BLOCK, LANES), lambda n: (0, n, 0))
    return pl.pallas_call(
        body, name=name, grid=(nb,),
        out_shape=(jax.ShapeDtypeStruct((t, d_model), _ACT_DTYPE), jax.ShapeDtypeStruct((t, LANES), F32)),
        in_specs=[pl.BlockSpec(memory_space=pltpu.SMEM), _rows(BLOCK, d_model), prev, own, prev, own],
        out_specs=(_rows(BLOCK, d_model), _rows(BLOCK, LANES)), compiler_params=_params(),
    )(sinks, qs, kdup, kdup, vdup, vdup)


def _attn_bwd(qs, kdup, vdup, sinks, lse, do, j, name):
    t, d_model = qs.shape
    nb = t // BLOCK
    n_pairs = d_model // LANES
    per_group = n_pairs // N_KV_HEADS
    rev = lambda n: nb - 1 - n

    def body(sink_ref, q_ref, kp_ref, ko_ref, vp_ref, vo_ref, lse_ref, do_ref, dq_ref, dk_ref, dv_ref, ds_ref, ck, cv):
        i = pl.program_id(0)
        n = nb - 1 - i

        @pl.when(i == 0)
        def _():
            ck[...] = jnp.zeros_like(ck)
            cv[...] = jnp.zeros_like(cv)
            ds_ref[...] = jnp.zeros_like(ds_ref)

        low = _low_half((BLOCK, LANES))
        lane = _lane_iota((BLOCK, LANES))
        lane1 = _lane_iota((1, LANES))
        lsev = lse_ref[...]
        dsink = jnp.zeros((1, LANES), F32)
        dk2 = [jnp.zeros((2 * BLOCK, LANES), F32) for _ in range(N_KV_HEADS)]
        dv2 = [jnp.zeros((2 * BLOCK, LANES), F32) for _ in range(N_KV_HEADS)]
        for p in range(n_pairs):
            hk = p // per_group
            k2 = jnp.concatenate([kp_ref[hk], ko_ref[hk]], axis=0)
            v2 = jnp.concatenate([vp_ref[hk], vo_ref[hk]], axis=0)
            qst, s, sink = _attn_scores(q_ref[:, p * LANES:(p + 1) * LANES], k2, n, sink_ref[j, 2 * p], sink_ref[j, 2 * p + 1])
            l2 = jnp.concatenate([jnp.sum(jnp.where(lane == 2 * p, lsev, 0.0), axis=-1, keepdims=True),
                                  jnp.sum(jnp.where(lane == 2 * p + 1, lsev, 0.0), axis=-1, keepdims=True)], axis=0)
            pn = jnp.exp(s - l2)
            dop = do_ref[:, p * LANES:(p + 1) * LANES]
            zero = jnp.zeros_like(dop)
            dost = jnp.concatenate([jnp.where(low, dop, zero), jnp.where(low, zero, dop)], axis=0)
            dp = _mm_tb(dost, v2)
            dr = jnp.sum(pn * dp, axis=-1, keepdims=True)
            dsm = (pn * (dp - dr)).astype(_MXU_DTYPE)
            dsk = -jnp.exp(sink - l2) * dr
            dsink = dsink + jnp.where(lane1 == 2 * p, jnp.sum(dsk[:BLOCK]), 0.0) + jnp.where(lane1 == 2 * p + 1, jnp.sum(dsk[BLOCK:]), 0.0)
            dq2 = _mm(dsm, k2)
            dq_ref[:, p * LANES:(p + 1) * LANES] = jnp.where(low, dq2[:BLOCK], dq2[BLOCK:])
            dk2[hk] = dk2[hk] + _mm_ta(dsm, qst)
            dv2[hk] = dv2[hk] + _mm_ta(pn, dost)
        ds_ref[...] += dsink
        for hk in range(N_KV_HEADS):
            for acc, carry, ref in ((dk2[hk], ck, dk_ref), (dv2[hk], cv, dv_ref)):
                folded = acc + pltpu.roll(acc, 64, 1)
                ref[hk] = folded[BLOCK:] + carry[hk]
                carry[hk] = folded[:BLOCK]

    prev = pl.BlockSpec((N_KV_HEADS, BLOCK, LANES), lambda n: (0, jnp.maximum(rev(n) - 1, 0), 0))
    own = pl.BlockSpec((N_KV_HEADS, BLOCK, LANES), lambda n: (0, rev(n), 0))
    rows = lambda cols: pl.BlockSpec((BLOCK, cols), lambda n: (rev(n), 0))
    heads = jax.ShapeDtypeStruct((N_KV_HEADS, t, LANES), F32)
    return pl.pallas_call(
        body, name=name, grid=(nb,),
        out_shape=(jax.ShapeDtypeStruct((t, d_model), F32), heads, heads, jax.ShapeDtypeStruct((1, LANES), F32)),
        in_specs=[pl.BlockSpec(memory_space=pltpu.SMEM), rows(d_model), prev, own, prev, own, rows(LANES), rows(d_model)],
        out_specs=(rows(d_model), own, own, _const((1, LANES))),
        scratch_shapes=[pltpu.VMEM((N_KV_HEADS, BLOCK, LANES), F32), pltpu.VMEM((N_KV_HEADS, BLOCK, LANES), F32)],
        compiler_params=_params(),
    )(sinks, qs, kdup, kdup, vdup, vdup, lse, do)


def _oproj_fwd(x, o, w_o, post_g, layer, j, name):
    t, d_model = x.shape
    tm = _row_tile(t)

    def body(x_ref, o_ref, w_ref, g_ref, xo_ref, mo_ref):
        mo = _mm(o_ref[...], w_ref[...])
        mo_ref[...] = mo
        xo_ref[...] = x_ref[...] + _rms_fwd(mo, g_ref[...])

    full = jax.ShapeDtypeStruct((t, d_model), F32)
    return pl.pallas_call(
        body, name=name, grid=(t // tm,), out_shape=(full, full),
        in_specs=[_rows(tm, d_model), _rows(tm, d_model), _const((None, d_model, d_model), (j, 0, 0)),
                  _const((None, 1, d_model), (layer, 0, 0))],
        out_specs=(_rows(tm, d_model), _rows(tm, d_model)), compiler_params=_params(),
    )(x, o, w_o, post_g)


def _oproj_bwd(dx, mo, w_o, post_g, layer, j, name):
    t, d_model = dx.shape
    tm = _row_tile(t)

    def body(dx_ref, mo_ref, w_ref, g_ref, do_ref, dmo_ref, dg_ref):
        i = pl.program_id(0)

        @pl.when(i == 0)
        def _():
            dg_ref[...] = jnp.zeros_like(dg_ref)

        dmo, dg = _rms_bwd(mo_ref[...], g_ref[...], dx_ref[...])
        dg_ref[...] += dg
        dmo = dmo.astype(_MXU_DTYPE)
        dmo_ref[...] = dmo
        do_ref[...] = _mm_tb(dmo, w_ref[...]).astype(do_ref.dtype)

    act = jax.ShapeDtypeStruct((t, d_model), _ACT_DTYPE)
    return pl.pallas_call(
        body, name=name, grid=(t // tm,), out_shape=(act, act, jax.ShapeDtypeStruct((1, d_model), F32)),
        in_specs=[_rows(tm, d_model), _rows(tm, d_model), _const((None, d_model, d_model), (j, 0, 0)),
                  _const((None, 1, d_model), (layer, 0, 0))],
        out_specs=(_rows(tm, d_model), _rows(tm, d_model), _const((1, d_model))), compiler_params=_params(),
    )(dx, mo, w_o, post_g)


def _loss_grad(y, target, name):
    t, d_model = y.shape
    tm = _row_tile(t)

    def body(y_ref, t_ref, dy_ref, loss_ref):
        i = pl.program_id(0)

        @pl.when(i == 0)
        def _():
            loss_ref[...] = jnp.zeros_like(loss_ref)

        err = y_ref[...] - t_ref[...]
        dy_ref[...] = err / d_model
        loss_ref[...] += 0.5 * jnp.sum(jnp.mean(err * err, axis=-1, keepdims=True), axis=0, keepdims=True)

    return pl.pallas_call(
        body, name=name, grid=(t // tm,),
        out_shape=(jax.ShapeDtypeStruct((t, d_model), F32), jax.ShapeDtypeStruct((1, 1), F32)),
        in_specs=[_rows(tm, d_model), _rows(tm, d_model)], out_specs=(_rows(tm, d_model), _const((1, 1))),
        compiler_params=_params(),
    )(y, target)


def _mesh_position():
    return lax.axis_index("x"), lax.axis_index("y"), lax.axis_index("c")


def _block_of(px, py, pc):
    return 4 * px + 2 * py + pc


def _at_block(ref, axis, block):
    return ref.at[(slice(None),) * axis + (block,)]


def _all_gather(shards, axes, name):
    n = len(shards)

    def body(*refs):
        srcs, outs = refs[:n], refs[n:2 * n]
        send_sems, recv_sems, local_sems = refs[2 * n:]
        x, y, c = _mesh_position()
        me, sibling = (x, y, c), (x, y, 1 - c)
        chips = [(1 - x, y), (x, 1 - y), (1 - x, 1 - y)]

        def blk(i, pos):
            return _at_block(outs[i], axes[i], _block_of(*pos))

        def copy(i, k, block, to, src=None):
            return pltpu.make_async_remote_copy(
                src_ref=blk(i, block) if src is None else src, dst_ref=blk(i, block),
                send_sem=send_sems.at[i, k], recv_sem=recv_sems.at[i, k], device_id=to, device_id_type=MESH)

        mine = [pltpu.make_async_copy(srcs[i], blk(i, me), local_sems.at[i]) for i in range(n)]
        for cp in mine:
            cp.start()
        sent = []
        for i in range(n):
            sent += [copy(i, 1 + k, me, (*chip, c), src=srcs[i]) for k, chip in enumerate(chips)]
            sent.append(copy(i, 0, me, sibling, src=srcs[i]))
        for cp in sent:
            cp.start()
        for i in range(n):
            for k, chip in enumerate(chips):
                copy(i, 1 + k, (*chip, c), me).wait_recv()
                passed = copy(i, 4 + k, (*chip, c), sibling)
                passed.start()
                sent.append(passed)
        for i in range(n):
            copy(i, 0, sibling, me).wait_recv()
            for k, chip in enumerate(chips):
                copy(i, 4 + k, (*chip, 1 - c), me).wait_recv()
        for cp in sent:
            cp.wait_send()
        for cp in mine:
            cp.wait()

    hbm = pl.BlockSpec(memory_space=pl.ANY)
    return pl.pallas_call(
        body, name=name,
        out_shape=tuple(jax.ShapeDtypeStruct(s.shape[:a] + (N_DEV,) + s.shape[a:], s.dtype) for s, a in zip(shards, axes)),
        in_specs=[hbm] * n, out_specs=(hbm,) * n,
        scratch_shapes=[pltpu.SemaphoreType.DMA((n, 7)), pltpu.SemaphoreType.DMA((n, 7)), pltpu.SemaphoreType.DMA((n,))],
    )(*shards)


GATHER, SCATTER = "gather", "scatter"
N_PEERS = N_DEV - 1


def _land_shape(kind, s, axis):
    if kind == GATHER:
        return s.shape[:axis] + (N_DEV,) + s.shape[axis:]
    return (N_DEV,) + s.shape[:axis] + s.shape[axis + 1:]


def _direct_plan(kind, srcs, lands, axes):
    x, y, c = _mesh_position()
    my_block = _block_of(x, y, c)
    peers = []
    for k in range(1, N_DEV):
        fx, fy, fc = (k >> 2) & 1, (k >> 1) & 1, k & 1
        peers.append((1 - x if fx else x, 1 - y if fy else y, 1 - c if fc else c))
    remote = []
    for src, land, axis in zip(srcs, lands, axes):
        if kind == GATHER:
            mine = _at_block(land, axis, my_block)
            remote += [(src, mine, peer, _at_block(land, axis, _block_of(*peer))) for peer in peers]
        else:
            mine = land.at[my_block]
            remote += [(_at_block(src, axis, _block_of(*peer)), mine, peer, land.at[_block_of(*peer)]) for peer in peers]
    return remote


def _remote(src, dst, send_sems, recv_sems, k, peer):
    return pltpu.make_async_remote_copy(src_ref=src, dst_ref=dst, send_sem=send_sems.at[k], recv_sem=recv_sems.at[k],
                                        device_id=peer, device_id_type=MESH)


_HBM = pl.BlockSpec(memory_space=pltpu.HBM)
_SEM = pl.BlockSpec(memory_space=pltpu.SEMAPHORE)
_SPLIT = dict(has_side_effects=pltpu.SideEffectType.DATAFLOW_SIDE_EFFECTING)


def _landing_zone(kind, s, axis, me):
    land = lax.empty(_land_shape(kind, s, axis), s.dtype)
    if kind == GATHER:
        return lax.dynamic_update_slice_in_dim(land, jnp.expand_dims(s, axis), me, axis)
    return lax.dynamic_update_slice_in_dim(land, lax.dynamic_slice_in_dim(s, me, 1, axis).reshape((1,) + land.shape[1:]), me, 0)


def _exchange_start(kind, arrays, axes, after, name):
    n = len(arrays)
    me = _block_of(*_mesh_position())
    lands = [_landing_zone(kind, s, a, me) for s, a in zip(arrays, axes)]

    def body(*refs):
        srcs, land_refs = refs[:n], refs[n:2 * n]
        send_sems, recv_sems = refs[2 * n + 1], refs[2 * n + 2]
        token = refs[-1]
        for k, (src, dst, peer, _) in enumerate(_direct_plan(kind, srcs, land_refs, axes)):
            _remote(src, dst, send_sems, recv_sems, k, peer).start()
        token[...] = jnp.zeros_like(token)

    both = list(arrays) + lands
    out = pl.pallas_call(
        body, name=name,
        out_shape=(pltpu.SemaphoreType.DMA((n * N_PEERS,)), pltpu.SemaphoreType.DMA((n * N_PEERS,)),
                   *[pltpu.HBM(a.shape, a.dtype) for a in both], jax.ShapeDtypeStruct((8, LANES), F32)),
        in_specs=[_HBM] * (2 * n) + [pl.BlockSpec(memory_space=pl.ANY)],
        out_specs=(_SEM, _SEM, *[_HBM] * (2 * n), pl.BlockSpec(memory_space=pltpu.VMEM)),
        input_output_aliases={i: 2 + i for i in range(2 * n)},
        compiler_params=pltpu.CompilerParams(**_SPLIT),
    )(*[pltpu.with_memory_space_constraint(a, pltpu.HBM) for a in both], after)
    return (kind, axes, n, out[:-1]), out[-1]


def _exchange_wait(handle, after, name):
    kind, axes, n, (send_sems, recv_sems, *thru) = handle

    def body(*refs):
        srcs, land_refs = refs[:n], refs[n:2 * n]
        send_sems, recv_sems = refs[2 * n], refs[2 * n + 1]
        for k, (src, _, peer, arrives) in enumerate(_direct_plan(kind, srcs, land_refs, axes)):
            cp = _remote(src, arrives, send_sems, recv_sems, k, peer)
            cp.wait_send()
            cp.wait_recv()

    out = pl.pallas_call(
        body, name=name,
        out_shape=tuple(pltpu.HBM(a.shape, a.dtype) for a in thru),
        in_specs=[_HBM] * (2 * n) + [_SEM, _SEM, pl.BlockSpec(memory_space=pl.ANY)], out_specs=(_HBM,) * (2 * n),
        input_output_aliases={i: i for i in range(2 * n)},
        compiler_params=pltpu.CompilerParams(**_SPLIT),
    )(*thru, send_sems, recv_sems, after)
    return out[n:]


def _adamw_math(w, g, m, v):
    m = ADAM_B1 * m + (1.0 - ADAM_B1) * g
    v = ADAM_B2 * v + (1.0 - ADAM_B2) * jnp.square(g)
    m_hat = m / (1.0 - ADAM_B1 ** ADAM_STEP)
    v_hat = v / (1.0 - ADAM_B2 ** ADAM_STEP)
    delta = -ADAM_LR * (m_hat / (jnp.sqrt(v_hat) + ADAM_EPS) + ADAM_WD * w)
    return delta, m, v


def _update_tile(rows):
    if rows <= 512:
        return rows
    for tr in (512, 384, 352, 256, 176, 128, 64, 32, 16):
        if rows % tr == 0:
            return tr
    raise ValueError(f"{rows} rows do not tile")


def _adamw(parts, w, m, v, slab, so_far, name):
    rows, c = w.shape
    r = parts.shape[1]
    tr = _update_tile(r)
    first = slab * (r // tr)
    if so_far is None:
        so_far = tuple(lax.empty((rows, c), F32) for _ in range(4))

    def body(p_ref, w_ref, m_ref, v_ref, *refs):
        g_ref, d_ref, mo_ref, vo_ref = refs[4:]
        g = p_ref[0].astype(F32)
        for s in range(1, N_DEV):
            g = g + p_ref[s].astype(F32)
        g_ref[...] = g
        d_ref[...], mo_ref[...], vo_ref[...] = _adamw_math(w_ref[...], g, m_ref[...], v_ref[...])

    out = jax.ShapeDtypeStruct((rows, c), F32)
    tile = pl.BlockSpec((tr, c), lambda i: (first + i, 0))
    return pl.pallas_call(
        body, name=name, grid=(r // tr,), out_shape=(out,) * 4,
        in_specs=[pl.BlockSpec((N_DEV, tr, c), lambda i: (0, i, 0))] + [tile] * 3 + [pl.BlockSpec(memory_space=pl.ANY)] * 4,
        out_specs=(tile,) * 4, input_output_aliases={4 + k: k for k in range(4)}, compiler_params=_params(),
    )(parts, w, m, v, *so_far)


def _adamw_small(parts, picks, weights, name):
    n = len(parts)

    def body(*refs):
        p_refs, wmv, outs = refs[:n], refs[n:4 * n], refs[4 * n:]
        me = _block_of(*_mesh_position())
        for i in range(n):
            g = picks[i](p_refs[i], 0, me)
            for s in range(1, N_DEV):
                g = g + picks[i](p_refs[i], s, me)
            w_ref, m_ref, v_ref = wmv[3 * i:3 * i + 3]
            g_ref, d_ref, mo_ref, vo_ref = outs[4 * i:4 * i + 4]
            g_ref[...] = g
            d_ref[...], mo_ref[...], vo_ref[...] = _adamw_math(w_ref[...], g, m_ref[...], v_ref[...])

    flat = [a for wmv in weights for a in wmv]
    out = pl.pallas_call(
        body, name=name,
        out_shape=tuple(jax.ShapeDtypeStruct(w.shape, F32) for w, _, _ in weights for _ in range(4)),
        compiler_params=pltpu.CompilerParams(vmem_limit_bytes=VMEM_LIMIT),
    )(*parts, *flat)
    return [tuple(out[4 * i:4 * i + 4]) for i in range(n)]


def kernel(x, positions, mix_pre_g, mix_post_g, pool_w, pool_scale, kv_norm_g, w_kv, w_q, w_o, sinks, ffn_pre_g, ffn_post_g, ffn_w_in, ffn_conv_w, ffn_conv_b, ffn_w_out, loss_target, m_mix_pre_g, m_mix_post_g, m_pool_w, m_pool_scale, m_kv_norm_g, m_w_kv, m_w_q, m_w_o, m_sinks, m_ffn_pre_g, m_ffn_post_g, m_ffn_w_in, m_ffn_conv_w, m_ffn_conv_b, m_ffn_w_out, v_mix_pre_g, v_mix_post_g, v_pool_w, v_pool_scale, v_kv_norm_g, v_w_kv, v_w_q, v_w_o, v_sinks, v_ffn_pre_g, v_ffn_post_g, v_ffn_w_in, v_ffn_conv_w, v_ffn_conv_b, v_ffn_w_out):
    depth, d_model = mix_pre_g.shape
    n_a = pool_w.shape[0]
    n_b = w_q.shape[0]
    t = x.shape[1]
    fs = ffn_w_in.shape[2]
    half = N_DEV // 2
    n_heads = d_model // HEAD_DIM
    x0 = x.reshape(t, d_model)
    target = loss_target.reshape(t, d_model)

    inv_freq = 1.0 / (ROPE_THETA ** (jnp.arange(0, HEAD_DIM, 2, dtype=F32) / HEAD_DIM))
    ang = positions.reshape(t).astype(F32)[:, None] * inv_freq
    cos, sin = jnp.cos(ang), jnp.sin(ang)
    cos = jnp.tile(cos, (1, 2 * LANES // HEAD_DIM))
    ssin = jnp.tile(jnp.concatenate([-sin, sin], axis=1), (1, LANES // HEAD_DIM))

    wire = lambda a: a.astype(_WIRE_DTYPE)
    w_in_b, w_out_b = wire(ffn_w_in), wire(ffn_w_out)
    pool_w_g, w_in_0, w_out_0, pool_scale_g, conv_w_g = _all_gather(
        [wire(pool_w), w_in_b[:1], w_out_b[:1], pool_scale, ffn_conv_w], [2, 1, 1, 0, 0], "gather_first")
    pending, tokens, after = {}, [], w_in_0
    for l in range(1, depth):
        if l == n_a:
            pending["attn"], after = _exchange_start(GATHER, [wire(w_kv), wire(w_q), wire(w_o)], [0, 1, 1], after, "gather_start_attn")
            tokens.append(after)
        pending[l], after = _exchange_start(GATHER, [w_in_b[l:l + 1], w_out_b[l:l + 1]], [1, 1], after, f"gather_start_{l}")
        tokens.append(after)
    started = functools.reduce(lambda a, b: a + b, [tk[0, 0] for tk in tokens])
    w_in_l, w_out_l = {0: w_in_0}, {0: w_out_0.reshape(1, half * fs, d_model)}
    pool_scale_f = pool_scale_g.transpose(1, 0, 2).reshape(n_a, 1, d_model)
    conv_w_f = conv_w_g.transpose(1, 2, 0, 3)
    pool_w_f = pool_w_g.reshape(n_a, len(POOL_WINDOWS), d_model // len(POOL_WINDOWS), -1)
    conv_b_f = ffn_conv_b.reshape(depth, N_DEV, fs)
    g3 = lambda a: a.reshape(a.shape[0], 1, a.shape[1])
    mix_pre, mix_post, ffn_pre, ffn_post = g3(mix_pre_g) + started, g3(mix_post_g), g3(ffn_pre_g), g3(ffn_post_g)
    kv_g = kv_norm_g.reshape(1, d_model)
    w_kv_f = w_q_f = w_o_f = None

    saved = []
    xc = x0
    kdup = vdup = x_kv = None
    for l in range(depth):
        x_in = xc
        if l < n_a:
            x_mid, dsave, yu = _pool_fwd(x_in, mix_pre, mix_post, pool_w_f, pool_scale_f, l, f"pool_fwd_{l}")
            mixer = (dsave, yu)
        else:
            j = l - n_a
            if j == 0:
                x_kv = x_in
                w_kv_g, w_q_g, w_o_g = _exchange_wait(pending["attn"], x_in, "gather_wait_attn")
                w_kv_f = w_kv_g.reshape(d_model, -1)
                w_q_f = w_q_g.reshape(n_b, d_model, d_model)
                w_o_f = w_o_g.reshape(n_b, d_model, d_model)
                kdup, vdup = _kv_fwd(x_kv, kv_g, w_kv_f, cos, ssin, "kv_fwd")
            qs = _q_fwd(x_in, mix_pre, w_q_f, cos, ssin, l, j, f"q_fwd_{l}")
            o, lse = _attn_fwd(qs, kdup, vdup, sinks, j, f"attn_fwd_{l}")
            x_mid, mo = _oproj_fwd(x_in, o, w_o_f, mix_post, l, j, f"oproj_fwd_{l}")
            mixer = (qs, o, lse, mo)
        if l > 0:
            w_in_l[l], w_out_g = _exchange_wait(pending[l], x_mid, f"gather_wait_{l}")
            w_out_l[l] = w_out_g.reshape(1, half * fs, d_model)
        xc, u, uc, f = _ffn_fwd(x_mid, ffn_pre, ffn_post, w_in_l[l], conv_w_f, conv_b_f, w_out_l[l], l, 0, f"ffn_fwd_{l}")
        saved.append((x_in, x_mid, u, uc, f, mixer))

    dx, loss_part = _loss_grad(xc, target, "loss")
    loss = lax.psum(loss_part[0, 0], AXES)

    gconv_w, gconv_b = [None] * depth, [None] * depth
    gmix_pre, gmix_post, gffn_pre, gffn_post = [None] * depth, [None] * depth, [None] * depth, [None] * depth
    gpool_scale, gsinks = [None] * n_a, [None] * n_b
    dks, dvs = [], []
    gkv_g = None
    whole = lambda cols: pl.BlockSpec((t, cols), lambda c: (0, 0), pipeline_mode=pl.Buffered(1))
    per_out = lambda cols: pl.BlockSpec((None, t, cols), lambda c: (c, 0, 0))
    by_dev = lambda g: g.reshape(N_DEV, -1, g.shape[-1])
    flying = []
    post = mix_post
    ffn_post_b = ffn_post
    for l in reversed(range(depth)):
        x_in, x_mid, u, uc, f, mixer = saved[l]
        duc, gact, df, gconv_b[l], gffn_post[l] = _ffn_bwd_out(dx, f, ffn_post_b, uc, conv_b_f, w_out_l[l], l, 0, f"ffn_bwd_out_{l}")
        dx, du, hf, gconv_w[l], gffn_pre[l] = _ffn_bwd_in(dx, x_mid, ffn_pre, duc, u, conv_w_f, w_in_l[l], l, 0, f"ffn_bwd_in_{l}")
        going = []
        if l < n_a:
            dsave, yu = mixer
            dx, dyu, gpool_scale[l], gmix_pre[l], gmix_post[l] = _pool_bwd(
                dx, x_in, mix_pre, post, dsave, yu, pool_w_f, pool_scale_f, l, f"pool_bwd_{l}")
            gc = d_model // len(POOL_WINDOWS)
            by_group = pl.BlockSpec((t, gc), lambda c: (0, c))
            gpool = _tn_matmul(dsave, dyu, by_group, by_group, len(POOL_WINDOWS), gc, gc, f"grad_pool_w_{l}", _WIRE_DTYPE)
            going.append(("pool_w", l, gpool.reshape(len(POOL_WINDOWS), N_DEV, -1, gc), 1))
        else:
            j = l - n_a
            qs, o, lse, mo = mixer
            do, dmo, gmix_post[l] = _oproj_bwd(dx, mo, w_o_f, post, l, j, f"oproj_bwd_{l}")
            dqs, dk, dv, gsinks[j] = _attn_bwd(qs, kdup, vdup, sinks, lse, do, j, f"attn_bwd_{l}")
            dks.append(dk)
            dvs.append(dv)
            dx, hq, dq, gmix_pre[l] = _q_bwd(dx, dqs, x_in, mix_pre, w_q_f, cos, ssin, l, j, f"q_bwd_{l}")
            if j == 0:
                dx, hkv, dkv, gkv_g = _kv_bwd(dx, x_kv, kv_g, w_kv_f, cos, ssin, dks, dvs, "kv_bwd")
                gkv = _tn_matmul(hkv, dkv, whole(d_model), whole(dkv.shape[1]), 1, d_model, dkv.shape[1], "grad_w_kv", _WIRE_DTYPE)
                going.append(("w_kv", 0, by_dev(gkv), 0))
            go = _tn_matmul(o, dmo, whole(d_model), whole(d_model), 1, d_model, d_model, f"grad_w_o_{l}", _WIRE_DTYPE)
            gq = _tn_matmul(hq, dq, whole(d_model), whole(d_model), 1, d_model, d_model, f"grad_w_q_{l}", _WIRE_DTYPE)
            going += [("w_o", j, by_dev(go), 0), ("w_q", j, by_dev(gq), 0)]
        gout = _tn_matmul(gact, df, per_out(fs), whole(d_model), half, fs, d_model, f"grad_w_out_{l}", _WIRE_DTYPE)
        gin = _tn_matmul(hf, du, whole(d_model), per_out(fs), N_DEV, d_model, fs, f"grad_w_in_{l}", _WIRE_DTYPE)
        going += [("ffn_w_out", l, by_dev(gout), 0), ("ffn_w_in", l, gin, 0)]
        handle, token = _exchange_start(SCATTER, [g for _, _, g, _ in going], [a for _, _, _, a in going], dx, f"scatter_start_{l}")
        flying.append(([(nm, slab) for nm, slab, _, _ in going], handle))
        ffn_post_b = ffn_post_b + token[0, 0]

    grad_x = dx.reshape(x.shape)

    shard = {"pool_w": (pool_w, m_pool_w, v_pool_w), "w_kv": (w_kv, m_w_kv, v_w_kv), "w_q": (w_q, m_w_q, v_w_q),
             "w_o": (w_o, m_w_o, v_w_o), "ffn_w_in": (ffn_w_in, m_ffn_w_in, v_ffn_w_in),
             "ffn_w_out": (ffn_w_out, m_ffn_w_out, v_ffn_w_out)}
    big = {}
    for names, handle in flying:
        parts = _exchange_wait(handle, token, f"scatter_wait_{names[-1][1]}")
        for (nm, slab), p in zip(names, parts):
            cols = p.shape[-1]
            w2, m2, v2 = (a.reshape(-1, cols) for a in shard[nm])
            big[nm] = _adamw(p.reshape(N_DEV, -1, cols), w2, m2, v2, slab, big.get(nm), f"adamw_{nm}_{slab}")
    big = {nm: tuple(r.reshape(shard[nm][0].shape) for r in res) for nm, res in big.items()}

    cat = lambda rows: jnp.concatenate(rows, axis=0)
    row = lambda a: a.reshape(1, -1)
    everything = lambda ref, s, me: ref[s]
    lanes = pool_scale.shape[1]
    small = [("mix_pre_g", cat(gmix_pre), everything, (mix_pre_g, m_mix_pre_g, v_mix_pre_g)),
             ("mix_post_g", cat(gmix_post), everything, (mix_post_g, m_mix_post_g, v_mix_post_g)),
             ("kv_norm_g", gkv_g, everything, (row(kv_norm_g), row(m_kv_norm_g), row(v_kv_norm_g))),
             ("sinks", cat(gsinks), lambda ref, s, me: ref[s, :, pl.ds(0, n_heads)], (sinks, m_sinks, v_sinks)),
             ("ffn_pre_g", cat(gffn_pre), everything, (ffn_pre_g, m_ffn_pre_g, v_ffn_pre_g)),
             ("ffn_post_g", cat(gffn_post), everything, (ffn_post_g, m_ffn_post_g, v_ffn_post_g)),
             ("ffn_conv_b", jnp.stack(gconv_b).reshape(depth, N_DEV * fs), everything, (ffn_conv_b, m_ffn_conv_b, v_ffn_conv_b)),
             ("pool_scale", cat(gpool_scale), lambda ref, s, me: ref[s, :, pl.ds(pl.multiple_of(me * lanes, lanes), lanes)],
              (pool_scale, m_pool_scale, v_pool_scale)),
             ("ffn_conv_w", jnp.stack(gconv_w).transpose(0, 2, 1, 3), lambda ref, s, me: ref[s, :, me],
              (ffn_conv_w, m_ffn_conv_w, v_ffn_conv_w))]
    small_parts = _all_gather([g for _, g, _, _ in small], [0] * len(small), "gather_small_grads")
    upd = _adamw_small(small_parts, [pick for _, _, pick, _ in small], [wmv for _, _, _, wmv in small], "adamw_small")
    res = dict(big)
    for (nm, _, _, _), out in zip(small, upd):
        res[nm] = tuple(a.reshape(kv_norm_g.shape) for a in out) if nm == "kv_norm_g" else out

    order = ["mix_pre_g", "mix_post_g", "pool_w", "pool_scale", "kv_norm_g", "w_kv", "w_q", "w_o", "sinks", "ffn_pre_g",
             "ffn_post_g", "ffn_w_in", "ffn_conv_w", "ffn_conv_b", "ffn_w_out"]
    return (loss, grad_x, *[res[nm][0] for nm in order], *[res[nm][1] for nm in order],
            *[res[nm][2] for nm in order], *[res[nm][3] for nm in order])
```

```python
import functools
import math

import jax
import jax.numpy as jnp
from jax import lax
from jax.experimental import pallas as pl
from jax.experimental.pallas import tpu as pltpu

F32 = jnp.float32
_MXU_DTYPE = jnp.bfloat16
_ACT_DTYPE = jnp.bfloat16
_WIRE_DTYPE = jnp.bfloat16
_SAVE_DTYPE = jnp.bfloat16

N_DEV = 8
POOL_WINDOWS = (2, 4, 8, 16)
POOL_HALO = 16
HEAD_DIM = 64
N_KV_HEADS = 4
WINDOW = 128
BLOCK = 128
LANES = 128
ROPE_THETA = 10000.0
ATTN_SCALE = 1.0 / math.sqrt(HEAD_DIM)
NEG_INF = -1e30
RMS_EPS = 1e-6
CONV_HALO = 8
SAVE_HALO = 16
PROJ_TILE = 512
ADAM_LR = 0.001
ADAM_B1 = 0.9
ADAM_B2 = 0.999
ADAM_EPS = 1e-08
ADAM_WD = 0.01
ADAM_STEP = 10
VMEM_LIMIT = 56 * 1024 * 1024
MESH = pl.DeviceIdType.MESH
AXES = ("x", "y", "c")


def _params(n_axes=1, vmem=VMEM_LIMIT):
    return pltpu.CompilerParams(dimension_semantics=("arbitrary",) * n_axes, vmem_limit_bytes=vmem)


def _resident(shape, index):
    return pl.BlockSpec(shape, lambda *_: index, pipeline_mode=pl.Buffered(1))


def _const(shape, index=None):
    index = (0,) * len(shape) if index is None else index
    return pl.BlockSpec(shape, lambda *_: index)


def _rows(tm, cols):
    return pl.BlockSpec((tm, cols), lambda i: (i, 0))


def _row_tile(t, most=256):
    for tm in (512, 256, 128, 64, 32, 16, 8):
        if tm <= most and t % tm == 0:
            return tm
    raise ValueError(f"sequence length {t} is not a multiple of 8")


def _mm(a, b):
    return jnp.dot(a.astype(_MXU_DTYPE), b.astype(_MXU_DTYPE), preferred_element_type=F32)


def _mm_tb(a, b):
    return lax.dot_general(a.astype(_MXU_DTYPE), b.astype(_MXU_DTYPE), (((1,), (1,)), ((), ())),
                           preferred_element_type=F32)


def _mm_ta(a, b):
    return lax.dot_general(a.astype(_MXU_DTYPE), b.astype(_MXU_DTYPE), (((0,), (0,)), ((), ())),
                           preferred_element_type=F32)


def _rms_r(x):
    return lax.rsqrt(jnp.mean(x * x, axis=-1, keepdims=True) + RMS_EPS)


def _rms_fwd(x, g):
    return (x * _rms_r(x)) * g


def _rms_bwd(x, g, dy):
    r = _rms_r(x)
    xh = x * r
    dg = jnp.sum(dy * xh, axis=0, keepdims=True)
    dxh = dy * g
    dx = r * (dxh - xh * jnp.mean(dxh * xh, axis=-1, keepdims=True))
    return dx, dg


_GELU_C = math.sqrt(2.0 / math.pi)


def _gelu_parts(z):
    z2 = z * z
    t = jnp.tanh(_GELU_C * (z + 0.044715 * (z2 * z)))
    cdf = 0.5 * (1.0 + t)
    dz = cdf + z * (0.5 * (1.0 - t * t)) * (_GELU_C * (1.0 + (3 * 0.044715) * z2))
    return cdf, dz


def _lane_iota(shape):
    return lax.broadcasted_iota(jnp.int32, shape, len(shape) - 1)


def _rope_partner(xb):
    first = (_lane_iota(xb.shape) & 32) == 0
    return jnp.where(first, pltpu.roll(xb, LANES - 32, 1), pltpu.roll(xb, 32, 1))


def _rope_fwd(xb, cos, ssin):
    return xb * cos + _rope_partner(xb) * ssin


def _rope_bwd(dyb, cos, ssin):
    return dyb * cos - _rope_partner(dyb) * ssin


def _low_half(shape):
    return (_lane_iota(shape) & 64) == 0


def _pool_fwd(x, pre_g, post_g, w, scale, layer, name):
    t, d_model = x.shape
    tm = _row_tile(t)
    n_groups, gc = w.shape[1], w.shape[2]

    def body(x_ref, pre_ref, post_ref, w_ref, sc_ref, xo_ref, d_ref, yu_ref, hbuf):
        i = pl.program_id(0)

        @pl.when(i == 0)
        def _():
            hbuf[pl.ds(0, POOL_HALO), :] = jnp.zeros((POOL_HALO, d_model), F32)

        xv = x_ref[...]
        hbuf[pl.ds(POOL_HALO, tm), :] = _rms_fwd(xv, pre_ref[...])
        tok = i * tm + lax.broadcasted_iota(jnp.int32, (tm, 1), 0)
        yus = []
        for gi, wnd in enumerate(POOL_WINDOWS):
            cols = pl.ds(gi * gc, gc)
            h = hbuf[pl.ds(POOL_HALO, tm), cols]
            acc = h
            for k in range(1, wnd):
                acc = acc + hbuf[pl.ds(POOL_HALO - k, tm), cols]
            cnt = jnp.minimum(tok + 1, wnd).astype(F32)
            dg = acc / cnt - h
            d_ref[:, cols] = dg.astype(d_ref.dtype)
            yus.append(_mm(dg, w_ref[gi]))
        hbuf[pl.ds(0, POOL_HALO), :] = hbuf[pl.ds(tm, POOL_HALO), :]
        yu = jnp.concatenate(yus, axis=1)
        yu_ref[...] = yu
        xo_ref[...] = xv + _rms_fwd(yu * sc_ref[...], post_ref[...])

    return pl.pallas_call(
        body, name=name, grid=(t // tm,),
        out_shape=(jax.ShapeDtypeStruct((t, d_model), F32), jax.ShapeDtypeStruct((t, d_model), _ACT_DTYPE),
                   jax.ShapeDtypeStruct((t, d_model), F32)),
        in_specs=[_rows(tm, d_model), _const((None, 1, d_model), (layer, 0, 0)), _const((None, 1, d_model), (layer, 0, 0)),
                  _const((None, n_groups, gc, gc), (layer, 0, 0, 0)), _const((None, 1, d_model), (layer, 0, 0))],
        out_specs=(_rows(tm, d_model), _rows(tm, d_model), _rows(tm, d_model)),
        scratch_shapes=[pltpu.VMEM((POOL_HALO + tm, d_model), F32)],
        compiler_params=_params(),
    )(x, pre_g, post_g, w, scale)


def _pool_bwd(dx, x, pre_g, post_g, d, yu, w, scale, layer, name):
    t, d_model = x.shape
    tm = _row_tile(t)
    nt = t // tm
    n_groups, gc = w.shape[1], w.shape[2]
    rev = lambda i: (nt - 1 - i, 0)
    rows = pl.BlockSpec((tm, d_model), rev)

    def body(dx_ref, x_ref, pre_ref, post_ref, d_ref, yu_ref, w_ref, sc_ref,
             dxi_ref, dyu_ref, dsc_ref, dpre_ref, dpost_ref, zbuf):
        i = pl.program_id(0)

        @pl.when(i == 0)
        def _():
            zbuf[pl.ds(tm, POOL_HALO), :] = jnp.zeros((POOL_HALO, d_model), F32)
            dsc_ref[...] = jnp.zeros_like(dsc_ref)
            dpre_ref[...] = jnp.zeros_like(dpre_ref)
            dpost_ref[...] = jnp.zeros_like(dpost_ref)

        dxo = dx_ref[...]
        yuv = yu_ref[...]
        sc = sc_ref[...]
        dm, dpost = _rms_bwd(yuv * sc, post_ref[...], dxo)
        dpost_ref[...] += dpost
        dsc_ref[...] += jnp.sum(dm * yuv, axis=0, keepdims=True)
        dyu = dm * sc
        dyu_ref[...] = dyu.astype(dyu_ref.dtype)
        tok = (nt - 1 - i) * tm + lax.broadcasted_iota(jnp.int32, (tm, 1), 0)
        dds = []
        for gi, wnd in enumerate(POOL_WINDOWS):
            cols = pl.ds(gi * gc, gc)
            dd = _mm_tb(dyu[:, gi * gc:(gi + 1) * gc], w_ref[gi])
            cnt = jnp.minimum(tok + 1, wnd).astype(F32)
            zbuf[pl.ds(0, tm), cols] = dd / cnt
            dds.append(dd)
        dhs = []
        for gi, wnd in enumerate(POOL_WINDOWS):
            cols = pl.ds(gi * gc, gc)
            acc = zbuf[pl.ds(0, tm), cols]
            for k in range(1, wnd):
                acc = acc + zbuf[pl.ds(k, tm), cols]
            dhs.append(acc - dds[gi])
        zbuf[pl.ds(tm, POOL_HALO), :] = zbuf[pl.ds(0, POOL_HALO), :]
        dh = jnp.concatenate(dhs, axis=1)
        dxp, dpre = _rms_bwd(x_ref[...], pre_ref[...], dh)
        dpre_ref[...] += dpre
        dxi_ref[...] = dxo + dxp

    vec = jax.ShapeDtypeStruct((1, d_model), F32)
    return pl.pallas_call(
        body, name=name, grid=(nt,),
        out_shape=(jax.ShapeDtypeStruct((t, d_model), F32), jax.ShapeDtypeStruct((t, d_model), _ACT_DTYPE), vec, vec, vec),
        in_specs=[rows, rows, _const((None, 1, d_model), (layer, 0, 0)), _const((None, 1, d_model), (layer, 0, 0)), rows, rows,
                  _const((None, n_groups, gc, gc), (layer, 0, 0, 0)), _const((None, 1, d_model), (layer, 0, 0))],
        out_specs=(rows, rows, _const((1, d_model)), _const((1, d_model)), _const((1, d_model))),
        scratch_shapes=[pltpu.VMEM((tm + POOL_HALO, d_model), F32)],
        compiler_params=_params(),
    )(dx, x, pre_g, post_g, d, yu, w, scale)


def _conv_taps(cw_ref, s):
    return [cw_ref[k, pl.ds(s, 1), :] for k in range(3)]


def _shift_down(v, k, before):
    rolled = pltpu.roll(v, k, 0)
    row = lax.broadcasted_iota(jnp.int32, before.shape, 0)
    head = jnp.where(row < k, pltpu.roll(before, k, 0), rolled[:CONV_HALO])
    return jnp.concatenate([head, rolled[CONV_HALO:]], axis=0)


def _shift_up(v, k, after):
    rows = v.shape[0]
    rolled = pltpu.roll(v, rows - k, 0)
    row = lax.broadcasted_iota(jnp.int32, after.shape, 0)
    tail = jnp.where(row >= CONV_HALO - k, pltpu.roll(after, CONV_HALO - k, 0), rolled[rows - CONV_HALO:])
    return jnp.concatenate([rolled[:rows - CONV_HALO], tail], axis=0)


def _ffn_fwd(x, pre_g, post_g, w_in, conv_w, conv_b, w_out, layer, w_layer, name):
    t, d_model = x.shape
    tm = _row_tile(t)
    fs = w_in.shape[3]
    half = N_DEV // 2

    def body(x_ref, pre_ref, post_ref, win_ref, cw_ref, cb_ref, wout_ref, xo_ref, u_ref, uc_ref, f_ref, carry):
        i = pl.program_id(0)

        @pl.when(i == 0)
        def _():
            carry[...] = jnp.zeros_like(carry)

        xv = x_ref[...]
        hf = _rms_fwd(xv, pre_ref[...]).astype(_MXU_DTYPE)
        f = jnp.zeros((tm, d_model), F32)
        project = lambda b: [jnp.dot(hf, win_ref[s], preferred_element_type=F32) for s in (b, b + half)]
        ahead = project(0)
        for b in range(half):
            us, ahead = ahead, project(b + 1) if b + 1 < half else None
            ucs = []
            for s, u in zip((b, b + half), us):
                u_ref[s] = u.astype(u_ref.dtype)
                before = carry[s]
                carry[s] = u[tm - CONV_HALO:]
                w0, w1, w2 = _conv_taps(cw_ref, s)
                uc = ((w0 * _shift_down(u, 2, before) + w1 * _shift_down(u, 1, before)) + w2 * u) + cb_ref[pl.ds(s, 1), :]
                uc_ref[s] = uc.astype(uc_ref.dtype)
                ucs.append(uc)
            gate, val = ucs
            cdf, _ = _gelu_parts(gate)
            f = f + _mm((gate * cdf) * val, wout_ref[pl.ds(b * fs, fs), :])
        f_ref[...] = f
        xo_ref[...] = xv + _rms_fwd(f, post_ref[...])

    tile3 = pl.BlockSpec((N_DEV, tm, fs), lambda i: (0, i, 0))
    saved = jax.ShapeDtypeStruct((N_DEV, t, fs), _SAVE_DTYPE)
    return pl.pallas_call(
        body, name=name, grid=(t // tm,),
        out_shape=(jax.ShapeDtypeStruct((t, d_model), F32), saved, saved, jax.ShapeDtypeStruct((t, d_model), F32)),
        in_specs=[_rows(tm, d_model), _const((None, 1, d_model), (layer, 0, 0)), _const((None, 1, d_model), (layer, 0, 0)),
                  _resident((None, N_DEV, d_model, fs), (w_layer, 0, 0, 0)), _const((None, 3, N_DEV, fs), (layer, 0, 0, 0)),
                  _const((None, N_DEV, fs), (layer, 0, 0)), _resident((None, half * fs, d_model), (w_layer, 0, 0))],
        out_specs=(_rows(tm, d_model), tile3, tile3, _rows(tm, d_model)),
        scratch_shapes=[pltpu.VMEM((N_DEV, CONV_HALO, fs), F32)],
        compiler_params=_params(),
    )(x, pre_g, post_g, w_in, conv_w, conv_b, w_out)


def _ffn_bwd_out(dx, f, post_g, uc, conv_b, w_out, layer, w_layer, name):
    t, d_model = dx.shape
    tm = _row_tile(t)
    fs = uc.shape[2]
    half = N_DEV // 2

    def body(dx_ref, f_ref, post_ref, uc_ref, wout_ref, duc_ref, g_ref, df_ref, dcb_ref, dpost_ref):
        i = pl.program_id(0)

        @pl.when(i == 0)
        def _():
            dcb_ref[...] = jnp.zeros_like(dcb_ref)
            dpost_ref[...] = jnp.zeros_like(dpost_ref)

        df, dpost = _rms_bwd(f_ref[...], post_ref[...], dx_ref[...])
        dpost_ref[...] += dpost
        dfm = df.astype(_MXU_DTYPE)
        df_ref[...] = dfm
        project = lambda b: _mm_tb(dfm, wout_ref[pl.ds(b * fs, fs), :])
        ahead = project(0)
        for b in range(half):
            dg, ahead = ahead, project(b + 1) if b + 1 < half else None
            gate = uc_ref[b].astype(F32)
            val = uc_ref[b + half].astype(F32)
            cdf, dgelu = _gelu_parts(gate)
            ge = gate * cdf
            g_ref[b] = (ge * val).astype(g_ref.dtype)
            for s, dd in ((b, dg * val * dgelu), (b + half, dg * ge)):
                duc_ref[s] = dd.astype(duc_ref.dtype)
                dcb_ref[pl.ds(s, 1), :] += jnp.sum(dd, axis=0, keepdims=True)

    tile3 = pl.BlockSpec((N_DEV, tm, fs), lambda i: (0, i, 0))
    return pl.pallas_call(
        body, name=name, grid=(t // tm,),
        out_shape=(jax.ShapeDtypeStruct((N_DEV, t, fs), _SAVE_DTYPE), jax.ShapeDtypeStruct((half, t, fs), _ACT_DTYPE),
                   jax.ShapeDtypeStruct((t, d_model), _ACT_DTYPE), jax.ShapeDtypeStruct((N_DEV, fs), F32),
                   jax.ShapeDtypeStruct((1, d_model), F32)),
        in_specs=[_rows(tm, d_model), _rows(tm, d_model), _const((None, 1, d_model), (layer, 0, 0)), tile3,
                  _resident((None, half * fs, d_model), (w_layer, 0, 0))],
        out_specs=(tile3, pl.BlockSpec((half, tm, fs), lambda i: (0, i, 0)), _rows(tm, d_model),
                   _const((N_DEV, fs)), _const((1, d_model))),
        compiler_params=_params(),
    )(dx, f, post_g, uc, w_out)


def _ffn_bwd_in(dx, x, pre_g, duc, u, conv_w, w_in, layer, w_layer, name):
    t, d_model = dx.shape
    tm = _row_tile(t)
    nt = t // tm
    fs = duc.shape[2]
    hb = SAVE_HALO
    per_tile = tm // hb

    def body(dx_ref, x_ref, pre_ref, duc_ref, dn_ref, u_ref, cw_ref, win_ref, dxi_ref, du_ref, hf_ref, dcw_ref, dpre_ref):
        i = pl.program_id(0)

        @pl.when(i == 0)
        def _():
            dcw_ref[...] = jnp.zeros_like(dcw_ref)
            dpre_ref[...] = jnp.zeros_like(dpre_ref)

        xv = x_ref[...]
        pre = pre_ref[...]
        hf_ref[...] = _rms_fwd(xv, pre).astype(hf_ref.dtype)
        dhf = jnp.zeros((tm, d_model), F32)
        for s in range(N_DEV):
            d0 = duc_ref[s].astype(F32)
            after = jnp.where(i == nt - 1, 0.0, dn_ref[s].astype(F32)[:CONV_HALO])
            d1 = _shift_up(d0, 1, after)
            d2 = _shift_up(d0, 2, after)
            uv = u_ref[s].astype(F32)
            for k, dk in ((2, d0), (1, d1), (0, d2)):
                dcw_ref[k, pl.ds(s, 1), :] += jnp.sum(dk * uv, axis=0, keepdims=True)
            w0, w1, w2 = _conv_taps(cw_ref, s)
            du = (w2 * d0 + w1 * d1 + w0 * d2).astype(_MXU_DTYPE)
            du_ref[s] = du
            dhf = dhf + _mm_tb(du, win_ref[s])
        dxp, dpre = _rms_bwd(xv, pre, dhf)
        dpre_ref[...] += dpre
        dxi_ref[...] = dx_ref[...] + dxp

    tile3 = pl.BlockSpec((N_DEV, tm, fs), lambda i: (0, i, 0))
    return pl.pallas_call(
        body, name=name, grid=(nt,),
        out_shape=(jax.ShapeDtypeStruct((t, d_model), F32), jax.ShapeDtypeStruct((N_DEV, t, fs), _ACT_DTYPE),
                   jax.ShapeDtypeStruct((t, d_model), _ACT_DTYPE), jax.ShapeDtypeStruct((3, N_DEV, fs), F32),
                   jax.ShapeDtypeStruct((1, d_model), F32)),
        in_specs=[_rows(tm, d_model), _rows(tm, d_model), _const((None, 1, d_model), (layer, 0, 0)), tile3,
                  pl.BlockSpec((N_DEV, hb, fs), lambda i: (0, jnp.minimum((i + 1) * per_tile, t // hb - 1), 0)), tile3,
                  _const((None, 3, N_DEV, fs), (layer, 0, 0, 0)), _resident((None, N_DEV, d_model, fs), (w_layer, 0, 0, 0))],
        out_specs=(_rows(tm, d_model), tile3, _rows(tm, d_model), _const((3, N_DEV, fs)), _const((1, d_model))),
        compiler_params=_params(),
    )(dx, x, pre_g, duc, duc, u, conv_w, w_in)


def _tn_matmul(a, b, a_spec, b_spec, n_out, m, n, name, out_dtype):
    def body(a_ref, b_ref, o_ref):
        o_ref[...] = _mm_ta(a_ref[...], b_ref[...]).astype(o_ref.dtype)

    return pl.pallas_call(
        body, name=name, grid=(n_out,),
        out_shape=jax.ShapeDtypeStruct((n_out, m, n), out_dtype),
        in_specs=[a_spec, b_spec],
        out_specs=pl.BlockSpec((None, m, n), lambda c: (c, 0, 0)),
        compiler_params=_params(),
    )(a, b)


def _kv_fwd(x, kv_g, w_kv, cos, ssin, name):
    t, d_model = x.shape
    tm = _row_tile(t, PROJ_TILE)
    kvd = w_kv.shape[1] // 2
    pairs = kvd // LANES

    def body(x_ref, g_ref, w_ref, cos_ref, sin_ref, k_ref, v_ref):
        kv = _mm(_rms_fwd(x_ref[...], g_ref[...]), w_ref[...])
        low = _low_half((tm, LANES))
        for j in range(pairs):
            kb = _rope_fwd(kv[:, j * LANES:(j + 1) * LANES], cos_ref[...], sin_ref[...])
            vb = kv[:, kvd + j * LANES:kvd + (j + 1) * LANES]
            for blk, ref in ((kb, k_ref), (vb, v_ref)):
                sw = pltpu.roll(blk, 64, 1)
                ref[2 * j] = jnp.where(low, blk, sw).astype(ref.dtype)
                ref[2 * j + 1] = jnp.where(low, sw, blk).astype(ref.dtype)

    heads = jax.ShapeDtypeStruct((N_KV_HEADS, t, LANES), _ACT_DTYPE)
    hspec = pl.BlockSpec((N_KV_HEADS, tm, LANES), lambda i: (0, i, 0))
    return pl.pallas_call(
        body, name=name, grid=(t // tm,), out_shape=(heads, heads),
        in_specs=[_rows(tm, d_model), _const((1, d_model)), _const(w_kv.shape), _rows(tm, LANES), _rows(tm, LANES)],
        out_specs=(hspec, hspec), compiler_params=_params(),
    )(x, kv_g, w_kv, cos, ssin)


def _kv_bwd(dx, x, kv_g, w_kv, cos, ssin, dks, dvs, name):
    t, d_model = x.shape
    tm = _row_tile(t, PROJ_TILE)
    kvd = w_kv.shape[1] // 2
    pairs = kvd // LANES
    n_users = len(dks)

    def body(dx_ref, x_ref, g_ref, w_ref, cos_ref, sin_ref, *refs):
        dk_refs, dv_refs = refs[:n_users], refs[n_users:2 * n_users]
        dxi_ref, h_ref, dkv_ref, dg_ref = refs[2 * n_users:]
        dk_ref = functools.reduce(lambda a, b: a + b, [r[...] for r in dk_refs])
        dv_ref = functools.reduce(lambda a, b: a + b, [r[...] for r in dv_refs])
        i = pl.program_id(0)

        @pl.when(i == 0)
        def _():
            dg_ref[...] = jnp.zeros_like(dg_ref)

        xv = x_ref[...]
        g = g_ref[...]
        h_ref[...] = _rms_fwd(xv, g).astype(h_ref.dtype)
        low = _low_half((tm, LANES))
        dks, dvs = [], []
        for j in range(pairs):
            dkb = jnp.where(low, dk_ref[2 * j], dk_ref[2 * j + 1])
            dks.append(_rope_bwd(dkb, cos_ref[...], sin_ref[...]))
            dvs.append(jnp.where(low, dv_ref[2 * j], dv_ref[2 * j + 1]))
        dkv = jnp.concatenate(dks + dvs, axis=1).astype(_MXU_DTYPE)
        dkv_ref[...] = dkv
        dxp, dg = _rms_bwd(xv, g, _mm_tb(dkv, w_ref[...]))
        dg_ref[...] += dg
        dxi_ref[...] = dx_ref[...] + dxp

    hspec = pl.BlockSpec((N_KV_HEADS, tm, LANES), lambda i: (0, i, 0))
    return pl.pallas_call(
        body, name=name, grid=(t // tm,),
        out_shape=(jax.ShapeDtypeStruct((t, d_model), F32), jax.ShapeDtypeStruct((t, d_model), _ACT_DTYPE),
                   jax.ShapeDtypeStruct((t, 2 * kvd), _ACT_DTYPE), jax.ShapeDtypeStruct((1, d_model), F32)),
        in_specs=[_rows(tm, d_model), _rows(tm, d_model), _const((1, d_model)), _const(w_kv.shape),
                  _rows(tm, LANES), _rows(tm, LANES)] + [hspec] * (2 * n_users),
        out_specs=(_rows(tm, d_model), _rows(tm, d_model), _rows(tm, 2 * kvd), _const((1, d_model))),
        compiler_params=_params(),
    )(dx, x, kv_g, w_kv, cos, ssin, *dks, *dvs)


def _q_fwd(x, pre_g, w_q, cos, ssin, layer, j, name):
    t, d_model = x.shape
    tm = _row_tile(t, PROJ_TILE)

    def body(x_ref, g_ref, w_ref, cos_ref, sin_ref, q_ref):
        q = _mm(_rms_fwd(x_ref[...], g_ref[...]), w_ref[...])
        for p in range(d_model // LANES):
            cols = slice(p * LANES, (p + 1) * LANES)
            q_ref[:, cols] = (_rope_fwd(q[:, cols], cos_ref[...], sin_ref[...]) * ATTN_SCALE).astype(q_ref.dtype)

    return pl.pallas_call(
        body, name=name, grid=(t // tm,), out_shape=jax.ShapeDtypeStruct((t, d_model), _ACT_DTYPE),
        in_specs=[_rows(tm, d_model), _const((None, 1, d_model), (layer, 0, 0)), _const((None, d_model, d_model), (j, 0, 0)),
                  _rows(tm, LANES), _rows(tm, LANES)],
        out_specs=_rows(tm, d_model), compiler_params=_params(),
    )(x, pre_g, w_q, cos, ssin)


def _q_bwd(dx, dqs, x, pre_g, w_q, cos, ssin, layer, j, name):
    t, d_model = x.shape
    tm = _row_tile(t, PROJ_TILE)

    def body(dx_ref, dq_ref, x_ref, g_ref, w_ref, cos_ref, sin_ref, dxi_ref, h_ref, dqo_ref, dg_ref):
        i = pl.program_id(0)

        @pl.when(i == 0)
        def _():
            dg_ref[...] = jnp.zeros_like(dg_ref)

        xv = x_ref[...]
        g = g_ref[...]
        h_ref[...] = _rms_fwd(xv, g).astype(h_ref.dtype)
        parts = []
        for p in range(d_model // LANES):
            cols = slice(p * LANES, (p + 1) * LANES)
            parts.append(_rope_bwd(dq_ref[:, cols] * ATTN_SCALE, cos_ref[...], sin_ref[...]))
        dq = jnp.concatenate(parts, axis=1).astype(_MXU_DTYPE)
        dqo_ref[...] = dq
        dxp, dg = _rms_bwd(xv, g, _mm_tb(dq, w_ref[...]))
        dg_ref[...] += dg
        dxi_ref[...] = dx_ref[...] + dxp

    act = jax.ShapeDtypeStruct((t, d_model), _ACT_DTYPE)
    return pl.pallas_call(
        body, name=name, grid=(t // tm,),
        out_shape=(jax.ShapeDtypeStruct((t, d_model), F32), act, act, jax.ShapeDtypeStruct((1, d_model), F32)),
        in_specs=[_rows(tm, d_model), _rows(tm, d_model), _rows(tm, d_model), _const((None, 1, d_model), (layer, 0, 0)),
                  _const((None, d_model, d_model), (j, 0, 0)), _rows(tm, LANES), _rows(tm, LANES)],
        out_specs=(_rows(tm, d_model), _rows(tm, d_model), _rows(tm, d_model), _const((1, d_model))),
        compiler_params=_params(),
    )(dx, dqs, x, pre_g, w_q, cos, ssin)


def _attn_scores(q_pair, k2, n, sink_a, sink_b):
    low = _low_half(q_pair.shape)
    zero = jnp.zeros_like(q_pair)
    qst = jnp.concatenate([jnp.where(low, q_pair, zero), jnp.where(low, zero, q_pair)], axis=0)
    s = _mm_tb(qst, k2)
    row = lax.broadcasted_iota(jnp.int32, s.shape, 0)
    col = lax.broadcasted_iota(jnp.int32, s.shape, 1)
    rel = BLOCK + (row & (BLOCK - 1)) - col
    valid = (rel >= 0) & (rel < WINDOW) & (n * BLOCK + col - BLOCK >= 0)
    s = jnp.where(valid, s, NEG_INF)
    rows1 = lax.broadcasted_iota(jnp.int32, (2 * BLOCK, 1), 0)
    sink = jnp.where(rows1 < BLOCK, sink_a, sink_b)
    return qst, s, sink


def _attn_fwd(qs, kdup, vdup, sinks, j, name):
    t, d_model = qs.shape
    nb = t // BLOCK
    n_pairs = d_model // LANES
    per_group = n_pairs // N_KV_HEADS

    def body(sink_ref, q_ref, kp_ref, ko_ref, vp_ref, vo_ref, o_ref, lse_ref):
        n = pl.program_id(0)
        low = _low_half((BLOCK, LANES))
        lane = _lane_iota((BLOCK, LANES))
        lse = jnp.zeros((BLOCK, LANES), F32)
        for p in range(n_pairs):
            hk = p // per_group
            k2 = jnp.concatenate([kp_ref[hk], ko_ref[hk]], axis=0)
            v2 = jnp.concatenate([vp_ref[hk], vo_ref[hk]], axis=0)
            _, s, sink = _attn_scores(q_ref[:, p * LANES:(p + 1) * LANES], k2, n, sink_ref[j, 2 * p], sink_ref[j, 2 * p + 1])
            m = jnp.maximum(jnp.max(s, axis=-1, keepdims=True), sink)
            pe = jnp.exp(s - m)
            denom = jnp.sum(pe, axis=-1, keepdims=True) + jnp.exp(sink - m)
            o2 = _mm(pe, v2) / denom
            o_ref[:, p * LANES:(p + 1) * LANES] = jnp.where(low, o2[:BLOCK], o2[BLOCK:]).astype(o_ref.dtype)
            l2 = m + jnp.log(denom)
            lse = jnp.where(lane == 2 * p, l2[:BLOCK], lse)
            lse = jnp.where(lane == 2 * p + 1, l2[BLOCK:], lse)
        lse_ref[...] = lse

    prev = pl.BlockSpec((N_KV_HEADS, BLOCK, LANES), lambda n: (0, jnp.maximum(n - 1, 0), 0))
    own = pl.BlockSpec((N_KV_HEADS, BLOCK, LANES), lambda n: (0, n, 0))
    return pl.pallas_call(
        body, name=name, grid=(nb,),
        out_shape=(jax.ShapeDtypeStruct((t, d_model), _ACT_DTYPE), jax.ShapeDtypeStruct((t, LANES), F32)),
        in_specs=[pl.BlockSpec(memory_space=pltpu.SMEM), _rows(BLOCK, d_model), prev, own, prev, own],
        out_specs=(_rows(BLOCK, d_model), _rows(BLOCK, LANES)), compiler_params=_params(),
    )(sinks, qs, kdup, kdup, vdup, vdup)


def _attn_bwd(qs, kdup, vdup, sinks, lse, do, j, name):
    t, d_model = qs.shape
    nb = t // BLOCK
    n_pairs = d_model // LANES
    per_group = n_pairs // N_KV_HEADS
    rev = lambda n: nb - 1 - n

    def body(sink_ref, q_ref, kp_ref, ko_ref, vp_ref, vo_ref, lse_ref, do_ref, dq_ref, dk_ref, dv_ref, ds_ref, ck, cv):
        i = pl.program_id(0)
        n = nb - 1 - i

        @pl.when(i == 0)
        def _():
            ck[...] = jnp.zeros_like(ck)
            cv[...] = jnp.zeros_like(cv)
            ds_ref[...] = jnp.zeros_like(ds_ref)

        low = _low_half((BLOCK, LANES))
        lane = _lane_iota((BLOCK, LANES))
        lane1 = _lane_iota((1, LANES))
        lsev = lse_ref[...]
        dsink = jnp.zeros((1, LANES), F32)
        dk2 = [jnp.zeros((2 * BLOCK, LANES), F32) for _ in range(N_KV_HEADS)]
        dv2 = [jnp.zeros((2 * BLOCK, LANES), F32) for _ in range(N_KV_HEADS)]
        for p in range(n_pairs):
            hk = p // per_group
            k2 = jnp.concatenate([kp_ref[hk], ko_ref[hk]], axis=0)
            v2 = jnp.concatenate([vp_ref[hk], vo_ref[hk]], axis=0)
            qst, s, sink = _attn_scores(q_ref[:, p * LANES:(p + 1) * LANES], k2, n, sink_ref[j, 2 * p], sink_ref[j, 2 * p + 1])
            l2 = jnp.concatenate([jnp.sum(jnp.where(lane == 2 * p, lsev, 0.0), axis=-1, keepdims=True),
                                  jnp.sum(jnp.where(lane == 2 * p + 1, lsev, 0.0), axis=-1, keepdims=True)], axis=0)
            pn = jnp.exp(s - l2)
            dop = do_ref[:, p * LANES:(p + 1) * LANES]
            zero = jnp.zeros_like(dop)
            dost = jnp.concatenate([jnp.where(low, dop, zero), jnp.where(low, zero, dop)], axis=0)
            dp = _mm_tb(dost, v2)
            dr = jnp.sum(pn * dp, axis=-1, keepdims=True)
            dsm = (pn * (dp - dr)).astype(_MXU_DTYPE)
            dsk = -jnp.exp(sink - l2) * dr
            dsink = dsink + jnp.where(lane1 == 2 * p, jnp.sum(dsk[:BLOCK]), 0.0) + jnp.where(lane1 == 2 * p + 1, jnp.sum(dsk[BLOCK:]), 0.0)
            dq2 = _mm(dsm, k2)
            dq_ref[:, p * LANES:(p + 1) * LANES] = jnp.where(low, dq2[:BLOCK], dq2[BLOCK:])
            dk2[hk] = dk2[hk] + _mm_ta(dsm, qst)
            dv2[hk] = dv2[hk] + _mm_ta(pn, dost)
        ds_ref[...] += dsink
        for hk in range(N_KV_HEADS):
            for acc, carry, ref in ((dk2[hk], ck, dk_ref), (dv2[hk], cv, dv_ref)):
                folded = acc + pltpu.roll(acc, 64, 1)
                ref[hk] = folded[BLOCK:] + carry[hk]
                carry[hk] = folded[:BLOCK]

    prev = pl.BlockSpec((N_KV_HEADS, BLOCK, LANES), lambda n: (0, jnp.maximum(rev(n) - 1, 0), 0))
    own = pl.BlockSpec((N_KV_HEADS, BLOCK, LANES), lambda n: (0, rev(n), 0))
    rows = lambda cols: pl.BlockSpec((BLOCK, cols), lambda n: (rev(n), 0))
    heads = jax.ShapeDtypeStruct((N_KV_HEADS, t, LANES), F32)
    return pl.pallas_call(
        body, name=name, grid=(nb,),
        out_shape=(jax.ShapeDtypeStruct((t, d_model), F32), heads, heads, jax.ShapeDtypeStruct((1, LANES), F32)),
        in_specs=[pl.BlockSpec(memory_space=pltpu.SMEM), rows(d_model), prev, own, prev, own, rows(LANES), rows(d_model)],
        out_specs=(rows(d_model), own, own, _const((1, LANES))),
        scratch_shapes=[pltpu.VMEM((N_KV_HEADS, BLOCK, LANES), F32), pltpu.VMEM((N_KV_HEADS, BLOCK, LANES), F32)],
        compiler_params=_params(),
    )(sinks, qs, kdup, kdup, vdup, vdup, lse, do)


def _oproj_fwd(x, o, w_o, post_g, layer, j, name):
    t, d_model = x.shape
    tm = _row_tile(t, PROJ_TILE)

    def body(x_ref, o_ref, w_ref, g_ref, xo_ref, mo_ref):
        mo = _mm(o_ref[...], w_ref[...])
        mo_ref[...] = mo
        xo_ref[...] = x_ref[...] + _rms_fwd(mo, g_ref[...])

    full = jax.ShapeDtypeStruct((t, d_model), F32)
    return pl.pallas_call(
        body, name=name, grid=(t // tm,), out_shape=(full, full),
        in_specs=[_rows(tm, d_model), _rows(tm, d_model), _const((None, d_model, d_model), (j, 0, 0)),
                  _const((None, 1, d_model), (layer, 0, 0))],
        out_specs=(_rows(tm, d_model), _rows(tm, d_model)), compiler_params=_params(),
    )(x, o, w_o, post_g)


def _oproj_bwd(dx, mo, w_o, post_g, layer, j, name):
    t, d_model = dx.shape
    tm = _row_tile(t, PROJ_TILE)

    def body(dx_ref, mo_ref, w_ref, g_ref, do_ref, dmo_ref, dg_ref):
        i = pl.program_id(0)

        @pl.when(i == 0)
        def _():
            dg_ref[...] = jnp.zeros_like(dg_ref)

        dmo, dg = _rms_bwd(mo_ref[...], g_ref[...], dx_ref[...])
        dg_ref[...] += dg
        dmo = dmo.astype(_MXU_DTYPE)
        dmo_ref[...] = dmo
        do_ref[...] = _mm_tb(dmo, w_ref[...]).astype(do_ref.dtype)

    act = jax.ShapeDtypeStruct((t, d_model), _ACT_DTYPE)
    return pl.pallas_call(
        body, name=name, grid=(t // tm,), out_shape=(act, act, jax.ShapeDtypeStruct((1, d_model), F32)),
        in_specs=[_rows(tm, d_model), _rows(tm, d_model), _const((None, d_model, d_model), (j, 0, 0)),
                  _const((None, 1, d_model), (layer, 0, 0))],
        out_specs=(_rows(tm, d_model), _rows(tm, d_model), _const((1, d_model))), compiler_params=_params(),
    )(dx, mo, w_o, post_g)


def _loss_grad(y, target, name):
    t, d_model = y.shape
    tm = _row_tile(t, PROJ_TILE)

    def body(y_ref, t_ref, dy_ref, loss_ref):
        i = pl.program_id(0)

        @pl.when(i == 0)
        def _():
            loss_ref[...] = jnp.zeros_like(loss_ref)

        err = y_ref[...] - t_ref[...]
        dy_ref[...] = err / d_model
        loss_ref[...] += 0.5 * jnp.sum(jnp.mean(err * err, axis=-1, keepdims=True), axis=0, keepdims=True)

    return pl.pallas_call(
        body, name=name, grid=(t // tm,),
        out_shape=(jax.ShapeDtypeStruct((t, d_model), F32), jax.ShapeDtypeStruct((1, 1), F32)),
        in_specs=[_rows(tm, d_model), _rows(tm, d_model)], out_specs=(_rows(tm, d_model), _const((1, 1))),
        compiler_params=_params(),
    )(y, target)


def _mesh_position():
    return lax.axis_index("x"), lax.axis_index("y"), lax.axis_index("c")


def _block_of(px, py, pc):
    return 4 * px + 2 * py + pc


def _at_block(ref, axis, block):
    return ref.at[(slice(None),) * axis + (block,)]


def _all_gather(shards, axes, name, after=None):
    n = len(shards)
    behind = [] if after is None else [after]

    def body(*refs):
        srcs, outs = refs[:n], refs[n + len(behind):2 * n + len(behind)]
        send_sems, recv_sems, local_sems = refs[2 * n + len(behind):]
        x, y, c = _mesh_position()
        me, sibling = (x, y, c), (x, y, 1 - c)
        chips = [(1 - x, y), (x, 1 - y), (1 - x, 1 - y)]

        def blk(i, pos):
            return _at_block(outs[i], axes[i], _block_of(*pos))

        def copy(i, k, block, to, src=None):
            return pltpu.make_async_remote_copy(
                src_ref=blk(i, block) if src is None else src, dst_ref=blk(i, block),
                send_sem=send_sems.at[i, k], recv_sem=recv_sems.at[i, k], device_id=to, device_id_type=MESH)

        mine = [pltpu.make_async_copy(srcs[i], blk(i, me), local_sems.at[i]) for i in range(n)]
        for cp in mine:
            cp.start()
        sent = []
        for i in range(n):
            sent += [copy(i, 1 + k, me, (*chip, c), src=srcs[i]) for k, chip in enumerate(chips)]
            sent.append(copy(i, 0, me, sibling, src=srcs[i]))
        for cp in sent:
            cp.start()
        for i in range(n):
            for k, chip in enumerate(chips):
                copy(i, 1 + k, (*chip, c), me).wait_recv()
                passed = copy(i, 4 + k, (*chip, c), sibling)
                passed.start()
                sent.append(passed)
        for i in range(n):
            copy(i, 0, sibling, me).wait_recv()
            for k, chip in enumerate(chips):
                copy(i, 4 + k, (*chip, 1 - c), me).wait_recv()
        for cp in sent:
            cp.wait_send()
        for cp in mine:
            cp.wait()

    hbm = pl.BlockSpec(memory_space=pl.ANY)
    return pl.pallas_call(
        body, name=name,
        out_shape=tuple(jax.ShapeDtypeStruct(s.shape[:a] + (N_DEV,) + s.shape[a:], s.dtype) for s, a in zip(shards, axes)),
        in_specs=[hbm] * (n + len(behind)), out_specs=(hbm,) * n,
        scratch_shapes=[pltpu.SemaphoreType.DMA((n, 7)), pltpu.SemaphoreType.DMA((n, 7)), pltpu.SemaphoreType.DMA((n,))],
    )(*shards, *behind)


GATHER, SCATTER = "gather", "scatter"
N_PEERS = N_DEV - 1


def _land_shape(kind, s, axis):
    if kind == GATHER:
        return s.shape[:axis] + (N_DEV,) + s.shape[axis:]
    return (N_DEV,) + s.shape[:axis] + s.shape[axis + 1:]


def _direct_plan(kind, srcs, lands, axes):
    x, y, c = _mesh_position()
    my_block = _block_of(x, y, c)
    peers = []
    for k in range(1, N_DEV):
        fx, fy, fc = (k >> 2) & 1, (k >> 1) & 1, k & 1
        peers.append((1 - x if fx else x, 1 - y if fy else y, 1 - c if fc else c))
    remote = []
    for src, land, axis in zip(srcs, lands, axes):
        if kind == GATHER:
            mine = _at_block(land, axis, my_block)
            remote += [(src, mine, peer, _at_block(land, axis, _block_of(*peer))) for peer in peers]
        else:
            mine = land.at[my_block]
            remote += [(_at_block(src, axis, _block_of(*peer)), mine, peer, land.at[_block_of(*peer)]) for peer in peers]
    return remote


def _remote(src, dst, send_sems, recv_sems, k, peer):
    return pltpu.make_async_remote_copy(src_ref=src, dst_ref=dst, send_sem=send_sems.at[k], recv_sem=recv_sems.at[k],
                                        device_id=peer, device_id_type=MESH)


_HBM = pl.BlockSpec(memory_space=pltpu.HBM)
_SEM = pl.BlockSpec(memory_space=pltpu.SEMAPHORE)
_SPLIT = dict(has_side_effects=pltpu.SideEffectType.DATAFLOW_SIDE_EFFECTING)


def _landing_zone(kind, s, axis, me):
    land = lax.empty(_land_shape(kind, s, axis), s.dtype)
    if kind == GATHER:
        return lax.dynamic_update_slice_in_dim(land, jnp.expand_dims(s, axis), me, axis)
    return lax.dynamic_update_slice_in_dim(land, lax.dynamic_slice_in_dim(s, me, 1, axis).reshape((1,) + land.shape[1:]), me, 0)


def _exchange_start(kind, arrays, axes, after, name):
    n = len(arrays)
    me = _block_of(*_mesh_position())
    lands = [_landing_zone(kind, s, a, me) for s, a in zip(arrays, axes)]

    def body(*refs):
        srcs, land_refs = refs[:n], refs[n:2 * n]
        send_sems, recv_sems = refs[2 * n + 1], refs[2 * n + 2]
        token = refs[-1]
        for k, (src, dst, peer, _) in enumerate(_direct_plan(kind, srcs, land_refs, axes)):
            _remote(src, dst, send_sems, recv_sems, k, peer).start()
        token[...] = jnp.zeros_like(token)

    both = list(arrays) + lands
    out = pl.pallas_call(
        body, name=name,
        out_shape=(pltpu.SemaphoreType.DMA((n * N_PEERS,)), pltpu.SemaphoreType.DMA((n * N_PEERS,)),
                   *[pltpu.HBM(a.shape, a.dtype) for a in both], jax.ShapeDtypeStruct((8, LANES), F32)),
        in_specs=[_HBM] * (2 * n) + [pl.BlockSpec(memory_space=pl.ANY)],
        out_specs=(_SEM, _SEM, *[_HBM] * (2 * n), pl.BlockSpec(memory_space=pltpu.VMEM)),
        input_output_aliases={i: 2 + i for i in range(2 * n)},
        compiler_params=pltpu.CompilerParams(**_SPLIT),
    )(*[pltpu.with_memory_space_constraint(a, pltpu.HBM) for a in both], after)
    return (kind, axes, n, out[:-1]), out[-1]


def _exchange_wait(handle, after, name):
    kind, axes, n, (send_sems, recv_sems, *thru) = handle

    def body(*refs):
        srcs, land_refs = refs[:n], refs[n:2 * n]
        send_sems, recv_sems = refs[2 * n], refs[2 * n + 1]
        for k, (src, _, peer, arrives) in enumerate(_direct_plan(kind, srcs, land_refs, axes)):
            cp = _remote(src, arrives, send_sems, recv_sems, k, peer)
            cp.wait_send()
            cp.wait_recv()

    out = pl.pallas_call(
        body, name=name,
        out_shape=tuple(pltpu.HBM(a.shape, a.dtype) for a in thru),
        in_specs=[_HBM] * (2 * n) + [_SEM, _SEM, pl.BlockSpec(memory_space=pl.ANY)], out_specs=(_HBM,) * (2 * n),
        input_output_aliases={i: i for i in range(2 * n)},
        compiler_params=pltpu.CompilerParams(**_SPLIT),
    )(*thru, send_sems, recv_sems, after)
    return out[n:]


def _adamw_math(w, g, m, v):
    m = ADAM_B1 * m + (1.0 - ADAM_B1) * g
    v = ADAM_B2 * v + (1.0 - ADAM_B2) * jnp.square(g)
    m_hat = m / (1.0 - ADAM_B1 ** ADAM_STEP)
    v_hat = v / (1.0 - ADAM_B2 ** ADAM_STEP)
    delta = -ADAM_LR * (m_hat / (jnp.sqrt(v_hat) + ADAM_EPS) + ADAM_WD * w)
    return delta, m, v


def _update_tile(rows):
    if rows <= 512:
        return rows
    for tr in (512, 384, 352, 256, 176, 128, 64, 32, 16):
        if rows % tr == 0:
            return tr
    raise ValueError(f"{rows} rows do not tile")


def _adamw(parts, w, m, v, slab, so_far, name):
    rows, c = w.shape
    r = parts.shape[1]
    tr = _update_tile(r)
    first = slab * (r // tr)
    if so_far is None:
        so_far = tuple(lax.empty((rows, c), F32) for _ in range(4))

    def body(p_ref, w_ref, m_ref, v_ref, *refs):
        g_ref, d_ref, mo_ref, vo_ref = refs[4:]
        g = p_ref[0].astype(F32)
        for s in range(1, N_DEV):
            g = g + p_ref[s].astype(F32)
        g_ref[...] = g
        d_ref[...], mo_ref[...], vo_ref[...] = _adamw_math(w_ref[...], g, m_ref[...], v_ref[...])

    out = jax.ShapeDtypeStruct((rows, c), F32)
    tile = pl.BlockSpec((tr, c), lambda i: (first + i, 0))
    return pl.pallas_call(
        body, name=name, grid=(r // tr,), out_shape=(out,) * 4,
        in_specs=[pl.BlockSpec((N_DEV, tr, c), lambda i: (0, i, 0))] + [tile] * 3 + [pl.BlockSpec(memory_space=pl.ANY)] * 4,
        out_specs=(tile,) * 4, input_output_aliases={4 + k: k for k in range(4)}, compiler_params=_params(),
    )(parts, w, m, v, *so_far)


def _adamw_small(parts, picks, weights, name):
    n = len(parts)

    def body(*refs):
        p_refs, wmv, outs = refs[:n], refs[n:4 * n], refs[4 * n:]
        me = _block_of(*_mesh_position())
        for i in range(n):
            g = picks[i](p_refs[i], 0, me)
            for s in range(1, N_DEV):
                g = g + picks[i](p_refs[i], s, me)
            w_ref, m_ref, v_ref = wmv[3 * i:3 * i + 3]
            g_ref, d_ref, mo_ref, vo_ref = outs[4 * i:4 * i + 4]
            g_ref[...] = g
            d_ref[...], mo_ref[...], vo_ref[...] = _adamw_math(w_ref[...], g, m_ref[...], v_ref[...])

    flat = [a for wmv in weights for a in wmv]
    out = pl.pallas_call(
        body, name=name,
        out_shape=tuple(jax.ShapeDtypeStruct(w.shape, F32) for w, _, _ in weights for _ in range(4)),
        compiler_params=pltpu.CompilerParams(vmem_limit_bytes=VMEM_LIMIT),
    )(*parts, *flat)
    return [tuple(out[4 * i:4 * i + 4]) for i in range(n)]


def kernel(x, positions, mix_pre_g, mix_post_g, pool_w, pool_scale, kv_norm_g, w_kv, w_q, w_o, sinks, ffn_pre_g, ffn_post_g, ffn_w_in, ffn_conv_w, ffn_conv_b, ffn_w_out, loss_target, m_mix_pre_g, m_mix_post_g, m_pool_w, m_pool_scale, m_kv_norm_g, m_w_kv, m_w_q, m_w_o, m_sinks, m_ffn_pre_g, m_ffn_post_g, m_ffn_w_in, m_ffn_conv_w, m_ffn_conv_b, m_ffn_w_out, v_mix_pre_g, v_mix_post_g, v_pool_w, v_pool_scale, v_kv_norm_g, v_w_kv, v_w_q, v_w_o, v_sinks, v_ffn_pre_g, v_ffn_post_g, v_ffn_w_in, v_ffn_conv_w, v_ffn_conv_b, v_ffn_w_out):
    depth, d_model = mix_pre_g.shape
    n_a = pool_w.shape[0]
    n_b = w_q.shape[0]
    t = x.shape[1]
    fs = ffn_w_in.shape[2]
    half = N_DEV // 2
    n_heads = d_model // HEAD_DIM
    x0 = x.reshape(t, d_model)
    target = loss_target.reshape(t, d_model)

    inv_freq = 1.0 / (ROPE_THETA ** (jnp.arange(0, HEAD_DIM, 2, dtype=F32) / HEAD_DIM))
    ang = positions.reshape(t).astype(F32)[:, None] * inv_freq
    cos, sin = jnp.cos(ang), jnp.sin(ang)
    cos = jnp.tile(cos, (1, 2 * LANES // HEAD_DIM))
    ssin = jnp.tile(jnp.concatenate([-sin, sin], axis=1), (1, LANES // HEAD_DIM))

    wire = lambda a: a.astype(_WIRE_DTYPE)
    w_in_b, w_out_b = wire(ffn_w_in), wire(ffn_w_out)
    pool_w_g, w_in_0, w_out_0, pool_scale_g, conv_w_g = _all_gather(
        [wire(pool_w), w_in_b[:1], w_out_b[:1], pool_scale, ffn_conv_w], [2, 1, 1, 0, 0], "gather_first")
    pending, tokens, after = {}, [], w_in_0
    for l in range(1, depth):
        if l == n_a:
            pending["attn"], after = _exchange_start(GATHER, [wire(w_kv), wire(w_q), wire(w_o)], [0, 1, 1], after, "gather_start_attn")
            tokens.append(after)
        pending[l], after = _exchange_start(GATHER, [w_in_b[l:l + 1], w_out_b[l:l + 1]], [1, 1], after, f"gather_start_{l}")
        tokens.append(after)
    started = functools.reduce(lambda a, b: a + b, [tk[0, 0] for tk in tokens])
    w_in_l, w_out_l = {0: w_in_0}, {0: w_out_0.reshape(1, half * fs, d_model)}
    pool_scale_f = pool_scale_g.transpose(1, 0, 2).reshape(n_a, 1, d_model)
    conv_w_f = conv_w_g.transpose(1, 2, 0, 3)
    pool_w_f = pool_w_g.reshape(n_a, len(POOL_WINDOWS), d_model // len(POOL_WINDOWS), -1)
    conv_b_f = ffn_conv_b.reshape(depth, N_DEV, fs)
    g3 = lambda a: a.reshape(a.shape[0], 1, a.shape[1])
    mix_pre, mix_post, ffn_pre, ffn_post = g3(mix_pre_g) + started, g3(mix_post_g), g3(ffn_pre_g), g3(ffn_post_g)
    kv_g = kv_norm_g.reshape(1, d_model)
    w_kv_f = w_q_f = w_o_f = None

    saved = []
    xc = x0
    kdup = vdup = x_kv = None
    for l in range(depth):
        x_in = xc
        if l < n_a:
            x_mid, dsave, yu = _pool_fwd(x_in, mix_pre, mix_post, pool_w_f, pool_scale_f, l, f"pool_fwd_{l}")
            mixer = (dsave, yu)
        else:
            j = l - n_a
            if j == 0:
                x_kv = x_in
                w_kv_g, w_q_g, w_o_g = _exchange_wait(pending["attn"], x_in, "gather_wait_attn")
                w_kv_f = w_kv_g.reshape(d_model, -1)
                w_q_f = w_q_g.reshape(n_b, d_model, d_model)
                w_o_f = w_o_g.reshape(n_b, d_model, d_model)
                kdup, vdup = _kv_fwd(x_kv, kv_g, w_kv_f, cos, ssin, "kv_fwd")
            qs = _q_fwd(x_in, mix_pre, w_q_f, cos, ssin, l, j, f"q_fwd_{l}")
            o, lse = _attn_fwd(qs, kdup, vdup, sinks, j, f"attn_fwd_{l}")
            x_mid, mo = _oproj_fwd(x_in, o, w_o_f, mix_post, l, j, f"oproj_fwd_{l}")
            mixer = (qs, o, lse, mo)
        if l > 0:
            w_in_l[l], w_out_g = _exchange_wait(pending[l], x_mid, f"gather_wait_{l}")
            w_out_l[l] = w_out_g.reshape(1, half * fs, d_model)
        xc, u, uc, f = _ffn_fwd(x_mid, ffn_pre, ffn_post, w_in_l[l], conv_w_f, conv_b_f, w_out_l[l], l, 0, f"ffn_fwd_{l}")
        saved.append((x_in, x_mid, u, uc, f, mixer))

    dx, loss_part = _loss_grad(xc, target, "loss")
    loss = lax.psum(loss_part[0, 0], AXES)

    gconv_w, gconv_b = [None] * depth, [None] * depth
    gmix_pre, gmix_post, gffn_pre, gffn_post = [None] * depth, [None] * depth, [None] * depth, [None] * depth
    gpool_scale, gsinks = [None] * n_a, [None] * n_b
    dks, dvs = [], []
    gkv_g = None
    whole = lambda cols: pl.BlockSpec((t, cols), lambda c: (0, 0), pipeline_mode=pl.Buffered(1))
    per_out = lambda cols: pl.BlockSpec((None, t, cols), lambda c: (c, 0, 0))
    by_dev = lambda g: g.reshape(N_DEV, -1, g.shape[-1])
    flying = []

    def launch(going, after, name):
        handle, token = _exchange_start(SCATTER, [g for _, _, g, _ in going], [a for _, _, _, a in going], after, name)
        flying.append(([(nm, slab) for nm, slab, _, _ in going], handle))
        return token

    post = mix_post
    ffn_post_b = ffn_post
    token = None
    for l in reversed(range(depth)):
        x_in, x_mid, u, uc, f, mixer = saved[l]
        duc, gact, df, gconv_b[l], gffn_post[l] = _ffn_bwd_out(dx, f, ffn_post_b, uc, conv_b_f, w_out_l[l], l, 0, f"ffn_bwd_out_{l}")
        dx, du, hf, gconv_w[l], gffn_pre[l] = _ffn_bwd_in(dx, x_mid, ffn_pre, duc, u, conv_w_f, w_in_l[l], l, 0, f"ffn_bwd_in_{l}")
        gin = _tn_matmul(hf, du, whole(d_model), per_out(fs), N_DEV, d_model, fs, f"grad_w_in_{l}", _WIRE_DTYPE)
        going = [("ffn_w_in", l, gin, 0)]
        if l == 0:
            token = launch(going, dx, "scatter_start_0_in")
            going = []
            post = post + token[0, 0]
        gout = _tn_matmul(gact, df, per_out(fs), whole(d_model), half, fs, d_model, f"grad_w_out_{l}", _WIRE_DTYPE)
        going.append(("ffn_w_out", l, by_dev(gout), 0))
        if l < n_a:
            dsave, yu = mixer
            dx, dyu, gpool_scale[l], gmix_pre[l], gmix_post[l] = _pool_bwd(
                dx, x_in, mix_pre, post, dsave, yu, pool_w_f, pool_scale_f, l, f"pool_bwd_{l}")
            gc = d_model // len(POOL_WINDOWS)
            by_group = pl.BlockSpec((t, gc), lambda c: (0, c))
            gpool = _tn_matmul(dsave, dyu, by_group, by_group, len(POOL_WINDOWS), gc, gc, f"grad_pool_w_{l}", _WIRE_DTYPE)
            going.append(("pool_w", l, gpool.reshape(len(POOL_WINDOWS), N_DEV, -1, gc), 1))
        else:
            j = l - n_a
            qs, o, lse, mo = mixer
            do, dmo, gmix_post[l] = _oproj_bwd(dx, mo, w_o_f, post, l, j, f"oproj_bwd_{l}")
            dqs, dk, dv, gsinks[j] = _attn_bwd(qs, kdup, vdup, sinks, lse, do, j, f"attn_bwd_{l}")
            dks.append(dk)
            dvs.append(dv)
            dx, hq, dq, gmix_pre[l] = _q_bwd(dx, dqs, x_in, mix_pre, w_q_f, cos, ssin, l, j, f"q_bwd_{l}")
            if j == 0:
                dx, hkv, dkv, gkv_g = _kv_bwd(dx, x_kv, kv_g, w_kv_f, cos, ssin, dks, dvs, "kv_bwd")
                gkv = _tn_matmul(hkv, dkv, whole(d_model), whole(dkv.shape[1]), 1, d_model, dkv.shape[1], "grad_w_kv", _WIRE_DTYPE)
                going.append(("w_kv", 0, by_dev(gkv), 0))
            go = _tn_matmul(o, dmo, whole(d_model), whole(d_model), 1, d_model, d_model, f"grad_w_o_{l}", _WIRE_DTYPE)
            gq = _tn_matmul(hq, dq, whole(d_model), whole(d_model), 1, d_model, d_model, f"grad_w_q_{l}", _WIRE_DTYPE)
            going += [("w_o", j, by_dev(go), 0), ("w_q", j, by_dev(gq), 0)]
        token = launch(going, dx, f"scatter_start_{l}")
        ffn_post_b = ffn_post_b + token[0, 0]

    grad_x = dx.reshape(x.shape)

    shard = {"pool_w": (pool_w, m_pool_w, v_pool_w), "w_kv": (w_kv, m_w_kv, v_w_kv), "w_q": (w_q, m_w_q, v_w_q),
             "w_o": (w_o, m_w_o, v_w_o), "ffn_w_in": (ffn_w_in, m_ffn_w_in, v_ffn_w_in),
             "ffn_w_out": (ffn_w_out, m_ffn_w_out, v_ffn_w_out)}
    big = {}

    def arrive(flights, after):
        for idx, (names, handle) in flights:
            parts = _exchange_wait(handle, after, f"scatter_wait_{idx}")
            for (nm, slab), p in zip(names, parts):
                cols = p.shape[-1]
                w2, m2, v2 = (a.reshape(-1, cols) for a in shard[nm])
                big[nm] = _adamw(p.reshape(N_DEV, -1, cols), w2, m2, v2, slab, big.get(nm), f"adamw_{nm}_{slab}")
                after = big[nm][0]
        return after

    flights = list(enumerate(flying))
    done = arrive(flights[:depth - 1], token)

    cat = lambda rows: jnp.concatenate(rows, axis=0)
    row = lambda a: a.reshape(1, -1)
    everything = lambda ref, s, me: ref[s]
    lanes = pool_scale.shape[1]
    small = [("mix_pre_g", cat(gmix_pre), everything, (mix_pre_g, m_mix_pre_g, v_mix_pre_g)),
             ("mix_post_g", cat(gmix_post), everything, (mix_post_g, m_mix_post_g, v_mix_post_g)),
             ("kv_norm_g", gkv_g, everything, (row(kv_norm_g), row(m_kv_norm_g), row(v_kv_norm_g))),
             ("sinks", cat(gsinks), lambda ref, s, me: ref[s, :, pl.ds(0, n_heads)], (sinks, m_sinks, v_sinks)),
             ("ffn_pre_g", cat(gffn_pre), everything, (ffn_pre_g, m_ffn_pre_g, v_ffn_pre_g)),
             ("ffn_post_g", cat(gffn_post), everything, (ffn_post_g, m_ffn_post_g, v_ffn_post_g)),
             ("ffn_conv_b", jnp.stack(gconv_b).reshape(depth, N_DEV * fs), everything, (ffn_conv_b, m_ffn_conv_b, v_ffn_conv_b)),
             ("pool_scale", cat(gpool_scale), lambda ref, s, me: ref[s, :, pl.ds(pl.multiple_of(me * lanes, lanes), lanes)],
              (pool_scale, m_pool_scale, v_pool_scale)),
             ("ffn_conv_w", jnp.stack(gconv_w).transpose(0, 2, 1, 3), lambda ref, s, me: ref[s, :, me],
              (ffn_conv_w, m_ffn_conv_w, v_ffn_conv_w))]
    small_parts = _all_gather([g for _, g, _, _ in small], [0] * len(small), "gather_small_grads", after=done)
    upd = _adamw_small(small_parts, [pick for _, _, pick, _ in small], [wmv for _, _, _, wmv in small], "adamw_small")
    arrive(flights[depth - 1:], upd[0][0])
    res = {nm: tuple(r.reshape(shard[nm][0].shape) for r in out) for nm, out in big.items()}
    for (nm, _, _, _), out in zip(small, upd):
        res[nm] = tuple(a.reshape(kv_norm_g.shape) for a in out) if nm == "kv_norm_g" else out

    order = ["mix_pre_g", "mix_post_g", "pool_w", "pool_scale", "kv_norm_g", "w_kv", "w_q", "w_o", "sinks", "ffn_pre_g",
             "ffn_post_g", "ffn_w_in", "ffn_conv_w", "ffn_conv_b", "ffn_w_out"]
    return (loss, grad_x, *[res[nm][0] for nm in order], *[res[nm][1] for nm in order],
            *[res[nm][2] for nm in order], *[res[nm][3] for nm in order])
```

```python
import functools
import math

import jax
import jax.numpy as jnp
from jax import lax
from jax.experimental import pallas as pl
from jax.experimental.pallas import tpu as pltpu

F32 = jnp.float32
_MXU_DTYPE = jnp.bfloat16
_ACT_DTYPE = jnp.bfloat16
_WIRE_DTYPE = jnp.bfloat16
_SAVE_DTYPE = jnp.bfloat16

N_DEV = 8
POOL_WINDOWS = (2, 4, 8, 16)
POOL_HALO = 16
HEAD_DIM = 64
N_KV_HEADS = 4
WINDOW = 128
BLOCK = 128
LANES = 128
ROPE_THETA = 10000.0
ATTN_SCALE = 1.0 / math.sqrt(HEAD_DIM)
NEG_INF = -1e30
RMS_EPS = 1e-6
CONV_HALO = 8
SAVE_HALO = 16
PROJ_TILE = 512
ADAM_LR = 0.001
ADAM_B1 = 0.9
ADAM_B2 = 0.999
ADAM_EPS = 1e-08
ADAM_WD = 0.01
ADAM_STEP = 10
VMEM_LIMIT = 56 * 1024 * 1024
MESH = pl.DeviceIdType.MESH
AXES = ("x", "y", "c")


def _params(n_axes=1, vmem=VMEM_LIMIT):
    return pltpu.CompilerParams(dimension_semantics=("arbitrary",) * n_axes, vmem_limit_bytes=vmem)


def _resident(shape, index):
    return pl.BlockSpec(shape, lambda *_: index, pipeline_mode=pl.Buffered(1))


def _const(shape, index=None):
    index = (0,) * len(shape) if index is None else index
    return pl.BlockSpec(shape, lambda *_: index)


def _rows(tm, cols):
    return pl.BlockSpec((tm, cols), lambda i: (i, 0))


def _row_tile(t, most=256):
    for tm in (512, 256, 128, 64, 32, 16, 8):
        if tm <= most and t % tm == 0:
            return tm
    raise ValueError(f"sequence length {t} is not a multiple of 8")


def _mm(a, b):
    return jnp.dot(a.astype(_MXU_DTYPE), b.astype(_MXU_DTYPE), preferred_element_type=F32)


def _mm_tb(a, b):
    return lax.dot_general(a.astype(_MXU_DTYPE), b.astype(_MXU_DTYPE), (((1,), (1,)), ((), ())),
                           preferred_element_type=F32)


def _mm_ta(a, b):
    return lax.dot_general(a.astype(_MXU_DTYPE), b.astype(_MXU_DTYPE), (((0,), (0,)), ((), ())),
                           preferred_element_type=F32)


def _rms_r(x):
    return lax.rsqrt(jnp.mean(x * x, axis=-1, keepdims=True) + RMS_EPS)


def _rms_fwd(x, g):
    return (x * _rms_r(x)) * g


def _rms_bwd(x, g, dy):
    r = _rms_r(x)
    xh = x * r
    dg = jnp.sum(dy * xh, axis=0, keepdims=True)
    dxh = dy * g
    dx = r * (dxh - xh * jnp.mean(dxh * xh, axis=-1, keepdims=True))
    return dx, dg


_GELU_C = math.sqrt(2.0 / math.pi)


def _gelu_parts(z):
    z2 = z * z
    t = jnp.tanh(_GELU_C * (z + 0.044715 * (z2 * z)))
    cdf = 0.5 * (1.0 + t)
    dz = cdf + z * (0.5 * (1.0 - t * t)) * (_GELU_C * (1.0 + (3 * 0.044715) * z2))
    return cdf, dz


def _lane_iota(shape):
    return lax.broadcasted_iota(jnp.int32, shape, len(shape) - 1)


def _rope_partner(xb):
    first = (_lane_iota(xb.shape) & 32) == 0
    return jnp.where(first, pltpu.roll(xb, LANES - 32, 1), pltpu.roll(xb, 32, 1))


def _rope_fwd(xb, cos, ssin):
    return xb * cos + _rope_partner(xb) * ssin


def _rope_bwd(dyb, cos, ssin):
    return dyb * cos - _rope_partner(dyb) * ssin


def _low_half(shape):
    return (_lane_iota(shape) & 64) == 0


def _pool_fwd(x, pre_g, post_g, w, scale, layer, name):
    t, d_model = x.shape
    tm = _row_tile(t)
    n_groups, gc = w.shape[1], w.shape[2]

    def body(x_ref, pre_ref, post_ref, w_ref, sc_ref, xo_ref, d_ref, yu_ref, hbuf):
        i = pl.program_id(0)

        @pl.when(i == 0)
        def _():
            hbuf[pl.ds(0, POOL_HALO), :] = jnp.zeros((POOL_HALO, d_model), F32)

        xv = x_ref[...]
        hbuf[pl.ds(POOL_HALO, tm), :] = _rms_fwd(xv, pre_ref[...])
        tok = i * tm + lax.broadcasted_iota(jnp.int32, (tm, 1), 0)
        yus = []
        for gi, wnd in enumerate(POOL_WINDOWS):
            cols = pl.ds(gi * gc, gc)
            h = hbuf[pl.ds(POOL_HALO, tm), cols]
            acc = h
            for k in range(1, wnd):
                acc = acc + hbuf[pl.ds(POOL_HALO - k, tm), cols]
            cnt = jnp.minimum(tok + 1, wnd).astype(F32)
            dg = acc / cnt - h
            d_ref[:, cols] = dg.astype(d_ref.dtype)
            yus.append(_mm(dg, w_ref[gi]))
        hbuf[pl.ds(0, POOL_HALO), :] = hbuf[pl.ds(tm, POOL_HALO), :]
        yu = jnp.concatenate(yus, axis=1)
        yu_ref[...] = yu
        xo_ref[...] = xv + _rms_fwd(yu * sc_ref[...], post_ref[...])

    return pl.pallas_call(
        body, name=name, grid=(t // tm,),
        out_shape=(jax.ShapeDtypeStruct((t, d_model), F32), jax.ShapeDtypeStruct((t, d_model), _ACT_DTYPE),
                   jax.ShapeDtypeStruct((t, d_model), F32)),
        in_specs=[_rows(tm, d_model), _const((None, 1, d_model), (layer, 0, 0)), _const((None, 1, d_model), (layer, 0, 0)),
                  _const((None, n_groups, gc, gc), (layer, 0, 0, 0)), _const((None, 1, d_model), (layer, 0, 0))],
        out_specs=(_rows(tm, d_model), _rows(tm, d_model), _rows(tm, d_model)),
        scratch_shapes=[pltpu.VMEM((POOL_HALO + tm, d_model), F32)],
        compiler_params=_params(),
    )(x, pre_g, post_g, w, scale)


def _pool_bwd(dx, x, pre_g, post_g, d, yu, w, scale, layer, name):
    t, d_model = x.shape
    tm = _row_tile(t)
    nt = t // tm
    n_groups, gc = w.shape[1], w.shape[2]
    rev = lambda i: (nt - 1 - i, 0)
    rows = pl.BlockSpec((tm, d_model), rev)

    def body(dx_ref, x_ref, pre_ref, post_ref, d_ref, yu_ref, w_ref, sc_ref,
             dxi_ref, dyu_ref, dsc_ref, dpre_ref, dpost_ref, zbuf):
        i = pl.program_id(0)

        @pl.when(i == 0)
        def _():
            zbuf[pl.ds(tm, POOL_HALO), :] = jnp.zeros((POOL_HALO, d_model), F32)
            dsc_ref[...] = jnp.zeros_like(dsc_ref)
            dpre_ref[...] = jnp.zeros_like(dpre_ref)
            dpost_ref[...] = jnp.zeros_like(dpost_ref)

        dxo = dx_ref[...]
        yuv = yu_ref[...]
        sc = sc_ref[...]
        dm, dpost = _rms_bwd(yuv * sc, post_ref[...], dxo)
        dpost_ref[...] += dpost
        dsc_ref[...] += jnp.sum(dm * yuv, axis=0, keepdims=True)
        dyu = dm * sc
        dyu_ref[...] = dyu.astype(dyu_ref.dtype)
        tok = (nt - 1 - i) * tm + lax.broadcasted_iota(jnp.int32, (tm, 1), 0)
        dds = []
        for gi, wnd in enumerate(POOL_WINDOWS):
            cols = pl.ds(gi * gc, gc)
            dd = _mm_tb(dyu[:, gi * gc:(gi + 1) * gc], w_ref[gi])
            cnt = jnp.minimum(tok + 1, wnd).astype(F32)
            zbuf[pl.ds(0, tm), cols] = dd / cnt
            dds.append(dd)
        dhs = []
        for gi, wnd in enumerate(POOL_WINDOWS):
            cols = pl.ds(gi * gc, gc)
            acc = zbuf[pl.ds(0, tm), cols]
            for k in range(1, wnd):
                acc = acc + zbuf[pl.ds(k, tm), cols]
            dhs.append(acc - dds[gi])
        zbuf[pl.ds(tm, POOL_HALO), :] = zbuf[pl.ds(0, POOL_HALO), :]
        dh = jnp.concatenate(dhs, axis=1)
        dxp, dpre = _rms_bwd(x_ref[...], pre_ref[...], dh)
        dpre_ref[...] += dpre
        dxi_ref[...] = dxo + dxp

    vec = jax.ShapeDtypeStruct((1, d_model), F32)
    return pl.pallas_call(
        body, name=name, grid=(nt,),
        out_shape=(jax.ShapeDtypeStruct((t, d_model), F32), jax.ShapeDtypeStruct((t, d_model), _ACT_DTYPE), vec, vec, vec),
        in_specs=[rows, rows, _const((None, 1, d_model), (layer, 0, 0)), _const((None, 1, d_model), (layer, 0, 0)), rows, rows,
                  _const((None, n_groups, gc, gc), (layer, 0, 0, 0)), _const((None, 1, d_model), (layer, 0, 0))],
        out_specs=(rows, rows, _const((1, d_model)), _const((1, d_model)), _const((1, d_model))),
        scratch_shapes=[pltpu.VMEM((tm + POOL_HALO, d_model), F32)],
        compiler_params=_params(),
    )(dx, x, pre_g, post_g, d, yu, w, scale)


def _conv_taps(cw_ref, s):
    return [cw_ref[k, pl.ds(s, 1), :] for k in range(3)]


def _shift_down(v, k, before):
    rolled = pltpu.roll(v, k, 0)
    row = lax.broadcasted_iota(jnp.int32, before.shape, 0)
    head = jnp.where(row < k, pltpu.roll(before, k, 0), rolled[:CONV_HALO])
    return jnp.concatenate([head, rolled[CONV_HALO:]], axis=0)


def _shift_up(v, k, after):
    rows = v.shape[0]
    rolled = pltpu.roll(v, rows - k, 0)
    row = lax.broadcasted_iota(jnp.int32, after.shape, 0)
    tail = jnp.where(row >= CONV_HALO - k, pltpu.roll(after, CONV_HALO - k, 0), rolled[rows - CONV_HALO:])
    return jnp.concatenate([rolled[:rows - CONV_HALO], tail], axis=0)


def _ffn_fwd(x, pre_g, post_g, w_in, conv_w, conv_b, w_out, layer, w_layer, name):
    t, d_model = x.shape
    tm = _row_tile(t)
    fs = w_in.shape[3]
    half = N_DEV // 2

    def body(x_ref, pre_ref, post_ref, win_ref, cw_ref, cb_ref, wout_ref, xo_ref, u_ref, uc_ref, f_ref, carry):
        i = pl.program_id(0)

        @pl.when(i == 0)
        def _():
            carry[...] = jnp.zeros_like(carry)

        xv = x_ref[...]
        hf = _rms_fwd(xv, pre_ref[...]).astype(_MXU_DTYPE)
        f = jnp.zeros((tm, d_model), F32)
        project = lambda b: [jnp.dot(hf, win_ref[s], preferred_element_type=F32) for s in (b, b + half)]
        ahead = project(0)
        for b in range(half):
            us, ahead = ahead, project(b + 1) if b + 1 < half else None
            ucs = []
            for s, u in zip((b, b + half), us):
                u_ref[s] = u.astype(u_ref.dtype)
                before = carry[s]
                carry[s] = u[tm - CONV_HALO:]
                w0, w1, w2 = _conv_taps(cw_ref, s)
                uc = ((w0 * _shift_down(u, 2, before) + w1 * _shift_down(u, 1, before)) + w2 * u) + cb_ref[pl.ds(s, 1), :]
                uc_ref[s] = uc.astype(uc_ref.dtype)
                ucs.append(uc)
            gate, val = ucs
            cdf, _ = _gelu_parts(gate)
            f = f + _mm((gate * cdf) * val, wout_ref[pl.ds(b * fs, fs), :])
        f_ref[...] = f
        xo_ref[...] = xv + _rms_fwd(f, post_ref[...])

    tile3 = pl.BlockSpec((N_DEV, tm, fs), lambda i: (0, i, 0))
    saved = jax.ShapeDtypeStruct((N_DEV, t, fs), _SAVE_DTYPE)
    return pl.pallas_call(
        body, name=name, grid=(t // tm,),
        out_shape=(jax.ShapeDtypeStruct((t, d_model), F32), saved, saved, jax.ShapeDtypeStruct((t, d_model), F32)),
        in_specs=[_rows(tm, d_model), _const((None, 1, d_model), (layer, 0, 0)), _const((None, 1, d_model), (layer, 0, 0)),
                  _resident((None, N_DEV, d_model, fs), (w_layer, 0, 0, 0)), _const((None, 3, N_DEV, fs), (layer, 0, 0, 0)),
                  _const((None, N_DEV, fs), (layer, 0, 0)), _resident((None, half * fs, d_model), (w_layer, 0, 0))],
        out_specs=(_rows(tm, d_model), tile3, tile3, _rows(tm, d_model)),
        scratch_shapes=[pltpu.VMEM((N_DEV, CONV_HALO, fs), F32)],
        compiler_params=_params(),
    )(x, pre_g, post_g, w_in, conv_w, conv_b, w_out)


def _ffn_bwd_out(dx, f, post_g, uc, conv_b, w_out, layer, w_layer, name):
    t, d_model = dx.shape
    tm = _row_tile(t)
    fs = uc.shape[2]
    half = N_DEV // 2

    def body(dx_ref, f_ref, post_ref, uc_ref, wout_ref, duc_ref, g_ref, df_ref, dcb_ref, dpost_ref):
        i = pl.program_id(0)

        @pl.when(i == 0)
        def _():
            dcb_ref[...] = jnp.zeros_like(dcb_ref)
            dpost_ref[...] = jnp.zeros_like(dpost_ref)

        df, dpost = _rms_bwd(f_ref[...], post_ref[...], dx_ref[...])
        dpost_ref[...] += dpost
        dfm = df.astype(_MXU_DTYPE)
        df_ref[...] = dfm
        project = lambda b: _mm_tb(dfm, wout_ref[pl.ds(b * fs, fs), :])
        ahead = project(0)
        for b in range(half):
            dg, ahead = ahead, project(b + 1) if b + 1 < half else None
            gate = uc_ref[b].astype(F32)
            val = uc_ref[b + half].astype(F32)
            cdf, dgelu = _gelu_parts(gate)
            ge = gate * cdf
            g_ref[b] = (ge * val).astype(g_ref.dtype)
            for s, dd in ((b, dg * val * dgelu), (b + half, dg * ge)):
                duc_ref[s] = dd.astype(duc_ref.dtype)
                dcb_ref[pl.ds(s, 1), :] += jnp.sum(dd, axis=0, keepdims=True)

    tile3 = pl.BlockSpec((N_DEV, tm, fs), lambda i: (0, i, 0))
    return pl.pallas_call(
        body, name=name, grid=(t // tm,),
        out_shape=(jax.ShapeDtypeStruct((N_DEV, t, fs), _SAVE_DTYPE), jax.ShapeDtypeStruct((half, t, fs), _ACT_DTYPE),
                   jax.ShapeDtypeStruct((t, d_model), _ACT_DTYPE), jax.ShapeDtypeStruct((N_DEV, fs), F32),
                   jax.ShapeDtypeStruct((1, d_model), F32)),
        in_specs=[_rows(tm, d_model), _rows(tm, d_model), _const((None, 1, d_model), (layer, 0, 0)), tile3,
                  _resident((None, half * fs, d_model), (w_layer, 0, 0))],
        out_specs=(tile3, pl.BlockSpec((half, tm, fs), lambda i: (0, i, 0)), _rows(tm, d_model),
                   _const((N_DEV, fs)), _const((1, d_model))),
        compiler_params=_params(),
    )(dx, f, post_g, uc, w_out)


def _ffn_bwd_in(dx, x, pre_g, duc, u, conv_w, w_in, layer, w_layer, name):
    t, d_model = dx.shape
    tm = _row_tile(t)
    nt = t // tm
    fs = duc.shape[2]
    hb = SAVE_HALO
    per_tile = tm // hb

    def body(dx_ref, x_ref, pre_ref, duc_ref, dn_ref, u_ref, cw_ref, win_ref, dxi_ref, du_ref, hf_ref, dcw_ref, dpre_ref):
        i = pl.program_id(0)

        @pl.when(i == 0)
        def _():
            dcw_ref[...] = jnp.zeros_like(dcw_ref)
            dpre_ref[...] = jnp.zeros_like(dpre_ref)

        xv = x_ref[...]
        pre = pre_ref[...]
        hf_ref[...] = _rms_fwd(xv, pre).astype(hf_ref.dtype)
        dhf = jnp.zeros((tm, d_model), F32)
        for s in range(N_DEV):
            d0 = duc_ref[s].astype(F32)
            after = jnp.where(i == nt - 1, 0.0, dn_ref[s].astype(F32)[:CONV_HALO])
            d1 = _shift_up(d0, 1, after)
            d2 = _shift_up(d0, 2, after)
            uv = u_ref[s].astype(F32)
            for k, dk in ((2, d0), (1, d1), (0, d2)):
                dcw_ref[k, pl.ds(s, 1), :] += jnp.sum(dk * uv, axis=0, keepdims=True)
            w0, w1, w2 = _conv_taps(cw_ref, s)
            du = (w2 * d0 + w1 * d1 + w0 * d2).astype(_MXU_DTYPE)
            du_ref[s] = du
            dhf = dhf + _mm_tb(du, win_ref[s])
        dxp, dpre = _rms_bwd(xv, pre, dhf)
        dpre_ref[...] += dpre
        dxi_ref[...] = dx_ref[...] + dxp

    tile3 = pl.BlockSpec((N_DEV, tm, fs), lambda i: (0, i, 0))
    return pl.pallas_call(
        body, name=name, grid=(nt,),
        out_shape=(jax.ShapeDtypeStruct((t, d_model), F32), jax.ShapeDtypeStruct((N_DEV, t, fs), _ACT_DTYPE),
                   jax.ShapeDtypeStruct((t, d_model), _ACT_DTYPE), jax.ShapeDtypeStruct((3, N_DEV, fs), F32),
                   jax.ShapeDtypeStruct((1, d_model), F32)),
        in_specs=[_rows(tm, d_model), _rows(tm, d_model), _const((None, 1, d_model), (layer, 0, 0)), tile3,
                  pl.BlockSpec((N_DEV, hb, fs), lambda i: (0, jnp.minimum((i + 1) * per_tile, t // hb - 1), 0)), tile3,
                  _const((None, 3, N_DEV, fs), (layer, 0, 0, 0)), _resident((None, N_DEV, d_model, fs), (w_layer, 0, 0, 0))],
        out_specs=(_rows(tm, d_model), tile3, _rows(tm, d_model), _const((3, N_DEV, fs)), _const((1, d_model))),
        compiler_params=_params(),
    )(dx, x, pre_g, duc, duc, u, conv_w, w_in)


def _tn_matmul(a, b, a_spec, b_spec, n_out, m, n, name, out_dtype):
    def body(a_ref, b_ref, o_ref):
        o_ref[...] = _mm_ta(a_ref[...], b_ref[...]).astype(o_ref.dtype)

    return pl.pallas_call(
        body, name=name, grid=(n_out,),
        out_shape=jax.ShapeDtypeStruct((n_out, m, n), out_dtype),
        in_specs=[a_spec, b_spec],
        out_specs=pl.BlockSpec((None, m, n), lambda c: (c, 0, 0)),
        compiler_params=_params(),
    )(a, b)


def _kv_fwd(x, kv_g, w_kv, cos, ssin, name):
    t, d_model = x.shape
    tm = _row_tile(t, PROJ_TILE)
    kvd = w_kv.shape[1] // 2
    pairs = kvd // LANES

    def body(x_ref, g_ref, w_ref, cos_ref, sin_ref, k_ref, v_ref):
        kv = _mm(_rms_fwd(x_ref[...], g_ref[...]), w_ref[...])
        low = _low_half((tm, LANES))
        for j in range(pairs):
            kb = _rope_fwd(kv[:, j * LANES:(j + 1) * LANES], cos_ref[...], sin_ref[...])
            vb = kv[:, kvd + j * LANES:kvd + (j + 1) * LANES]
            for blk, ref in ((kb, k_ref), (vb, v_ref)):
                sw = pltpu.roll(blk, 64, 1)
                ref[2 * j] = jnp.where(low, blk, sw).astype(ref.dtype)
                ref[2 * j + 1] = jnp.where(low, sw, blk).astype(ref.dtype)

    heads = jax.ShapeDtypeStruct((N_KV_HEADS, t, LANES), _ACT_DTYPE)
    hspec = pl.BlockSpec((N_KV_HEADS, tm, LANES), lambda i: (0, i, 0))
    return pl.pallas_call(
        body, name=name, grid=(t // tm,), out_shape=(heads, heads),
        in_specs=[_rows(tm, d_model), _const((1, d_model)), _const(w_kv.shape), _rows(tm, LANES), _rows(tm, LANES)],
        out_specs=(hspec, hspec), compiler_params=_params(),
    )(x, kv_g, w_kv, cos, ssin)


def _kv_bwd(dx, x, kv_g, w_kv, cos, ssin, dks, dvs, name):
    t, d_model = x.shape
    tm = _row_tile(t, PROJ_TILE)
    kvd = w_kv.shape[1] // 2
    pairs = kvd // LANES
    n_users = len(dks)

    def body(dx_ref, x_ref, g_ref, w_ref, cos_ref, sin_ref, *refs):
        dk_refs, dv_refs = refs[:n_users], refs[n_users:2 * n_users]
        dxi_ref, h_ref, dkv_ref, dg_ref = refs[2 * n_users:]
        dk_ref = functools.reduce(lambda a, b: a + b, [r[...] for r in dk_refs])
        dv_ref = functools.reduce(lambda a, b: a + b, [r[...] for r in dv_refs])
        i = pl.program_id(0)

        @pl.when(i == 0)
        def _():
            dg_ref[...] = jnp.zeros_like(dg_ref)

        xv = x_ref[...]
        g = g_ref[...]
        h_ref[...] = _rms_fwd(xv, g).astype(h_ref.dtype)
        low = _low_half((tm, LANES))
        dks, dvs = [], []
        for j in range(pairs):
            dkb = jnp.where(low, dk_ref[2 * j], dk_ref[2 * j + 1])
            dks.append(_rope_bwd(dkb, cos_ref[...], sin_ref[...]))
            dvs.append(jnp.where(low, dv_ref[2 * j], dv_ref[2 * j + 1]))
        dkv = jnp.concatenate(dks + dvs, axis=1).astype(_MXU_DTYPE)
        dkv_ref[...] = dkv
        dxp, dg = _rms_bwd(xv, g, _mm_tb(dkv, w_ref[...]))
        dg_ref[...] += dg
        dxi_ref[...] = dx_ref[...] + dxp

    hspec = pl.BlockSpec((N_KV_HEADS, tm, LANES), lambda i: (0, i, 0))
    return pl.pallas_call(
        body, name=name, grid=(t // tm,),
        out_shape=(jax.ShapeDtypeStruct((t, d_model), F32), jax.ShapeDtypeStruct((t, d_model), _ACT_DTYPE),
                   jax.ShapeDtypeStruct((t, 2 * kvd), _ACT_DTYPE), jax.ShapeDtypeStruct((1, d_model), F32)),
        in_specs=[_rows(tm, d_model), _rows(tm, d_model), _const((1, d_model)), _const(w_kv.shape),
                  _rows(tm, LANES), _rows(tm, LANES)] + [hspec] * (2 * n_users),
        out_specs=(_rows(tm, d_model), _rows(tm, d_model), _rows(tm, 2 * kvd), _const((1, d_model))),
        compiler_params=_params(),
    )(dx, x, kv_g, w_kv, cos, ssin, *dks, *dvs)


def _q_fwd(x, pre_g, w_q, cos, ssin, layer, j, name):
    t, d_model = x.shape
    tm = _row_tile(t, PROJ_TILE)

    def body(x_ref, g_ref, w_ref, cos_ref, sin_ref, q_ref):
        q = _mm(_rms_fwd(x_ref[...], g_ref[...]), w_ref[...])
        for p in range(d_model // LANES):
            cols = slice(p * LANES, (p + 1) * LANES)
            q_ref[:, cols] = (_rope_fwd(q[:, cols], cos_ref[...], sin_ref[...]) * ATTN_SCALE).astype(q_ref.dtype)

    return pl.pallas_call(
        body, name=name, grid=(t // tm,), out_shape=jax.ShapeDtypeStruct((t, d_model), _ACT_DTYPE),
        in_specs=[_rows(tm, d_model), _const((None, 1, d_model), (layer, 0, 0)), _const((None, d_model, d_model), (j, 0, 0)),
                  _rows(tm, LANES), _rows(tm, LANES)],
        out_specs=_rows(tm, d_model), compiler_params=_params(),
    )(x, pre_g, w_q, cos, ssin)


def _q_bwd(dx, dqs, x, pre_g, w_q, cos, ssin, layer, j, name):
    t, d_model = x.shape
    tm = _row_tile(t, PROJ_TILE)

    def body(dx_ref, dq_ref, x_ref, g_ref, w_ref, cos_ref, sin_ref, dxi_ref, h_ref, dqo_ref, dg_ref):
        i = pl.program_id(0)

        @pl.when(i == 0)
        def _():
            dg_ref[...] = jnp.zeros_like(dg_ref)

        xv = x_ref[...]
        g = g_ref[...]
        h_ref[...] = _rms_fwd(xv, g).astype(h_ref.dtype)
        parts = []
        for p in range(d_model // LANES):
            cols = slice(p * LANES, (p + 1) * LANES)
            parts.append(_rope_bwd(dq_ref[:, cols] * ATTN_SCALE, cos_ref[...], sin_ref[...]))
        dq = jnp.concatenate(parts, axis=1).astype(_MXU_DTYPE)
        dqo_ref[...] = dq
        dxp, dg = _rms_bwd(xv, g, _mm_tb(dq, w_ref[...]))
        dg_ref[...] += dg
        dxi_ref[...] = dx_ref[...] + dxp

    act = jax.ShapeDtypeStruct((t, d_model), _ACT_DTYPE)
    return pl.pallas_call(
        body, name=name, grid=(t // tm,),
        out_shape=(jax.ShapeDtypeStruct((t, d_model), F32), act, act, jax.ShapeDtypeStruct((1, d_model), F32)),
        in_specs=[_rows(tm, d_model), _rows(tm, d_model), _rows(tm, d_model), _const((None, 1, d_model), (layer, 0, 0)),
                  _const((None, d_model, d_model), (j, 0, 0)), _rows(tm, LANES), _rows(tm, LANES)],
        out_specs=(_rows(tm, d_model), _rows(tm, d_model), _rows(tm, d_model), _const((1, d_model))),
        compiler_params=_params(),
    )(dx, dqs, x, pre_g, w_q, cos, ssin)


def _attn_scores(q_pair, k2, n, sink_a, sink_b):
    low = _low_half(q_pair.shape)
    zero = jnp.zeros_like(q_pair)
    qst = jnp.concatenate([jnp.where(low, q_pair, zero), jnp.where(low, zero, q_pair)], axis=0)
    s = _mm_tb(qst, k2)
    row = lax.broadcasted_iota(jnp.int32, s.shape, 0)
    col = lax.broadcasted_iota(jnp.int32, s.shape, 1)
    rel = BLOCK + (row & (BLOCK - 1)) - col
    valid = (rel >= 0) & (rel < WINDOW) & (n * BLOCK + col - BLOCK >= 0)
    s = jnp.where(valid, s, NEG_INF)
    rows1 = lax.broadcasted_iota(jnp.int32, (2 * BLOCK, 1), 0)
    sink = jnp.where(rows1 < BLOCK, sink_a, sink_b)
    return qst, s, sink


def _attn_fwd(qs, kdup, vdup, sinks, j, name):
    t, d_model = qs.shape
    nb = t // BLOCK
    n_pairs = d_model // LANES
    per_group = n_pairs // N_KV_HEADS

    def body(sink_ref, q_ref, kp_ref, ko_ref, vp_ref, vo_ref, o_ref, lse_ref):
        n = pl.program_id(0)
        low = _low_half((BLOCK, LANES))
        lane = _lane_iota((BLOCK, LANES))
        lse = jnp.zeros((BLOCK, LANES), F32)
        for p in range(n_pairs):
            hk = p // per_group
            k2 = jnp.concatenate([kp_ref[hk], ko_ref[hk]], axis=0)
            v2 = jnp.concatenate([vp_ref[hk], vo_ref[hk]], axis=0)
            _, s, sink = _attn_scores(q_ref[:, p * LANES:(p + 1) * LANES], k2, n, sink_ref[j, 2 * p], sink_ref[j, 2 * p + 1])
            m = jnp.maximum(jnp.max(s, axis=-1, keepdims=True), sink)
            pe = jnp.exp(s - m)
            denom = jnp.sum(pe, axis=-1, keepdims=True) + jnp.exp(sink - m)
            o2 = _mm(pe, v2) / denom
            o_ref[:, p * LANES:(p + 1) * LANES] = jnp.where(low, o2[:BLOCK], o2[BLOCK:]).astype(o_ref.dtype)
            l2 = m + jnp.log(denom)
            lse = jnp.where(lane == 2 * p, l2[:BLOCK], lse)
            lse = jnp.where(lane == 2 * p + 1, l2[BLOCK:], lse)
        lse_ref[...] = lse

    prev = pl.BlockSpec((N_KV_HEADS, BLOCK, LANES), lambda n: (0, jnp.maximum(n - 1, 0), 0))
    own = pl.BlockSpec((N_KV_HEADS, BLOCK, LANES), lambda n: (0, n, 0))
    return pl.pallas_call(
        body, name=name, grid=(nb,),
        out_shape=(jax.ShapeDtypeStruct((t, d_model), _ACT_DTYPE), jax.ShapeDtypeStruct((t, LANES), F32)),
        in_specs=[pl.BlockSpec(memory_space=pltpu.SMEM), _rows(BLOCK, d_model), prev, own, prev, own],
        out_specs=(_rows(BLOCK, d_model), _rows(BLOCK, LANES)), compiler_params=_params(),
    )(sinks, qs, kdup, kdup, vdup, vdup)


def _attn_bwd(qs, kdup, vdup, sinks, lse, do, j, name):
    t, d_model = qs.shape
    nb = t // BLOCK
    n_pairs = d_model // LANES
    per_group = n_pairs // N_KV_HEADS
    rev = lambda n: nb - 1 - n

    def body(sink_ref, q_ref, kp_ref, ko_ref, vp_ref, vo_ref, lse_ref, do_ref, dq_ref, dk_ref, dv_ref, ds_ref, ck, cv):
        i = pl.program_id(0)
        n = nb - 1 - i

        @pl.when(i == 0)
        def _():
            ck[...] = jnp.zeros_like(ck)
            cv[...] = jnp.zeros_like(cv)
            ds_ref[...] = jnp.zeros_like(ds_ref)

        low = _low_half((BLOCK, LANES))
        lane = _lane_iota((BLOCK, LANES))
        lane1 = _lane_iota((1, LANES))
        lsev = lse_ref[...]
        dsink = jnp.zeros((1, LANES), F32)
        dk2 = [jnp.zeros((2 * BLOCK, LANES), F32) for _ in range(N_KV_HEADS)]
        dv2 = [jnp.zeros((2 * BLOCK, LANES), F32) for _ in range(N_KV_HEADS)]
        for p in range(n_pairs):
            hk = p // per_group
            k2 = jnp.concatenate([kp_ref[hk], ko_ref[hk]], axis=0)
            v2 = jnp.concatenate([vp_ref[hk], vo_ref[hk]], axis=0)
            qst, s, sink = _attn_scores(q_ref[:, p * LANES:(p + 1) * LANES], k2, n, sink_ref[j, 2 * p], sink_ref[j, 2 * p + 1])
            l2 = jnp.concatenate([jnp.sum(jnp.where(lane == 2 * p, lsev, 0.0), axis=-1, keepdims=True),
                                  jnp.sum(jnp.where(lane == 2 * p + 1, lsev, 0.0), axis=-1, keepdims=True)], axis=0)
            pn = jnp.exp(s - l2)
            dop = do_ref[:, p * LANES:(p + 1) * LANES]
            zero = jnp.zeros_like(dop)
            dost = jnp.concatenate([jnp.where(low, dop, zero), jnp.where(low, zero, dop)], axis=0)
            dp = _mm_tb(dost, v2)
            dr = jnp.sum(pn * dp, axis=-1, keepdims=True)
            dsm = (pn * (dp - dr)).astype(_MXU_DTYPE)
            dsk = -jnp.exp(sink - l2) * dr
            dsink = dsink + jnp.where(lane1 == 2 * p, jnp.sum(dsk[:BLOCK]), 0.0) + jnp.where(lane1 == 2 * p + 1, jnp.sum(dsk[BLOCK:]), 0.0)
            dq2 = _mm(dsm, k2)
            dq_ref[:, p * LANES:(p + 1) * LANES] = jnp.where(low, dq2[:BLOCK], dq2[BLOCK:])
            dk2[hk] = dk2[hk] + _mm_ta(dsm, qst)
            dv2[hk] = dv2[hk] + _mm_ta(pn, dost)
        ds_ref[...] += dsink
        for hk in range(N_KV_HEADS):
            for acc, carry, ref in ((dk2[hk], ck, dk_ref), (dv2[hk], cv, dv_ref)):
                folded = acc + pltpu.roll(acc, 64, 1)
                ref[hk] = folded[BLOCK:] + carry[hk]
                carry[hk] = folded[:BLOCK]

    prev = pl.BlockSpec((N_KV_HEADS, BLOCK, LANES), lambda n: (0, jnp.maximum(rev(n) - 1, 0), 0))
    own = pl.BlockSpec((N_KV_HEADS, BLOCK, LANES), lambda n: (0, rev(n), 0))
    rows = lambda cols: pl.BlockSpec((BLOCK, cols), lambda n: (rev(n), 0))
    heads = jax.ShapeDtypeStruct((N_KV_HEADS, t, LANES), F32)
    return pl.pallas_call(
        body, name=name, grid=(nb,),
        out_shape=(jax.ShapeDtypeStruct((t, d_model), F32), heads, heads, jax.ShapeDtypeStruct((1, LANES), F32)),
        in_specs=[pl.BlockSpec(memory_space=pltpu.SMEM), rows(d_model), prev, own, prev, own, rows(LANES), rows(d_model)],
        out_specs=(rows(d_model), own, own, _const((1, LANES))),
        scratch_shapes=[pltpu.VMEM((N_KV_HEADS, BLOCK, LANES), F32), pltpu.VMEM((N_KV_HEADS, BLOCK, LANES), F32)],
        compiler_params=_params(),
    )(sinks, qs, kdup, kdup, vdup, vdup, lse, do)


def _oproj_fwd(x, o, w_o, post_g, layer, j, name):
    t, d_model = x.shape
    tm = _row_tile(t, PROJ_TILE)

    def body(x_ref, o_ref, w_ref, g_ref, xo_ref, mo_ref):
        mo = _mm(o_ref[...], w_ref[...])
        mo_ref[...] = mo
        xo_ref[...] = x_ref[...] + _rms_fwd(mo, g_ref[...])

    full = jax.ShapeDtypeStruct((t, d_model), F32)
    return pl.pallas_call(
        body, name=name, grid=(t // tm,), out_shape=(full, full),
        in_specs=[_rows(tm, d_model), _rows(tm, d_model), _const((None, d_model, d_model), (j, 0, 0)),
                  _const((None, 1, d_model), (layer, 0, 0))],
        out_specs=(_rows(tm, d_model), _rows(tm, d_model)), compiler_params=_params(),
    )(x, o, w_o, post_g)


def _oproj_bwd(dx, mo, w_o, post_g, layer, j, name):
    t, d_model = dx.shape
    tm = _row_tile(t, PROJ_TILE)

    def body(dx_ref, mo_ref, w_ref, g_ref, do_ref, dmo_ref, dg_ref):
        i = pl.program_id(0)

        @pl.when(i == 0)
        def _():
            dg_ref[...] = jnp.zeros_like(dg_ref)

        dmo, dg = _rms_bwd(mo_ref[...], g_ref[...], dx_ref[...])
        dg_ref[...] += dg
        dmo = dmo.astype(_MXU_DTYPE)
        dmo_ref[...] = dmo
        do_ref[...] = _mm_tb(dmo, w_ref[...]).astype(do_ref.dtype)

    act = jax.ShapeDtypeStruct((t, d_model), _ACT_DTYPE)
    return pl.pallas_call(
        body, name=name, grid=(t // tm,), out_shape=(act, act, jax.ShapeDtypeStruct((1, d_model), F32)),
        in_specs=[_rows(tm, d_model), _rows(tm, d_model), _const((None, d_model, d_model), (j, 0, 0)),
                  _const((None, 1, d_model), (layer, 0, 0))],
        out_specs=(_rows(tm, d_model), _rows(tm, d_model), _const((1, d_model))), compiler_params=_params(),
    )(dx, mo, w_o, post_g)


def _loss_grad(y, target, name):
    t, d_model = y.shape
    tm = _row_tile(t, PROJ_TILE)

    def body(y_ref, t_ref, dy_ref, loss_ref):
        i = pl.program_id(0)

        @pl.when(i == 0)
        def _():
            loss_ref[...] = jnp.zeros_like(loss_ref)

        err = y_ref[...] - t_ref[...]
        dy_ref[...] = err / d_model
        loss_ref[...] += 0.5 * jnp.sum(jnp.mean(err * err, axis=-1, keepdims=True), axis=0, keepdims=True)

    return pl.pallas_call(
        body, name=name, grid=(t // tm,),
        out_shape=(jax.ShapeDtypeStruct((t, d_model), F32), jax.ShapeDtypeStruct((1, 1), F32)),
        in_specs=[_rows(tm, d_model), _rows(tm, d_model)], out_specs=(_rows(tm, d_model), _const((1, 1))),
        compiler_params=_params(),
    )(y, target)


def _mesh_position():
    return lax.axis_index("x"), lax.axis_index("y"), lax.axis_index("c")


def _block_of(px, py, pc):
    return 4 * px + 2 * py + pc


def _at_block(ref, axis, block):
    return ref.at[(slice(None),) * axis + (block,)]


def _all_gather(shards, axes, name):
    n = len(shards)

    def body(*refs):
        srcs, outs = refs[:n], refs[n:2 * n]
        send_sems, recv_sems, local_sems = refs[2 * n:]
        x, y, c = _mesh_position()
        me, sibling = (x, y, c), (x, y, 1 - c)
        chips = [(1 - x, y), (x, 1 - y), (1 - x, 1 - y)]

        def blk(i, pos):
            return _at_block(outs[i], axes[i], _block_of(*pos))

        def copy(i, k, block, to, src=None):
            return pltpu.make_async_remote_copy(
                src_ref=blk(i, block) if src is None else src, dst_ref=blk(i, block),
                send_sem=send_sems.at[i, k], recv_sem=recv_sems.at[i, k], device_id=to, device_id_type=MESH)

        mine = [pltpu.make_async_copy(srcs[i], blk(i, me), local_sems.at[i]) for i in range(n)]
        for cp in mine:
            cp.start()
        sent = []
        for i in range(n):
            sent += [copy(i, 1 + k, me, (*chip, c), src=srcs[i]) for k, chip in enumerate(chips)]
            sent.append(copy(i, 0, me, sibling, src=srcs[i]))
        for cp in sent:
            cp.start()
        for i in range(n):
            for k, chip in enumerate(chips):
                copy(i, 1 + k, (*chip, c), me).wait_recv()
                passed = copy(i, 4 + k, (*chip, c), sibling)
                passed.start()
                sent.append(passed)
        for i in range(n):
            copy(i, 0, sibling, me).wait_recv()
            for k, chip in enumerate(chips):
                copy(i, 4 + k, (*chip, 1 - c), me).wait_recv()
        for cp in sent:
            cp.wait_send()
        for cp in mine:
            cp.wait()

    hbm = pl.BlockSpec(memory_space=pl.ANY)
    return pl.pallas_call(
        body, name=name,
        out_shape=tuple(jax.ShapeDtypeStruct(s.shape[:a] + (N_DEV,) + s.shape[a:], s.dtype) for s, a in zip(shards, axes)),
        in_specs=[hbm] * n, out_specs=(hbm,) * n,
        scratch_shapes=[pltpu.SemaphoreType.DMA((n, 7)), pltpu.SemaphoreType.DMA((n, 7)), pltpu.SemaphoreType.DMA((n,))],
    )(*shards)


GATHER, SCATTER, GATHER_CHIPS, GATHER_SIBLING = "gather", "scatter", "gather_chips", "gather_sibling"
COPIES = {GATHER: N_DEV - 1, SCATTER: N_DEV - 1, GATHER_CHIPS: 4, GATHER_SIBLING: 3}


def _land_shape(kind, s, axis):
    if kind == SCATTER:
        return (N_DEV,) + s.shape[:axis] + s.shape[axis + 1:]
    return s.shape[:axis] + (N_DEV,) + s.shape[axis:]


def _plan(kind, srcs, lands, axes):
    x, y, c = _mesh_position()
    my_block = _block_of(x, y, c)
    flips = {GATHER_CHIPS: (1, 4, 2, 6), GATHER_SIBLING: (4, 2, 6)}.get(kind, range(1, N_DEV))
    others = [(1 - x if k & 4 else x, 1 - y if k & 2 else y, 1 - c if k & 1 else c) for k in flips]
    remote = []
    for src, land, axis in zip(srcs, lands, axes):
        if kind == SCATTER:
            mine = land.at[my_block]
            remote += [(_at_block(src, axis, _block_of(*peer)), mine, peer, land.at[_block_of(*peer)]) for peer in others]
        elif kind == GATHER_SIBLING:
            for px, py, _ in others:
                mine, theirs = _at_block(land, axis, _block_of(px, py, c)), _at_block(land, axis, _block_of(px, py, 1 - c))
                remote.append((mine, mine, (x, y, 1 - c), theirs))
        else:
            mine = _at_block(land, axis, my_block)
            remote += [(src, mine, peer, _at_block(land, axis, _block_of(*peer))) for peer in others]
    return remote


def _remote(src, dst, send_sems, recv_sems, k, peer):
    return pltpu.make_async_remote_copy(src_ref=src, dst_ref=dst, send_sem=send_sems.at[k], recv_sem=recv_sems.at[k],
                                        device_id=peer, device_id_type=MESH)


_HBM = pl.BlockSpec(memory_space=pltpu.HBM)
_SEM = pl.BlockSpec(memory_space=pltpu.SEMAPHORE)
_SPLIT = dict(has_side_effects=pltpu.SideEffectType.DATAFLOW_SIDE_EFFECTING)


def _landing_zone(kind, s, axis, me):
    land = lax.empty(_land_shape(kind, s, axis), s.dtype)
    if kind == SCATTER:
        return lax.dynamic_update_slice_in_dim(land, lax.dynamic_slice_in_dim(s, me, 1, axis).reshape((1,) + land.shape[1:]), me, 0)
    return lax.dynamic_update_slice_in_dim(land, jnp.expand_dims(s, axis), me, axis)


def _exchange_start(kind, arrays, axes, after, name):
    n = len(arrays)
    if kind == GATHER_SIBLING:
        passed = list(arrays)
    else:
        me = _block_of(*_mesh_position())
        passed = list(arrays) + [_landing_zone(kind, s, a, me) for s, a in zip(arrays, axes)]
    n_sems = n * COPIES[kind]

    def body(*refs):
        land_refs = refs[len(passed) - n:len(passed)]
        send_sems, recv_sems = refs[len(passed) + 1], refs[len(passed) + 2]
        token = refs[-1]
        for k, (src, dst, peer, _) in enumerate(_plan(kind, refs[:n], land_refs, axes)):
            _remote(src, dst, send_sems, recv_sems, k, peer).start()
        token[...] = jnp.zeros_like(token)

    out = pl.pallas_call(
        body, name=name,
        out_shape=(pltpu.SemaphoreType.DMA((n_sems,)), pltpu.SemaphoreType.DMA((n_sems,)),
                   *[pltpu.HBM(a.shape, a.dtype) for a in passed], jax.ShapeDtypeStruct((8, LANES), F32)),
        in_specs=[_HBM] * len(passed) + [pl.BlockSpec(memory_space=pl.ANY)],
        out_specs=(_SEM, _SEM, *[_HBM] * len(passed), pl.BlockSpec(memory_space=pltpu.VMEM)),
        input_output_aliases={i: 2 + i for i in range(len(passed))},
        compiler_params=pltpu.CompilerParams(**_SPLIT),
    )(*[pltpu.with_memory_space_constraint(a, pltpu.HBM) for a in passed], after)
    return (kind, axes, n, out[:-1]), out[-1]


def _exchange_wait(handle, after, name):
    kind, axes, n, (send_sems, recv_sems, *thru) = handle

    def body(*refs):
        land_refs = refs[len(thru) - n:len(thru)]
        send_sems, recv_sems = refs[len(thru)], refs[len(thru) + 1]
        for k, (src, _, peer, arrives) in enumerate(_plan(kind, refs[:n], land_refs, axes)):
            cp = _remote(src, arrives, send_sems, recv_sems, k, peer)
            cp.wait_send()
            cp.wait_recv()

    out = pl.pallas_call(
        body, name=name,
        out_shape=tuple(pltpu.HBM(a.shape, a.dtype) for a in thru),
        in_specs=[_HBM] * len(thru) + [_SEM, _SEM, pl.BlockSpec(memory_space=pl.ANY)], out_specs=(_HBM,) * len(thru),
        input_output_aliases={i: i for i in range(len(thru))},
        compiler_params=pltpu.CompilerParams(**_SPLIT),
    )(*thru, send_sems, recv_sems, after)
    return out[len(thru) - n:]


def _adamw_math(w, g, m, v):
    m = ADAM_B1 * m + (1.0 - ADAM_B1) * g
    v = ADAM_B2 * v + (1.0 - ADAM_B2) * jnp.square(g)
    m_hat = m / (1.0 - ADAM_B1 ** ADAM_STEP)
    v_hat = v / (1.0 - ADAM_B2 ** ADAM_STEP)
    delta = -ADAM_LR * (m_hat / (jnp.sqrt(v_hat) + ADAM_EPS) + ADAM_WD * w)
    return delta, m, v


def _update_tile(rows):
    if rows <= 512:
        return rows
    for tr in (512, 384, 352, 256, 176, 128, 64, 32, 16):
        if rows % tr == 0:
            return tr
    raise ValueError(f"{rows} rows do not tile")


def _adamw(parts, w, m, v, slab, so_far, name):
    rows, c = w.shape
    r = parts.shape[1]
    tr = _update_tile(r)
    first = slab * (r // tr)
    if so_far is None:
        so_far = tuple(lax.empty((rows, c), F32) for _ in range(4))

    def body(p_ref, w_ref, m_ref, v_ref, *refs):
        g_ref, d_ref, mo_ref, vo_ref = refs[4:]
        g = p_ref[0].astype(F32)
        for s in range(1, N_DEV):
            g = g + p_ref[s].astype(F32)
        g_ref[...] = g
        d_ref[...], mo_ref[...], vo_ref[...] = _adamw_math(w_ref[...], g, m_ref[...], v_ref[...])

    out = jax.ShapeDtypeStruct((rows, c), F32)
    tile = pl.BlockSpec((tr, c), lambda i: (first + i, 0))
    return pl.pallas_call(
        body, name=name, grid=(r // tr,), out_shape=(out,) * 4,
        in_specs=[pl.BlockSpec((N_DEV, tr, c), lambda i: (0, i, 0))] + [tile] * 3 + [pl.BlockSpec(memory_space=pl.ANY)] * 4,
        out_specs=(tile,) * 4, input_output_aliases={4 + k: k for k in range(4)}, compiler_params=_params(),
    )(parts, w, m, v, *so_far)


def _adamw_small(parts, picks, weights, name):
    n = len(parts)

    def body(*refs):
        p_refs, wmv, outs = refs[:n], refs[n:4 * n], refs[4 * n:]
        me = _block_of(*_mesh_position())
        for i in range(n):
            g = picks[i](p_refs[i], 0, me)
            for s in range(1, N_DEV):
                g = g + picks[i](p_refs[i], s, me)
            w_ref, m_ref, v_ref = wmv[3 * i:3 * i + 3]
            g_ref, d_ref, mo_ref, vo_ref = outs[4 * i:4 * i + 4]
            g_ref[...] = g
            d_ref[...], mo_ref[...], vo_ref[...] = _adamw_math(w_ref[...], g, m_ref[...], v_ref[...])

    flat = [a for wmv in weights for a in wmv]
    out = pl.pallas_call(
        body, name=name,
        out_shape=tuple(jax.ShapeDtypeStruct(w.shape, F32) for w, _, _ in weights for _ in range(4)),
        compiler_params=pltpu.CompilerParams(vmem_limit_bytes=VMEM_LIMIT),
    )(*parts, *flat)
    return [tuple(out[4 * i:4 * i + 4]) for i in range(n)]


def kernel(x, positions, mix_pre_g, mix_post_g, pool_w, pool_scale, kv_norm_g, w_kv, w_q, w_o, sinks, ffn_pre_g, ffn_post_g, ffn_w_in, ffn_conv_w, ffn_conv_b, ffn_w_out, loss_target, m_mix_pre_g, m_mix_post_g, m_pool_w, m_pool_scale, m_kv_norm_g, m_w_kv, m_w_q, m_w_o, m_sinks, m_ffn_pre_g, m_ffn_post_g, m_ffn_w_in, m_ffn_conv_w, m_ffn_conv_b, m_ffn_w_out, v_mix_pre_g, v_mix_post_g, v_pool_w, v_pool_scale, v_kv_norm_g, v_w_kv, v_w_q, v_w_o, v_sinks, v_ffn_pre_g, v_ffn_post_g, v_ffn_w_in, v_ffn_conv_w, v_ffn_conv_b, v_ffn_w_out):
    depth, d_model = mix_pre_g.shape
    n_a = pool_w.shape[0]
    n_b = w_q.shape[0]
    t = x.shape[1]
    fs = ffn_w_in.shape[2]
    half = N_DEV // 2
    n_heads = d_model // HEAD_DIM
    x0 = x.reshape(t, d_model)
    target = loss_target.reshape(t, d_model)

    inv_freq = 1.0 / (ROPE_THETA ** (jnp.arange(0, HEAD_DIM, 2, dtype=F32) / HEAD_DIM))
    ang = positions.reshape(t).astype(F32)[:, None] * inv_freq
    cos, sin = jnp.cos(ang), jnp.sin(ang)
    cos = jnp.tile(cos, (1, 2 * LANES // HEAD_DIM))
    ssin = jnp.tile(jnp.concatenate([-sin, sin], axis=1), (1, LANES // HEAD_DIM))

    wire = lambda a: a.astype(_WIRE_DTYPE)
    w_in_b, w_out_b = wire(ffn_w_in), wire(ffn_w_out)
    pool_w_g, w_in_0, w_out_0, pool_scale_g, conv_w_g = _all_gather(
        [wire(pool_w), w_in_b[:1], w_out_b[:1], pool_scale, ffn_conv_w], [2, 1, 1, 0, 0], "gather_first")
    groups = []
    for l in range(1, depth):
        if l == n_a:
            groups.append(("attn", [wire(w_kv), wire(w_q), wire(w_o)], [0, 1, 1], l))
        groups.append((l, [w_in_b[l:l + 1], w_out_b[l:l + 1]], [1, 1], l))
    over_ici, to_sibling, tokens, after = {}, {}, [], w_in_0
    for key, shards, axes, _ in groups:
        over_ici[key], after = _exchange_start(GATHER_CHIPS, shards, axes, after, f"gather_chips_{key}")
        tokens.append(after)
    started = functools.reduce(lambda a, b: a + b, [tk[0, 0] for tk in tokens])

    def pass_on(layer, after):
        sent = jnp.zeros((), F32)
        for key, _, axes, first in groups:
            if first == layer:
                lands = _exchange_wait(over_ici[key], after, f"gather_chips_wait_{key}")
                to_sibling[key], tk = _exchange_start(GATHER_SIBLING, lands, axes, after, f"gather_sibling_{key}")
                sent = sent + tk[0, 0]
        return sent

    w_in_l, w_out_l = {0: w_in_0}, {0: w_out_0.reshape(1, half * fs, d_model)}
    pool_scale_f = pool_scale_g.transpose(1, 0, 2).reshape(n_a, 1, d_model)
    conv_w_f = conv_w_g.transpose(1, 2, 0, 3)
    pool_w_f = pool_w_g.reshape(n_a, len(POOL_WINDOWS), d_model // len(POOL_WINDOWS), -1)
    conv_b_f = ffn_conv_b.reshape(depth, N_DEV, fs)
    g3 = lambda a: a.reshape(a.shape[0], 1, a.shape[1])
    mix_pre, mix_post, ffn_pre, ffn_post = g3(mix_pre_g) + started, g3(mix_post_g), g3(ffn_pre_g), g3(ffn_post_g)
    kv_g = kv_norm_g.reshape(1, d_model)
    w_kv_f = w_q_f = w_o_f = None

    saved = []
    xc = x0
    kdup = vdup = x_kv = None
    for l in range(depth):
        x_in = xc
        if l < n_a:
            x_mid, dsave, yu = _pool_fwd(x_in, mix_pre, mix_post, pool_w_f, pool_scale_f, l, f"pool_fwd_{l}")
            mixer = (dsave, yu)
        else:
            j = l - n_a
            if j == 0:
                x_kv = x_in
                w_kv_g, w_q_g, w_o_g = _exchange_wait(to_sibling["attn"], x_in, "gather_sibling_wait_attn")
                w_kv_f = w_kv_g.reshape(d_model, -1)
                w_q_f = w_q_g.reshape(n_b, d_model, d_model)
                w_o_f = w_o_g.reshape(n_b, d_model, d_model)
                kdup, vdup = _kv_fwd(x_kv, kv_g, w_kv_f, cos, ssin, "kv_fwd")
            qs = _q_fwd(x_in, mix_pre, w_q_f, cos, ssin, l, j, f"q_fwd_{l}")
            o, lse = _attn_fwd(qs, kdup, vdup, sinks, j, f"attn_fwd_{l}")
            x_mid, mo = _oproj_fwd(x_in, o, w_o_f, mix_post, l, j, f"oproj_fwd_{l}")
            mixer = (qs, o, lse, mo)
        if l > 0:
            w_in_l[l], w_out_g = _exchange_wait(to_sibling[l], x_mid, f"gather_sibling_wait_{l}")
            w_out_l[l] = w_out_g.reshape(1, half * fs, d_model)
        xc, u, uc, f = _ffn_fwd(x_mid, ffn_pre, ffn_post, w_in_l[l], conv_w_f, conv_b_f, w_out_l[l], l, 0, f"ffn_fwd_{l}")
        saved.append((x_in, x_mid, u, uc, f, mixer))
        if l + 1 < depth:
            mix_pre = mix_pre + pass_on(l + 1, xc)

    dx, loss_part = _loss_grad(xc, target, "loss")
    loss = lax.psum(loss_part[0, 0], AXES)

    gconv_w, gconv_b = [None] * depth, [None] * depth
    gmix_pre, gmix_post, gffn_pre, gffn_post = [None] * depth, [None] * depth, [None] * depth, [None] * depth
    gpool_scale, gsinks = [None] * n_a, [None] * n_b
    dks, dvs = [], []
    gkv_g = None
    whole = lambda cols: pl.BlockSpec((t, cols), lambda c: (0, 0), pipeline_mode=pl.Buffered(1))
    per_out = lambda cols: pl.BlockSpec((None, t, cols), lambda c: (c, 0, 0))
    by_dev = lambda g: g.reshape(N_DEV, -1, g.shape[-1])
    def small_grads():
        cat = lambda rows: jnp.concatenate(rows, axis=0)
        row = lambda a: a.reshape(1, -1)
        everything = lambda ref, s, me: ref[s]
        lanes = pool_scale.shape[1]
        return [("mix_pre_g", cat(gmix_pre), everything, (mix_pre_g, m_mix_pre_g, v_mix_pre_g)),
                ("mix_post_g", cat(gmix_post), everything, (mix_post_g, m_mix_post_g, v_mix_post_g)),
                ("kv_norm_g", gkv_g, everything, (row(kv_norm_g), row(m_kv_norm_g), row(v_kv_norm_g))),
                ("sinks", cat(gsinks), lambda ref, s, me: ref[s, :, pl.ds(0, n_heads)], (sinks, m_sinks, v_sinks)),
                ("ffn_pre_g", cat(gffn_pre), everything, (ffn_pre_g, m_ffn_pre_g, v_ffn_pre_g)),
                ("ffn_post_g", cat(gffn_post), everything, (ffn_post_g, m_ffn_post_g, v_ffn_post_g)),
                ("ffn_conv_b", jnp.stack(gconv_b).reshape(depth, N_DEV * fs), everything, (ffn_conv_b, m_ffn_conv_b, v_ffn_conv_b)),
                ("pool_scale", cat(gpool_scale), lambda ref, s, me: ref[s, :, pl.ds(pl.multiple_of(me * lanes, lanes), lanes)],
                 (pool_scale, m_pool_scale, v_pool_scale)),
                ("ffn_conv_w", jnp.stack(gconv_w).transpose(0, 2, 1, 3), lambda ref, s, me: ref[s, :, me],
                 (ffn_conv_w, m_ffn_conv_w, v_ffn_conv_w))]

    flying = []

    def launch(going, after, name):
        handle, token = _exchange_start(SCATTER, [g for _, _, g, _ in going], [a for _, _, _, a in going], after, name)
        flying.append(([(nm, slab) for nm, slab, _, _ in going], handle))
        return token

    post = mix_post
    ffn_post_b = ffn_post
    token = None
    for l in reversed(range(depth)):
        x_in, x_mid, u, uc, f, mixer = saved[l]
        duc, gact, df, gconv_b[l], gffn_post[l] = _ffn_bwd_out(dx, f, ffn_post_b, uc, conv_b_f, w_out_l[l], l, 0, f"ffn_bwd_out_{l}")
        dx, du, hf, gconv_w[l], gffn_pre[l] = _ffn_bwd_in(dx, x_mid, ffn_pre, duc, u, conv_w_f, w_in_l[l], l, 0, f"ffn_bwd_in_{l}")
        gin = _tn_matmul(hf, du, whole(d_model), per_out(fs), N_DEV, d_model, fs, f"grad_w_in_{l}", _WIRE_DTYPE)
        going = [("ffn_w_in", l, gin, 0)]
        if l == 0:
            token = launch(going, dx, "scatter_start_0_in")
            going = []
            post = post + token[0, 0]
        gout = _tn_matmul(gact, df, per_out(fs), whole(d_model), half, fs, d_model, f"grad_w_out_{l}", _WIRE_DTYPE)
        going.append(("ffn_w_out", l, by_dev(gout), 0))
        if l < n_a:
            dsave, yu = mixer
            dx, dyu, gpool_scale[l], gmix_pre[l], gmix_post[l] = _pool_bwd(
                dx, x_in, mix_pre, post, dsave, yu, pool_w_f, pool_scale_f, l, f"pool_bwd_{l}")
            gc = d_model // len(POOL_WINDOWS)
            by_group = pl.BlockSpec((t, gc), lambda c: (0, c))
            gpool = _tn_matmul(dsave, dyu, by_group, by_group, len(POOL_WINDOWS), gc, gc, f"grad_pool_w_{l}", _WIRE_DTYPE)
            going.append(("pool_w", l, gpool.reshape(len(POOL_WINDOWS), N_DEV, -1, gc), 1))
        else:
            j = l - n_a
            qs, o, lse, mo = mixer
            do, dmo, gmix_post[l] = _oproj_bwd(dx, mo, w_o_f, post, l, j, f"oproj_bwd_{l}")
            dqs, dk, dv, gsinks[j] = _attn_bwd(qs, kdup, vdup, sinks, lse, do, j, f"attn_bwd_{l}")
            dks.append(dk)
            dvs.append(dv)
            dx, hq, dq, gmix_pre[l] = _q_bwd(dx, dqs, x_in, mix_pre, w_q_f, cos, ssin, l, j, f"q_bwd_{l}")
            if j == 0:
                dx, hkv, dkv, gkv_g = _kv_bwd(dx, x_kv, kv_g, w_kv_f, cos, ssin, dks, dvs, "kv_bwd")
                gkv = _tn_matmul(hkv, dkv, whole(d_model), whole(dkv.shape[1]), 1, d_model, dkv.shape[1], "grad_w_kv", _WIRE_DTYPE)
                going.append(("w_kv", 0, by_dev(gkv), 0))
            go = _tn_matmul(o, dmo, whole(d_model), whole(d_model), 1, d_model, d_model, f"grad_w_o_{l}", _WIRE_DTYPE)
            gq = _tn_matmul(hq, dq, whole(d_model), whole(d_model), 1, d_model, d_model, f"grad_w_q_{l}", _WIRE_DTYPE)
            going += [("w_o", j, by_dev(go), 0), ("w_q", j, by_dev(gq), 0)]
        after = dx
        if l == 0:
            small = small_grads()
            small_flight, after = _exchange_start(GATHER, [g for _, g, _, _ in small], [0] * len(small), dx, "gather_small_grads")
        token = launch(going, after, f"scatter_start_{l}")
        ffn_post_b = ffn_post_b + token[0, 0]

    grad_x = dx.reshape(x.shape)

    shard = {"pool_w": (pool_w, m_pool_w, v_pool_w), "w_kv": (w_kv, m_w_kv, v_w_kv), "w_q": (w_q, m_w_q, v_w_q),
             "w_o": (w_o, m_w_o, v_w_o), "ffn_w_in": (ffn_w_in, m_ffn_w_in, v_ffn_w_in),
             "ffn_w_out": (ffn_w_out, m_ffn_w_out, v_ffn_w_out)}
    big = {}

    def arrive(flights, after):
        for idx, (names, handle) in flights:
            parts = _exchange_wait(handle, after, f"scatter_wait_{idx}")
            for (nm, slab), p in zip(names, parts):
                cols = p.shape[-1]
                w2, m2, v2 = (a.reshape(-1, cols) for a in shard[nm])
                big[nm] = _adamw(p.reshape(N_DEV, -1, cols), w2, m2, v2, slab, big.get(nm), f"adamw_{nm}_{slab}")
                after = big[nm][0]
        return after

    flights = list(enumerate(flying))
    done = arrive(flights[:depth - 1], token)
    small_parts = _exchange_wait(small_flight, done, "gather_small_grads_wait")
    upd = _adamw_small(small_parts, [pick for _, _, pick, _ in small], [wmv for _, _, _, wmv in small], "adamw_small")
    arrive(flights[depth - 1:], upd[0][0])
    res = {nm: tuple(r.reshape(shard[nm][0].shape) for r in out) for nm, out in big.items()}
    for (nm, _, _, _), out in zip(small, upd):
        res[nm] = tuple(a.reshape(kv_norm_g.shape) for a in out) if nm == "kv_norm_g" else out

    order = ["mix_pre_g", "mix_post_g", "pool_w", "pool_scale", "kv_norm_g", "w_kv", "w_q", "w_o", "sinks", "ffn_pre_g",
             "ffn_post_g", "ffn_w_in", "ffn_conv_w", "ffn_conv_b", "ffn_w_out"]
    return (loss, grad_x, *[res[nm][0] for nm in order], *[res[nm][1] for nm in order],
            *[res[nm][2] for nm in order], *[res[nm][3] for nm in order])
```

```python
import functools
import math

import jax
import jax.numpy as jnp
from jax import lax
from jax.experimental import pallas as pl
from jax.experimental.pallas import tpu as pltpu

F32 = jnp.float32
_MXU_DTYPE = jnp.bfloat16
_ACT_DTYPE = jnp.bfloat16
_WIRE_DTYPE = jnp.bfloat16
_SAVE_DTYPE = jnp.bfloat16

N_DEV = 8
POOL_WINDOWS = (2, 4, 8, 16)
POOL_HALO = 16
HEAD_DIM = 64
N_KV_HEADS = 4
WINDOW = 128
BLOCK = 128
LANES = 128
ROPE_THETA = 10000.0
ATTN_SCALE = 1.0 / math.sqrt(HEAD_DIM)
NEG_INF = -1e30
RMS_EPS = 1e-6
CONV_HALO = 8
SAVE_HALO = 16
PROJ_TILE = 512
ADAM_LR = 0.001
ADAM_B1 = 0.9
ADAM_B2 = 0.999
ADAM_EPS = 1e-08
ADAM_WD = 0.01
ADAM_STEP = 10
VMEM_LIMIT = 56 * 1024 * 1024
MESH = pl.DeviceIdType.MESH
AXES = ("x", "y", "c")


def _params(n_axes=1, vmem=VMEM_LIMIT):
    return pltpu.CompilerParams(dimension_semantics=("arbitrary",) * n_axes, vmem_limit_bytes=vmem)


def _resident(shape, index):
    return pl.BlockSpec(shape, lambda *_: index, pipeline_mode=pl.Buffered(1))


def _const(shape, index=None):
    index = (0,) * len(shape) if index is None else index
    return pl.BlockSpec(shape, lambda *_: index)


def _rows(tm, cols):
    return pl.BlockSpec((tm, cols), lambda i: (i, 0))


def _row_tile(t, most=256):
    for tm in (512, 256, 128, 64, 32, 16, 8):
        if tm <= most and t % tm == 0:
            return tm
    raise ValueError(f"sequence length {t} is not a multiple of 8")


def _mm(a, b):
    return jnp.dot(a.astype(_MXU_DTYPE), b.astype(_MXU_DTYPE), preferred_element_type=F32)


def _mm_tb(a, b):
    return lax.dot_general(a.astype(_MXU_DTYPE), b.astype(_MXU_DTYPE), (((1,), (1,)), ((), ())),
                           preferred_element_type=F32)


def _mm_ta(a, b):
    return lax.dot_general(a.astype(_MXU_DTYPE), b.astype(_MXU_DTYPE), (((0,), (0,)), ((), ())),
                           preferred_element_type=F32)


def _rms_r(x):
    return lax.rsqrt(jnp.mean(x * x, axis=-1, keepdims=True) + RMS_EPS)


def _rms_fwd(x, g):
    return (x * _rms_r(x)) * g


def _rms_bwd(x, g, dy):
    r = _rms_r(x)
    xh = x * r
    dg = jnp.sum(dy * xh, axis=0, keepdims=True)
    dxh = dy * g
    dx = r * (dxh - xh * jnp.mean(dxh * xh, axis=-1, keepdims=True))
    return dx, dg


_GELU_C = math.sqrt(2.0 / math.pi)


def _gelu_parts(z):
    z2 = z * z
    t = jnp.tanh(_GELU_C * (z + 0.044715 * (z2 * z)))
    cdf = 0.5 * (1.0 + t)
    dz = cdf + z * (0.5 * (1.0 - t * t)) * (_GELU_C * (1.0 + (3 * 0.044715) * z2))
    return cdf, dz


def _lane_iota(shape):
    return lax.broadcasted_iota(jnp.int32, shape, len(shape) - 1)


def _rope_partner(xb):
    first = (_lane_iota(xb.shape) & 32) == 0
    return jnp.where(first, pltpu.roll(xb, LANES - 32, 1), pltpu.roll(xb, 32, 1))


def _rope_fwd(xb, cos, ssin):
    return xb * cos + _rope_partner(xb) * ssin


def _rope_bwd(dyb, cos, ssin):
    return dyb * cos - _rope_partner(dyb) * ssin


def _low_half(shape):
    return (_lane_iota(shape) & 64) == 0


def _pool_fwd(x, pre_g, post_g, w, scale, layer, name):
    t, d_model = x.shape
    tm = _row_tile(t)
    n_groups, gc = w.shape[1], w.shape[2]

    def body(x_ref, pre_ref, post_ref, w_ref, sc_ref, xo_ref, d_ref, yu_ref, hbuf):
        i = pl.program_id(0)

        @pl.when(i == 0)
        def _():
            hbuf[pl.ds(0, POOL_HALO), :] = jnp.zeros((POOL_HALO, d_model), F32)

        xv = x_ref[...]
        hbuf[pl.ds(POOL_HALO, tm), :] = _rms_fwd(xv, pre_ref[...])
        tok = i * tm + lax.broadcasted_iota(jnp.int32, (tm, 1), 0)
        yus = []
        for gi, wnd in enumerate(POOL_WINDOWS):
            cols = pl.ds(gi * gc, gc)
            h = hbuf[pl.ds(POOL_HALO, tm), cols]
            acc = h
            for k in range(1, wnd):
                acc = acc + hbuf[pl.ds(POOL_HALO - k, tm), cols]
            cnt = jnp.minimum(tok + 1, wnd).astype(F32)
            dg = acc / cnt - h
            d_ref[:, cols] = dg.astype(d_ref.dtype)
            yus.append(_mm(dg, w_ref[gi]))
        hbuf[pl.ds(0, POOL_HALO), :] = hbuf[pl.ds(tm, POOL_HALO), :]
        yu = jnp.concatenate(yus, axis=1)
        yu_ref[...] = yu
        xo_ref[...] = xv + _rms_fwd(yu * sc_ref[...], post_ref[...])

    return pl.pallas_call(
        body, name=name, grid=(t // tm,),
        out_shape=(jax.ShapeDtypeStruct((t, d_model), F32), jax.ShapeDtypeStruct((t, d_model), _ACT_DTYPE),
                   jax.ShapeDtypeStruct((t, d_model), F32)),
        in_specs=[_rows(tm, d_model), _const((None, 1, d_model), (layer, 0, 0)), _const((None, 1, d_model), (layer, 0, 0)),
                  _const((None, n_groups, gc, gc), (layer, 0, 0, 0)), _const((None, 1, d_model), (layer, 0, 0))],
        out_specs=(_rows(tm, d_model), _rows(tm, d_model), _rows(tm, d_model)),
        scratch_shapes=[pltpu.VMEM((POOL_HALO + tm, d_model), F32)],
        compiler_params=_params(),
    )(x, pre_g, post_g, w, scale)


def _pool_bwd(dx, x, pre_g, post_g, d, yu, w, scale, layer, name):
    t, d_model = x.shape
    tm = _row_tile(t)
    nt = t // tm
    n_groups, gc = w.shape[1], w.shape[2]
    rev = lambda i: (nt - 1 - i, 0)
    rows = pl.BlockSpec((tm, d_model), rev)

    def body(dx_ref, x_ref, pre_ref, post_ref, d_ref, yu_ref, w_ref, sc_ref,
             dxi_ref, dyu_ref, dsc_ref, dpre_ref, dpost_ref, zbuf):
        i = pl.program_id(0)

        @pl.when(i == 0)
        def _():
            zbuf[pl.ds(tm, POOL_HALO), :] = jnp.zeros((POOL_HALO, d_model), F32)
            dsc_ref[...] = jnp.zeros_like(dsc_ref)
            dpre_ref[...] = jnp.zeros_like(dpre_ref)
            dpost_ref[...] = jnp.zeros_like(dpost_ref)

        dxo = dx_ref[...]
        yuv = yu_ref[...]
        sc = sc_ref[...]
        dm, dpost = _rms_bwd(yuv * sc, post_ref[...], dxo)
        dpost_ref[...] += dpost
        dsc_ref[...] += jnp.sum(dm * yuv, axis=0, keepdims=True)
        dyu = dm * sc
        dyu_ref[...] = dyu.astype(dyu_ref.dtype)
        tok = (nt - 1 - i) * tm + lax.broadcasted_iota(jnp.int32, (tm, 1), 0)
        dds = []
        for gi, wnd in enumerate(POOL_WINDOWS):
            cols = pl.ds(gi * gc, gc)
            dd = _mm_tb(dyu[:, gi * gc:(gi + 1) * gc], w_ref[gi])
            cnt = jnp.minimum(tok + 1, wnd).astype(F32)
            zbuf[pl.ds(0, tm), cols] = dd / cnt
            dds.append(dd)
        dhs = []
        for gi, wnd in enumerate(POOL_WINDOWS):
            cols = pl.ds(gi * gc, gc)
            acc = zbuf[pl.ds(0, tm), cols]
            for k in range(1, wnd):
                acc = acc + zbuf[pl.ds(k, tm), cols]
            dhs.append(acc - dds[gi])
        zbuf[pl.ds(tm, POOL_HALO), :] = zbuf[pl.ds(0, POOL_HALO), :]
        dh = jnp.concatenate(dhs, axis=1)
        dxp, dpre = _rms_bwd(x_ref[...], pre_ref[...], dh)
        dpre_ref[...] += dpre
        dxi_ref[...] = dxo + dxp

    vec = jax.ShapeDtypeStruct((1, d_model), F32)
    return pl.pallas_call(
        body, name=name, grid=(nt,),
        out_shape=(jax.ShapeDtypeStruct((t, d_model), F32), jax.ShapeDtypeStruct((t, d_model), _ACT_DTYPE), vec, vec, vec),
        in_specs=[rows, rows, _const((None, 1, d_model), (layer, 0, 0)), _const((None, 1, d_model), (layer, 0, 0)), rows, rows,
                  _const((None, n_groups, gc, gc), (layer, 0, 0, 0)), _const((None, 1, d_model), (layer, 0, 0))],
        out_specs=(rows, rows, _const((1, d_model)), _const((1, d_model)), _const((1, d_model))),
        scratch_shapes=[pltpu.VMEM((tm + POOL_HALO, d_model), F32)],
        compiler_params=_params(),
    )(dx, x, pre_g, post_g, d, yu, w, scale)


def _conv_taps(cw_ref, s):
    return [cw_ref[k, pl.ds(s, 1), :] for k in range(3)]


def _shift_down(v, k, before):
    rolled = pltpu.roll(v, k, 0)
    row = lax.broadcasted_iota(jnp.int32, before.shape, 0)
    head = jnp.where(row < k, pltpu.roll(before, k, 0), rolled[:CONV_HALO])
    return jnp.concatenate([head, rolled[CONV_HALO:]], axis=0)


def _shift_up(v, k, after):
    rows = v.shape[0]
    rolled = pltpu.roll(v, rows - k, 0)
    row = lax.broadcasted_iota(jnp.int32, after.shape, 0)
    tail = jnp.where(row >= CONV_HALO - k, pltpu.roll(after, CONV_HALO - k, 0), rolled[rows - CONV_HALO:])
    return jnp.concatenate([rolled[:rows - CONV_HALO], tail], axis=0)


def _ffn_fwd(x, pre_g, post_g, w_in, conv_w, conv_b, w_out, layer, w_layer, name):
    t, d_model = x.shape
    tm = _row_tile(t)
    fs = w_in.shape[3]
    half = N_DEV // 2

    def body(x_ref, pre_ref, post_ref, win_ref, cw_ref, cb_ref, wout_ref, xo_ref, u_ref, uc_ref, f_ref, carry):
        i = pl.program_id(0)

        @pl.when(i == 0)
        def _():
            carry[...] = jnp.zeros_like(carry)

        xv = x_ref[...]
        hf = _rms_fwd(xv, pre_ref[...]).astype(_MXU_DTYPE)
        f = jnp.zeros((tm, d_model), F32)
        project = lambda b: [jnp.dot(hf, win_ref[s], preferred_element_type=F32) for s in (b, b + half)]
        ahead = project(0)
        for b in range(half):
            us, ahead = ahead, project(b + 1) if b + 1 < half else None
            ucs = []
            for s, u in zip((b, b + half), us):
                u_ref[s] = u.astype(u_ref.dtype)
                before = carry[s]
                carry[s] = u[tm - CONV_HALO:]
                w0, w1, w2 = _conv_taps(cw_ref, s)
                uc = ((w0 * _shift_down(u, 2, before) + w1 * _shift_down(u, 1, before)) + w2 * u) + cb_ref[pl.ds(s, 1), :]
                uc_ref[s] = uc.astype(uc_ref.dtype)
                ucs.append(uc)
            gate, val = ucs
            cdf, _ = _gelu_parts(gate)
            f = f + _mm((gate * cdf) * val, wout_ref[pl.ds(b * fs, fs), :])
        f_ref[...] = f
        xo_ref[...] = xv + _rms_fwd(f, post_ref[...])

    tile3 = pl.BlockSpec((N_DEV, tm, fs), lambda i: (0, i, 0))
    saved = jax.ShapeDtypeStruct((N_DEV, t, fs), _SAVE_DTYPE)
    return pl.pallas_call(
        body, name=name, grid=(t // tm,),
        out_shape=(jax.ShapeDtypeStruct((t, d_model), F32), saved, saved, jax.ShapeDtypeStruct((t, d_model), F32)),
        in_specs=[_rows(tm, d_model), _const((None, 1, d_model), (layer, 0, 0)), _const((None, 1, d_model), (layer, 0, 0)),
                  _resident((None, N_DEV, d_model, fs), (w_layer, 0, 0, 0)), _const((None, 3, N_DEV, fs), (layer, 0, 0, 0)),
                  _const((None, N_DEV, fs), (layer, 0, 0)), _resident((None, half * fs, d_model), (w_layer, 0, 0))],
        out_specs=(_rows(tm, d_model), tile3, tile3, _rows(tm, d_model)),
        scratch_shapes=[pltpu.VMEM((N_DEV, CONV_HALO, fs), F32)],
        compiler_params=_params(),
    )(x, pre_g, post_g, w_in, conv_w, conv_b, w_out)


def _ffn_bwd_out(dx, f, post_g, uc, conv_b, w_out, layer, w_layer, name):
    t, d_model = dx.shape
    tm = _row_tile(t)
    fs = uc.shape[2]
    half = N_DEV // 2

    def body(dx_ref, f_ref, post_ref, uc_ref, wout_ref, duc_ref, g_ref, df_ref, dcb_ref, dpost_ref):
        i = pl.program_id(0)

        @pl.when(i == 0)
        def _():
            dcb_ref[...] = jnp.zeros_like(dcb_ref)
            dpost_ref[...] = jnp.zeros_like(dpost_ref)

        df, dpost = _rms_bwd(f_ref[...], post_ref[...], dx_ref[...])
        dpost_ref[...] += dpost
        dfm = df.astype(_MXU_DTYPE)
        df_ref[...] = dfm
        project = lambda b: _mm_tb(dfm, wout_ref[pl.ds(b * fs, fs), :])
        ahead = project(0)
        for b in range(half):
            dg, ahead = ahead, project(b + 1) if b + 1 < half else None
            gate = uc_ref[b].astype(F32)
            val = uc_ref[b + half].astype(F32)
            cdf, dgelu = _gelu_parts(gate)
            ge = gate * cdf
            g_ref[b] = (ge * val).astype(g_ref.dtype)
            for s, dd in ((b, dg * val * dgelu), (b + half, dg * ge)):
                duc_ref[s] = dd.astype(duc_ref.dtype)
                dcb_ref[pl.ds(s, 1), :] += jnp.sum(dd, axis=0, keepdims=True)

    tile3 = pl.BlockSpec((N_DEV, tm, fs), lambda i: (0, i, 0))
    return pl.pallas_call(
        body, name=name, grid=(t // tm,),
        out_shape=(jax.ShapeDtypeStruct((N_DEV, t, fs), _SAVE_DTYPE), jax.ShapeDtypeStruct((half, t, fs), _ACT_DTYPE),
                   jax.ShapeDtypeStruct((t, d_model), _ACT_DTYPE), jax.ShapeDtypeStruct((N_DEV, fs), F32),
                   jax.ShapeDtypeStruct((1, d_model), F32)),
        in_specs=[_rows(tm, d_model), _rows(tm, d_model), _const((None, 1, d_model), (layer, 0, 0)), tile3,
                  _resident((None, half * fs, d_model), (w_layer, 0, 0))],
        out_specs=(tile3, pl.BlockSpec((half, tm, fs), lambda i: (0, i, 0)), _rows(tm, d_model),
                   _const((N_DEV, fs)), _const((1, d_model))),
        compiler_params=_params(),
    )(dx, f, post_g, uc, w_out)


def _ffn_bwd_in(dx, x, pre_g, duc, u, conv_w, w_in, layer, w_layer, name):
    t, d_model = dx.shape
    tm = _row_tile(t)
    nt = t // tm
    fs = duc.shape[2]
    hb = SAVE_HALO
    per_tile = tm // hb

    def body(dx_ref, x_ref, pre_ref, duc_ref, dn_ref, u_ref, cw_ref, win_ref, dxi_ref, du_ref, hf_ref, dcw_ref, dpre_ref):
        i = pl.program_id(0)

        @pl.when(i == 0)
        def _():
            dcw_ref[...] = jnp.zeros_like(dcw_ref)
            dpre_ref[...] = jnp.zeros_like(dpre_ref)

        xv = x_ref[...]
        pre = pre_ref[...]
        hf_ref[...] = _rms_fwd(xv, pre).astype(hf_ref.dtype)
        dhf = jnp.zeros((tm, d_model), F32)
        for s in range(N_DEV):
            d0 = duc_ref[s].astype(F32)
            after = jnp.where(i == nt - 1, 0.0, dn_ref[s].astype(F32)[:CONV_HALO])
            d1 = _shift_up(d0, 1, after)
            d2 = _shift_up(d0, 2, after)
            uv = u_ref[s].astype(F32)
            for k, dk in ((2, d0), (1, d1), (0, d2)):
                dcw_ref[k, pl.ds(s, 1), :] += jnp.sum(dk * uv, axis=0, keepdims=True)
            w0, w1, w2 = _conv_taps(cw_ref, s)
            du = (w2 * d0 + w1 * d1 + w0 * d2).astype(_MXU_DTYPE)
            du_ref[s] = du
            dhf = dhf + _mm_tb(du, win_ref[s])
        dxp, dpre = _rms_bwd(xv, pre, dhf)
        dpre_ref[...] += dpre
        dxi_ref[...] = dx_ref[...] + dxp

    tile3 = pl.BlockSpec((N_DEV, tm, fs), lambda i: (0, i, 0))
    return pl.pallas_call(
        body, name=name, grid=(nt,),
        out_shape=(jax.ShapeDtypeStruct((t, d_model), F32), jax.ShapeDtypeStruct((N_DEV, t, fs), _ACT_DTYPE),
                   jax.ShapeDtypeStruct((t, d_model), _ACT_DTYPE), jax.ShapeDtypeStruct((3, N_DEV, fs), F32),
                   jax.ShapeDtypeStruct((1, d_model), F32)),
        in_specs=[_rows(tm, d_model), _rows(tm, d_model), _const((None, 1, d_model), (layer, 0, 0)), tile3,
                  pl.BlockSpec((N_DEV, hb, fs), lambda i: (0, jnp.minimum((i + 1) * per_tile, t // hb - 1), 0)), tile3,
                  _const((None, 3, N_DEV, fs), (layer, 0, 0, 0)), _resident((None, N_DEV, d_model, fs), (w_layer, 0, 0, 0))],
        out_specs=(_rows(tm, d_model), tile3, _rows(tm, d_model), _const((3, N_DEV, fs)), _const((1, d_model))),
        compiler_params=_params(),
    )(dx, x, pre_g, duc, duc, u, conv_w, w_in)


def _tn_matmul(a, b, a_spec, b_spec, n_out, m, n, name, out_dtype):
    def body(a_ref, b_ref, o_ref):
        o_ref[...] = _mm_ta(a_ref[...], b_ref[...]).astype(o_ref.dtype)

    return pl.pallas_call(
        body, name=name, grid=(n_out,),
        out_shape=jax.ShapeDtypeStruct((n_out, m, n), out_dtype),
        in_specs=[a_spec, b_spec],
        out_specs=pl.BlockSpec((None, m, n), lambda c: (c, 0, 0)),
        compiler_params=_params(),
    )(a, b)


def _kv_fwd(x, kv_g, w_kv, cos, ssin, name):
    t, d_model = x.shape
    tm = _row_tile(t, PROJ_TILE)
    kvd = w_kv.shape[1] // 2
    pairs = kvd // LANES

    def body(x_ref, g_ref, w_ref, cos_ref, sin_ref, k_ref, v_ref):
        kv = _mm(_rms_fwd(x_ref[...], g_ref[...]), w_ref[...])
        low = _low_half((tm, LANES))
        for j in range(pairs):
            kb = _rope_fwd(kv[:, j * LANES:(j + 1) * LANES], cos_ref[...], sin_ref[...])
            vb = kv[:, kvd + j * LANES:kvd + (j + 1) * LANES]
            for blk, ref in ((kb, k_ref), (vb, v_ref)):
                sw = pltpu.roll(blk, 64, 1)
                ref[2 * j] = jnp.where(low, blk, sw).astype(ref.dtype)
                ref[2 * j + 1] = jnp.where(low, sw, blk).astype(ref.dtype)

    heads = jax.ShapeDtypeStruct((N_KV_HEADS, t, LANES), _ACT_DTYPE)
    hspec = pl.BlockSpec((N_KV_HEADS, tm, LANES), lambda i: (0, i, 0))
    return pl.pallas_call(
        body, name=name, grid=(t // tm,), out_shape=(heads, heads),
        in_specs=[_rows(tm, d_model), _const((1, d_model)), _const(w_kv.shape), _rows(tm, LANES), _rows(tm, LANES)],
        out_specs=(hspec, hspec), compiler_params=_params(),
    )(x, kv_g, w_kv, cos, ssin)


def _kv_bwd(dx, x, kv_g, w_kv, cos, ssin, dks, dvs, name):
    t, d_model = x.shape
    tm = _row_tile(t, PROJ_TILE)
    kvd = w_kv.shape[1] // 2
    pairs = kvd // LANES
    n_users = len(dks)

    def body(dx_ref, x_ref, g_ref, w_ref, cos_ref, sin_ref, *refs):
        dk_refs, dv_refs = refs[:n_users], refs[n_users:2 * n_users]
        dxi_ref, h_ref, dkv_ref, dg_ref = refs[2 * n_users:]
        dk_ref = functools.reduce(lambda a, b: a + b, [r[...] for r in dk_refs])
        dv_ref = functools.reduce(lambda a, b: a + b, [r[...] for r in dv_refs])
        i = pl.program_id(0)

        @pl.when(i == 0)
        def _():
            dg_ref[...] = jnp.zeros_like(dg_ref)

        xv = x_ref[...]
        g = g_ref[...]
        h_ref[...] = _rms_fwd(xv, g).astype(h_ref.dtype)
        low = _low_half((tm, LANES))
        dks, dvs = [], []
        for j in range(pairs):
            dkb = jnp.where(low, dk_ref[2 * j], dk_ref[2 * j + 1])
            dks.append(_rope_bwd(dkb, cos_ref[...], sin_ref[...]))
            dvs.append(jnp.where(low, dv_ref[2 * j], dv_ref[2 * j + 1]))
        dkv = jnp.concatenate(dks + dvs, axis=1).astype(_MXU_DTYPE)
        dkv_ref[...] = dkv
        dxp, dg = _rms_bwd(xv, g, _mm_tb(dkv, w_ref[...]))
        dg_ref[...] += dg
        dxi_ref[...] = dx_ref[...] + dxp

    hspec = pl.BlockSpec((N_KV_HEADS, tm, LANES), lambda i: (0, i, 0))
    return pl.pallas_call(
        body, name=name, grid=(t // tm,),
        out_shape=(jax.ShapeDtypeStruct((t, d_model), F32), jax.ShapeDtypeStruct((t, d_model), _ACT_DTYPE),
                   jax.ShapeDtypeStruct((t, 2 * kvd), _ACT_DTYPE), jax.ShapeDtypeStruct((1, d_model), F32)),
        in_specs=[_rows(tm, d_model), _rows(tm, d_model), _const((1, d_model)), _const(w_kv.shape),
                  _rows(tm, LANES), _rows(tm, LANES)] + [hspec] * (2 * n_users),
        out_specs=(_rows(tm, d_model), _rows(tm, d_model), _rows(tm, 2 * kvd), _const((1, d_model))),
        compiler_params=_params(),
    )(dx, x, kv_g, w_kv, cos, ssin, *dks, *dvs)


def _q_fwd(x, pre_g, w_q, cos, ssin, layer, j, name):
    t, d_model = x.shape
    tm = _row_tile(t, PROJ_TILE)

    def body(x_ref, g_ref, w_ref, cos_ref, sin_ref, q_ref):
        q = _mm(_rms_fwd(x_ref[...], g_ref[...]), w_ref[...])
        for p in range(d_model // LANES):
            cols = slice(p * LANES, (p + 1) * LANES)
            q_ref[:, cols] = (_rope_fwd(q[:, cols], cos_ref[...], sin_ref[...]) * ATTN_SCALE).astype(q_ref.dtype)

    return pl.pallas_call(
        body, name=name, grid=(t // tm,), out_shape=jax.ShapeDtypeStruct((t, d_model), _ACT_DTYPE),
        in_specs=[_rows(tm, d_model), _const((None, 1, d_model), (layer, 0, 0)), _const((None, d_model, d_model), (j, 0, 0)),
                  _rows(tm, LANES), _rows(tm, LANES)],
        out_specs=_rows(tm, d_model), compiler_params=_params(),
    )(x, pre_g, w_q, cos, ssin)


def _q_bwd(dx, dqs, x, pre_g, w_q, cos, ssin, layer, j, name):
    t, d_model = x.shape
    tm = _row_tile(t, PROJ_TILE)

    def body(dx_ref, dq_ref, x_ref, g_ref, w_ref, cos_ref, sin_ref, dxi_ref, h_ref, dqo_ref, dg_ref):
        i = pl.program_id(0)

        @pl.when(i == 0)
        def _():
            dg_ref[...] = jnp.zeros_like(dg_ref)

        xv = x_ref[...]
        g = g_ref[...]
        h_ref[...] = _rms_fwd(xv, g).astype(h_ref.dtype)
        parts = []
        for p in range(d_model // LANES):
            cols = slice(p * LANES, (p + 1) * LANES)
            parts.append(_rope_bwd(dq_ref[:, cols] * ATTN_SCALE, cos_ref[...], sin_ref[...]))
        dq = jnp.concatenate(parts, axis=1).astype(_MXU_DTYPE)
        dqo_ref[...] = dq
        dxp, dg = _rms_bwd(xv, g, _mm_tb(dq, w_ref[...]))
        dg_ref[...] += dg
        dxi_ref[...] = dx_ref[...] + dxp

    act = jax.ShapeDtypeStruct((t, d_model), _ACT_DTYPE)
    return pl.pallas_call(
        body, name=name, grid=(t // tm,),
        out_shape=(jax.ShapeDtypeStruct((t, d_model), F32), act, act, jax.ShapeDtypeStruct((1, d_model), F32)),
        in_specs=[_rows(tm, d_model), _rows(tm, d_model), _rows(tm, d_model), _const((None, 1, d_model), (layer, 0, 0)),
                  _const((None, d_model, d_model), (j, 0, 0)), _rows(tm, LANES), _rows(tm, LANES)],
        out_specs=(_rows(tm, d_model), _rows(tm, d_model), _rows(tm, d_model), _const((1, d_model))),
        compiler_params=_params(),
    )(dx, dqs, x, pre_g, w_q, cos, ssin)


def _stack_heads(pairs):
    low = _low_half(pairs[0].shape)
    zero = jnp.zeros_like(pairs[0])
    return jnp.concatenate([h for blk in pairs for h in (jnp.where(low, blk, zero), jnp.where(low, zero, blk))], axis=0)


def _unstack_heads(stacked, i):
    a, b = stacked[2 * i * BLOCK:(2 * i + 1) * BLOCK], stacked[(2 * i + 1) * BLOCK:(2 * i + 2) * BLOCK]
    return jnp.where(_low_half(a.shape), a, b)


def _attn_scores(q_pairs, k2, n, sinks):
    qst = _stack_heads(q_pairs)
    s = _mm_tb(qst, k2)
    row = lax.broadcasted_iota(jnp.int32, s.shape, 0)
    col = lax.broadcasted_iota(jnp.int32, s.shape, 1)
    rel = BLOCK + (row & (BLOCK - 1)) - col
    valid = (rel >= 0) & (rel < WINDOW) & (n * BLOCK + col - BLOCK >= 0)
    s = jnp.where(valid, s, NEG_INF)
    rows1 = lax.broadcasted_iota(jnp.int32, (s.shape[0], 1), 0)
    sink = jnp.full((s.shape[0], 1), sinks[-1], F32)
    for i in reversed(range(len(sinks) - 1)):
        sink = jnp.where(rows1 < (i + 1) * BLOCK, sinks[i], sink)
    return qst, s, sink


def _attn_fwd(qs, kdup, vdup, sinks, j, name):
    t, d_model = qs.shape
    nb = t // BLOCK
    n_pairs = d_model // LANES
    per_group = n_pairs // N_KV_HEADS

    def body(sink_ref, q_ref, kp_ref, ko_ref, vp_ref, vo_ref, o_ref, lse_ref):
        n = pl.program_id(0)
        lane = _lane_iota((BLOCK, LANES))
        lse = jnp.zeros((BLOCK, LANES), F32)
        for hk in range(N_KV_HEADS):
            pairs = range(hk * per_group, (hk + 1) * per_group)
            heads = range(2 * pairs[0], 2 * pairs[-1] + 2)
            k2 = jnp.concatenate([kp_ref[hk], ko_ref[hk]], axis=0)
            v2 = jnp.concatenate([vp_ref[hk], vo_ref[hk]], axis=0)
            _, s, sink = _attn_scores([q_ref[:, p * LANES:(p + 1) * LANES] for p in pairs], k2, n,
                                      [sink_ref[j, h] for h in heads])
            m = jnp.maximum(jnp.max(s, axis=-1, keepdims=True), sink)
            pe = jnp.exp(s - m)
            denom = jnp.sum(pe, axis=-1, keepdims=True) + jnp.exp(sink - m)
            o2 = _mm(pe, v2) / denom
            l2 = m + jnp.log(denom)
            for i, p in enumerate(pairs):
                o_ref[:, p * LANES:(p + 1) * LANES] = _unstack_heads(o2, i).astype(o_ref.dtype)
            for i, h in enumerate(heads):
                lse = jnp.where(lane == h, l2[i * BLOCK:(i + 1) * BLOCK], lse)
        lse_ref[...] = lse

    prev = pl.BlockSpec((N_KV_HEADS, BLOCK, LANES), lambda n: (0, jnp.maximum(n - 1, 0), 0))
    own = pl.BlockSpec((N_KV_HEADS, BLOCK, LANES), lambda n: (0, n, 0))
    return pl.pallas_call(
        body, name=name, grid=(nb,),
        out_shape=(jax.ShapeDtypeStruct((t, d_model), _ACT_DTYPE), jax.ShapeDtypeStruct((t, LANES), F32)),
        in_specs=[pl.BlockSpec(memory_space=pltpu.SMEM), _rows(BLOCK, d_model), prev, own, prev, own],
        out_specs=(_rows(BLOCK, d_model), _rows(BLOCK, LANES)), compiler_params=_params(),
    )(sinks, qs, kdup, kdup, vdup, vdup)


def _attn_bwd(qs, kdup, vdup, sinks, lse, do, j, name):
    t, d_model = qs.shape
    nb = t // BLOCK
    n_pairs = d_model // LANES
    per_group = n_pairs // N_KV_HEADS
    rev = lambda n: nb - 1 - n

    def body(sink_ref, q_ref, kp_ref, ko_ref, vp_ref, vo_ref, lse_ref, do_ref, dq_ref, dk_ref, dv_ref, ds_ref, ck, cv):
        i = pl.program_id(0)
        n = nb - 1 - i

        @pl.when(i == 0)
        def _():
            ck[...] = jnp.zeros_like(ck)
            cv[...] = jnp.zeros_like(cv)
            ds_ref[...] = jnp.zeros_like(ds_ref)

        lane = _lane_iota((BLOCK, LANES))
        lane1 = _lane_iota((1, LANES))
        lsev = lse_ref[...]
        dsink = jnp.zeros((1, LANES), F32)
        for hk in range(N_KV_HEADS):
            pairs = range(hk * per_group, (hk + 1) * per_group)
            heads = range(2 * pairs[0], 2 * pairs[-1] + 2)
            k2 = jnp.concatenate([kp_ref[hk], ko_ref[hk]], axis=0)
            v2 = jnp.concatenate([vp_ref[hk], vo_ref[hk]], axis=0)
            qst, s, sink = _attn_scores([q_ref[:, p * LANES:(p + 1) * LANES] for p in pairs], k2, n,
                                        [sink_ref[j, h] for h in heads])
            l2 = jnp.concatenate([jnp.sum(jnp.where(lane == h, lsev, 0.0), axis=-1, keepdims=True) for h in heads], axis=0)
            pn = jnp.exp(s - l2)
            dost = _stack_heads([do_ref[:, p * LANES:(p + 1) * LANES] for p in pairs])
            dp = _mm_tb(dost, v2)
            dr = jnp.sum(pn * dp, axis=-1, keepdims=True)
            dsm = (pn * (dp - dr)).astype(_MXU_DTYPE)
            dsk = -jnp.exp(sink - l2) * dr
            for i, h in enumerate(heads):
                dsink = dsink + jnp.where(lane1 == h, jnp.sum(dsk[i * BLOCK:(i + 1) * BLOCK]), 0.0)
            dq2 = _mm(dsm, k2)
            for i, p in enumerate(pairs):
                dq_ref[:, p * LANES:(p + 1) * LANES] = _unstack_heads(dq2, i)
            for acc, carry, ref in ((_mm_ta(dsm, qst), ck, dk_ref), (_mm_ta(pn, dost), cv, dv_ref)):
                folded = acc + pltpu.roll(acc, 64, 1)
                ref[hk] = folded[BLOCK:] + carry[hk]
                carry[hk] = folded[:BLOCK]
        ds_ref[...] += dsink

    prev = pl.BlockSpec((N_KV_HEADS, BLOCK, LANES), lambda n: (0, jnp.maximum(rev(n) - 1, 0), 0))
    own = pl.BlockSpec((N_KV_HEADS, BLOCK, LANES), lambda n: (0, rev(n), 0))
    rows = lambda cols: pl.BlockSpec((BLOCK, cols), lambda n: (rev(n), 0))
    heads = jax.ShapeDtypeStruct((N_KV_HEADS, t, LANES), F32)
    return pl.pallas_call(
        body, name=name, grid=(nb,),
        out_shape=(jax.ShapeDtypeStruct((t, d_model), F32), heads, heads, jax.ShapeDtypeStruct((1, LANES), F32)),
        in_specs=[pl.BlockSpec(memory_space=pltpu.SMEM), rows(d_model), prev, own, prev, own, rows(LANES), rows(d_model)],
        out_specs=(rows(d_model), own, own, _const((1, LANES))),
        scratch_shapes=[pltpu.VMEM((N_KV_HEADS, BLOCK, LANES), F32), pltpu.VMEM((N_KV_HEADS, BLOCK, LANES), F32)],
        compiler_params=_params(),
    )(sinks, qs, kdup, kdup, vdup, vdup, lse, do)


def _oproj_fwd(x, o, w_o, post_g, layer, j, name):
    t, d_model = x.shape
    tm = _row_tile(t, PROJ_TILE)

    def body(x_ref, o_ref, w_ref, g_ref, xo_ref, mo_ref):
        mo = _mm(o_ref[...], w_ref[...])
        mo_ref[...] = mo
        xo_ref[...] = x_ref[...] + _rms_fwd(mo, g_ref[...])

    full = jax.ShapeDtypeStruct((t, d_model), F32)
    return pl.pallas_call(
        body, name=name, grid=(t // tm,), out_shape=(full, full),
        in_specs=[_rows(tm, d_model), _rows(tm, d_model), _const((None, d_model, d_model), (j, 0, 0)),
                  _const((None, 1, d_model), (layer, 0, 0))],
        out_specs=(_rows(tm, d_model), _rows(tm, d_model)), compiler_params=_params(),
    )(x, o, w_o, post_g)


def _oproj_bwd(dx, mo, w_o, post_g, layer, j, name):
    t, d_model = dx.shape
    tm = _row_tile(t, PROJ_TILE)

    def body(dx_ref, mo_ref, w_ref, g_ref, do_ref, dmo_ref, dg_ref):
        i = pl.program_id(0)

        @pl.when(i == 0)
        def _():
            dg_ref[...] = jnp.zeros_like(dg_ref)

        dmo, dg = _rms_bwd(mo_ref[...], g_ref[...], dx_ref[...])
        dg_ref[...] += dg
        dmo = dmo.astype(_MXU_DTYPE)
        dmo_ref[...] = dmo
        do_ref[...] = _mm_tb(dmo, w_ref[...]).astype(do_ref.dtype)

    act = jax.ShapeDtypeStruct((t, d_model), _ACT_DTYPE)
    return pl.pallas_call(
        body, name=name, grid=(t // tm,), out_shape=(act, act, jax.ShapeDtypeStruct((1, d_model), F32)),
        in_specs=[_rows(tm, d_model), _rows(tm, d_model), _const((None, d_model, d_model), (j, 0, 0)),
                  _const((None, 1, d_model), (layer, 0, 0))],
        out_specs=(_rows(tm, d_model), _rows(tm, d_model), _const((1, d_model))), compiler_params=_params(),
    )(dx, mo, w_o, post_g)


def _loss_grad(y, target, name):
    t, d_model = y.shape
    tm = _row_tile(t, PROJ_TILE)

    def body(y_ref, t_ref, dy_ref, loss_ref):
        i = pl.program_id(0)

        @pl.when(i == 0)
        def _():
            loss_ref[...] = jnp.zeros_like(loss_ref)

        err = y_ref[...] - t_ref[...]
        dy_ref[...] = err / d_model
        loss_ref[...] += 0.5 * jnp.sum(jnp.mean(err * err, axis=-1, keepdims=True), axis=0, keepdims=True)

    return pl.pallas_call(
        body, name=name, grid=(t // tm,),
        out_shape=(jax.ShapeDtypeStruct((t, d_model), F32), jax.ShapeDtypeStruct((1, 1), F32)),
        in_specs=[_rows(tm, d_model), _rows(tm, d_model)], out_specs=(_rows(tm, d_model), _const((1, 1))),
        compiler_params=_params(),
    )(y, target)


def _mesh_position():
    return lax.axis_index("x"), lax.axis_index("y"), lax.axis_index("c")


def _block_of(px, py, pc):
    return 4 * px + 2 * py + pc


def _at_block(ref, axis, block):
    return ref.at[(slice(None),) * axis + (block,)]


def _all_gather(shards, axes, name):
    n = len(shards)

    def body(*refs):
        srcs, outs = refs[:n], refs[n:2 * n]
        send_sems, recv_sems, local_sems = refs[2 * n:]
        x, y, c = _mesh_position()
        me, sibling = (x, y, c), (x, y, 1 - c)
        chips = [(1 - x, y), (x, 1 - y), (1 - x, 1 - y)]

        def blk(i, pos):
            return _at_block(outs[i], axes[i], _block_of(*pos))

        def copy(i, k, block, to, src=None):
            return pltpu.make_async_remote_copy(
                src_ref=blk(i, block) if src is None else src, dst_ref=blk(i, block),
                send_sem=send_sems.at[i, k], recv_sem=recv_sems.at[i, k], device_id=to, device_id_type=MESH)

        mine = [pltpu.make_async_copy(srcs[i], blk(i, me), local_sems.at[i]) for i in range(n)]
        for cp in mine:
            cp.start()
        sent = []
        for i in range(n):
            sent += [copy(i, 1 + k, me, (*chip, c), src=srcs[i]) for k, chip in enumerate(chips)]
            sent.append(copy(i, 0, me, sibling, src=srcs[i]))
        for cp in sent:
            cp.start()
        for i in range(n):
            for k, chip in enumerate(chips):
                copy(i, 1 + k, (*chip, c), me).wait_recv()
                passed = copy(i, 4 + k, (*chip, c), sibling)
                passed.start()
                sent.append(passed)
        for i in range(n):
            copy(i, 0, sibling, me).wait_recv()
            for k, chip in enumerate(chips):
                copy(i, 4 + k, (*chip, 1 - c), me).wait_recv()
        for cp in sent:
            cp.wait_send()
        for cp in mine:
            cp.wait()

    hbm = pl.BlockSpec(memory_space=pl.ANY)
    return pl.pallas_call(
        body, name=name,
        out_shape=tuple(jax.ShapeDtypeStruct(s.shape[:a] + (N_DEV,) + s.shape[a:], s.dtype) for s, a in zip(shards, axes)),
        in_specs=[hbm] * n, out_specs=(hbm,) * n,
        scratch_shapes=[pltpu.SemaphoreType.DMA((n, 7)), pltpu.SemaphoreType.DMA((n, 7)), pltpu.SemaphoreType.DMA((n,))],
    )(*shards)


GATHER, SCATTER, GATHER_CHIPS, GATHER_SIBLING = "gather", "scatter", "gather_chips", "gather_sibling"
COPIES = {GATHER: N_DEV - 1, SCATTER: N_DEV - 1, GATHER_CHIPS: 4, GATHER_SIBLING: 3}


def _land_shape(kind, s, axis):
    if kind == SCATTER:
        return (N_DEV,) + s.shape[:axis] + s.shape[axis + 1:]
    return s.shape[:axis] + (N_DEV,) + s.shape[axis:]


def _plan(kind, srcs, lands, axes):
    x, y, c = _mesh_position()
    my_block = _block_of(x, y, c)
    flips = {GATHER_CHIPS: (1, 4, 2, 6), GATHER_SIBLING: (4, 2, 6)}.get(kind, range(1, N_DEV))
    others = [(1 - x if k & 4 else x, 1 - y if k & 2 else y, 1 - c if k & 1 else c) for k in flips]
    remote = []
    for src, land, axis in zip(srcs, lands, axes):
        if kind == SCATTER:
            mine = land.at[my_block]
            remote += [(_at_block(src, axis, _block_of(*peer)), mine, peer, land.at[_block_of(*peer)]) for peer in others]
        elif kind == GATHER_SIBLING:
            for px, py, _ in others:
                mine, theirs = _at_block(land, axis, _block_of(px, py, c)), _at_block(land, axis, _block_of(px, py, 1 - c))
                remote.append((mine, mine, (x, y, 1 - c), theirs))
        else:
            mine = _at_block(land, axis, my_block)
            remote += [(src, mine, peer, _at_block(land, axis, _block_of(*peer))) for peer in others]
    return remote


def _remote(src, dst, send_sems, recv_sems, k, peer):
    return pltpu.make_async_remote_copy(src_ref=src, dst_ref=dst, send_sem=send_sems.at[k], recv_sem=recv_sems.at[k],
                                        device_id=peer, device_id_type=MESH)


_HBM = pl.BlockSpec(memory_space=pltpu.HBM)
_SEM = pl.BlockSpec(memory_space=pltpu.SEMAPHORE)
_SPLIT = dict(has_side_effects=pltpu.SideEffectType.DATAFLOW_SIDE_EFFECTING)


def _landing_zone(kind, s, axis, me):
    land = lax.empty(_land_shape(kind, s, axis), s.dtype)
    if kind == SCATTER:
        return lax.dynamic_update_slice_in_dim(land, lax.dynamic_slice_in_dim(s, me, 1, axis).reshape((1,) + land.shape[1:]), me, 0)
    return lax.dynamic_update_slice_in_dim(land, jnp.expand_dims(s, axis), me, axis)


def _exchange_start(kind, arrays, axes, after, name):
    n = len(arrays)
    if kind == GATHER_SIBLING:
        passed = list(arrays)
    else:
        me = _block_of(*_mesh_position())
        passed = list(arrays) + [_landing_zone(kind, s, a, me) for s, a in zip(arrays, axes)]
    n_sems = n * COPIES[kind]

    def body(*refs):
        land_refs = refs[len(passed) - n:len(passed)]
        send_sems, recv_sems = refs[len(passed) + 1], refs[len(passed) + 2]
        token = refs[-1]
        for k, (src, dst, peer, _) in enumerate(_plan(kind, refs[:n], land_refs, axes)):
            _remote(src, dst, send_sems, recv_sems, k, peer).start()
        token[...] = jnp.zeros_like(token)

    out = pl.pallas_call(
        body, name=name,
        out_shape=(pltpu.SemaphoreType.DMA((n_sems,)), pltpu.SemaphoreType.DMA((n_sems,)),
                   *[pltpu.HBM(a.shape, a.dtype) for a in passed], jax.ShapeDtypeStruct((8, LANES), F32)),
        in_specs=[_HBM] * len(passed) + [pl.BlockSpec(memory_space=pl.ANY)],
        out_specs=(_SEM, _SEM, *[_HBM] * len(passed), pl.BlockSpec(memory_space=pltpu.VMEM)),
        input_output_aliases={i: 2 + i for i in range(len(passed))},
        compiler_params=pltpu.CompilerParams(**_SPLIT),
    )(*[pltpu.with_memory_space_constraint(a, pltpu.HBM) for a in passed], after)
    return (kind, axes, n, out[:-1]), out[-1]


def _exchange_wait(handle, after, name):
    kind, axes, n, (send_sems, recv_sems, *thru) = handle

    def body(*refs):
        land_refs = refs[len(thru) - n:len(thru)]
        send_sems, recv_sems = refs[len(thru)], refs[len(thru) + 1]
        for k, (src, _, peer, arrives) in enumerate(_plan(kind, refs[:n], land_refs, axes)):
            cp = _remote(src, arrives, send_sems, recv_sems, k, peer)
            cp.wait_send()
            cp.wait_recv()

    out = pl.pallas_call(
        body, name=name,
        out_shape=tuple(pltpu.HBM(a.shape, a.dtype) for a in thru),
        in_specs=[_HBM] * len(thru) + [_SEM, _SEM, pl.BlockSpec(memory_space=pl.ANY)], out_specs=(_HBM,) * len(thru),
        input_output_aliases={i: i for i in range(len(thru))},
        compiler_params=pltpu.CompilerParams(**_SPLIT),
    )(*thru, send_sems, recv_sems, after)
    return out[len(thru) - n:]


def _adamw_math(w, g, m, v):
    m = ADAM_B1 * m + (1.0 - ADAM_B1) * g
    v = ADAM_B2 * v + (1.0 - ADAM_B2) * jnp.square(g)
    m_hat = m / (1.0 - ADAM_B1 ** ADAM_STEP)
    v_hat = v / (1.0 - ADAM_B2 ** ADAM_STEP)
    delta = -ADAM_LR * (m_hat / (jnp.sqrt(v_hat) + ADAM_EPS) + ADAM_WD * w)
    return delta, m, v


def _update_tile(rows):
    if rows <= 512:
        return rows
    for tr in (512, 384, 352, 256, 176, 128, 64, 32, 16):
        if rows % tr == 0:
            return tr
    raise ValueError(f"{rows} rows do not tile")


def _adamw(parts, w, m, v, slab, so_far, name):
    rows, c = w.shape
    r = parts.shape[1]
    tr = _update_tile(r)
    first = slab * (r // tr)
    if so_far is None:
        so_far = tuple(lax.empty((rows, c), F32) for _ in range(4))

    def body(p_ref, w_ref, m_ref, v_ref, *refs):
        g_ref, d_ref, mo_ref, vo_ref = refs[4:]
        g = p_ref[0].astype(F32)
        for s in range(1, N_DEV):
            g = g + p_ref[s].astype(F32)
        g_ref[...] = g
        d_ref[...], mo_ref[...], vo_ref[...] = _adamw_math(w_ref[...], g, m_ref[...], v_ref[...])

    out = jax.ShapeDtypeStruct((rows, c), F32)
    tile = pl.BlockSpec((tr, c), lambda i: (first + i, 0))
    return pl.pallas_call(
        body, name=name, grid=(r // tr,), out_shape=(out,) * 4,
        in_specs=[pl.BlockSpec((N_DEV, tr, c), lambda i: (0, i, 0))] + [tile] * 3 + [pl.BlockSpec(memory_space=pl.ANY)] * 4,
        out_specs=(tile,) * 4, input_output_aliases={4 + k: k for k in range(4)}, compiler_params=_params(),
    )(parts, w, m, v, *so_far)


def _adamw_small(parts, picks, weights, name):
    n = len(parts)

    def body(*refs):
        p_refs, wmv, outs = refs[:n], refs[n:4 * n], refs[4 * n:]
        me = _block_of(*_mesh_position())
        for i in range(n):
            g = picks[i](p_refs[i], 0, me)
            for s in range(1, N_DEV):
                g = g + picks[i](p_refs[i], s, me)
            w_ref, m_ref, v_ref = wmv[3 * i:3 * i + 3]
            g_ref, d_ref, mo_ref, vo_ref = outs[4 * i:4 * i + 4]
            g_ref[...] = g
            d_ref[...], mo_ref[...], vo_ref[...] = _adamw_math(w_ref[...], g, m_ref[...], v_ref[...])

    flat = [a for wmv in weights for a in wmv]
    out = pl.pallas_call(
        body, name=name,
        out_shape=tuple(jax.ShapeDtypeStruct(w.shape, F32) for w, _, _ in weights for _ in range(4)),
        compiler_params=pltpu.CompilerParams(vmem_limit_bytes=VMEM_LIMIT),
    )(*parts, *flat)
    return [tuple(out[4 * i:4 * i + 4]) for i in range(n)]


def kernel(x, positions, mix_pre_g, mix_post_g, pool_w, pool_scale, kv_norm_g, w_kv, w_q, w_o, sinks, ffn_pre_g, ffn_post_g, ffn_w_in, ffn_conv_w, ffn_conv_b, ffn_w_out, loss_target, m_mix_pre_g, m_mix_post_g, m_pool_w, m_pool_scale, m_kv_norm_g, m_w_kv, m_w_q, m_w_o, m_sinks, m_ffn_pre_g, m_ffn_post_g, m_ffn_w_in, m_ffn_conv_w, m_ffn_conv_b, m_ffn_w_out, v_mix_pre_g, v_mix_post_g, v_pool_w, v_pool_scale, v_kv_norm_g, v_w_kv, v_w_q, v_w_o, v_sinks, v_ffn_pre_g, v_ffn_post_g, v_ffn_w_in, v_ffn_conv_w, v_ffn_conv_b, v_ffn_w_out):
    depth, d_model = mix_pre_g.shape
    n_a = pool_w.shape[0]
    n_b = w_q.shape[0]
    t = x.shape[1]
    fs = ffn_w_in.shape[2]
    half = N_DEV // 2
    n_heads = d_model // HEAD_DIM
    x0 = x.reshape(t, d_model)
    target = loss_target.reshape(t, d_model)

    inv_freq = 1.0 / (ROPE_THETA ** (jnp.arange(0, HEAD_DIM, 2, dtype=F32) / HEAD_DIM))
    ang = positions.reshape(t).astype(F32)[:, None] * inv_freq
    cos, sin = jnp.cos(ang), jnp.sin(ang)
    cos = jnp.tile(cos, (1, 2 * LANES // HEAD_DIM))
    ssin = jnp.tile(jnp.concatenate([-sin, sin], axis=1), (1, LANES // HEAD_DIM))

    wire = lambda a: a.astype(_WIRE_DTYPE)
    w_in_b, w_out_b = wire(ffn_w_in), wire(ffn_w_out)
    pool_w_g, w_in_0, w_out_0, pool_scale_g, conv_w_g = _all_gather(
        [wire(pool_w), w_in_b[:1], w_out_b[:1], pool_scale, ffn_conv_w], [2, 1, 1, 0, 0], "gather_first")
    groups = []
    for l in range(1, depth):
        if l == n_a:
            groups.append(("attn", [wire(w_kv), wire(w_q), wire(w_o)], [0, 1, 1], l))
        groups.append((l, [w_in_b[l:l + 1], w_out_b[l:l + 1]], [1, 1], l))
    over_ici, to_sibling, tokens, after = {}, {}, [], w_in_0
    for key, shards, axes, _ in groups:
        over_ici[key], after = _exchange_start(GATHER_CHIPS, shards, axes, after, f"gather_chips_{key}")
        tokens.append(after)
    started = functools.reduce(lambda a, b: a + b, [tk[0, 0] for tk in tokens])

    def pass_on(layer, after):
        sent = jnp.zeros((), F32)
        for key, _, axes, first in groups:
            if first == layer:
                lands = _exchange_wait(over_ici[key], after, f"gather_chips_wait_{key}")
                to_sibling[key], tk = _exchange_start(GATHER_SIBLING, lands, axes, after, f"gather_sibling_{key}")
                sent = sent + tk[0, 0]
        return sent

    w_in_l, w_out_l = {0: w_in_0}, {0: w_out_0.reshape(1, half * fs, d_model)}
    pool_scale_f = pool_scale_g.transpose(1, 0, 2).reshape(n_a, 1, d_model)
    conv_w_f = conv_w_g.transpose(1, 2, 0, 3)
    pool_w_f = pool_w_g.reshape(n_a, len(POOL_WINDOWS), d_model // len(POOL_WINDOWS), -1)
    conv_b_f = ffn_conv_b.reshape(depth, N_DEV, fs)
    g3 = lambda a: a.reshape(a.shape[0], 1, a.shape[1])
    mix_pre, mix_post, ffn_pre, ffn_post = g3(mix_pre_g) + started, g3(mix_post_g), g3(ffn_pre_g), g3(ffn_post_g)
    kv_g = kv_norm_g.reshape(1, d_model)
    w_kv_f = w_q_f = w_o_f = None

    saved = []
    xc = x0
    kdup = vdup = x_kv = None
    for l in range(depth):
        x_in = xc
        if l < n_a:
            x_mid, dsave, yu = _pool_fwd(x_in, mix_pre, mix_post, pool_w_f, pool_scale_f, l, f"pool_fwd_{l}")
            mixer = (dsave, yu)
        else:
            j = l - n_a
            if j == 0:
                x_kv = x_in
                w_kv_g, w_q_g, w_o_g = _exchange_wait(to_sibling["attn"], x_in, "gather_sibling_wait_attn")
                w_kv_f = w_kv_g.reshape(d_model, -1)
                w_q_f = w_q_g.reshape(n_b, d_model, d_model)
                w_o_f = w_o_g.reshape(n_b, d_model, d_model)
                kdup, vdup = _kv_fwd(x_kv, kv_g, w_kv_f, cos, ssin, "kv_fwd")
            qs = _q_fwd(x_in, mix_pre, w_q_f, cos, ssin, l, j, f"q_fwd_{l}")
            o, lse = _attn_fwd(qs, kdup, vdup, sinks, j, f"attn_fwd_{l}")
            x_mid, mo = _oproj_fwd(x_in, o, w_o_f, mix_post, l, j, f"oproj_fwd_{l}")
            mixer = (qs, o, lse, mo)
        if l > 0:
            w_in_l[l], w_out_g = _exchange_wait(to_sibling[l], x_mid, f"gather_sibling_wait_{l}")
            w_out_l[l] = w_out_g.reshape(1, half * fs, d_model)
        xc, u, uc, f = _ffn_fwd(x_mid, ffn_pre, ffn_post, w_in_l[l], conv_w_f, conv_b_f, w_out_l[l], l, 0, f"ffn_fwd_{l}")
        saved.append((x_in, x_mid, u, uc, f, mixer))
        if l + 1 < depth:
            mix_pre = mix_pre + pass_on(l + 1, xc)

    dx, loss_part = _loss_grad(xc, target, "loss")
    loss = lax.psum(loss_part[0, 0], AXES)

    gconv_w, gconv_b = [None] * depth, [None] * depth
    gmix_pre, gmix_post, gffn_pre, gffn_post = [None] * depth, [None] * depth, [None] * depth, [None] * depth
    gpool_scale, gsinks = [None] * n_a, [None] * n_b
    dks, dvs = [], []
    gkv_g = None
    whole = lambda cols: pl.BlockSpec((t, cols), lambda c: (0, 0), pipeline_mode=pl.Buffered(1))
    per_out = lambda cols: pl.BlockSpec((None, t, cols), lambda c: (c, 0, 0))
    by_dev = lambda g: g.reshape(N_DEV, -1, g.shape[-1])
    def small_grads():
        cat = lambda rows: jnp.concatenate(rows, axis=0)
        row = lambda a: a.reshape(1, -1)
        everything = lambda ref, s, me: ref[s]
        lanes = pool_scale.shape[1]
        return [("mix_pre_g", cat(gmix_pre), everything, (mix_pre_g, m_mix_pre_g, v_mix_pre_g)),
                ("mix_post_g", cat(gmix_post), everything, (mix_post_g, m_mix_post_g, v_mix_post_g)),
                ("kv_norm_g", gkv_g, everything, (row(kv_norm_g), row(m_kv_norm_g), row(v_kv_norm_g))),
                ("sinks", cat(gsinks), lambda ref, s, me: ref[s, :, pl.ds(0, n_heads)], (sinks, m_sinks, v_sinks)),
                ("ffn_pre_g", cat(gffn_pre), everything, (ffn_pre_g, m_ffn_pre_g, v_ffn_pre_g)),
                ("ffn_post_g", cat(gffn_post), everything, (ffn_post_g, m_ffn_post_g, v_ffn_post_g)),
                ("ffn_conv_b", jnp.stack(gconv_b).reshape(depth, N_DEV * fs), everything, (ffn_conv_b, m_ffn_conv_b, v_ffn_conv_b)),
                ("pool_scale", cat(gpool_scale), lambda ref, s, me: ref[s, :, pl.ds(pl.multiple_of(me * lanes, lanes), lanes)],
                 (pool_scale, m_pool_scale, v_pool_scale)),
                ("ffn_conv_w", jnp.stack(gconv_w).transpose(0, 2, 1, 3), lambda ref, s, me: ref[s, :, me],
                 (ffn_conv_w, m_ffn_conv_w, v_ffn_conv_w))]

    flying = []

    def launch(going, after, name):
        handle, token = _exchange_start(SCATTER, [g for _, _, g, _ in going], [a for _, _, _, a in going], after, name)
        flying.append(([(nm, slab) for nm, slab, _, _ in going], handle))
        return token

    post = mix_post
    ffn_post_b = ffn_post
    token = None
    for l in reversed(range(depth)):
        x_in, x_mid, u, uc, f, mixer = saved[l]
        duc, gact, df, gconv_b[l], gffn_post[l] = _ffn_bwd_out(dx, f, ffn_post_b, uc, conv_b_f, w_out_l[l], l, 0, f"ffn_bwd_out_{l}")
        dx, du, hf, gconv_w[l], gffn_pre[l] = _ffn_bwd_in(dx, x_mid, ffn_pre, duc, u, conv_w_f, w_in_l[l], l, 0, f"ffn_bwd_in_{l}")
        gin = _tn_matmul(hf, du, whole(d_model), per_out(fs), N_DEV, d_model, fs, f"grad_w_in_{l}", _WIRE_DTYPE)
        going = [("ffn_w_in", l, gin, 0)]
        if l == 0:
            token = launch(going, dx, "scatter_start_0_in")
            going = []
            post = post + token[0, 0]
        gout = _tn_matmul(gact, df, per_out(fs), whole(d_model), half, fs, d_model, f"grad_w_out_{l}", _WIRE_DTYPE)
        going.append(("ffn_w_out", l, by_dev(gout), 0))
        if l < n_a:
            dsave, yu = mixer
            dx, dyu, gpool_scale[l], gmix_pre[l], gmix_post[l] = _pool_bwd(
                dx, x_in, mix_pre, post, dsave, yu, pool_w_f, pool_scale_f, l, f"pool_bwd_{l}")
            gc = d_model // len(POOL_WINDOWS)
            by_group = pl.BlockSpec((t, gc), lambda c: (0, c))
            gpool = _tn_matmul(dsave, dyu, by_group, by_group, len(POOL_WINDOWS), gc, gc, f"grad_pool_w_{l}", _WIRE_DTYPE)
            going.append(("pool_w", l, gpool.reshape(len(POOL_WINDOWS), N_DEV, -1, gc), 1))
        else:
            j = l - n_a
            qs, o, lse, mo = mixer
            do, dmo, gmix_post[l] = _oproj_bwd(dx, mo, w_o_f, post, l, j, f"oproj_bwd_{l}")
            dqs, dk, dv, gsinks[j] = _attn_bwd(qs, kdup, vdup, sinks, lse, do, j, f"attn_bwd_{l}")
            dks.append(dk)
            dvs.append(dv)
            dx, hq, dq, gmix_pre[l] = _q_bwd(dx, dqs, x_in, mix_pre, w_q_f, cos, ssin, l, j, f"q_bwd_{l}")
            if j == 0:
                dx, hkv, dkv, gkv_g = _kv_bwd(dx, x_kv, kv_g, w_kv_f, cos, ssin, dks, dvs, "kv_bwd")
                gkv = _tn_matmul(hkv, dkv, whole(d_model), whole(dkv.shape[1]), 1, d_model, dkv.shape[1], "grad_w_kv", _WIRE_DTYPE)
                going.append(("w_kv", 0, by_dev(gkv), 0))
            go = _tn_matmul(o, dmo, whole(d_model), whole(d_model), 1, d_model, d_model, f"grad_w_o_{l}", _WIRE_DTYPE)
            gq = _tn_matmul(hq, dq, whole(d_model), whole(d_model), 1, d_model, d_model, f"grad_w_q_{l}", _WIRE_DTYPE)
            going += [("w_o", j, by_dev(go), 0), ("w_q", j, by_dev(gq), 0)]
        after = dx
        if l == 0:
            small = small_grads()
            small_flight, after = _exchange_start(GATHER, [g for _, g, _, _ in small], [0] * len(small), dx, "gather_small_grads")
        token = launch(going, after, f"scatter_start_{l}")
        ffn_post_b = ffn_post_b + token[0, 0]

    grad_x = dx.reshape(x.shape)

    shard = {"pool_w": (pool_w, m_pool_w, v_pool_w), "w_kv": (w_kv, m_w_kv, v_w_kv), "w_q": (w_q, m_w_q, v_w_q),
             "w_o": (w_o, m_w_o, v_w_o), "ffn_w_in": (ffn_w_in, m_ffn_w_in, v_ffn_w_in),
             "ffn_w_out": (ffn_w_out, m_ffn_w_out, v_ffn_w_out)}
    big = {}

    def arrive(flights, after):
        for idx, (names, handle) in flights:
            parts = _exchange_wait(handle, after, f"scatter_wait_{idx}")
            for (nm, slab), p in zip(names, parts):
                cols = p.shape[-1]
                w2, m2, v2 = (a.reshape(-1, cols) for a in shard[nm])
                big[nm] = _adamw(p.reshape(N_DEV, -1, cols), w2, m2, v2, slab, big.get(nm), f"adamw_{nm}_{slab}")
                after = big[nm][0]
        return after

    done = arrive(list(enumerate(flying)), token)
    small_parts = _exchange_wait(small_flight, done, "gather_small_grads_wait")
    upd = _adamw_small(small_parts, [pick for _, _, pick, _ in small], [wmv for _, _, _, wmv in small], "adamw_small")
    res = {nm: tuple(r.reshape(shard[nm][0].shape) for r in out) for nm, out in big.items()}
    for (nm, _, _, _), out in zip(small, upd):
        res[nm] = tuple(a.reshape(kv_norm_g.shape) for a in out) if nm == "kv_norm_g" else out

    order = ["mix_pre_g", "mix_post_g", "pool_w", "pool_scale", "kv_norm_g", "w_kv", "w_q", "w_o", "sinks", "ffn_pre_g",
             "ffn_post_g", "ffn_w_in", "ffn_conv_w", "ffn_conv_b", "ffn_w_out"]
    return (loss, grad_x, *[res[nm][0] for nm in order], *[res[nm][1] for nm in order],
            *[res[nm][2] for nm in order], *[res[nm][3] for nm in order])
```

```python
import functools
import math

import jax
import jax.numpy as jnp
from jax import lax
from jax.experimental import pallas as pl
from jax.experimental.pallas import tpu as pltpu

F32 = jnp.float32
_MXU_DTYPE = jnp.bfloat16
_ACT_DTYPE = jnp.bfloat16
_WIRE_DTYPE = jnp.bfloat16
_SAVE_DTYPE = jnp.bfloat16

N_DEV = 8
POOL_WINDOWS = (2, 4, 8, 16)
POOL_HALO = 32
assert POOL_WINDOWS == tuple(2 ** (g + 1) for g in range(len(POOL_WINDOWS))) and 8 * len(POOL_WINDOWS) <= POOL_HALO
HEAD_DIM = 64
N_KV_HEADS = 4
WINDOW = 128
BLOCK = 128
LANES = 128
ROPE_THETA = 10000.0
ATTN_SCALE = 1.0 / math.sqrt(HEAD_DIM)
NEG_INF = -1e30
RMS_EPS = 1e-6
CONV_HALO = 8
SAVE_HALO = 16
PROJ_TILE = 512
ADAM_LR = 0.001
ADAM_B1 = 0.9
ADAM_B2 = 0.999
ADAM_EPS = 1e-08
ADAM_WD = 0.01
ADAM_STEP = 10
VMEM_LIMIT = 56 * 1024 * 1024
MESH = pl.DeviceIdType.MESH


def _params(n_axes=1, vmem=VMEM_LIMIT):
    return pltpu.CompilerParams(dimension_semantics=("arbitrary",) * n_axes, vmem_limit_bytes=vmem)


def _resident(shape, index):
    return pl.BlockSpec(shape, lambda *_: index, pipeline_mode=pl.Buffered(1))


def _const(shape, index=None):
    index = (0,) * len(shape) if index is None else index
    return pl.BlockSpec(shape, lambda *_: index)


def _rows(tm, cols):
    return pl.BlockSpec((tm, cols), lambda i: (i, 0))


def _row_tile(t, most=256):
    for tm in (512, 256, 128, 64, 32, 16, 8):
        if tm <= most and t % tm == 0:
            return tm
    raise ValueError(f"sequence length {t} is not a multiple of 8")


def _mm(a, b):
    return jnp.dot(a.astype(_MXU_DTYPE), b.astype(_MXU_DTYPE), preferred_element_type=F32)


def _mm_tb(a, b):
    return lax.dot_general(a.astype(_MXU_DTYPE), b.astype(_MXU_DTYPE), (((1,), (1,)), ((), ())),
                           preferred_element_type=F32)


def _mm_ta(a, b):
    return lax.dot_general(a.astype(_MXU_DTYPE), b.astype(_MXU_DTYPE), (((0,), (0,)), ((), ())),
                           preferred_element_type=F32)


def _rms_r(x):
    return lax.rsqrt(jnp.mean(x * x, axis=-1, keepdims=True) + RMS_EPS)


def _rms_fwd(x, g):
    return (x * _rms_r(x)) * g


def _rms_bwd(x, g, dy):
    r = _rms_r(x)
    xh = x * r
    dg = jnp.sum(dy * xh, axis=0, keepdims=True)
    dxh = dy * g
    dx = r * (dxh - xh * jnp.mean(dxh * xh, axis=-1, keepdims=True))
    return dx, dg


_GELU_C = math.sqrt(2.0 / math.pi)


def _gelu_parts(z):
    z2 = z * z
    t = jnp.tanh(_GELU_C * (z + 0.044715 * (z2 * z)))
    cdf = 0.5 * (1.0 + t)
    dz = cdf + z * (0.5 * (1.0 - t * t)) * (_GELU_C * (1.0 + (3 * 0.044715) * z2))
    return cdf, dz


def _lane_iota(shape):
    return lax.broadcasted_iota(jnp.int32, shape, len(shape) - 1)


def _rope_partner(xb):
    first = (_lane_iota(xb.shape) & 32) == 0
    return jnp.where(first, pltpu.roll(xb, LANES - 32, 1), pltpu.roll(xb, 32, 1))


def _rope_fwd(xb, cos, ssin):
    return xb * cos + _rope_partner(xb) * ssin


def _rope_bwd(dyb, cos, ssin):
    return dyb * cos - _rope_partner(dyb) * ssin


def _low_half(shape):
    return (_lane_iota(shape) & 64) == 0


def _pool_fwd(x, pre_g, post_g, w, scale, layer, name):
    t, d_model = x.shape
    tm = _row_tile(t)
    n_groups, gc = w.shape[1], w.shape[2]

    def body(x_ref, pre_ref, post_ref, w_ref, sc_ref, xo_ref, d_ref, yu_ref, hbuf, sbuf):
        i = pl.program_id(0)

        @pl.when(i == 0)
        def _():
            hbuf[pl.ds(0, POOL_HALO), :] = jnp.zeros((POOL_HALO, d_model), F32)

        xv = x_ref[...]
        hbuf[pl.ds(POOL_HALO, tm), :] = _rms_fwd(xv, pre_ref[...])
        tok = i * tm + lax.broadcasted_iota(jnp.int32, (tm, 1), 0)
        yus = []
        for gi, wnd in enumerate(POOL_WINDOWS):
            first, reach = CONV_HALO * (gi + 1), wnd // 2
            n = POOL_HALO + tm - first
            cols = pl.ds(gi * gc, d_model - gi * gc)
            src = hbuf if gi == 0 else sbuf.at[gi - 1]
            level = src[pl.ds(first, n), cols] + src[pl.ds(first - reach, n), cols]
            if gi + 1 < len(POOL_WINDOWS):
                sbuf[gi, pl.ds(first, n), cols] = level
            h = hbuf[pl.ds(POOL_HALO, tm), pl.ds(gi * gc, gc)]
            cnt = jnp.minimum(tok + 1, wnd).astype(F32)
            dg = level[POOL_HALO - first:, :gc] / cnt - h
            d_ref[:, pl.ds(gi * gc, gc)] = dg.astype(d_ref.dtype)
            yus.append(_mm(dg, w_ref[gi]))
        hbuf[pl.ds(0, POOL_HALO), :] = hbuf[pl.ds(tm, POOL_HALO), :]
        yu = jnp.concatenate(yus, axis=1)
        yu_ref[...] = yu
        xo_ref[...] = xv + _rms_fwd(yu * sc_ref[...], post_ref[...])

    return pl.pallas_call(
        body, name=name, grid=(t // tm,),
        out_shape=(jax.ShapeDtypeStruct((t, d_model), F32), jax.ShapeDtypeStruct((t, d_model), _ACT_DTYPE),
                   jax.ShapeDtypeStruct((t, d_model), F32)),
        in_specs=[_rows(tm, d_model), _const((None, 1, d_model), (layer, 0, 0)), _const((None, 1, d_model), (layer, 0, 0)),
                  _const((None, n_groups, gc, gc), (layer, 0, 0, 0)), _const((None, 1, d_model), (layer, 0, 0))],
        out_specs=(_rows(tm, d_model), _rows(tm, d_model), _rows(tm, d_model)),
        scratch_shapes=[pltpu.VMEM((POOL_HALO + tm, d_model), F32),
                        pltpu.VMEM((len(POOL_WINDOWS) - 1, POOL_HALO + tm, d_model), F32)],
        compiler_params=_params(),
    )(x, pre_g, post_g, w, scale)


def _pool_bwd(dx, x, pre_g, post_g, d, yu, w, scale, layer, name):
    t, d_model = x.shape
    tm = _row_tile(t)
    nt = t // tm
    n_groups, gc = w.shape[1], w.shape[2]
    rev = lambda i: (nt - 1 - i, 0)
    rows = pl.BlockSpec((tm, d_model), rev)

    def body(dx_ref, x_ref, pre_ref, post_ref, d_ref, yu_ref, w_ref, sc_ref,
             dxi_ref, dyu_ref, dsc_ref, dpre_ref, dpost_ref, zbuf, sbuf):
        i = pl.program_id(0)

        @pl.when(i == 0)
        def _():
            zbuf[pl.ds(tm, POOL_HALO), :] = jnp.zeros((POOL_HALO, d_model), F32)
            dsc_ref[...] = jnp.zeros_like(dsc_ref)
            dpre_ref[...] = jnp.zeros_like(dpre_ref)
            dpost_ref[...] = jnp.zeros_like(dpost_ref)

        dxo = dx_ref[...]
        yuv = yu_ref[...]
        sc = sc_ref[...]
        dm, dpost = _rms_bwd(yuv * sc, post_ref[...], dxo)
        dpost_ref[...] += dpost
        dsc_ref[...] += jnp.sum(dm * yuv, axis=0, keepdims=True)
        dyu = dm * sc
        dyu_ref[...] = dyu.astype(dyu_ref.dtype)
        tok = (nt - 1 - i) * tm + lax.broadcasted_iota(jnp.int32, (tm, 1), 0)
        dds = []
        for gi, wnd in enumerate(POOL_WINDOWS):
            cols = pl.ds(gi * gc, gc)
            dd = _mm_tb(dyu[:, gi * gc:(gi + 1) * gc], w_ref[gi])
            cnt = jnp.minimum(tok + 1, wnd).astype(F32)
            zbuf[pl.ds(0, tm), cols] = dd / cnt
            dds.append(dd)
        dhs = []
        for gi, wnd in enumerate(POOL_WINDOWS):
            reach = wnd // 2
            n = tm + POOL_HALO - CONV_HALO * (gi + 1)
            cols = pl.ds(gi * gc, d_model - gi * gc)
            src = zbuf if gi == 0 else sbuf.at[gi - 1]
            level = src[pl.ds(0, n), cols] + src[pl.ds(reach, n), cols]
            if gi + 1 < len(POOL_WINDOWS):
                sbuf[gi, pl.ds(0, n), cols] = level
            dhs.append(level[:tm, :gc] - dds[gi])
        zbuf[pl.ds(tm, POOL_HALO), :] = zbuf[pl.ds(0, POOL_HALO), :]
        dh = jnp.concatenate(dhs, axis=1)
        dxp, dpre = _rms_bwd(x_ref[...], pre_ref[...], dh)
        dpre_ref[...] += dpre
        dxi_ref[...] = dxo + dxp

    vec = jax.ShapeDtypeStruct((1, d_model), F32)
    return pl.pallas_call(
        body, name=name, grid=(nt,),
        out_shape=(jax.ShapeDtypeStruct((t, d_model), F32), jax.ShapeDtypeStruct((t, d_model), _ACT_DTYPE), vec, vec, vec),
        in_specs=[rows, rows, _const((None, 1, d_model), (layer, 0, 0)), _const((None, 1, d_model), (layer, 0, 0)), rows, rows,
                  _const((None, n_groups, gc, gc), (layer, 0, 0, 0)), _const((None, 1, d_model), (layer, 0, 0))],
        out_specs=(rows, rows, _const((1, d_model)), _const((1, d_model)), _const((1, d_model))),
        scratch_shapes=[pltpu.VMEM((tm + POOL_HALO, d_model), F32),
                        pltpu.VMEM((len(POOL_WINDOWS) - 1, tm + POOL_HALO, d_model), F32)],
        compiler_params=_params(),
    )(dx, x, pre_g, post_g, d, yu, w, scale)


def _conv_taps(cw_ref, s):
    return [cw_ref[k, pl.ds(s, 1), :] for k in range(3)]


def _shift_down(v, k, before):
    rolled = pltpu.roll(v, k, 0)
    row = lax.broadcasted_iota(jnp.int32, before.shape, 0)
    head = jnp.where(row < k, pltpu.roll(before, k, 0), rolled[:CONV_HALO])
    return jnp.concatenate([head, rolled[CONV_HALO:]], axis=0)


def _shift_up(v, k, after):
    rows = v.shape[0]
    rolled = pltpu.roll(v, rows - k, 0)
    row = lax.broadcasted_iota(jnp.int32, after.shape, 0)
    tail = jnp.where(row >= CONV_HALO - k, pltpu.roll(after, CONV_HALO - k, 0), rolled[rows - CONV_HALO:])
    return jnp.concatenate([rolled[:rows - CONV_HALO], tail], axis=0)


def _ffn_fwd(x, pre_g, post_g, w_in, conv_w, conv_b, w_out, layer, w_layer, name):
    t, d_model = x.shape
    tm = _row_tile(t)
    fs = w_in.shape[3]
    half = N_DEV // 2

    def body(x_ref, pre_ref, post_ref, win_ref, cw_ref, cb_ref, wout_ref, xo_ref, u_ref, uc_ref, f_ref, carry):
        i = pl.program_id(0)

        @pl.when(i == 0)
        def _():
            carry[...] = jnp.zeros_like(carry)

        xv = x_ref[...]
        hf = _rms_fwd(xv, pre_ref[...]).astype(_MXU_DTYPE)
        f = jnp.zeros((tm, d_model), F32)
        project = lambda b: [jnp.dot(hf, win_ref[s], preferred_element_type=F32) for s in (b, b + half)]
        ahead = project(0)
        for b in range(half):
            us, ahead = ahead, project(b + 1) if b + 1 < half else None
            ucs = []
            for s, u in zip((b, b + half), us):
                u_ref[s] = u.astype(u_ref.dtype)
                before = carry[s]
                carry[s] = u[tm - CONV_HALO:]
                w0, w1, w2 = _conv_taps(cw_ref, s)
                uc = ((w0 * _shift_down(u, 2, before) + w1 * _shift_down(u, 1, before)) + w2 * u) + cb_ref[pl.ds(s, 1), :]
                uc_ref[s] = uc.astype(uc_ref.dtype)
                ucs.append(uc)
            gate, val = ucs
            cdf, _ = _gelu_parts(gate)
            f = f + _mm((gate * cdf) * val, wout_ref[pl.ds(b * fs, fs), :])
        f_ref[...] = f
        xo_ref[...] = xv + _rms_fwd(f, post_ref[...])

    tile3 = pl.BlockSpec((N_DEV, tm, fs), lambda i: (0, i, 0))
    saved = jax.ShapeDtypeStruct((N_DEV, t, fs), _SAVE_DTYPE)
    return pl.pallas_call(
        body, name=name, grid=(t // tm,),
        out_shape=(jax.ShapeDtypeStruct((t, d_model), F32), saved, saved, jax.ShapeDtypeStruct((t, d_model), F32)),
        in_specs=[_rows(tm, d_model), _const((None, 1, d_model), (layer, 0, 0)), _const((None, 1, d_model), (layer, 0, 0)),
                  _resident((None, N_DEV, d_model, fs), (w_layer, 0, 0, 0)), _const((None, 3, N_DEV, fs), (layer, 0, 0, 0)),
                  _const((None, N_DEV, fs), (layer, 0, 0)), _resident((None, half * fs, d_model), (w_layer, 0, 0))],
        out_specs=(_rows(tm, d_model), tile3, tile3, _rows(tm, d_model)),
        scratch_shapes=[pltpu.VMEM((N_DEV, CONV_HALO, fs), F32)],
        compiler_params=_params(),
    )(x, pre_g, post_g, w_in, conv_w, conv_b, w_out)


def _ffn_bwd_out(dx, f, post_g, uc, conv_b, w_out, layer, w_layer, name):
    t, d_model = dx.shape
    tm = _row_tile(t)
    fs = uc.shape[2]
    half = N_DEV // 2

    def body(dx_ref, f_ref, post_ref, uc_ref, wout_ref, duc_ref, g_ref, df_ref, dcb_ref, dpost_ref):
        i = pl.program_id(0)

        @pl.when(i == 0)
        def _():
            dcb_ref[...] = jnp.zeros_like(dcb_ref)
            dpost_ref[...] = jnp.zeros_like(dpost_ref)

        df, dpost = _rms_bwd(f_ref[...], post_ref[...], dx_ref[...])
        dpost_ref[...] += dpost
        dfm = df.astype(_MXU_DTYPE)
        df_ref[...] = dfm
        project = lambda b: _mm_tb(dfm, wout_ref[pl.ds(b * fs, fs), :])
        ahead = project(0)
        for b in range(half):
            dg, ahead = ahead, project(b + 1) if b + 1 < half else None
            gate = uc_ref[b].astype(F32)
            val = uc_ref[b + half].astype(F32)
            cdf, dgelu = _gelu_parts(gate)
            ge = gate * cdf
            g_ref[b] = (ge * val).astype(g_ref.dtype)
            for s, dd in ((b, dg * val * dgelu), (b + half, dg * ge)):
                duc_ref[s] = dd.astype(duc_ref.dtype)
                dcb_ref[pl.ds(s, 1), :] += jnp.sum(dd, axis=0, keepdims=True)

    tile3 = pl.BlockSpec((N_DEV, tm, fs), lambda i: (0, i, 0))
    return pl.pallas_call(
        body, name=name, grid=(t // tm,),
        out_shape=(jax.ShapeDtypeStruct((N_DEV, t, fs), _SAVE_DTYPE), jax.ShapeDtypeStruct((half, t, fs), _ACT_DTYPE),
                   jax.ShapeDtypeStruct((t, d_model), _ACT_DTYPE), jax.ShapeDtypeStruct((N_DEV, fs), F32),
                   jax.ShapeDtypeStruct((1, d_model), F32)),
        in_specs=[_rows(tm, d_model), _rows(tm, d_model), _const((None, 1, d_model), (layer, 0, 0)), tile3,
                  _resident((None, half * fs, d_model), (w_layer, 0, 0))],
        out_specs=(tile3, pl.BlockSpec((half, tm, fs), lambda i: (0, i, 0)), _rows(tm, d_model),
                   _const((N_DEV, fs)), _const((1, d_model))),
        compiler_params=_params(),
    )(dx, f, post_g, uc, w_out)


def _ffn_bwd_in(dx, x, pre_g, duc, u, conv_w, w_in, layer, w_layer, name):
    t, d_model = dx.shape
    tm = _row_tile(t)
    nt = t // tm
    fs = duc.shape[2]
    hb = SAVE_HALO
    per_tile = tm // hb

    def body(dx_ref, x_ref, pre_ref, duc_ref, dn_ref, u_ref, cw_ref, win_ref, dxi_ref, du_ref, hf_ref, dcw_ref, dpre_ref):
        i = pl.program_id(0)

        @pl.when(i == 0)
        def _():
            dcw_ref[...] = jnp.zeros_like(dcw_ref)
            dpre_ref[...] = jnp.zeros_like(dpre_ref)

        xv = x_ref[...]
        pre = pre_ref[...]
        hf_ref[...] = _rms_fwd(xv, pre).astype(hf_ref.dtype)
        dhf = jnp.zeros((tm, d_model), F32)
        for s in range(N_DEV):
            d0 = duc_ref[s].astype(F32)
            after = jnp.where(i == nt - 1, 0.0, dn_ref[s].astype(F32)[:CONV_HALO])
            d1 = _shift_up(d0, 1, after)
            d2 = _shift_up(d0, 2, after)
            uv = u_ref[s].astype(F32)
            for k, dk in ((2, d0), (1, d1), (0, d2)):
                dcw_ref[k, pl.ds(s, 1), :] += jnp.sum(dk * uv, axis=0, keepdims=True)
            w0, w1, w2 = _conv_taps(cw_ref, s)
            du = (w2 * d0 + w1 * d1 + w0 * d2).astype(_MXU_DTYPE)
            du_ref[s] = du
            dhf = dhf + _mm_tb(du, win_ref[s])
        dxp, dpre = _rms_bwd(xv, pre, dhf)
        dpre_ref[...] += dpre
        dxi_ref[...] = dx_ref[...] + dxp

    tile3 = pl.BlockSpec((N_DEV, tm, fs), lambda i: (0, i, 0))
    return pl.pallas_call(
        body, name=name, grid=(nt,),
        out_shape=(jax.ShapeDtypeStruct((t, d_model), F32), jax.ShapeDtypeStruct((N_DEV, t, fs), _ACT_DTYPE),
                   jax.ShapeDtypeStruct((t, d_model), _ACT_DTYPE), jax.ShapeDtypeStruct((3, N_DEV, fs), F32),
                   jax.ShapeDtypeStruct((1, d_model), F32)),
        in_specs=[_rows(tm, d_model), _rows(tm, d_model), _const((None, 1, d_model), (layer, 0, 0)), tile3,
                  pl.BlockSpec((N_DEV, hb, fs), lambda i: (0, jnp.minimum((i + 1) * per_tile, t // hb - 1), 0)), tile3,
                  _const((None, 3, N_DEV, fs), (layer, 0, 0, 0)), _resident((None, N_DEV, d_model, fs), (w_layer, 0, 0, 0))],
        out_specs=(_rows(tm, d_model), tile3, _rows(tm, d_model), _const((3, N_DEV, fs)), _const((1, d_model))),
        compiler_params=_params(),
    )(dx, x, pre_g, duc, duc, u, conv_w, w_in)


def _tn_matmul(a, b, a_spec, b_spec, n_out, m, n, name, out_dtype):
    def body(a_ref, b_ref, o_ref):
        o_ref[...] = _mm_ta(a_ref[...], b_ref[...]).astype(o_ref.dtype)

    return pl.pallas_call(
        body, name=name, grid=(n_out,),
        out_shape=jax.ShapeDtypeStruct((n_out, m, n), out_dtype),
        in_specs=[a_spec, b_spec],
        out_specs=pl.BlockSpec((None, m, n), lambda c: (c, 0, 0)),
        compiler_params=_params(),
    )(a, b)


def _kv_fwd(x, kv_g, w_kv, cos, ssin, name):
    t, d_model = x.shape
    tm = _row_tile(t, PROJ_TILE)
    kvd = w_kv.shape[1] // 2
    pairs = kvd // LANES

    def body(x_ref, g_ref, w_ref, cos_ref, sin_ref, k_ref, v_ref):
        kv = _mm(_rms_fwd(x_ref[...], g_ref[...]), w_ref[...])
        low = _low_half((tm, LANES))
        for j in range(pairs):
            kb = _rope_fwd(kv[:, j * LANES:(j + 1) * LANES], cos_ref[...], sin_ref[...])
            vb = kv[:, kvd + j * LANES:kvd + (j + 1) * LANES]
            for blk, ref in ((kb, k_ref), (vb, v_ref)):
                sw = pltpu.roll(blk, 64, 1)
                ref[2 * j] = jnp.where(low, blk, sw).astype(ref.dtype)
                ref[2 * j + 1] = jnp.where(low, sw, blk).astype(ref.dtype)

    heads = jax.ShapeDtypeStruct((N_KV_HEADS, t, LANES), _ACT_DTYPE)
    hspec = pl.BlockSpec((N_KV_HEADS, tm, LANES), lambda i: (0, i, 0))
    return pl.pallas_call(
        body, name=name, grid=(t // tm,), out_shape=(heads, heads),
        in_specs=[_rows(tm, d_model), _const((1, d_model)), _const(w_kv.shape), _rows(tm, LANES), _rows(tm, LANES)],
        out_specs=(hspec, hspec), compiler_params=_params(),
    )(x, kv_g, w_kv, cos, ssin)


def _kv_bwd(dx, x, kv_g, w_kv, cos, ssin, dks, dvs, name):
    t, d_model = x.shape
    tm = _row_tile(t, PROJ_TILE)
    kvd = w_kv.shape[1] // 2
    pairs = kvd // LANES
    n_users = len(dks)

    def body(dx_ref, x_ref, g_ref, w_ref, cos_ref, sin_ref, *refs):
        dk_refs, dv_refs = refs[:n_users], refs[n_users:2 * n_users]
        dxi_ref, h_ref, dkv_ref, dg_ref = refs[2 * n_users:]
        dk_ref = functools.reduce(lambda a, b: a + b, [r[...] for r in dk_refs])
        dv_ref = functools.reduce(lambda a, b: a + b, [r[...] for r in dv_refs])
        i = pl.program_id(0)

        @pl.when(i == 0)
        def _():
            dg_ref[...] = jnp.zeros_like(dg_ref)

        xv = x_ref[...]
        g = g_ref[...]
        h_ref[...] = _rms_fwd(xv, g).astype(h_ref.dtype)
        low = _low_half((tm, LANES))
        dks, dvs = [], []
        for j in range(pairs):
            dkb = jnp.where(low, dk_ref[2 * j], dk_ref[2 * j + 1])
            dks.append(_rope_bwd(dkb, cos_ref[...], sin_ref[...]))
            dvs.append(jnp.where(low, dv_ref[2 * j], dv_ref[2 * j + 1]))
        dkv = jnp.concatenate(dks + dvs, axis=1).astype(_MXU_DTYPE)
        dkv_ref[...] = dkv
        dxp, dg = _rms_bwd(xv, g, _mm_tb(dkv, w_ref[...]))
        dg_ref[...] += dg
        dxi_ref[...] = dx_ref[...] + dxp

    hspec = pl.BlockSpec((N_KV_HEADS, tm, LANES), lambda i: (0, i, 0))
    return pl.pallas_call(
        body, name=name, grid=(t // tm,),
        out_shape=(jax.ShapeDtypeStruct((t, d_model), F32), jax.ShapeDtypeStruct((t, d_model), _ACT_DTYPE),
                   jax.ShapeDtypeStruct((t, 2 * kvd), _ACT_DTYPE), jax.ShapeDtypeStruct((1, d_model), F32)),
        in_specs=[_rows(tm, d_model), _rows(tm, d_model), _const((1, d_model)), _const(w_kv.shape),
                  _rows(tm, LANES), _rows(tm, LANES)] + [hspec] * (2 * n_users),
        out_specs=(_rows(tm, d_model), _rows(tm, d_model), _rows(tm, 2 * kvd), _const((1, d_model))),
        compiler_params=_params(),
    )(dx, x, kv_g, w_kv, cos, ssin, *dks, *dvs)


def _q_fwd(x, pre_g, w_q, cos, ssin, layer, j, name):
    t, d_model = x.shape
    tm = _row_tile(t, PROJ_TILE)

    def body(x_ref, g_ref, w_ref, cos_ref, sin_ref, q_ref):
        q = _mm(_rms_fwd(x_ref[...], g_ref[...]), w_ref[...])
        for p in range(d_model // LANES):
            cols = slice(p * LANES, (p + 1) * LANES)
            q_ref[:, cols] = (_rope_fwd(q[:, cols], cos_ref[...], sin_ref[...]) * ATTN_SCALE).astype(q_ref.dtype)

    return pl.pallas_call(
        body, name=name, grid=(t // tm,), out_shape=jax.ShapeDtypeStruct((t, d_model), _ACT_DTYPE),
        in_specs=[_rows(tm, d_model), _const((None, 1, d_model), (layer, 0, 0)), _const((None, d_model, d_model), (j, 0, 0)),
                  _rows(tm, LANES), _rows(tm, LANES)],
        out_specs=_rows(tm, d_model), compiler_params=_params(),
    )(x, pre_g, w_q, cos, ssin)


def _q_bwd(dx, dqs, x, pre_g, w_q, cos, ssin, layer, j, name):
    t, d_model = x.shape
    tm = _row_tile(t, PROJ_TILE)

    def body(dx_ref, dq_ref, x_ref, g_ref, w_ref, cos_ref, sin_ref, dxi_ref, h_ref, dqo_ref, dg_ref):
        i = pl.program_id(0)

        @pl.when(i == 0)
        def _():
            dg_ref[...] = jnp.zeros_like(dg_ref)

        xv = x_ref[...]
        g = g_ref[...]
        h_ref[...] = _rms_fwd(xv, g).astype(h_ref.dtype)
        parts = []
        for p in range(d_model // LANES):
            cols = slice(p * LANES, (p + 1) * LANES)
            parts.append(_rope_bwd(dq_ref[:, cols] * ATTN_SCALE, cos_ref[...], sin_ref[...]))
        dq = jnp.concatenate(parts, axis=1).astype(_MXU_DTYPE)
        dqo_ref[...] = dq
        dxp, dg = _rms_bwd(xv, g, _mm_tb(dq, w_ref[...]))
        dg_ref[...] += dg
        dxi_ref[...] = dx_ref[...] + dxp

    act = jax.ShapeDtypeStruct((t, d_model), _ACT_DTYPE)
    return pl.pallas_call(
        body, name=name, grid=(t // tm,),
        out_shape=(jax.ShapeDtypeStruct((t, d_model), F32), act, act, jax.ShapeDtypeStruct((1, d_model), F32)),
        in_specs=[_rows(tm, d_model), _rows(tm, d_model), _rows(tm, d_model), _const((None, 1, d_model), (layer, 0, 0)),
                  _const((None, d_model, d_model), (j, 0, 0)), _rows(tm, LANES), _rows(tm, LANES)],
        out_specs=(_rows(tm, d_model), _rows(tm, d_model), _rows(tm, d_model), _const((1, d_model))),
        compiler_params=_params(),
    )(dx, dqs, x, pre_g, w_q, cos, ssin)


def _stack_heads(pairs):
    low = _low_half(pairs[0].shape)
    zero = jnp.zeros_like(pairs[0])
    return jnp.concatenate([h for blk in pairs for h in (jnp.where(low, blk, zero), jnp.where(low, zero, blk))], axis=0)


def _unstack_heads(stacked, i):
    a, b = stacked[2 * i * BLOCK:(2 * i + 1) * BLOCK], stacked[(2 * i + 1) * BLOCK:(2 * i + 2) * BLOCK]
    return jnp.where(_low_half(a.shape), a, b)


def _attn_scores(q_pairs, k2, n, sinks):
    qst = _stack_heads(q_pairs)
    s = _mm_tb(qst, k2)
    row = lax.broadcasted_iota(jnp.int32, s.shape, 0)
    col = lax.broadcasted_iota(jnp.int32, s.shape, 1)
    rel = BLOCK + (row & (BLOCK - 1)) - col
    valid = (rel >= 0) & (rel < WINDOW) & (n * BLOCK + col - BLOCK >= 0)
    s = jnp.where(valid, s, NEG_INF)
    rows1 = lax.broadcasted_iota(jnp.int32, (s.shape[0], 1), 0)
    sink = jnp.full((s.shape[0], 1), sinks[-1], F32)
    for i in reversed(range(len(sinks) - 1)):
        sink = jnp.where(rows1 < (i + 1) * BLOCK, sinks[i], sink)
    return qst, s, sink


def _attn_fwd(qs, kdup, vdup, sinks, j, name):
    t, d_model = qs.shape
    nb = t // BLOCK
    n_pairs = d_model // LANES
    per_group = n_pairs // N_KV_HEADS

    def body(sink_ref, q_ref, kp_ref, ko_ref, vp_ref, vo_ref, o_ref, lse_ref):
        n = pl.program_id(0)
        lane = _lane_iota((BLOCK, LANES))
        lse = jnp.zeros((BLOCK, LANES), F32)
        for hk in range(N_KV_HEADS):
            pairs = range(hk * per_group, (hk + 1) * per_group)
            heads = range(2 * pairs[0], 2 * pairs[-1] + 2)
            k2 = jnp.concatenate([kp_ref[hk], ko_ref[hk]], axis=0)
            v2 = jnp.concatenate([vp_ref[hk], vo_ref[hk]], axis=0)
            _, s, sink = _attn_scores([q_ref[:, p * LANES:(p + 1) * LANES] for p in pairs], k2, n,
                                      [sink_ref[j, h] for h in heads])
            m = jnp.maximum(jnp.max(s, axis=-1, keepdims=True), sink)
            pe = jnp.exp(s - m)
            denom = jnp.sum(pe, axis=-1, keepdims=True) + jnp.exp(sink - m)
            o2 = _mm(pe, v2) / denom
            l2 = m + jnp.log(denom)
            for i, p in enumerate(pairs):
                o_ref[:, p * LANES:(p + 1) * LANES] = _unstack_heads(o2, i).astype(o_ref.dtype)
            for i, h in enumerate(heads):
                lse = jnp.where(lane == h, l2[i * BLOCK:(i + 1) * BLOCK], lse)
        lse_ref[...] = lse

    prev = pl.BlockSpec((N_KV_HEADS, BLOCK, LANES), lambda n: (0, jnp.maximum(n - 1, 0), 0))
    own = pl.BlockSpec((N_KV_HEADS, BLOCK, LANES), lambda n: (0, n, 0))
    return pl.pallas_call(
        body, name=name, grid=(nb,),
        out_shape=(jax.ShapeDtypeStruct((t, d_model), _ACT_DTYPE), jax.ShapeDtypeStruct((t, LANES), F32)),
        in_specs=[pl.BlockSpec(memory_space=pltpu.SMEM), _rows(BLOCK, d_model), prev, own, prev, own],
        out_specs=(_rows(BLOCK, d_model), _rows(BLOCK, LANES)), compiler_params=_params(),
    )(sinks, qs, kdup, kdup, vdup, vdup)


def _attn_bwd(qs, kdup, vdup, sinks, lse, do, j, name):
    t, d_model = qs.shape
    nb = t // BLOCK
    n_pairs = d_model // LANES
    per_group = n_pairs // N_KV_HEADS
    rev = lambda n: nb - 1 - n

    def body(sink_ref, q_ref, kp_ref, ko_ref, vp_ref, vo_ref, lse_ref, do_ref, dq_ref, dk_ref, dv_ref, ds_ref, ck, cv):
        i = pl.program_id(0)
        n = nb - 1 - i

        @pl.when(i == 0)
        def _():
            ck[...] = jnp.zeros_like(ck)
            cv[...] = jnp.zeros_like(cv)
            ds_ref[...] = jnp.zeros_like(ds_ref)

        lane = _lane_iota((BLOCK, LANES))
        lane1 = _lane_iota((1, LANES))
        lsev = lse_ref[...]
        dsink = jnp.zeros((1, LANES), F32)
        for hk in range(N_KV_HEADS):
            pairs = range(hk * per_group, (hk + 1) * per_group)
            heads = range(2 * pairs[0], 2 * pairs[-1] + 2)
            k2 = jnp.concatenate([kp_ref[hk], ko_ref[hk]], axis=0)
            v2 = jnp.concatenate([vp_ref[hk], vo_ref[hk]], axis=0)
            qst, s, sink = _attn_scores([q_ref[:, p * LANES:(p + 1) * LANES] for p in pairs], k2, n,
                                        [sink_ref[j, h] for h in heads])
            l2 = jnp.concatenate([jnp.sum(jnp.where(lane == h, lsev, 0.0), axis=-1, keepdims=True) for h in heads], axis=0)
            pn = jnp.exp(s - l2)
            dost = _stack_heads([do_ref[:, p * LANES:(p + 1) * LANES] for p in pairs])
            dp = _mm_tb(dost, v2)
            dr = jnp.sum(pn * dp, axis=-1, keepdims=True)
            dsm = (pn * (dp - dr)).astype(_MXU_DTYPE)
            dsk = -jnp.exp(sink - l2) * dr
            for i, h in enumerate(heads):
                dsink = dsink + jnp.where(lane1 == h, jnp.sum(dsk[i * BLOCK:(i + 1) * BLOCK]), 0.0)
            dq2 = _mm(dsm, k2)
            for i, p in enumerate(pairs):
                dq_ref[:, p * LANES:(p + 1) * LANES] = _unstack_heads(dq2, i)
            for acc, carry, ref in ((_mm_ta(dsm, qst), ck, dk_ref), (_mm_ta(pn, dost), cv, dv_ref)):
                folded = acc + pltpu.roll(acc, 64, 1)
                ref[hk] = folded[BLOCK:] + carry[hk]
                carry[hk] = folded[:BLOCK]
        ds_ref[...] += dsink

    prev = pl.BlockSpec((N_KV_HEADS, BLOCK, LANES), lambda n: (0, jnp.maximum(rev(n) - 1, 0), 0))
    own = pl.BlockSpec((N_KV_HEADS, BLOCK, LANES), lambda n: (0, rev(n), 0))
    rows = lambda cols: pl.BlockSpec((BLOCK, cols), lambda n: (rev(n), 0))
    heads = jax.ShapeDtypeStruct((N_KV_HEADS, t, LANES), F32)
    return pl.pallas_call(
        body, name=name, grid=(nb,),
        out_shape=(jax.ShapeDtypeStruct((t, d_model), F32), heads, heads, jax.ShapeDtypeStruct((1, LANES), F32)),
        in_specs=[pl.BlockSpec(memory_space=pltpu.SMEM), rows(d_model), prev, own, prev, own, rows(LANES), rows(d_model)],
        out_specs=(rows(d_model), own, own, _const((1, LANES))),
        scratch_shapes=[pltpu.VMEM((N_KV_HEADS, BLOCK, LANES), F32), pltpu.VMEM((N_KV_HEADS, BLOCK, LANES), F32)],
        compiler_params=_params(),
    )(sinks, qs, kdup, kdup, vdup, vdup, lse, do)


def _oproj_fwd(x, o, w_o, post_g, layer, j, name):
    t, d_model = x.shape
    tm = _row_tile(t, PROJ_TILE)

    def body(x_ref, o_ref, w_ref, g_ref, xo_ref, mo_ref):
        mo = _mm(o_ref[...], w_ref[...])
        mo_ref[...] = mo
        xo_ref[...] = x_ref[...] + _rms_fwd(mo, g_ref[...])

    full = jax.ShapeDtypeStruct((t, d_model), F32)
    return pl.pallas_call(
        body, name=name, grid=(t // tm,), out_shape=(full, full),
        in_specs=[_rows(tm, d_model), _rows(tm, d_model), _const((None, d_model, d_model), (j, 0, 0)),
                  _const((None, 1, d_model), (layer, 0, 0))],
        out_specs=(_rows(tm, d_model), _rows(tm, d_model)), compiler_params=_params(),
    )(x, o, w_o, post_g)


def _oproj_bwd(dx, mo, w_o, post_g, layer, j, name):
    t, d_model = dx.shape
    tm = _row_tile(t, PROJ_TILE)

    def body(dx_ref, mo_ref, w_ref, g_ref, do_ref, dmo_ref, dg_ref):
        i = pl.program_id(0)

        @pl.when(i == 0)
        def _():
            dg_ref[...] = jnp.zeros_like(dg_ref)

        dmo, dg = _rms_bwd(mo_ref[...], g_ref[...], dx_ref[...])
        dg_ref[...] += dg
        dmo = dmo.astype(_MXU_DTYPE)
        dmo_ref[...] = dmo
        do_ref[...] = _mm_tb(dmo, w_ref[...]).astype(do_ref.dtype)

    act = jax.ShapeDtypeStruct((t, d_model), _ACT_DTYPE)
    return pl.pallas_call(
        body, name=name, grid=(t // tm,), out_shape=(act, act, jax.ShapeDtypeStruct((1, d_model), F32)),
        in_specs=[_rows(tm, d_model), _rows(tm, d_model), _const((None, d_model, d_model), (j, 0, 0)),
                  _const((None, 1, d_model), (layer, 0, 0))],
        out_specs=(_rows(tm, d_model), _rows(tm, d_model), _const((1, d_model))), compiler_params=_params(),
    )(dx, mo, w_o, post_g)


def _loss_grad(y, target, name):
    t, d_model = y.shape
    tm = _row_tile(t, PROJ_TILE)

    def body(y_ref, t_ref, dy_ref, loss_ref):
        i = pl.program_id(0)

        @pl.when(i == 0)
        def _():
            loss_ref[...] = jnp.zeros_like(loss_ref)

        err = y_ref[...] - t_ref[...]
        dy_ref[...] = err / d_model
        loss_ref[...] += 0.5 * jnp.sum(jnp.mean(err * err, axis=-1, keepdims=True), axis=0, keepdims=True)

    return pl.pallas_call(
        body, name=name, grid=(t // tm,),
        out_shape=(jax.ShapeDtypeStruct((t, d_model), F32), jax.ShapeDtypeStruct((1, 1), F32)),
        in_specs=[_rows(tm, d_model), _rows(tm, d_model)], out_specs=(_rows(tm, d_model), _const((1, 1))),
        compiler_params=_params(),
    )(y, target)


def _mesh_position():
    return lax.axis_index("x"), lax.axis_index("y"), lax.axis_index("c")


def _block_of(px, py, pc):
    return 4 * px + 2 * py + pc


def _at_block(ref, axis, block):
    return ref.at[(slice(None),) * axis + (block,)]


def _all_gather(shards, axes, name):
    n = len(shards)

    def body(*refs):
        srcs, outs = refs[:n], refs[n:2 * n]
        send_sems, recv_sems, local_sems = refs[2 * n:]
        x, y, c = _mesh_position()
        me, sibling = (x, y, c), (x, y, 1 - c)
        chips = [(1 - x, y), (x, 1 - y), (1 - x, 1 - y)]

        def blk(i, pos):
            return _at_block(outs[i], axes[i], _block_of(*pos))

        def copy(i, k, block, to, src=None):
            return pltpu.make_async_remote_copy(
                src_ref=blk(i, block) if src is None else src, dst_ref=blk(i, block),
                send_sem=send_sems.at[i, k], recv_sem=recv_sems.at[i, k], device_id=to, device_id_type=MESH)

        mine = [pltpu.make_async_copy(srcs[i], blk(i, me), local_sems.at[i]) for i in range(n)]
        for cp in mine:
            cp.start()
        sent = []
        for i in range(n):
            sent += [copy(i, 1 + k, me, (*chip, c), src=srcs[i]) for k, chip in enumerate(chips)]
            sent.append(copy(i, 0, me, sibling, src=srcs[i]))
        for cp in sent:
            cp.start()
        for i in range(n):
            for k, chip in enumerate(chips):
                copy(i, 1 + k, (*chip, c), me).wait_recv()
                passed = copy(i, 4 + k, (*chip, c), sibling)
                passed.start()
                sent.append(passed)
        for i in range(n):
            copy(i, 0, sibling, me).wait_recv()
            for k, chip in enumerate(chips):
                copy(i, 4 + k, (*chip, 1 - c), me).wait_recv()
        for cp in sent:
            cp.wait_send()
        for cp in mine:
            cp.wait()

    hbm = pl.BlockSpec(memory_space=pl.ANY)
    return pl.pallas_call(
        body, name=name,
        out_shape=tuple(jax.ShapeDtypeStruct(s.shape[:a] + (N_DEV,) + s.shape[a:], s.dtype) for s, a in zip(shards, axes)),
        in_specs=[hbm] * n, out_specs=(hbm,) * n,
        scratch_shapes=[pltpu.SemaphoreType.DMA((n, 7)), pltpu.SemaphoreType.DMA((n, 7)), pltpu.SemaphoreType.DMA((n,))],
    )(*shards)


GATHER, SCATTER, GATHER_CHIPS, GATHER_SIBLING = "gather", "scatter", "gather_chips", "gather_sibling"
COPIES = {GATHER: N_DEV - 1, SCATTER: N_DEV - 1, GATHER_CHIPS: 4, GATHER_SIBLING: 3}


def _land_shape(kind, s, axis):
    if kind == SCATTER:
        return (N_DEV,) + s.shape[:axis] + s.shape[axis + 1:]
    return s.shape[:axis] + (N_DEV,) + s.shape[axis:]


def _plan(kind, srcs, lands, axes):
    x, y, c = _mesh_position()
    my_block = _block_of(x, y, c)
    flips = {GATHER_CHIPS: (1, 4, 2, 6), GATHER_SIBLING: (4, 2, 6)}.get(kind, range(1, N_DEV))
    others = [(1 - x if k & 4 else x, 1 - y if k & 2 else y, 1 - c if k & 1 else c) for k in flips]
    remote = []
    for src, land, axis in zip(srcs, lands, axes):
        if kind == SCATTER:
            mine = land.at[my_block]
            remote += [(_at_block(src, axis, _block_of(*peer)), mine, peer, land.at[_block_of(*peer)]) for peer in others]
        elif kind == GATHER_SIBLING:
            for px, py, _ in others:
                mine, theirs = _at_block(land, axis, _block_of(px, py, c)), _at_block(land, axis, _block_of(px, py, 1 - c))
                remote.append((mine, mine, (x, y, 1 - c), theirs))
        else:
            mine = _at_block(land, axis, my_block)
            remote += [(src, mine, peer, _at_block(land, axis, _block_of(*peer))) for peer in others]
    return remote


def _remote(src, dst, send_sems, recv_sems, k, peer):
    return pltpu.make_async_remote_copy(src_ref=src, dst_ref=dst, send_sem=send_sems.at[k], recv_sem=recv_sems.at[k],
                                        device_id=peer, device_id_type=MESH)


_HBM = pl.BlockSpec(memory_space=pltpu.HBM)
_SEM = pl.BlockSpec(memory_space=pltpu.SEMAPHORE)
_SPLIT = dict(has_side_effects=pltpu.SideEffectType.DATAFLOW_SIDE_EFFECTING)


def _landing_zone(kind, s, axis, me):
    land = lax.empty(_land_shape(kind, s, axis), s.dtype)
    if kind == SCATTER:
        return lax.dynamic_update_slice_in_dim(land, lax.dynamic_slice_in_dim(s, me, 1, axis).reshape((1,) + land.shape[1:]), me, 0)
    return lax.dynamic_update_slice_in_dim(land, jnp.expand_dims(s, axis), me, axis)


def _exchange_start(kind, arrays, axes, after, name):
    n = len(arrays)
    if kind == GATHER_SIBLING:
        passed = list(arrays)
    else:
        me = _block_of(*_mesh_position())
        passed = list(arrays) + [_landing_zone(kind, s, a, me) for s, a in zip(arrays, axes)]
    n_sems = n * COPIES[kind]

    def body(*refs):
        land_refs = refs[len(passed) - n:len(passed)]
        send_sems, recv_sems = refs[len(passed) + 1], refs[len(passed) + 2]
        token = refs[-1]
        for k, (src, dst, peer, _) in enumerate(_plan(kind, refs[:n], land_refs, axes)):
            _remote(src, dst, send_sems, recv_sems, k, peer).start()
        token[...] = jnp.zeros_like(token)

    out = pl.pallas_call(
        body, name=name,
        out_shape=(pltpu.SemaphoreType.DMA((n_sems,)), pltpu.SemaphoreType.DMA((n_sems,)),
                   *[pltpu.HBM(a.shape, a.dtype) for a in passed], jax.ShapeDtypeStruct((8, LANES), F32)),
        in_specs=[_HBM] * len(passed) + [pl.BlockSpec(memory_space=pl.ANY)],
        out_specs=(_SEM, _SEM, *[_HBM] * len(passed), pl.BlockSpec(memory_space=pltpu.VMEM)),
        input_output_aliases={i: 2 + i for i in range(len(passed))},
        compiler_params=pltpu.CompilerParams(**_SPLIT),
    )(*[pltpu.with_memory_space_constraint(a, pltpu.HBM) for a in passed], after)
    return (kind, axes, n, out[:-1]), out[-1]


def _exchange_wait(handle, after, name):
    kind, axes, n, (send_sems, recv_sems, *thru) = handle

    def body(*refs):
        land_refs = refs[len(thru) - n:len(thru)]
        send_sems, recv_sems = refs[len(thru)], refs[len(thru) + 1]
        for k, (src, _, peer, arrives) in enumerate(_plan(kind, refs[:n], land_refs, axes)):
            cp = _remote(src, arrives, send_sems, recv_sems, k, peer)
            cp.wait_send()
            cp.wait_recv()

    out = pl.pallas_call(
        body, name=name,
        out_shape=tuple(pltpu.HBM(a.shape, a.dtype) for a in thru),
        in_specs=[_HBM] * len(thru) + [_SEM, _SEM, pl.BlockSpec(memory_space=pl.ANY)], out_specs=(_HBM,) * len(thru),
        input_output_aliases={i: i for i in range(len(thru))},
        compiler_params=pltpu.CompilerParams(**_SPLIT),
    )(*thru, send_sems, recv_sems, after)
    return out[len(thru) - n:]


def _adamw_math(w, g, m, v):
    m = ADAM_B1 * m + (1.0 - ADAM_B1) * g
    v = ADAM_B2 * v + (1.0 - ADAM_B2) * jnp.square(g)
    m_hat = m / (1.0 - ADAM_B1 ** ADAM_STEP)
    v_hat = v / (1.0 - ADAM_B2 ** ADAM_STEP)
    delta = -ADAM_LR * (m_hat / (jnp.sqrt(v_hat) + ADAM_EPS) + ADAM_WD * w)
    return delta, m, v


def _update_tile(rows):
    if rows <= 512:
        return rows
    for tr in (512, 384, 352, 256, 176, 128, 64, 32, 16):
        if rows % tr == 0:
            return tr
    raise ValueError(f"{rows} rows do not tile")


def _adamw(parts, w, m, v, slab, so_far, name):
    rows, c = w.shape
    r = parts.shape[1]
    tr = _update_tile(r)
    first = slab * (r // tr)
    if so_far is None:
        so_far = tuple(lax.empty((rows, c), F32) for _ in range(4))

    def body(p_ref, w_ref, m_ref, v_ref, *refs):
        g_ref, d_ref, mo_ref, vo_ref = refs[4:]
        g = p_ref[0].astype(F32)
        for s in range(1, N_DEV):
            g = g + p_ref[s].astype(F32)
        g_ref[...] = g
        d_ref[...], mo_ref[...], vo_ref[...] = _adamw_math(w_ref[...], g, m_ref[...], v_ref[...])

    out = jax.ShapeDtypeStruct((rows, c), F32)
    tile = pl.BlockSpec((tr, c), lambda i: (first + i, 0))
    return pl.pallas_call(
        body, name=name, grid=(r // tr,), out_shape=(out,) * 4,
        in_specs=[pl.BlockSpec((N_DEV, tr, c), lambda i: (0, i, 0))] + [tile] * 3 + [pl.BlockSpec(memory_space=pl.ANY)] * 4,
        out_specs=(tile,) * 4, input_output_aliases={4 + k: k for k in range(4)}, compiler_params=_params(),
    )(parts, w, m, v, *so_far)


def _adamw_small(parts, picks, weights, name):
    n = len(parts)

    def body(*refs):
        p_refs, wmv, outs = refs[:n], refs[n:4 * n], refs[4 * n:]
        me = _block_of(*_mesh_position())
        for i in range(n):
            g = picks[i](p_refs[i], 0, me)
            for s in range(1, N_DEV):
                g = g + picks[i](p_refs[i], s, me)
            w_ref, m_ref, v_ref = wmv[3 * i:3 * i + 3]
            g_ref, d_ref, mo_ref, vo_ref = outs[4 * i:4 * i + 4]
            g_ref[...] = g
            d_ref[...], mo_ref[...], vo_ref[...] = _adamw_math(w_ref[...], g, m_ref[...], v_ref[...])

    flat = [a for wmv in weights for a in wmv]
    out = pl.pallas_call(
        body, name=name,
        out_shape=tuple(jax.ShapeDtypeStruct(w.shape, F32) for w, _, _ in weights for _ in range(4)),
        compiler_params=pltpu.CompilerParams(vmem_limit_bytes=VMEM_LIMIT),
    )(*parts, *flat)
    return [tuple(out[4 * i:4 * i + 4]) for i in range(n)]


def kernel(x, positions, mix_pre_g, mix_post_g, pool_w, pool_scale, kv_norm_g, w_kv, w_q, w_o, sinks, ffn_pre_g, ffn_post_g, ffn_w_in, ffn_conv_w, ffn_conv_b, ffn_w_out, loss_target, m_mix_pre_g, m_mix_post_g, m_pool_w, m_pool_scale, m_kv_norm_g, m_w_kv, m_w_q, m_w_o, m_sinks, m_ffn_pre_g, m_ffn_post_g, m_ffn_w_in, m_ffn_conv_w, m_ffn_conv_b, m_ffn_w_out, v_mix_pre_g, v_mix_post_g, v_pool_w, v_pool_scale, v_kv_norm_g, v_w_kv, v_w_q, v_w_o, v_sinks, v_ffn_pre_g, v_ffn_post_g, v_ffn_w_in, v_ffn_conv_w, v_ffn_conv_b, v_ffn_w_out):
    depth, d_model = mix_pre_g.shape
    n_a = pool_w.shape[0]
    n_b = w_q.shape[0]
    t = x.shape[1]
    fs = ffn_w_in.shape[2]
    half = N_DEV // 2
    n_heads = d_model // HEAD_DIM
    x0 = x.reshape(t, d_model)
    target = loss_target.reshape(t, d_model)

    inv_freq = 1.0 / (ROPE_THETA ** (jnp.arange(0, HEAD_DIM, 2, dtype=F32) / HEAD_DIM))
    ang = positions.reshape(t).astype(F32)[:, None] * inv_freq
    cos, sin = jnp.cos(ang), jnp.sin(ang)
    cos = jnp.tile(cos, (1, 2 * LANES // HEAD_DIM))
    ssin = jnp.tile(jnp.concatenate([-sin, sin], axis=1), (1, LANES // HEAD_DIM))

    wire = lambda a: a.astype(_WIRE_DTYPE)
    w_in_b, w_out_b = wire(ffn_w_in), wire(ffn_w_out)
    pool_w_g, w_in_0, w_out_0, pool_scale_g, conv_w_g = _all_gather(
        [wire(pool_w), w_in_b[:1], w_out_b[:1], pool_scale, ffn_conv_w], [2, 1, 1, 0, 0], "gather_first")
    groups = []
    for l in range(1, depth):
        if l == n_a:
            groups.append(("attn", [wire(w_kv), wire(w_q), wire(w_o)], [0, 1, 1], l))
        groups.append((l, [w_in_b[l:l + 1], w_out_b[l:l + 1]], [1, 1], l))
    over_ici, to_sibling, tokens, after = {}, {}, [], w_in_0
    for key, shards, axes, _ in groups:
        over_ici[key], after = _exchange_start(GATHER_CHIPS, shards, axes, after, f"gather_chips_{key}")
        tokens.append(after)
    started = functools.reduce(lambda a, b: a + b, [tk[0, 0] for tk in tokens])

    def pass_on(layer, after):
        sent = jnp.zeros((), F32)
        for key, _, axes, first in groups:
            if first == layer:
                lands = _exchange_wait(over_ici[key], after, f"gather_chips_wait_{key}")
                to_sibling[key], tk = _exchange_start(GATHER_SIBLING, lands, axes, after, f"gather_sibling_{key}")
                sent = sent + tk[0, 0]
        return sent

    w_in_l, w_out_l = {0: w_in_0}, {0: w_out_0.reshape(1, half * fs, d_model)}
    pool_scale_f = pool_scale_g.transpose(1, 0, 2).reshape(n_a, 1, d_model)
    conv_w_f = conv_w_g.transpose(1, 2, 0, 3)
    pool_w_f = pool_w_g.reshape(n_a, len(POOL_WINDOWS), d_model // len(POOL_WINDOWS), -1)
    conv_b_f = ffn_conv_b.reshape(depth, N_DEV, fs)
    g3 = lambda a: a.reshape(a.shape[0], 1, a.shape[1])
    mix_pre, mix_post, ffn_pre, ffn_post = g3(mix_pre_g) + started, g3(mix_post_g), g3(ffn_pre_g), g3(ffn_post_g)
    kv_g = kv_norm_g.reshape(1, d_model)
    w_kv_f = w_q_f = w_o_f = None

    saved = []
    xc = x0
    kdup = vdup = x_kv = None
    for l in range(depth):
        x_in = xc
        if l < n_a:
            x_mid, dsave, yu = _pool_fwd(x_in, mix_pre, mix_post, pool_w_f, pool_scale_f, l, f"pool_fwd_{l}")
            mixer = (dsave, yu)
        else:
            j = l - n_a
            if j == 0:
                x_kv = x_in
                w_kv_g, w_q_g, w_o_g = _exchange_wait(to_sibling["attn"], x_in, "gather_sibling_wait_attn")
                w_kv_f = w_kv_g.reshape(d_model, -1)
                w_q_f = w_q_g.reshape(n_b, d_model, d_model)
                w_o_f = w_o_g.reshape(n_b, d_model, d_model)
                kdup, vdup = _kv_fwd(x_kv, kv_g, w_kv_f, cos, ssin, "kv_fwd")
            qs = _q_fwd(x_in, mix_pre, w_q_f, cos, ssin, l, j, f"q_fwd_{l}")
            o, lse = _attn_fwd(qs, kdup, vdup, sinks, j, f"attn_fwd_{l}")
            x_mid, mo = _oproj_fwd(x_in, o, w_o_f, mix_post, l, j, f"oproj_fwd_{l}")
            mixer = (qs, o, lse, mo)
        if l > 0:
            w_in_l[l], w_out_g = _exchange_wait(to_sibling[l], x_mid, f"gather_sibling_wait_{l}")
            w_out_l[l] = w_out_g.reshape(1, half * fs, d_model)
        xc, u, uc, f = _ffn_fwd(x_mid, ffn_pre, ffn_post, w_in_l[l], conv_w_f, conv_b_f, w_out_l[l], l, 0, f"ffn_fwd_{l}")
        saved.append((x_in, x_mid, u, uc, f, mixer))
        if l + 1 < depth:
            mix_pre = mix_pre + pass_on(l + 1, xc)

    dx, loss_part = _loss_grad(xc, target, "loss")

    gconv_w, gconv_b = [None] * depth, [None] * depth
    gmix_pre, gmix_post, gffn_pre, gffn_post = [None] * depth, [None] * depth, [None] * depth, [None] * depth
    gpool_scale, gsinks = [None] * n_a, [None] * n_b
    dks, dvs = [], []
    gkv_g = None
    whole = lambda cols: pl.BlockSpec((t, cols), lambda c: (0, 0), pipeline_mode=pl.Buffered(1))
    per_out = lambda cols: pl.BlockSpec((None, t, cols), lambda c: (c, 0, 0))
    by_dev = lambda g: g.reshape(N_DEV, -1, g.shape[-1])
    def small_grads():
        cat = lambda rows: jnp.concatenate(rows, axis=0)
        row = lambda a: a.reshape(1, -1)
        everything = lambda ref, s, me: ref[s]
        lanes = pool_scale.shape[1]
        return [("mix_pre_g", cat(gmix_pre), everything, (mix_pre_g, m_mix_pre_g, v_mix_pre_g)),
                ("mix_post_g", cat(gmix_post), everything, (mix_post_g, m_mix_post_g, v_mix_post_g)),
                ("kv_norm_g", gkv_g, everything, (row(kv_norm_g), row(m_kv_norm_g), row(v_kv_norm_g))),
                ("sinks", cat(gsinks), lambda ref, s, me: ref[s, :, pl.ds(0, n_heads)], (sinks, m_sinks, v_sinks)),
                ("ffn_pre_g", cat(gffn_pre), everything, (ffn_pre_g, m_ffn_pre_g, v_ffn_pre_g)),
                ("ffn_post_g", cat(gffn_post), everything, (ffn_post_g, m_ffn_post_g, v_ffn_post_g)),
                ("ffn_conv_b", jnp.stack(gconv_b).reshape(depth, N_DEV * fs), everything, (ffn_conv_b, m_ffn_conv_b, v_ffn_conv_b)),
                ("pool_scale", cat(gpool_scale), lambda ref, s, me: ref[s, :, pl.ds(pl.multiple_of(me * lanes, lanes), lanes)],
                 (pool_scale, m_pool_scale, v_pool_scale)),
                ("ffn_conv_w", jnp.stack(gconv_w).transpose(0, 2, 1, 3), lambda ref, s, me: ref[s, :, me],
                 (ffn_conv_w, m_ffn_conv_w, v_ffn_conv_w))]

    flying = []

    def launch(going, after, name):
        handle, token = _exchange_start(SCATTER, [g for _, _, g, _ in going], [a for _, _, _, a in going], after, name)
        flying.append(([(nm, slab) for nm, slab, _, _ in going], handle))
        return token

    post = mix_post
    ffn_post_b = ffn_post
    token = None
    for l in reversed(range(depth)):
        x_in, x_mid, u, uc, f, mixer = saved[l]
        duc, gact, df, gconv_b[l], gffn_post[l] = _ffn_bwd_out(dx, f, ffn_post_b, uc, conv_b_f, w_out_l[l], l, 0, f"ffn_bwd_out_{l}")
        dx, du, hf, gconv_w[l], gffn_pre[l] = _ffn_bwd_in(dx, x_mid, ffn_pre, duc, u, conv_w_f, w_in_l[l], l, 0, f"ffn_bwd_in_{l}")
        gin = _tn_matmul(hf, du, whole(d_model), per_out(fs), N_DEV, d_model, fs, f"grad_w_in_{l}", _WIRE_DTYPE)
        going = [("ffn_w_in", l, gin, 0)]
        if l == 0:
            token = launch(going, dx, "scatter_start_0_in")
            going = []
            post = post + token[0, 0]
        gout = _tn_matmul(gact, df, per_out(fs), whole(d_model), half, fs, d_model, f"grad_w_out_{l}", _WIRE_DTYPE)
        going.append(("ffn_w_out", l, by_dev(gout), 0))
        if l < n_a:
            dsave, yu = mixer
            dx, dyu, gpool_scale[l], gmix_pre[l], gmix_post[l] = _pool_bwd(
                dx, x_in, mix_pre, post, dsave, yu, pool_w_f, pool_scale_f, l, f"pool_bwd_{l}")
            gc = d_model // len(POOL_WINDOWS)
            by_group = pl.BlockSpec((t, gc), lambda c: (0, c))
            gpool = _tn_matmul(dsave, dyu, by_group, by_group, len(POOL_WINDOWS), gc, gc, f"grad_pool_w_{l}", _WIRE_DTYPE)
            going.append(("pool_w", l, gpool.reshape(len(POOL_WINDOWS), N_DEV, -1, gc), 1))
        else:
            j = l - n_a
            qs, o, lse, mo = mixer
            do, dmo, gmix_post[l] = _oproj_bwd(dx, mo, w_o_f, post, l, j, f"oproj_bwd_{l}")
            dqs, dk, dv, gsinks[j] = _attn_bwd(qs, kdup, vdup, sinks, lse, do, j, f"attn_bwd_{l}")
            dks.append(dk)
            dvs.append(dv)
            dx, hq, dq, gmix_pre[l] = _q_bwd(dx, dqs, x_in, mix_pre, w_q_f, cos, ssin, l, j, f"q_bwd_{l}")
            if j == 0:
                dx, hkv, dkv, gkv_g = _kv_bwd(dx, x_kv, kv_g, w_kv_f, cos, ssin, dks, dvs, "kv_bwd")
                gkv = _tn_matmul(hkv, dkv, whole(d_model), whole(dkv.shape[1]), 1, d_model, dkv.shape[1], "grad_w_kv", _WIRE_DTYPE)
                going.append(("w_kv", 0, by_dev(gkv), 0))
            go = _tn_matmul(o, dmo, whole(d_model), whole(d_model), 1, d_model, d_model, f"grad_w_o_{l}", _WIRE_DTYPE)
            gq = _tn_matmul(hq, dq, whole(d_model), whole(d_model), 1, d_model, d_model, f"grad_w_q_{l}", _WIRE_DTYPE)
            going += [("w_o", j, by_dev(go), 0), ("w_q", j, by_dev(gq), 0)]
        after = dx
        if l == 0:
            small = small_grads()
            leaving = [g for _, g, _, _ in small] + [jnp.broadcast_to(loss_part, (1, LANES))]
            small_flight, after = _exchange_start(GATHER, leaving, [0] * len(leaving), dx, "gather_small_grads")
        token = launch(going, after, f"scatter_start_{l}")
        ffn_post_b = ffn_post_b + token[0, 0]

    grad_x = dx.reshape(x.shape)

    shard = {"pool_w": (pool_w, m_pool_w, v_pool_w), "w_kv": (w_kv, m_w_kv, v_w_kv), "w_q": (w_q, m_w_q, v_w_q),
             "w_o": (w_o, m_w_o, v_w_o), "ffn_w_in": (ffn_w_in, m_ffn_w_in, v_ffn_w_in),
             "ffn_w_out": (ffn_w_out, m_ffn_w_out, v_ffn_w_out)}
    big = {}

    def arrive(flights, after):
        for idx, (names, handle) in flights:
            parts = _exchange_wait(handle, after, f"scatter_wait_{idx}")
            for (nm, slab), p in zip(names, parts):
                cols = p.shape[-1]
                w2, m2, v2 = (a.reshape(-1, cols) for a in shard[nm])
                big[nm] = _adamw(p.reshape(N_DEV, -1, cols), w2, m2, v2, slab, big.get(nm), f"adamw_{nm}_{slab}")
                after = big[nm][0]
        return after

    done = arrive(list(enumerate(flying)), token)
    *small_parts, loss_parts = _exchange_wait(small_flight, done, "gather_small_grads_wait")
    loss = jnp.sum(loss_parts[:, 0, 0])
    upd = _adamw_small(small_parts, [pick for _, _, pick, _ in small], [wmv for _, _, _, wmv in small], "adamw_small")
    res = {nm: tuple(r.reshape(shard[nm][0].shape) for r in out) for nm, out in big.items()}
    for (nm, _, _, _), out in zip(small, upd):
        res[nm] = tuple(a.reshape(kv_norm_g.shape) for a in out) if nm == "kv_norm_g" else out

    order = ["mix_pre_g", "mix_post_g", "pool_w", "pool_scale", "kv_norm_g", "w_kv", "w_q", "w_o", "sinks", "ffn_pre_g",
             "ffn_post_g", "ffn_w_in", "ffn_conv_w", "ffn_conv_b", "ffn_w_out"]
    return (loss, grad_x, *[res[nm][0] for nm in order], *[res[nm][1] for nm in order],
            *[res[nm][2] for nm in order], *[res[nm][3] for nm in order])
```

```python
import functools
import math

import jax
import jax.numpy as jnp
from jax import lax
from jax.experimental import pallas as pl
from jax.experimental.pallas import tpu as pltpu

F32 = jnp.float32
_MXU_DTYPE = jnp.bfloat16
_ACT_DTYPE = jnp.bfloat16
_WIRE_DTYPE = jnp.bfloat16
_SAVE_DTYPE = jnp.bfloat16

N_DEV = 8
POOL_WINDOWS = (2, 4, 8, 16)
POOL_HALO = 32
assert POOL_WINDOWS == tuple(2 ** (g + 1) for g in range(len(POOL_WINDOWS))) and 8 * len(POOL_WINDOWS) <= POOL_HALO
HEAD_DIM = 64
N_KV_HEADS = 4
WINDOW = 128
BLOCK = 128
LANES = 128
ROPE_THETA = 10000.0
ATTN_SCALE = 1.0 / math.sqrt(HEAD_DIM)
NEG_INF = -1e30
RMS_EPS = 1e-6
CONV_HALO = 8
SAVE_HALO = 16
PROJ_TILE = 512
ADAM_LR = 0.001
ADAM_B1 = 0.9
ADAM_B2 = 0.999
ADAM_EPS = 1e-08
ADAM_WD = 0.01
ADAM_STEP = 10
VMEM_LIMIT = 56 * 1024 * 1024
MESH = pl.DeviceIdType.MESH


def _params(n_axes=1, vmem=VMEM_LIMIT):
    return pltpu.CompilerParams(dimension_semantics=("arbitrary",) * n_axes, vmem_limit_bytes=vmem)


def _resident(shape, index):
    return pl.BlockSpec(shape, lambda *_: index, pipeline_mode=pl.Buffered(1))


def _const(shape, index=None):
    index = (0,) * len(shape) if index is None else index
    return pl.BlockSpec(shape, lambda *_: index)


def _rows(tm, cols):
    return pl.BlockSpec((tm, cols), lambda i: (i, 0))


def _row_tile(t, most=256):
    for tm in (512, 256, 128, 64, 32, 16, 8):
        if tm <= most and t % tm == 0:
            return tm
    raise ValueError(f"sequence length {t} is not a multiple of 8")


def _mm(a, b):
    return jnp.dot(a.astype(_MXU_DTYPE), b.astype(_MXU_DTYPE), preferred_element_type=F32)


def _mm_tb(a, b):
    return lax.dot_general(a.astype(_MXU_DTYPE), b.astype(_MXU_DTYPE), (((1,), (1,)), ((), ())),
                           preferred_element_type=F32)


def _mm_ta(a, b):
    return lax.dot_general(a.astype(_MXU_DTYPE), b.astype(_MXU_DTYPE), (((0,), (0,)), ((), ())),
                           preferred_element_type=F32)


def _rms_r(x):
    return lax.rsqrt(jnp.mean(x * x, axis=-1, keepdims=True) + RMS_EPS)


def _rms_fwd(x, g):
    return (x * _rms_r(x)) * g


def _rms_bwd(x, g, dy):
    r = _rms_r(x)
    xh = x * r
    dg = jnp.sum(dy * xh, axis=0, keepdims=True)
    dxh = dy * g
    dx = r * (dxh - xh * jnp.mean(dxh * xh, axis=-1, keepdims=True))
    return dx, dg


_GELU_C = math.sqrt(2.0 / math.pi)


def _gelu_parts(z):
    z2 = z * z
    e = jnp.exp(z * (-2.0 * _GELU_C - (2.0 * _GELU_C * 0.044715) * z2))
    cdf = pl.reciprocal(1.0 + e, approx=False)
    dz = cdf + (z * (cdf * (1.0 - cdf))) * (2.0 * _GELU_C + (6.0 * _GELU_C * 0.044715) * z2)
    return cdf, dz


def _lane_iota(shape):
    return lax.broadcasted_iota(jnp.int32, shape, len(shape) - 1)


def _rope_partner(xb):
    first = (_lane_iota(xb.shape) & 32) == 0
    return jnp.where(first, pltpu.roll(xb, LANES - 32, 1), pltpu.roll(xb, 32, 1))


def _rope_fwd(xb, cos, ssin):
    return xb * cos + _rope_partner(xb) * ssin


def _rope_bwd(dyb, cos, ssin):
    return dyb * cos - _rope_partner(dyb) * ssin


def _low_half(shape):
    return (_lane_iota(shape) & 64) == 0


def _pool_fwd(x, pre_g, post_g, w, scale, layer, name):
    t, d_model = x.shape
    tm = _row_tile(t)
    n_groups, gc = w.shape[1], w.shape[2]

    def body(x_ref, pre_ref, post_ref, w_ref, sc_ref, xo_ref, d_ref, yu_ref, hbuf, sbuf):
        i = pl.program_id(0)

        @pl.when(i == 0)
        def _():
            hbuf[pl.ds(0, POOL_HALO), :] = jnp.zeros((POOL_HALO, d_model), F32)

        xv = x_ref[...]
        hbuf[pl.ds(POOL_HALO, tm), :] = _rms_fwd(xv, pre_ref[...])
        tok = i * tm + lax.broadcasted_iota(jnp.int32, (tm, 1), 0)
        yus = []
        for gi, wnd in enumerate(POOL_WINDOWS):
            first, reach = CONV_HALO * (gi + 1), wnd // 2
            n = POOL_HALO + tm - first
            cols = pl.ds(gi * gc, d_model - gi * gc)
            src = hbuf if gi == 0 else sbuf.at[gi - 1]
            level = src[pl.ds(first, n), cols] + src[pl.ds(first - reach, n), cols]
            if gi + 1 < len(POOL_WINDOWS):
                sbuf[gi, pl.ds(first, n), cols] = level
            h = hbuf[pl.ds(POOL_HALO, tm), pl.ds(gi * gc, gc)]
            cnt = jnp.minimum(tok + 1, wnd).astype(F32)
            dg = level[POOL_HALO - first:, :gc] / cnt - h
            d_ref[:, pl.ds(gi * gc, gc)] = dg.astype(d_ref.dtype)
            yus.append(_mm(dg, w_ref[gi]))
        hbuf[pl.ds(0, POOL_HALO), :] = hbuf[pl.ds(tm, POOL_HALO), :]
        yu = jnp.concatenate(yus, axis=1)
        yu_ref[...] = yu
        xo_ref[...] = xv + _rms_fwd(yu * sc_ref[...], post_ref[...])

    return pl.pallas_call(
        body, name=name, grid=(t // tm,),
        out_shape=(jax.ShapeDtypeStruct((t, d_model), F32), jax.ShapeDtypeStruct((t, d_model), _ACT_DTYPE),
                   jax.ShapeDtypeStruct((t, d_model), F32)),
        in_specs=[_rows(tm, d_model), _const((None, 1, d_model), (layer, 0, 0)), _const((None, 1, d_model), (layer, 0, 0)),
                  _const((None, n_groups, gc, gc), (layer, 0, 0, 0)), _const((None, 1, d_model), (layer, 0, 0))],
        out_specs=(_rows(tm, d_model), _rows(tm, d_model), _rows(tm, d_model)),
        scratch_shapes=[pltpu.VMEM((POOL_HALO + tm, d_model), F32),
                        pltpu.VMEM((len(POOL_WINDOWS) - 1, POOL_HALO + tm, d_model), F32)],
        compiler_params=_params(),
    )(x, pre_g, post_g, w, scale)


def _pool_bwd(dx, x, pre_g, post_g, d, yu, w, scale, layer, name):
    t, d_model = x.shape
    tm = _row_tile(t)
    nt = t // tm
    n_groups, gc = w.shape[1], w.shape[2]
    rev = lambda i: (nt - 1 - i, 0)
    rows = pl.BlockSpec((tm, d_model), rev)

    def body(dx_ref, x_ref, pre_ref, post_ref, d_ref, yu_ref, w_ref, sc_ref,
             dxi_ref, dyu_ref, dsc_ref, dpre_ref, dpost_ref, zbuf, sbuf):
        i = pl.program_id(0)

        @pl.when(i == 0)
        def _():
            zbuf[pl.ds(tm, POOL_HALO), :] = jnp.zeros((POOL_HALO, d_model), F32)
            dsc_ref[...] = jnp.zeros_like(dsc_ref)
            dpre_ref[...] = jnp.zeros_like(dpre_ref)
            dpost_ref[...] = jnp.zeros_like(dpost_ref)

        dxo = dx_ref[...]
        yuv = yu_ref[...]
        sc = sc_ref[...]
        dm, dpost = _rms_bwd(yuv * sc, post_ref[...], dxo)
        dpost_ref[...] += dpost
        dsc_ref[...] += jnp.sum(dm * yuv, axis=0, keepdims=True)
        dyu = dm * sc
        dyu_ref[...] = dyu.astype(dyu_ref.dtype)
        tok = (nt - 1 - i) * tm + lax.broadcasted_iota(jnp.int32, (tm, 1), 0)
        dds = []
        for gi, wnd in enumerate(POOL_WINDOWS):
            cols = pl.ds(gi * gc, gc)
            dd = _mm_tb(dyu[:, gi * gc:(gi + 1) * gc], w_ref[gi])
            cnt = jnp.minimum(tok + 1, wnd).astype(F32)
            zbuf[pl.ds(0, tm), cols] = dd / cnt
            dds.append(dd)
        dhs = []
        for gi, wnd in enumerate(POOL_WINDOWS):
            reach = wnd // 2
            n = tm + POOL_HALO - CONV_HALO * (gi + 1)
            cols = pl.ds(gi * gc, d_model - gi * gc)
            src = zbuf if gi == 0 else sbuf.at[gi - 1]
            level = src[pl.ds(0, n), cols] + src[pl.ds(reach, n), cols]
            if gi + 1 < len(POOL_WINDOWS):
                sbuf[gi, pl.ds(0, n), cols] = level
            dhs.append(level[:tm, :gc] - dds[gi])
        zbuf[pl.ds(tm, POOL_HALO), :] = zbuf[pl.ds(0, POOL_HALO), :]
        dh = jnp.concatenate(dhs, axis=1)
        dxp, dpre = _rms_bwd(x_ref[...], pre_ref[...], dh)
        dpre_ref[...] += dpre
        dxi_ref[...] = dxo + dxp

    vec = jax.ShapeDtypeStruct((1, d_model), F32)
    return pl.pallas_call(
        body, name=name, grid=(nt,),
        out_shape=(jax.ShapeDtypeStruct((t, d_model), F32), jax.ShapeDtypeStruct((t, d_model), _ACT_DTYPE), vec, vec, vec),
        in_specs=[rows, rows, _const((None, 1, d_model), (layer, 0, 0)), _const((None, 1, d_model), (layer, 0, 0)), rows, rows,
                  _const((None, n_groups, gc, gc), (layer, 0, 0, 0)), _const((None, 1, d_model), (layer, 0, 0))],
        out_specs=(rows, rows, _const((1, d_model)), _const((1, d_model)), _const((1, d_model))),
        scratch_shapes=[pltpu.VMEM((tm + POOL_HALO, d_model), F32),
                        pltpu.VMEM((len(POOL_WINDOWS) - 1, tm + POOL_HALO, d_model), F32)],
        compiler_params=_params(),
    )(dx, x, pre_g, post_g, d, yu, w, scale)


def _conv_taps(cw_ref, s):
    return [cw_ref[k, pl.ds(s, 1), :] for k in range(3)]


def _shift_down(v, k, before):
    rolled = pltpu.roll(v, k, 0)
    row = lax.broadcasted_iota(jnp.int32, before.shape, 0)
    head = jnp.where(row < k, pltpu.roll(before, k, 0), rolled[:CONV_HALO])
    return jnp.concatenate([head, rolled[CONV_HALO:]], axis=0)


def _shift_up(v, k, after):
    rows = v.shape[0]
    rolled = pltpu.roll(v, rows - k, 0)
    row = lax.broadcasted_iota(jnp.int32, after.shape, 0)
    tail = jnp.where(row >= CONV_HALO - k, pltpu.roll(after, CONV_HALO - k, 0), rolled[rows - CONV_HALO:])
    return jnp.concatenate([rolled[:rows - CONV_HALO], tail], axis=0)


def _ffn_fwd(x, pre_g, post_g, w_in, conv_w, conv_b, w_out, layer, w_layer, name):
    t, d_model = x.shape
    tm = _row_tile(t)
    fs = w_in.shape[3]
    half = N_DEV // 2

    def body(x_ref, xn_ref, pre_ref, post_ref, win_ref, cw_ref, cb_ref, wout_ref, xo_ref, u_ref, uc_ref, f_ref,
             carry, h_next, u_next):
        i = pl.program_id(0)
        project = lambda h, b: [jnp.dot(h, win_ref[s], preferred_element_type=F32) for s in (b, b + half)]

        @pl.when(i == 0)
        def _():
            carry[...] = jnp.zeros_like(carry)
            h_next[...] = _rms_fwd(x_ref[...], pre_ref[...]).astype(_MXU_DTYPE)
            u_next[0], u_next[1] = project(h_next[...], 0)

        xv = x_ref[...]
        hf = h_next[...]
        f = jnp.zeros((tm, d_model), F32)
        ahead = [u_next[0], u_next[1]]
        for b in range(half):
            if b + 1 < half:
                us, ahead = ahead, project(hf, b + 1)
            else:
                hfn = _rms_fwd(xn_ref[...], pre_ref[...]).astype(_MXU_DTYPE)
                us, ahead = ahead, project(hfn, 0)
            ucs = []
            for s, u in zip((b, b + half), us):
                u_ref[s] = u.astype(u_ref.dtype)
                before = carry[s]
                carry[s] = u[tm - CONV_HALO:]
                w0, w1, w2 = _conv_taps(cw_ref, s)
                uc = ((w0 * _shift_down(u, 2, before) + w1 * _shift_down(u, 1, before)) + w2 * u) + cb_ref[pl.ds(s, 1), :]
                uc_ref[s] = uc.astype(uc_ref.dtype)
                ucs.append(uc)
            gate, val = ucs
            cdf, _ = _gelu_parts(gate)
            f = f + _mm((gate * cdf) * val, wout_ref[pl.ds(b * fs, fs), :])
        h_next[...] = hfn
        u_next[0], u_next[1] = ahead
        f_ref[...] = f
        xo_ref[...] = xv + _rms_fwd(f, post_ref[...])

    nt = t // tm
    tile3 = pl.BlockSpec((N_DEV, tm, fs), lambda i: (0, i, 0))
    saved = jax.ShapeDtypeStruct((N_DEV, t, fs), _SAVE_DTYPE)
    return pl.pallas_call(
        body, name=name, grid=(nt,),
        out_shape=(jax.ShapeDtypeStruct((t, d_model), F32), saved, saved, jax.ShapeDtypeStruct((t, d_model), F32)),
        in_specs=[_rows(tm, d_model), pl.BlockSpec((tm, d_model), lambda i: (jnp.minimum(i + 1, nt - 1), 0)),
                  _const((None, 1, d_model), (layer, 0, 0)), _const((None, 1, d_model), (layer, 0, 0)),
                  _resident((None, N_DEV, d_model, fs), (w_layer, 0, 0, 0)), _const((None, 3, N_DEV, fs), (layer, 0, 0, 0)),
                  _const((None, N_DEV, fs), (layer, 0, 0)), _resident((None, half * fs, d_model), (w_layer, 0, 0))],
        out_specs=(_rows(tm, d_model), tile3, tile3, _rows(tm, d_model)),
        scratch_shapes=[pltpu.VMEM((N_DEV, CONV_HALO, fs), F32), pltpu.VMEM((tm, d_model), _MXU_DTYPE),
                        pltpu.VMEM((2, tm, fs), F32)],
        compiler_params=_params(),
    )(x, x, pre_g, post_g, w_in, conv_w, conv_b, w_out)


def _ffn_bwd_out(dx, f, post_g, uc, conv_b, w_out, layer, w_layer, name):
    t, d_model = dx.shape
    tm = _row_tile(t)
    fs = uc.shape[2]
    half = N_DEV // 2

    def body(dx_ref, f_ref, post_ref, uc_ref, wout_ref, duc_ref, g_ref, df_ref, dcb_ref, dpost_ref):
        i = pl.program_id(0)

        @pl.when(i == 0)
        def _():
            dcb_ref[...] = jnp.zeros_like(dcb_ref)
            dpost_ref[...] = jnp.zeros_like(dpost_ref)

        df, dpost = _rms_bwd(f_ref[...], post_ref[...], dx_ref[...])
        dpost_ref[...] += dpost
        dfm = df.astype(_MXU_DTYPE)
        df_ref[...] = dfm
        project = lambda b: _mm_tb(dfm, wout_ref[pl.ds(b * fs, fs), :])
        ahead = project(0)
        for b in range(half):
            dg, ahead = ahead, project(b + 1) if b + 1 < half else None
            gate = uc_ref[b].astype(F32)
            val = uc_ref[b + half].astype(F32)
            cdf, dgelu = _gelu_parts(gate)
            ge = gate * cdf
            g_ref[b] = (ge * val).astype(g_ref.dtype)
            for s, dd in ((b, dg * val * dgelu), (b + half, dg * ge)):
                duc_ref[s] = dd.astype(duc_ref.dtype)
                dcb_ref[pl.ds(s, 1), :] += jnp.sum(dd, axis=0, keepdims=True)

    tile3 = pl.BlockSpec((N_DEV, tm, fs), lambda i: (0, i, 0))
    return pl.pallas_call(
        body, name=name, grid=(t // tm,),
        out_shape=(jax.ShapeDtypeStruct((N_DEV, t, fs), _SAVE_DTYPE), jax.ShapeDtypeStruct((half, t, fs), _ACT_DTYPE),
                   jax.ShapeDtypeStruct((t, d_model), _ACT_DTYPE), jax.ShapeDtypeStruct((N_DEV, fs), F32),
                   jax.ShapeDtypeStruct((1, d_model), F32)),
        in_specs=[_rows(tm, d_model), _rows(tm, d_model), _const((None, 1, d_model), (layer, 0, 0)), tile3,
                  _resident((None, half * fs, d_model), (w_layer, 0, 0))],
        out_specs=(tile3, pl.BlockSpec((half, tm, fs), lambda i: (0, i, 0)), _rows(tm, d_model),
                   _const((N_DEV, fs)), _const((1, d_model))),
        compiler_params=_params(),
    )(dx, f, post_g, uc, w_out)


def _ffn_bwd_in(dx, x, pre_g, duc, u, conv_w, w_in, layer, w_layer, name):
    t, d_model = dx.shape
    tm = _row_tile(t)
    nt = t // tm
    fs = duc.shape[2]
    hb = SAVE_HALO
    per_tile = tm // hb

    def body(dx_ref, x_ref, pre_ref, duc_ref, dn_ref, u_ref, cw_ref, win_ref, dxi_ref, du_ref, hf_ref, dcw_ref, dpre_ref):
        i = pl.program_id(0)

        @pl.when(i == 0)
        def _():
            dcw_ref[...] = jnp.zeros_like(dcw_ref)
            dpre_ref[...] = jnp.zeros_like(dpre_ref)

        xv = x_ref[...]
        pre = pre_ref[...]
        hf_ref[...] = _rms_fwd(xv, pre).astype(hf_ref.dtype)
        dhf = jnp.zeros((tm, d_model), F32)
        for s in range(N_DEV):
            d0 = duc_ref[s].astype(F32)
            after = jnp.where(i == nt - 1, 0.0, dn_ref[s].astype(F32)[:CONV_HALO])
            d1 = _shift_up(d0, 1, after)
            d2 = _shift_up(d0, 2, after)
            uv = u_ref[s].astype(F32)
            for k, dk in ((2, d0), (1, d1), (0, d2)):
                dcw_ref[k, pl.ds(s, 1), :] += jnp.sum(dk * uv, axis=0, keepdims=True)
            w0, w1, w2 = _conv_taps(cw_ref, s)
            du = (w2 * d0 + w1 * d1 + w0 * d2).astype(_MXU_DTYPE)
            du_ref[s] = du
            dhf = dhf + _mm_tb(du, win_ref[s])
        dxp, dpre = _rms_bwd(xv, pre, dhf)
        dpre_ref[...] += dpre
        dxi_ref[...] = dx_ref[...] + dxp

    tile3 = pl.BlockSpec((N_DEV, tm, fs), lambda i: (0, i, 0))
    return pl.pallas_call(
        body, name=name, grid=(nt,),
        out_shape=(jax.ShapeDtypeStruct((t, d_model), F32), jax.ShapeDtypeStruct((N_DEV, t, fs), _ACT_DTYPE),
                   jax.ShapeDtypeStruct((t, d_model), _ACT_DTYPE), jax.ShapeDtypeStruct((3, N_DEV, fs), F32),
                   jax.ShapeDtypeStruct((1, d_model), F32)),
        in_specs=[_rows(tm, d_model), _rows(tm, d_model), _const((None, 1, d_model), (layer, 0, 0)), tile3,
                  pl.BlockSpec((N_DEV, hb, fs), lambda i: (0, jnp.minimum((i + 1) * per_tile, t // hb - 1), 0)), tile3,
                  _const((None, 3, N_DEV, fs), (layer, 0, 0, 0)), _resident((None, N_DEV, d_model, fs), (w_layer, 0, 0, 0))],
        out_specs=(_rows(tm, d_model), tile3, _rows(tm, d_model), _const((3, N_DEV, fs)), _const((1, d_model))),
        compiler_params=_params(),
    )(dx, x, pre_g, duc, duc, u, conv_w, w_in)


def _tn_matmul(a, b, a_spec, b_spec, n_out, m, n, name, out_dtype):
    def body(a_ref, b_ref, o_ref):
        o_ref[...] = _mm_ta(a_ref[...], b_ref[...]).astype(o_ref.dtype)

    return pl.pallas_call(
        body, name=name, grid=(n_out,),
        out_shape=jax.ShapeDtypeStruct((n_out, m, n), out_dtype),
        in_specs=[a_spec, b_spec],
        out_specs=pl.BlockSpec((None, m, n), lambda c: (c, 0, 0)),
        compiler_params=_params(),
    )(a, b)


def _kv_fwd(x, kv_g, w_kv, cos, ssin, name):
    t, d_model = x.shape
    tm = _row_tile(t, PROJ_TILE)
    kvd = w_kv.shape[1] // 2
    pairs = kvd // LANES

    def body(x_ref, g_ref, w_ref, cos_ref, sin_ref, k_ref, v_ref):
        kv = _mm(_rms_fwd(x_ref[...], g_ref[...]), w_ref[...])
        low = _low_half((tm, LANES))
        for j in range(pairs):
            kb = _rope_fwd(kv[:, j * LANES:(j + 1) * LANES], cos_ref[...], sin_ref[...])
            vb = kv[:, kvd + j * LANES:kvd + (j + 1) * LANES]
            for blk, ref in ((kb, k_ref), (vb, v_ref)):
                sw = pltpu.roll(blk, 64, 1)
                ref[2 * j] = jnp.where(low, blk, sw).astype(ref.dtype)
                ref[2 * j + 1] = jnp.where(low, sw, blk).astype(ref.dtype)

    heads = jax.ShapeDtypeStruct((N_KV_HEADS, t, LANES), _ACT_DTYPE)
    hspec = pl.BlockSpec((N_KV_HEADS, tm, LANES), lambda i: (0, i, 0))
    return pl.pallas_call(
        body, name=name, grid=(t // tm,), out_shape=(heads, heads),
        in_specs=[_rows(tm, d_model), _const((1, d_model)), _const(w_kv.shape), _rows(tm, LANES), _rows(tm, LANES)],
        out_specs=(hspec, hspec), compiler_params=_params(),
    )(x, kv_g, w_kv, cos, ssin)


def _kv_bwd(dx, x, kv_g, w_kv, cos, ssin, dks, dvs, name):
    t, d_model = x.shape
    tm = _row_tile(t, PROJ_TILE)
    kvd = w_kv.shape[1] // 2
    pairs = kvd // LANES
    n_users = len(dks)

    def body(dx_ref, x_ref, g_ref, w_ref, cos_ref, sin_ref, *refs):
        dk_refs, dv_refs = refs[:n_users], refs[n_users:2 * n_users]
        dxi_ref, h_ref, dkv_ref, dg_ref = refs[2 * n_users:]
        dk_ref = functools.reduce(lambda a, b: a + b, [r[...] for r in dk_refs])
        dv_ref = functools.reduce(lambda a, b: a + b, [r[...] for r in dv_refs])
        i = pl.program_id(0)

        @pl.when(i == 0)
        def _():
            dg_ref[...] = jnp.zeros_like(dg_ref)

        xv = x_ref[...]
        g = g_ref[...]
        h_ref[...] = _rms_fwd(xv, g).astype(h_ref.dtype)
        low = _low_half((tm, LANES))
        dks, dvs = [], []
        for j in range(pairs):
            dkb = jnp.where(low, dk_ref[2 * j], dk_ref[2 * j + 1])
            dks.append(_rope_bwd(dkb, cos_ref[...], sin_ref[...]))
            dvs.append(jnp.where(low, dv_ref[2 * j], dv_ref[2 * j + 1]))
        dkv = jnp.concatenate(dks + dvs, axis=1).astype(_MXU_DTYPE)
        dkv_ref[...] = dkv
        dxp, dg = _rms_bwd(xv, g, _mm_tb(dkv, w_ref[...]))
        dg_ref[...] += dg
        dxi_ref[...] = dx_ref[...] + dxp

    hspec = pl.BlockSpec((N_KV_HEADS, tm, LANES), lambda i: (0, i, 0))
    return pl.pallas_call(
        body, name=name, grid=(t // tm,),
        out_shape=(jax.ShapeDtypeStruct((t, d_model), F32), jax.ShapeDtypeStruct((t, d_model), _ACT_DTYPE),
                   jax.ShapeDtypeStruct((t, 2 * kvd), _ACT_DTYPE), jax.ShapeDtypeStruct((1, d_model), F32)),
        in_specs=[_rows(tm, d_model), _rows(tm, d_model), _const((1, d_model)), _const(w_kv.shape),
                  _rows(tm, LANES), _rows(tm, LANES)] + [hspec] * (2 * n_users),
        out_specs=(_rows(tm, d_model), _rows(tm, d_model), _rows(tm, 2 * kvd), _const((1, d_model))),
        compiler_params=_params(),
    )(dx, x, kv_g, w_kv, cos, ssin, *dks, *dvs)


def _q_fwd(x, pre_g, w_q, cos, ssin, layer, j, name):
    t, d_model = x.shape
    tm = _row_tile(t, PROJ_TILE)

    def body(x_ref, g_ref, w_ref, cos_ref, sin_ref, q_ref):
        q = _mm(_rms_fwd(x_ref[...], g_ref[...]), w_ref[...])
        for p in range(d_model // LANES):
            cols = slice(p * LANES, (p + 1) * LANES)
            q_ref[:, cols] = (_rope_fwd(q[:, cols], cos_ref[...], sin_ref[...]) * ATTN_SCALE).astype(q_ref.dtype)

    return pl.pallas_call(
        body, name=name, grid=(t // tm,), out_shape=jax.ShapeDtypeStruct((t, d_model), _ACT_DTYPE),
        in_specs=[_rows(tm, d_model), _const((None, 1, d_model), (layer, 0, 0)), _const((None, d_model, d_model), (j, 0, 0)),
                  _rows(tm, LANES), _rows(tm, LANES)],
        out_specs=_rows(tm, d_model), compiler_params=_params(),
    )(x, pre_g, w_q, cos, ssin)


def _q_bwd(dx, dqs, x, pre_g, w_q, cos, ssin, layer, j, name):
    t, d_model = x.shape
    tm = _row_tile(t, PROJ_TILE)

    def body(dx_ref, dq_ref, x_ref, g_ref, w_ref, cos_ref, sin_ref, dxi_ref, h_ref, dqo_ref, dg_ref):
        i = pl.program_id(0)

        @pl.when(i == 0)
        def _():
            dg_ref[...] = jnp.zeros_like(dg_ref)

        xv = x_ref[...]
        g = g_ref[...]
        h_ref[...] = _rms_fwd(xv, g).astype(h_ref.dtype)
        parts = []
        for p in range(d_model // LANES):
            cols = slice(p * LANES, (p + 1) * LANES)
            parts.append(_rope_bwd(dq_ref[:, cols] * ATTN_SCALE, cos_ref[...], sin_ref[...]))
        dq = jnp.concatenate(parts, axis=1).astype(_MXU_DTYPE)
        dqo_ref[...] = dq
        dxp, dg = _rms_bwd(xv, g, _mm_tb(dq, w_ref[...]))
        dg_ref[...] += dg
        dxi_ref[...] = dx_ref[...] + dxp

    act = jax.ShapeDtypeStruct((t, d_model), _ACT_DTYPE)
    return pl.pallas_call(
        body, name=name, grid=(t // tm,),
        out_shape=(jax.ShapeDtypeStruct((t, d_model), F32), act, act, jax.ShapeDtypeStruct((1, d_model), F32)),
        in_specs=[_rows(tm, d_model), _rows(tm, d_model), _rows(tm, d_model), _const((None, 1, d_model), (layer, 0, 0)),
                  _const((None, d_model, d_model), (j, 0, 0)), _rows(tm, LANES), _rows(tm, LANES)],
        out_specs=(_rows(tm, d_model), _rows(tm, d_model), _rows(tm, d_model), _const((1, d_model))),
        compiler_params=_params(),
    )(dx, dqs, x, pre_g, w_q, cos, ssin)


def _stack_heads(pairs):
    low = _low_half(pairs[0].shape)
    zero = jnp.zeros_like(pairs[0])
    return jnp.concatenate([h for blk in pairs for h in (jnp.where(low, blk, zero), jnp.where(low, zero, blk))], axis=0)


def _unstack_heads(stacked, i):
    a, b = stacked[2 * i * BLOCK:(2 * i + 1) * BLOCK], stacked[(2 * i + 1) * BLOCK:(2 * i + 2) * BLOCK]
    return jnp.where(_low_half(a.shape), a, b)


def _attn_scores(q_pairs, k2, n, sinks):
    qst = _stack_heads(q_pairs)
    s = _mm_tb(qst, k2)
    row = lax.broadcasted_iota(jnp.int32, s.shape, 0)
    col = lax.broadcasted_iota(jnp.int32, s.shape, 1)
    rel = BLOCK + (row & (BLOCK - 1)) - col
    valid = (rel >= 0) & (rel < WINDOW) & (n * BLOCK + col - BLOCK >= 0)
    s = jnp.where(valid, s, NEG_INF)
    rows1 = lax.broadcasted_iota(jnp.int32, (s.shape[0], 1), 0)
    sink = jnp.full((s.shape[0], 1), sinks[-1], F32)
    for i in reversed(range(len(sinks) - 1)):
        sink = jnp.where(rows1 < (i + 1) * BLOCK, sinks[i], sink)
    return qst, s, sink


def _attn_fwd(qs, kdup, vdup, sinks, j, name):
    t, d_model = qs.shape
    nb = t // BLOCK
    n_pairs = d_model // LANES
    per_group = n_pairs // N_KV_HEADS

    def body(sink_ref, q_ref, kp_ref, ko_ref, vp_ref, vo_ref, o_ref, lse_ref):
        n = pl.program_id(0)
        lane = _lane_iota((BLOCK, LANES))
        lse = jnp.zeros((BLOCK, LANES), F32)
        for hk in range(N_KV_HEADS):
            pairs = range(hk * per_group, (hk + 1) * per_group)
            heads = range(2 * pairs[0], 2 * pairs[-1] + 2)
            k2 = jnp.concatenate([kp_ref[hk], ko_ref[hk]], axis=0)
            v2 = jnp.concatenate([vp_ref[hk], vo_ref[hk]], axis=0)
            _, s, sink = _attn_scores([q_ref[:, p * LANES:(p + 1) * LANES] for p in pairs], k2, n,
                                      [sink_ref[j, h] for h in heads])
            m = jnp.maximum(jnp.max(s, axis=-1, keepdims=True), sink)
            pe = jnp.exp(s - m)
            denom = jnp.sum(pe, axis=-1, keepdims=True) + jnp.exp(sink - m)
            o2 = _mm(pe, v2) / denom
            l2 = m + jnp.log(denom)
            for i, p in enumerate(pairs):
                o_ref[:, p * LANES:(p + 1) * LANES] = _unstack_heads(o2, i).astype(o_ref.dtype)
            for i, h in enumerate(heads):
                lse = jnp.where(lane == h, l2[i * BLOCK:(i + 1) * BLOCK], lse)
        lse_ref[...] = lse

    prev = pl.BlockSpec((N_KV_HEADS, BLOCK, LANES), lambda n: (0, jnp.maximum(n - 1, 0), 0))
    own = pl.BlockSpec((N_KV_HEADS, BLOCK, LANES), lambda n: (0, n, 0))
    return pl.pallas_call(
        body, name=name, grid=(nb,),
        out_shape=(jax.ShapeDtypeStruct((t, d_model), _ACT_DTYPE), jax.ShapeDtypeStruct((t, LANES), F32)),
        in_specs=[pl.BlockSpec(memory_space=pltpu.SMEM), _rows(BLOCK, d_model), prev, own, prev, own],
        out_specs=(_rows(BLOCK, d_model), _rows(BLOCK, LANES)), compiler_params=_params(),
    )(sinks, qs, kdup, kdup, vdup, vdup)


def _attn_bwd(qs, kdup, vdup, sinks, lse, do, j, name):
    t, d_model = qs.shape
    nb = t // BLOCK
    n_pairs = d_model // LANES
    per_group = n_pairs // N_KV_HEADS
    rev = lambda n: nb - 1 - n

    def body(sink_ref, q_ref, kp_ref, ko_ref, vp_ref, vo_ref, lse_ref, do_ref, dq_ref, dk_ref, dv_ref, ds_ref, ck, cv):
        i = pl.program_id(0)
        n = nb - 1 - i

        @pl.when(i == 0)
        def _():
            ck[...] = jnp.zeros_like(ck)
            cv[...] = jnp.zeros_like(cv)
            ds_ref[...] = jnp.zeros_like(ds_ref)

        lane = _lane_iota((BLOCK, LANES))
        lane1 = _lane_iota((1, LANES))
        lsev = lse_ref[...]
        dsink = jnp.zeros((1, LANES), F32)
        for hk in range(N_KV_HEADS):
            pairs = range(hk * per_group, (hk + 1) * per_group)
            heads = range(2 * pairs[0], 2 * pairs[-1] + 2)
            k2 = jnp.concatenate([kp_ref[hk], ko_ref[hk]], axis=0)
            v2 = jnp.concatenate([vp_ref[hk], vo_ref[hk]], axis=0)
            qst, s, sink = _attn_scores([q_ref[:, p * LANES:(p + 1) * LANES] for p in pairs], k2, n,
                                        [sink_ref[j, h] for h in heads])
            l2 = jnp.concatenate([jnp.sum(jnp.where(lane == h, lsev, 0.0), axis=-1, keepdims=True) for h in heads], axis=0)
            pn = jnp.exp(s - l2)
            dost = _stack_heads([do_ref[:, p * LANES:(p + 1) * LANES] for p in pairs])
            dp = _mm_tb(dost, v2)
            dr = jnp.sum(pn * dp, axis=-1, keepdims=True)
            dsm = (pn * (dp - dr)).astype(_MXU_DTYPE)
            dsk = -jnp.exp(sink - l2) * dr
            for i, h in enumerate(heads):
                dsink = dsink + jnp.where(lane1 == h, jnp.sum(dsk[i * BLOCK:(i + 1) * BLOCK]), 0.0)
            dq2 = _mm(dsm, k2)
            for i, p in enumerate(pairs):
                dq_ref[:, p * LANES:(p + 1) * LANES] = _unstack_heads(dq2, i)
            for acc, carry, ref in ((_mm_ta(dsm, qst), ck, dk_ref), (_mm_ta(pn, dost), cv, dv_ref)):
                folded = acc + pltpu.roll(acc, 64, 1)
                ref[hk] = folded[BLOCK:] + carry[hk]
                carry[hk] = folded[:BLOCK]
        ds_ref[...] += dsink

    prev = pl.BlockSpec((N_KV_HEADS, BLOCK, LANES), lambda n: (0, jnp.maximum(rev(n) - 1, 0), 0))
    own = pl.BlockSpec((N_KV_HEADS, BLOCK, LANES), lambda n: (0, rev(n), 0))
    rows = lambda cols: pl.BlockSpec((BLOCK, cols), lambda n: (rev(n), 0))
    heads = jax.ShapeDtypeStruct((N_KV_HEADS, t, LANES), F32)
    return pl.pallas_call(
        body, name=name, grid=(nb,),
        out_shape=(jax.ShapeDtypeStruct((t, d_model), F32), heads, heads, jax.ShapeDtypeStruct((1, LANES), F32)),
        in_specs=[pl.BlockSpec(memory_space=pltpu.SMEM), rows(d_model), prev, own, prev, own, rows(LANES), rows(d_model)],
        out_specs=(rows(d_model), own, own, _const((1, LANES))),
        scratch_shapes=[pltpu.VMEM((N_KV_HEADS, BLOCK, LANES), F32), pltpu.VMEM((N_KV_HEADS, BLOCK, LANES), F32)],
        compiler_params=_params(),
    )(sinks, qs, kdup, kdup, vdup, vdup, lse, do)


def _oproj_fwd(x, o, w_o, post_g, layer, j, name):
    t, d_model = x.shape
    tm = _row_tile(t, PROJ_TILE)

    def body(x_ref, o_ref, w_ref, g_ref, xo_ref, mo_ref):
        mo = _mm(o_ref[...], w_ref[...])
        mo_ref[...] = mo
        xo_ref[...] = x_ref[...] + _rms_fwd(mo, g_ref[...])

    full = jax.ShapeDtypeStruct((t, d_model), F32)
    return pl.pallas_call(
        body, name=name, grid=(t // tm,), out_shape=(full, full),
        in_specs=[_rows(tm, d_model), _rows(tm, d_model), _const((None, d_model, d_model), (j, 0, 0)),
                  _const((None, 1, d_model), (layer, 0, 0))],
        out_specs=(_rows(tm, d_model), _rows(tm, d_model)), compiler_params=_params(),
    )(x, o, w_o, post_g)


def _oproj_bwd(dx, mo, w_o, post_g, layer, j, name):
    t, d_model = dx.shape
    tm = _row_tile(t, PROJ_TILE)

    def body(dx_ref, mo_ref, w_ref, g_ref, do_ref, dmo_ref, dg_ref):
        i = pl.program_id(0)

        @pl.when(i == 0)
        def _():
            dg_ref[...] = jnp.zeros_like(dg_ref)

        dmo, dg = _rms_bwd(mo_ref[...], g_ref[...], dx_ref[...])
        dg_ref[...] += dg
        dmo = dmo.astype(_MXU_DTYPE)
        dmo_ref[...] = dmo
        do_ref[...] = _mm_tb(dmo, w_ref[...]).astype(do_ref.dtype)

    act = jax.ShapeDtypeStruct((t, d_model), _ACT_DTYPE)
    return pl.pallas_call(
        body, name=name, grid=(t // tm,), out_shape=(act, act, jax.ShapeDtypeStruct((1, d_model), F32)),
        in_specs=[_rows(tm, d_model), _rows(tm, d_model), _const((None, d_model, d_model), (j, 0, 0)),
                  _const((None, 1, d_model), (layer, 0, 0))],
        out_specs=(_rows(tm, d_model), _rows(tm, d_model), _const((1, d_model))), compiler_params=_params(),
    )(dx, mo, w_o, post_g)


def _loss_grad(y, target, name):
    t, d_model = y.shape
    tm = _row_tile(t, PROJ_TILE)

    def body(y_ref, t_ref, dy_ref, loss_ref):
        i = pl.program_id(0)

        @pl.when(i == 0)
        def _():
            loss_ref[...] = jnp.zeros_like(loss_ref)

        err = y_ref[...] - t_ref[...]
        dy_ref[...] = err / d_model
        loss_ref[...] += 0.5 * jnp.sum(jnp.mean(err * err, axis=-1, keepdims=True), axis=0, keepdims=True)

    return pl.pallas_call(
        body, name=name, grid=(t // tm,),
        out_shape=(jax.ShapeDtypeStruct((t, d_model), F32), jax.ShapeDtypeStruct((1, 1), F32)),
        in_specs=[_rows(tm, d_model), _rows(tm, d_model)], out_specs=(_rows(tm, d_model), _const((1, 1))),
        compiler_params=_params(),
    )(y, target)


def _mesh_position():
    return lax.axis_index("x"), lax.axis_index("y"), lax.axis_index("c")


def _block_of(px, py, pc):
    return 4 * px + 2 * py + pc


def _at_block(ref, axis, block):
    return ref.at[(slice(None),) * axis + (block,)]


def _all_gather(shards, axes, name):
    n = len(shards)

    def body(*refs):
        srcs, outs = refs[:n], refs[n:2 * n]
        send_sems, recv_sems, local_sems = refs[2 * n:]
        x, y, c = _mesh_position()
        me, sibling = (x, y, c), (x, y, 1 - c)
        chips = [(1 - x, y), (x, 1 - y), (1 - x, 1 - y)]

        def blk(i, pos):
            return _at_block(outs[i], axes[i], _block_of(*pos))

        def copy(i, k, block, to, src=None):
            return pltpu.make_async_remote_copy(
                src_ref=blk(i, block) if src is None else src, dst_ref=blk(i, block),
                send_sem=send_sems.at[i, k], recv_sem=recv_sems.at[i, k], device_id=to, device_id_type=MESH)

        mine = [pltpu.make_async_copy(srcs[i], blk(i, me), local_sems.at[i]) for i in range(n)]
        for cp in mine:
            cp.start()
        sent = []
        for i in range(n):
            sent += [copy(i, 1 + k, me, (*chip, c), src=srcs[i]) for k, chip in enumerate(chips)]
            sent.append(copy(i, 0, me, sibling, src=srcs[i]))
        for cp in sent:
            cp.start()
        for i in range(n):
            for k, chip in enumerate(chips):
                copy(i, 1 + k, (*chip, c), me).wait_recv()
                passed = copy(i, 4 + k, (*chip, c), sibling)
                passed.start()
                sent.append(passed)
        for i in range(n):
            copy(i, 0, sibling, me).wait_recv()
            for k, chip in enumerate(chips):
                copy(i, 4 + k, (*chip, 1 - c), me).wait_recv()
        for cp in sent:
            cp.wait_send()
        for cp in mine:
            cp.wait()

    hbm = pl.BlockSpec(memory_space=pl.ANY)
    return pl.pallas_call(
        body, name=name,
        out_shape=tuple(jax.ShapeDtypeStruct(s.shape[:a] + (N_DEV,) + s.shape[a:], s.dtype) for s, a in zip(shards, axes)),
        in_specs=[hbm] * n, out_specs=(hbm,) * n,
        scratch_shapes=[pltpu.SemaphoreType.DMA((n, 7)), pltpu.SemaphoreType.DMA((n, 7)), pltpu.SemaphoreType.DMA((n,))],
    )(*shards)


GATHER, SCATTER, GATHER_CHIPS, GATHER_SIBLING = "gather", "scatter", "gather_chips", "gather_sibling"
COPIES = {GATHER: N_DEV - 1, SCATTER: N_DEV - 1, GATHER_CHIPS: 4, GATHER_SIBLING: 3}


def _land_shape(kind, s, axis):
    if kind == SCATTER:
        return (N_DEV,) + s.shape[:axis] + s.shape[axis + 1:]
    return s.shape[:axis] + (N_DEV,) + s.shape[axis:]


def _plan(kind, srcs, lands, axes):
    x, y, c = _mesh_position()
    my_block = _block_of(x, y, c)
    flips = {GATHER_CHIPS: (1, 4, 2, 6), GATHER_SIBLING: (4, 2, 6)}.get(kind, range(1, N_DEV))
    others = [(1 - x if k & 4 else x, 1 - y if k & 2 else y, 1 - c if k & 1 else c) for k in flips]
    remote = []
    for src, land, axis in zip(srcs, lands, axes):
        if kind == SCATTER:
            mine = land.at[my_block]
            remote += [(_at_block(src, axis, _block_of(*peer)), mine, peer, land.at[_block_of(*peer)]) for peer in others]
        elif kind == GATHER_SIBLING:
            for px, py, _ in others:
                mine, theirs = _at_block(land, axis, _block_of(px, py, c)), _at_block(land, axis, _block_of(px, py, 1 - c))
                remote.append((mine, mine, (x, y, 1 - c), theirs))
        else:
            mine = _at_block(land, axis, my_block)
            remote += [(src, mine, peer, _at_block(land, axis, _block_of(*peer))) for peer in others]
    return remote


def _remote(src, dst, send_sems, recv_sems, k, peer):
    return pltpu.make_async_remote_copy(src_ref=src, dst_ref=dst, send_sem=send_sems.at[k], recv_sem=recv_sems.at[k],
                                        device_id=peer, device_id_type=MESH)


_HBM = pl.BlockSpec(memory_space=pltpu.HBM)
_SEM = pl.BlockSpec(memory_space=pltpu.SEMAPHORE)
_SPLIT = dict(has_side_effects=pltpu.SideEffectType.DATAFLOW_SIDE_EFFECTING)


def _landing_zone(kind, s, axis, me):
    land = lax.empty(_land_shape(kind, s, axis), s.dtype)
    if kind == SCATTER:
        return lax.dynamic_update_slice_in_dim(land, lax.dynamic_slice_in_dim(s, me, 1, axis).reshape((1,) + land.shape[1:]), me, 0)
    return lax.dynamic_update_slice_in_dim(land, jnp.expand_dims(s, axis), me, axis)


def _exchange_start(kind, arrays, axes, after, name):
    n = len(arrays)
    if kind == GATHER_SIBLING:
        passed = list(arrays)
    else:
        me = _block_of(*_mesh_position())
        passed = list(arrays) + [_landing_zone(kind, s, a, me) for s, a in zip(arrays, axes)]
    n_sems = n * COPIES[kind]

    def body(*refs):
        land_refs = refs[len(passed) - n:len(passed)]
        send_sems, recv_sems = refs[len(passed) + 1], refs[len(passed) + 2]
        token = refs[-1]
        for k, (src, dst, peer, _) in enumerate(_plan(kind, refs[:n], land_refs, axes)):
            _remote(src, dst, send_sems, recv_sems, k, peer).start()
        token[...] = jnp.zeros_like(token)

    out = pl.pallas_call(
        body, name=name,
        out_shape=(pltpu.SemaphoreType.DMA((n_sems,)), pltpu.SemaphoreType.DMA((n_sems,)),
                   *[pltpu.HBM(a.shape, a.dtype) for a in passed], jax.ShapeDtypeStruct((8, LANES), F32)),
        in_specs=[_HBM] * len(passed) + [pl.BlockSpec(memory_space=pl.ANY)],
        out_specs=(_SEM, _SEM, *[_HBM] * len(passed), pl.BlockSpec(memory_space=pltpu.VMEM)),
        input_output_aliases={i: 2 + i for i in range(len(passed))},
        compiler_params=pltpu.CompilerParams(**_SPLIT),
    )(*[pltpu.with_memory_space_constraint(a, pltpu.HBM) for a in passed], after)
    return (kind, axes, n, out[:-1]), out[-1]


def _exchange_wait(handle, after, name):
    kind, axes, n, (send_sems, recv_sems, *thru) = handle

    def body(*refs):
        land_refs = refs[len(thru) - n:len(thru)]
        send_sems, recv_sems = refs[len(thru)], refs[len(thru) + 1]
        for k, (src, _, peer, arrives) in enumerate(_plan(kind, refs[:n], land_refs, axes)):
            cp = _remote(src, arrives, send_sems, recv_sems, k, peer)
            cp.wait_send()
            cp.wait_recv()

    out = pl.pallas_call(
        body, name=name,
        out_shape=tuple(pltpu.HBM(a.shape, a.dtype) for a in thru),
        in_specs=[_HBM] * len(thru) + [_SEM, _SEM, pl.BlockSpec(memory_space=pl.ANY)], out_specs=(_HBM,) * len(thru),
        input_output_aliases={i: i for i in range(len(thru))},
        compiler_params=pltpu.CompilerParams(**_SPLIT),
    )(*thru, send_sems, recv_sems, after)
    return out[len(thru) - n:]


def _adamw_math(w, g, m, v):
    m = ADAM_B1 * m + (1.0 - ADAM_B1) * g
    v = ADAM_B2 * v + (1.0 - ADAM_B2) * jnp.square(g)
    m_hat = m / (1.0 - ADAM_B1 ** ADAM_STEP)
    v_hat = v / (1.0 - ADAM_B2 ** ADAM_STEP)
    delta = -ADAM_LR * (m_hat / (jnp.sqrt(v_hat) + ADAM_EPS) + ADAM_WD * w)
    return delta, m, v


def _update_tile(rows):
    if rows <= 512:
        return rows
    for tr in (512, 384, 352, 256, 176, 128, 64, 32, 16):
        if rows % tr == 0:
            return tr
    raise ValueError(f"{rows} rows do not tile")


def _adamw(parts, w, m, v, slab, so_far, name):
    rows, c = w.shape
    r = parts.shape[1]
    tr = _update_tile(r)
    first = slab * (r // tr)
    if so_far is None:
        so_far = tuple(lax.empty((rows, c), F32) for _ in range(4))

    def body(p_ref, w_ref, m_ref, v_ref, *refs):
        g_ref, d_ref, mo_ref, vo_ref = refs[4:]
        g = p_ref[0].astype(F32)
        for s in range(1, N_DEV):
            g = g + p_ref[s].astype(F32)
        g_ref[...] = g
        d_ref[...], mo_ref[...], vo_ref[...] = _adamw_math(w_ref[...], g, m_ref[...], v_ref[...])

    out = jax.ShapeDtypeStruct((rows, c), F32)
    tile = pl.BlockSpec((tr, c), lambda i: (first + i, 0))
    return pl.pallas_call(
        body, name=name, grid=(r // tr,), out_shape=(out,) * 4,
        in_specs=[pl.BlockSpec((N_DEV, tr, c), lambda i: (0, i, 0))] + [tile] * 3 + [pl.BlockSpec(memory_space=pl.ANY)] * 4,
        out_specs=(tile,) * 4, input_output_aliases={4 + k: k for k in range(4)}, compiler_params=_params(),
    )(parts, w, m, v, *so_far)


def _adamw_small(parts, picks, weights, name):
    n = len(parts)

    def body(*refs):
        p_refs, wmv, outs = refs[:n], refs[n:4 * n], refs[4 * n:]
        me = _block_of(*_mesh_position())
        for i in range(n):
            g = picks[i](p_refs[i], 0, me)
            for s in range(1, N_DEV):
                g = g + picks[i](p_refs[i], s, me)
            w_ref, m_ref, v_ref = wmv[3 * i:3 * i + 3]
            g_ref, d_ref, mo_ref, vo_ref = outs[4 * i:4 * i + 4]
            g_ref[...] = g
            d_ref[...], mo_ref[...], vo_ref[...] = _adamw_math(w_ref[...], g, m_ref[...], v_ref[...])

    flat = [a for wmv in weights for a in wmv]
    out = pl.pallas_call(
        body, name=name,
        out_shape=tuple(jax.ShapeDtypeStruct(w.shape, F32) for w, _, _ in weights for _ in range(4)),
        compiler_params=pltpu.CompilerParams(vmem_limit_bytes=VMEM_LIMIT),
    )(*parts, *flat)
    return [tuple(out[4 * i:4 * i + 4]) for i in range(n)]


def kernel(x, positions, mix_pre_g, mix_post_g, pool_w, pool_scale, kv_norm_g, w_kv, w_q, w_o, sinks, ffn_pre_g, ffn_post_g, ffn_w_in, ffn_conv_w, ffn_conv_b, ffn_w_out, loss_target, m_mix_pre_g, m_mix_post_g, m_pool_w, m_pool_scale, m_kv_norm_g, m_w_kv, m_w_q, m_w_o, m_sinks, m_ffn_pre_g, m_ffn_post_g, m_ffn_w_in, m_ffn_conv_w, m_ffn_conv_b, m_ffn_w_out, v_mix_pre_g, v_mix_post_g, v_pool_w, v_pool_scale, v_kv_norm_g, v_w_kv, v_w_q, v_w_o, v_sinks, v_ffn_pre_g, v_ffn_post_g, v_ffn_w_in, v_ffn_conv_w, v_ffn_conv_b, v_ffn_w_out):
    depth, d_model = mix_pre_g.shape
    n_a = pool_w.shape[0]
    n_b = w_q.shape[0]
    t = x.shape[1]
    fs = ffn_w_in.shape[2]
    half = N_DEV // 2
    n_heads = d_model // HEAD_DIM
    x0 = x.reshape(t, d_model)
    target = loss_target.reshape(t, d_model)

    inv_freq = 1.0 / (ROPE_THETA ** (jnp.arange(0, HEAD_DIM, 2, dtype=F32) / HEAD_DIM))
    ang = positions.reshape(t).astype(F32)[:, None] * inv_freq
    cos, sin = jnp.cos(ang), jnp.sin(ang)
    cos = jnp.tile(cos, (1, 2 * LANES // HEAD_DIM))
    ssin = jnp.tile(jnp.concatenate([-sin, sin], axis=1), (1, LANES // HEAD_DIM))

    wire = lambda a: a.astype(_WIRE_DTYPE)
    w_in_b, w_out_b = wire(ffn_w_in), wire(ffn_w_out)
    pool_w_g, w_in_0, w_out_0, pool_scale_g, conv_w_g = _all_gather(
        [wire(pool_w), w_in_b[:1], w_out_b[:1], pool_scale, ffn_conv_w], [2, 1, 1, 0, 0], "gather_first")
    groups = []
    for l in range(1, depth):
        if l == n_a:
            groups.append(("attn", [wire(w_kv), wire(w_q), wire(w_o)], [0, 1, 1], l))
        groups.append((l, [w_in_b[l:l + 1], w_out_b[l:l + 1]], [1, 1], l))
    over_ici, to_sibling, tokens, after = {}, {}, [], w_in_0
    for key, shards, axes, _ in groups:
        over_ici[key], after = _exchange_start(GATHER_CHIPS, shards, axes, after, f"gather_chips_{key}")
        tokens.append(after)
    started = functools.reduce(lambda a, b: a + b, [tk[0, 0] for tk in tokens])

    def pass_on(layer, after):
        sent = jnp.zeros((), F32)
        for key, _, axes, first in groups:
            if first == layer:
                lands = _exchange_wait(over_ici[key], after, f"gather_chips_wait_{key}")
                to_sibling[key], tk = _exchange_start(GATHER_SIBLING, lands, axes, after, f"gather_sibling_{key}")
                sent = sent + tk[0, 0]
        return sent

    w_in_l, w_out_l = {0: w_in_0}, {0: w_out_0.reshape(1, half * fs, d_model)}
    pool_scale_f = pool_scale_g.transpose(1, 0, 2).reshape(n_a, 1, d_model)
    conv_w_f = conv_w_g.transpose(1, 2, 0, 3)
    pool_w_f = pool_w_g.reshape(n_a, len(POOL_WINDOWS), d_model // len(POOL_WINDOWS), -1)
    conv_b_f = ffn_conv_b.reshape(depth, N_DEV, fs)
    g3 = lambda a: a.reshape(a.shape[0], 1, a.shape[1])
    mix_pre, mix_post, ffn_pre, ffn_post = g3(mix_pre_g) + started, g3(mix_post_g), g3(ffn_pre_g), g3(ffn_post_g)
    kv_g = kv_norm_g.reshape(1, d_model)
    w_kv_f = w_q_f = w_o_f = None

    saved = []
    xc = x0
    kdup = vdup = x_kv = None
    for l in range(depth):
        x_in = xc
        if l < n_a:
            x_mid, dsave, yu = _pool_fwd(x_in, mix_pre, mix_post, pool_w_f, pool_scale_f, l, f"pool_fwd_{l}")
            mixer = (dsave, yu)
        else:
            j = l - n_a
            if j == 0:
                x_kv = x_in
                w_kv_g, w_q_g, w_o_g = _exchange_wait(to_sibling["attn"], x_in, "gather_sibling_wait_attn")
                w_kv_f = w_kv_g.reshape(d_model, -1)
                w_q_f = w_q_g.reshape(n_b, d_model, d_model)
                w_o_f = w_o_g.reshape(n_b, d_model, d_model)
                kdup, vdup = _kv_fwd(x_kv, kv_g, w_kv_f, cos, ssin, "kv_fwd")
            qs = _q_fwd(x_in, mix_pre, w_q_f, cos, ssin, l, j, f"q_fwd_{l}")
            o, lse = _attn_fwd(qs, kdup, vdup, sinks, j, f"attn_fwd_{l}")
            x_mid, mo = _oproj_fwd(x_in, o, w_o_f, mix_post, l, j, f"oproj_fwd_{l}")
            mixer = (qs, o, lse, mo)
        if l > 0:
            w_in_l[l], w_out_g = _exchange_wait(to_sibling[l], x_mid, f"gather_sibling_wait_{l}")
            w_out_l[l] = w_out_g.reshape(1, half * fs, d_model)
        xc, u, uc, f = _ffn_fwd(x_mid, ffn_pre, ffn_post, w_in_l[l], conv_w_f, conv_b_f, w_out_l[l], l, 0, f"ffn_fwd_{l}")
        saved.append((x_in, x_mid, u, uc, f, mixer))
        if l + 1 < depth:
            mix_pre = mix_pre + pass_on(l + 1, xc)

    dx, loss_part = _loss_grad(xc, target, "loss")

    gconv_w, gconv_b = [None] * depth, [None] * depth
    gmix_pre, gmix_post, gffn_pre, gffn_post = [None] * depth, [None] * depth, [None] * depth, [None] * depth
    gpool_scale, gsinks = [None] * n_a, [None] * n_b
    dks, dvs = [], []
    gkv_g = None
    whole = lambda cols: pl.BlockSpec((t, cols), lambda c: (0, 0), pipeline_mode=pl.Buffered(1))
    per_out = lambda cols: pl.BlockSpec((None, t, cols), lambda c: (c, 0, 0))
    by_dev = lambda g: g.reshape(N_DEV, -1, g.shape[-1])
    def small_grads():
        cat = lambda rows: jnp.concatenate(rows, axis=0)
        row = lambda a: a.reshape(1, -1)
        everything = lambda ref, s, me: ref[s]
        lanes = pool_scale.shape[1]
        return [("mix_pre_g", cat(gmix_pre), everything, (mix_pre_g, m_mix_pre_g, v_mix_pre_g)),
                ("mix_post_g", cat(gmix_post), everything, (mix_post_g, m_mix_post_g, v_mix_post_g)),
                ("kv_norm_g", gkv_g, everything, (row(kv_norm_g), row(m_kv_norm_g), row(v_kv_norm_g))),
                ("sinks", cat(gsinks), lambda ref, s, me: ref[s, :, pl.ds(0, n_heads)], (sinks, m_sinks, v_sinks)),
                ("ffn_pre_g", cat(gffn_pre), everything, (ffn_pre_g, m_ffn_pre_g, v_ffn_pre_g)),
                ("ffn_post_g", cat(gffn_post), everything, (ffn_post_g, m_ffn_post_g, v_ffn_post_g)),
                ("ffn_conv_b", jnp.stack(gconv_b).reshape(depth, N_DEV * fs), everything, (ffn_conv_b, m_ffn_conv_b, v_ffn_conv_b)),
                ("pool_scale", cat(gpool_scale), lambda ref, s, me: ref[s, :, pl.ds(pl.multiple_of(me * lanes, lanes), lanes)],
                 (pool_scale, m_pool_scale, v_pool_scale)),
                ("ffn_conv_w", jnp.stack(gconv_w).transpose(0, 2, 1, 3), lambda ref, s, me: ref[s, :, me],
                 (ffn_conv_w, m_ffn_conv_w, v_ffn_conv_w))]

    flying = []

    def launch(going, after, name):
        handle, token = _exchange_start(SCATTER, [g for _, _, g, _ in going], [a for _, _, _, a in going], after, name)
        flying.append(([(nm, slab) for nm, slab, _, _ in going], handle))
        return token

    post = mix_post
    ffn_post_b = ffn_post
    token = None
    for l in reversed(range(depth)):
        x_in, x_mid, u, uc, f, mixer = saved[l]
        duc, gact, df, gconv_b[l], gffn_post[l] = _ffn_bwd_out(dx, f, ffn_post_b, uc, conv_b_f, w_out_l[l], l, 0, f"ffn_bwd_out_{l}")
        dx, du, hf, gconv_w[l], gffn_pre[l] = _ffn_bwd_in(dx, x_mid, ffn_pre, duc, u, conv_w_f, w_in_l[l], l, 0, f"ffn_bwd_in_{l}")
        gin = _tn_matmul(hf, du, whole(d_model), per_out(fs), N_DEV, d_model, fs, f"grad_w_in_{l}", _WIRE_DTYPE)
        going = [("ffn_w_in", l, gin, 0)]
        if l == 0:
            token = launch(going, dx, "scatter_start_0_in")
            going = []
        gout = _tn_matmul(gact, df, per_out(fs), whole(d_model), half, fs, d_model, f"grad_w_out_{l}", _WIRE_DTYPE)
        going.append(("ffn_w_out", l, by_dev(gout), 0))
        if l == 0:
            token = launch(going, token, "scatter_start_0_out")
            going = []
            post = post + token[0, 0]
        if l < n_a:
            dsave, yu = mixer
            dx, dyu, gpool_scale[l], gmix_pre[l], gmix_post[l] = _pool_bwd(
                dx, x_in, mix_pre, post, dsave, yu, pool_w_f, pool_scale_f, l, f"pool_bwd_{l}")
            gc = d_model // len(POOL_WINDOWS)
            by_group = pl.BlockSpec((t, gc), lambda c: (0, c))
            gpool = _tn_matmul(dsave, dyu, by_group, by_group, len(POOL_WINDOWS), gc, gc, f"grad_pool_w_{l}", _WIRE_DTYPE)
            going.append(("pool_w", l, gpool.reshape(len(POOL_WINDOWS), N_DEV, -1, gc), 1))
        else:
            j = l - n_a
            qs, o, lse, mo = mixer
            do, dmo, gmix_post[l] = _oproj_bwd(dx, mo, w_o_f, post, l, j, f"oproj_bwd_{l}")
            dqs, dk, dv, gsinks[j] = _attn_bwd(qs, kdup, vdup, sinks, lse, do, j, f"attn_bwd_{l}")
            dks.append(dk)
            dvs.append(dv)
            dx, hq, dq, gmix_pre[l] = _q_bwd(dx, dqs, x_in, mix_pre, w_q_f, cos, ssin, l, j, f"q_bwd_{l}")
            if j == 0:
                dx, hkv, dkv, gkv_g = _kv_bwd(dx, x_kv, kv_g, w_kv_f, cos, ssin, dks, dvs, "kv_bwd")
                gkv = _tn_matmul(hkv, dkv, whole(d_model), whole(dkv.shape[1]), 1, d_model, dkv.shape[1], "grad_w_kv", _WIRE_DTYPE)
                going.append(("w_kv", 0, by_dev(gkv), 0))
            go = _tn_matmul(o, dmo, whole(d_model), whole(d_model), 1, d_model, d_model, f"grad_w_o_{l}", _WIRE_DTYPE)
            gq = _tn_matmul(hq, dq, whole(d_model), whole(d_model), 1, d_model, d_model, f"grad_w_q_{l}", _WIRE_DTYPE)
            going += [("w_o", j, by_dev(go), 0), ("w_q", j, by_dev(gq), 0)]
        after = dx
        if l == 0:
            small = small_grads()
            leaving = [g for _, g, _, _ in small] + [jnp.broadcast_to(loss_part, (1, LANES))]
            small_flight, after = _exchange_start(GATHER, leaving, [0] * len(leaving), dx, "gather_small_grads")
        token = launch(going, after, f"scatter_start_{l}")
        ffn_post_b = ffn_post_b + token[0, 0]

    grad_x = dx.reshape(x.shape)

    shard = {"pool_w": (pool_w, m_pool_w, v_pool_w), "w_kv": (w_kv, m_w_kv, v_w_kv), "w_q": (w_q, m_w_q, v_w_q),
             "w_o": (w_o, m_w_o, v_w_o), "ffn_w_in": (ffn_w_in, m_ffn_w_in, v_ffn_w_in),
             "ffn_w_out": (ffn_w_out, m_ffn_w_out, v_ffn_w_out)}
    big = {}

    def arrive(flights, after):
        for idx, (names, handle) in flights:
            parts = _exchange_wait(handle, after, f"scatter_wait_{idx}")
            for (nm, slab), p in zip(names, parts):
                cols = p.shape[-1]
                w2, m2, v2 = (a.reshape(-1, cols) for a in shard[nm])
                big[nm] = _adamw(p.reshape(N_DEV, -1, cols), w2, m2, v2, slab, big.get(nm), f"adamw_{nm}_{slab}")
                after = big[nm][0]
        return after

    done = arrive(list(enumerate(flying)), token)
    *small_parts, loss_parts = _exchange_wait(small_flight, done, "gather_small_grads_wait")
    loss = jnp.sum(loss_parts[:, 0, 0])
    upd = _adamw_small(small_parts, [pick for _, _, pick, _ in small], [wmv for _, _, _, wmv in small], "adamw_small")
    res = {nm: tuple(r.reshape(shard[nm][0].shape) for r in out) for nm, out in big.items()}
    for (nm, _, _, _), out in zip(small, upd):
        res[nm] = tuple(a.reshape(kv_norm_g.shape) for a in out) if nm == "kv_norm_g" else out

    order = ["mix_pre_g", "mix_post_g", "pool_w", "pool_scale", "kv_norm_g", "w_kv", "w_q", "w_o", "sinks", "ffn_pre_g",
             "ffn_post_g", "ffn_w_in", "ffn_conv_w", "ffn_conv_b", "ffn_w_out"]
    return (loss, grad_x, *[res[nm][0] for nm in order], *[res[nm][1] for nm in order],
            *[res[nm][2] for nm in order], *[res[nm][3] for nm in order])
```

```python
import functools
import math

import jax
import jax.numpy as jnp
from jax import lax
from jax.experimental import pallas as pl
from jax.experimental.pallas import tpu as pltpu

F32 = jnp.float32
_MXU_DTYPE = jnp.bfloat16
_ACT_DTYPE = jnp.bfloat16
_WIRE_DTYPE = jnp.bfloat16
_SAVE_DTYPE = jnp.bfloat16

N_DEV = 8
POOL_WINDOWS = (2, 4, 8, 16)
POOL_HALO = 32
assert POOL_WINDOWS == tuple(2 ** (g + 1) for g in range(len(POOL_WINDOWS))) and 8 * len(POOL_WINDOWS) <= POOL_HALO
HEAD_DIM = 64
N_KV_HEADS = 4
WINDOW = 128
BLOCK = 128
LANES = 128
ROPE_THETA = 10000.0
ATTN_SCALE = 1.0 / math.sqrt(HEAD_DIM)
NEG_INF = -1e30
RMS_EPS = 1e-6
CONV_HALO = 8
SAVE_HALO = 16
PROJ_TILE = 512
ADAM_LR = 0.001
ADAM_B1 = 0.9
ADAM_B2 = 0.999
ADAM_EPS = 1e-08
ADAM_WD = 0.01
ADAM_STEP = 10
VMEM_LIMIT = 56 * 1024 * 1024
MESH = pl.DeviceIdType.MESH


def _params(n_axes=1, vmem=VMEM_LIMIT):
    return pltpu.CompilerParams(dimension_semantics=("arbitrary",) * n_axes, vmem_limit_bytes=vmem)


def _resident(shape, index):
    return pl.BlockSpec(shape, lambda *_: index, pipeline_mode=pl.Buffered(1))


def _const(shape, index=None):
    index = (0,) * len(shape) if index is None else index
    return pl.BlockSpec(shape, lambda *_: index)


def _rows(tm, cols):
    return pl.BlockSpec((tm, cols), lambda i: (i, 0))


def _row_tile(t, most=256):
    for tm in (512, 256, 128, 64, 32, 16, 8):
        if tm <= most and t % tm == 0:
            return tm
    raise ValueError(f"sequence length {t} is not a multiple of 8")


def _mm(a, b):
    return jnp.dot(a.astype(_MXU_DTYPE), b.astype(_MXU_DTYPE), preferred_element_type=F32)


def _mm_tb(a, b):
    return lax.dot_general(a.astype(_MXU_DTYPE), b.astype(_MXU_DTYPE), (((1,), (1,)), ((), ())),
                           preferred_element_type=F32)


def _mm_ta(a, b):
    return lax.dot_general(a.astype(_MXU_DTYPE), b.astype(_MXU_DTYPE), (((0,), (0,)), ((), ())),
                           preferred_element_type=F32)


def _rms_r(x):
    return lax.rsqrt(jnp.mean(x * x, axis=-1, keepdims=True) + RMS_EPS)


def _rms_fwd(x, g):
    return (x * _rms_r(x)) * g


def _rms_bwd(x, g, dy):
    r = _rms_r(x)
    xh = x * r
    dg = jnp.sum(dy * xh, axis=0, keepdims=True)
    dxh = dy * g
    dx = r * (dxh - xh * jnp.mean(dxh * xh, axis=-1, keepdims=True))
    return dx, dg


_GELU_C = math.sqrt(2.0 / math.pi)


def _gelu_parts(z):
    z2 = z * z
    e = jnp.exp(z * (-2.0 * _GELU_C - (2.0 * _GELU_C * 0.044715) * z2))
    cdf = pl.reciprocal(1.0 + e, approx=False)
    dz = cdf + (z * (cdf * (1.0 - cdf))) * (2.0 * _GELU_C + (6.0 * _GELU_C * 0.044715) * z2)
    return cdf, dz


def _lane_iota(shape):
    return lax.broadcasted_iota(jnp.int32, shape, len(shape) - 1)


def _rope_partner(xb):
    first = (_lane_iota(xb.shape) & 32) == 0
    return jnp.where(first, pltpu.roll(xb, LANES - 32, 1), pltpu.roll(xb, 32, 1))


def _rope_fwd(xb, cos, ssin):
    return xb * cos + _rope_partner(xb) * ssin


def _rope_bwd(dyb, cos, ssin):
    return dyb * cos - _rope_partner(dyb) * ssin


def _low_half(shape):
    return (_lane_iota(shape) & 64) == 0


def _pool_fwd(x, pre_g, post_g, w, scale, layer, name):
    t, d_model = x.shape
    tm = _row_tile(t)
    n_groups, gc = w.shape[1], w.shape[2]

    def body(x_ref, pre_ref, post_ref, w_ref, sc_ref, xo_ref, d_ref, yu_ref, hbuf, sbuf):
        i = pl.program_id(0)

        @pl.when(i == 0)
        def _():
            hbuf[pl.ds(0, POOL_HALO), :] = jnp.zeros((POOL_HALO, d_model), F32)

        xv = x_ref[...]
        hbuf[pl.ds(POOL_HALO, tm), :] = _rms_fwd(xv, pre_ref[...])
        tok = i * tm + lax.broadcasted_iota(jnp.int32, (tm, 1), 0)
        yus = []
        for gi, wnd in enumerate(POOL_WINDOWS):
            first, reach = CONV_HALO * (gi + 1), wnd // 2
            n = POOL_HALO + tm - first
            cols = pl.ds(gi * gc, d_model - gi * gc)
            src = hbuf if gi == 0 else sbuf.at[gi - 1]
            level = src[pl.ds(first, n), cols] + src[pl.ds(first - reach, n), cols]
            if gi + 1 < len(POOL_WINDOWS):
                sbuf[gi, pl.ds(first, n), cols] = level
            h = hbuf[pl.ds(POOL_HALO, tm), pl.ds(gi * gc, gc)]
            cnt = jnp.minimum(tok + 1, wnd).astype(F32)
            dg = level[POOL_HALO - first:, :gc] / cnt - h
            d_ref[:, pl.ds(gi * gc, gc)] = dg.astype(d_ref.dtype)
            yus.append(_mm(dg, w_ref[gi]))
        hbuf[pl.ds(0, POOL_HALO), :] = hbuf[pl.ds(tm, POOL_HALO), :]
        yu = jnp.concatenate(yus, axis=1)
        yu_ref[...] = yu
        xo_ref[...] = xv + _rms_fwd(yu * sc_ref[...], post_ref[...])

    return pl.pallas_call(
        body, name=name, grid=(t // tm,),
        out_shape=(jax.ShapeDtypeStruct((t, d_model), F32), jax.ShapeDtypeStruct((t, d_model), _ACT_DTYPE),
                   jax.ShapeDtypeStruct((t, d_model), F32)),
        in_specs=[_rows(tm, d_model), _const((None, 1, d_model), (layer, 0, 0)), _const((None, 1, d_model), (layer, 0, 0)),
                  _const((None, n_groups, gc, gc), (layer, 0, 0, 0)), _const((None, 1, d_model), (layer, 0, 0))],
        out_specs=(_rows(tm, d_model), _rows(tm, d_model), _rows(tm, d_model)),
        scratch_shapes=[pltpu.VMEM((POOL_HALO + tm, d_model), F32),
                        pltpu.VMEM((len(POOL_WINDOWS) - 1, POOL_HALO + tm, d_model), F32)],
        compiler_params=_params(),
    )(x, pre_g, post_g, w, scale)


def _pool_bwd(dx, x, pre_g, post_g, d, yu, w, scale, layer, name):
    t, d_model = x.shape
    tm = _row_tile(t)
    nt = t // tm
    n_groups, gc = w.shape[1], w.shape[2]
    rev = lambda i: (nt - 1 - i, 0)
    rows = pl.BlockSpec((tm, d_model), rev)

    def body(dx_ref, x_ref, pre_ref, post_ref, d_ref, yu_ref, w_ref, sc_ref,
             dxi_ref, dyu_ref, dsc_ref, dpre_ref, dpost_ref, zbuf, sbuf):
        i = pl.program_id(0)

        @pl.when(i == 0)
        def _():
            zbuf[pl.ds(tm, POOL_HALO), :] = jnp.zeros((POOL_HALO, d_model), F32)
            dsc_ref[...] = jnp.zeros_like(dsc_ref)
            dpre_ref[...] = jnp.zeros_like(dpre_ref)
            dpost_ref[...] = jnp.zeros_like(dpost_ref)

        dxo = dx_ref[...]
        yuv = yu_ref[...]
        sc = sc_ref[...]
        dm, dpost = _rms_bwd(yuv * sc, post_ref[...], dxo)
        dpost_ref[...] += dpost
        dsc_ref[...] += jnp.sum(dm * yuv, axis=0, keepdims=True)
        dyu = dm * sc
        dyu_ref[...] = dyu.astype(dyu_ref.dtype)
        tok = (nt - 1 - i) * tm + lax.broadcasted_iota(jnp.int32, (tm, 1), 0)
        dds = []
        for gi, wnd in enumerate(POOL_WINDOWS):
            cols = pl.ds(gi * gc, gc)
            dd = _mm_tb(dyu[:, gi * gc:(gi + 1) * gc], w_ref[gi])
            cnt = jnp.minimum(tok + 1, wnd).astype(F32)
            zbuf[pl.ds(0, tm), cols] = dd / cnt
            dds.append(dd)
        dhs = []
        for gi, wnd in enumerate(POOL_WINDOWS):
            reach = wnd // 2
            n = tm + POOL_HALO - CONV_HALO * (gi + 1)
            cols = pl.ds(gi * gc, d_model - gi * gc)
            src = zbuf if gi == 0 else sbuf.at[gi - 1]
            level = src[pl.ds(0, n), cols] + src[pl.ds(reach, n), cols]
            if gi + 1 < len(POOL_WINDOWS):
                sbuf[gi, pl.ds(0, n), cols] = level
            dhs.append(level[:tm, :gc] - dds[gi])
        zbuf[pl.ds(tm, POOL_HALO), :] = zbuf[pl.ds(0, POOL_HALO), :]
        dh = jnp.concatenate(dhs, axis=1)
        dxp, dpre = _rms_bwd(x_ref[...], pre_ref[...], dh)
        dpre_ref[...] += dpre
        dxi_ref[...] = dxo + dxp

    vec = jax.ShapeDtypeStruct((1, d_model), F32)
    return pl.pallas_call(
        body, name=name, grid=(nt,),
        out_shape=(jax.ShapeDtypeStruct((t, d_model), F32), jax.ShapeDtypeStruct((t, d_model), _ACT_DTYPE), vec, vec, vec),
        in_specs=[rows, rows, _const((None, 1, d_model), (layer, 0, 0)), _const((None, 1, d_model), (layer, 0, 0)), rows, rows,
                  _const((None, n_groups, gc, gc), (layer, 0, 0, 0)), _const((None, 1, d_model), (layer, 0, 0))],
        out_specs=(rows, rows, _const((1, d_model)), _const((1, d_model)), _const((1, d_model))),
        scratch_shapes=[pltpu.VMEM((tm + POOL_HALO, d_model), F32),
                        pltpu.VMEM((len(POOL_WINDOWS) - 1, tm + POOL_HALO, d_model), F32)],
        compiler_params=_params(),
    )(dx, x, pre_g, post_g, d, yu, w, scale)


def _conv_taps(cw_ref, s):
    return [cw_ref[k, pl.ds(s, 1), :] for k in range(3)]


def _shift_down(v, k, before):
    rolled = pltpu.roll(v, k, 0)
    row = lax.broadcasted_iota(jnp.int32, before.shape, 0)
    head = jnp.where(row < k, pltpu.roll(before, k, 0), rolled[:CONV_HALO])
    return jnp.concatenate([head, rolled[CONV_HALO:]], axis=0)


def _shift_up(v, k, after):
    rows = v.shape[0]
    rolled = pltpu.roll(v, rows - k, 0)
    row = lax.broadcasted_iota(jnp.int32, after.shape, 0)
    tail = jnp.where(row >= CONV_HALO - k, pltpu.roll(after, CONV_HALO - k, 0), rolled[rows - CONV_HALO:])
    return jnp.concatenate([rolled[:rows - CONV_HALO], tail], axis=0)


def _ffn_fwd(x, pre_g, post_g, w_in, conv_w, conv_b, w_out, layer, w_layer, name):
    t, d_model = x.shape
    tm = _row_tile(t)
    fs = w_in.shape[2]
    half = N_DEV // 2

    def body(x_ref, pre_ref, post_ref, win_ref, cw_ref, cb_ref, wout_ref, xo_ref, u_ref, uc_ref, f_ref, carry):
        i = pl.program_id(0)

        @pl.when(i == 0)
        def _():
            carry[...] = jnp.zeros_like(carry)

        xv = x_ref[...]
        hf = _rms_fwd(xv, pre_ref[...]).astype(_MXU_DTYPE)
        f = jnp.zeros((tm, d_model), F32)
        project = lambda b: [_mm_tb(hf, win_ref[s]) for s in (b, b + half)]
        ahead = project(0)
        for b in range(half):
            us, ahead = ahead, project(b + 1) if b + 1 < half else None
            ucs = []
            for s, u in zip((b, b + half), us):
                u_ref[s] = u.astype(u_ref.dtype)
                before = carry[s]
                carry[s] = u[tm - CONV_HALO:]
                w0, w1, w2 = _conv_taps(cw_ref, s)
                uc = ((w0 * _shift_down(u, 2, before) + w1 * _shift_down(u, 1, before)) + w2 * u) + cb_ref[pl.ds(s, 1), :]
                uc_ref[s] = uc.astype(uc_ref.dtype)
                ucs.append(uc)
            gate, val = ucs
            cdf, _ = _gelu_parts(gate)
            f = f + _mm((gate * cdf) * val, wout_ref[pl.ds(b * fs, fs), :])
        f_ref[...] = f
        xo_ref[...] = xv + _rms_fwd(f, post_ref[...])

    tile3 = pl.BlockSpec((N_DEV, tm, fs), lambda i: (0, i, 0))
    saved = jax.ShapeDtypeStruct((N_DEV, t, fs), _SAVE_DTYPE)
    return pl.pallas_call(
        body, name=name, grid=(t // tm,),
        out_shape=(jax.ShapeDtypeStruct((t, d_model), F32), saved, saved, jax.ShapeDtypeStruct((t, d_model), F32)),
        in_specs=[_rows(tm, d_model), _const((None, 1, d_model), (layer, 0, 0)), _const((None, 1, d_model), (layer, 0, 0)),
                  _resident((None, N_DEV, fs, d_model), (w_layer, 0, 0, 0)), _const((None, 3, N_DEV, fs), (layer, 0, 0, 0)),
                  _const((None, N_DEV, fs), (layer, 0, 0)), _resident((None, half * fs, d_model), (w_layer, 0, 0))],
        out_specs=(_rows(tm, d_model), tile3, tile3, _rows(tm, d_model)),
        scratch_shapes=[pltpu.VMEM((N_DEV, CONV_HALO, fs), F32)],
        compiler_params=_params(),
    )(x, pre_g, post_g, w_in, conv_w, conv_b, w_out)


def _ffn_bwd_out(dx, f, post_g, uc, conv_b, w_out, layer, w_layer, name):
    t, d_model = dx.shape
    tm = _row_tile(t)
    nt = t // tm
    fs = uc.shape[2]
    half = N_DEV // 2

    def body(dx_ref, f_ref, post_ref, uc_ref, wout_ref, duc_ref, dwout_ref, dcb_ref, dpost_ref, acc):
        i = pl.program_id(0)

        @pl.when(i == 0)
        def _():
            acc[...] = jnp.zeros_like(acc)
            dcb_ref[...] = jnp.zeros_like(dcb_ref)
            dpost_ref[...] = jnp.zeros_like(dpost_ref)

        df, dpost = _rms_bwd(f_ref[...], post_ref[...], dx_ref[...])
        dpost_ref[...] += dpost
        dfm = df.astype(_MXU_DTYPE)
        project = lambda b: _mm_tb(dfm, wout_ref[pl.ds(b * fs, fs), :])
        ahead = project(0)
        for b in range(half):
            dg, ahead = ahead, project(b + 1) if b + 1 < half else None
            gate = uc_ref[b].astype(F32)
            val = uc_ref[b + half].astype(F32)
            cdf, dgelu = _gelu_parts(gate)
            ge = gate * cdf
            acc[pl.ds(b * fs, fs), :] += _mm_ta(ge * val, dfm)
            for s, dd in ((b, dg * val * dgelu), (b + half, dg * ge)):
                duc_ref[s] = dd.astype(duc_ref.dtype)
                dcb_ref[pl.ds(s, 1), :] += jnp.sum(dd, axis=0, keepdims=True)

        @pl.when(i == nt - 1)
        def _():
            dwout_ref[...] = acc[...].astype(dwout_ref.dtype)

    tile3 = pl.BlockSpec((N_DEV, tm, fs), lambda i: (0, i, 0))
    return pl.pallas_call(
        body, name=name, grid=(nt,),
        out_shape=(jax.ShapeDtypeStruct((N_DEV, t, fs), _SAVE_DTYPE), jax.ShapeDtypeStruct((half * fs, d_model), _WIRE_DTYPE),
                   jax.ShapeDtypeStruct((N_DEV, fs), F32), jax.ShapeDtypeStruct((1, d_model), F32)),
        in_specs=[_rows(tm, d_model), _rows(tm, d_model), _const((None, 1, d_model), (layer, 0, 0)), tile3,
                  _resident((None, half * fs, d_model), (w_layer, 0, 0))],
        out_specs=(tile3, _resident((half * fs, d_model), (0, 0)), _const((N_DEV, fs)), _const((1, d_model))),
        scratch_shapes=[pltpu.VMEM((half * fs, d_model), F32)],
        compiler_params=_params(),
    )(dx, f, post_g, uc, w_out)


def _ffn_bwd_in(dx, x, pre_g, duc, u, conv_w, w_in, layer, w_layer, name):
    t, d_model = dx.shape
    tm = _row_tile(t)
    nt = t // tm
    fs = duc.shape[2]
    hb = SAVE_HALO
    per_tile = tm // hb

    def body(dx_ref, x_ref, pre_ref, duc_ref, dn_ref, u_ref, cw_ref, win_ref, dxi_ref, du_ref, hf_ref, dcw_ref, dpre_ref):
        i = pl.program_id(0)

        @pl.when(i == 0)
        def _():
            dcw_ref[...] = jnp.zeros_like(dcw_ref)
            dpre_ref[...] = jnp.zeros_like(dpre_ref)

        xv = x_ref[...]
        pre = pre_ref[...]
        hf_ref[...] = _rms_fwd(xv, pre).astype(hf_ref.dtype)
        dhf = jnp.zeros((tm, d_model), F32)
        for s in range(N_DEV):
            d0 = duc_ref[s].astype(F32)
            after = jnp.where(i == nt - 1, 0.0, dn_ref[s].astype(F32)[:CONV_HALO])
            d1 = _shift_up(d0, 1, after)
            d2 = _shift_up(d0, 2, after)
            uv = u_ref[s].astype(F32)
            for k, dk in ((2, d0), (1, d1), (0, d2)):
                dcw_ref[k, pl.ds(s, 1), :] += jnp.sum(dk * uv, axis=0, keepdims=True)
            w0, w1, w2 = _conv_taps(cw_ref, s)
            du = (w2 * d0 + w1 * d1 + w0 * d2).astype(_MXU_DTYPE)
            du_ref[s] = du
            dhf = dhf + _mm(du, win_ref[s])
        dxp, dpre = _rms_bwd(xv, pre, dhf)
        dpre_ref[...] += dpre
        dxi_ref[...] = dx_ref[...] + dxp

    tile3 = pl.BlockSpec((N_DEV, tm, fs), lambda i: (0, i, 0))
    return pl.pallas_call(
        body, name=name, grid=(nt,),
        out_shape=(jax.ShapeDtypeStruct((t, d_model), F32), jax.ShapeDtypeStruct((N_DEV, t, fs), _ACT_DTYPE),
                   jax.ShapeDtypeStruct((t, d_model), _ACT_DTYPE), jax.ShapeDtypeStruct((3, N_DEV, fs), F32),
                   jax.ShapeDtypeStruct((1, d_model), F32)),
        in_specs=[_rows(tm, d_model), _rows(tm, d_model), _const((None, 1, d_model), (layer, 0, 0)), tile3,
                  pl.BlockSpec((N_DEV, hb, fs), lambda i: (0, jnp.minimum((i + 1) * per_tile, t // hb - 1), 0)), tile3,
                  _const((None, 3, N_DEV, fs), (layer, 0, 0, 0)), _resident((None, N_DEV, fs, d_model), (w_layer, 0, 0, 0))],
        out_specs=(_rows(tm, d_model), tile3, _rows(tm, d_model), _const((3, N_DEV, fs)), _const((1, d_model))),
        compiler_params=_params(),
    )(dx, x, pre_g, duc, duc, u, conv_w, w_in)


def _tn_matmul(a, b, a_spec, b_spec, n_out, m, n, name, out_dtype):
    def body(a_ref, b_ref, o_ref):
        o_ref[...] = _mm_ta(a_ref[...], b_ref[...]).astype(o_ref.dtype)

    return pl.pallas_call(
        body, name=name, grid=(n_out,),
        out_shape=jax.ShapeDtypeStruct((n_out, m, n), out_dtype),
        in_specs=[a_spec, b_spec],
        out_specs=pl.BlockSpec((None, m, n), lambda c: (c, 0, 0)),
        compiler_params=_params(),
    )(a, b)


def _kv_fwd(x, kv_g, w_kv, cos, ssin, name):
    t, d_model = x.shape
    tm = _row_tile(t, PROJ_TILE)
    kvd = w_kv.shape[1] // 2
    pairs = kvd // LANES

    def body(x_ref, g_ref, w_ref, cos_ref, sin_ref, k_ref, v_ref):
        kv = _mm(_rms_fwd(x_ref[...], g_ref[...]), w_ref[...])
        low = _low_half((tm, LANES))
        for j in range(pairs):
            kb = _rope_fwd(kv[:, j * LANES:(j + 1) * LANES], cos_ref[...], sin_ref[...])
            vb = kv[:, kvd + j * LANES:kvd + (j + 1) * LANES]
            for blk, ref in ((kb, k_ref), (vb, v_ref)):
                sw = pltpu.roll(blk, 64, 1)
                ref[2 * j] = jnp.where(low, blk, sw).astype(ref.dtype)
                ref[2 * j + 1] = jnp.where(low, sw, blk).astype(ref.dtype)

    heads = jax.ShapeDtypeStruct((N_KV_HEADS, t, LANES), _ACT_DTYPE)
    hspec = pl.BlockSpec((N_KV_HEADS, tm, LANES), lambda i: (0, i, 0))
    return pl.pallas_call(
        body, name=name, grid=(t // tm,), out_shape=(heads, heads),
        in_specs=[_rows(tm, d_model), _const((1, d_model)), _const(w_kv.shape), _rows(tm, LANES), _rows(tm, LANES)],
        out_specs=(hspec, hspec), compiler_params=_params(),
    )(x, kv_g, w_kv, cos, ssin)


def _kv_bwd(dx, x, kv_g, w_kv, cos, ssin, dks, dvs, name):
    t, d_model = x.shape
    tm = _row_tile(t, PROJ_TILE)
    kvd = w_kv.shape[1] // 2
    pairs = kvd // LANES
    n_users = len(dks)

    def body(dx_ref, x_ref, g_ref, w_ref, cos_ref, sin_ref, *refs):
        dk_refs, dv_refs = refs[:n_users], refs[n_users:2 * n_users]
        dxi_ref, h_ref, dkv_ref, dg_ref = refs[2 * n_users:]
        dk_ref = functools.reduce(lambda a, b: a + b, [r[...] for r in dk_refs])
        dv_ref = functools.reduce(lambda a, b: a + b, [r[...] for r in dv_refs])
        i = pl.program_id(0)

        @pl.when(i == 0)
        def _():
            dg_ref[...] = jnp.zeros_like(dg_ref)

        xv = x_ref[...]
        g = g_ref[...]
        h_ref[...] = _rms_fwd(xv, g).astype(h_ref.dtype)
        low = _low_half((tm, LANES))
        dks, dvs = [], []
        for j in range(pairs):
            dkb = jnp.where(low, dk_ref[2 * j], dk_ref[2 * j + 1])
            dks.append(_rope_bwd(dkb, cos_ref[...], sin_ref[...]))
            dvs.append(jnp.where(low, dv_ref[2 * j], dv_ref[2 * j + 1]))
        dkv = jnp.concatenate(dks + dvs, axis=1).astype(_MXU_DTYPE)
        dkv_ref[...] = dkv
        dxp, dg = _rms_bwd(xv, g, _mm_tb(dkv, w_ref[...]))
        dg_ref[...] += dg
        dxi_ref[...] = dx_ref[...] + dxp

    hspec = pl.BlockSpec((N_KV_HEADS, tm, LANES), lambda i: (0, i, 0))
    return pl.pallas_call(
        body, name=name, grid=(t // tm,),
        out_shape=(jax.ShapeDtypeStruct((t, d_model), F32), jax.ShapeDtypeStruct((t, d_model), _ACT_DTYPE),
                   jax.ShapeDtypeStruct((t, 2 * kvd), _ACT_DTYPE), jax.ShapeDtypeStruct((1, d_model), F32)),
        in_specs=[_rows(tm, d_model), _rows(tm, d_model), _const((1, d_model)), _const(w_kv.shape),
                  _rows(tm, LANES), _rows(tm, LANES)] + [hspec] * (2 * n_users),
        out_specs=(_rows(tm, d_model), _rows(tm, d_model), _rows(tm, 2 * kvd), _const((1, d_model))),
        compiler_params=_params(),
    )(dx, x, kv_g, w_kv, cos, ssin, *dks, *dvs)


def _q_fwd(x, pre_g, w_q, cos, ssin, layer, j, name):
    t, d_model = x.shape
    tm = _row_tile(t, PROJ_TILE)

    def body(x_ref, g_ref, w_ref, cos_ref, sin_ref, q_ref):
        q = _mm(_rms_fwd(x_ref[...], g_ref[...]), w_ref[...])
        for p in range(d_model // LANES):
            cols = slice(p * LANES, (p + 1) * LANES)
            q_ref[:, cols] = (_rope_fwd(q[:, cols], cos_ref[...], sin_ref[...]) * ATTN_SCALE).astype(q_ref.dtype)

    return pl.pallas_call(
        body, name=name, grid=(t // tm,), out_shape=jax.ShapeDtypeStruct((t, d_model), _ACT_DTYPE),
        in_specs=[_rows(tm, d_model), _const((None, 1, d_model), (layer, 0, 0)), _const((None, d_model, d_model), (j, 0, 0)),
                  _rows(tm, LANES), _rows(tm, LANES)],
        out_specs=_rows(tm, d_model), compiler_params=_params(),
    )(x, pre_g, w_q, cos, ssin)


def _q_bwd(dx, dqs, x, pre_g, w_q, cos, ssin, layer, j, name):
    t, d_model = x.shape
    tm = _row_tile(t, PROJ_TILE)

    def body(dx_ref, dq_ref, x_ref, g_ref, w_ref, cos_ref, sin_ref, dxi_ref, h_ref, dqo_ref, dg_ref):
        i = pl.program_id(0)

        @pl.when(i == 0)
        def _():
            dg_ref[...] = jnp.zeros_like(dg_ref)

        xv = x_ref[...]
        g = g_ref[...]
        h_ref[...] = _rms_fwd(xv, g).astype(h_ref.dtype)
        parts = []
        for p in range(d_model // LANES):
            cols = slice(p * LANES, (p + 1) * LANES)
            parts.append(_rope_bwd(dq_ref[:, cols] * ATTN_SCALE, cos_ref[...], sin_ref[...]))
        dq = jnp.concatenate(parts, axis=1).astype(_MXU_DTYPE)
        dqo_ref[...] = dq
        dxp, dg = _rms_bwd(xv, g, _mm_tb(dq, w_ref[...]))
        dg_ref[...] += dg
        dxi_ref[...] = dx_ref[...] + dxp

    act = jax.ShapeDtypeStruct((t, d_model), _ACT_DTYPE)
    return pl.pallas_call(
        body, name=name, grid=(t // tm,),
        out_shape=(jax.ShapeDtypeStruct((t, d_model), F32), act, act, jax.ShapeDtypeStruct((1, d_model), F32)),
        in_specs=[_rows(tm, d_model), _rows(tm, d_model), _rows(tm, d_model), _const((None, 1, d_model), (layer, 0, 0)),
                  _const((None, d_model, d_model), (j, 0, 0)), _rows(tm, LANES), _rows(tm, LANES)],
        out_specs=(_rows(tm, d_model), _rows(tm, d_model), _rows(tm, d_model), _const((1, d_model))),
        compiler_params=_params(),
    )(dx, dqs, x, pre_g, w_q, cos, ssin)


def _stack_heads(pairs):
    low = _low_half(pairs[0].shape)
    zero = jnp.zeros_like(pairs[0])
    return jnp.concatenate([h for blk in pairs for h in (jnp.where(low, blk, zero), jnp.where(low, zero, blk))], axis=0)


def _unstack_heads(stacked, i):
    a, b = stacked[2 * i * BLOCK:(2 * i + 1) * BLOCK], stacked[(2 * i + 1) * BLOCK:(2 * i + 2) * BLOCK]
    return jnp.where(_low_half(a.shape), a, b)


def _attn_scores(q_pairs, k2, n, sinks):
    qst = _stack_heads(q_pairs)
    s = _mm_tb(qst, k2)
    row = lax.broadcasted_iota(jnp.int32, s.shape, 0)
    col = lax.broadcasted_iota(jnp.int32, s.shape, 1)
    rel = BLOCK + (row & (BLOCK - 1)) - col
    valid = (rel >= 0) & (rel < WINDOW) & (n * BLOCK + col - BLOCK >= 0)
    s = jnp.where(valid, s, NEG_INF)
    rows1 = lax.broadcasted_iota(jnp.int32, (s.shape[0], 1), 0)
    sink = jnp.full((s.shape[0], 1), sinks[-1], F32)
    for i in reversed(range(len(sinks) - 1)):
        sink = jnp.where(rows1 < (i + 1) * BLOCK, sinks[i], sink)
    return qst, s, sink


def _attn_fwd(qs, kdup, vdup, sinks, j, name):
    t, d_model = qs.shape
    nb = t // BLOCK
    n_pairs = d_model // LANES
    per_group = n_pairs // N_KV_HEADS

    def body(sink_ref, q_ref, kp_ref, ko_ref, vp_ref, vo_ref, o_ref, lse_ref):
        n = pl.program_id(0)
        lane = _lane_iota((BLOCK, LANES))
        lse = jnp.zeros((BLOCK, LANES), F32)
        for hk in range(N_KV_HEADS):
            pairs = range(hk * per_group, (hk + 1) * per_group)
            heads = range(2 * pairs[0], 2 * pairs[-1] + 2)
            k2 = jnp.concatenate([kp_ref[hk], ko_ref[hk]], axis=0)
            v2 = jnp.concatenate([vp_ref[hk], vo_ref[hk]], axis=0)
            _, s, sink = _attn_scores([q_ref[:, p * LANES:(p + 1) * LANES] for p in pairs], k2, n,
                                      [sink_ref[j, h] for h in heads])
            m = jnp.maximum(jnp.max(s, axis=-1, keepdims=True), sink)
            pe = jnp.exp(s - m)
            denom = jnp.sum(pe, axis=-1, keepdims=True) + jnp.exp(sink - m)
            o2 = _mm(pe, v2) / denom
            l2 = m + jnp.log(denom)
            for i, p in enumerate(pairs):
                o_ref[:, p * LANES:(p + 1) * LANES] = _unstack_heads(o2, i).astype(o_ref.dtype)
            for i, h in enumerate(heads):
                lse = jnp.where(lane == h, l2[i * BLOCK:(i + 1) * BLOCK], lse)
        lse_ref[...] = lse

    prev = pl.BlockSpec((N_KV_HEADS, BLOCK, LANES), lambda n: (0, jnp.maximum(n - 1, 0), 0))
    own = pl.BlockSpec((N_KV_HEADS, BLOCK, LANES), lambda n: (0, n, 0))
    return pl.pallas_call(
        body, name=name, grid=(nb,),
        out_shape=(jax.ShapeDtypeStruct((t, d_model), _ACT_DTYPE), jax.ShapeDtypeStruct((t, LANES), F32)),
        in_specs=[pl.BlockSpec(memory_space=pltpu.SMEM), _rows(BLOCK, d_model), prev, own, prev, own],
        out_specs=(_rows(BLOCK, d_model), _rows(BLOCK, LANES)), compiler_params=_params(),
    )(sinks, qs, kdup, kdup, vdup, vdup)


def _attn_bwd(qs, kdup, vdup, sinks, lse, do, j, name):
    t, d_model = qs.shape
    nb = t // BLOCK
    n_pairs = d_model // LANES
    per_group = n_pairs // N_KV_HEADS
    rev = lambda n: nb - 1 - n

    def body(sink_ref, q_ref, kp_ref, ko_ref, vp_ref, vo_ref, lse_ref, do_ref, dq_ref, dk_ref, dv_ref, ds_ref, ck, cv):
        i = pl.program_id(0)
        n = nb - 1 - i

        @pl.when(i == 0)
        def _():
            ck[...] = jnp.zeros_like(ck)
            cv[...] = jnp.zeros_like(cv)
            ds_ref[...] = jnp.zeros_like(ds_ref)

        lane = _lane_iota((BLOCK, LANES))
        lane1 = _lane_iota((1, LANES))
        lsev = lse_ref[...]
        dsink = jnp.zeros((1, LANES), F32)
        for hk in range(N_KV_HEADS):
            pairs = range(hk * per_group, (hk + 1) * per_group)
            heads = range(2 * pairs[0], 2 * pairs[-1] + 2)
            k2 = jnp.concatenate([kp_ref[hk], ko_ref[hk]], axis=0)
            v2 = jnp.concatenate([vp_ref[hk], vo_ref[hk]], axis=0)
            qst, s, sink = _attn_scores([q_ref[:, p * LANES:(p + 1) * LANES] for p in pairs], k2, n,
                                        [sink_ref[j, h] for h in heads])
            l2 = jnp.concatenate([jnp.sum(jnp.where(lane == h, lsev, 0.0), axis=-1, keepdims=True) for h in heads], axis=0)
            pn = jnp.exp(s - l2)
            dost = _stack_heads([do_ref[:, p * LANES:(p + 1) * LANES] for p in pairs])
            dp = _mm_tb(dost, v2)
            dr = jnp.sum(pn * dp, axis=-1, keepdims=True)
            dsm = (pn * (dp - dr)).astype(_MXU_DTYPE)
            dsk = -jnp.exp(sink - l2) * dr
            for i, h in enumerate(heads):
                dsink = dsink + jnp.where(lane1 == h, jnp.sum(dsk[i * BLOCK:(i + 1) * BLOCK]), 0.0)
            dq2 = _mm(dsm, k2)
            for i, p in enumerate(pairs):
                dq_ref[:, p * LANES:(p + 1) * LANES] = _unstack_heads(dq2, i)
            for acc, carry, ref in ((_mm_ta(dsm, qst), ck, dk_ref), (_mm_ta(pn, dost), cv, dv_ref)):
                folded = acc + pltpu.roll(acc, 64, 1)
                ref[hk] = folded[BLOCK:] + carry[hk]
                carry[hk] = folded[:BLOCK]
        ds_ref[...] += dsink

    prev = pl.BlockSpec((N_KV_HEADS, BLOCK, LANES), lambda n: (0, jnp.maximum(rev(n) - 1, 0), 0))
    own = pl.BlockSpec((N_KV_HEADS, BLOCK, LANES), lambda n: (0, rev(n), 0))
    rows = lambda cols: pl.BlockSpec((BLOCK, cols), lambda n: (rev(n), 0))
    heads = jax.ShapeDtypeStruct((N_KV_HEADS, t, LANES), F32)
    return pl.pallas_call(
        body, name=name, grid=(nb,),
        out_shape=(jax.ShapeDtypeStruct((t, d_model), F32), heads, heads, jax.ShapeDtypeStruct((1, LANES), F32)),
        in_specs=[pl.BlockSpec(memory_space=pltpu.SMEM), rows(d_model), prev, own, prev, own, rows(LANES), rows(d_model)],
        out_specs=(rows(d_model), own, own, _const((1, LANES))),
        scratch_shapes=[pltpu.VMEM((N_KV_HEADS, BLOCK, LANES), F32), pltpu.VMEM((N_KV_HEADS, BLOCK, LANES), F32)],
        compiler_params=_params(),
    )(sinks, qs, kdup, kdup, vdup, vdup, lse, do)


def _oproj_fwd(x, o, w_o, post_g, layer, j, name):
    t, d_model = x.shape
    tm = _row_tile(t, PROJ_TILE)

    def body(x_ref, o_ref, w_ref, g_ref, xo_ref, mo_ref):
        mo = _mm(o_ref[...], w_ref[...])
        mo_ref[...] = mo
        xo_ref[...] = x_ref[...] + _rms_fwd(mo, g_ref[...])

    full = jax.ShapeDtypeStruct((t, d_model), F32)
    return pl.pallas_call(
        body, name=name, grid=(t // tm,), out_shape=(full, full),
        in_specs=[_rows(tm, d_model), _rows(tm, d_model), _const((None, d_model, d_model), (j, 0, 0)),
                  _const((None, 1, d_model), (layer, 0, 0))],
        out_specs=(_rows(tm, d_model), _rows(tm, d_model)), compiler_params=_params(),
    )(x, o, w_o, post_g)


def _oproj_bwd(dx, mo, w_o, post_g, layer, j, name):
    t, d_model = dx.shape
    tm = _row_tile(t, PROJ_TILE)

    def body(dx_ref, mo_ref, w_ref, g_ref, do_ref, dmo_ref, dg_ref):
        i = pl.program_id(0)

        @pl.when(i == 0)
        def _():
            dg_ref[...] = jnp.zeros_like(dg_ref)

        dmo, dg = _rms_bwd(mo_ref[...], g_ref[...], dx_ref[...])
        dg_ref[...] += dg
        dmo = dmo.astype(_MXU_DTYPE)
        dmo_ref[...] = dmo
        do_ref[...] = _mm_tb(dmo, w_ref[...]).astype(do_ref.dtype)

    act = jax.ShapeDtypeStruct((t, d_model), _ACT_DTYPE)
    return pl.pallas_call(
        body, name=name, grid=(t // tm,), out_shape=(act, act, jax.ShapeDtypeStruct((1, d_model), F32)),
        in_specs=[_rows(tm, d_model), _rows(tm, d_model), _const((None, d_model, d_model), (j, 0, 0)),
                  _const((None, 1, d_model), (layer, 0, 0))],
        out_specs=(_rows(tm, d_model), _rows(tm, d_model), _const((1, d_model))), compiler_params=_params(),
    )(dx, mo, w_o, post_g)


def _loss_grad(y, target, name):
    t, d_model = y.shape
    tm = _row_tile(t, PROJ_TILE)

    def body(y_ref, t_ref, dy_ref, loss_ref):
        i = pl.program_id(0)

        @pl.when(i == 0)
        def _():
            loss_ref[...] = jnp.zeros_like(loss_ref)

        err = y_ref[...] - t_ref[...]
        dy_ref[...] = err / d_model
        loss_ref[...] += 0.5 * jnp.sum(jnp.mean(err * err, axis=-1, keepdims=True), axis=0, keepdims=True)

    return pl.pallas_call(
        body, name=name, grid=(t // tm,),
        out_shape=(jax.ShapeDtypeStruct((t, d_model), F32), jax.ShapeDtypeStruct((1, 1), F32)),
        in_specs=[_rows(tm, d_model), _rows(tm, d_model)], out_specs=(_rows(tm, d_model), _const((1, 1))),
        compiler_params=_params(),
    )(y, target)


def _mesh_position():
    return lax.axis_index("x"), lax.axis_index("y"), lax.axis_index("c")


def _block_of(px, py, pc):
    return 4 * px + 2 * py + pc


def _at_block(ref, axis, block):
    return ref.at[(slice(None),) * axis + (block,)]


def _all_gather(shards, axes, name):
    n = len(shards)

    def body(*refs):
        srcs, outs = refs[:n], refs[n:2 * n]
        send_sems, recv_sems, local_sems = refs[2 * n:]
        x, y, c = _mesh_position()
        me, sibling = (x, y, c), (x, y, 1 - c)
        chips = [(1 - x, y), (x, 1 - y), (1 - x, 1 - y)]

        def blk(i, pos):
            return _at_block(outs[i], axes[i], _block_of(*pos))

        def copy(i, k, block, to, src=None):
            return pltpu.make_async_remote_copy(
                src_ref=blk(i, block) if src is None else src, dst_ref=blk(i, block),
                send_sem=send_sems.at[i, k], recv_sem=recv_sems.at[i, k], device_id=to, device_id_type=MESH)

        mine = [pltpu.make_async_copy(srcs[i], blk(i, me), local_sems.at[i]) for i in range(n)]
        for cp in mine:
            cp.start()
        sent = []
        for i in range(n):
            sent += [copy(i, 1 + k, me, (*chip, c), src=srcs[i]) for k, chip in enumerate(chips)]
            sent.append(copy(i, 0, me, sibling, src=srcs[i]))
        for cp in sent:
            cp.start()
        for i in range(n):
            for k, chip in enumerate(chips):
                copy(i, 1 + k, (*chip, c), me).wait_recv()
                passed = copy(i, 4 + k, (*chip, c), sibling)
                passed.start()
                sent.append(passed)
        for i in range(n):
            copy(i, 0, sibling, me).wait_recv()
            for k, chip in enumerate(chips):
                copy(i, 4 + k, (*chip, 1 - c), me).wait_recv()
        for cp in sent:
            cp.wait_send()
        for cp in mine:
            cp.wait()

    hbm = pl.BlockSpec(memory_space=pl.ANY)
    return pl.pallas_call(
        body, name=name,
        out_shape=tuple(jax.ShapeDtypeStruct(s.shape[:a] + (N_DEV,) + s.shape[a:], s.dtype) for s, a in zip(shards, axes)),
        in_specs=[hbm] * n, out_specs=(hbm,) * n,
        scratch_shapes=[pltpu.SemaphoreType.DMA((n, 7)), pltpu.SemaphoreType.DMA((n, 7)), pltpu.SemaphoreType.DMA((n,))],
    )(*shards)


GATHER, SCATTER, GATHER_CHIPS, GATHER_SIBLING = "gather", "scatter", "gather_chips", "gather_sibling"
COPIES = {GATHER: N_DEV - 1, SCATTER: N_DEV - 1, GATHER_CHIPS: 4, GATHER_SIBLING: 3}


def _land_shape(kind, s, axis):
    if kind == SCATTER:
        return (N_DEV,) + s.shape[:axis] + s.shape[axis + 1:]
    return s.shape[:axis] + (N_DEV,) + s.shape[axis:]


def _plan(kind, srcs, lands, axes):
    x, y, c = _mesh_position()
    my_block = _block_of(x, y, c)
    flips = {GATHER_CHIPS: (1, 4, 2, 6), GATHER_SIBLING: (4, 2, 6)}.get(kind, range(1, N_DEV))
    others = [(1 - x if k & 4 else x, 1 - y if k & 2 else y, 1 - c if k & 1 else c) for k in flips]
    remote = []
    for src, land, axis in zip(srcs, lands, axes):
        if kind == SCATTER:
            mine = land.at[my_block]
            remote += [(_at_block(src, axis, _block_of(*peer)), mine, peer, land.at[_block_of(*peer)]) for peer in others]
        elif kind == GATHER_SIBLING:
            for px, py, _ in others:
                mine, theirs = _at_block(land, axis, _block_of(px, py, c)), _at_block(land, axis, _block_of(px, py, 1 - c))
                remote.append((mine, mine, (x, y, 1 - c), theirs))
        else:
            mine = _at_block(land, axis, my_block)
            remote += [(src, mine, peer, _at_block(land, axis, _block_of(*peer))) for peer in others]
    return remote


def _remote(src, dst, send_sems, recv_sems, k, peer):
    return pltpu.make_async_remote_copy(src_ref=src, dst_ref=dst, send_sem=send_sems.at[k], recv_sem=recv_sems.at[k],
                                        device_id=peer, device_id_type=MESH)


_HBM = pl.BlockSpec(memory_space=pltpu.HBM)
_SEM = pl.BlockSpec(memory_space=pltpu.SEMAPHORE)
_SPLIT = dict(has_side_effects=pltpu.SideEffectType.DATAFLOW_SIDE_EFFECTING)


def _landing_zone(kind, s, axis, me):
    land = lax.empty(_land_shape(kind, s, axis), s.dtype)
    if kind == SCATTER:
        return lax.dynamic_update_slice_in_dim(land, lax.dynamic_slice_in_dim(s, me, 1, axis).reshape((1,) + land.shape[1:]), me, 0)
    return lax.dynamic_update_slice_in_dim(land, jnp.expand_dims(s, axis), me, axis)


def _exchange_start(kind, arrays, axes, after, name):
    n = len(arrays)
    if kind == GATHER_SIBLING:
        passed = list(arrays)
    else:
        me = _block_of(*_mesh_position())
        passed = list(arrays) + [_landing_zone(kind, s, a, me) for s, a in zip(arrays, axes)]
    n_sems = n * COPIES[kind]

    def body(*refs):
        land_refs = refs[len(passed) - n:len(passed)]
        send_sems, recv_sems = refs[len(passed) + 1], refs[len(passed) + 2]
        token = refs[-1]
        for k, (src, dst, peer, _) in enumerate(_plan(kind, refs[:n], land_refs, axes)):
            _remote(src, dst, send_sems, recv_sems, k, peer).start()
        token[...] = jnp.zeros_like(token)

    out = pl.pallas_call(
        body, name=name,
        out_shape=(pltpu.SemaphoreType.DMA((n_sems,)), pltpu.SemaphoreType.DMA((n_sems,)),
                   *[pltpu.HBM(a.shape, a.dtype) for a in passed], jax.ShapeDtypeStruct((8, LANES), F32)),
        in_specs=[_HBM] * len(passed) + [pl.BlockSpec(memory_space=pl.ANY)],
        out_specs=(_SEM, _SEM, *[_HBM] * len(passed), pl.BlockSpec(memory_space=pltpu.VMEM)),
        input_output_aliases={i: 2 + i for i in range(len(passed))},
        compiler_params=pltpu.CompilerParams(**_SPLIT),
    )(*[pltpu.with_memory_space_constraint(a, pltpu.HBM) for a in passed], after)
    return (kind, axes, n, out[:-1]), out[-1]


def _exchange_wait(handle, after, name):
    kind, axes, n, (send_sems, recv_sems, *thru) = handle

    def body(*refs):
        land_refs = refs[len(thru) - n:len(thru)]
        send_sems, recv_sems = refs[len(thru)], refs[len(thru) + 1]
        for k, (src, _, peer, arrives) in enumerate(_plan(kind, refs[:n], land_refs, axes)):
            cp = _remote(src, arrives, send_sems, recv_sems, k, peer)
            cp.wait_send()
            cp.wait_recv()

    out = pl.pallas_call(
        body, name=name,
        out_shape=tuple(pltpu.HBM(a.shape, a.dtype) for a in thru),
        in_specs=[_HBM] * len(thru) + [_SEM, _SEM, pl.BlockSpec(memory_space=pl.ANY)], out_specs=(_HBM,) * len(thru),
        input_output_aliases={i: i for i in range(len(thru))},
        compiler_params=pltpu.CompilerParams(**_SPLIT),
    )(*thru, send_sems, recv_sems, after)
    return out[len(thru) - n:]


def _adamw_math(w, g, m, v):
    m = ADAM_B1 * m + (1.0 - ADAM_B1) * g
    v = ADAM_B2 * v + (1.0 - ADAM_B2) * jnp.square(g)
    m_hat = m / (1.0 - ADAM_B1 ** ADAM_STEP)
    v_hat = v / (1.0 - ADAM_B2 ** ADAM_STEP)
    delta = -ADAM_LR * (m_hat / (jnp.sqrt(v_hat) + ADAM_EPS) + ADAM_WD * w)
    return delta, m, v


def _update_tile(rows):
    if rows <= 512:
        return rows
    for tr in (512, 384, 352, 256, 176, 128, 64, 32, 16):
        if rows % tr == 0:
            return tr
    raise ValueError(f"{rows} rows do not tile")


def _adamw(parts, w, m, v, slab, so_far, name):
    rows, c = w.shape
    r = parts.shape[1]
    tr = _update_tile(r)
    first = slab * (r // tr)
    if so_far is None:
        so_far = tuple(lax.empty((rows, c), F32) for _ in range(4))

    def body(p_ref, w_ref, m_ref, v_ref, *refs):
        g_ref, d_ref, mo_ref, vo_ref = refs[4:]
        g = p_ref[0].astype(F32)
        for s in range(1, N_DEV):
            g = g + p_ref[s].astype(F32)
        g_ref[...] = g
        d_ref[...], mo_ref[...], vo_ref[...] = _adamw_math(w_ref[...], g, m_ref[...], v_ref[...])

    out = jax.ShapeDtypeStruct((rows, c), F32)
    tile = pl.BlockSpec((tr, c), lambda i: (first + i, 0))
    return pl.pallas_call(
        body, name=name, grid=(r // tr,), out_shape=(out,) * 4,
        in_specs=[pl.BlockSpec((N_DEV, tr, c), lambda i: (0, i, 0))] + [tile] * 3 + [pl.BlockSpec(memory_space=pl.ANY)] * 4,
        out_specs=(tile,) * 4, input_output_aliases={4 + k: k for k in range(4)}, compiler_params=_params(),
    )(parts, w, m, v, *so_far)


def _adamw_small(parts, picks, weights, name):
    n = len(parts)

    def body(*refs):
        p_refs, wmv, outs = refs[:n], refs[n:4 * n], refs[4 * n:]
        me = _block_of(*_mesh_position())
        for i in range(n):
            g = picks[i](p_refs[i], 0, me)
            for s in range(1, N_DEV):
                g = g + picks[i](p_refs[i], s, me)
            w_ref, m_ref, v_ref = wmv[3 * i:3 * i + 3]
            g_ref, d_ref, mo_ref, vo_ref = outs[4 * i:4 * i + 4]
            g_ref[...] = g
            d_ref[...], mo_ref[...], vo_ref[...] = _adamw_math(w_ref[...], g, m_ref[...], v_ref[...])

    flat = [a for wmv in weights for a in wmv]
    out = pl.pallas_call(
        body, name=name,
        out_shape=tuple(jax.ShapeDtypeStruct(w.shape, F32) for w, _, _ in weights for _ in range(4)),
        compiler_params=pltpu.CompilerParams(vmem_limit_bytes=VMEM_LIMIT),
    )(*parts, *flat)
    return [tuple(out[4 * i:4 * i + 4]) for i in range(n)]


def kernel(x, positions, mix_pre_g, mix_post_g, pool_w, pool_scale, kv_norm_g, w_kv, w_q, w_o, sinks, ffn_pre_g, ffn_post_g, ffn_w_in, ffn_conv_w, ffn_conv_b, ffn_w_out, loss_target, m_mix_pre_g, m_mix_post_g, m_pool_w, m_pool_scale, m_kv_norm_g, m_w_kv, m_w_q, m_w_o, m_sinks, m_ffn_pre_g, m_ffn_post_g, m_ffn_w_in, m_ffn_conv_w, m_ffn_conv_b, m_ffn_w_out, v_mix_pre_g, v_mix_post_g, v_pool_w, v_pool_scale, v_kv_norm_g, v_w_kv, v_w_q, v_w_o, v_sinks, v_ffn_pre_g, v_ffn_post_g, v_ffn_w_in, v_ffn_conv_w, v_ffn_conv_b, v_ffn_w_out):
    depth, d_model = mix_pre_g.shape
    n_a = pool_w.shape[0]
    n_b = w_q.shape[0]
    t = x.shape[1]
    fs = ffn_w_in.shape[2]
    half = N_DEV // 2
    n_heads = d_model // HEAD_DIM
    x0 = x.reshape(t, d_model)
    target = loss_target.reshape(t, d_model)

    inv_freq = 1.0 / (ROPE_THETA ** (jnp.arange(0, HEAD_DIM, 2, dtype=F32) / HEAD_DIM))
    ang = positions.reshape(t).astype(F32)[:, None] * inv_freq
    cos, sin = jnp.cos(ang), jnp.sin(ang)
    cos = jnp.tile(cos, (1, 2 * LANES // HEAD_DIM))
    ssin = jnp.tile(jnp.concatenate([-sin, sin], axis=1), (1, LANES // HEAD_DIM))

    wire = lambda a: a.astype(_WIRE_DTYPE)
    by_hidden = lambda a: a.transpose(0, 2, 1)
    w_in_b, w_out_b = wire(by_hidden(ffn_w_in)), wire(ffn_w_out)
    pool_w_g, w_in_0, w_out_0, pool_scale_g, conv_w_g = _all_gather(
        [wire(pool_w), w_in_b[:1], w_out_b[:1], pool_scale, ffn_conv_w], [2, 1, 1, 0, 0], "gather_first")
    groups = []
    for l in range(1, depth):
        if l == n_a:
            groups.append(("attn", [wire(w_kv), wire(w_q), wire(w_o)], [0, 1, 1], l))
        groups.append((l, [w_in_b[l:l + 1], w_out_b[l:l + 1]], [1, 1], l))
    over_ici, to_sibling, tokens, after = {}, {}, [], w_in_0
    for key, shards, axes, _ in groups:
        over_ici[key], after = _exchange_start(GATHER_CHIPS, shards, axes, after, f"gather_chips_{key}")
        tokens.append(after)
    started = functools.reduce(lambda a, b: a + b, [tk[0, 0] for tk in tokens])

    def pass_on(layer, after):
        sent = jnp.zeros((), F32)
        for key, _, axes, first in groups:
            if first == layer:
                lands = _exchange_wait(over_ici[key], after, f"gather_chips_wait_{key}")
                to_sibling[key], tk = _exchange_start(GATHER_SIBLING, lands, axes, after, f"gather_sibling_{key}")
                sent = sent + tk[0, 0]
        return sent

    w_in_l, w_out_l = {0: w_in_0}, {0: w_out_0.reshape(1, half * fs, d_model)}
    pool_scale_f = pool_scale_g.transpose(1, 0, 2).reshape(n_a, 1, d_model)
    conv_w_f = conv_w_g.transpose(1, 2, 0, 3)
    pool_w_f = pool_w_g.reshape(n_a, len(POOL_WINDOWS), d_model // len(POOL_WINDOWS), -1)
    conv_b_f = ffn_conv_b.reshape(depth, N_DEV, fs)
    g3 = lambda a: a.reshape(a.shape[0], 1, a.shape[1])
    mix_pre, mix_post, ffn_pre, ffn_post = g3(mix_pre_g) + started, g3(mix_post_g), g3(ffn_pre_g), g3(ffn_post_g)
    kv_g = kv_norm_g.reshape(1, d_model)
    w_kv_f = w_q_f = w_o_f = None

    saved = []
    xc = x0
    kdup = vdup = x_kv = None
    for l in range(depth):
        x_in = xc
        if l < n_a:
            x_mid, dsave, yu = _pool_fwd(x_in, mix_pre, mix_post, pool_w_f, pool_scale_f, l, f"pool_fwd_{l}")
            mixer = (dsave, yu)
        else:
            j = l - n_a
            if j == 0:
                x_kv = x_in
                w_kv_g, w_q_g, w_o_g = _exchange_wait(to_sibling["attn"], x_in, "gather_sibling_wait_attn")
                w_kv_f = w_kv_g.reshape(d_model, -1)
                w_q_f = w_q_g.reshape(n_b, d_model, d_model)
                w_o_f = w_o_g.reshape(n_b, d_model, d_model)
                kdup, vdup = _kv_fwd(x_kv, kv_g, w_kv_f, cos, ssin, "kv_fwd")
            qs = _q_fwd(x_in, mix_pre, w_q_f, cos, ssin, l, j, f"q_fwd_{l}")
            o, lse = _attn_fwd(qs, kdup, vdup, sinks, j, f"attn_fwd_{l}")
            x_mid, mo = _oproj_fwd(x_in, o, w_o_f, mix_post, l, j, f"oproj_fwd_{l}")
            mixer = (qs, o, lse, mo)
        if l > 0:
            w_in_l[l], w_out_g = _exchange_wait(to_sibling[l], x_mid, f"gather_sibling_wait_{l}")
            w_out_l[l] = w_out_g.reshape(1, half * fs, d_model)
        xc, u, uc, f = _ffn_fwd(x_mid, ffn_pre, ffn_post, w_in_l[l], conv_w_f, conv_b_f, w_out_l[l], l, 0, f"ffn_fwd_{l}")
        saved.append((x_in, x_mid, u, uc, f, mixer))
        if l + 1 < depth:
            mix_pre = mix_pre + pass_on(l + 1, xc)

    dx, loss_part = _loss_grad(xc, target, "loss")

    gconv_w, gconv_b = [None] * depth, [None] * depth
    gmix_pre, gmix_post, gffn_pre, gffn_post = [None] * depth, [None] * depth, [None] * depth, [None] * depth
    gpool_scale, gsinks = [None] * n_a, [None] * n_b
    dks, dvs = [], []
    gkv_g = None
    whole = lambda cols: pl.BlockSpec((t, cols), lambda c: (0, 0), pipeline_mode=pl.Buffered(1))
    per_out = lambda cols: pl.BlockSpec((None, t, cols), lambda c: (c, 0, 0))
    by_dev = lambda g: g.reshape(N_DEV, -1, g.shape[-1])
    def small_grads():
        cat = lambda rows: jnp.concatenate(rows, axis=0)
        row = lambda a: a.reshape(1, -1)
        everything = lambda ref, s, me: ref[s]
        lanes = pool_scale.shape[1]
        return [("mix_pre_g", cat(gmix_pre), everything, (mix_pre_g, m_mix_pre_g, v_mix_pre_g)),
                ("mix_post_g", cat(gmix_post), everything, (mix_post_g, m_mix_post_g, v_mix_post_g)),
                ("kv_norm_g", gkv_g, everything, (row(kv_norm_g), row(m_kv_norm_g), row(v_kv_norm_g))),
                ("sinks", cat(gsinks), lambda ref, s, me: ref[s, :, pl.ds(0, n_heads)], (sinks, m_sinks, v_sinks)),
                ("ffn_pre_g", cat(gffn_pre), everything, (ffn_pre_g, m_ffn_pre_g, v_ffn_pre_g)),
                ("ffn_post_g", cat(gffn_post), everything, (ffn_post_g, m_ffn_post_g, v_ffn_post_g)),
                ("ffn_conv_b", jnp.stack(gconv_b).reshape(depth, N_DEV * fs), everything, (ffn_conv_b, m_ffn_conv_b, v_ffn_conv_b)),
                ("pool_scale", cat(gpool_scale), lambda ref, s, me: ref[s, :, pl.ds(pl.multiple_of(me * lanes, lanes), lanes)],
                 (pool_scale, m_pool_scale, v_pool_scale)),
                ("ffn_conv_w", jnp.stack(gconv_w).transpose(0, 2, 1, 3), lambda ref, s, me: ref[s, :, me],
                 (ffn_conv_w, m_ffn_conv_w, v_ffn_conv_w))]

    flying = []

    def launch(going, after, name):
        handle, token = _exchange_start(SCATTER, [g for _, _, g, _ in going], [a for _, _, _, a in going], after, name)
        flying.append(([(nm, slab) for nm, slab, _, _ in going], handle))
        return token

    post = mix_post
    ffn_post_b = ffn_post
    token = None
    for l in reversed(range(depth)):
        x_in, x_mid, u, uc, f, mixer = saved[l]
        duc, gout, gconv_b[l], gffn_post[l] = _ffn_bwd_out(dx, f, ffn_post_b, uc, conv_b_f, w_out_l[l], l, 0, f"ffn_bwd_out_{l}")
        going = [("ffn_w_out", l, by_dev(gout), 0)]
        pre = ffn_pre
        if l == 0:
            token = launch(going, gout, "scatter_start_0_out")
            going = []
            pre = pre + token[0, 0]
        dx, du, hf, gconv_w[l], gffn_pre[l] = _ffn_bwd_in(dx, x_mid, pre, duc, u, conv_w_f, w_in_l[l], l, 0, f"ffn_bwd_in_{l}")
        gin = _tn_matmul(du, hf, per_out(fs), whole(d_model), N_DEV, fs, d_model, f"grad_w_in_{l}", _WIRE_DTYPE)
        going.append(("ffn_w_in", l, gin, 0))
        if l == 0:
            token = launch(going, dx, "scatter_start_0_in")
            going = []
            post = post + token[0, 0]
        if l < n_a:
            dsave, yu = mixer
            dx, dyu, gpool_scale[l], gmix_pre[l], gmix_post[l] = _pool_bwd(
                dx, x_in, mix_pre, post, dsave, yu, pool_w_f, pool_scale_f, l, f"pool_bwd_{l}")
            gc = d_model // len(POOL_WINDOWS)
            by_group = pl.BlockSpec((t, gc), lambda c: (0, c))
            gpool = _tn_matmul(dsave, dyu, by_group, by_group, len(POOL_WINDOWS), gc, gc, f"grad_pool_w_{l}", _WIRE_DTYPE)
            going.append(("pool_w", l, gpool.reshape(len(POOL_WINDOWS), N_DEV, -1, gc), 1))
        else:
            j = l - n_a
            qs, o, lse, mo = mixer
            do, dmo, gmix_post[l] = _oproj_bwd(dx, mo, w_o_f, post, l, j, f"oproj_bwd_{l}")
            dqs, dk, dv, gsinks[j] = _attn_bwd(qs, kdup, vdup, sinks, lse, do, j, f"attn_bwd_{l}")
            dks.append(dk)
            dvs.append(dv)
            dx, hq, dq, gmix_pre[l] = _q_bwd(dx, dqs, x_in, mix_pre, w_q_f, cos, ssin, l, j, f"q_bwd_{l}")
            if j == 0:
                dx, hkv, dkv, gkv_g = _kv_bwd(dx, x_kv, kv_g, w_kv_f, cos, ssin, dks, dvs, "kv_bwd")
                gkv = _tn_matmul(hkv, dkv, whole(d_model), whole(dkv.shape[1]), 1, d_model, dkv.shape[1], "grad_w_kv", _WIRE_DTYPE)
                going.append(("w_kv", 0, by_dev(gkv), 0))
            go = _tn_matmul(o, dmo, whole(d_model), whole(d_model), 1, d_model, d_model, f"grad_w_o_{l}", _WIRE_DTYPE)
            gq = _tn_matmul(hq, dq, whole(d_model), whole(d_model), 1, d_model, d_model, f"grad_w_q_{l}", _WIRE_DTYPE)
            going += [("w_o", j, by_dev(go), 0), ("w_q", j, by_dev(gq), 0)]
        after = dx
        if l == 0:
            small = small_grads()
            leaving = [g for _, g, _, _ in small] + [jnp.broadcast_to(loss_part, (1, LANES))]
            small_flight, after = _exchange_start(GATHER, leaving, [0] * len(leaving), dx, "gather_small_grads")
        token = launch(going, after, f"scatter_start_{l}")
        ffn_post_b = ffn_post_b + token[0, 0]

    grad_x = dx.reshape(x.shape)

    shard = {"pool_w": (pool_w, m_pool_w, v_pool_w), "w_kv": (w_kv, m_w_kv, v_w_kv), "w_q": (w_q, m_w_q, v_w_q),
             "w_o": (w_o, m_w_o, v_w_o), "ffn_w_in": tuple(by_hidden(a) for a in (ffn_w_in, m_ffn_w_in, v_ffn_w_in)),
             "ffn_w_out": (ffn_w_out, m_ffn_w_out, v_ffn_w_out)}
    big = {}

    def arrive(flights, after):
        for idx, (names, handle) in flights:
            parts = _exchange_wait(handle, after, f"scatter_wait_{idx}")
            for (nm, slab), p in zip(names, parts):
                cols = p.shape[-1]
                w2, m2, v2 = (a.reshape(-1, cols) for a in shard[nm])
                big[nm] = _adamw(p.reshape(N_DEV, -1, cols), w2, m2, v2, slab, big.get(nm), f"adamw_{nm}_{slab}")
                after = big[nm][0]
        return after

    done = arrive(list(enumerate(flying)), token)
    *small_parts, loss_parts = _exchange_wait(small_flight, done, "gather_small_grads_wait")
    loss = jnp.sum(loss_parts[:, 0, 0])
    upd = _adamw_small(small_parts, [pick for _, _, pick, _ in small], [wmv for _, _, _, wmv in small], "adamw_small")
    res = {nm: tuple(r.reshape(shard[nm][0].shape) for r in out) for nm, out in big.items()}
    res["ffn_w_in"] = tuple(by_hidden(r) for r in res["ffn_w_in"])
    for (nm, _, _, _), out in zip(small, upd):
        res[nm] = tuple(a.reshape(kv_norm_g.shape) for a in out) if nm == "kv_norm_g" else out

    order = ["mix_pre_g", "mix_post_g", "pool_w", "pool_scale", "kv_norm_g", "w_kv", "w_q", "w_o", "sinks", "ffn_pre_g",
             "ffn_post_g", "ffn_w_in", "ffn_conv_w", "ffn_conv_b", "ffn_w_out"]
    return (loss, grad_x, *[res[nm][0] for nm in order], *[res[nm][1] for nm in order],
            *[res[nm][2] for nm in order], *[res[nm][3] for nm in order])
```

```python
import functools
import math

import jax
import jax.numpy as jnp
from jax import lax
from jax.experimental import pallas as pl
from jax.experimental.pallas import tpu as pltpu

F32 = jnp.float32
_MXU_DTYPE = jnp.bfloat16
_ACT_DTYPE = jnp.bfloat16
_WIRE_DTYPE = jnp.bfloat16
_SAVE_DTYPE = jnp.bfloat16

N_DEV = 8
POOL_WINDOWS = (2, 4, 8, 16)
POOL_HALO = 32
assert POOL_WINDOWS == tuple(2 ** (g + 1) for g in range(len(POOL_WINDOWS))) and 8 * len(POOL_WINDOWS) <= POOL_HALO
HEAD_DIM = 64
N_KV_HEADS = 4
WINDOW = 128
BLOCK = 128
LANES = 128
ROPE_THETA = 10000.0
ATTN_SCALE = 1.0 / math.sqrt(HEAD_DIM)
NEG_INF = -1e30
RMS_EPS = 1e-6
CONV_HALO = 8
SAVE_HALO = 16
PROJ_TILE = 512
ADAM_LR = 0.001
ADAM_B1 = 0.9
ADAM_B2 = 0.999
ADAM_EPS = 1e-08
ADAM_WD = 0.01
ADAM_STEP = 10
VMEM_LIMIT = 56 * 1024 * 1024
MESH = pl.DeviceIdType.MESH


def _params(n_axes=1, vmem=VMEM_LIMIT):
    return pltpu.CompilerParams(dimension_semantics=("arbitrary",) * n_axes, vmem_limit_bytes=vmem)


def _resident(shape, index):
    return pl.BlockSpec(shape, lambda *_: index, pipeline_mode=pl.Buffered(1))


def _const(shape, index=None):
    index = (0,) * len(shape) if index is None else index
    return pl.BlockSpec(shape, lambda *_: index)


def _rows(tm, cols):
    return pl.BlockSpec((tm, cols), lambda i: (i, 0))


def _row_tile(t, most=256):
    for tm in (512, 256, 128, 64, 32, 16, 8):
        if tm <= most and t % tm == 0:
            return tm
    raise ValueError(f"sequence length {t} is not a multiple of 8")


def _mm(a, b):
    return jnp.dot(a.astype(_MXU_DTYPE), b.astype(_MXU_DTYPE), preferred_element_type=F32)


def _mm_tb(a, b):
    return lax.dot_general(a.astype(_MXU_DTYPE), b.astype(_MXU_DTYPE), (((1,), (1,)), ((), ())),
                           preferred_element_type=F32)


def _mm_ta(a, b):
    return lax.dot_general(a.astype(_MXU_DTYPE), b.astype(_MXU_DTYPE), (((0,), (0,)), ((), ())),
                           preferred_element_type=F32)


def _rms_r(x):
    return lax.rsqrt(jnp.mean(x * x, axis=-1, keepdims=True) + RMS_EPS)


def _rms_fwd(x, g):
    return (x * _rms_r(x)) * g


def _rms_bwd(x, g, dy):
    r = _rms_r(x)
    xh = x * r
    dg = jnp.sum(dy * xh, axis=0, keepdims=True)
    dxh = dy * g
    dx = r * (dxh - xh * jnp.mean(dxh * xh, axis=-1, keepdims=True))
    return dx, dg


_GELU_C = math.sqrt(2.0 / math.pi)


def _gelu_parts(z):
    z2 = z * z
    e = jnp.exp(z * (-2.0 * _GELU_C - (2.0 * _GELU_C * 0.044715) * z2))
    cdf = pl.reciprocal(1.0 + e, approx=False)
    dz = cdf + (z * (cdf * (1.0 - cdf))) * (2.0 * _GELU_C + (6.0 * _GELU_C * 0.044715) * z2)
    return cdf, dz


def _lane_iota(shape):
    return lax.broadcasted_iota(jnp.int32, shape, len(shape) - 1)


def _rope_partner(xb):
    first = (_lane_iota(xb.shape) & 32) == 0
    return jnp.where(first, pltpu.roll(xb, LANES - 32, 1), pltpu.roll(xb, 32, 1))


def _rope_fwd(xb, cos, ssin):
    return xb * cos + _rope_partner(xb) * ssin


def _rope_bwd(dyb, cos, ssin):
    return dyb * cos - _rope_partner(dyb) * ssin


def _low_half(shape):
    return (_lane_iota(shape) & 64) == 0


def _pool_fwd(x, pre_g, post_g, w, scale, layer, name):
    t, d_model = x.shape
    tm = _row_tile(t)
    n_groups, gc = w.shape[1], w.shape[2]

    def body(x_ref, pre_ref, post_ref, w_ref, sc_ref, xo_ref, d_ref, yu_ref, hbuf, sbuf):
        i = pl.program_id(0)

        @pl.when(i == 0)
        def _():
            hbuf[pl.ds(0, POOL_HALO), :] = jnp.zeros((POOL_HALO, d_model), F32)

        xv = x_ref[...]
        hbuf[pl.ds(POOL_HALO, tm), :] = _rms_fwd(xv, pre_ref[...])
        tok = i * tm + lax.broadcasted_iota(jnp.int32, (tm, 1), 0)
        yus = []
        for gi, wnd in enumerate(POOL_WINDOWS):
            first, reach = CONV_HALO * (gi + 1), wnd // 2
            n = POOL_HALO + tm - first
            cols = pl.ds(gi * gc, d_model - gi * gc)
            src = hbuf if gi == 0 else sbuf.at[gi - 1]
            level = src[pl.ds(first, n), cols] + src[pl.ds(first - reach, n), cols]
            if gi + 1 < len(POOL_WINDOWS):
                sbuf[gi, pl.ds(first, n), cols] = level
            h = hbuf[pl.ds(POOL_HALO, tm), pl.ds(gi * gc, gc)]
            cnt = jnp.minimum(tok + 1, wnd).astype(F32)
            dg = level[POOL_HALO - first:, :gc] / cnt - h
            d_ref[:, pl.ds(gi * gc, gc)] = dg.astype(d_ref.dtype)
            yus.append(_mm(dg, w_ref[gi]))
        hbuf[pl.ds(0, POOL_HALO), :] = hbuf[pl.ds(tm, POOL_HALO), :]
        yu = jnp.concatenate(yus, axis=1)
        yu_ref[...] = yu
        xo_ref[...] = xv + _rms_fwd(yu * sc_ref[...], post_ref[...])

    return pl.pallas_call(
        body, name=name, grid=(t // tm,),
        out_shape=(jax.ShapeDtypeStruct((t, d_model), F32), jax.ShapeDtypeStruct((t, d_model), _ACT_DTYPE),
                   jax.ShapeDtypeStruct((t, d_model), F32)),
        in_specs=[_rows(tm, d_model), _const((None, 1, d_model), (layer, 0, 0)), _const((None, 1, d_model), (layer, 0, 0)),
                  _const((None, n_groups, gc, gc), (layer, 0, 0, 0)), _const((None, 1, d_model), (layer, 0, 0))],
        out_specs=(_rows(tm, d_model), _rows(tm, d_model), _rows(tm, d_model)),
        scratch_shapes=[pltpu.VMEM((POOL_HALO + tm, d_model), F32),
                        pltpu.VMEM((len(POOL_WINDOWS) - 1, POOL_HALO + tm, d_model), F32)],
        compiler_params=_params(),
    )(x, pre_g, post_g, w, scale)


def _pool_bwd(dx, x, pre_g, post_g, d, yu, w, scale, layer, name):
    t, d_model = x.shape
    tm = _row_tile(t)
    nt = t // tm
    n_groups, gc = w.shape[1], w.shape[2]
    rev = lambda i: (nt - 1 - i, 0)
    rows = pl.BlockSpec((tm, d_model), rev)

    def body(dx_ref, x_ref, pre_ref, post_ref, d_ref, yu_ref, w_ref, sc_ref,
             dxi_ref, dyu_ref, dsc_ref, dpre_ref, dpost_ref, zbuf, sbuf):
        i = pl.program_id(0)

        @pl.when(i == 0)
        def _():
            zbuf[pl.ds(tm, POOL_HALO), :] = jnp.zeros((POOL_HALO, d_model), F32)
            dsc_ref[...] = jnp.zeros_like(dsc_ref)
            dpre_ref[...] = jnp.zeros_like(dpre_ref)
            dpost_ref[...] = jnp.zeros_like(dpost_ref)

        dxo = dx_ref[...]
        yuv = yu_ref[...]
        sc = sc_ref[...]
        dm, dpost = _rms_bwd(yuv * sc, post_ref[...], dxo)
        dpost_ref[...] += dpost
        dsc_ref[...] += jnp.sum(dm * yuv, axis=0, keepdims=True)
        dyu = dm * sc
        dyu_ref[...] = dyu.astype(dyu_ref.dtype)
        tok = (nt - 1 - i) * tm + lax.broadcasted_iota(jnp.int32, (tm, 1), 0)
        dds = []
        for gi, wnd in enumerate(POOL_WINDOWS):
            cols = pl.ds(gi * gc, gc)
            dd = _mm_tb(dyu[:, gi * gc:(gi + 1) * gc], w_ref[gi])
            cnt = jnp.minimum(tok + 1, wnd).astype(F32)
            zbuf[pl.ds(0, tm), cols] = dd / cnt
            dds.append(dd)
        dhs = []
        for gi, wnd in enumerate(POOL_WINDOWS):
            reach = wnd // 2
            n = tm + POOL_HALO - CONV_HALO * (gi + 1)
            cols = pl.ds(gi * gc, d_model - gi * gc)
            src = zbuf if gi == 0 else sbuf.at[gi - 1]
            level = src[pl.ds(0, n), cols] + src[pl.ds(reach, n), cols]
            if gi + 1 < len(POOL_WINDOWS):
                sbuf[gi, pl.ds(0, n), cols] = level
            dhs.append(level[:tm, :gc] - dds[gi])
        zbuf[pl.ds(tm, POOL_HALO), :] = zbuf[pl.ds(0, POOL_HALO), :]
        dh = jnp.concatenate(dhs, axis=1)
        dxp, dpre = _rms_bwd(x_ref[...], pre_ref[...], dh)
        dpre_ref[...] += dpre
        dxi_ref[...] = dxo + dxp

    vec = jax.ShapeDtypeStruct((1, d_model), F32)
    return pl.pallas_call(
        body, name=name, grid=(nt,),
        out_shape=(jax.ShapeDtypeStruct((t, d_model), F32), jax.ShapeDtypeStruct((t, d_model), _ACT_DTYPE), vec, vec, vec),
        in_specs=[rows, rows, _const((None, 1, d_model), (layer, 0, 0)), _const((None, 1, d_model), (layer, 0, 0)), rows, rows,
                  _const((None, n_groups, gc, gc), (layer, 0, 0, 0)), _const((None, 1, d_model), (layer, 0, 0))],
        out_specs=(rows, rows, _const((1, d_model)), _const((1, d_model)), _const((1, d_model))),
        scratch_shapes=[pltpu.VMEM((tm + POOL_HALO, d_model), F32),
                        pltpu.VMEM((len(POOL_WINDOWS) - 1, tm + POOL_HALO, d_model), F32)],
        compiler_params=_params(),
    )(dx, x, pre_g, post_g, d, yu, w, scale)


def _conv_taps(cw_ref, s):
    return [cw_ref[k, pl.ds(s, 1), :] for k in range(3)]


def _shift_down(v, k, before):
    rolled = pltpu.roll(v, k, 0)
    row = lax.broadcasted_iota(jnp.int32, before.shape, 0)
    head = jnp.where(row < k, pltpu.roll(before, k, 0), rolled[:CONV_HALO])
    return jnp.concatenate([head, rolled[CONV_HALO:]], axis=0)


def _shift_up(v, k, after):
    rows = v.shape[0]
    rolled = pltpu.roll(v, rows - k, 0)
    row = lax.broadcasted_iota(jnp.int32, after.shape, 0)
    tail = jnp.where(row >= CONV_HALO - k, pltpu.roll(after, CONV_HALO - k, 0), rolled[rows - CONV_HALO:])
    return jnp.concatenate([rolled[:rows - CONV_HALO], tail], axis=0)


def _ffn_fwd(x, pre_g, post_g, w_in, conv_w, conv_b, w_out, layer, w_layer, name):
    t, d_model = x.shape
    tm = _row_tile(t)
    fs = w_in.shape[2]
    half = N_DEV // 2

    def body(x_ref, pre_ref, post_ref, win_ref, cw_ref, cb_ref, wout_ref, xo_ref, u_ref, uc_ref, f_ref, carry):
        i = pl.program_id(0)

        @pl.when(i == 0)
        def _():
            carry[...] = jnp.zeros_like(carry)

        xv = x_ref[...]
        hf = _rms_fwd(xv, pre_ref[...]).astype(_MXU_DTYPE)
        f = jnp.zeros((tm, d_model), F32)
        project = lambda b: [_mm_tb(hf, win_ref[s]) for s in (b, b + half)]
        ahead = project(0)
        for b in range(half):
            us, ahead = ahead, project(b + 1) if b + 1 < half else None
            ucs = []
            for s, u in zip((b, b + half), us):
                u_ref[s] = u.astype(u_ref.dtype)
                before = carry[s]
                carry[s] = u[tm - CONV_HALO:]
                w0, w1, w2 = _conv_taps(cw_ref, s)
                uc = ((w0 * _shift_down(u, 2, before) + w1 * _shift_down(u, 1, before)) + w2 * u) + cb_ref[pl.ds(s, 1), :]
                uc_ref[s] = uc.astype(uc_ref.dtype)
                ucs.append(uc)
            gate, val = ucs
            cdf, _ = _gelu_parts(gate)
            f = f + _mm((gate * cdf) * val, wout_ref[pl.ds(b * fs, fs), :])
        f_ref[...] = f
        xo_ref[...] = xv + _rms_fwd(f, post_ref[...])

    tile3 = pl.BlockSpec((N_DEV, tm, fs), lambda i: (0, i, 0))
    saved = jax.ShapeDtypeStruct((N_DEV, t, fs), _SAVE_DTYPE)
    return pl.pallas_call(
        body, name=name, grid=(t // tm,),
        out_shape=(jax.ShapeDtypeStruct((t, d_model), F32), saved, saved, jax.ShapeDtypeStruct((t, d_model), F32)),
        in_specs=[_rows(tm, d_model), _const((None, 1, d_model), (layer, 0, 0)), _const((None, 1, d_model), (layer, 0, 0)),
                  _resident((None, N_DEV, fs, d_model), (w_layer, 0, 0, 0)), _const((None, 3, N_DEV, fs), (layer, 0, 0, 0)),
                  _const((None, N_DEV, fs), (layer, 0, 0)), _resident((None, half * fs, d_model), (w_layer, 0, 0))],
        out_specs=(_rows(tm, d_model), tile3, tile3, _rows(tm, d_model)),
        scratch_shapes=[pltpu.VMEM((N_DEV, CONV_HALO, fs), F32)],
        compiler_params=_params(),
    )(x, pre_g, post_g, w_in, conv_w, conv_b, w_out)


def _ffn_bwd_out(dx, f, post_g, uc, conv_b, w_out, layer, w_layer, name):
    t, d_model = dx.shape
    tm = _row_tile(t)
    nt = t // tm
    fs = uc.shape[2]
    half = N_DEV // 2

    def body(dx_ref, f_ref, post_ref, uc_ref, wout_ref, duc_ref, dwout_ref, dcb_ref, dpost_ref, acc):
        i = pl.program_id(0)

        @pl.when(i == 0)
        def _():
            acc[...] = jnp.zeros_like(acc)
            dcb_ref[...] = jnp.zeros_like(dcb_ref)
            dpost_ref[...] = jnp.zeros_like(dpost_ref)

        df, dpost = _rms_bwd(f_ref[...], post_ref[...], dx_ref[...])
        dpost_ref[...] += dpost
        dfm = df.astype(_MXU_DTYPE)
        project = lambda b: _mm_tb(dfm, wout_ref[pl.ds(b * fs, fs), :])
        ahead = project(0)
        for b in range(half):
            dg, ahead = ahead, project(b + 1) if b + 1 < half else None
            gate = uc_ref[b].astype(F32)
            val = uc_ref[b + half].astype(F32)
            cdf, dgelu = _gelu_parts(gate)
            ge = gate * cdf
            acc[pl.ds(b * fs, fs), :] += _mm_ta(ge * val, dfm)
            for s, dd in ((b, dg * val * dgelu), (b + half, dg * ge)):
                duc_ref[s] = dd.astype(duc_ref.dtype)
                dcb_ref[pl.ds(s, 1), :] += jnp.sum(dd, axis=0, keepdims=True)

        @pl.when(i == nt - 1)
        def _():
            dwout_ref[...] = acc[...].astype(dwout_ref.dtype)

    tile3 = pl.BlockSpec((N_DEV, tm, fs), lambda i: (0, i, 0))
    return pl.pallas_call(
        body, name=name, grid=(nt,),
        out_shape=(jax.ShapeDtypeStruct((N_DEV, t, fs), _SAVE_DTYPE), jax.ShapeDtypeStruct((half * fs, d_model), _WIRE_DTYPE),
                   jax.ShapeDtypeStruct((N_DEV, fs), F32), jax.ShapeDtypeStruct((1, d_model), F32)),
        in_specs=[_rows(tm, d_model), _rows(tm, d_model), _const((None, 1, d_model), (layer, 0, 0)), tile3,
                  _resident((None, half * fs, d_model), (w_layer, 0, 0))],
        out_specs=(tile3, _resident((half * fs, d_model), (0, 0)), _const((N_DEV, fs)), _const((1, d_model))),
        scratch_shapes=[pltpu.VMEM((half * fs, d_model), F32)],
        compiler_params=_params(),
    )(dx, f, post_g, uc, w_out)


def _ffn_bwd_in(dx, x, pre_g, duc, u, conv_w, w_in, layer, w_layer, name):
    t, d_model = dx.shape
    tm = _row_tile(t)
    nt = t // tm
    fs = duc.shape[2]
    hb = SAVE_HALO
    per_tile = tm // hb

    def body(dx_ref, x_ref, pre_ref, duc_ref, dn_ref, u_ref, cw_ref, win_ref, dxi_ref, du_ref, hf_ref, dcw_ref, dpre_ref):
        i = pl.program_id(0)

        @pl.when(i == 0)
        def _():
            dcw_ref[...] = jnp.zeros_like(dcw_ref)
            dpre_ref[...] = jnp.zeros_like(dpre_ref)

        xv = x_ref[...]
        pre = pre_ref[...]
        hf_ref[...] = _rms_fwd(xv, pre).astype(hf_ref.dtype)
        dhf = jnp.zeros((tm, d_model), F32)
        for s in range(N_DEV):
            d0 = duc_ref[s].astype(F32)
            after = jnp.where(i == nt - 1, 0.0, dn_ref[s].astype(F32)[:CONV_HALO])
            d1 = _shift_up(d0, 1, after)
            d2 = _shift_up(d0, 2, after)
            uv = u_ref[s].astype(F32)
            for k, dk in ((2, d0), (1, d1), (0, d2)):
                dcw_ref[k, pl.ds(s, 1), :] += jnp.sum(dk * uv, axis=0, keepdims=True)
            w0, w1, w2 = _conv_taps(cw_ref, s)
            du = (w2 * d0 + w1 * d1 + w0 * d2).astype(_MXU_DTYPE)
            du_ref[s] = du
            dhf = dhf + _mm(du, win_ref[s])
        dxp, dpre = _rms_bwd(xv, pre, dhf)
        dpre_ref[...] += dpre
        dxi_ref[...] = dx_ref[...] + dxp

    tile3 = pl.BlockSpec((N_DEV, tm, fs), lambda i: (0, i, 0))
    return pl.pallas_call(
        body, name=name, grid=(nt,),
        out_shape=(jax.ShapeDtypeStruct((t, d_model), F32), jax.ShapeDtypeStruct((N_DEV, t, fs), _ACT_DTYPE),
                   jax.ShapeDtypeStruct((t, d_model), _ACT_DTYPE), jax.ShapeDtypeStruct((3, N_DEV, fs), F32),
                   jax.ShapeDtypeStruct((1, d_model), F32)),
        in_specs=[_rows(tm, d_model), _rows(tm, d_model), _const((None, 1, d_model), (layer, 0, 0)), tile3,
                  pl.BlockSpec((N_DEV, hb, fs), lambda i: (0, jnp.minimum((i + 1) * per_tile, t // hb - 1), 0)), tile3,
                  _const((None, 3, N_DEV, fs), (layer, 0, 0, 0)), _resident((None, N_DEV, fs, d_model), (w_layer, 0, 0, 0))],
        out_specs=(_rows(tm, d_model), tile3, _rows(tm, d_model), _const((3, N_DEV, fs)), _const((1, d_model))),
        compiler_params=_params(),
    )(dx, x, pre_g, duc, duc, u, conv_w, w_in)


def _tn_matmul(a, b, a_spec, b_spec, n_out, m, n, name, out_dtype):
    def body(a_ref, b_ref, o_ref):
        o_ref[...] = _mm_ta(a_ref[...], b_ref[...]).astype(o_ref.dtype)

    return pl.pallas_call(
        body, name=name, grid=(n_out,),
        out_shape=jax.ShapeDtypeStruct((n_out, m, n), out_dtype),
        in_specs=[a_spec, b_spec],
        out_specs=pl.BlockSpec((None, m, n), lambda c: (c, 0, 0)),
        compiler_params=_params(),
    )(a, b)


def _kv_fwd(x, kv_g, w_kv, cos, ssin, name):
    t, d_model = x.shape
    tm = _row_tile(t, PROJ_TILE)
    kvd = w_kv.shape[1] // 2
    pairs = kvd // LANES

    def body(x_ref, g_ref, w_ref, cos_ref, sin_ref, k_ref, v_ref):
        kv = _mm(_rms_fwd(x_ref[...], g_ref[...]), w_ref[...])
        low = _low_half((tm, LANES))
        for j in range(pairs):
            kb = _rope_fwd(kv[:, j * LANES:(j + 1) * LANES], cos_ref[...], sin_ref[...])
            vb = kv[:, kvd + j * LANES:kvd + (j + 1) * LANES]
            for blk, ref in ((kb, k_ref), (vb, v_ref)):
                sw = pltpu.roll(blk, 64, 1)
                ref[2 * j] = jnp.where(low, blk, sw).astype(ref.dtype)
                ref[2 * j + 1] = jnp.where(low, sw, blk).astype(ref.dtype)

    heads = jax.ShapeDtypeStruct((N_KV_HEADS, t, LANES), _ACT_DTYPE)
    hspec = pl.BlockSpec((N_KV_HEADS, tm, LANES), lambda i: (0, i, 0))
    return pl.pallas_call(
        body, name=name, grid=(t // tm,), out_shape=(heads, heads),
        in_specs=[_rows(tm, d_model), _const((1, d_model)), _const(w_kv.shape), _rows(tm, LANES), _rows(tm, LANES)],
        out_specs=(hspec, hspec), compiler_params=_params(),
    )(x, kv_g, w_kv, cos, ssin)


def _kv_bwd(dx, x, kv_g, w_kv, cos, ssin, dks, dvs, name):
    t, d_model = x.shape
    tm = _row_tile(t, PROJ_TILE)
    nt = t // tm
    kvd = w_kv.shape[1] // 2
    pairs = kvd // LANES
    n_users = len(dks)

    def body(dx_ref, x_ref, g_ref, w_ref, cos_ref, sin_ref, *refs):
        dk_refs, dv_refs = refs[:n_users], refs[n_users:2 * n_users]
        dxi_ref, dw_ref, dg_ref, acc = refs[2 * n_users:]
        dk_ref = functools.reduce(lambda a, b: a + b, [r[...] for r in dk_refs])
        dv_ref = functools.reduce(lambda a, b: a + b, [r[...] for r in dv_refs])
        i = pl.program_id(0)

        @pl.when(i == 0)
        def _():
            dg_ref[...] = jnp.zeros_like(dg_ref)
            acc[...] = jnp.zeros_like(acc)

        xv = x_ref[...]
        g = g_ref[...]
        low = _low_half((tm, LANES))
        dks, dvs = [], []
        for j in range(pairs):
            dkb = jnp.where(low, dk_ref[2 * j], dk_ref[2 * j + 1])
            dks.append(_rope_bwd(dkb, cos_ref[...], sin_ref[...]))
            dvs.append(jnp.where(low, dv_ref[2 * j], dv_ref[2 * j + 1]))
        dkv = jnp.concatenate(dks + dvs, axis=1).astype(_MXU_DTYPE)
        acc[...] += _mm_ta(_rms_fwd(xv, g), dkv)
        dxp, dg = _rms_bwd(xv, g, _mm_tb(dkv, w_ref[...]))
        dg_ref[...] += dg
        dxi_ref[...] = dx_ref[...] + dxp

        @pl.when(i == nt - 1)
        def _():
            dw_ref[...] = acc[...].astype(dw_ref.dtype)

    hspec = pl.BlockSpec((N_KV_HEADS, tm, LANES), lambda i: (0, i, 0))
    return pl.pallas_call(
        body, name=name, grid=(nt,),
        out_shape=(jax.ShapeDtypeStruct((t, d_model), F32), jax.ShapeDtypeStruct(w_kv.shape, _WIRE_DTYPE),
                   jax.ShapeDtypeStruct((1, d_model), F32)),
        in_specs=[_rows(tm, d_model), _rows(tm, d_model), _const((1, d_model)), _const(w_kv.shape),
                  _rows(tm, LANES), _rows(tm, LANES)] + [hspec] * (2 * n_users),
        out_specs=(_rows(tm, d_model), _resident(w_kv.shape, (0, 0)), _const((1, d_model))),
        scratch_shapes=[pltpu.VMEM(w_kv.shape, F32)],
        compiler_params=_params(),
    )(dx, x, kv_g, w_kv, cos, ssin, *dks, *dvs)


def _q_fwd(x, pre_g, w_q, cos, ssin, layer, j, name):
    t, d_model = x.shape
    tm = _row_tile(t, PROJ_TILE)

    def body(x_ref, g_ref, w_ref, cos_ref, sin_ref, q_ref):
        q = _mm(_rms_fwd(x_ref[...], g_ref[...]), w_ref[...])
        for p in range(d_model // LANES):
            cols = slice(p * LANES, (p + 1) * LANES)
            q_ref[:, cols] = (_rope_fwd(q[:, cols], cos_ref[...], sin_ref[...]) * ATTN_SCALE).astype(q_ref.dtype)

    return pl.pallas_call(
        body, name=name, grid=(t // tm,), out_shape=jax.ShapeDtypeStruct((t, d_model), _ACT_DTYPE),
        in_specs=[_rows(tm, d_model), _const((None, 1, d_model), (layer, 0, 0)), _const((None, d_model, d_model), (j, 0, 0)),
                  _rows(tm, LANES), _rows(tm, LANES)],
        out_specs=_rows(tm, d_model), compiler_params=_params(),
    )(x, pre_g, w_q, cos, ssin)


def _q_bwd(dx, dqs, x, pre_g, w_q, cos, ssin, layer, j, name):
    t, d_model = x.shape
    tm = _row_tile(t, PROJ_TILE)
    nt = t // tm

    def body(dx_ref, dq_ref, x_ref, g_ref, w_ref, cos_ref, sin_ref, dxi_ref, dw_ref, dg_ref, acc):
        i = pl.program_id(0)

        @pl.when(i == 0)
        def _():
            dg_ref[...] = jnp.zeros_like(dg_ref)
            acc[...] = jnp.zeros_like(acc)

        xv = x_ref[...]
        g = g_ref[...]
        parts = []
        for p in range(d_model // LANES):
            cols = slice(p * LANES, (p + 1) * LANES)
            parts.append(_rope_bwd(dq_ref[:, cols] * ATTN_SCALE, cos_ref[...], sin_ref[...]))
        dq = jnp.concatenate(parts, axis=1).astype(_MXU_DTYPE)
        acc[...] += _mm_ta(_rms_fwd(xv, g), dq)
        dxp, dg = _rms_bwd(xv, g, _mm_tb(dq, w_ref[...]))
        dg_ref[...] += dg
        dxi_ref[...] = dx_ref[...] + dxp

        @pl.when(i == nt - 1)
        def _():
            dw_ref[...] = acc[...].astype(dw_ref.dtype)

    return pl.pallas_call(
        body, name=name, grid=(nt,),
        out_shape=(jax.ShapeDtypeStruct((t, d_model), F32), jax.ShapeDtypeStruct((d_model, d_model), _WIRE_DTYPE),
                   jax.ShapeDtypeStruct((1, d_model), F32)),
        in_specs=[_rows(tm, d_model), _rows(tm, d_model), _rows(tm, d_model), _const((None, 1, d_model), (layer, 0, 0)),
                  _const((None, d_model, d_model), (j, 0, 0)), _rows(tm, LANES), _rows(tm, LANES)],
        out_specs=(_rows(tm, d_model), _resident((d_model, d_model), (0, 0)), _const((1, d_model))),
        scratch_shapes=[pltpu.VMEM((d_model, d_model), F32)],
        compiler_params=_params(),
    )(dx, dqs, x, pre_g, w_q, cos, ssin)


def _stack_heads(pairs):
    low = _low_half(pairs[0].shape)
    zero = jnp.zeros_like(pairs[0])
    return jnp.concatenate([h for blk in pairs for h in (jnp.where(low, blk, zero), jnp.where(low, zero, blk))], axis=0)


def _unstack_heads(stacked, i):
    a, b = stacked[2 * i * BLOCK:(2 * i + 1) * BLOCK], stacked[(2 * i + 1) * BLOCK:(2 * i + 2) * BLOCK]
    return jnp.where(_low_half(a.shape), a, b)


def _attn_scores(q_pairs, k2, n, sinks):
    qst = _stack_heads(q_pairs)
    s = _mm_tb(qst, k2)
    row = lax.broadcasted_iota(jnp.int32, s.shape, 0)
    col = lax.broadcasted_iota(jnp.int32, s.shape, 1)
    rel = BLOCK + (row & (BLOCK - 1)) - col
    valid = (rel >= 0) & (rel < WINDOW) & (n * BLOCK + col - BLOCK >= 0)
    s = jnp.where(valid, s, NEG_INF)
    rows1 = lax.broadcasted_iota(jnp.int32, (s.shape[0], 1), 0)
    sink = jnp.full((s.shape[0], 1), sinks[-1], F32)
    for i in reversed(range(len(sinks) - 1)):
        sink = jnp.where(rows1 < (i + 1) * BLOCK, sinks[i], sink)
    return qst, s, sink


def _attn_fwd(qs, kdup, vdup, sinks, j, name):
    t, d_model = qs.shape
    nb = t // BLOCK
    n_pairs = d_model // LANES
    per_group = n_pairs // N_KV_HEADS

    def body(sink_ref, q_ref, kp_ref, ko_ref, vp_ref, vo_ref, o_ref, lse_ref):
        n = pl.program_id(0)
        lane = _lane_iota((BLOCK, LANES))
        lse = jnp.zeros((BLOCK, LANES), F32)
        for hk in range(N_KV_HEADS):
            pairs = range(hk * per_group, (hk + 1) * per_group)
            heads = range(2 * pairs[0], 2 * pairs[-1] + 2)
            k2 = jnp.concatenate([kp_ref[hk], ko_ref[hk]], axis=0)
            v2 = jnp.concatenate([vp_ref[hk], vo_ref[hk]], axis=0)
            _, s, sink = _attn_scores([q_ref[:, p * LANES:(p + 1) * LANES] for p in pairs], k2, n,
                                      [sink_ref[j, h] for h in heads])
            m = jnp.maximum(jnp.max(s, axis=-1, keepdims=True), sink)
            pe = jnp.exp(s - m)
            denom = jnp.sum(pe, axis=-1, keepdims=True) + jnp.exp(sink - m)
            o2 = _mm(pe, v2) / denom
            l2 = m + jnp.log(denom)
            for i, p in enumerate(pairs):
                o_ref[:, p * LANES:(p + 1) * LANES] = _unstack_heads(o2, i).astype(o_ref.dtype)
            for i, h in enumerate(heads):
                lse = jnp.where(lane == h, l2[i * BLOCK:(i + 1) * BLOCK], lse)
        lse_ref[...] = lse

    prev = pl.BlockSpec((N_KV_HEADS, BLOCK, LANES), lambda n: (0, jnp.maximum(n - 1, 0), 0))
    own = pl.BlockSpec((N_KV_HEADS, BLOCK, LANES), lambda n: (0, n, 0))
    return pl.pallas_call(
        body, name=name, grid=(nb,),
        out_shape=(jax.ShapeDtypeStruct((t, d_model), _ACT_DTYPE), jax.ShapeDtypeStruct((t, LANES), F32)),
        in_specs=[pl.BlockSpec(memory_space=pltpu.SMEM), _rows(BLOCK, d_model), prev, own, prev, own],
        out_specs=(_rows(BLOCK, d_model), _rows(BLOCK, LANES)), compiler_params=_params(),
    )(sinks, qs, kdup, kdup, vdup, vdup)


def _attn_bwd(qs, kdup, vdup, sinks, lse, do, j, name):
    t, d_model = qs.shape
    nb = t // BLOCK
    n_pairs = d_model // LANES
    per_group = n_pairs // N_KV_HEADS
    rev = lambda n: nb - 1 - n

    def body(sink_ref, q_ref, kp_ref, ko_ref, vp_ref, vo_ref, lse_ref, do_ref, dq_ref, dk_ref, dv_ref, ds_ref, ck, cv):
        i = pl.program_id(0)
        n = nb - 1 - i

        @pl.when(i == 0)
        def _():
            ck[...] = jnp.zeros_like(ck)
            cv[...] = jnp.zeros_like(cv)
            ds_ref[...] = jnp.zeros_like(ds_ref)

        lane = _lane_iota((BLOCK, LANES))
        lane1 = _lane_iota((1, LANES))
        lsev = lse_ref[...]
        dsink = jnp.zeros((1, LANES), F32)
        for hk in range(N_KV_HEADS):
            pairs = range(hk * per_group, (hk + 1) * per_group)
            heads = range(2 * pairs[0], 2 * pairs[-1] + 2)
            k2 = jnp.concatenate([kp_ref[hk], ko_ref[hk]], axis=0)
            v2 = jnp.concatenate([vp_ref[hk], vo_ref[hk]], axis=0)
            qst, s, sink = _attn_scores([q_ref[:, p * LANES:(p + 1) * LANES] for p in pairs], k2, n,
                                        [sink_ref[j, h] for h in heads])
            l2 = jnp.concatenate([jnp.sum(jnp.where(lane == h, lsev, 0.0), axis=-1, keepdims=True) for h in heads], axis=0)
            pn = jnp.exp(s - l2)
            dost = _stack_heads([do_ref[:, p * LANES:(p + 1) * LANES] for p in pairs])
            dp = _mm_tb(dost, v2)
            dr = jnp.sum(pn * dp, axis=-1, keepdims=True)
            dsm = (pn * (dp - dr)).astype(_MXU_DTYPE)
            dsk = -jnp.exp(sink - l2) * dr
            for i, h in enumerate(heads):
                dsink = dsink + jnp.where(lane1 == h, jnp.sum(dsk[i * BLOCK:(i + 1) * BLOCK]), 0.0)
            dq2 = _mm(dsm, k2)
            for i, p in enumerate(pairs):
                dq_ref[:, p * LANES:(p + 1) * LANES] = _unstack_heads(dq2, i)
            for acc, carry, ref in ((_mm_ta(dsm, qst), ck, dk_ref), (_mm_ta(pn, dost), cv, dv_ref)):
                folded = acc + pltpu.roll(acc, 64, 1)
                ref[hk] = folded[BLOCK:] + carry[hk]
                carry[hk] = folded[:BLOCK]
        ds_ref[...] += dsink

    prev = pl.BlockSpec((N_KV_HEADS, BLOCK, LANES), lambda n: (0, jnp.maximum(rev(n) - 1, 0), 0))
    own = pl.BlockSpec((N_KV_HEADS, BLOCK, LANES), lambda n: (0, rev(n), 0))
    rows = lambda cols: pl.BlockSpec((BLOCK, cols), lambda n: (rev(n), 0))
    heads = jax.ShapeDtypeStruct((N_KV_HEADS, t, LANES), F32)
    return pl.pallas_call(
        body, name=name, grid=(nb,),
        out_shape=(jax.ShapeDtypeStruct((t, d_model), F32), heads, heads, jax.ShapeDtypeStruct((1, LANES), F32)),
        in_specs=[pl.BlockSpec(memory_space=pltpu.SMEM), rows(d_model), prev, own, prev, own, rows(LANES), rows(d_model)],
        out_specs=(rows(d_model), own, own, _const((1, LANES))),
        scratch_shapes=[pltpu.VMEM((N_KV_HEADS, BLOCK, LANES), F32), pltpu.VMEM((N_KV_HEADS, BLOCK, LANES), F32)],
        compiler_params=_params(),
    )(sinks, qs, kdup, kdup, vdup, vdup, lse, do)


def _oproj_fwd(x, o, w_o, post_g, layer, j, name):
    t, d_model = x.shape
    tm = _row_tile(t, PROJ_TILE)

    def body(x_ref, o_ref, w_ref, g_ref, xo_ref, mo_ref):
        mo = _mm(o_ref[...], w_ref[...])
        mo_ref[...] = mo
        xo_ref[...] = x_ref[...] + _rms_fwd(mo, g_ref[...])

    full = jax.ShapeDtypeStruct((t, d_model), F32)
    return pl.pallas_call(
        body, name=name, grid=(t // tm,), out_shape=(full, full),
        in_specs=[_rows(tm, d_model), _rows(tm, d_model), _const((None, d_model, d_model), (j, 0, 0)),
                  _const((None, 1, d_model), (layer, 0, 0))],
        out_specs=(_rows(tm, d_model), _rows(tm, d_model)), compiler_params=_params(),
    )(x, o, w_o, post_g)


def _oproj_bwd(dx, mo, o, w_o, post_g, layer, j, name):
    t, d_model = dx.shape
    tm = _row_tile(t, PROJ_TILE)
    nt = t // tm

    def body(dx_ref, mo_ref, o_ref, w_ref, g_ref, do_ref, dw_ref, dg_ref, acc):
        i = pl.program_id(0)

        @pl.when(i == 0)
        def _():
            dg_ref[...] = jnp.zeros_like(dg_ref)
            acc[...] = jnp.zeros_like(acc)

        dmo, dg = _rms_bwd(mo_ref[...], g_ref[...], dx_ref[...])
        dg_ref[...] += dg
        dmo = dmo.astype(_MXU_DTYPE)
        acc[...] += _mm_ta(o_ref[...], dmo)
        do_ref[...] = _mm_tb(dmo, w_ref[...]).astype(do_ref.dtype)

        @pl.when(i == nt - 1)
        def _():
            dw_ref[...] = acc[...].astype(dw_ref.dtype)

    return pl.pallas_call(
        body, name=name, grid=(nt,),
        out_shape=(jax.ShapeDtypeStruct((t, d_model), _ACT_DTYPE), jax.ShapeDtypeStruct((d_model, d_model), _WIRE_DTYPE),
                   jax.ShapeDtypeStruct((1, d_model), F32)),
        in_specs=[_rows(tm, d_model), _rows(tm, d_model), _rows(tm, d_model), _const((None, d_model, d_model), (j, 0, 0)),
                  _const((None, 1, d_model), (layer, 0, 0))],
        out_specs=(_rows(tm, d_model), _resident((d_model, d_model), (0, 0)), _const((1, d_model))),
        scratch_shapes=[pltpu.VMEM((d_model, d_model), F32)],
        compiler_params=_params(),
    )(dx, mo, o, w_o, post_g)


def _loss_grad(y, target, name):
    t, d_model = y.shape
    tm = _row_tile(t, PROJ_TILE)

    def body(y_ref, t_ref, dy_ref, loss_ref):
        i = pl.program_id(0)

        @pl.when(i == 0)
        def _():
            loss_ref[...] = jnp.zeros_like(loss_ref)

        err = y_ref[...] - t_ref[...]
        dy_ref[...] = err / d_model
        loss_ref[...] += 0.5 * jnp.sum(jnp.mean(err * err, axis=-1, keepdims=True), axis=0, keepdims=True)

    return pl.pallas_call(
        body, name=name, grid=(t // tm,),
        out_shape=(jax.ShapeDtypeStruct((t, d_model), F32), jax.ShapeDtypeStruct((1, 1), F32)),
        in_specs=[_rows(tm, d_model), _rows(tm, d_model)], out_specs=(_rows(tm, d_model), _const((1, 1))),
        compiler_params=_params(),
    )(y, target)


def _mesh_position():
    return lax.axis_index("x"), lax.axis_index("y"), lax.axis_index("c")


def _block_of(px, py, pc):
    return 4 * px + 2 * py + pc


def _at_block(ref, axis, block):
    return ref.at[(slice(None),) * axis + (block,)]


def _all_gather(shards, axes, name):
    n = len(shards)

    def body(*refs):
        srcs, outs = refs[:n], refs[n:2 * n]
        send_sems, recv_sems, local_sems = refs[2 * n:]
        x, y, c = _mesh_position()
        me, sibling = (x, y, c), (x, y, 1 - c)
        chips = [(1 - x, y), (x, 1 - y), (1 - x, 1 - y)]

        def blk(i, pos):
            return _at_block(outs[i], axes[i], _block_of(*pos))

        def copy(i, k, block, to, src=None):
            return pltpu.make_async_remote_copy(
                src_ref=blk(i, block) if src is None else src, dst_ref=blk(i, block),
                send_sem=send_sems.at[i, k], recv_sem=recv_sems.at[i, k], device_id=to, device_id_type=MESH)

        mine = [pltpu.make_async_copy(srcs[i], blk(i, me), local_sems.at[i]) for i in range(n)]
        for cp in mine:
            cp.start()
        sent = []
        for i in range(n):
            sent += [copy(i, 1 + k, me, (*chip, c), src=srcs[i]) for k, chip in enumerate(chips)]
            sent.append(copy(i, 0, me, sibling, src=srcs[i]))
        for cp in sent:
            cp.start()
        for i in range(n):
            for k, chip in enumerate(chips):
                copy(i, 1 + k, (*chip, c), me).wait_recv()
                passed = copy(i, 4 + k, (*chip, c), sibling)
                passed.start()
                sent.append(passed)
        for i in range(n):
            copy(i, 0, sibling, me).wait_recv()
            for k, chip in enumerate(chips):
                copy(i, 4 + k, (*chip, 1 - c), me).wait_recv()
        for cp in sent:
            cp.wait_send()
        for cp in mine:
            cp.wait()

    hbm = pl.BlockSpec(memory_space=pl.ANY)
    return pl.pallas_call(
        body, name=name,
        out_shape=tuple(jax.ShapeDtypeStruct(s.shape[:a] + (N_DEV,) + s.shape[a:], s.dtype) for s, a in zip(shards, axes)),
        in_specs=[hbm] * n, out_specs=(hbm,) * n,
        scratch_shapes=[pltpu.SemaphoreType.DMA((n, 7)), pltpu.SemaphoreType.DMA((n, 7)), pltpu.SemaphoreType.DMA((n,))],
    )(*shards)


GATHER, SCATTER, GATHER_CHIPS, GATHER_SIBLING = "gather", "scatter", "gather_chips", "gather_sibling"
COPIES = {GATHER: N_DEV - 1, SCATTER: N_DEV - 1, GATHER_CHIPS: 4, GATHER_SIBLING: 3}


def _land_shape(kind, s, axis):
    if kind == SCATTER:
        return (N_DEV,) + s.shape[:axis] + s.shape[axis + 1:]
    return s.shape[:axis] + (N_DEV,) + s.shape[axis:]


def _plan(kind, srcs, lands, axes):
    x, y, c = _mesh_position()
    my_block = _block_of(x, y, c)
    flips = {GATHER_CHIPS: (1, 4, 2, 6), GATHER_SIBLING: (4, 2, 6)}.get(kind, range(1, N_DEV))
    others = [(1 - x if k & 4 else x, 1 - y if k & 2 else y, 1 - c if k & 1 else c) for k in flips]
    remote = []
    for src, land, axis in zip(srcs, lands, axes):
        if kind == SCATTER:
            mine = land.at[my_block]
            remote += [(_at_block(src, axis, _block_of(*peer)), mine, peer, land.at[_block_of(*peer)]) for peer in others]
        elif kind == GATHER_SIBLING:
            for px, py, _ in others:
                mine, theirs = _at_block(land, axis, _block_of(px, py, c)), _at_block(land, axis, _block_of(px, py, 1 - c))
                remote.append((mine, mine, (x, y, 1 - c), theirs))
        else:
            mine = _at_block(land, axis, my_block)
            remote += [(src, mine, peer, _at_block(land, axis, _block_of(*peer))) for peer in others]
    return remote


def _remote(src, dst, send_sems, recv_sems, k, peer):
    return pltpu.make_async_remote_copy(src_ref=src, dst_ref=dst, send_sem=send_sems.at[k], recv_sem=recv_sems.at[k],
                                        device_id=peer, device_id_type=MESH)


_HBM = pl.BlockSpec(memory_space=pltpu.HBM)
_SEM = pl.BlockSpec(memory_space=pltpu.SEMAPHORE)
_SPLIT = dict(has_side_effects=pltpu.SideEffectType.DATAFLOW_SIDE_EFFECTING)


def _landing_zone(kind, s, axis, me):
    land = lax.empty(_land_shape(kind, s, axis), s.dtype)
    if kind == SCATTER:
        return lax.dynamic_update_slice_in_dim(land, lax.dynamic_slice_in_dim(s, me, 1, axis).reshape((1,) + land.shape[1:]), me, 0)
    return lax.dynamic_update_slice_in_dim(land, jnp.expand_dims(s, axis), me, axis)


def _exchange_start(kind, arrays, axes, after, name):
    n = len(arrays)
    if kind == GATHER_SIBLING:
        passed = list(arrays)
    else:
        me = _block_of(*_mesh_position())
        passed = list(arrays) + [_landing_zone(kind, s, a, me) for s, a in zip(arrays, axes)]
    n_sems = n * COPIES[kind]

    def body(*refs):
        land_refs = refs[len(passed) - n:len(passed)]
        send_sems, recv_sems = refs[len(passed) + 1], refs[len(passed) + 2]
        token = refs[-1]
        for k, (src, dst, peer, _) in enumerate(_plan(kind, refs[:n], land_refs, axes)):
            _remote(src, dst, send_sems, recv_sems, k, peer).start()
        token[...] = jnp.zeros_like(token)

    out = pl.pallas_call(
        body, name=name,
        out_shape=(pltpu.SemaphoreType.DMA((n_sems,)), pltpu.SemaphoreType.DMA((n_sems,)),
                   *[pltpu.HBM(a.shape, a.dtype) for a in passed], jax.ShapeDtypeStruct((8, LANES), F32)),
        in_specs=[_HBM] * len(passed) + [pl.BlockSpec(memory_space=pl.ANY)],
        out_specs=(_SEM, _SEM, *[_HBM] * len(passed), pl.BlockSpec(memory_space=pltpu.VMEM)),
        input_output_aliases={i: 2 + i for i in range(len(passed))},
        compiler_params=pltpu.CompilerParams(**_SPLIT),
    )(*[pltpu.with_memory_space_constraint(a, pltpu.HBM) for a in passed], after)
    return (kind, axes, n, out[:-1]), out[-1]


def _exchange_wait(handle, after, name):
    kind, axes, n, (send_sems, recv_sems, *thru) = handle

    def body(*refs):
        land_refs = refs[len(thru) - n:len(thru)]
        send_sems, recv_sems = refs[len(thru)], refs[len(thru) + 1]
        for k, (src, _, peer, arrives) in enumerate(_plan(kind, refs[:n], land_refs, axes)):
            cp = _remote(src, arrives, send_sems, recv_sems, k, peer)
            cp.wait_send()
            cp.wait_recv()

    out = pl.pallas_call(
        body, name=name,
        out_shape=tuple(pltpu.HBM(a.shape, a.dtype) for a in thru),
        in_specs=[_HBM] * len(thru) + [_SEM, _SEM, pl.BlockSpec(memory_space=pl.ANY)], out_specs=(_HBM,) * len(thru),
        input_output_aliases={i: i for i in range(len(thru))},
        compiler_params=pltpu.CompilerParams(**_SPLIT),
    )(*thru, send_sems, recv_sems, after)
    return out[len(thru) - n:]


def _adamw_math(w, g, m, v):
    m = ADAM_B1 * m + (1.0 - ADAM_B1) * g
    v = ADAM_B2 * v + (1.0 - ADAM_B2) * jnp.square(g)
    m_hat = m / (1.0 - ADAM_B1 ** ADAM_STEP)
    v_hat = v / (1.0 - ADAM_B2 ** ADAM_STEP)
    delta = -ADAM_LR * (m_hat / (jnp.sqrt(v_hat) + ADAM_EPS) + ADAM_WD * w)
    return delta, m, v


def _update_tile(rows):
    if rows <= 512:
        return rows
    for tr in (512, 384, 352, 256, 176, 128, 64, 32, 16):
        if rows % tr == 0:
            return tr
    raise ValueError(f"{rows} rows do not tile")


def _adamw(parts, w, m, v, slab, so_far, name):
    rows, c = w.shape
    r = parts.shape[1]
    tr = _update_tile(r)
    first = slab * (r // tr)
    if so_far is None:
        so_far = tuple(lax.empty((rows, c), F32) for _ in range(4))

    def body(p_ref, w_ref, m_ref, v_ref, *refs):
        g_ref, d_ref, mo_ref, vo_ref = refs[4:]
        g = p_ref[0].astype(F32)
        for s in range(1, N_DEV):
            g = g + p_ref[s].astype(F32)
        g_ref[...] = g
        d_ref[...], mo_ref[...], vo_ref[...] = _adamw_math(w_ref[...], g, m_ref[...], v_ref[...])

    out = jax.ShapeDtypeStruct((rows, c), F32)
    tile = pl.BlockSpec((tr, c), lambda i: (first + i, 0))
    return pl.pallas_call(
        body, name=name, grid=(r // tr,), out_shape=(out,) * 4,
        in_specs=[pl.BlockSpec((N_DEV, tr, c), lambda i: (0, i, 0))] + [tile] * 3 + [pl.BlockSpec(memory_space=pl.ANY)] * 4,
        out_specs=(tile,) * 4, input_output_aliases={4 + k: k for k in range(4)}, compiler_params=_params(),
    )(parts, w, m, v, *so_far)


def _adamw_small(parts, picks, weights, name):
    n = len(parts)

    def body(*refs):
        p_refs, wmv, outs = refs[:n], refs[n:4 * n], refs[4 * n:]
        me = _block_of(*_mesh_position())
        for i in range(n):
            g = picks[i](p_refs[i], 0, me)
            for s in range(1, N_DEV):
                g = g + picks[i](p_refs[i], s, me)
            w_ref, m_ref, v_ref = wmv[3 * i:3 * i + 3]
            g_ref, d_ref, mo_ref, vo_ref = outs[4 * i:4 * i + 4]
            g_ref[...] = g
            d_ref[...], mo_ref[...], vo_ref[...] = _adamw_math(w_ref[...], g, m_ref[...], v_ref[...])

    flat = [a for wmv in weights for a in wmv]
    out = pl.pallas_call(
        body, name=name,
        out_shape=tuple(jax.ShapeDtypeStruct(w.shape, F32) for w, _, _ in weights for _ in range(4)),
        compiler_params=pltpu.CompilerParams(vmem_limit_bytes=VMEM_LIMIT),
    )(*parts, *flat)
    return [tuple(out[4 * i:4 * i + 4]) for i in range(n)]


def kernel(x, positions, mix_pre_g, mix_post_g, pool_w, pool_scale, kv_norm_g, w_kv, w_q, w_o, sinks, ffn_pre_g, ffn_post_g, ffn_w_in, ffn_conv_w, ffn_conv_b, ffn_w_out, loss_target, m_mix_pre_g, m_mix_post_g, m_pool_w, m_pool_scale, m_kv_norm_g, m_w_kv, m_w_q, m_w_o, m_sinks, m_ffn_pre_g, m_ffn_post_g, m_ffn_w_in, m_ffn_conv_w, m_ffn_conv_b, m_ffn_w_out, v_mix_pre_g, v_mix_post_g, v_pool_w, v_pool_scale, v_kv_norm_g, v_w_kv, v_w_q, v_w_o, v_sinks, v_ffn_pre_g, v_ffn_post_g, v_ffn_w_in, v_ffn_conv_w, v_ffn_conv_b, v_ffn_w_out):
    depth, d_model = mix_pre_g.shape
    n_a = pool_w.shape[0]
    n_b = w_q.shape[0]
    t = x.shape[1]
    fs = ffn_w_in.shape[2]
    half = N_DEV // 2
    n_heads = d_model // HEAD_DIM
    x0 = x.reshape(t, d_model)
    target = loss_target.reshape(t, d_model)

    inv_freq = 1.0 / (ROPE_THETA ** (jnp.arange(0, HEAD_DIM, 2, dtype=F32) / HEAD_DIM))
    ang = positions.reshape(t).astype(F32)[:, None] * inv_freq
    cos, sin = jnp.cos(ang), jnp.sin(ang)
    cos = jnp.tile(cos, (1, 2 * LANES // HEAD_DIM))
    ssin = jnp.tile(jnp.concatenate([-sin, sin], axis=1), (1, LANES // HEAD_DIM))

    wire = lambda a: a.astype(_WIRE_DTYPE)
    by_hidden = lambda a: a.transpose(0, 2, 1)
    w_in_b, w_out_b = wire(by_hidden(ffn_w_in)), wire(ffn_w_out)
    pool_w_g, w_in_0, w_out_0, pool_scale_g, conv_w_g = _all_gather(
        [wire(pool_w), w_in_b[:1], w_out_b[:1], pool_scale, ffn_conv_w], [2, 1, 1, 0, 0], "gather_first")
    groups = []
    for l in range(1, depth):
        if l == n_a:
            groups.append(("attn", [wire(w_kv), wire(w_q), wire(w_o)], [0, 1, 1], l))
        groups.append((l, [w_in_b[l:l + 1], w_out_b[l:l + 1]], [1, 1], l))
    over_ici, to_sibling, tokens, after = {}, {}, [], w_in_0
    for key, shards, axes, _ in groups:
        over_ici[key], after = _exchange_start(GATHER_CHIPS, shards, axes, after, f"gather_chips_{key}")
        tokens.append(after)
    started = functools.reduce(lambda a, b: a + b, [tk[0, 0] for tk in tokens])

    def pass_on(layer, after):
        sent = jnp.zeros((), F32)
        for key, _, axes, first in groups:
            if first == layer:
                lands = _exchange_wait(over_ici[key], after, f"gather_chips_wait_{key}")
                to_sibling[key], tk = _exchange_start(GATHER_SIBLING, lands, axes, after, f"gather_sibling_{key}")
                sent = sent + tk[0, 0]
        return sent

    w_in_l, w_out_l = {0: w_in_0}, {0: w_out_0.reshape(1, half * fs, d_model)}
    pool_scale_f = pool_scale_g.transpose(1, 0, 2).reshape(n_a, 1, d_model)
    conv_w_f = conv_w_g.transpose(1, 2, 0, 3)
    pool_w_f = pool_w_g.reshape(n_a, len(POOL_WINDOWS), d_model // len(POOL_WINDOWS), -1)
    conv_b_f = ffn_conv_b.reshape(depth, N_DEV, fs)
    g3 = lambda a: a.reshape(a.shape[0], 1, a.shape[1])
    mix_pre, mix_post, ffn_pre, ffn_post = g3(mix_pre_g) + started, g3(mix_post_g), g3(ffn_pre_g), g3(ffn_post_g)
    kv_g = kv_norm_g.reshape(1, d_model)
    w_kv_f = w_q_f = w_o_f = None

    saved = []
    xc = x0
    kdup = vdup = x_kv = None
    for l in range(depth):
        x_in = xc
        if l < n_a:
            x_mid, dsave, yu = _pool_fwd(x_in, mix_pre, mix_post, pool_w_f, pool_scale_f, l, f"pool_fwd_{l}")
            mixer = (dsave, yu)
        else:
            j = l - n_a
            if j == 0:
                x_kv = x_in
                w_kv_g, w_q_g, w_o_g = _exchange_wait(to_sibling["attn"], x_in, "gather_sibling_wait_attn")
                w_kv_f = w_kv_g.reshape(d_model, -1)
                w_q_f = w_q_g.reshape(n_b, d_model, d_model)
                w_o_f = w_o_g.reshape(n_b, d_model, d_model)
                kdup, vdup = _kv_fwd(x_kv, kv_g, w_kv_f, cos, ssin, "kv_fwd")
            qs = _q_fwd(x_in, mix_pre, w_q_f, cos, ssin, l, j, f"q_fwd_{l}")
            o, lse = _attn_fwd(qs, kdup, vdup, sinks, j, f"attn_fwd_{l}")
            x_mid, mo = _oproj_fwd(x_in, o, w_o_f, mix_post, l, j, f"oproj_fwd_{l}")
            mixer = (qs, o, lse, mo)
        if l > 0:
            w_in_l[l], w_out_g = _exchange_wait(to_sibling[l], x_mid, f"gather_sibling_wait_{l}")
            w_out_l[l] = w_out_g.reshape(1, half * fs, d_model)
        xc, u, uc, f = _ffn_fwd(x_mid, ffn_pre, ffn_post, w_in_l[l], conv_w_f, conv_b_f, w_out_l[l], l, 0, f"ffn_fwd_{l}")
        saved.append((x_in, x_mid, u, uc, f, mixer))
        if l + 1 < depth:
            mix_pre = mix_pre + pass_on(l + 1, xc)

    dx, loss_part = _loss_grad(xc, target, "loss")

    gconv_w, gconv_b = [None] * depth, [None] * depth
    gmix_pre, gmix_post, gffn_pre, gffn_post = [None] * depth, [None] * depth, [None] * depth, [None] * depth
    gpool_scale, gsinks = [None] * n_a, [None] * n_b
    dks, dvs = [], []
    gkv_g = None
    whole = lambda cols: pl.BlockSpec((t, cols), lambda c: (0, 0), pipeline_mode=pl.Buffered(1))
    per_out = lambda cols: pl.BlockSpec((None, t, cols), lambda c: (c, 0, 0))
    by_dev = lambda g: g.reshape(N_DEV, -1, g.shape[-1])
    def small_grads():
        cat = lambda rows: jnp.concatenate(rows, axis=0)
        row = lambda a: a.reshape(1, -1)
        everything = lambda ref, s, me: ref[s]
        lanes = pool_scale.shape[1]
        return [("mix_pre_g", cat(gmix_pre), everything, (mix_pre_g, m_mix_pre_g, v_mix_pre_g)),
                ("mix_post_g", cat(gmix_post), everything, (mix_post_g, m_mix_post_g, v_mix_post_g)),
                ("kv_norm_g", gkv_g, everything, (row(kv_norm_g), row(m_kv_norm_g), row(v_kv_norm_g))),
                ("sinks", cat(gsinks), lambda ref, s, me: ref[s, :, pl.ds(0, n_heads)], (sinks, m_sinks, v_sinks)),
                ("ffn_pre_g", cat(gffn_pre), everything, (ffn_pre_g, m_ffn_pre_g, v_ffn_pre_g)),
                ("ffn_post_g", cat(gffn_post), everything, (ffn_post_g, m_ffn_post_g, v_ffn_post_g)),
                ("ffn_conv_b", jnp.stack(gconv_b).reshape(depth, N_DEV * fs), everything, (ffn_conv_b, m_ffn_conv_b, v_ffn_conv_b)),
                ("pool_scale", cat(gpool_scale), lambda ref, s, me: ref[s, :, pl.ds(pl.multiple_of(me * lanes, lanes), lanes)],
                 (pool_scale, m_pool_scale, v_pool_scale)),
                ("ffn_conv_w", jnp.stack(gconv_w).transpose(0, 2, 1, 3), lambda ref, s, me: ref[s, :, me],
                 (ffn_conv_w, m_ffn_conv_w, v_ffn_conv_w))]

    flying = []

    def launch(going, after, name):
        handle, token = _exchange_start(SCATTER, [g for _, _, g, _ in going], [a for _, _, _, a in going], after, name)
        flying.append(([(nm, slab) for nm, slab, _, _ in going], handle))
        return token

    post = mix_post
    ffn_post_b = ffn_post
    token = None
    for l in reversed(range(depth)):
        x_in, x_mid, u, uc, f, mixer = saved[l]
        duc, gout, gconv_b[l], gffn_post[l] = _ffn_bwd_out(dx, f, ffn_post_b, uc, conv_b_f, w_out_l[l], l, 0, f"ffn_bwd_out_{l}")
        going = [("ffn_w_out", l, by_dev(gout), 0)]
        pre = ffn_pre
        if l == 0:
            token = launch(going, gout, "scatter_start_0_out")
            going = []
            pre = pre + token[0, 0]
        dx, du, hf, gconv_w[l], gffn_pre[l] = _ffn_bwd_in(dx, x_mid, pre, duc, u, conv_w_f, w_in_l[l], l, 0, f"ffn_bwd_in_{l}")
        gin = _tn_matmul(du, hf, per_out(fs), whole(d_model), N_DEV, fs, d_model, f"grad_w_in_{l}", _WIRE_DTYPE)
        going.append(("ffn_w_in", l, gin, 0))
        if l == 0:
            token = launch(going, dx, "scatter_start_0_in")
            going = []
            post = post + token[0, 0]
        if l < n_a:
            dsave, yu = mixer
            dx, dyu, gpool_scale[l], gmix_pre[l], gmix_post[l] = _pool_bwd(
                dx, x_in, mix_pre, post, dsave, yu, pool_w_f, pool_scale_f, l, f"pool_bwd_{l}")
            gc = d_model // len(POOL_WINDOWS)
            by_group = pl.BlockSpec((t, gc), lambda c: (0, c))
            gpool = _tn_matmul(dsave, dyu, by_group, by_group, len(POOL_WINDOWS), gc, gc, f"grad_pool_w_{l}", _WIRE_DTYPE)
            going.append(("pool_w", l, gpool.reshape(len(POOL_WINDOWS), N_DEV, -1, gc), 1))
        else:
            j = l - n_a
            qs, o, lse, mo = mixer
            do, go, gmix_post[l] = _oproj_bwd(dx, mo, o, w_o_f, post, l, j, f"oproj_bwd_{l}")
            dqs, dk, dv, gsinks[j] = _attn_bwd(qs, kdup, vdup, sinks, lse, do, j, f"attn_bwd_{l}")
            dks.append(dk)
            dvs.append(dv)
            dx, gq, gmix_pre[l] = _q_bwd(dx, dqs, x_in, mix_pre, w_q_f, cos, ssin, l, j, f"q_bwd_{l}")
            if j == 0:
                dx, gkv, gkv_g = _kv_bwd(dx, x_kv, kv_g, w_kv_f, cos, ssin, dks, dvs, "kv_bwd")
                going.append(("w_kv", 0, by_dev(gkv), 0))
            going += [("w_o", j, by_dev(go), 0), ("w_q", j, by_dev(gq), 0)]
        after = dx
        if l == 0:
            small = small_grads()
            leaving = [g for _, g, _, _ in small] + [jnp.broadcast_to(loss_part, (1, LANES))]
            small_flight, after = _exchange_start(GATHER, leaving, [0] * len(leaving), dx, "gather_small_grads")
        token = launch(going, after, f"scatter_start_{l}")
        ffn_post_b = ffn_post_b + token[0, 0]

    grad_x = dx.reshape(x.shape)

    shard = {"pool_w": (pool_w, m_pool_w, v_pool_w), "w_kv": (w_kv, m_w_kv, v_w_kv), "w_q": (w_q, m_w_q, v_w_q),
             "w_o": (w_o, m_w_o, v_w_o), "ffn_w_in": tuple(by_hidden(a) for a in (ffn_w_in, m_ffn_w_in, v_ffn_w_in)),
             "ffn_w_out": (ffn_w_out, m_ffn_w_out, v_ffn_w_out)}
    big = {}

    def arrive(flights, after):
        for idx, (names, handle) in flights:
            parts = _exchange_wait(handle, after, f"scatter_wait_{idx}")
            for (nm, slab), p in zip(names, parts):
                cols = p.shape[-1]
                w2, m2, v2 = (a.reshape(-1, cols) for a in shard[nm])
                big[nm] = _adamw(p.reshape(N_DEV, -1, cols), w2, m2, v2, slab, big.get(nm), f"adamw_{nm}_{slab}")
                after = big[nm][0]
        return after

    done = arrive(list(enumerate(flying)), token)
    *small_parts, loss_parts = _exchange_wait(small_flight, done, "gather_small_grads_wait")
    loss = jnp.sum(loss_parts[:, 0, 0])
    upd = _adamw_small(small_parts, [pick for _, _, pick, _ in small], [wmv for _, _, _, wmv in small], "adamw_small")
    res = {nm: tuple(r.reshape(shard[nm][0].shape) for r in out) for nm, out in big.items()}
    res["ffn_w_in"] = tuple(by_hidden(r) for r in res["ffn_w_in"])
    for (nm, _, _, _), out in zip(small, upd):
        res[nm] = tuple(a.reshape(kv_norm_g.shape) for a in out) if nm == "kv_norm_g" else out

    order = ["mix_pre_g", "mix_post_g", "pool_w", "pool_scale", "kv_norm_g", "w_kv", "w_q", "w_o", "sinks", "ffn_pre_g",
             "ffn_post_g", "ffn_w_in", "ffn_conv_w", "ffn_conv_b", "ffn_w_out"]
    return (loss, grad_x, *[res[nm][0] for nm in order], *[res[nm][1] for nm in order],
            *[res[nm][2] for nm in order], *[res[nm][3] for nm in order])
```

```python
import functools
import math

import jax
import jax.numpy as jnp
from jax import lax
from jax.experimental import pallas as pl
from jax.experimental.pallas import tpu as pltpu

F32 = jnp.float32
_MXU_DTYPE = jnp.bfloat16
_ACT_DTYPE = jnp.bfloat16
_WIRE_DTYPE = jnp.bfloat16
_SAVE_DTYPE = jnp.bfloat16

N_DEV = 8
POOL_WINDOWS = (2, 4, 8, 16)
POOL_HALO = 32
assert POOL_WINDOWS == tuple(2 ** (g + 1) for g in range(len(POOL_WINDOWS))) and 8 * len(POOL_WINDOWS) <= POOL_HALO
HEAD_DIM = 64
N_KV_HEADS = 4
WINDOW = 128
BLOCK = 128
LANES = 128
ROPE_THETA = 10000.0
ATTN_SCALE = 1.0 / math.sqrt(HEAD_DIM)
NEG_INF = -1e30
RMS_EPS = 1e-6
CONV_HALO = 8
SAVE_HALO = 16
PROJ_TILE = 512
ADAM_LR = 0.001
ADAM_B1 = 0.9
ADAM_B2 = 0.999
ADAM_EPS = 1e-08
ADAM_WD = 0.01
ADAM_STEP = 10
VMEM_LIMIT = 56 * 1024 * 1024
MESH = pl.DeviceIdType.MESH


def _params(n_axes=1, vmem=VMEM_LIMIT):
    return pltpu.CompilerParams(dimension_semantics=("arbitrary",) * n_axes, vmem_limit_bytes=vmem)


def _resident(shape, index):
    return pl.BlockSpec(shape, lambda *_: index, pipeline_mode=pl.Buffered(1))


def _const(shape, index=None):
    index = (0,) * len(shape) if index is None else index
    return pl.BlockSpec(shape, lambda *_: index)


def _rows(tm, cols):
    return pl.BlockSpec((tm, cols), lambda i: (i, 0))


def _row_tile(t, most=256):
    for tm in (512, 256, 128, 64, 32, 16, 8):
        if tm <= most and t % tm == 0:
            return tm
    raise ValueError(f"sequence length {t} is not a multiple of 8")


def _mm(a, b):
    return jnp.dot(a.astype(_MXU_DTYPE), b.astype(_MXU_DTYPE), preferred_element_type=F32)


def _mm_tb(a, b):
    return lax.dot_general(a.astype(_MXU_DTYPE), b.astype(_MXU_DTYPE), (((1,), (1,)), ((), ())),
                           preferred_element_type=F32)


def _mm_ta(a, b):
    return lax.dot_general(a.astype(_MXU_DTYPE), b.astype(_MXU_DTYPE), (((0,), (0,)), ((), ())),
                           preferred_element_type=F32)


def _rms_r(x):
    return lax.rsqrt(jnp.mean(x * x, axis=-1, keepdims=True) + RMS_EPS)


def _rms_parts(x):
    r = _rms_r(x)
    return r, x * r


def _rms_fwd(x, g):
    return _rms_parts(x)[1] * g


def _rms_bwd(x, g, dy, parts=None):
    r, xh = _rms_parts(x) if parts is None else parts
    dg = jnp.sum(dy * xh, axis=0, keepdims=True)
    dxh = dy * g
    dx = r * (dxh - xh * jnp.mean(dxh * xh, axis=-1, keepdims=True))
    return dx, dg


_GELU_C = math.sqrt(2.0 / math.pi)


def _gelu_parts(z):
    z2 = z * z
    e = jnp.exp(z * (-2.0 * _GELU_C - (2.0 * _GELU_C * 0.044715) * z2))
    cdf = pl.reciprocal(1.0 + e, approx=False)
    dz = cdf + (z * (cdf * (1.0 - cdf))) * (2.0 * _GELU_C + (6.0 * _GELU_C * 0.044715) * z2)
    return cdf, dz


def _lane_iota(shape):
    return lax.broadcasted_iota(jnp.int32, shape, len(shape) - 1)


def _rope_partner(xb):
    first = (_lane_iota(xb.shape) & 32) == 0
    return jnp.where(first, pltpu.roll(xb, LANES - 32, 1), pltpu.roll(xb, 32, 1))


def _rope_fwd(xb, cos, ssin):
    return xb * cos + _rope_partner(xb) * ssin


def _rope_bwd(dyb, cos, ssin):
    return dyb * cos - _rope_partner(dyb) * ssin


def _low_half(shape):
    return (_lane_iota(shape) & 64) == 0


def _pool_fwd(x, pre_g, post_g, w, scale, layer, name):
    t, d_model = x.shape
    tm = _row_tile(t)
    n_groups, gc = w.shape[1], w.shape[2]

    def body(x_ref, pre_ref, post_ref, w_ref, sc_ref, xo_ref, d_ref, yu_ref, hbuf, sbuf):
        i = pl.program_id(0)

        @pl.when(i == 0)
        def _():
            hbuf[pl.ds(0, POOL_HALO), :] = jnp.zeros((POOL_HALO, d_model), F32)

        xv = x_ref[...]
        hbuf[pl.ds(POOL_HALO, tm), :] = _rms_fwd(xv, pre_ref[...])
        tok = i * tm + lax.broadcasted_iota(jnp.int32, (tm, 1), 0)
        yus = []
        for gi, wnd in enumerate(POOL_WINDOWS):
            first, reach = CONV_HALO * (gi + 1), wnd // 2
            n = POOL_HALO + tm - first
            cols = pl.ds(gi * gc, d_model - gi * gc)
            src = hbuf if gi == 0 else sbuf.at[gi - 1]
            level = src[pl.ds(first, n), cols] + src[pl.ds(first - reach, n), cols]
            if gi + 1 < len(POOL_WINDOWS):
                sbuf[gi, pl.ds(first, n), cols] = level
            h = hbuf[pl.ds(POOL_HALO, tm), pl.ds(gi * gc, gc)]
            cnt = jnp.minimum(tok + 1, wnd).astype(F32)
            dg = level[POOL_HALO - first:, :gc] / cnt - h
            d_ref[:, pl.ds(gi * gc, gc)] = dg.astype(d_ref.dtype)
            yus.append(_mm(dg, w_ref[gi]))
        hbuf[pl.ds(0, POOL_HALO), :] = hbuf[pl.ds(tm, POOL_HALO), :]
        yu = jnp.concatenate(yus, axis=1)
        yu_ref[...] = yu
        xo_ref[...] = xv + _rms_fwd(yu * sc_ref[...], post_ref[...])

    return pl.pallas_call(
        body, name=name, grid=(t // tm,),
        out_shape=(jax.ShapeDtypeStruct((t, d_model), F32), jax.ShapeDtypeStruct((t, d_model), _ACT_DTYPE),
                   jax.ShapeDtypeStruct((t, d_model), F32)),
        in_specs=[_rows(tm, d_model), _const((None, 1, d_model), (layer, 0, 0)), _const((None, 1, d_model), (layer, 0, 0)),
                  _const((None, n_groups, gc, gc), (layer, 0, 0, 0)), _const((None, 1, d_model), (layer, 0, 0))],
        out_specs=(_rows(tm, d_model), _rows(tm, d_model), _rows(tm, d_model)),
        scratch_shapes=[pltpu.VMEM((POOL_HALO + tm, d_model), F32),
                        pltpu.VMEM((len(POOL_WINDOWS) - 1, POOL_HALO + tm, d_model), F32)],
        compiler_params=_params(),
    )(x, pre_g, post_g, w, scale)


def _pool_bwd(dx, x, pre_g, post_g, d, yu, w, scale, layer, name):
    t, d_model = x.shape
    tm = _row_tile(t)
    nt = t // tm
    n_groups, gc = w.shape[1], w.shape[2]
    rev = lambda i: (nt - 1 - i, 0)
    rows = pl.BlockSpec((tm, d_model), rev)

    def body(dx_ref, x_ref, pre_ref, post_ref, d_ref, yu_ref, w_ref, sc_ref,
             dxi_ref, dw_ref, dsc_ref, dpre_ref, dpost_ref, zbuf, sbuf, acc):
        i = pl.program_id(0)

        @pl.when(i == 0)
        def _():
            zbuf[pl.ds(tm, POOL_HALO), :] = jnp.zeros((POOL_HALO, d_model), F32)
            acc[...] = jnp.zeros_like(acc)
            dsc_ref[...] = jnp.zeros_like(dsc_ref)
            dpre_ref[...] = jnp.zeros_like(dpre_ref)
            dpost_ref[...] = jnp.zeros_like(dpost_ref)

        dxo = dx_ref[...]
        yuv = yu_ref[...]
        sc = sc_ref[...]
        dm, dpost = _rms_bwd(yuv * sc, post_ref[...], dxo)
        dpost_ref[...] += dpost
        dsc_ref[...] += jnp.sum(dm * yuv, axis=0, keepdims=True)
        dyu = (dm * sc).astype(_MXU_DTYPE)
        tok = (nt - 1 - i) * tm + lax.broadcasted_iota(jnp.int32, (tm, 1), 0)
        dds = []
        for gi, wnd in enumerate(POOL_WINDOWS):
            cols = pl.ds(gi * gc, gc)
            acc[gi] += _mm_ta(d_ref[:, cols], dyu[:, gi * gc:(gi + 1) * gc])
            dd = _mm_tb(dyu[:, gi * gc:(gi + 1) * gc], w_ref[gi])
            cnt = jnp.minimum(tok + 1, wnd).astype(F32)
            zbuf[pl.ds(0, tm), cols] = dd / cnt
            dds.append(dd)
        dhs = []
        for gi, wnd in enumerate(POOL_WINDOWS):
            reach = wnd // 2
            n = tm + POOL_HALO - CONV_HALO * (gi + 1)
            cols = pl.ds(gi * gc, d_model - gi * gc)
            src = zbuf if gi == 0 else sbuf.at[gi - 1]
            level = src[pl.ds(0, n), cols] + src[pl.ds(reach, n), cols]
            if gi + 1 < len(POOL_WINDOWS):
                sbuf[gi, pl.ds(0, n), cols] = level
            dhs.append(level[:tm, :gc] - dds[gi])
        zbuf[pl.ds(tm, POOL_HALO), :] = zbuf[pl.ds(0, POOL_HALO), :]
        dh = jnp.concatenate(dhs, axis=1)
        dxp, dpre = _rms_bwd(x_ref[...], pre_ref[...], dh)
        dpre_ref[...] += dpre
        dxi_ref[...] = dxo + dxp

        @pl.when(i == nt - 1)
        def _():
            dw_ref[...] = acc[...].astype(dw_ref.dtype)

    vec = jax.ShapeDtypeStruct((1, d_model), F32)
    return pl.pallas_call(
        body, name=name, grid=(nt,),
        out_shape=(jax.ShapeDtypeStruct((t, d_model), F32), jax.ShapeDtypeStruct((n_groups, gc, gc), _WIRE_DTYPE), vec, vec, vec),
        in_specs=[rows, rows, _const((None, 1, d_model), (layer, 0, 0)), _const((None, 1, d_model), (layer, 0, 0)), rows, rows,
                  _const((None, n_groups, gc, gc), (layer, 0, 0, 0)), _const((None, 1, d_model), (layer, 0, 0))],
        out_specs=(rows, _const((n_groups, gc, gc)), _const((1, d_model)), _const((1, d_model)), _const((1, d_model))),
        scratch_shapes=[pltpu.VMEM((tm + POOL_HALO, d_model), F32),
                        pltpu.VMEM((len(POOL_WINDOWS) - 1, tm + POOL_HALO, d_model), F32),
                        pltpu.VMEM((n_groups, gc, gc), F32)],
        compiler_params=_params(),
    )(dx, x, pre_g, post_g, d, yu, w, scale)


def _conv_taps(cw_ref, s):
    return [cw_ref[k, pl.ds(s, 1), :] for k in range(3)]


def _shift_down(v, k, before):
    rolled = pltpu.roll(v, k, 0)
    row = lax.broadcasted_iota(jnp.int32, before.shape, 0)
    head = jnp.where(row < k, pltpu.roll(before, k, 0), rolled[:CONV_HALO])
    return jnp.concatenate([head, rolled[CONV_HALO:]], axis=0)


def _shift_up(v, k, after):
    rows = v.shape[0]
    rolled = pltpu.roll(v, rows - k, 0)
    row = lax.broadcasted_iota(jnp.int32, after.shape, 0)
    tail = jnp.where(row >= CONV_HALO - k, pltpu.roll(after, CONV_HALO - k, 0), rolled[rows - CONV_HALO:])
    return jnp.concatenate([rolled[:rows - CONV_HALO], tail], axis=0)


def _ffn_fwd(x, pre_g, post_g, w_in, conv_w, conv_b, w_out, layer, w_layer, name):
    t, d_model = x.shape
    tm = _row_tile(t)
    fs = w_in.shape[2]
    half = N_DEV // 2

    def body(x_ref, pre_ref, post_ref, win_ref, cw_ref, cb_ref, wout_ref, xo_ref, u_ref, uc_ref, f_ref, carry):
        i = pl.program_id(0)

        @pl.when(i == 0)
        def _():
            carry[...] = jnp.zeros_like(carry)

        xv = x_ref[...]
        hf = _rms_fwd(xv, pre_ref[...]).astype(_MXU_DTYPE)
        f = jnp.zeros((tm, d_model), F32)
        project = lambda b: [_mm_tb(hf, win_ref[s]) for s in (b, b + half)]
        ahead = project(0)
        for b in range(half):
            us, ahead = ahead, project(b + 1) if b + 1 < half else None
            ucs = []
            for s, u in zip((b, b + half), us):
                u_ref[s] = u.astype(u_ref.dtype)
                before = carry[s]
                carry[s] = u[tm - CONV_HALO:]
                w0, w1, w2 = _conv_taps(cw_ref, s)
                uc = ((w0 * _shift_down(u, 2, before) + w1 * _shift_down(u, 1, before)) + w2 * u) + cb_ref[pl.ds(s, 1), :]
                uc_ref[s] = uc.astype(uc_ref.dtype)
                ucs.append(uc)
            gate, val = ucs
            cdf, _ = _gelu_parts(gate)
            f = f + _mm((gate * cdf) * val, wout_ref[pl.ds(b * fs, fs), :])
        f_ref[...] = f
        xo_ref[...] = xv + _rms_fwd(f, post_ref[...])

    tile3 = pl.BlockSpec((N_DEV, tm, fs), lambda i: (0, i, 0))
    saved = jax.ShapeDtypeStruct((N_DEV, t, fs), _SAVE_DTYPE)
    return pl.pallas_call(
        body, name=name, grid=(t // tm,),
        out_shape=(jax.ShapeDtypeStruct((t, d_model), F32), saved, saved, jax.ShapeDtypeStruct((t, d_model), F32)),
        in_specs=[_rows(tm, d_model), _const((None, 1, d_model), (layer, 0, 0)), _const((None, 1, d_model), (layer, 0, 0)),
                  _resident((None, N_DEV, fs, d_model), (w_layer, 0, 0, 0)), _const((None, 3, N_DEV, fs), (layer, 0, 0, 0)),
                  _const((None, N_DEV, fs), (layer, 0, 0)), _resident((None, half * fs, d_model), (w_layer, 0, 0))],
        out_specs=(_rows(tm, d_model), tile3, tile3, _rows(tm, d_model)),
        scratch_shapes=[pltpu.VMEM((N_DEV, CONV_HALO, fs), F32)],
        compiler_params=_params(),
    )(x, pre_g, post_g, w_in, conv_w, conv_b, w_out)


def _ffn_bwd_out(dx, f, post_g, uc, conv_b, w_out, layer, w_layer, name):
    t, d_model = dx.shape
    tm = _row_tile(t)
    nt = t // tm
    fs = uc.shape[2]
    half = N_DEV // 2

    def body(dx_ref, f_ref, post_ref, uc_ref, wout_ref, duc_ref, dwout_ref, dcb_ref, dpost_ref, acc):
        i = pl.program_id(0)

        @pl.when(i == 0)
        def _():
            acc[...] = jnp.zeros_like(acc)
            dcb_ref[...] = jnp.zeros_like(dcb_ref)
            dpost_ref[...] = jnp.zeros_like(dpost_ref)

        df, dpost = _rms_bwd(f_ref[...], post_ref[...], dx_ref[...])
        dpost_ref[...] += dpost
        dfm = df.astype(_MXU_DTYPE)
        project = lambda b: _mm_tb(dfm, wout_ref[pl.ds(b * fs, fs), :])
        ahead = project(0)
        for b in range(half):
            dg, ahead = ahead, project(b + 1) if b + 1 < half else None
            gate = uc_ref[b].astype(F32)
            val = uc_ref[b + half].astype(F32)
            cdf, dgelu = _gelu_parts(gate)
            ge = gate * cdf
            acc[pl.ds(b * fs, fs), :] += _mm_ta(ge * val, dfm)
            for s, dd in ((b, dg * val * dgelu), (b + half, dg * ge)):
                duc_ref[s] = dd.astype(duc_ref.dtype)
                dcb_ref[pl.ds(s, 1), :] += jnp.sum(dd, axis=0, keepdims=True)

        @pl.when(i == nt - 1)
        def _():
            dwout_ref[...] = acc[...].astype(dwout_ref.dtype)

    tile3 = pl.BlockSpec((N_DEV, tm, fs), lambda i: (0, i, 0))
    return pl.pallas_call(
        body, name=name, grid=(nt,),
        out_shape=(jax.ShapeDtypeStruct((N_DEV, t, fs), _SAVE_DTYPE), jax.ShapeDtypeStruct((half * fs, d_model), _WIRE_DTYPE),
                   jax.ShapeDtypeStruct((N_DEV, fs), F32), jax.ShapeDtypeStruct((1, d_model), F32)),
        in_specs=[_rows(tm, d_model), _rows(tm, d_model), _const((None, 1, d_model), (layer, 0, 0)), tile3,
                  _resident((None, half * fs, d_model), (w_layer, 0, 0))],
        out_specs=(tile3, _resident((half * fs, d_model), (0, 0)), _const((N_DEV, fs)), _const((1, d_model))),
        scratch_shapes=[pltpu.VMEM((half * fs, d_model), F32)],
        compiler_params=_params(),
    )(dx, f, post_g, uc, w_out)


def _ffn_bwd_in(dx, x, pre_g, duc, u, conv_w, w_in, layer, w_layer, name):
    t, d_model = dx.shape
    tm = _row_tile(t)
    nt = t // tm
    fs = duc.shape[2]
    hb = SAVE_HALO
    per_tile = tm // hb

    def body(dx_ref, x_ref, pre_ref, duc_ref, dn_ref, u_ref, cw_ref, win_ref, dxi_ref, du_ref, hf_ref, dcw_ref, dpre_ref):
        i = pl.program_id(0)

        @pl.when(i == 0)
        def _():
            dcw_ref[...] = jnp.zeros_like(dcw_ref)
            dpre_ref[...] = jnp.zeros_like(dpre_ref)

        xv = x_ref[...]
        pre = pre_ref[...]
        normed = _rms_parts(xv)
        hf_ref[...] = (normed[1] * pre).astype(hf_ref.dtype)
        dhf = jnp.zeros((tm, d_model), F32)
        for s in range(N_DEV):
            d0 = duc_ref[s].astype(F32)
            after = jnp.where(i == nt - 1, 0.0, dn_ref[s].astype(F32)[:CONV_HALO])
            d1 = _shift_up(d0, 1, after)
            d2 = _shift_up(d0, 2, after)
            uv = u_ref[s].astype(F32)
            for k, dk in ((2, d0), (1, d1), (0, d2)):
                dcw_ref[k, pl.ds(s, 1), :] += jnp.sum(dk * uv, axis=0, keepdims=True)
            w0, w1, w2 = _conv_taps(cw_ref, s)
            du = (w2 * d0 + w1 * d1 + w0 * d2).astype(_MXU_DTYPE)
            du_ref[s] = du
            dhf = dhf + _mm(du, win_ref[s])
        dxp, dpre = _rms_bwd(xv, pre, dhf, normed)
        dpre_ref[...] += dpre
        dxi_ref[...] = dx_ref[...] + dxp

    tile3 = pl.BlockSpec((N_DEV, tm, fs), lambda i: (0, i, 0))
    return pl.pallas_call(
        body, name=name, grid=(nt,),
        out_shape=(jax.ShapeDtypeStruct((t, d_model), F32), jax.ShapeDtypeStruct((N_DEV, t, fs), _ACT_DTYPE),
                   jax.ShapeDtypeStruct((t, d_model), _ACT_DTYPE), jax.ShapeDtypeStruct((3, N_DEV, fs), F32),
                   jax.ShapeDtypeStruct((1, d_model), F32)),
        in_specs=[_rows(tm, d_model), _rows(tm, d_model), _const((None, 1, d_model), (layer, 0, 0)), tile3,
                  pl.BlockSpec((N_DEV, hb, fs), lambda i: (0, jnp.minimum((i + 1) * per_tile, t // hb - 1), 0)), tile3,
                  _const((None, 3, N_DEV, fs), (layer, 0, 0, 0)), _resident((None, N_DEV, fs, d_model), (w_layer, 0, 0, 0))],
        out_specs=(_rows(tm, d_model), tile3, _rows(tm, d_model), _const((3, N_DEV, fs)), _const((1, d_model))),
        compiler_params=_params(),
    )(dx, x, pre_g, duc, duc, u, conv_w, w_in)


def _tn_matmul(a, b, a_spec, b_spec, n_out, m, n, name, out_dtype):
    def body(a_ref, b_ref, o_ref):
        o_ref[...] = _mm_ta(a_ref[...], b_ref[...]).astype(o_ref.dtype)

    return pl.pallas_call(
        body, name=name, grid=(n_out,),
        out_shape=jax.ShapeDtypeStruct((n_out, m, n), out_dtype),
        in_specs=[a_spec, b_spec],
        out_specs=pl.BlockSpec((None, m, n), lambda c: (c, 0, 0)),
        compiler_params=_params(),
    )(a, b)


def _kv_fwd(x, kv_g, w_kv, cos, ssin, name):
    t, d_model = x.shape
    tm = _row_tile(t, PROJ_TILE)
    kvd = w_kv.shape[1] // 2
    pairs = kvd // LANES

    def body(x_ref, g_ref, w_ref, cos_ref, sin_ref, k_ref, v_ref):
        kv = _mm(_rms_fwd(x_ref[...], g_ref[...]), w_ref[...])
        low = _low_half((tm, LANES))
        for j in range(pairs):
            kb = _rope_fwd(kv[:, j * LANES:(j + 1) * LANES], cos_ref[...], sin_ref[...])
            vb = kv[:, kvd + j * LANES:kvd + (j + 1) * LANES]
            for blk, ref in ((kb, k_ref), (vb, v_ref)):
                sw = pltpu.roll(blk, 64, 1)
                ref[2 * j] = jnp.where(low, blk, sw).astype(ref.dtype)
                ref[2 * j + 1] = jnp.where(low, sw, blk).astype(ref.dtype)

    heads = jax.ShapeDtypeStruct((N_KV_HEADS, t, LANES), _ACT_DTYPE)
    hspec = pl.BlockSpec((N_KV_HEADS, tm, LANES), lambda i: (0, i, 0))
    return pl.pallas_call(
        body, name=name, grid=(t // tm,), out_shape=(heads, heads),
        in_specs=[_rows(tm, d_model), _const((1, d_model)), _const(w_kv.shape), _rows(tm, LANES), _rows(tm, LANES)],
        out_specs=(hspec, hspec), compiler_params=_params(),
    )(x, kv_g, w_kv, cos, ssin)


def _kv_bwd(dx, x, kv_g, w_kv, cos, ssin, dks, dvs, name):
    t, d_model = x.shape
    tm = _row_tile(t, PROJ_TILE)
    nt = t // tm
    kvd = w_kv.shape[1] // 2
    pairs = kvd // LANES
    n_users = len(dks)

    def body(dx_ref, x_ref, g_ref, w_ref, cos_ref, sin_ref, *refs):
        dk_refs, dv_refs = refs[:n_users], refs[n_users:2 * n_users]
        dxi_ref, dw_ref, dg_ref, acc = refs[2 * n_users:]
        dk_ref = functools.reduce(lambda a, b: a + b, [r[...] for r in dk_refs])
        dv_ref = functools.reduce(lambda a, b: a + b, [r[...] for r in dv_refs])
        i = pl.program_id(0)

        @pl.when(i == 0)
        def _():
            dg_ref[...] = jnp.zeros_like(dg_ref)
            acc[...] = jnp.zeros_like(acc)

        xv = x_ref[...]
        g = g_ref[...]
        low = _low_half((tm, LANES))
        dks, dvs = [], []
        for j in range(pairs):
            dkb = jnp.where(low, dk_ref[2 * j], dk_ref[2 * j + 1])
            dks.append(_rope_bwd(dkb, cos_ref[...], sin_ref[...]))
            dvs.append(jnp.where(low, dv_ref[2 * j], dv_ref[2 * j + 1]))
        dkv = jnp.concatenate(dks + dvs, axis=1).astype(_MXU_DTYPE)
        normed = _rms_parts(xv)
        acc[...] += _mm_ta(normed[1] * g, dkv)
        dxp, dg = _rms_bwd(xv, g, _mm_tb(dkv, w_ref[...]), normed)
        dg_ref[...] += dg
        dxi_ref[...] = dx_ref[...] + dxp

        @pl.when(i == nt - 1)
        def _():
            dw_ref[...] = acc[...].astype(dw_ref.dtype)

    hspec = pl.BlockSpec((N_KV_HEADS, tm, LANES), lambda i: (0, i, 0))
    return pl.pallas_call(
        body, name=name, grid=(nt,),
        out_shape=(jax.ShapeDtypeStruct((t, d_model), F32), jax.ShapeDtypeStruct(w_kv.shape, _WIRE_DTYPE),
                   jax.ShapeDtypeStruct((1, d_model), F32)),
        in_specs=[_rows(tm, d_model), _rows(tm, d_model), _const((1, d_model)), _const(w_kv.shape),
                  _rows(tm, LANES), _rows(tm, LANES)] + [hspec] * (2 * n_users),
        out_specs=(_rows(tm, d_model), _resident(w_kv.shape, (0, 0)), _const((1, d_model))),
        scratch_shapes=[pltpu.VMEM(w_kv.shape, F32)],
        compiler_params=_params(),
    )(dx, x, kv_g, w_kv, cos, ssin, *dks, *dvs)


def _q_fwd(x, pre_g, w_q, cos, ssin, layer, j, name):
    t, d_model = x.shape
    tm = _row_tile(t, PROJ_TILE)

    def body(x_ref, g_ref, w_ref, cos_ref, sin_ref, q_ref):
        q = _mm(_rms_fwd(x_ref[...], g_ref[...]), w_ref[...])
        for p in range(d_model // LANES):
            cols = slice(p * LANES, (p + 1) * LANES)
            q_ref[:, cols] = (_rope_fwd(q[:, cols], cos_ref[...], sin_ref[...]) * ATTN_SCALE).astype(q_ref.dtype)

    return pl.pallas_call(
        body, name=name, grid=(t // tm,), out_shape=jax.ShapeDtypeStruct((t, d_model), _ACT_DTYPE),
        in_specs=[_rows(tm, d_model), _const((None, 1, d_model), (layer, 0, 0)), _const((None, d_model, d_model), (j, 0, 0)),
                  _rows(tm, LANES), _rows(tm, LANES)],
        out_specs=_rows(tm, d_model), compiler_params=_params(),
    )(x, pre_g, w_q, cos, ssin)


def _q_bwd(dx, dqs, x, pre_g, w_q, cos, ssin, layer, j, name):
    t, d_model = x.shape
    tm = _row_tile(t, PROJ_TILE)
    nt = t // tm

    def body(dx_ref, dq_ref, x_ref, g_ref, w_ref, cos_ref, sin_ref, dxi_ref, dw_ref, dg_ref, acc):
        i = pl.program_id(0)

        @pl.when(i == 0)
        def _():
            dg_ref[...] = jnp.zeros_like(dg_ref)
            acc[...] = jnp.zeros_like(acc)

        xv = x_ref[...]
        g = g_ref[...]
        parts = []
        for p in range(d_model // LANES):
            cols = slice(p * LANES, (p + 1) * LANES)
            parts.append(_rope_bwd(dq_ref[:, cols] * ATTN_SCALE, cos_ref[...], sin_ref[...]))
        dq = jnp.concatenate(parts, axis=1).astype(_MXU_DTYPE)
        normed = _rms_parts(xv)
        acc[...] += _mm_ta(normed[1] * g, dq)
        dxp, dg = _rms_bwd(xv, g, _mm_tb(dq, w_ref[...]), normed)
        dg_ref[...] += dg
        dxi_ref[...] = dx_ref[...] + dxp

        @pl.when(i == nt - 1)
        def _():
            dw_ref[...] = acc[...].astype(dw_ref.dtype)

    return pl.pallas_call(
        body, name=name, grid=(nt,),
        out_shape=(jax.ShapeDtypeStruct((t, d_model), F32), jax.ShapeDtypeStruct((d_model, d_model), _WIRE_DTYPE),
                   jax.ShapeDtypeStruct((1, d_model), F32)),
        in_specs=[_rows(tm, d_model), _rows(tm, d_model), _rows(tm, d_model), _const((None, 1, d_model), (layer, 0, 0)),
                  _const((None, d_model, d_model), (j, 0, 0)), _rows(tm, LANES), _rows(tm, LANES)],
        out_specs=(_rows(tm, d_model), _resident((d_model, d_model), (0, 0)), _const((1, d_model))),
        scratch_shapes=[pltpu.VMEM((d_model, d_model), F32)],
        compiler_params=_params(),
    )(dx, dqs, x, pre_g, w_q, cos, ssin)


def _stack_heads(pairs):
    low = _low_half(pairs[0].shape)
    zero = jnp.zeros_like(pairs[0])
    return jnp.concatenate([h for blk in pairs for h in (jnp.where(low, blk, zero), jnp.where(low, zero, blk))], axis=0)


def _unstack_heads(stacked, i):
    a, b = stacked[2 * i * BLOCK:(2 * i + 1) * BLOCK], stacked[(2 * i + 1) * BLOCK:(2 * i + 2) * BLOCK]
    return jnp.where(_low_half(a.shape), a, b)


def _attn_scores(q_pairs, k2, n, sinks):
    qst = _stack_heads(q_pairs)
    s = _mm_tb(qst, k2)
    row = lax.broadcasted_iota(jnp.int32, s.shape, 0)
    col = lax.broadcasted_iota(jnp.int32, s.shape, 1)
    rel = BLOCK + (row & (BLOCK - 1)) - col
    valid = (rel >= 0) & (rel < WINDOW) & (n * BLOCK + col - BLOCK >= 0)
    s = jnp.where(valid, s, NEG_INF)
    rows1 = lax.broadcasted_iota(jnp.int32, (s.shape[0], 1), 0)
    sink = jnp.full((s.shape[0], 1), sinks[-1], F32)
    for i in reversed(range(len(sinks) - 1)):
        sink = jnp.where(rows1 < (i + 1) * BLOCK, sinks[i], sink)
    return qst, s, sink


def _attn_fwd(qs, kdup, vdup, sinks, j, name):
    t, d_model = qs.shape
    nb = t // BLOCK
    n_pairs = d_model // LANES
    per_group = n_pairs // N_KV_HEADS

    def body(sink_ref, q_ref, kp_ref, ko_ref, vp_ref, vo_ref, o_ref, lse_ref):
        n = pl.program_id(0)
        lane = _lane_iota((BLOCK, LANES))
        lse = jnp.zeros((BLOCK, LANES), F32)
        for hk in range(N_KV_HEADS):
            pairs = range(hk * per_group, (hk + 1) * per_group)
            heads = range(2 * pairs[0], 2 * pairs[-1] + 2)
            k2 = jnp.concatenate([kp_ref[hk], ko_ref[hk]], axis=0)
            v2 = jnp.concatenate([vp_ref[hk], vo_ref[hk]], axis=0)
            _, s, sink = _attn_scores([q_ref[:, p * LANES:(p + 1) * LANES] for p in pairs], k2, n,
                                      [sink_ref[j, h] for h in heads])
            m = jnp.maximum(jnp.max(s, axis=-1, keepdims=True), sink)
            pe = jnp.exp(s - m)
            denom = jnp.sum(pe, axis=-1, keepdims=True) + jnp.exp(sink - m)
            o2 = _mm(pe, v2) / denom
            l2 = m + jnp.log(denom)
            for i, p in enumerate(pairs):
                o_ref[:, p * LANES:(p + 1) * LANES] = _unstack_heads(o2, i).astype(o_ref.dtype)
            for i, h in enumerate(heads):
                lse = jnp.where(lane == h, l2[i * BLOCK:(i + 1) * BLOCK], lse)
        lse_ref[...] = lse

    prev = pl.BlockSpec((N_KV_HEADS, BLOCK, LANES), lambda n: (0, jnp.maximum(n - 1, 0), 0))
    own = pl.BlockSpec((N_KV_HEADS, BLOCK, LANES), lambda n: (0, n, 0))
    return pl.pallas_call(
        body, name=name, grid=(nb,),
        out_shape=(jax.ShapeDtypeStruct((t, d_model), _ACT_DTYPE), jax.ShapeDtypeStruct((t, LANES), F32)),
        in_specs=[pl.BlockSpec(memory_space=pltpu.SMEM), _rows(BLOCK, d_model), prev, own, prev, own],
        out_specs=(_rows(BLOCK, d_model), _rows(BLOCK, LANES)), compiler_params=_params(),
    )(sinks, qs, kdup, kdup, vdup, vdup)


def _attn_bwd(qs, kdup, vdup, sinks, lse, do, j, name):
    t, d_model = qs.shape
    nb = t // BLOCK
    n_pairs = d_model // LANES
    per_group = n_pairs // N_KV_HEADS
    rev = lambda n: nb - 1 - n

    def body(sink_ref, q_ref, kp_ref, ko_ref, vp_ref, vo_ref, lse_ref, do_ref, dq_ref, dk_ref, dv_ref, ds_ref, ck, cv):
        i = pl.program_id(0)
        n = nb - 1 - i

        @pl.when(i == 0)
        def _():
            ck[...] = jnp.zeros_like(ck)
            cv[...] = jnp.zeros_like(cv)
            ds_ref[...] = jnp.zeros_like(ds_ref)

        lane = _lane_iota((BLOCK, LANES))
        lane1 = _lane_iota((1, LANES))
        lsev = lse_ref[...]
        dsink = jnp.zeros((1, LANES), F32)
        for hk in range(N_KV_HEADS):
            pairs = range(hk * per_group, (hk + 1) * per_group)
            heads = range(2 * pairs[0], 2 * pairs[-1] + 2)
            k2 = jnp.concatenate([kp_ref[hk], ko_ref[hk]], axis=0)
            v2 = jnp.concatenate([vp_ref[hk], vo_ref[hk]], axis=0)
            qst, s, sink = _attn_scores([q_ref[:, p * LANES:(p + 1) * LANES] for p in pairs], k2, n,
                                        [sink_ref[j, h] for h in heads])
            l2 = jnp.concatenate([jnp.sum(jnp.where(lane == h, lsev, 0.0), axis=-1, keepdims=True) for h in heads], axis=0)
            pn = jnp.exp(s - l2)
            dost = _stack_heads([do_ref[:, p * LANES:(p + 1) * LANES] for p in pairs])
            dp = _mm_tb(dost, v2)
            dr = jnp.sum(pn * dp, axis=-1, keepdims=True)
            dsm = (pn * (dp - dr)).astype(_MXU_DTYPE)
            dsk = -jnp.exp(sink - l2) * dr
            for i, h in enumerate(heads):
                dsink = dsink + jnp.where(lane1 == h, jnp.sum(dsk[i * BLOCK:(i + 1) * BLOCK]), 0.0)
            dq2 = _mm(dsm, k2)
            for i, p in enumerate(pairs):
                dq_ref[:, p * LANES:(p + 1) * LANES] = _unstack_heads(dq2, i)
            for acc, carry, ref in ((_mm_ta(dsm, qst), ck, dk_ref), (_mm_ta(pn, dost), cv, dv_ref)):
                folded = acc + pltpu.roll(acc, 64, 1)
                ref[hk] = folded[BLOCK:] + carry[hk]
                carry[hk] = folded[:BLOCK]
        ds_ref[...] += dsink

    prev = pl.BlockSpec((N_KV_HEADS, BLOCK, LANES), lambda n: (0, jnp.maximum(rev(n) - 1, 0), 0))
    own = pl.BlockSpec((N_KV_HEADS, BLOCK, LANES), lambda n: (0, rev(n), 0))
    rows = lambda cols: pl.BlockSpec((BLOCK, cols), lambda n: (rev(n), 0))
    heads = jax.ShapeDtypeStruct((N_KV_HEADS, t, LANES), F32)
    return pl.pallas_call(
        body, name=name, grid=(nb,),
        out_shape=(jax.ShapeDtypeStruct((t, d_model), F32), heads, heads, jax.ShapeDtypeStruct((1, LANES), F32)),
        in_specs=[pl.BlockSpec(memory_space=pltpu.SMEM), rows(d_model), prev, own, prev, own, rows(LANES), rows(d_model)],
        out_specs=(rows(d_model), own, own, _const((1, LANES))),
        scratch_shapes=[pltpu.VMEM((N_KV_HEADS, BLOCK, LANES), F32), pltpu.VMEM((N_KV_HEADS, BLOCK, LANES), F32)],
        compiler_params=_params(),
    )(sinks, qs, kdup, kdup, vdup, vdup, lse, do)


def _oproj_fwd(x, o, w_o, post_g, layer, j, name):
    t, d_model = x.shape
    tm = _row_tile(t, PROJ_TILE)

    def body(x_ref, o_ref, w_ref, g_ref, xo_ref, mo_ref):
        mo = _mm(o_ref[...], w_ref[...])
        mo_ref[...] = mo
        xo_ref[...] = x_ref[...] + _rms_fwd(mo, g_ref[...])

    full = jax.ShapeDtypeStruct((t, d_model), F32)
    return pl.pallas_call(
        body, name=name, grid=(t // tm,), out_shape=(full, full),
        in_specs=[_rows(tm, d_model), _rows(tm, d_model), _const((None, d_model, d_model), (j, 0, 0)),
                  _const((None, 1, d_model), (layer, 0, 0))],
        out_specs=(_rows(tm, d_model), _rows(tm, d_model)), compiler_params=_params(),
    )(x, o, w_o, post_g)


def _oproj_bwd(dx, mo, o, w_o, post_g, layer, j, name):
    t, d_model = dx.shape
    tm = _row_tile(t, PROJ_TILE)
    nt = t // tm

    def body(dx_ref, mo_ref, o_ref, w_ref, g_ref, do_ref, dw_ref, dg_ref, acc):
        i = pl.program_id(0)

        @pl.when(i == 0)
        def _():
            dg_ref[...] = jnp.zeros_like(dg_ref)
            acc[...] = jnp.zeros_like(acc)

        dmo, dg = _rms_bwd(mo_ref[...], g_ref[...], dx_ref[...])
        dg_ref[...] += dg
        dmo = dmo.astype(_MXU_DTYPE)
        acc[...] += _mm_ta(o_ref[...], dmo)
        do_ref[...] = _mm_tb(dmo, w_ref[...]).astype(do_ref.dtype)

        @pl.when(i == nt - 1)
        def _():
            dw_ref[...] = acc[...].astype(dw_ref.dtype)

    return pl.pallas_call(
        body, name=name, grid=(nt,),
        out_shape=(jax.ShapeDtypeStruct((t, d_model), _ACT_DTYPE), jax.ShapeDtypeStruct((d_model, d_model), _WIRE_DTYPE),
                   jax.ShapeDtypeStruct((1, d_model), F32)),
        in_specs=[_rows(tm, d_model), _rows(tm, d_model), _rows(tm, d_model), _const((None, d_model, d_model), (j, 0, 0)),
                  _const((None, 1, d_model), (layer, 0, 0))],
        out_specs=(_rows(tm, d_model), _resident((d_model, d_model), (0, 0)), _const((1, d_model))),
        scratch_shapes=[pltpu.VMEM((d_model, d_model), F32)],
        compiler_params=_params(),
    )(dx, mo, o, w_o, post_g)


def _loss_grad(y, target, name):
    t, d_model = y.shape
    tm = _row_tile(t, PROJ_TILE)

    def body(y_ref, t_ref, dy_ref, loss_ref):
        i = pl.program_id(0)

        @pl.when(i == 0)
        def _():
            loss_ref[...] = jnp.zeros_like(loss_ref)

        err = y_ref[...] - t_ref[...]
        dy_ref[...] = err / d_model
        loss_ref[...] += 0.5 * jnp.sum(jnp.mean(err * err, axis=-1, keepdims=True), axis=0, keepdims=True)

    return pl.pallas_call(
        body, name=name, grid=(t // tm,),
        out_shape=(jax.ShapeDtypeStruct((t, d_model), F32), jax.ShapeDtypeStruct((1, 1), F32)),
        in_specs=[_rows(tm, d_model), _rows(tm, d_model)], out_specs=(_rows(tm, d_model), _const((1, 1))),
        compiler_params=_params(),
    )(y, target)


def _mesh_position():
    return lax.axis_index("x"), lax.axis_index("y"), lax.axis_index("c")


def _block_of(px, py, pc):
    return 4 * px + 2 * py + pc


def _at_block(ref, axis, block):
    return ref.at[(slice(None),) * axis + (block,)]


def _all_gather(shards, axes, name):
    n = len(shards)

    def body(*refs):
        srcs, outs = refs[:n], refs[n:2 * n]
        send_sems, recv_sems, local_sems = refs[2 * n:]
        x, y, c = _mesh_position()
        me, sibling = (x, y, c), (x, y, 1 - c)
        chips = [(1 - x, y), (x, 1 - y), (1 - x, 1 - y)]

        def blk(i, pos):
            return _at_block(outs[i], axes[i], _block_of(*pos))

        def copy(i, k, block, to, src=None):
            return pltpu.make_async_remote_copy(
                src_ref=blk(i, block) if src is None else src, dst_ref=blk(i, block),
                send_sem=send_sems.at[i, k], recv_sem=recv_sems.at[i, k], device_id=to, device_id_type=MESH)

        mine = [pltpu.make_async_copy(srcs[i], blk(i, me), local_sems.at[i]) for i in range(n)]
        for cp in mine:
            cp.start()
        sent = []
        for i in range(n):
            sent += [copy(i, 1 + k, me, (*chip, c), src=srcs[i]) for k, chip in enumerate(chips)]
            sent.append(copy(i, 0, me, sibling, src=srcs[i]))
        for cp in sent:
            cp.start()
        for i in range(n):
            for k, chip in enumerate(chips):
                copy(i, 1 + k, (*chip, c), me).wait_recv()
                passed = copy(i, 4 + k, (*chip, c), sibling)
                passed.start()
                sent.append(passed)
        for i in range(n):
            copy(i, 0, sibling, me).wait_recv()
            for k, chip in enumerate(chips):
                copy(i, 4 + k, (*chip, 1 - c), me).wait_recv()
        for cp in sent:
            cp.wait_send()
        for cp in mine:
            cp.wait()

    hbm = pl.BlockSpec(memory_space=pl.ANY)
    return pl.pallas_call(
        body, name=name,
        out_shape=tuple(jax.ShapeDtypeStruct(s.shape[:a] + (N_DEV,) + s.shape[a:], s.dtype) for s, a in zip(shards, axes)),
        in_specs=[hbm] * n, out_specs=(hbm,) * n,
        scratch_shapes=[pltpu.SemaphoreType.DMA((n, 7)), pltpu.SemaphoreType.DMA((n, 7)), pltpu.SemaphoreType.DMA((n,))],
    )(*shards)


GATHER, SCATTER, GATHER_CHIPS, GATHER_SIBLING = "gather", "scatter", "gather_chips", "gather_sibling"
COPIES = {GATHER: N_DEV - 1, SCATTER: N_DEV - 1, GATHER_CHIPS: 4, GATHER_SIBLING: 3}


def _land_shape(kind, s, axis):
    if kind == SCATTER:
        return (N_DEV,) + s.shape[:axis] + s.shape[axis + 1:]
    return s.shape[:axis] + (N_DEV,) + s.shape[axis:]


def _plan(kind, srcs, lands, axes):
    x, y, c = _mesh_position()
    my_block = _block_of(x, y, c)
    flips = {GATHER_CHIPS: (1, 4, 2, 6), GATHER_SIBLING: (4, 2, 6)}.get(kind, range(1, N_DEV))
    others = [(1 - x if k & 4 else x, 1 - y if k & 2 else y, 1 - c if k & 1 else c) for k in flips]
    remote = []
    for src, land, axis in zip(srcs, lands, axes):
        if kind == SCATTER:
            mine = land.at[my_block]
            remote += [(_at_block(src, axis, _block_of(*peer)), mine, peer, land.at[_block_of(*peer)]) for peer in others]
        elif kind == GATHER_SIBLING:
            for px, py, _ in others:
                mine, theirs = _at_block(land, axis, _block_of(px, py, c)), _at_block(land, axis, _block_of(px, py, 1 - c))
                remote.append((mine, mine, (x, y, 1 - c), theirs))
        else:
            mine = _at_block(land, axis, my_block)
            remote += [(src, mine, peer, _at_block(land, axis, _block_of(*peer))) for peer in others]
    return remote


def _remote(src, dst, send_sems, recv_sems, k, peer):
    return pltpu.make_async_remote_copy(src_ref=src, dst_ref=dst, send_sem=send_sems.at[k], recv_sem=recv_sems.at[k],
                                        device_id=peer, device_id_type=MESH)


_HBM = pl.BlockSpec(memory_space=pltpu.HBM)
_SEM = pl.BlockSpec(memory_space=pltpu.SEMAPHORE)
_SPLIT = dict(has_side_effects=pltpu.SideEffectType.DATAFLOW_SIDE_EFFECTING)


def _landing_zone(kind, s, axis, me):
    land = lax.empty(_land_shape(kind, s, axis), s.dtype)
    if kind == SCATTER:
        return lax.dynamic_update_slice_in_dim(land, lax.dynamic_slice_in_dim(s, me, 1, axis).reshape((1,) + land.shape[1:]), me, 0)
    return lax.dynamic_update_slice_in_dim(land, jnp.expand_dims(s, axis), me, axis)


def _exchange_start(kind, arrays, axes, after, name):
    n = len(arrays)
    if kind == GATHER_SIBLING:
        passed = list(arrays)
    else:
        me = _block_of(*_mesh_position())
        passed = list(arrays) + [_landing_zone(kind, s, a, me) for s, a in zip(arrays, axes)]
    n_sems = n * COPIES[kind]

    def body(*refs):
        land_refs = refs[len(passed) - n:len(passed)]
        send_sems, recv_sems = refs[len(passed) + 1], refs[len(passed) + 2]
        token = refs[-1]
        for k, (src, dst, peer, _) in enumerate(_plan(kind, refs[:n], land_refs, axes)):
            _remote(src, dst, send_sems, recv_sems, k, peer).start()
        token[...] = jnp.zeros_like(token)

    out = pl.pallas_call(
        body, name=name,
        out_shape=(pltpu.SemaphoreType.DMA((n_sems,)), pltpu.SemaphoreType.DMA((n_sems,)),
                   *[pltpu.HBM(a.shape, a.dtype) for a in passed], jax.ShapeDtypeStruct((8, LANES), F32)),
        in_specs=[_HBM] * len(passed) + [pl.BlockSpec(memory_space=pl.ANY)],
        out_specs=(_SEM, _SEM, *[_HBM] * len(passed), pl.BlockSpec(memory_space=pltpu.VMEM)),
        input_output_aliases={i: 2 + i for i in range(len(passed))},
        compiler_params=pltpu.CompilerParams(**_SPLIT),
    )(*[pltpu.with_memory_space_constraint(a, pltpu.HBM) for a in passed], after)
    return (kind, axes, n, out[:-1]), out[-1]


def _exchange_wait(handle, after, name):
    kind, axes, n, (send_sems, recv_sems, *thru) = handle

    def body(*refs):
        land_refs = refs[len(thru) - n:len(thru)]
        send_sems, recv_sems = refs[len(thru)], refs[len(thru) + 1]
        for k, (src, _, peer, arrives) in enumerate(_plan(kind, refs[:n], land_refs, axes)):
            cp = _remote(src, arrives, send_sems, recv_sems, k, peer)
            cp.wait_send()
            cp.wait_recv()

    out = pl.pallas_call(
        body, name=name,
        out_shape=tuple(pltpu.HBM(a.shape, a.dtype) for a in thru),
        in_specs=[_HBM] * len(thru) + [_SEM, _SEM, pl.BlockSpec(memory_space=pl.ANY)], out_specs=(_HBM,) * len(thru),
        input_output_aliases={i: i for i in range(len(thru))},
        compiler_params=pltpu.CompilerParams(**_SPLIT),
    )(*thru, send_sems, recv_sems, after)
    return out[len(thru) - n:]


def _adamw_math(w, g, m, v):
    m = ADAM_B1 * m + (1.0 - ADAM_B1) * g
    v = ADAM_B2 * v + (1.0 - ADAM_B2) * jnp.square(g)
    m_hat = m / (1.0 - ADAM_B1 ** ADAM_STEP)
    v_hat = v / (1.0 - ADAM_B2 ** ADAM_STEP)
    delta = -ADAM_LR * (m_hat / (jnp.sqrt(v_hat) + ADAM_EPS) + ADAM_WD * w)
    return delta, m, v


def _update_tile(rows):
    if rows <= 512:
        return rows
    for tr in (512, 384, 352, 256, 176, 128, 64, 32, 16):
        if rows % tr == 0:
            return tr
    raise ValueError(f"{rows} rows do not tile")


def _adamw(parts, w, m, v, slab, so_far, name):
    rows, c = w.shape
    r = parts.shape[1]
    tr = _update_tile(r)
    first = slab * (r // tr)
    if so_far is None:
        so_far = tuple(lax.empty((rows, c), F32) for _ in range(4))

    def body(p_ref, w_ref, m_ref, v_ref, *refs):
        g_ref, d_ref, mo_ref, vo_ref = refs[4:]
        g = p_ref[0].astype(F32)
        for s in range(1, N_DEV):
            g = g + p_ref[s].astype(F32)
        g_ref[...] = g
        d_ref[...], mo_ref[...], vo_ref[...] = _adamw_math(w_ref[...], g, m_ref[...], v_ref[...])

    out = jax.ShapeDtypeStruct((rows, c), F32)
    tile = pl.BlockSpec((tr, c), lambda i: (first + i, 0))
    return pl.pallas_call(
        body, name=name, grid=(r // tr,), out_shape=(out,) * 4,
        in_specs=[pl.BlockSpec((N_DEV, tr, c), lambda i: (0, i, 0))] + [tile] * 3 + [pl.BlockSpec(memory_space=pl.ANY)] * 4,
        out_specs=(tile,) * 4, input_output_aliases={4 + k: k for k in range(4)}, compiler_params=_params(),
    )(parts, w, m, v, *so_far)


def _adamw_small(parts, picks, weights, name):
    n = len(parts)

    def body(*refs):
        p_refs, wmv, outs = refs[:n], refs[n:4 * n], refs[4 * n:]
        me = _block_of(*_mesh_position())
        for i in range(n):
            g = picks[i](p_refs[i], 0, me)
            for s in range(1, N_DEV):
                g = g + picks[i](p_refs[i], s, me)
            w_ref, m_ref, v_ref = wmv[3 * i:3 * i + 3]
            g_ref, d_ref, mo_ref, vo_ref = outs[4 * i:4 * i + 4]
            g_ref[...] = g
            d_ref[...], mo_ref[...], vo_ref[...] = _adamw_math(w_ref[...], g, m_ref[...], v_ref[...])

    flat = [a for wmv in weights for a in wmv]
    out = pl.pallas_call(
        body, name=name,
        out_shape=tuple(jax.ShapeDtypeStruct(w.shape, F32) for w, _, _ in weights for _ in range(4)),
        compiler_params=pltpu.CompilerParams(vmem_limit_bytes=VMEM_LIMIT),
    )(*parts, *flat)
    return [tuple(out[4 * i:4 * i + 4]) for i in range(n)]


def kernel(x, positions, mix_pre_g, mix_post_g, pool_w, pool_scale, kv_norm_g, w_kv, w_q, w_o, sinks, ffn_pre_g, ffn_post_g, ffn_w_in, ffn_conv_w, ffn_conv_b, ffn_w_out, loss_target, m_mix_pre_g, m_mix_post_g, m_pool_w, m_pool_scale, m_kv_norm_g, m_w_kv, m_w_q, m_w_o, m_sinks, m_ffn_pre_g, m_ffn_post_g, m_ffn_w_in, m_ffn_conv_w, m_ffn_conv_b, m_ffn_w_out, v_mix_pre_g, v_mix_post_g, v_pool_w, v_pool_scale, v_kv_norm_g, v_w_kv, v_w_q, v_w_o, v_sinks, v_ffn_pre_g, v_ffn_post_g, v_ffn_w_in, v_ffn_conv_w, v_ffn_conv_b, v_ffn_w_out):
    depth, d_model = mix_pre_g.shape
    n_a = pool_w.shape[0]
    n_b = w_q.shape[0]
    t = x.shape[1]
    fs = ffn_w_in.shape[2]
    half = N_DEV // 2
    n_heads = d_model // HEAD_DIM
    x0 = x.reshape(t, d_model)
    target = loss_target.reshape(t, d_model)

    inv_freq = 1.0 / (ROPE_THETA ** (jnp.arange(0, HEAD_DIM, 2, dtype=F32) / HEAD_DIM))
    ang = positions.reshape(t).astype(F32)[:, None] * inv_freq
    cos, sin = jnp.cos(ang), jnp.sin(ang)
    cos = jnp.tile(cos, (1, 2 * LANES // HEAD_DIM))
    ssin = jnp.tile(jnp.concatenate([-sin, sin], axis=1), (1, LANES // HEAD_DIM))

    wire = lambda a: a.astype(_WIRE_DTYPE)
    by_hidden = lambda a: a.transpose(0, 2, 1)
    w_in_b, w_out_b = wire(by_hidden(ffn_w_in)), wire(ffn_w_out)
    pool_w_g, w_in_0, w_out_0, pool_scale_g, conv_w_g = _all_gather(
        [wire(pool_w), w_in_b[:1], w_out_b[:1], pool_scale, ffn_conv_w], [2, 1, 1, 0, 0], "gather_first")
    groups = []
    for l in range(1, depth):
        if l == n_a:
            groups.append(("attn", [wire(w_kv), wire(w_q), wire(w_o)], [0, 1, 1], l))
        groups.append((l, [w_in_b[l:l + 1], w_out_b[l:l + 1]], [1, 1], l))
    over_ici, to_sibling, tokens, after = {}, {}, [], w_in_0
    for key, shards, axes, _ in groups:
        over_ici[key], after = _exchange_start(GATHER_CHIPS, shards, axes, after, f"gather_chips_{key}")
        tokens.append(after)
    started = functools.reduce(lambda a, b: a + b, [tk[0, 0] for tk in tokens])

    def pass_on(layer, after):
        sent = jnp.zeros((), F32)
        for key, _, axes, first in groups:
            if first == layer:
                lands = _exchange_wait(over_ici[key], after, f"gather_chips_wait_{key}")
                to_sibling[key], tk = _exchange_start(GATHER_SIBLING, lands, axes, after, f"gather_sibling_{key}")
                sent = sent + tk[0, 0]
        return sent

    w_in_l, w_out_l = {0: w_in_0}, {0: w_out_0.reshape(1, half * fs, d_model)}
    pool_scale_f = pool_scale_g.transpose(1, 0, 2).reshape(n_a, 1, d_model)
    conv_w_f = conv_w_g.transpose(1, 2, 0, 3)
    pool_w_f = pool_w_g.reshape(n_a, len(POOL_WINDOWS), d_model // len(POOL_WINDOWS), -1)
    conv_b_f = ffn_conv_b.reshape(depth, N_DEV, fs)
    g3 = lambda a: a.reshape(a.shape[0], 1, a.shape[1])
    mix_pre, mix_post, ffn_pre, ffn_post = g3(mix_pre_g) + started, g3(mix_post_g), g3(ffn_pre_g), g3(ffn_post_g)
    kv_g = kv_norm_g.reshape(1, d_model)
    w_kv_f = w_q_f = w_o_f = None

    saved = []
    xc = x0
    kdup = vdup = x_kv = None
    for l in range(depth):
        x_in = xc
        if l < n_a:
            x_mid, dsave, yu = _pool_fwd(x_in, mix_pre, mix_post, pool_w_f, pool_scale_f, l, f"pool_fwd_{l}")
            mixer = (dsave, yu)
        else:
            j = l - n_a
            if j == 0:
                x_kv = x_in
                w_kv_g, w_q_g, w_o_g = _exchange_wait(to_sibling["attn"], x_in, "gather_sibling_wait_attn")
                w_kv_f = w_kv_g.reshape(d_model, -1)
                w_q_f = w_q_g.reshape(n_b, d_model, d_model)
                w_o_f = w_o_g.reshape(n_b, d_model, d_model)
                kdup, vdup = _kv_fwd(x_kv, kv_g, w_kv_f, cos, ssin, "kv_fwd")
            qs = _q_fwd(x_in, mix_pre, w_q_f, cos, ssin, l, j, f"q_fwd_{l}")
            o, lse = _attn_fwd(qs, kdup, vdup, sinks, j, f"attn_fwd_{l}")
            x_mid, mo = _oproj_fwd(x_in, o, w_o_f, mix_post, l, j, f"oproj_fwd_{l}")
            mixer = (qs, o, lse, mo)
        if l > 0:
            w_in_l[l], w_out_g = _exchange_wait(to_sibling[l], x_mid, f"gather_sibling_wait_{l}")
            w_out_l[l] = w_out_g.reshape(1, half * fs, d_model)
        xc, u, uc, f = _ffn_fwd(x_mid, ffn_pre, ffn_post, w_in_l[l], conv_w_f, conv_b_f, w_out_l[l], l, 0, f"ffn_fwd_{l}")
        saved.append((x_in, x_mid, u, uc, f, mixer))
        if l + 1 < depth:
            mix_pre = mix_pre + pass_on(l + 1, xc)

    dx, loss_part = _loss_grad(xc, target, "loss")

    gconv_w, gconv_b = [None] * depth, [None] * depth
    gmix_pre, gmix_post, gffn_pre, gffn_post = [None] * depth, [None] * depth, [None] * depth, [None] * depth
    gpool_scale, gsinks = [None] * n_a, [None] * n_b
    dks, dvs = [], []
    gkv_g = None
    whole = lambda cols: pl.BlockSpec((t, cols), lambda c: (0, 0), pipeline_mode=pl.Buffered(1))
    per_out = lambda cols: pl.BlockSpec((None, t, cols), lambda c: (c, 0, 0))
    by_dev = lambda g: g.reshape(N_DEV, -1, g.shape[-1])
    def small_grads():
        cat = lambda rows: jnp.concatenate(rows, axis=0)
        row = lambda a: a.reshape(1, -1)
        everything = lambda ref, s, me: ref[s]
        lanes = pool_scale.shape[1]
        return [("mix_pre_g", cat(gmix_pre), everything, (mix_pre_g, m_mix_pre_g, v_mix_pre_g)),
                ("mix_post_g", cat(gmix_post), everything, (mix_post_g, m_mix_post_g, v_mix_post_g)),
                ("kv_norm_g", gkv_g, everything, (row(kv_norm_g), row(m_kv_norm_g), row(v_kv_norm_g))),
                ("sinks", cat(gsinks), lambda ref, s, me: ref[s, :, pl.ds(0, n_heads)], (sinks, m_sinks, v_sinks)),
                ("ffn_pre_g", cat(gffn_pre), everything, (ffn_pre_g, m_ffn_pre_g, v_ffn_pre_g)),
                ("ffn_post_g", cat(gffn_post), everything, (ffn_post_g, m_ffn_post_g, v_ffn_post_g)),
                ("ffn_conv_b", jnp.stack(gconv_b).reshape(depth, N_DEV * fs), everything, (ffn_conv_b, m_ffn_conv_b, v_ffn_conv_b)),
                ("pool_scale", cat(gpool_scale), lambda ref, s, me: ref[s, :, pl.ds(pl.multiple_of(me * lanes, lanes), lanes)],
                 (pool_scale, m_pool_scale, v_pool_scale)),
                ("ffn_conv_w", jnp.stack(gconv_w).transpose(0, 2, 1, 3), lambda ref, s, me: ref[s, :, me],
                 (ffn_conv_w, m_ffn_conv_w, v_ffn_conv_w))]

    flying = []

    def launch(going, after, name):
        handle, token = _exchange_start(SCATTER, [g for _, _, g, _ in going], [a for _, _, _, a in going], after, name)
        flying.append(([(nm, slab) for nm, slab, _, _ in going], handle))
        return token

    post = mix_post
    ffn_post_b = ffn_post
    token = None
    for l in reversed(range(depth)):
        x_in, x_mid, u, uc, f, mixer = saved[l]
        duc, gout, gconv_b[l], gffn_post[l] = _ffn_bwd_out(dx, f, ffn_post_b, uc, conv_b_f, w_out_l[l], l, 0, f"ffn_bwd_out_{l}")
        going = [("ffn_w_out", l, by_dev(gout), 0)]
        pre = ffn_pre
        if l == 0:
            token = launch(going, gout, "scatter_start_0_out")
            going = []
            pre = pre + token[0, 0]
        dx, du, hf, gconv_w[l], gffn_pre[l] = _ffn_bwd_in(dx, x_mid, pre, duc, u, conv_w_f, w_in_l[l], l, 0, f"ffn_bwd_in_{l}")
        gin = _tn_matmul(du, hf, per_out(fs), whole(d_model), N_DEV, fs, d_model, f"grad_w_in_{l}", _WIRE_DTYPE)
        going.append(("ffn_w_in", l, gin, 0))
        if l == 0:
            token = launch(going, dx, "scatter_start_0_in")
            going = []
            post = post + token[0, 0]
        if l < n_a:
            dsave, yu = mixer
            dx, gpool, gpool_scale[l], gmix_pre[l], gmix_post[l] = _pool_bwd(
                dx, x_in, mix_pre, post, dsave, yu, pool_w_f, pool_scale_f, l, f"pool_bwd_{l}")
            going.append(("pool_w", l, gpool.reshape(len(POOL_WINDOWS), N_DEV, -1, gpool.shape[-1]), 1))
        else:
            j = l - n_a
            qs, o, lse, mo = mixer
            do, go, gmix_post[l] = _oproj_bwd(dx, mo, o, w_o_f, post, l, j, f"oproj_bwd_{l}")
            dqs, dk, dv, gsinks[j] = _attn_bwd(qs, kdup, vdup, sinks, lse, do, j, f"attn_bwd_{l}")
            dks.append(dk)
            dvs.append(dv)
            dx, gq, gmix_pre[l] = _q_bwd(dx, dqs, x_in, mix_pre, w_q_f, cos, ssin, l, j, f"q_bwd_{l}")
            if j == 0:
                dx, gkv, gkv_g = _kv_bwd(dx, x_kv, kv_g, w_kv_f, cos, ssin, dks, dvs, "kv_bwd")
                going.append(("w_kv", 0, by_dev(gkv), 0))
            going += [("w_o", j, by_dev(go), 0), ("w_q", j, by_dev(gq), 0)]
        after = dx
        if l == 0:
            small = small_grads()
            leaving = [g for _, g, _, _ in small] + [jnp.broadcast_to(loss_part, (1, LANES))]
            small_flight, after = _exchange_start(GATHER, leaving, [0] * len(leaving), dx, "gather_small_grads")
        token = launch(going, after, f"scatter_start_{l}")
        ffn_post_b = ffn_post_b + token[0, 0]

    grad_x = dx.reshape(x.shape)

    shard = {"pool_w": (pool_w, m_pool_w, v_pool_w), "w_kv": (w_kv, m_w_kv, v_w_kv), "w_q": (w_q, m_w_q, v_w_q),
             "w_o": (w_o, m_w_o, v_w_o), "ffn_w_in": tuple(by_hidden(a) for a in (ffn_w_in, m_ffn_w_in, v_ffn_w_in)),
             "ffn_w_out": (ffn_w_out, m_ffn_w_out, v_ffn_w_out)}
    big = {}

    def arrive(flights, after):
        for idx, (names, handle) in flights:
            parts = _exchange_wait(handle, after, f"scatter_wait_{idx}")
            for (nm, slab), p in zip(names, parts):
                cols = p.shape[-1]
                w2, m2, v2 = (a.reshape(-1, cols) for a in shard[nm])
                big[nm] = _adamw(p.reshape(N_DEV, -1, cols), w2, m2, v2, slab, big.get(nm), f"adamw_{nm}_{slab}")
                after = big[nm][0]
        return after

    done = arrive(list(enumerate(flying)), token)
    *small_parts, loss_parts = _exchange_wait(small_flight, done, "gather_small_grads_wait")
    loss = jnp.sum(loss_parts[:, 0, 0])
    upd = _adamw_small(small_parts, [pick for _, _, pick, _ in small], [wmv for _, _, _, wmv in small], "adamw_small")
    res = {nm: tuple(r.reshape(shard[nm][0].shape) for r in out) for nm, out in big.items()}
    res["ffn_w_in"] = tuple(by_hidden(r) for r in res["ffn_w_in"])
    for (nm, _, _, _), out in zip(small, upd):
        res[nm] = tuple(a.reshape(kv_norm_g.shape) for a in out) if nm == "kv_norm_g" else out

    order = ["mix_pre_g", "mix_post_g", "pool_w", "pool_scale", "kv_norm_g", "w_kv", "w_q", "w_o", "sinks", "ffn_pre_g",
             "ffn_post_g", "ffn_w_in", "ffn_conv_w", "ffn_conv_b", "ffn_w_out"]
    return (loss, grad_x, *[res[nm][0] for nm in order], *[res[nm][1] for nm in order],
            *[res[nm][2] for nm in order], *[res[nm][3] for nm in order])
```

```python
import functools
import math

import jax
import jax.numpy as jnp
from jax import lax
from jax.experimental import pallas as pl
from jax.experimental.pallas import tpu as pltpu

F32 = jnp.float32
_MXU_DTYPE = jnp.bfloat16
_ACT_DTYPE = jnp.bfloat16
_WIRE_DTYPE = jnp.bfloat16
_SAVE_DTYPE = jnp.bfloat16

N_DEV = 8
POOL_WINDOWS = (2, 4, 8, 16)
POOL_HALO = 32
assert POOL_WINDOWS == tuple(2 ** (g + 1) for g in range(len(POOL_WINDOWS))) and 8 * len(POOL_WINDOWS) <= POOL_HALO
HEAD_DIM = 64
N_KV_HEADS = 4
WINDOW = 128
BLOCK = 128
LANES = 128
ROPE_THETA = 10000.0
ATTN_SCALE = 1.0 / math.sqrt(HEAD_DIM)
NEG_INF = -1e30
RMS_EPS = 1e-6
CONV_HALO = 8
SAVE_HALO = 16
PROJ_TILE = 512
ADAM_LR = 0.001
ADAM_B1 = 0.9
ADAM_B2 = 0.999
ADAM_EPS = 1e-08
ADAM_WD = 0.01
ADAM_STEP = 10
VMEM_LIMIT = 56 * 1024 * 1024
MESH = pl.DeviceIdType.MESH


def _params(n_axes=1, vmem=VMEM_LIMIT):
    return pltpu.CompilerParams(dimension_semantics=("arbitrary",) * n_axes, vmem_limit_bytes=vmem)


def _resident(shape, index):
    return pl.BlockSpec(shape, lambda *_: index, pipeline_mode=pl.Buffered(1))


def _const(shape, index=None):
    index = (0,) * len(shape) if index is None else index
    return pl.BlockSpec(shape, lambda *_: index)


def _rows(tm, cols):
    return pl.BlockSpec((tm, cols), lambda i: (i, 0))


def _row_tile(t, most=256):
    for tm in (512, 256, 128, 64, 32, 16, 8):
        if tm <= most and t % tm == 0:
            return tm
    raise ValueError(f"sequence length {t} is not a multiple of 8")


def _mm(a, b):
    return jnp.dot(a.astype(_MXU_DTYPE), b.astype(_MXU_DTYPE), preferred_element_type=F32)


def _mm_tb(a, b):
    return lax.dot_general(a.astype(_MXU_DTYPE), b.astype(_MXU_DTYPE), (((1,), (1,)), ((), ())),
                           preferred_element_type=F32)


def _mm_ta(a, b):
    return lax.dot_general(a.astype(_MXU_DTYPE), b.astype(_MXU_DTYPE), (((0,), (0,)), ((), ())),
                           preferred_element_type=F32)


def _rms_r(x):
    return lax.rsqrt(jnp.mean(x * x, axis=-1, keepdims=True) + RMS_EPS)


def _rms_fwd(x, g):
    return (x * _rms_r(x)) * g


def _rms_bwd(x, g, dy):
    r = _rms_r(x)
    xh = x * r
    dg = jnp.sum(dy * xh, axis=0, keepdims=True)
    dxh = dy * g
    dx = r * (dxh - xh * jnp.mean(dxh * xh, axis=-1, keepdims=True))
    return dx, dg


_GELU_C = math.sqrt(2.0 / math.pi)


def _gelu_parts(z):
    z2 = z * z
    e = jnp.exp(z * (-2.0 * _GELU_C - (2.0 * _GELU_C * 0.044715) * z2))
    cdf = pl.reciprocal(1.0 + e, approx=False)
    dz = cdf + (z * (cdf * (1.0 - cdf))) * (2.0 * _GELU_C + (6.0 * _GELU_C * 0.044715) * z2)
    return cdf, dz


def _lane_iota(shape):
    return lax.broadcasted_iota(jnp.int32, shape, len(shape) - 1)


def _rope_partner(xb):
    first = (_lane_iota(xb.shape) & 32) == 0
    return jnp.where(first, pltpu.roll(xb, LANES - 32, 1), pltpu.roll(xb, 32, 1))


def _rope_fwd(xb, cos, ssin):
    return xb * cos + _rope_partner(xb) * ssin


def _rope_bwd(dyb, cos, ssin):
    return dyb * cos - _rope_partner(dyb) * ssin


def _low_half(shape):
    return (_lane_iota(shape) & 64) == 0


def _pool_fwd(x, pre_g, post_g, w, scale, layer, name):
    t, d_model = x.shape
    tm = _row_tile(t)
    n_groups, gc = w.shape[1], w.shape[2]

    def body(x_ref, pre_ref, post_ref, w_ref, sc_ref, xo_ref, d_ref, yu_ref, hbuf, sbuf):
        i = pl.program_id(0)

        @pl.when(i == 0)
        def _():
            hbuf[pl.ds(0, POOL_HALO), :] = jnp.zeros((POOL_HALO, d_model), F32)

        xv = x_ref[...]
        hbuf[pl.ds(POOL_HALO, tm), :] = _rms_fwd(xv, pre_ref[...])
        tok = i * tm + lax.broadcasted_iota(jnp.int32, (tm, 1), 0)
        yus = []
        for gi, wnd in enumerate(POOL_WINDOWS):
            first, reach = CONV_HALO * (gi + 1), wnd // 2
            n = POOL_HALO + tm - first
            cols = pl.ds(gi * gc, d_model - gi * gc)
            src = hbuf if gi == 0 else sbuf.at[gi - 1]
            level = src[pl.ds(first, n), cols] + src[pl.ds(first - reach, n), cols]
            if gi + 1 < len(POOL_WINDOWS):
                sbuf[gi, pl.ds(first, n), cols] = level
            h = hbuf[pl.ds(POOL_HALO, tm), pl.ds(gi * gc, gc)]
            cnt = jnp.minimum(tok + 1, wnd).astype(F32)
            dg = level[POOL_HALO - first:, :gc] / cnt - h
            d_ref[:, pl.ds(gi * gc, gc)] = dg.astype(d_ref.dtype)
            yus.append(_mm(dg, w_ref[gi]))
        hbuf[pl.ds(0, POOL_HALO), :] = hbuf[pl.ds(tm, POOL_HALO), :]
        yu = jnp.concatenate(yus, axis=1)
        yu_ref[...] = yu
        xo_ref[...] = xv + _rms_fwd(yu * sc_ref[...], post_ref[...])

    return pl.pallas_call(
        body, name=name, grid=(t // tm,),
        out_shape=(jax.ShapeDtypeStruct((t, d_model), F32), jax.ShapeDtypeStruct((t, d_model), _ACT_DTYPE),
                   jax.ShapeDtypeStruct((t, d_model), F32)),
        in_specs=[_rows(tm, d_model), _const((None, 1, d_model), (layer, 0, 0)), _const((None, 1, d_model), (layer, 0, 0)),
                  _const((None, n_groups, gc, gc), (layer, 0, 0, 0)), _const((None, 1, d_model), (layer, 0, 0))],
        out_specs=(_rows(tm, d_model), _rows(tm, d_model), _rows(tm, d_model)),
        scratch_shapes=[pltpu.VMEM((POOL_HALO + tm, d_model), F32),
                        pltpu.VMEM((len(POOL_WINDOWS) - 1, POOL_HALO + tm, d_model), F32)],
        compiler_params=_params(),
    )(x, pre_g, post_g, w, scale)


def _pool_bwd(dx, x, pre_g, post_g, d, yu, w, scale, layer, name):
    t, d_model = x.shape
    tm = _row_tile(t)
    nt = t // tm
    n_groups, gc = w.shape[1], w.shape[2]
    rev = lambda i: (nt - 1 - i, 0)
    rows = pl.BlockSpec((tm, d_model), rev)

    def body(dx_ref, x_ref, pre_ref, post_ref, d_ref, yu_ref, w_ref, sc_ref,
             dxi_ref, dyu_ref, dsc_ref, dpre_ref, dpost_ref, zbuf, sbuf):
        i = pl.program_id(0)

        @pl.when(i == 0)
        def _():
            zbuf[pl.ds(tm, POOL_HALO), :] = jnp.zeros((POOL_HALO, d_model), F32)
            dsc_ref[...] = jnp.zeros_like(dsc_ref)
            dpre_ref[...] = jnp.zeros_like(dpre_ref)
            dpost_ref[...] = jnp.zeros_like(dpost_ref)

        dxo = dx_ref[...]
        yuv = yu_ref[...]
        sc = sc_ref[...]
        dm, dpost = _rms_bwd(yuv * sc, post_ref[...], dxo)
        dpost_ref[...] += dpost
        dsc_ref[...] += jnp.sum(dm * yuv, axis=0, keepdims=True)
        dyu = dm * sc
        dyu_ref[...] = dyu.astype(dyu_ref.dtype)
        tok = (nt - 1 - i) * tm + lax.broadcasted_iota(jnp.int32, (tm, 1), 0)
        dds = []
        for gi, wnd in enumerate(POOL_WINDOWS):
            cols = pl.ds(gi * gc, gc)
            dd = _mm_tb(dyu[:, gi * gc:(gi + 1) * gc], w_ref[gi])
            cnt = jnp.minimum(tok + 1, wnd).astype(F32)
            zbuf[pl.ds(0, tm), cols] = dd / cnt
            dds.append(dd)
        dhs = []
        for gi, wnd in enumerate(POOL_WINDOWS):
            reach = wnd // 2
            n = tm + POOL_HALO - CONV_HALO * (gi + 1)
            cols = pl.ds(gi * gc, d_model - gi * gc)
            src = zbuf if gi == 0 else sbuf.at[gi - 1]
            level = src[pl.ds(0, n), cols] + src[pl.ds(reach, n), cols]
            if gi + 1 < len(POOL_WINDOWS):
                sbuf[gi, pl.ds(0, n), cols] = level
            dhs.append(level[:tm, :gc] - dds[gi])
        zbuf[pl.ds(tm, POOL_HALO), :] = zbuf[pl.ds(0, POOL_HALO), :]
        dh = jnp.concatenate(dhs, axis=1)
        dxp, dpre = _rms_bwd(x_ref[...], pre_ref[...], dh)
        dpre_ref[...] += dpre
        dxi_ref[...] = dxo + dxp

    vec = jax.ShapeDtypeStruct((1, d_model), F32)
    return pl.pallas_call(
        body, name=name, grid=(nt,),
        out_shape=(jax.ShapeDtypeStruct((t, d_model), F32), jax.ShapeDtypeStruct((t, d_model), _ACT_DTYPE), vec, vec, vec),
        in_specs=[rows, rows, _const((None, 1, d_model), (layer, 0, 0)), _const((None, 1, d_model), (layer, 0, 0)), rows, rows,
                  _const((None, n_groups, gc, gc), (layer, 0, 0, 0)), _const((None, 1, d_model), (layer, 0, 0))],
        out_specs=(rows, rows, _const((1, d_model)), _const((1, d_model)), _const((1, d_model))),
        scratch_shapes=[pltpu.VMEM((tm + POOL_HALO, d_model), F32),
                        pltpu.VMEM((len(POOL_WINDOWS) - 1, tm + POOL_HALO, d_model), F32)],
        compiler_params=_params(),
    )(dx, x, pre_g, post_g, d, yu, w, scale)


def _conv_taps(cw_ref, s):
    return [cw_ref[k, pl.ds(s, 1), :] for k in range(3)]


def _shift_down(v, k, before):
    rolled = pltpu.roll(v, k, 0)
    row = lax.broadcasted_iota(jnp.int32, before.shape, 0)
    head = jnp.where(row < k, pltpu.roll(before, k, 0), rolled[:CONV_HALO])
    return jnp.concatenate([head, rolled[CONV_HALO:]], axis=0)


def _shift_up(v, k, after):
    rows = v.shape[0]
    rolled = pltpu.roll(v, rows - k, 0)
    row = lax.broadcasted_iota(jnp.int32, after.shape, 0)
    tail = jnp.where(row >= CONV_HALO - k, pltpu.roll(after, CONV_HALO - k, 0), rolled[rows - CONV_HALO:])
    return jnp.concatenate([rolled[:rows - CONV_HALO], tail], axis=0)


def _ffn_fwd(x, pre_g, post_g, w_in, conv_w, conv_b, w_out, layer, w_layer, name):
    t, d_model = x.shape
    tm = _row_tile(t)
    fs = w_in.shape[2]
    half = N_DEV // 2

    def body(x_ref, pre_ref, post_ref, win_ref, cw_ref, cb_ref, wout_ref, xo_ref, u_ref, uc_ref, f_ref, carry):
        i = pl.program_id(0)

        @pl.when(i == 0)
        def _():
            carry[...] = jnp.zeros_like(carry)

        xv = x_ref[...]
        hf = _rms_fwd(xv, pre_ref[...]).astype(_MXU_DTYPE)
        f = jnp.zeros((tm, d_model), F32)
        project = lambda b: [_mm_tb(hf, win_ref[s]) for s in (b, b + half)]
        ahead = project(0)
        for b in range(half):
            us, ahead = ahead, project(b + 1) if b + 1 < half else None
            ucs = []
            for s, u in zip((b, b + half), us):
                u_ref[s] = u.astype(u_ref.dtype)
                before = carry[s]
                carry[s] = u[tm - CONV_HALO:]
                w0, w1, w2 = _conv_taps(cw_ref, s)
                uc = ((w0 * _shift_down(u, 2, before) + w1 * _shift_down(u, 1, before)) + w2 * u) + cb_ref[pl.ds(s, 1), :]
                uc_ref[s] = uc.astype(uc_ref.dtype)
                ucs.append(uc)
            gate, val = ucs
            cdf, _ = _gelu_parts(gate)
            f = f + _mm((gate * cdf) * val, wout_ref[pl.ds(b * fs, fs), :])
        f_ref[...] = f
        xo_ref[...] = xv + _rms_fwd(f, post_ref[...])

    tile3 = pl.BlockSpec((N_DEV, tm, fs), lambda i: (0, i, 0))
    saved = jax.ShapeDtypeStruct((N_DEV, t, fs), _SAVE_DTYPE)
    return pl.pallas_call(
        body, name=name, grid=(t // tm,),
        out_shape=(jax.ShapeDtypeStruct((t, d_model), F32), saved, saved, jax.ShapeDtypeStruct((t, d_model), F32)),
        in_specs=[_rows(tm, d_model), _const((None, 1, d_model), (layer, 0, 0)), _const((None, 1, d_model), (layer, 0, 0)),
                  _resident((None, N_DEV, fs, d_model), (w_layer, 0, 0, 0)), _const((None, 3, N_DEV, fs), (layer, 0, 0, 0)),
                  _const((None, N_DEV, fs), (layer, 0, 0)), _resident((None, half * fs, d_model), (w_layer, 0, 0))],
        out_specs=(_rows(tm, d_model), tile3, tile3, _rows(tm, d_model)),
        scratch_shapes=[pltpu.VMEM((N_DEV, CONV_HALO, fs), F32)],
        compiler_params=_params(),
    )(x, pre_g, post_g, w_in, conv_w, conv_b, w_out)


def _ffn_bwd_out(dx, f, post_g, uc, conv_b, w_out, layer, w_layer, name):
    t, d_model = dx.shape
    tm = _row_tile(t)
    nt = t // tm
    fs = uc.shape[2]
    half = N_DEV // 2

    def body(dx_ref, f_ref, post_ref, uc_ref, wout_ref, duc_ref, dwout_ref, dcb_ref, dpost_ref, acc):
        i = pl.program_id(0)

        @pl.when(i == 0)
        def _():
            acc[...] = jnp.zeros_like(acc)
            dcb_ref[...] = jnp.zeros_like(dcb_ref)
            dpost_ref[...] = jnp.zeros_like(dpost_ref)

        df, dpost = _rms_bwd(f_ref[...], post_ref[...], dx_ref[...])
        dpost_ref[...] += dpost
        dfm = df.astype(_MXU_DTYPE)
        project = lambda b: _mm_tb(dfm, wout_ref[pl.ds(b * fs, fs), :])
        ahead = project(0)
        for b in range(half):
            dg, ahead = ahead, project(b + 1) if b + 1 < half else None
            gate = uc_ref[b].astype(F32)
            val = uc_ref[b + half].astype(F32)
            cdf, dgelu = _gelu_parts(gate)
            ge = gate * cdf
            acc[pl.ds(b * fs, fs), :] += _mm_ta(ge * val, dfm)
            for s, dd in ((b, dg * val * dgelu), (b + half, dg * ge)):
                duc_ref[s] = dd.astype(duc_ref.dtype)
                dcb_ref[pl.ds(s, 1), :] += jnp.sum(dd, axis=0, keepdims=True)

        @pl.when(i == nt - 1)
        def _():
            dwout_ref[...] = acc[...].astype(dwout_ref.dtype)

    tile3 = pl.BlockSpec((N_DEV, tm, fs), lambda i: (0, i, 0))
    return pl.pallas_call(
        body, name=name, grid=(nt,),
        out_shape=(jax.ShapeDtypeStruct((N_DEV, t, fs), _SAVE_DTYPE), jax.ShapeDtypeStruct((half * fs, d_model), _WIRE_DTYPE),
                   jax.ShapeDtypeStruct((N_DEV, fs), F32), jax.ShapeDtypeStruct((1, d_model), F32)),
        in_specs=[_rows(tm, d_model), _rows(tm, d_model), _const((None, 1, d_model), (layer, 0, 0)), tile3,
                  _resident((None, half * fs, d_model), (w_layer, 0, 0))],
        out_specs=(tile3, _resident((half * fs, d_model), (0, 0)), _const((N_DEV, fs)), _const((1, d_model))),
        scratch_shapes=[pltpu.VMEM((half * fs, d_model), F32)],
        compiler_params=_params(),
    )(dx, f, post_g, uc, w_out)


def _ffn_bwd_in(dx, x, pre_g, duc, u, conv_w, w_in, layer, w_layer, name):
    t, d_model = dx.shape
    tm = _row_tile(t)
    nt = t // tm
    fs = duc.shape[2]
    hb = SAVE_HALO
    per_tile = tm // hb

    def body(dx_ref, x_ref, pre_ref, duc_ref, dn_ref, u_ref, cw_ref, win_ref, dxi_ref, du_ref, hf_ref, dcw_ref, dpre_ref):
        i = pl.program_id(0)

        @pl.when(i == 0)
        def _():
            dcw_ref[...] = jnp.zeros_like(dcw_ref)
            dpre_ref[...] = jnp.zeros_like(dpre_ref)

        xv = x_ref[...]
        pre = pre_ref[...]
        hf_ref[...] = _rms_fwd(xv, pre).astype(hf_ref.dtype)
        dhf = jnp.zeros((tm, d_model), F32)
        for s in range(N_DEV):
            d0 = duc_ref[s].astype(F32)
            after = jnp.where(i == nt - 1, 0.0, dn_ref[s].astype(F32)[:CONV_HALO])
            d1 = _shift_up(d0, 1, after)
            d2 = _shift_up(d0, 2, after)
            uv = u_ref[s].astype(F32)
            for k, dk in ((2, d0), (1, d1), (0, d2)):
                dcw_ref[k, pl.ds(s, 1), :] += jnp.sum(dk * uv, axis=0, keepdims=True)
            w0, w1, w2 = _conv_taps(cw_ref, s)
            du = (w2 * d0 + w1 * d1 + w0 * d2).astype(_MXU_DTYPE)
            du_ref[s] = du
            dhf = dhf + _mm(du, win_ref[s])
        dxp, dpre = _rms_bwd(xv, pre, dhf)
        dpre_ref[...] += dpre
        dxi_ref[...] = dx_ref[...] + dxp

    tile3 = pl.BlockSpec((N_DEV, tm, fs), lambda i: (0, i, 0))
    return pl.pallas_call(
        body, name=name, grid=(nt,),
        out_shape=(jax.ShapeDtypeStruct((t, d_model), F32), jax.ShapeDtypeStruct((N_DEV, t, fs), _ACT_DTYPE),
                   jax.ShapeDtypeStruct((t, d_model), _ACT_DTYPE), jax.ShapeDtypeStruct((3, N_DEV, fs), F32),
                   jax.ShapeDtypeStruct((1, d_model), F32)),
        in_specs=[_rows(tm, d_model), _rows(tm, d_model), _const((None, 1, d_model), (layer, 0, 0)), tile3,
                  pl.BlockSpec((N_DEV, hb, fs), lambda i: (0, jnp.minimum((i + 1) * per_tile, t // hb - 1), 0)), tile3,
                  _const((None, 3, N_DEV, fs), (layer, 0, 0, 0)), _resident((None, N_DEV, fs, d_model), (w_layer, 0, 0, 0))],
        out_specs=(_rows(tm, d_model), tile3, _rows(tm, d_model), _const((3, N_DEV, fs)), _const((1, d_model))),
        compiler_params=_params(),
    )(dx, x, pre_g, duc, duc, u, conv_w, w_in)


def _tn_matmul(a, b, a_spec, b_spec, n_out, m, n, name, out_dtype):
    def body(a_ref, b_ref, o_ref):
        o_ref[...] = _mm_ta(a_ref[...], b_ref[...]).astype(o_ref.dtype)

    return pl.pallas_call(
        body, name=name, grid=(n_out,),
        out_shape=jax.ShapeDtypeStruct((n_out, m, n), out_dtype),
        in_specs=[a_spec, b_spec],
        out_specs=pl.BlockSpec((None, m, n), lambda c: (c, 0, 0)),
        compiler_params=_params(),
    )(a, b)


def _kv_fwd(x, kv_g, w_kv, cos, ssin, name):
    t, d_model = x.shape
    tm = _row_tile(t, PROJ_TILE)
    kvd = w_kv.shape[1] // 2
    pairs = kvd // LANES

    def body(x_ref, g_ref, w_ref, cos_ref, sin_ref, k_ref, v_ref):
        kv = _mm(_rms_fwd(x_ref[...], g_ref[...]), w_ref[...])
        low = _low_half((tm, LANES))
        for j in range(pairs):
            kb = _rope_fwd(kv[:, j * LANES:(j + 1) * LANES], cos_ref[...], sin_ref[...])
            vb = kv[:, kvd + j * LANES:kvd + (j + 1) * LANES]
            for blk, ref in ((kb, k_ref), (vb, v_ref)):
                sw = pltpu.roll(blk, 64, 1)
                ref[2 * j] = jnp.where(low, blk, sw).astype(ref.dtype)
                ref[2 * j + 1] = jnp.where(low, sw, blk).astype(ref.dtype)

    heads = jax.ShapeDtypeStruct((N_KV_HEADS, t, LANES), _ACT_DTYPE)
    hspec = pl.BlockSpec((N_KV_HEADS, tm, LANES), lambda i: (0, i, 0))
    return pl.pallas_call(
        body, name=name, grid=(t // tm,), out_shape=(heads, heads),
        in_specs=[_rows(tm, d_model), _const((1, d_model)), _const(w_kv.shape), _rows(tm, LANES), _rows(tm, LANES)],
        out_specs=(hspec, hspec), compiler_params=_params(),
    )(x, kv_g, w_kv, cos, ssin)


def _kv_bwd(dx, x, kv_g, w_kv, cos, ssin, dks, dvs, name):
    t, d_model = x.shape
    tm = _row_tile(t, PROJ_TILE)
    nt = t // tm
    kvd = w_kv.shape[1] // 2
    pairs = kvd // LANES
    n_users = len(dks)

    def body(dx_ref, x_ref, g_ref, w_ref, cos_ref, sin_ref, *refs):
        dk_refs, dv_refs = refs[:n_users], refs[n_users:2 * n_users]
        dxi_ref, dw_ref, dg_ref, acc = refs[2 * n_users:]
        dk_ref = functools.reduce(lambda a, b: a + b, [r[...] for r in dk_refs])
        dv_ref = functools.reduce(lambda a, b: a + b, [r[...] for r in dv_refs])
        i = pl.program_id(0)

        @pl.when(i == 0)
        def _():
            dg_ref[...] = jnp.zeros_like(dg_ref)
            acc[...] = jnp.zeros_like(acc)

        xv = x_ref[...]
        g = g_ref[...]
        low = _low_half((tm, LANES))
        dks, dvs = [], []
        for j in range(pairs):
            dkb = jnp.where(low, dk_ref[2 * j], dk_ref[2 * j + 1])
            dks.append(_rope_bwd(dkb, cos_ref[...], sin_ref[...]))
            dvs.append(jnp.where(low, dv_ref[2 * j], dv_ref[2 * j + 1]))
        dkv = jnp.concatenate(dks + dvs, axis=1).astype(_MXU_DTYPE)
        acc[...] += _mm_ta(_rms_fwd(xv, g), dkv)
        dxp, dg = _rms_bwd(xv, g, _mm_tb(dkv, w_ref[...]))
        dg_ref[...] += dg
        dxi_ref[...] = dx_ref[...] + dxp

        @pl.when(i == nt - 1)
        def _():
            dw_ref[...] = acc[...].astype(dw_ref.dtype)

    hspec = pl.BlockSpec((N_KV_HEADS, tm, LANES), lambda i: (0, i, 0))
    return pl.pallas_call(
        body, name=name, grid=(nt,),
        out_shape=(jax.ShapeDtypeStruct((t, d_model), F32), jax.ShapeDtypeStruct(w_kv.shape, _WIRE_DTYPE),
                   jax.ShapeDtypeStruct((1, d_model), F32)),
        in_specs=[_rows(tm, d_model), _rows(tm, d_model), _const((1, d_model)), _const(w_kv.shape),
                  _rows(tm, LANES), _rows(tm, LANES)] + [hspec] * (2 * n_users),
        out_specs=(_rows(tm, d_model), _resident(w_kv.shape, (0, 0)), _const((1, d_model))),
        scratch_shapes=[pltpu.VMEM(w_kv.shape, F32)],
        compiler_params=_params(),
    )(dx, x, kv_g, w_kv, cos, ssin, *dks, *dvs)


def _q_fwd(x, pre_g, w_q, cos, ssin, layer, j, name):
    t, d_model = x.shape
    tm = _row_tile(t, PROJ_TILE)

    def body(x_ref, g_ref, w_ref, cos_ref, sin_ref, q_ref):
        q = _mm(_rms_fwd(x_ref[...], g_ref[...]), w_ref[...])
        for p in range(d_model // LANES):
            cols = slice(p * LANES, (p + 1) * LANES)
            q_ref[:, cols] = (_rope_fwd(q[:, cols], cos_ref[...], sin_ref[...]) * ATTN_SCALE).astype(q_ref.dtype)

    return pl.pallas_call(
        body, name=name, grid=(t // tm,), out_shape=jax.ShapeDtypeStruct((t, d_model), _ACT_DTYPE),
        in_specs=[_rows(tm, d_model), _const((None, 1, d_model), (layer, 0, 0)), _const((None, d_model, d_model), (j, 0, 0)),
                  _rows(tm, LANES), _rows(tm, LANES)],
        out_specs=_rows(tm, d_model), compiler_params=_params(),
    )(x, pre_g, w_q, cos, ssin)


def _q_bwd(dx, dqs, x, pre_g, w_q, cos, ssin, layer, j, name):
    t, d_model = x.shape
    tm = _row_tile(t, PROJ_TILE)
    nt = t // tm

    def body(dx_ref, dq_ref, x_ref, g_ref, w_ref, cos_ref, sin_ref, dxi_ref, dw_ref, dg_ref, acc):
        i = pl.program_id(0)

        @pl.when(i == 0)
        def _():
            dg_ref[...] = jnp.zeros_like(dg_ref)
            acc[...] = jnp.zeros_like(acc)

        xv = x_ref[...]
        g = g_ref[...]
        parts = []
        for p in range(d_model // LANES):
            cols = slice(p * LANES, (p + 1) * LANES)
            parts.append(_rope_bwd(dq_ref[:, cols] * ATTN_SCALE, cos_ref[...], sin_ref[...]))
        dq = jnp.concatenate(parts, axis=1).astype(_MXU_DTYPE)
        acc[...] += _mm_ta(_rms_fwd(xv, g), dq)
        dxp, dg = _rms_bwd(xv, g, _mm_tb(dq, w_ref[...]))
        dg_ref[...] += dg
        dxi_ref[...] = dx_ref[...] + dxp

        @pl.when(i == nt - 1)
        def _():
            dw_ref[...] = acc[...].astype(dw_ref.dtype)

    return pl.pallas_call(
        body, name=name, grid=(nt,),
        out_shape=(jax.ShapeDtypeStruct((t, d_model), F32), jax.ShapeDtypeStruct((d_model, d_model), _WIRE_DTYPE),
                   jax.ShapeDtypeStruct((1, d_model), F32)),
        in_specs=[_rows(tm, d_model), _rows(tm, d_model), _rows(tm, d_model), _const((None, 1, d_model), (layer, 0, 0)),
                  _const((None, d_model, d_model), (j, 0, 0)), _rows(tm, LANES), _rows(tm, LANES)],
        out_specs=(_rows(tm, d_model), _resident((d_model, d_model), (0, 0)), _const((1, d_model))),
        scratch_shapes=[pltpu.VMEM((d_model, d_model), F32)],
        compiler_params=_params(),
    )(dx, dqs, x, pre_g, w_q, cos, ssin)


def _stack_heads(pairs):
    low = _low_half(pairs[0].shape)
    zero = jnp.zeros_like(pairs[0])
    return jnp.concatenate([h for blk in pairs for h in (jnp.where(low, blk, zero), jnp.where(low, zero, blk))], axis=0)


def _unstack_heads(stacked, i):
    a, b = stacked[2 * i * BLOCK:(2 * i + 1) * BLOCK], stacked[(2 * i + 1) * BLOCK:(2 * i + 2) * BLOCK]
    return jnp.where(_low_half(a.shape), a, b)


def _attn_scores(q_pairs, k2, n, sinks):
    qst = _stack_heads(q_pairs)
    s = _mm_tb(qst, k2)
    row = lax.broadcasted_iota(jnp.int32, s.shape, 0)
    col = lax.broadcasted_iota(jnp.int32, s.shape, 1)
    rel = BLOCK + (row & (BLOCK - 1)) - col
    valid = (rel >= 0) & (rel < WINDOW) & (n * BLOCK + col - BLOCK >= 0)
    s = jnp.where(valid, s, NEG_INF)
    rows1 = lax.broadcasted_iota(jnp.int32, (s.shape[0], 1), 0)
    sink = jnp.full((s.shape[0], 1), sinks[-1], F32)
    for i in reversed(range(len(sinks) - 1)):
        sink = jnp.where(rows1 < (i + 1) * BLOCK, sinks[i], sink)
    return qst, s, sink


def _attn_fwd(qs, kdup, vdup, sinks, j, name):
    t, d_model = qs.shape
    nb = t // BLOCK
    n_pairs = d_model // LANES
    per_group = n_pairs // N_KV_HEADS

    def body(sink_ref, q_ref, kp_ref, ko_ref, vp_ref, vo_ref, o_ref, lse_ref):
        n = pl.program_id(0)
        lane = _lane_iota((BLOCK, LANES))
        lse = jnp.zeros((BLOCK, LANES), F32)
        for hk in range(N_KV_HEADS):
            pairs = range(hk * per_group, (hk + 1) * per_group)
            heads = range(2 * pairs[0], 2 * pairs[-1] + 2)
            k2 = jnp.concatenate([kp_ref[hk], ko_ref[hk]], axis=0)
            v2 = jnp.concatenate([vp_ref[hk], vo_ref[hk]], axis=0)
            _, s, sink = _attn_scores([q_ref[:, p * LANES:(p + 1) * LANES] for p in pairs], k2, n,
                                      [sink_ref[j, h] for h in heads])
            m = jnp.maximum(jnp.max(s, axis=-1, keepdims=True), sink)
            pe = jnp.exp(s - m)
            denom = jnp.sum(pe, axis=-1, keepdims=True) + jnp.exp(sink - m)
            o2 = _mm(pe, v2) / denom
            l2 = m + jnp.log(denom)
            for i, p in enumerate(pairs):
                o_ref[:, p * LANES:(p + 1) * LANES] = _unstack_heads(o2, i).astype(o_ref.dtype)
            for i, h in enumerate(heads):
                lse = jnp.where(lane == h, l2[i * BLOCK:(i + 1) * BLOCK], lse)
        lse_ref[...] = lse

    prev = pl.BlockSpec((N_KV_HEADS, BLOCK, LANES), lambda n: (0, jnp.maximum(n - 1, 0), 0))
    own = pl.BlockSpec((N_KV_HEADS, BLOCK, LANES), lambda n: (0, n, 0))
    return pl.pallas_call(
        body, name=name, grid=(nb,),
        out_shape=(jax.ShapeDtypeStruct((t, d_model), _ACT_DTYPE), jax.ShapeDtypeStruct((t, LANES), F32)),
        in_specs=[pl.BlockSpec(memory_space=pltpu.SMEM), _rows(BLOCK, d_model), prev, own, prev, own],
        out_specs=(_rows(BLOCK, d_model), _rows(BLOCK, LANES)), compiler_params=_params(),
    )(sinks, qs, kdup, kdup, vdup, vdup)


def _attn_bwd(qs, kdup, vdup, sinks, lse, do, j, name):
    t, d_model = qs.shape
    nb = t // BLOCK
    n_pairs = d_model // LANES
    per_group = n_pairs // N_KV_HEADS
    rev = lambda n: nb - 1 - n

    def body(sink_ref, q_ref, kp_ref, ko_ref, vp_ref, vo_ref, lse_ref, do_ref, dq_ref, dk_ref, dv_ref, ds_ref, ck, cv):
        i = pl.program_id(0)
        n = nb - 1 - i

        @pl.when(i == 0)
        def _():
            ck[...] = jnp.zeros_like(ck)
            cv[...] = jnp.zeros_like(cv)
            ds_ref[...] = jnp.zeros_like(ds_ref)

        lane = _lane_iota((BLOCK, LANES))
        lane1 = _lane_iota((1, LANES))
        lsev = lse_ref[...]
        dsink = jnp.zeros((1, LANES), F32)
        for hk in range(N_KV_HEADS):
            pairs = range(hk * per_group, (hk + 1) * per_group)
            heads = range(2 * pairs[0], 2 * pairs[-1] + 2)
            k2 = jnp.concatenate([kp_ref[hk], ko_ref[hk]], axis=0)
            v2 = jnp.concatenate([vp_ref[hk], vo_ref[hk]], axis=0)
            qst, s, sink = _attn_scores([q_ref[:, p * LANES:(p + 1) * LANES] for p in pairs], k2, n,
                                        [sink_ref[j, h] for h in heads])
            l2 = jnp.concatenate([jnp.sum(jnp.where(lane == h, lsev, 0.0), axis=-1, keepdims=True) for h in heads], axis=0)
            pn = jnp.exp(s - l2)
            dost = _stack_heads([do_ref[:, p * LANES:(p + 1) * LANES] for p in pairs])
            dp = _mm_tb(dost, v2)
            dr = jnp.sum(pn * dp, axis=-1, keepdims=True)
            dsm = (pn * (dp - dr)).astype(_MXU_DTYPE)
            dsk = -jnp.exp(sink - l2) * dr
            for i, h in enumerate(heads):
                dsink = dsink + jnp.where(lane1 == h, jnp.sum(dsk[i * BLOCK:(i + 1) * BLOCK]), 0.0)
            dq2 = _mm(dsm, k2)
            for i, p in enumerate(pairs):
                dq_ref[:, p * LANES:(p + 1) * LANES] = _unstack_heads(dq2, i)
            for acc, carry, ref in ((_mm_ta(dsm, qst), ck, dk_ref), (_mm_ta(pn, dost), cv, dv_ref)):
                folded = acc + pltpu.roll(acc, 64, 1)
                ref[hk] = folded[BLOCK:] + carry[hk]
                carry[hk] = folded[:BLOCK]
        ds_ref[...] += dsink

    prev = pl.BlockSpec((N_KV_HEADS, BLOCK, LANES), lambda n: (0, jnp.maximum(rev(n) - 1, 0), 0))
    own = pl.BlockSpec((N_KV_HEADS, BLOCK, LANES), lambda n: (0, rev(n), 0))
    rows = lambda cols: pl.BlockSpec((BLOCK, cols), lambda n: (rev(n), 0))
    heads = jax.ShapeDtypeStruct((N_KV_HEADS, t, LANES), F32)
    return pl.pallas_call(
        body, name=name, grid=(nb,),
        out_shape=(jax.ShapeDtypeStruct((t, d_model), F32), heads, heads, jax.ShapeDtypeStruct((1, LANES), F32)),
        in_specs=[pl.BlockSpec(memory_space=pltpu.SMEM), rows(d_model), prev, own, prev, own, rows(LANES), rows(d_model)],
        out_specs=(rows(d_model), own, own, _const((1, LANES))),
        scratch_shapes=[pltpu.VMEM((N_KV_HEADS, BLOCK, LANES), F32), pltpu.VMEM((N_KV_HEADS, BLOCK, LANES), F32)],
        compiler_params=_params(),
    )(sinks, qs, kdup, kdup, vdup, vdup, lse, do)


def _oproj_fwd(x, o, w_o, post_g, layer, j, name):
    t, d_model = x.shape
    tm = _row_tile(t, PROJ_TILE)

    def body(x_ref, o_ref, w_ref, g_ref, xo_ref, mo_ref):
        mo = _mm(o_ref[...], w_ref[...])
        mo_ref[...] = mo
        xo_ref[...] = x_ref[...] + _rms_fwd(mo, g_ref[...])

    full = jax.ShapeDtypeStruct((t, d_model), F32)
    return pl.pallas_call(
        body, name=name, grid=(t // tm,), out_shape=(full, full),
        in_specs=[_rows(tm, d_model), _rows(tm, d_model), _const((None, d_model, d_model), (j, 0, 0)),
                  _const((None, 1, d_model), (layer, 0, 0))],
        out_specs=(_rows(tm, d_model), _rows(tm, d_model)), compiler_params=_params(),
    )(x, o, w_o, post_g)


def _oproj_bwd(dx, mo, o, w_o, post_g, layer, j, name):
    t, d_model = dx.shape
    tm = _row_tile(t, PROJ_TILE)
    nt = t // tm

    def body(dx_ref, mo_ref, o_ref, w_ref, g_ref, do_ref, dw_ref, dg_ref, acc):
        i = pl.program_id(0)

        @pl.when(i == 0)
        def _():
            dg_ref[...] = jnp.zeros_like(dg_ref)
            acc[...] = jnp.zeros_like(acc)

        dmo, dg = _rms_bwd(mo_ref[...], g_ref[...], dx_ref[...])
        dg_ref[...] += dg
        dmo = dmo.astype(_MXU_DTYPE)
        acc[...] += _mm_ta(o_ref[...], dmo)
        do_ref[...] = _mm_tb(dmo, w_ref[...]).astype(do_ref.dtype)

        @pl.when(i == nt - 1)
        def _():
            dw_ref[...] = acc[...].astype(dw_ref.dtype)

    return pl.pallas_call(
        body, name=name, grid=(nt,),
        out_shape=(jax.ShapeDtypeStruct((t, d_model), _ACT_DTYPE), jax.ShapeDtypeStruct((d_model, d_model), _WIRE_DTYPE),
                   jax.ShapeDtypeStruct((1, d_model), F32)),
        in_specs=[_rows(tm, d_model), _rows(tm, d_model), _rows(tm, d_model), _const((None, d_model, d_model), (j, 0, 0)),
                  _const((None, 1, d_model), (layer, 0, 0))],
        out_specs=(_rows(tm, d_model), _resident((d_model, d_model), (0, 0)), _const((1, d_model))),
        scratch_shapes=[pltpu.VMEM((d_model, d_model), F32)],
        compiler_params=_params(),
    )(dx, mo, o, w_o, post_g)


def _loss_grad(y, target, name):
    t, d_model = y.shape
    tm = _row_tile(t, PROJ_TILE)

    def body(y_ref, t_ref, dy_ref, loss_ref):
        i = pl.program_id(0)

        @pl.when(i == 0)
        def _():
            loss_ref[...] = jnp.zeros_like(loss_ref)

        err = y_ref[...] - t_ref[...]
        dy_ref[...] = err / d_model
        loss_ref[...] += 0.5 * jnp.sum(jnp.mean(err * err, axis=-1, keepdims=True), axis=0, keepdims=True)

    return pl.pallas_call(
        body, name=name, grid=(t // tm,),
        out_shape=(jax.ShapeDtypeStruct((t, d_model), F32), jax.ShapeDtypeStruct((1, 1), F32)),
        in_specs=[_rows(tm, d_model), _rows(tm, d_model)], out_specs=(_rows(tm, d_model), _const((1, 1))),
        compiler_params=_params(),
    )(y, target)


def _mesh_position():
    return lax.axis_index("x"), lax.axis_index("y"), lax.axis_index("c")


def _block_of(px, py, pc):
    return 4 * px + 2 * py + pc


def _at_block(ref, axis, block):
    return ref.at[(slice(None),) * axis + (block,)]


def _all_gather(shards, axes, name):
    n = len(shards)

    def body(*refs):
        srcs, outs = refs[:n], refs[n:2 * n]
        send_sems, recv_sems, local_sems = refs[2 * n:]
        x, y, c = _mesh_position()
        me, sibling = (x, y, c), (x, y, 1 - c)
        chips = [(1 - x, y), (x, 1 - y), (1 - x, 1 - y)]

        def blk(i, pos):
            return _at_block(outs[i], axes[i], _block_of(*pos))

        def copy(i, k, block, to, src=None):
            return pltpu.make_async_remote_copy(
                src_ref=blk(i, block) if src is None else src, dst_ref=blk(i, block),
                send_sem=send_sems.at[i, k], recv_sem=recv_sems.at[i, k], device_id=to, device_id_type=MESH)

        mine = [pltpu.make_async_copy(srcs[i], blk(i, me), local_sems.at[i]) for i in range(n)]
        for cp in mine:
            cp.start()
        sent = []
        for i in range(n):
            sent += [copy(i, 1 + k, me, (*chip, c), src=srcs[i]) for k, chip in enumerate(chips)]
            sent.append(copy(i, 0, me, sibling, src=srcs[i]))
        for cp in sent:
            cp.start()
        for i in range(n):
            for k, chip in enumerate(chips):
                copy(i, 1 + k, (*chip, c), me).wait_recv()
                passed = copy(i, 4 + k, (*chip, c), sibling)
                passed.start()
                sent.append(passed)
        for i in range(n):
            copy(i, 0, sibling, me).wait_recv()
            for k, chip in enumerate(chips):
                copy(i, 4 + k, (*chip, 1 - c), me).wait_recv()
        for cp in sent:
            cp.wait_send()
        for cp in mine:
            cp.wait()

    hbm = pl.BlockSpec(memory_space=pl.ANY)
    return pl.pallas_call(
        body, name=name,
        out_shape=tuple(jax.ShapeDtypeStruct(s.shape[:a] + (N_DEV,) + s.shape[a:], s.dtype) for s, a in zip(shards, axes)),
        in_specs=[hbm] * n, out_specs=(hbm,) * n,
        scratch_shapes=[pltpu.SemaphoreType.DMA((n, 7)), pltpu.SemaphoreType.DMA((n, 7)), pltpu.SemaphoreType.DMA((n,))],
    )(*shards)


GATHER, SCATTER, GATHER_CHIPS, GATHER_SIBLING = "gather", "scatter", "gather_chips", "gather_sibling"
COPIES = {GATHER: N_DEV - 1, SCATTER: N_DEV - 1, GATHER_CHIPS: 4, GATHER_SIBLING: 3}


def _land_shape(kind, s, axis):
    if kind == SCATTER:
        return (N_DEV,) + s.shape[:axis] + s.shape[axis + 1:]
    return s.shape[:axis] + (N_DEV,) + s.shape[axis:]


def _plan(kind, srcs, lands, axes):
    x, y, c = _mesh_position()
    my_block = _block_of(x, y, c)
    flips = {GATHER_CHIPS: (1, 4, 2, 6), GATHER_SIBLING: (4, 2, 6)}.get(kind, range(1, N_DEV))
    others = [(1 - x if k & 4 else x, 1 - y if k & 2 else y, 1 - c if k & 1 else c) for k in flips]
    remote = []
    for src, land, axis in zip(srcs, lands, axes):
        if kind == SCATTER:
            mine = land.at[my_block]
            remote += [(_at_block(src, axis, _block_of(*peer)), mine, peer, land.at[_block_of(*peer)]) for peer in others]
        elif kind == GATHER_SIBLING:
            for px, py, _ in others:
                mine, theirs = _at_block(land, axis, _block_of(px, py, c)), _at_block(land, axis, _block_of(px, py, 1 - c))
                remote.append((mine, mine, (x, y, 1 - c), theirs))
        else:
            mine = _at_block(land, axis, my_block)
            remote += [(src, mine, peer, _at_block(land, axis, _block_of(*peer))) for peer in others]
    return remote


def _remote(src, dst, send_sems, recv_sems, k, peer):
    return pltpu.make_async_remote_copy(src_ref=src, dst_ref=dst, send_sem=send_sems.at[k], recv_sem=recv_sems.at[k],
                                        device_id=peer, device_id_type=MESH)


_HBM = pl.BlockSpec(memory_space=pltpu.HBM)
_SEM = pl.BlockSpec(memory_space=pltpu.SEMAPHORE)
_SPLIT = dict(has_side_effects=pltpu.SideEffectType.DATAFLOW_SIDE_EFFECTING)


def _landing_zone(kind, s, axis, me):
    land = lax.empty(_land_shape(kind, s, axis), s.dtype)
    if kind == SCATTER:
        return lax.dynamic_update_slice_in_dim(land, lax.dynamic_slice_in_dim(s, me, 1, axis).reshape((1,) + land.shape[1:]), me, 0)
    return lax.dynamic_update_slice_in_dim(land, jnp.expand_dims(s, axis), me, axis)


def _exchange_start(kind, arrays, axes, after, name):
    n = len(arrays)
    if kind == GATHER_SIBLING:
        passed = list(arrays)
    else:
        me = _block_of(*_mesh_position())
        passed = list(arrays) + [_landing_zone(kind, s, a, me) for s, a in zip(arrays, axes)]
    n_sems = n * COPIES[kind]

    def body(*refs):
        land_refs = refs[len(passed) - n:len(passed)]
        send_sems, recv_sems = refs[len(passed) + 1], refs[len(passed) + 2]
        token = refs[-1]
        for k, (src, dst, peer, _) in enumerate(_plan(kind, refs[:n], land_refs, axes)):
            _remote(src, dst, send_sems, recv_sems, k, peer).start()
        token[...] = jnp.zeros_like(token)

    out = pl.pallas_call(
        body, name=name,
        out_shape=(pltpu.SemaphoreType.DMA((n_sems,)), pltpu.SemaphoreType.DMA((n_sems,)),
                   *[pltpu.HBM(a.shape, a.dtype) for a in passed], jax.ShapeDtypeStruct((8, LANES), F32)),
        in_specs=[_HBM] * len(passed) + [pl.BlockSpec(memory_space=pl.ANY)],
        out_specs=(_SEM, _SEM, *[_HBM] * len(passed), pl.BlockSpec(memory_space=pltpu.VMEM)),
        input_output_aliases={i: 2 + i for i in range(len(passed))},
        compiler_params=pltpu.CompilerParams(**_SPLIT),
    )(*[pltpu.with_memory_space_constraint(a, pltpu.HBM) for a in passed], after)
    return (kind, axes, n, out[:-1]), out[-1]


def _exchange_wait(handle, after, name):
    kind, axes, n, (send_sems, recv_sems, *thru) = handle

    def body(*refs):
        land_refs = refs[len(thru) - n:len(thru)]
        send_sems, recv_sems = refs[len(thru)], refs[len(thru) + 1]
        for k, (src, _, peer, arrives) in enumerate(_plan(kind, refs[:n], land_refs, axes)):
            cp = _remote(src, arrives, send_sems, recv_sems, k, peer)
            cp.wait_send()
            cp.wait_recv()

    out = pl.pallas_call(
        body, name=name,
        out_shape=tuple(pltpu.HBM(a.shape, a.dtype) for a in thru),
        in_specs=[_HBM] * len(thru) + [_SEM, _SEM, pl.BlockSpec(memory_space=pl.ANY)], out_specs=(_HBM,) * len(thru),
        input_output_aliases={i: i for i in range(len(thru))},
        compiler_params=pltpu.CompilerParams(**_SPLIT),
    )(*thru, send_sems, recv_sems, after)
    return out[len(thru) - n:]


def _adamw_math(w, g, m, v):
    m = ADAM_B1 * m + (1.0 - ADAM_B1) * g
    v = ADAM_B2 * v + (1.0 - ADAM_B2) * jnp.square(g)
    m_hat = m / (1.0 - ADAM_B1 ** ADAM_STEP)
    v_hat = v / (1.0 - ADAM_B2 ** ADAM_STEP)
    delta = -ADAM_LR * (m_hat / (jnp.sqrt(v_hat) + ADAM_EPS) + ADAM_WD * w)
    return delta, m, v


def _update_tile(rows):
    if rows <= 512:
        return rows
    for tr in (512, 384, 352, 256, 176, 128, 64, 32, 16):
        if rows % tr == 0:
            return tr
    raise ValueError(f"{rows} rows do not tile")


def _adamw(parts, w, m, v, slab, so_far, name):
    rows, c = w.shape
    r = parts.shape[1]
    tr = _update_tile(r)
    first = slab * (r // tr)
    if so_far is None:
        so_far = tuple(lax.empty((rows, c), F32) for _ in range(4))

    def body(p_ref, w_ref, m_ref, v_ref, *refs):
        g_ref, d_ref, mo_ref, vo_ref = refs[4:]
        g = p_ref[0].astype(F32)
        for s in range(1, N_DEV):
            g = g + p_ref[s].astype(F32)
        g_ref[...] = g
        d_ref[...], mo_ref[...], vo_ref[...] = _adamw_math(w_ref[...], g, m_ref[...], v_ref[...])

    out = jax.ShapeDtypeStruct((rows, c), F32)
    tile = pl.BlockSpec((tr, c), lambda i: (first + i, 0))
    return pl.pallas_call(
        body, name=name, grid=(r // tr,), out_shape=(out,) * 4,
        in_specs=[pl.BlockSpec((N_DEV, tr, c), lambda i: (0, i, 0))] + [tile] * 3 + [pl.BlockSpec(memory_space=pl.ANY)] * 4,
        out_specs=(tile,) * 4, input_output_aliases={4 + k: k for k in range(4)}, compiler_params=_params(),
    )(parts, w, m, v, *so_far)


def _adamw_small(parts, picks, weights, name):
    n = len(parts)

    def body(*refs):
        p_refs, wmv, outs = refs[:n], refs[n:4 * n], refs[4 * n:]
        me = _block_of(*_mesh_position())
        for i in range(n):
            g = picks[i](p_refs[i], 0, me)
            for s in range(1, N_DEV):
                g = g + picks[i](p_refs[i], s, me)
            w_ref, m_ref, v_ref = wmv[3 * i:3 * i + 3]
            g_ref, d_ref, mo_ref, vo_ref = outs[4 * i:4 * i + 4]
            g_ref[...] = g
            d_ref[...], mo_ref[...], vo_ref[...] = _adamw_math(w_ref[...], g, m_ref[...], v_ref[...])

    flat = [a for wmv in weights for a in wmv]
    out = pl.pallas_call(
        body, name=name,
        out_shape=tuple(jax.ShapeDtypeStruct(w.shape, F32) for w, _, _ in weights for _ in range(4)),
        compiler_params=pltpu.CompilerParams(vmem_limit_bytes=VMEM_LIMIT),
    )(*parts, *flat)
    return [tuple(out[4 * i:4 * i + 4]) for i in range(n)]


def kernel(x, positions, mix_pre_g, mix_post_g, pool_w, pool_scale, kv_norm_g, w_kv, w_q, w_o, sinks, ffn_pre_g, ffn_post_g, ffn_w_in, ffn_conv_w, ffn_conv_b, ffn_w_out, loss_target, m_mix_pre_g, m_mix_post_g, m_pool_w, m_pool_scale, m_kv_norm_g, m_w_kv, m_w_q, m_w_o, m_sinks, m_ffn_pre_g, m_ffn_post_g, m_ffn_w_in, m_ffn_conv_w, m_ffn_conv_b, m_ffn_w_out, v_mix_pre_g, v_mix_post_g, v_pool_w, v_pool_scale, v_kv_norm_g, v_w_kv, v_w_q, v_w_o, v_sinks, v_ffn_pre_g, v_ffn_post_g, v_ffn_w_in, v_ffn_conv_w, v_ffn_conv_b, v_ffn_w_out):
    depth, d_model = mix_pre_g.shape
    n_a = pool_w.shape[0]
    n_b = w_q.shape[0]
    t = x.shape[1]
    fs = ffn_w_in.shape[2]
    half = N_DEV // 2
    n_heads = d_model // HEAD_DIM
    x0 = x.reshape(t, d_model)
    target = loss_target.reshape(t, d_model)

    inv_freq = 1.0 / (ROPE_THETA ** (jnp.arange(0, HEAD_DIM, 2, dtype=F32) / HEAD_DIM))
    ang = positions.reshape(t).astype(F32)[:, None] * inv_freq
    cos, sin = jnp.cos(ang), jnp.sin(ang)
    cos = jnp.tile(cos, (1, 2 * LANES // HEAD_DIM))
    ssin = jnp.tile(jnp.concatenate([-sin, sin], axis=1), (1, LANES // HEAD_DIM))

    wire = lambda a: a.astype(_WIRE_DTYPE)
    by_hidden = lambda a: a.transpose(0, 2, 1)
    w_in_b, w_out_b = wire(by_hidden(ffn_w_in)), wire(ffn_w_out)
    pool_w_g, pool_scale_g, conv_w_g = _all_gather([wire(pool_w), pool_scale, ffn_conv_w], [2, 0, 0], "gather_first")
    groups = []
    for l in range(depth):
        if l == n_a:
            groups.append(("attn", [wire(w_kv), wire(w_q), wire(w_o)], [0, 1, 1], l))
        groups.append((l, [w_in_b[l:l + 1], w_out_b[l:l + 1]], [1, 1], l))
    over_ici, to_sibling, tokens, after = {}, {}, [], pool_w_g
    for key, shards, axes, _ in groups:
        over_ici[key], after = _exchange_start(GATHER_CHIPS, shards, axes, after, f"gather_chips_{key}")
        tokens.append(after)
    started = functools.reduce(lambda a, b: a + b, [tk[0, 0] for tk in tokens])

    def pass_on(layer, after):
        sent = jnp.zeros((), F32)
        for key, _, axes, first in groups:
            if first == layer:
                lands = _exchange_wait(over_ici[key], after, f"gather_chips_wait_{key}")
                to_sibling[key], tk = _exchange_start(GATHER_SIBLING, lands, axes, after, f"gather_sibling_{key}")
                sent = sent + tk[0, 0]
        return sent

    w_in_l, w_out_l = {}, {}
    pool_scale_f = pool_scale_g.transpose(1, 0, 2).reshape(n_a, 1, d_model)
    conv_w_f = conv_w_g.transpose(1, 2, 0, 3)
    pool_w_f = pool_w_g.reshape(n_a, len(POOL_WINDOWS), d_model // len(POOL_WINDOWS), -1)
    conv_b_f = ffn_conv_b.reshape(depth, N_DEV, fs)
    g3 = lambda a: a.reshape(a.shape[0], 1, a.shape[1])
    mix_pre, mix_post, ffn_pre, ffn_post = g3(mix_pre_g) + started, g3(mix_post_g), g3(ffn_pre_g), g3(ffn_post_g)
    kv_g = kv_norm_g.reshape(1, d_model)
    w_kv_f = w_q_f = w_o_f = None

    saved = []
    xc = x0
    kdup = vdup = x_kv = None
    for l in range(depth):
        x_in = xc
        if l < n_a:
            x_mid, dsave, yu = _pool_fwd(x_in, mix_pre, mix_post, pool_w_f, pool_scale_f, l, f"pool_fwd_{l}")
            mixer = (dsave, yu)
        else:
            j = l - n_a
            if j == 0:
                x_kv = x_in
                w_kv_g, w_q_g, w_o_g = _exchange_wait(to_sibling["attn"], x_in, "gather_sibling_wait_attn")
                w_kv_f = w_kv_g.reshape(d_model, -1)
                w_q_f = w_q_g.reshape(n_b, d_model, d_model)
                w_o_f = w_o_g.reshape(n_b, d_model, d_model)
                kdup, vdup = _kv_fwd(x_kv, kv_g, w_kv_f, cos, ssin, "kv_fwd")
            qs = _q_fwd(x_in, mix_pre, w_q_f, cos, ssin, l, j, f"q_fwd_{l}")
            o, lse = _attn_fwd(qs, kdup, vdup, sinks, j, f"attn_fwd_{l}")
            x_mid, mo = _oproj_fwd(x_in, o, w_o_f, mix_post, l, j, f"oproj_fwd_{l}")
            mixer = (qs, o, lse, mo)
        if l == 0:
            pass_on(0, x_mid)
        w_in_l[l], w_out_g = _exchange_wait(to_sibling[l], x_mid, f"gather_sibling_wait_{l}")
        w_out_l[l] = w_out_g.reshape(1, half * fs, d_model)
        xc, u, uc, f = _ffn_fwd(x_mid, ffn_pre, ffn_post, w_in_l[l], conv_w_f, conv_b_f, w_out_l[l], l, 0, f"ffn_fwd_{l}")
        saved.append((x_in, x_mid, u, uc, f, mixer))
        if l + 1 < depth:
            mix_pre = mix_pre + pass_on(l + 1, xc)

    dx, loss_part = _loss_grad(xc, target, "loss")

    gconv_w, gconv_b = [None] * depth, [None] * depth
    gmix_pre, gmix_post, gffn_pre, gffn_post = [None] * depth, [None] * depth, [None] * depth, [None] * depth
    gpool_scale, gsinks = [None] * n_a, [None] * n_b
    dks, dvs = [], []
    gkv_g = None
    whole = lambda cols: pl.BlockSpec((t, cols), lambda c: (0, 0), pipeline_mode=pl.Buffered(1))
    per_out = lambda cols: pl.BlockSpec((None, t, cols), lambda c: (c, 0, 0))
    by_dev = lambda g: g.reshape(N_DEV, -1, g.shape[-1])
    def small_grads():
        cat = lambda rows: jnp.concatenate(rows, axis=0)
        row = lambda a: a.reshape(1, -1)
        everything = lambda ref, s, me: ref[s]
        lanes = pool_scale.shape[1]
        return [("mix_pre_g", cat(gmix_pre), everything, (mix_pre_g, m_mix_pre_g, v_mix_pre_g)),
                ("mix_post_g", cat(gmix_post), everything, (mix_post_g, m_mix_post_g, v_mix_post_g)),
                ("kv_norm_g", gkv_g, everything, (row(kv_norm_g), row(m_kv_norm_g), row(v_kv_norm_g))),
                ("sinks", cat(gsinks), lambda ref, s, me: ref[s, :, pl.ds(0, n_heads)], (sinks, m_sinks, v_sinks)),
                ("ffn_pre_g", cat(gffn_pre), everything, (ffn_pre_g, m_ffn_pre_g, v_ffn_pre_g)),
                ("ffn_post_g", cat(gffn_post), everything, (ffn_post_g, m_ffn_post_g, v_ffn_post_g)),
                ("ffn_conv_b", jnp.stack(gconv_b).reshape(depth, N_DEV * fs), everything, (ffn_conv_b, m_ffn_conv_b, v_ffn_conv_b)),
                ("pool_scale", cat(gpool_scale), lambda ref, s, me: ref[s, :, pl.ds(pl.multiple_of(me * lanes, lanes), lanes)],
                 (pool_scale, m_pool_scale, v_pool_scale)),
                ("ffn_conv_w", jnp.stack(gconv_w).transpose(0, 2, 1, 3), lambda ref, s, me: ref[s, :, me],
                 (ffn_conv_w, m_ffn_conv_w, v_ffn_conv_w))]

    flying = []

    def launch(going, after, name):
        handle, token = _exchange_start(SCATTER, [g for _, _, g, _ in going], [a for _, _, _, a in going], after, name)
        flying.append(([(nm, slab) for nm, slab, _, _ in going], handle))
        return token

    post = mix_post
    ffn_post_b = ffn_post
    token = None
    for l in reversed(range(depth)):
        x_in, x_mid, u, uc, f, mixer = saved[l]
        duc, gout, gconv_b[l], gffn_post[l] = _ffn_bwd_out(dx, f, ffn_post_b, uc, conv_b_f, w_out_l[l], l, 0, f"ffn_bwd_out_{l}")
        going = [("ffn_w_out", l, by_dev(gout), 0)]
        pre = ffn_pre
        if l == 0:
            token = launch(going, gout, "scatter_start_0_out")
            going = []
            pre = pre + token[0, 0]
        dx, du, hf, gconv_w[l], gffn_pre[l] = _ffn_bwd_in(dx, x_mid, pre, duc, u, conv_w_f, w_in_l[l], l, 0, f"ffn_bwd_in_{l}")
        gin = _tn_matmul(du, hf, per_out(fs), whole(d_model), N_DEV, fs, d_model, f"grad_w_in_{l}", _WIRE_DTYPE)
        going.append(("ffn_w_in", l, gin, 0))
        if l == 0:
            token = launch(going, dx, "scatter_start_0_in")
            going = []
            post = post + token[0, 0]
        if l < n_a:
            dsave, yu = mixer
            dx, dyu, gpool_scale[l], gmix_pre[l], gmix_post[l] = _pool_bwd(
                dx, x_in, mix_pre, post, dsave, yu, pool_w_f, pool_scale_f, l, f"pool_bwd_{l}")
            gc = d_model // len(POOL_WINDOWS)
            by_group = pl.BlockSpec((t, gc), lambda c: (0, c))
            gpool = _tn_matmul(dsave, dyu, by_group, by_group, len(POOL_WINDOWS), gc, gc, f"grad_pool_w_{l}", _WIRE_DTYPE)
            going.append(("pool_w", l, gpool.reshape(len(POOL_WINDOWS), N_DEV, -1, gc), 1))
        else:
            j = l - n_a
            qs, o, lse, mo = mixer
            do, go, gmix_post[l] = _oproj_bwd(dx, mo, o, w_o_f, post, l, j, f"oproj_bwd_{l}")
            dqs, dk, dv, gsinks[j] = _attn_bwd(qs, kdup, vdup, sinks, lse, do, j, f"attn_bwd_{l}")
            dks.append(dk)
            dvs.append(dv)
            dx, gq, gmix_pre[l] = _q_bwd(dx, dqs, x_in, mix_pre, w_q_f, cos, ssin, l, j, f"q_bwd_{l}")
            if j == 0:
                dx, gkv, gkv_g = _kv_bwd(dx, x_kv, kv_g, w_kv_f, cos, ssin, dks, dvs, "kv_bwd")
                going.append(("w_kv", 0, by_dev(gkv), 0))
            going += [("w_o", j, by_dev(go), 0), ("w_q", j, by_dev(gq), 0)]
        after = dx
        if l == 0:
            small = small_grads()
            leaving = [g for _, g, _, _ in small] + [jnp.broadcast_to(loss_part, (1, LANES))]
            small_flight, after = _exchange_start(GATHER, leaving, [0] * len(leaving), dx, "gather_small_grads")
        token = launch(going, after, f"scatter_start_{l}")
        ffn_post_b = ffn_post_b + token[0, 0]

    grad_x = dx.reshape(x.shape)

    shard = {"pool_w": (pool_w, m_pool_w, v_pool_w), "w_kv": (w_kv, m_w_kv, v_w_kv), "w_q": (w_q, m_w_q, v_w_q),
             "w_o": (w_o, m_w_o, v_w_o), "ffn_w_in": tuple(by_hidden(a) for a in (ffn_w_in, m_ffn_w_in, v_ffn_w_in)),
             "ffn_w_out": (ffn_w_out, m_ffn_w_out, v_ffn_w_out)}
    big = {}

    def arrive(flights, after):
        for idx, (names, handle) in flights:
            parts = _exchange_wait(handle, after, f"scatter_wait_{idx}")
            for (nm, slab), p in zip(names, parts):
                cols = p.shape[-1]
                w2, m2, v2 = (a.reshape(-1, cols) for a in shard[nm])
                big[nm] = _adamw(p.reshape(N_DEV, -1, cols), w2, m2, v2, slab, big.get(nm), f"adamw_{nm}_{slab}")
                after = big[nm][0]
        return after

    done = arrive(list(enumerate(flying)), token)
    *small_parts, loss_parts = _exchange_wait(small_flight, done, "gather_small_grads_wait")
    loss = jnp.sum(loss_parts[:, 0, 0])
    upd = _adamw_small(small_parts, [pick for _, _, pick, _ in small], [wmv for _, _, _, wmv in small], "adamw_small")
    res = {nm: tuple(r.reshape(shard[nm][0].shape) for r in out) for nm, out in big.items()}
    res["ffn_w_in"] = tuple(by_hidden(r) for r in res["ffn_w_in"])
    for (nm, _, _, _), out in zip(small, upd):
        res[nm] = tuple(a.reshape(kv_norm_g.shape) for a in out) if nm == "kv_norm_g" else out

    order = ["mix_pre_g", "mix_post_g", "pool_w", "pool_scale", "kv_norm_g", "w_kv", "w_q", "w_o", "sinks", "ffn_pre_g",
             "ffn_post_g", "ffn_w_in", "ffn_conv_w", "ffn_conv_b", "ffn_w_out"]
    return (loss, grad_x, *[res[nm][0] for nm in order], *[res[nm][1] for nm in order],
            *[res[nm][2] for nm in order], *[res[nm][3] for nm in order])
```

```python
import functools
import math

import jax
import jax.numpy as jnp
from jax import lax
from jax.experimental import pallas as pl
from jax.experimental.pallas import tpu as pltpu

F32 = jnp.float32
_MXU_DTYPE = jnp.bfloat16
_ACT_DTYPE = jnp.bfloat16
_WIRE_DTYPE = jnp.bfloat16
_SAVE_DTYPE = jnp.bfloat16

N_DEV = 8
POOL_WINDOWS = (2, 4, 8, 16)
POOL_HALO = 32
assert POOL_WINDOWS == tuple(2 ** (g + 1) for g in range(len(POOL_WINDOWS))) and 8 * len(POOL_WINDOWS) <= POOL_HALO
HEAD_DIM = 64
N_KV_HEADS = 4
WINDOW = 128
BLOCK = 128
LANES = 128
ROPE_THETA = 10000.0
ATTN_SCALE = 1.0 / math.sqrt(HEAD_DIM)
NEG_INF = -1e30
RMS_EPS = 1e-6
CONV_HALO = 8
SAVE_HALO = 16
PROJ_TILE = 512
ADAM_LR = 0.001
ADAM_B1 = 0.9
ADAM_B2 = 0.999
ADAM_EPS = 1e-08
ADAM_WD = 0.01
ADAM_STEP = 10
VMEM_LIMIT = 56 * 1024 * 1024
MESH = pl.DeviceIdType.MESH


def _params(n_axes=1, vmem=VMEM_LIMIT):
    return pltpu.CompilerParams(dimension_semantics=("arbitrary",) * n_axes, vmem_limit_bytes=vmem)


def _resident(shape, index):
    return pl.BlockSpec(shape, lambda *_: index, pipeline_mode=pl.Buffered(1))


def _const(shape, index=None):
    index = (0,) * len(shape) if index is None else index
    return pl.BlockSpec(shape, lambda *_: index)


def _rows(tm, cols):
    return pl.BlockSpec((tm, cols), lambda i: (i, 0))


def _row_tile(t, most=256):
    for tm in (512, 256, 128, 64, 32, 16, 8):
        if tm <= most and t % tm == 0:
            return tm
    raise ValueError(f"sequence length {t} is not a multiple of 8")


def _mm(a, b):
    return jnp.dot(a.astype(_MXU_DTYPE), b.astype(_MXU_DTYPE), preferred_element_type=F32)


def _mm_tb(a, b):
    return lax.dot_general(a.astype(_MXU_DTYPE), b.astype(_MXU_DTYPE), (((1,), (1,)), ((), ())),
                           preferred_element_type=F32)


def _mm_ta(a, b):
    return lax.dot_general(a.astype(_MXU_DTYPE), b.astype(_MXU_DTYPE), (((0,), (0,)), ((), ())),
                           preferred_element_type=F32)


def _rms_r(x):
    return lax.rsqrt(jnp.mean(x * x, axis=-1, keepdims=True) + RMS_EPS)


def _rms_fwd(x, g):
    return (x * _rms_r(x)) * g


def _rms_bwd(x, g, dy):
    r = _rms_r(x)
    xh = x * r
    dg = jnp.sum(dy * xh, axis=0, keepdims=True)
    dxh = dy * g
    dx = r * (dxh - xh * jnp.mean(dxh * xh, axis=-1, keepdims=True))
    return dx, dg


_GELU_C = math.sqrt(2.0 / math.pi)


def _gelu_parts(z):
    z2 = z * z
    e = jnp.exp(z * (-2.0 * _GELU_C - (2.0 * _GELU_C * 0.044715) * z2))
    cdf = pl.reciprocal(1.0 + e, approx=False)
    dz = cdf + (z * (cdf * (1.0 - cdf))) * (2.0 * _GELU_C + (6.0 * _GELU_C * 0.044715) * z2)
    return cdf, dz


def _lane_iota(shape):
    return lax.broadcasted_iota(jnp.int32, shape, len(shape) - 1)


def _rope_partner(xb):
    first = (_lane_iota(xb.shape) & 32) == 0
    return jnp.where(first, pltpu.roll(xb, LANES - 32, 1), pltpu.roll(xb, 32, 1))


def _rope_fwd(xb, cos, ssin):
    return xb * cos + _rope_partner(xb) * ssin


def _rope_bwd(dyb, cos, ssin):
    return dyb * cos - _rope_partner(dyb) * ssin


def _low_half(shape):
    return (_lane_iota(shape) & 64) == 0


def _pool_fwd(x, pre_g, post_g, w, scale, layer, name):
    t, d_model = x.shape
    tm = _row_tile(t)
    n_groups, gc = w.shape[1], w.shape[2]

    def body(x_ref, pre_ref, post_ref, w_ref, sc_ref, xo_ref, d_ref, yu_ref, hbuf, sbuf):
        i = pl.program_id(0)

        @pl.when(i == 0)
        def _():
            hbuf[pl.ds(0, POOL_HALO), :] = jnp.zeros((POOL_HALO, d_model), F32)

        xv = x_ref[...]
        hbuf[pl.ds(POOL_HALO, tm), :] = _rms_fwd(xv, pre_ref[...])
        tok = i * tm + lax.broadcasted_iota(jnp.int32, (tm, 1), 0)
        yus = []
        for gi, wnd in enumerate(POOL_WINDOWS):
            first, reach = CONV_HALO * (gi + 1), wnd // 2
            n = POOL_HALO + tm - first
            cols = pl.ds(gi * gc, d_model - gi * gc)
            src = hbuf if gi == 0 else sbuf.at[gi - 1]
            level = src[pl.ds(first, n), cols] + src[pl.ds(first - reach, n), cols]
            if gi + 1 < len(POOL_WINDOWS):
                sbuf[gi, pl.ds(first, n), cols] = level
            h = hbuf[pl.ds(POOL_HALO, tm), pl.ds(gi * gc, gc)]
            cnt = jnp.minimum(tok + 1, wnd).astype(F32)
            dg = level[POOL_HALO - first:, :gc] / cnt - h
            d_ref[:, pl.ds(gi * gc, gc)] = dg.astype(d_ref.dtype)
            yus.append(_mm(dg, w_ref[gi]))
        hbuf[pl.ds(0, POOL_HALO), :] = hbuf[pl.ds(tm, POOL_HALO), :]
        yu = jnp.concatenate(yus, axis=1)
        yu_ref[...] = yu
        xo_ref[...] = xv + _rms_fwd(yu * sc_ref[...], post_ref[...])

    return pl.pallas_call(
        body, name=name, grid=(t // tm,),
        out_shape=(jax.ShapeDtypeStruct((t, d_model), F32), jax.ShapeDtypeStruct((t, d_model), _ACT_DTYPE),
                   jax.ShapeDtypeStruct((t, d_model), F32)),
        in_specs=[_rows(tm, d_model), _const((None, 1, d_model), (layer, 0, 0)), _const((None, 1, d_model), (layer, 0, 0)),
                  _const((None, n_groups, gc, gc), (layer, 0, 0, 0)), _const((None, 1, d_model), (layer, 0, 0))],
        out_specs=(_rows(tm, d_model), _rows(tm, d_model), _rows(tm, d_model)),
        scratch_shapes=[pltpu.VMEM((POOL_HALO + tm, d_model), F32),
                        pltpu.VMEM((len(POOL_WINDOWS) - 1, POOL_HALO + tm, d_model), F32)],
        compiler_params=_params(),
    )(x, pre_g, post_g, w, scale)


def _pool_bwd(dx, x, pre_g, post_g, d, yu, w, scale, layer, name):
    t, d_model = x.shape
    tm = _row_tile(t)
    nt = t // tm
    n_groups, gc = w.shape[1], w.shape[2]
    rev = lambda i: (nt - 1 - i, 0)
    rows = pl.BlockSpec((tm, d_model), rev)

    def body(dx_ref, x_ref, pre_ref, post_ref, d_ref, yu_ref, w_ref, sc_ref,
             dxi_ref, dyu_ref, dsc_ref, dpre_ref, dpost_ref, zbuf, sbuf):
        i = pl.program_id(0)

        @pl.when(i == 0)
        def _():
            zbuf[pl.ds(tm, POOL_HALO), :] = jnp.zeros((POOL_HALO, d_model), F32)
            dsc_ref[...] = jnp.zeros_like(dsc_ref)
            dpre_ref[...] = jnp.zeros_like(dpre_ref)
            dpost_ref[...] = jnp.zeros_like(dpost_ref)

        dxo = dx_ref[...]
        yuv = yu_ref[...]
        sc = sc_ref[...]
        dm, dpost = _rms_bwd(yuv * sc, post_ref[...], dxo)
        dpost_ref[...] += dpost
        dsc_ref[...] += jnp.sum(dm * yuv, axis=0, keepdims=True)
        dyu = dm * sc
        dyu_ref[...] = dyu.astype(dyu_ref.dtype)
        tok = (nt - 1 - i) * tm + lax.broadcasted_iota(jnp.int32, (tm, 1), 0)
        dds = []
        for gi, wnd in enumerate(POOL_WINDOWS):
            cols = pl.ds(gi * gc, gc)
            dd = _mm_tb(dyu[:, gi * gc:(gi + 1) * gc], w_ref[gi])
            cnt = jnp.minimum(tok + 1, wnd).astype(F32)
            zbuf[pl.ds(0, tm), cols] = dd / cnt
            dds.append(dd)
        dhs = []
        for gi, wnd in enumerate(POOL_WINDOWS):
            reach = wnd // 2
            n = tm + POOL_HALO - CONV_HALO * (gi + 1)
            cols = pl.ds(gi * gc, d_model - gi * gc)
            src = zbuf if gi == 0 else sbuf.at[gi - 1]
            level = src[pl.ds(0, n), cols] + src[pl.ds(reach, n), cols]
            if gi + 1 < len(POOL_WINDOWS):
                sbuf[gi, pl.ds(0, n), cols] = level
            dhs.append(level[:tm, :gc] - dds[gi])
        zbuf[pl.ds(tm, POOL_HALO), :] = zbuf[pl.ds(0, POOL_HALO), :]
        dh = jnp.concatenate(dhs, axis=1)
        dxp, dpre = _rms_bwd(x_ref[...], pre_ref[...], dh)
        dpre_ref[...] += dpre
        dxi_ref[...] = dxo + dxp

    vec = jax.ShapeDtypeStruct((1, d_model), F32)
    return pl.pallas_call(
        body, name=name, grid=(nt,),
        out_shape=(jax.ShapeDtypeStruct((t, d_model), F32), jax.ShapeDtypeStruct((t, d_model), _ACT_DTYPE), vec, vec, vec),
        in_specs=[rows, rows, _const((None, 1, d_model), (layer, 0, 0)), _const((None, 1, d_model), (layer, 0, 0)), rows, rows,
                  _const((None, n_groups, gc, gc), (layer, 0, 0, 0)), _const((None, 1, d_model), (layer, 0, 0))],
        out_specs=(rows, rows, _const((1, d_model)), _const((1, d_model)), _const((1, d_model))),
        scratch_shapes=[pltpu.VMEM((tm + POOL_HALO, d_model), F32),
                        pltpu.VMEM((len(POOL_WINDOWS) - 1, tm + POOL_HALO, d_model), F32)],
        compiler_params=_params(),
    )(dx, x, pre_g, post_g, d, yu, w, scale)


def _conv_taps(cw_ref, s):
    return [cw_ref[k, pl.ds(s, 1), :] for k in range(3)]


def _shift_down(v, k, before):
    rolled = pltpu.roll(v, k, 0)
    row = lax.broadcasted_iota(jnp.int32, before.shape, 0)
    head = jnp.where(row < k, pltpu.roll(before, k, 0), rolled[:CONV_HALO])
    return jnp.concatenate([head, rolled[CONV_HALO:]], axis=0)


def _shift_up(v, k, after):
    rows = v.shape[0]
    rolled = pltpu.roll(v, rows - k, 0)
    row = lax.broadcasted_iota(jnp.int32, after.shape, 0)
    tail = jnp.where(row >= CONV_HALO - k, pltpu.roll(after, CONV_HALO - k, 0), rolled[rows - CONV_HALO:])
    return jnp.concatenate([rolled[:rows - CONV_HALO], tail], axis=0)


def _ffn_fwd(x, pre_g, post_g, w_in, conv_w, conv_b, w_out, layer, w_layer, name):
    t, d_model = x.shape
    tm = _row_tile(t)
    fs = w_in.shape[2]
    half = N_DEV // 2

    def body(x_ref, pre_ref, post_ref, win_ref, cw_ref, cb_ref, wout_ref, xo_ref, u_ref, uc_ref, f_ref, carry):
        i = pl.program_id(0)

        @pl.when(i == 0)
        def _():
            carry[...] = jnp.zeros_like(carry)

        xv = x_ref[...]
        hf = _rms_fwd(xv, pre_ref[...]).astype(_MXU_DTYPE)
        f = jnp.zeros((tm, d_model), F32)
        project = lambda b: [_mm_tb(hf, win_ref[s]) for s in (b, b + half)]
        ahead = project(0)
        for b in range(half):
            us, ahead = ahead, project(b + 1) if b + 1 < half else None
            ucs = []
            for s, u in zip((b, b + half), us):
                u_ref[s] = u.astype(u_ref.dtype)
                before = carry[s]
                carry[s] = u[tm - CONV_HALO:]
                w0, w1, w2 = _conv_taps(cw_ref, s)
                uc = ((w0 * _shift_down(u, 2, before) + w1 * _shift_down(u, 1, before)) + w2 * u) + cb_ref[pl.ds(s, 1), :]
                uc_ref[s] = uc.astype(uc_ref.dtype)
                ucs.append(uc)
            gate, val = ucs
            cdf, _ = _gelu_parts(gate)
            f = f + _mm((gate * cdf) * val, wout_ref[pl.ds(b * fs, fs), :])
        f_ref[...] = f
        xo_ref[...] = xv + _rms_fwd(f, post_ref[...])

    tile3 = pl.BlockSpec((N_DEV, tm, fs), lambda i: (0, i, 0))
    saved = jax.ShapeDtypeStruct((N_DEV, t, fs), _SAVE_DTYPE)
    return pl.pallas_call(
        body, name=name, grid=(t // tm,),
        out_shape=(jax.ShapeDtypeStruct((t, d_model), F32), saved, saved, jax.ShapeDtypeStruct((t, d_model), F32)),
        in_specs=[_rows(tm, d_model), _const((None, 1, d_model), (layer, 0, 0)), _const((None, 1, d_model), (layer, 0, 0)),
                  _resident((None, N_DEV, fs, d_model), (w_layer, 0, 0, 0)), _const((None, 3, N_DEV, fs), (layer, 0, 0, 0)),
                  _const((None, N_DEV, fs), (layer, 0, 0)), _resident((None, half * fs, d_model), (w_layer, 0, 0))],
        out_specs=(_rows(tm, d_model), tile3, tile3, _rows(tm, d_model)),
        scratch_shapes=[pltpu.VMEM((N_DEV, CONV_HALO, fs), F32)],
        compiler_params=_params(),
    )(x, pre_g, post_g, w_in, conv_w, conv_b, w_out)


def _ffn_bwd_out(dx, f, post_g, uc, conv_b, w_out, layer, w_layer, name):
    t, d_model = dx.shape
    tm = _row_tile(t)
    nt = t // tm
    fs = uc.shape[2]
    half = N_DEV // 2

    def body(dx_ref, f_ref, post_ref, uc_ref, wout_ref, duc_ref, dwout_ref, dcb_ref, dpost_ref, acc):
        i = pl.program_id(0)

        @pl.when(i == 0)
        def _():
            acc[...] = jnp.zeros_like(acc)
            dcb_ref[...] = jnp.zeros_like(dcb_ref)
            dpost_ref[...] = jnp.zeros_like(dpost_ref)

        df, dpost = _rms_bwd(f_ref[...], post_ref[...], dx_ref[...])
        dpost_ref[...] += dpost
        dfm = df.astype(_MXU_DTYPE)
        project = lambda b: _mm_tb(dfm, wout_ref[pl.ds(b * fs, fs), :])
        ahead = project(0)
        for b in range(half):
            dg, ahead = ahead, project(b + 1) if b + 1 < half else None
            gate = uc_ref[b].astype(F32)
            val = uc_ref[b + half].astype(F32)
            cdf, dgelu = _gelu_parts(gate)
            ge = gate * cdf
            acc[pl.ds(b * fs, fs), :] += _mm_ta(ge * val, dfm)
            for s, dd in ((b, dg * val * dgelu), (b + half, dg * ge)):
                duc_ref[s] = dd.astype(duc_ref.dtype)
                dcb_ref[pl.ds(s, 1), :] += jnp.sum(dd, axis=0, keepdims=True)

        @pl.when(i == nt - 1)
        def _():
            dwout_ref[...] = acc[...].astype(dwout_ref.dtype)

    tile3 = pl.BlockSpec((N_DEV, tm, fs), lambda i: (0, i, 0))
    return pl.pallas_call(
        body, name=name, grid=(nt,),
        out_shape=(jax.ShapeDtypeStruct((N_DEV, t, fs), _SAVE_DTYPE), jax.ShapeDtypeStruct((half * fs, d_model), _WIRE_DTYPE),
                   jax.ShapeDtypeStruct((N_DEV, fs), F32), jax.ShapeDtypeStruct((1, d_model), F32)),
        in_specs=[_rows(tm, d_model), _rows(tm, d_model), _const((None, 1, d_model), (layer, 0, 0)), tile3,
                  _resident((None, half * fs, d_model), (w_layer, 0, 0))],
        out_specs=(tile3, _resident((half * fs, d_model), (0, 0)), _const((N_DEV, fs)), _const((1, d_model))),
        scratch_shapes=[pltpu.VMEM((half * fs, d_model), F32)],
        compiler_params=_params(),
    )(dx, f, post_g, uc, w_out)


def _ffn_bwd_in(dx, x, pre_g, duc, u, conv_w, w_in, layer, w_layer, name):
    t, d_model = dx.shape
    tm = _row_tile(t)
    nt = t // tm
    fs = duc.shape[2]
    hb = SAVE_HALO
    per_tile = tm // hb

    def body(dx_ref, x_ref, pre_ref, duc_ref, dn_ref, u_ref, cw_ref, win_ref, dxi_ref, du_ref, hf_ref, dcw_ref, dpre_ref):
        i = pl.program_id(0)

        @pl.when(i == 0)
        def _():
            dcw_ref[...] = jnp.zeros_like(dcw_ref)
            dpre_ref[...] = jnp.zeros_like(dpre_ref)

        xv = x_ref[...]
        pre = pre_ref[...]
        hf_ref[...] = _rms_fwd(xv, pre).astype(hf_ref.dtype)
        dhf = jnp.zeros((tm, d_model), F32)
        for s in range(N_DEV):
            d0 = duc_ref[s].astype(F32)
            after = jnp.where(i == nt - 1, 0.0, dn_ref[s].astype(F32)[:CONV_HALO])
            d1 = _shift_up(d0, 1, after)
            d2 = _shift_up(d0, 2, after)
            uv = u_ref[s].astype(F32)
            for k, dk in ((2, d0), (1, d1), (0, d2)):
                dcw_ref[k, pl.ds(s, 1), :] += jnp.sum(dk * uv, axis=0, keepdims=True)
            w0, w1, w2 = _conv_taps(cw_ref, s)
            du = (w2 * d0 + w1 * d1 + w0 * d2).astype(_MXU_DTYPE)
            du_ref[s] = du
            dhf = dhf + _mm(du, win_ref[s])
        dxp, dpre = _rms_bwd(xv, pre, dhf)
        dpre_ref[...] += dpre
        dxi_ref[...] = dx_ref[...] + dxp

    tile3 = pl.BlockSpec((N_DEV, tm, fs), lambda i: (0, i, 0))
    return pl.pallas_call(
        body, name=name, grid=(nt,),
        out_shape=(jax.ShapeDtypeStruct((t, d_model), F32), jax.ShapeDtypeStruct((N_DEV, t, fs), _ACT_DTYPE),
                   jax.ShapeDtypeStruct((t, d_model), _ACT_DTYPE), jax.ShapeDtypeStruct((3, N_DEV, fs), F32),
                   jax.ShapeDtypeStruct((1, d_model), F32)),
        in_specs=[_rows(tm, d_model), _rows(tm, d_model), _const((None, 1, d_model), (layer, 0, 0)), tile3,
                  pl.BlockSpec((N_DEV, hb, fs), lambda i: (0, jnp.minimum((i + 1) * per_tile, t // hb - 1), 0)), tile3,
                  _const((None, 3, N_DEV, fs), (layer, 0, 0, 0)), _resident((None, N_DEV, fs, d_model), (w_layer, 0, 0, 0))],
        out_specs=(_rows(tm, d_model), tile3, _rows(tm, d_model), _const((3, N_DEV, fs)), _const((1, d_model))),
        compiler_params=_params(),
    )(dx, x, pre_g, duc, duc, u, conv_w, w_in)


def _tn_matmul(a, b, a_spec, b_spec, n_out, m, n, name, out_dtype):
    def body(a_ref, b_ref, o_ref):
        o_ref[...] = _mm_ta(a_ref[...], b_ref[...]).astype(o_ref.dtype)

    return pl.pallas_call(
        body, name=name, grid=(n_out,),
        out_shape=jax.ShapeDtypeStruct((n_out, m, n), out_dtype),
        in_specs=[a_spec, b_spec],
        out_specs=pl.BlockSpec((None, m, n), lambda c: (c, 0, 0)),
        compiler_params=_params(),
    )(a, b)


def _kv_fwd(x, kv_g, w_kv, cos, ssin, name):
    t, d_model = x.shape
    tm = _row_tile(t, PROJ_TILE)
    kvd = w_kv.shape[1] // 2
    pairs = kvd // LANES

    def body(x_ref, g_ref, w_ref, cos_ref, sin_ref, k_ref, v_ref):
        kv = _mm(_rms_fwd(x_ref[...], g_ref[...]), w_ref[...])
        low = _low_half((tm, LANES))
        for j in range(pairs):
            kb = _rope_fwd(kv[:, j * LANES:(j + 1) * LANES], cos_ref[...], sin_ref[...])
            vb = kv[:, kvd + j * LANES:kvd + (j + 1) * LANES]
            for blk, ref in ((kb, k_ref), (vb, v_ref)):
                sw = pltpu.roll(blk, 64, 1)
                ref[2 * j] = jnp.where(low, blk, sw).astype(ref.dtype)
                ref[2 * j + 1] = jnp.where(low, sw, blk).astype(ref.dtype)

    heads = jax.ShapeDtypeStruct((N_KV_HEADS, t, LANES), _ACT_DTYPE)
    hspec = pl.BlockSpec((N_KV_HEADS, tm, LANES), lambda i: (0, i, 0))
    return pl.pallas_call(
        body, name=name, grid=(t // tm,), out_shape=(heads, heads),
        in_specs=[_rows(tm, d_model), _const((1, d_model)), _const(w_kv.shape), _rows(tm, LANES), _rows(tm, LANES)],
        out_specs=(hspec, hspec), compiler_params=_params(),
    )(x, kv_g, w_kv, cos, ssin)


def _kv_bwd(dx, x, kv_g, w_kv, cos, ssin, dks, dvs, name):
    t, d_model = x.shape
    tm = _row_tile(t, PROJ_TILE)
    nt = t // tm
    kvd = w_kv.shape[1] // 2
    pairs = kvd // LANES
    n_users = len(dks)

    def body(dx_ref, x_ref, g_ref, w_ref, cos_ref, sin_ref, *refs):
        dk_refs, dv_refs = refs[:n_users], refs[n_users:2 * n_users]
        dxi_ref, dw_ref, dg_ref, acc = refs[2 * n_users:]
        dk_ref = functools.reduce(lambda a, b: a + b, [r[...] for r in dk_refs])
        dv_ref = functools.reduce(lambda a, b: a + b, [r[...] for r in dv_refs])
        i = pl.program_id(0)

        @pl.when(i == 0)
        def _():
            dg_ref[...] = jnp.zeros_like(dg_ref)
            acc[...] = jnp.zeros_like(acc)

        xv = x_ref[...]
        g = g_ref[...]
        low = _low_half((tm, LANES))
        dks, dvs = [], []
        for j in range(pairs):
            dkb = jnp.where(low, dk_ref[2 * j], dk_ref[2 * j + 1])
            dks.append(_rope_bwd(dkb, cos_ref[...], sin_ref[...]))
            dvs.append(jnp.where(low, dv_ref[2 * j], dv_ref[2 * j + 1]))
        dkv = jnp.concatenate(dks + dvs, axis=1).astype(_MXU_DTYPE)
        acc[...] += _mm_ta(_rms_fwd(xv, g), dkv)
        dxp, dg = _rms_bwd(xv, g, _mm_tb(dkv, w_ref[...]))
        dg_ref[...] += dg
        dxi_ref[...] = dx_ref[...] + dxp

        @pl.when(i == nt - 1)
        def _():
            dw_ref[...] = acc[...].astype(dw_ref.dtype)

    hspec = pl.BlockSpec((N_KV_HEADS, tm, LANES), lambda i: (0, i, 0))
    return pl.pallas_call(
        body, name=name, grid=(nt,),
        out_shape=(jax.ShapeDtypeStruct((t, d_model), F32), jax.ShapeDtypeStruct(w_kv.shape, _WIRE_DTYPE),
                   jax.ShapeDtypeStruct((1, d_model), F32)),
        in_specs=[_rows(tm, d_model), _rows(tm, d_model), _const((1, d_model)), _const(w_kv.shape),
                  _rows(tm, LANES), _rows(tm, LANES)] + [hspec] * (2 * n_users),
        out_specs=(_rows(tm, d_model), _resident(w_kv.shape, (0, 0)), _const((1, d_model))),
        scratch_shapes=[pltpu.VMEM(w_kv.shape, F32)],
        compiler_params=_params(),
    )(dx, x, kv_g, w_kv, cos, ssin, *dks, *dvs)


def _q_fwd(x, pre_g, w_q, cos, ssin, layer, j, name):
    t, d_model = x.shape
    tm = _row_tile(t, PROJ_TILE)

    def body(x_ref, g_ref, w_ref, cos_ref, sin_ref, q_ref):
        q = _mm(_rms_fwd(x_ref[...], g_ref[...]), w_ref[...])
        for p in range(d_model // LANES):
            cols = slice(p * LANES, (p + 1) * LANES)
            q_ref[:, cols] = (_rope_fwd(q[:, cols], cos_ref[...], sin_ref[...]) * ATTN_SCALE).astype(q_ref.dtype)

    return pl.pallas_call(
        body, name=name, grid=(t // tm,), out_shape=jax.ShapeDtypeStruct((t, d_model), _ACT_DTYPE),
        in_specs=[_rows(tm, d_model), _const((None, 1, d_model), (layer, 0, 0)), _const((None, d_model, d_model), (j, 0, 0)),
                  _rows(tm, LANES), _rows(tm, LANES)],
        out_specs=_rows(tm, d_model), compiler_params=_params(),
    )(x, pre_g, w_q, cos, ssin)


def _q_bwd(dx, dqs, x, pre_g, w_q, cos, ssin, layer, j, name):
    t, d_model = x.shape
    tm = _row_tile(t, PROJ_TILE)
    nt = t // tm

    def body(dx_ref, dq_ref, x_ref, g_ref, w_ref, cos_ref, sin_ref, dxi_ref, dw_ref, dg_ref, acc):
        i = pl.program_id(0)

        @pl.when(i == 0)
        def _():
            dg_ref[...] = jnp.zeros_like(dg_ref)
            acc[...] = jnp.zeros_like(acc)

        xv = x_ref[...]
        g = g_ref[...]
        parts = []
        for p in range(d_model // LANES):
            cols = slice(p * LANES, (p + 1) * LANES)
            parts.append(_rope_bwd(dq_ref[:, cols] * ATTN_SCALE, cos_ref[...], sin_ref[...]))
        dq = jnp.concatenate(parts, axis=1).astype(_MXU_DTYPE)
        acc[...] += _mm_ta(_rms_fwd(xv, g), dq)
        dxp, dg = _rms_bwd(xv, g, _mm_tb(dq, w_ref[...]))
        dg_ref[...] += dg
        dxi_ref[...] = dx_ref[...] + dxp

        @pl.when(i == nt - 1)
        def _():
            dw_ref[...] = acc[...].astype(dw_ref.dtype)

    return pl.pallas_call(
        body, name=name, grid=(nt,),
        out_shape=(jax.ShapeDtypeStruct((t, d_model), F32), jax.ShapeDtypeStruct((d_model, d_model), _WIRE_DTYPE),
                   jax.ShapeDtypeStruct((1, d_model), F32)),
        in_specs=[_rows(tm, d_model), _rows(tm, d_model), _rows(tm, d_model), _const((None, 1, d_model), (layer, 0, 0)),
                  _const((None, d_model, d_model), (j, 0, 0)), _rows(tm, LANES), _rows(tm, LANES)],
        out_specs=(_rows(tm, d_model), _resident((d_model, d_model), (0, 0)), _const((1, d_model))),
        scratch_shapes=[pltpu.VMEM((d_model, d_model), F32)],
        compiler_params=_params(),
    )(dx, dqs, x, pre_g, w_q, cos, ssin)


def _stack_heads(pairs):
    low = _low_half(pairs[0].shape)
    zero = jnp.zeros_like(pairs[0])
    return jnp.concatenate([h for blk in pairs for h in (jnp.where(low, blk, zero), jnp.where(low, zero, blk))], axis=0)


def _unstack_heads(stacked, i):
    a, b = stacked[2 * i * BLOCK:(2 * i + 1) * BLOCK], stacked[(2 * i + 1) * BLOCK:(2 * i + 2) * BLOCK]
    return jnp.where(_low_half(a.shape), a, b)


def _attn_scores(q_pairs, k2, n, sinks):
    qst = _stack_heads(q_pairs)
    s = _mm_tb(qst, k2)
    row = lax.broadcasted_iota(jnp.int32, s.shape, 0)
    col = lax.broadcasted_iota(jnp.int32, s.shape, 1)
    rel = BLOCK + (row & (BLOCK - 1)) - col
    valid = (rel >= 0) & (rel < WINDOW) & (n * BLOCK + col - BLOCK >= 0)
    s = jnp.where(valid, s, NEG_INF)
    rows1 = lax.broadcasted_iota(jnp.int32, (s.shape[0], 1), 0)
    sink = jnp.full((s.shape[0], 1), sinks[-1], F32)
    for i in reversed(range(len(sinks) - 1)):
        sink = jnp.where(rows1 < (i + 1) * BLOCK, sinks[i], sink)
    return qst, s, sink


def _attn_fwd(qs, kdup, vdup, sinks, j, name):
    t, d_model = qs.shape
    nb = t // BLOCK
    n_pairs = d_model // LANES
    per_group = n_pairs // N_KV_HEADS

    def body(sink_ref, q_ref, kp_ref, ko_ref, vp_ref, vo_ref, o_ref, lse_ref):
        n = pl.program_id(0)
        lane = _lane_iota((BLOCK, LANES))
        lse = jnp.zeros((BLOCK, LANES), F32)
        for hk in range(N_KV_HEADS):
            pairs = range(hk * per_group, (hk + 1) * per_group)
            heads = range(2 * pairs[0], 2 * pairs[-1] + 2)
            k2 = jnp.concatenate([kp_ref[hk], ko_ref[hk]], axis=0)
            v2 = jnp.concatenate([vp_ref[hk], vo_ref[hk]], axis=0)
            _, s, sink = _attn_scores([q_ref[:, p * LANES:(p + 1) * LANES] for p in pairs], k2, n,
                                      [sink_ref[j, h] for h in heads])
            m = jnp.maximum(jnp.max(s, axis=-1, keepdims=True), sink)
            pe = jnp.exp(s - m)
            denom = jnp.sum(pe, axis=-1, keepdims=True) + jnp.exp(sink - m)
            o2 = _mm(pe, v2) / denom
            l2 = m + jnp.log(denom)
            for i, p in enumerate(pairs):
                o_ref[:, p * LANES:(p + 1) * LANES] = _unstack_heads(o2, i).astype(o_ref.dtype)
            for i, h in enumerate(heads):
                lse = jnp.where(lane == h, l2[i * BLOCK:(i + 1) * BLOCK], lse)
        lse_ref[...] = lse

    prev = pl.BlockSpec((N_KV_HEADS, BLOCK, LANES), lambda n: (0, jnp.maximum(n - 1, 0), 0))
    own = pl.BlockSpec((N_KV_HEADS, BLOCK, LANES), lambda n: (0, n, 0))
    return pl.pallas_call(
        body, name=name, grid=(nb,),
        out_shape=(jax.ShapeDtypeStruct((t, d_model), _ACT_DTYPE), jax.ShapeDtypeStruct((t, LANES), F32)),
        in_specs=[pl.BlockSpec(memory_space=pltpu.SMEM), _rows(BLOCK, d_model), prev, own, prev, own],
        out_specs=(_rows(BLOCK, d_model), _rows(BLOCK, LANES)), compiler_params=_params(),
    )(sinks, qs, kdup, kdup, vdup, vdup)


def _attn_bwd(qs, kdup, vdup, sinks, lse, do, j, name):
    t, d_model = qs.shape
    nb = t // BLOCK
    n_pairs = d_model // LANES
    per_group = n_pairs // N_KV_HEADS
    rev = lambda n: nb - 1 - n

    def body(sink_ref, q_ref, kp_ref, ko_ref, vp_ref, vo_ref, lse_ref, do_ref, dq_ref, dk_ref, dv_ref, ds_ref, ck, cv):
        i = pl.program_id(0)
        n = nb - 1 - i

        @pl.when(i == 0)
        def _():
            ck[...] = jnp.zeros_like(ck)
            cv[...] = jnp.zeros_like(cv)
            ds_ref[...] = jnp.zeros_like(ds_ref)

        lane = _lane_iota((BLOCK, LANES))
        lane1 = _lane_iota((1, LANES))
        lsev = lse_ref[...]
        dsink = jnp.zeros((1, LANES), F32)
        for hk in range(N_KV_HEADS):
            pairs = range(hk * per_group, (hk + 1) * per_group)
            heads = range(2 * pairs[0], 2 * pairs[-1] + 2)
            k2 = jnp.concatenate([kp_ref[hk], ko_ref[hk]], axis=0)
            v2 = jnp.concatenate([vp_ref[hk], vo_ref[hk]], axis=0)
            qst, s, sink = _attn_scores([q_ref[:, p * LANES:(p + 1) * LANES] for p in pairs], k2, n,
                                        [sink_ref[j, h] for h in heads])
            l2 = jnp.concatenate([jnp.sum(jnp.where(lane == h, lsev, 0.0), axis=-1, keepdims=True) for h in heads], axis=0)
            pn = jnp.exp(s - l2)
            dost = _stack_heads([do_ref[:, p * LANES:(p + 1) * LANES] for p in pairs])
            dp = _mm_tb(dost, v2)
            dr = jnp.sum(pn * dp, axis=-1, keepdims=True)
            dsm = (pn * (dp - dr)).astype(_MXU_DTYPE)
            dsk = -jnp.exp(sink - l2) * dr
            for i, h in enumerate(heads):
                dsink = dsink + jnp.where(lane1 == h, jnp.sum(dsk[i * BLOCK:(i + 1) * BLOCK]), 0.0)
            dq2 = _mm(dsm, k2)
            for i, p in enumerate(pairs):
                dq_ref[:, p * LANES:(p + 1) * LANES] = _unstack_heads(dq2, i)
            for acc, carry, ref in ((_mm_ta(dsm, qst), ck, dk_ref), (_mm_ta(pn, dost), cv, dv_ref)):
                folded = acc + pltpu.roll(acc, 64, 1)
                ref[hk] = folded[BLOCK:] + carry[hk]
                carry[hk] = folded[:BLOCK]
        ds_ref[...] += dsink

    prev = pl.BlockSpec((N_KV_HEADS, BLOCK, LANES), lambda n: (0, jnp.maximum(rev(n) - 1, 0), 0))
    own = pl.BlockSpec((N_KV_HEADS, BLOCK, LANES), lambda n: (0, rev(n), 0))
    rows = lambda cols: pl.BlockSpec((BLOCK, cols), lambda n: (rev(n), 0))
    heads = jax.ShapeDtypeStruct((N_KV_HEADS, t, LANES), F32)
    return pl.pallas_call(
        body, name=name, grid=(nb,),
        out_shape=(jax.ShapeDtypeStruct((t, d_model), F32), heads, heads, jax.ShapeDtypeStruct((1, LANES), F32)),
        in_specs=[pl.BlockSpec(memory_space=pltpu.SMEM), rows(d_model), prev, own, prev, own, rows(LANES), rows(d_model)],
        out_specs=(rows(d_model), own, own, _const((1, LANES))),
        scratch_shapes=[pltpu.VMEM((N_KV_HEADS, BLOCK, LANES), F32), pltpu.VMEM((N_KV_HEADS, BLOCK, LANES), F32)],
        compiler_params=_params(),
    )(sinks, qs, kdup, kdup, vdup, vdup, lse, do)


def _oproj_fwd(x, o, w_o, post_g, layer, j, name):
    t, d_model = x.shape
    tm = _row_tile(t, PROJ_TILE)

    def body(x_ref, o_ref, w_ref, g_ref, xo_ref, mo_ref):
        mo = _mm(o_ref[...], w_ref[...])
        mo_ref[...] = mo
        xo_ref[...] = x_ref[...] + _rms_fwd(mo, g_ref[...])

    full = jax.ShapeDtypeStruct((t, d_model), F32)
    return pl.pallas_call(
        body, name=name, grid=(t // tm,), out_shape=(full, full),
        in_specs=[_rows(tm, d_model), _rows(tm, d_model), _const((None, d_model, d_model), (j, 0, 0)),
                  _const((None, 1, d_model), (layer, 0, 0))],
        out_specs=(_rows(tm, d_model), _rows(tm, d_model)), compiler_params=_params(),
    )(x, o, w_o, post_g)


def _oproj_bwd(dx, mo, o, w_o, post_g, layer, j, name):
    t, d_model = dx.shape
    tm = _row_tile(t, PROJ_TILE)
    nt = t // tm

    def body(dx_ref, mo_ref, o_ref, w_ref, g_ref, do_ref, dw_ref, dg_ref, acc):
        i = pl.program_id(0)

        @pl.when(i == 0)
        def _():
            dg_ref[...] = jnp.zeros_like(dg_ref)
            acc[...] = jnp.zeros_like(acc)

        dmo, dg = _rms_bwd(mo_ref[...], g_ref[...], dx_ref[...])
        dg_ref[...] += dg
        dmo = dmo.astype(_MXU_DTYPE)
        acc[...] += _mm_ta(o_ref[...], dmo)
        do_ref[...] = _mm_tb(dmo, w_ref[...]).astype(do_ref.dtype)

        @pl.when(i == nt - 1)
        def _():
            dw_ref[...] = acc[...].astype(dw_ref.dtype)

    return pl.pallas_call(
        body, name=name, grid=(nt,),
        out_shape=(jax.ShapeDtypeStruct((t, d_model), _ACT_DTYPE), jax.ShapeDtypeStruct((d_model, d_model), _WIRE_DTYPE),
                   jax.ShapeDtypeStruct((1, d_model), F32)),
        in_specs=[_rows(tm, d_model), _rows(tm, d_model), _rows(tm, d_model), _const((None, d_model, d_model), (j, 0, 0)),
                  _const((None, 1, d_model), (layer, 0, 0))],
        out_specs=(_rows(tm, d_model), _resident((d_model, d_model), (0, 0)), _const((1, d_model))),
        scratch_shapes=[pltpu.VMEM((d_model, d_model), F32)],
        compiler_params=_params(),
    )(dx, mo, o, w_o, post_g)


def _loss_grad(y, target, name):
    t, d_model = y.shape
    tm = _row_tile(t, PROJ_TILE)

    def body(y_ref, t_ref, dy_ref, loss_ref):
        i = pl.program_id(0)

        @pl.when(i == 0)
        def _():
            loss_ref[...] = jnp.zeros_like(loss_ref)

        err = y_ref[...] - t_ref[...]
        dy_ref[...] = err / d_model
        loss_ref[...] += 0.5 * jnp.sum(jnp.mean(err * err, axis=-1, keepdims=True), axis=0, keepdims=True)

    return pl.pallas_call(
        body, name=name, grid=(t // tm,),
        out_shape=(jax.ShapeDtypeStruct((t, d_model), F32), jax.ShapeDtypeStruct((1, 1), F32)),
        in_specs=[_rows(tm, d_model), _rows(tm, d_model)], out_specs=(_rows(tm, d_model), _const((1, 1))),
        compiler_params=_params(),
    )(y, target)


def _mesh_position():
    return lax.axis_index("x"), lax.axis_index("y"), lax.axis_index("c")


def _block_of(px, py, pc):
    return 4 * px + 2 * py + pc


def _at_block(ref, axis, block):
    return ref.at[(slice(None),) * axis + (block,)]


def _all_gather(shards, axes, name):
    n = len(shards)

    def body(*refs):
        srcs, outs = refs[:n], refs[n:2 * n]
        send_sems, recv_sems, local_sems = refs[2 * n:]
        x, y, c = _mesh_position()
        me, sibling = (x, y, c), (x, y, 1 - c)
        chips = [(1 - x, y), (x, 1 - y), (1 - x, 1 - y)]

        def blk(i, pos):
            return _at_block(outs[i], axes[i], _block_of(*pos))

        def copy(i, k, block, to, src=None):
            return pltpu.make_async_remote_copy(
                src_ref=blk(i, block) if src is None else src, dst_ref=blk(i, block),
                send_sem=send_sems.at[i, k], recv_sem=recv_sems.at[i, k], device_id=to, device_id_type=MESH)

        mine = [pltpu.make_async_copy(srcs[i], blk(i, me), local_sems.at[i]) for i in range(n)]
        for cp in mine:
            cp.start()
        sent = []
        for i in range(n):
            sent += [copy(i, 1 + k, me, (*chip, c), src=srcs[i]) for k, chip in enumerate(chips)]
            sent.append(copy(i, 0, me, sibling, src=srcs[i]))
        for cp in sent:
            cp.start()
        for i in range(n):
            for k, chip in enumerate(chips):
                copy(i, 1 + k, (*chip, c), me).wait_recv()
                passed = copy(i, 4 + k, (*chip, c), sibling)
                passed.start()
                sent.append(passed)
        for i in range(n):
            copy(i, 0, sibling, me).wait_recv()
            for k, chip in enumerate(chips):
                copy(i, 4 + k, (*chip, 1 - c), me).wait_recv()
        for cp in sent:
            cp.wait_send()
        for cp in mine:
            cp.wait()

    hbm = pl.BlockSpec(memory_space=pl.ANY)
    return pl.pallas_call(
        body, name=name,
        out_shape=tuple(jax.ShapeDtypeStruct(s.shape[:a] + (N_DEV,) + s.shape[a:], s.dtype) for s, a in zip(shards, axes)),
        in_specs=[hbm] * n, out_specs=(hbm,) * n,
        scratch_shapes=[pltpu.SemaphoreType.DMA((n, 7)), pltpu.SemaphoreType.DMA((n, 7)), pltpu.SemaphoreType.DMA((n,))],
    )(*shards)


GATHER, SCATTER, GATHER_CHIPS, GATHER_SIBLING = "gather", "scatter", "gather_chips", "gather_sibling"
COPIES = {GATHER: N_DEV - 1, SCATTER: N_DEV - 1, GATHER_CHIPS: 4, GATHER_SIBLING: 3}


def _land_shape(kind, s, axis):
    if kind == SCATTER:
        return (N_DEV,) + s.shape[:axis] + s.shape[axis + 1:]
    return s.shape[:axis] + (N_DEV,) + s.shape[axis:]


def _plan(kind, srcs, lands, axes):
    x, y, c = _mesh_position()
    my_block = _block_of(x, y, c)
    flips = {GATHER_CHIPS: (1, 4, 2, 6), GATHER_SIBLING: (4, 2, 6)}.get(kind, range(1, N_DEV))
    others = [(1 - x if k & 4 else x, 1 - y if k & 2 else y, 1 - c if k & 1 else c) for k in flips]
    remote = []
    for src, land, axis in zip(srcs, lands, axes):
        if kind == SCATTER:
            mine = land.at[my_block]
            remote += [(_at_block(src, axis, _block_of(*peer)), mine, peer, land.at[_block_of(*peer)]) for peer in others]
        elif kind == GATHER_SIBLING:
            for px, py, _ in others:
                mine, theirs = _at_block(land, axis, _block_of(px, py, c)), _at_block(land, axis, _block_of(px, py, 1 - c))
                remote.append((mine, mine, (x, y, 1 - c), theirs))
        else:
            mine = _at_block(land, axis, my_block)
            remote += [(src, mine, peer, _at_block(land, axis, _block_of(*peer))) for peer in others]
    return remote


def _remote(src, dst, send_sems, recv_sems, k, peer):
    return pltpu.make_async_remote_copy(src_ref=src, dst_ref=dst, send_sem=send_sems.at[k], recv_sem=recv_sems.at[k],
                                        device_id=peer, device_id_type=MESH)


_HBM = pl.BlockSpec(memory_space=pltpu.HBM)
_SEM = pl.BlockSpec(memory_space=pltpu.SEMAPHORE)
_SPLIT = dict(has_side_effects=pltpu.SideEffectType.DATAFLOW_SIDE_EFFECTING)


def _landing_zone(kind, s, axis, me):
    land = lax.empty(_land_shape(kind, s, axis), s.dtype)
    if kind == SCATTER:
        return lax.dynamic_update_slice_in_dim(land, lax.dynamic_slice_in_dim(s, me, 1, axis).reshape((1,) + land.shape[1:]), me, 0)
    return lax.dynamic_update_slice_in_dim(land, jnp.expand_dims(s, axis), me, axis)


def _exchange_start(kind, arrays, axes, after, name):
    n = len(arrays)
    if kind == GATHER_SIBLING:
        passed = list(arrays)
    else:
        me = _block_of(*_mesh_position())
        passed = list(arrays) + [_landing_zone(kind, s, a, me) for s, a in zip(arrays, axes)]
    n_sems = n * COPIES[kind]

    def body(*refs):
        land_refs = refs[len(passed) - n:len(passed)]
        send_sems, recv_sems = refs[len(passed) + 1], refs[len(passed) + 2]
        token = refs[-1]
        for k, (src, dst, peer, _) in enumerate(_plan(kind, refs[:n], land_refs, axes)):
            _remote(src, dst, send_sems, recv_sems, k, peer).start()
        token[...] = jnp.zeros_like(token)

    out = pl.pallas_call(
        body, name=name,
        out_shape=(pltpu.SemaphoreType.DMA((n_sems,)), pltpu.SemaphoreType.DMA((n_sems,)),
                   *[pltpu.HBM(a.shape, a.dtype) for a in passed], jax.ShapeDtypeStruct((8, LANES), F32)),
        in_specs=[_HBM] * len(passed) + [pl.BlockSpec(memory_space=pl.ANY)],
        out_specs=(_SEM, _SEM, *[_HBM] * len(passed), pl.BlockSpec(memory_space=pltpu.VMEM)),
        input_output_aliases={i: 2 + i for i in range(len(passed))},
        compiler_params=pltpu.CompilerParams(**_SPLIT),
    )(*[pltpu.with_memory_space_constraint(a, pltpu.HBM) for a in passed], after)
    return (kind, axes, n, out[:-1]), out[-1]


def _exchange_wait(handle, after, name):
    kind, axes, n, (send_sems, recv_sems, *thru) = handle

    def body(*refs):
        land_refs = refs[len(thru) - n:len(thru)]
        send_sems, recv_sems = refs[len(thru)], refs[len(thru) + 1]
        for k, (src, _, peer, arrives) in enumerate(_plan(kind, refs[:n], land_refs, axes)):
            cp = _remote(src, arrives, send_sems, recv_sems, k, peer)
            cp.wait_send()
            cp.wait_recv()

    out = pl.pallas_call(
        body, name=name,
        out_shape=tuple(pltpu.HBM(a.shape, a.dtype) for a in thru),
        in_specs=[_HBM] * len(thru) + [_SEM, _SEM, pl.BlockSpec(memory_space=pl.ANY)], out_specs=(_HBM,) * len(thru),
        input_output_aliases={i: i for i in range(len(thru))},
        compiler_params=pltpu.CompilerParams(**_SPLIT),
    )(*thru, send_sems, recv_sems, after)
    return out[len(thru) - n:]


def _adamw_math(w, g, m, v):
    m = ADAM_B1 * m + (1.0 - ADAM_B1) * g
    v = ADAM_B2 * v + (1.0 - ADAM_B2) * jnp.square(g)
    m_hat = m / (1.0 - ADAM_B1 ** ADAM_STEP)
    v_hat = v / (1.0 - ADAM_B2 ** ADAM_STEP)
    delta = -ADAM_LR * (m_hat / (jnp.sqrt(v_hat) + ADAM_EPS) + ADAM_WD * w)
    return delta, m, v


def _update_tile(rows):
    for tr in (256, 176, 128, 64, 32, 16):
        if rows % tr == 0 and rows >= 2 * tr:
            return tr
    return rows


def _adamw(parts, w, m, v, slab, so_far, name):
    rows, c = w.shape
    r = parts.shape[1]
    tr = _update_tile(r)
    first = slab * (r // tr)
    if so_far is None:
        so_far = tuple(lax.empty((rows, c), F32) for _ in range(4))

    def body(p_ref, w_ref, m_ref, v_ref, *refs):
        g_ref, d_ref, mo_ref, vo_ref = refs[4:]
        g = p_ref[0].astype(F32)
        for s in range(1, N_DEV):
            g = g + p_ref[s].astype(F32)
        g_ref[...] = g
        d_ref[...], mo_ref[...], vo_ref[...] = _adamw_math(w_ref[...], g, m_ref[...], v_ref[...])

    out = jax.ShapeDtypeStruct((rows, c), F32)
    tile = pl.BlockSpec((tr, c), lambda i: (first + i, 0))
    return pl.pallas_call(
        body, name=name, grid=(r // tr,), out_shape=(out,) * 4,
        in_specs=[pl.BlockSpec((N_DEV, tr, c), lambda i: (0, i, 0))] + [tile] * 3 + [pl.BlockSpec(memory_space=pl.ANY)] * 4,
        out_specs=(tile,) * 4, input_output_aliases={4 + k: k for k in range(4)}, compiler_params=_params(),
    )(parts, w, m, v, *so_far)


def _adamw_small(parts, picks, weights, name):
    n = len(parts)

    def body(*refs):
        p_refs, wmv, outs = refs[:n], refs[n:4 * n], refs[4 * n:]
        me = _block_of(*_mesh_position())
        for i in range(n):
            g = picks[i](p_refs[i], 0, me)
            for s in range(1, N_DEV):
                g = g + picks[i](p_refs[i], s, me)
            w_ref, m_ref, v_ref = wmv[3 * i:3 * i + 3]
            g_ref, d_ref, mo_ref, vo_ref = outs[4 * i:4 * i + 4]
            g_ref[...] = g
            d_ref[...], mo_ref[...], vo_ref[...] = _adamw_math(w_ref[...], g, m_ref[...], v_ref[...])

    flat = [a for wmv in weights for a in wmv]
    out = pl.pallas_call(
        body, name=name,
        out_shape=tuple(jax.ShapeDtypeStruct(w.shape, F32) for w, _, _ in weights for _ in range(4)),
        compiler_params=pltpu.CompilerParams(vmem_limit_bytes=VMEM_LIMIT),
    )(*parts, *flat)
    return [tuple(out[4 * i:4 * i + 4]) for i in range(n)]


def kernel(x, positions, mix_pre_g, mix_post_g, pool_w, pool_scale, kv_norm_g, w_kv, w_q, w_o, sinks, ffn_pre_g, ffn_post_g, ffn_w_in, ffn_conv_w, ffn_conv_b, ffn_w_out, loss_target, m_mix_pre_g, m_mix_post_g, m_pool_w, m_pool_scale, m_kv_norm_g, m_w_kv, m_w_q, m_w_o, m_sinks, m_ffn_pre_g, m_ffn_post_g, m_ffn_w_in, m_ffn_conv_w, m_ffn_conv_b, m_ffn_w_out, v_mix_pre_g, v_mix_post_g, v_pool_w, v_pool_scale, v_kv_norm_g, v_w_kv, v_w_q, v_w_o, v_sinks, v_ffn_pre_g, v_ffn_post_g, v_ffn_w_in, v_ffn_conv_w, v_ffn_conv_b, v_ffn_w_out):
    depth, d_model = mix_pre_g.shape
    n_a = pool_w.shape[0]
    n_b = w_q.shape[0]
    t = x.shape[1]
    fs = ffn_w_in.shape[2]
    half = N_DEV // 2
    n_heads = d_model // HEAD_DIM
    x0 = x.reshape(t, d_model)
    target = loss_target.reshape(t, d_model)

    inv_freq = 1.0 / (ROPE_THETA ** (jnp.arange(0, HEAD_DIM, 2, dtype=F32) / HEAD_DIM))
    ang = positions.reshape(t).astype(F32)[:, None] * inv_freq
    cos, sin = jnp.cos(ang), jnp.sin(ang)
    cos = jnp.tile(cos, (1, 2 * LANES // HEAD_DIM))
    ssin = jnp.tile(jnp.concatenate([-sin, sin], axis=1), (1, LANES // HEAD_DIM))

    wire = lambda a: a.astype(_WIRE_DTYPE)
    by_hidden = lambda a: a.transpose(0, 2, 1)
    w_in_b, w_out_b = wire(by_hidden(ffn_w_in)), wire(ffn_w_out)
    pool_w_g, pool_scale_g, conv_w_g = _all_gather([wire(pool_w), pool_scale, ffn_conv_w], [2, 0, 0], "gather_first")
    groups = []
    for l in range(depth):
        if l == n_a:
            groups.append(("attn", [wire(w_kv), wire(w_q), wire(w_o)], [0, 1, 1], l))
        groups.append((l, [w_in_b[l:l + 1], w_out_b[l:l + 1]], [1, 1], l))
    over_ici, to_sibling, tokens, after = {}, {}, [], pool_w_g
    for key, shards, axes, _ in groups:
        over_ici[key], after = _exchange_start(GATHER_CHIPS, shards, axes, after, f"gather_chips_{key}")
        tokens.append(after)
    started = functools.reduce(lambda a, b: a + b, [tk[0, 0] for tk in tokens])

    def pass_on(layer, after):
        sent = jnp.zeros((), F32)
        for key, _, axes, first in groups:
            if first == layer:
                lands = _exchange_wait(over_ici[key], after, f"gather_chips_wait_{key}")
                to_sibling[key], tk = _exchange_start(GATHER_SIBLING, lands, axes, after, f"gather_sibling_{key}")
                sent = sent + tk[0, 0]
        return sent

    w_in_l, w_out_l = {}, {}
    pool_scale_f = pool_scale_g.transpose(1, 0, 2).reshape(n_a, 1, d_model)
    conv_w_f = conv_w_g.transpose(1, 2, 0, 3)
    pool_w_f = pool_w_g.reshape(n_a, len(POOL_WINDOWS), d_model // len(POOL_WINDOWS), -1)
    conv_b_f = ffn_conv_b.reshape(depth, N_DEV, fs)
    g3 = lambda a: a.reshape(a.shape[0], 1, a.shape[1])
    mix_pre, mix_post, ffn_pre, ffn_post = g3(mix_pre_g) + started, g3(mix_post_g), g3(ffn_pre_g), g3(ffn_post_g)
    kv_g = kv_norm_g.reshape(1, d_model)
    w_kv_f = w_q_f = w_o_f = None

    saved = []
    xc = x0
    kdup = vdup = x_kv = None
    for l in range(depth):
        x_in = xc
        if l < n_a:
            x_mid, dsave, yu = _pool_fwd(x_in, mix_pre, mix_post, pool_w_f, pool_scale_f, l, f"pool_fwd_{l}")
            mixer = (dsave, yu)
        else:
            j = l - n_a
            if j == 0:
                x_kv = x_in
                w_kv_g, w_q_g, w_o_g = _exchange_wait(to_sibling["attn"], x_in, "gather_sibling_wait_attn")
                w_kv_f = w_kv_g.reshape(d_model, -1)
                w_q_f = w_q_g.reshape(n_b, d_model, d_model)
                w_o_f = w_o_g.reshape(n_b, d_model, d_model)
                kdup, vdup = _kv_fwd(x_kv, kv_g, w_kv_f, cos, ssin, "kv_fwd")
            qs = _q_fwd(x_in, mix_pre, w_q_f, cos, ssin, l, j, f"q_fwd_{l}")
            o, lse = _attn_fwd(qs, kdup, vdup, sinks, j, f"attn_fwd_{l}")
            x_mid, mo = _oproj_fwd(x_in, o, w_o_f, mix_post, l, j, f"oproj_fwd_{l}")
            mixer = (qs, o, lse, mo)
        if l == 0:
            pass_on(0, x_mid)
        w_in_l[l], w_out_g = _exchange_wait(to_sibling[l], x_mid, f"gather_sibling_wait_{l}")
        w_out_l[l] = w_out_g.reshape(1, half * fs, d_model)
        xc, u, uc, f = _ffn_fwd(x_mid, ffn_pre, ffn_post, w_in_l[l], conv_w_f, conv_b_f, w_out_l[l], l, 0, f"ffn_fwd_{l}")
        saved.append((x_in, x_mid, u, uc, f, mixer))
        if l + 1 < depth:
            mix_pre = mix_pre + pass_on(l + 1, xc)

    dx, loss_part = _loss_grad(xc, target, "loss")

    gconv_w, gconv_b = [None] * depth, [None] * depth
    gmix_pre, gmix_post, gffn_pre, gffn_post = [None] * depth, [None] * depth, [None] * depth, [None] * depth
    gpool_scale, gsinks = [None] * n_a, [None] * n_b
    dks, dvs = [], []
    gkv_g = None
    whole = lambda cols: pl.BlockSpec((t, cols), lambda c: (0, 0), pipeline_mode=pl.Buffered(1))
    per_out = lambda cols: pl.BlockSpec((None, t, cols), lambda c: (c, 0, 0))
    by_dev = lambda g: g.reshape(N_DEV, -1, g.shape[-1])
    def small_grads():
        cat = lambda rows: jnp.concatenate(rows, axis=0)
        row = lambda a: a.reshape(1, -1)
        everything = lambda ref, s, me: ref[s]
        lanes = pool_scale.shape[1]
        return [("mix_pre_g", cat(gmix_pre), everything, (mix_pre_g, m_mix_pre_g, v_mix_pre_g)),
                ("mix_post_g", cat(gmix_post), everything, (mix_post_g, m_mix_post_g, v_mix_post_g)),
                ("kv_norm_g", gkv_g, everything, (row(kv_norm_g), row(m_kv_norm_g), row(v_kv_norm_g))),
                ("sinks", cat(gsinks), lambda ref, s, me: ref[s, :, pl.ds(0, n_heads)], (sinks, m_sinks, v_sinks)),
                ("ffn_pre_g", cat(gffn_pre), everything, (ffn_pre_g, m_ffn_pre_g, v_ffn_pre_g)),
                ("ffn_post_g", cat(gffn_post), everything, (ffn_post_g, m_ffn_post_g, v_ffn_post_g)),
                ("ffn_conv_b", jnp.stack(gconv_b).reshape(depth, N_DEV * fs), everything, (ffn_conv_b, m_ffn_conv_b, v_ffn_conv_b)),
                ("pool_scale", cat(gpool_scale), lambda ref, s, me: ref[s, :, pl.ds(pl.multiple_of(me * lanes, lanes), lanes)],
                 (pool_scale, m_pool_scale, v_pool_scale)),
                ("ffn_conv_w", jnp.stack(gconv_w).transpose(0, 2, 1, 3), lambda ref, s, me: ref[s, :, me],
                 (ffn_conv_w, m_ffn_conv_w, v_ffn_conv_w))]

    flying = []

    def launch(going, after, name):
        handle, token = _exchange_start(SCATTER, [g for _, _, g, _ in going], [a for _, _, _, a in going], after, name)
        flying.append(([(nm, slab) for nm, slab, _, _ in going], handle))
        return token

    post = mix_post
    ffn_post_b = ffn_post
    token = None
    for l in reversed(range(depth)):
        x_in, x_mid, u, uc, f, mixer = saved[l]
        duc, gout, gconv_b[l], gffn_post[l] = _ffn_bwd_out(dx, f, ffn_post_b, uc, conv_b_f, w_out_l[l], l, 0, f"ffn_bwd_out_{l}")
        going = [("ffn_w_out", l, by_dev(gout), 0)]
        pre = ffn_pre
        if l == 0:
            token = launch(going, gout, "scatter_start_0_out")
            going = []
            pre = pre + token[0, 0]
        dx, du, hf, gconv_w[l], gffn_pre[l] = _ffn_bwd_in(dx, x_mid, pre, duc, u, conv_w_f, w_in_l[l], l, 0, f"ffn_bwd_in_{l}")
        gin = _tn_matmul(du, hf, per_out(fs), whole(d_model), N_DEV, fs, d_model, f"grad_w_in_{l}", _WIRE_DTYPE)
        going.append(("ffn_w_in", l, gin, 0))
        if l == 0:
            token = launch(going, dx, "scatter_start_0_in")
            going = []
            post = post + token[0, 0]
        if l < n_a:
            dsave, yu = mixer
            dx, dyu, gpool_scale[l], gmix_pre[l], gmix_post[l] = _pool_bwd(
                dx, x_in, mix_pre, post, dsave, yu, pool_w_f, pool_scale_f, l, f"pool_bwd_{l}")
            gc = d_model // len(POOL_WINDOWS)
            by_group = pl.BlockSpec((t, gc), lambda c: (0, c))
            gpool = _tn_matmul(dsave, dyu, by_group, by_group, len(POOL_WINDOWS), gc, gc, f"grad_pool_w_{l}", _WIRE_DTYPE)
            going.append(("pool_w", l, gpool.reshape(len(POOL_WINDOWS), N_DEV, -1, gc), 1))
        else:
            j = l - n_a
            qs, o, lse, mo = mixer
            do, go, gmix_post[l] = _oproj_bwd(dx, mo, o, w_o_f, post, l, j, f"oproj_bwd_{l}")
            dqs, dk, dv, gsinks[j] = _attn_bwd(qs, kdup, vdup, sinks, lse, do, j, f"attn_bwd_{l}")
            dks.append(dk)
            dvs.append(dv)
            dx, gq, gmix_pre[l] = _q_bwd(dx, dqs, x_in, mix_pre, w_q_f, cos, ssin, l, j, f"q_bwd_{l}")
            if j == 0:
                dx, gkv, gkv_g = _kv_bwd(dx, x_kv, kv_g, w_kv_f, cos, ssin, dks, dvs, "kv_bwd")
                going.append(("w_kv", 0, by_dev(gkv), 0))
            going += [("w_o", j, by_dev(go), 0), ("w_q", j, by_dev(gq), 0)]
        after = dx
        if l == 0:
            small = small_grads()
            leaving = [g for _, g, _, _ in small] + [jnp.broadcast_to(loss_part, (1, LANES))]
            small_flight, after = _exchange_start(GATHER, leaving, [0] * len(leaving), dx, "gather_small_grads")
        token = launch(going, after, f"scatter_start_{l}")
        ffn_post_b = ffn_post_b + token[0, 0]

    grad_x = dx.reshape(x.shape)

    shard = {"pool_w": (pool_w, m_pool_w, v_pool_w), "w_kv": (w_kv, m_w_kv, v_w_kv), "w_q": (w_q, m_w_q, v_w_q),
             "w_o": (w_o, m_w_o, v_w_o), "ffn_w_in": tuple(by_hidden(a) for a in (ffn_w_in, m_ffn_w_in, v_ffn_w_in)),
             "ffn_w_out": (ffn_w_out, m_ffn_w_out, v_ffn_w_out)}
    big = {}

    def arrive(flights, after):
        for idx, (names, handle) in flights:
            parts = _exchange_wait(handle, after, f"scatter_wait_{idx}")
            for (nm, slab), p in zip(names, parts):
                cols = p.shape[-1]
                w2, m2, v2 = (a.reshape(-1, cols) for a in shard[nm])
                big[nm] = _adamw(p.reshape(N_DEV, -1, cols), w2, m2, v2, slab, big.get(nm), f"adamw_{nm}_{slab}")
                after = big[nm][0]
        return after

    done = arrive(list(enumerate(flying)), token)
    *small_parts, loss_parts = _exchange_wait(small_flight, done, "gather_small_grads_wait")
    loss = jnp.sum(loss_parts[:, 0, 0])
    upd = _adamw_small(small_parts, [pick for _, _, pick, _ in small], [wmv for _, _, _, wmv in small], "adamw_small")
    res = {nm: tuple(r.reshape(shard[nm][0].shape) for r in out) for nm, out in big.items()}
    res["ffn_w_in"] = tuple(by_hidden(r) for r in res["ffn_w_in"])
    for (nm, _, _, _), out in zip(small, upd):
        res[nm] = tuple(a.reshape(kv_norm_g.shape) for a in out) if nm == "kv_norm_g" else out

    order = ["mix_pre_g", "mix_post_g", "pool_w", "pool_scale", "kv_norm_g", "w_kv", "w_q", "w_o", "sinks", "ffn_pre_g",
             "ffn_post_g", "ffn_w_in", "ffn_conv_w", "ffn_conv_b", "ffn_w_out"]
    return (loss, grad_x, *[res[nm][0] for nm in order], *[res[nm][1] for nm in order],
            *[res[nm][2] for nm in order], *[res[nm][3] for nm in order])
```

```python
import functools
import math

import jax
import jax.numpy as jnp
from jax import lax
from jax.experimental import pallas as pl
from jax.experimental.pallas import tpu as pltpu

F32 = jnp.float32
_MXU_DTYPE = jnp.bfloat16
_ACT_DTYPE = jnp.bfloat16
_WIRE_DTYPE = jnp.bfloat16
_SAVE_DTYPE = jnp.bfloat16

N_DEV = 8
POOL_WINDOWS = (2, 4, 8, 16)
POOL_HALO = 32
assert POOL_WINDOWS == tuple(2 ** (g + 1) for g in range(len(POOL_WINDOWS))) and 8 * len(POOL_WINDOWS) <= POOL_HALO
HEAD_DIM = 64
N_KV_HEADS = 4
WINDOW = 128
BLOCK = 128
LANES = 128
ROPE_THETA = 10000.0
ATTN_SCALE = 1.0 / math.sqrt(HEAD_DIM)
NEG_INF = -1e30
RMS_EPS = 1e-6
CONV_HALO = 8
SAVE_HALO = 16
PROJ_TILE = 512
ADAM_LR = 0.001
ADAM_B1 = 0.9
ADAM_B2 = 0.999
ADAM_EPS = 1e-08
ADAM_WD = 0.01
ADAM_STEP = 10
VMEM_LIMIT = 56 * 1024 * 1024
MESH = pl.DeviceIdType.MESH


def _params(n_axes=1, vmem=VMEM_LIMIT):
    return pltpu.CompilerParams(dimension_semantics=("arbitrary",) * n_axes, vmem_limit_bytes=vmem)


def _resident(shape, index):
    return pl.BlockSpec(shape, lambda *_: index, pipeline_mode=pl.Buffered(1))


def _const(shape, index=None):
    index = (0,) * len(shape) if index is None else index
    return pl.BlockSpec(shape, lambda *_: index)


def _rows(tm, cols):
    return pl.BlockSpec((tm, cols), lambda i: (i, 0))


def _row_tile(t, most=256):
    for tm in (512, 256, 128, 64, 32, 16, 8):
        if tm <= most and t % tm == 0:
            return tm
    raise ValueError(f"sequence length {t} is not a multiple of 8")


def _mm(a, b):
    return jnp.dot(a.astype(_MXU_DTYPE), b.astype(_MXU_DTYPE), preferred_element_type=F32)


def _mm_tb(a, b):
    return lax.dot_general(a.astype(_MXU_DTYPE), b.astype(_MXU_DTYPE), (((1,), (1,)), ((), ())),
                           preferred_element_type=F32)


def _mm_ta(a, b):
    return lax.dot_general(a.astype(_MXU_DTYPE), b.astype(_MXU_DTYPE), (((0,), (0,)), ((), ())),
                           preferred_element_type=F32)


def _rms_r(x):
    return lax.rsqrt(jnp.mean(x * x, axis=-1, keepdims=True) + RMS_EPS)


def _rms_fwd(x, g):
    return (x * _rms_r(x)) * g


def _rms_bwd(x, g, dy):
    r = _rms_r(x)
    xh = x * r
    dg = jnp.sum(dy * xh, axis=0, keepdims=True)
    dxh = dy * g
    dx = r * (dxh - xh * jnp.mean(dxh * xh, axis=-1, keepdims=True))
    return dx, dg


_GELU_C = math.sqrt(2.0 / math.pi)


def _gelu_parts(z):
    z2 = z * z
    e = jnp.exp(z * (-2.0 * _GELU_C - (2.0 * _GELU_C * 0.044715) * z2))
    cdf = pl.reciprocal(1.0 + e, approx=False)
    dz = cdf + (z * (cdf * (1.0 - cdf))) * (2.0 * _GELU_C + (6.0 * _GELU_C * 0.044715) * z2)
    return cdf, dz


def _lane_iota(shape):
    return lax.broadcasted_iota(jnp.int32, shape, len(shape) - 1)


def _rope_partner(xb):
    first = (_lane_iota(xb.shape) & 32) == 0
    return jnp.where(first, pltpu.roll(xb, LANES - 32, 1), pltpu.roll(xb, 32, 1))


def _rope_fwd(xb, cos, ssin):
    return xb * cos + _rope_partner(xb) * ssin


def _rope_bwd(dyb, cos, ssin):
    return dyb * cos - _rope_partner(dyb) * ssin


def _low_half(shape):
    return (_lane_iota(shape) & 64) == 0


def _pool_fwd(x, pre_g, post_g, w, scale, layer, name):
    t, d_model = x.shape
    tm = _row_tile(t)
    n_groups, gc = w.shape[1], w.shape[2]

    def body(x_ref, pre_ref, post_ref, w_ref, sc_ref, xo_ref, d_ref, yu_ref, hbuf, sbuf):
        i = pl.program_id(0)

        @pl.when(i == 0)
        def _():
            hbuf[pl.ds(0, POOL_HALO), :] = jnp.zeros((POOL_HALO, d_model), F32)

        xv = x_ref[...]
        hbuf[pl.ds(POOL_HALO, tm), :] = _rms_fwd(xv, pre_ref[...])
        tok = i * tm + lax.broadcasted_iota(jnp.int32, (tm, 1), 0)
        yus = []
        for gi, wnd in enumerate(POOL_WINDOWS):
            first, reach = CONV_HALO * (gi + 1), wnd // 2
            n = POOL_HALO + tm - first
            cols = pl.ds(gi * gc, d_model - gi * gc)
            src = hbuf if gi == 0 else sbuf.at[gi - 1]
            level = src[pl.ds(first, n), cols] + src[pl.ds(first - reach, n), cols]
            if gi + 1 < len(POOL_WINDOWS):
                sbuf[gi, pl.ds(first, n), cols] = level
            h = hbuf[pl.ds(POOL_HALO, tm), pl.ds(gi * gc, gc)]
            cnt = jnp.minimum(tok + 1, wnd).astype(F32)
            dg = level[POOL_HALO - first:, :gc] / cnt - h
            d_ref[:, pl.ds(gi * gc, gc)] = dg.astype(d_ref.dtype)
            yus.append(_mm(dg, w_ref[gi]))
        hbuf[pl.ds(0, POOL_HALO), :] = hbuf[pl.ds(tm, POOL_HALO), :]
        yu = jnp.concatenate(yus, axis=1)
        yu_ref[...] = yu
        xo_ref[...] = xv + _rms_fwd(yu * sc_ref[...], post_ref[...])

    return pl.pallas_call(
        body, name=name, grid=(t // tm,),
        out_shape=(jax.ShapeDtypeStruct((t, d_model), F32), jax.ShapeDtypeStruct((t, d_model), _ACT_DTYPE),
                   jax.ShapeDtypeStruct((t, d_model), F32)),
        in_specs=[_rows(tm, d_model), _const((None, 1, d_model), (layer, 0, 0)), _const((None, 1, d_model), (layer, 0, 0)),
                  _const((None, n_groups, gc, gc), (layer, 0, 0, 0)), _const((None, 1, d_model), (layer, 0, 0))],
        out_specs=(_rows(tm, d_model), _rows(tm, d_model), _rows(tm, d_model)),
        scratch_shapes=[pltpu.VMEM((POOL_HALO + tm, d_model), F32),
                        pltpu.VMEM((len(POOL_WINDOWS) - 1, POOL_HALO + tm, d_model), F32)],
        compiler_params=_params(),
    )(x, pre_g, post_g, w, scale)


def _pool_bwd(dx, x, pre_g, post_g, d, yu, w, scale, layer, name):
    t, d_model = x.shape
    tm = _row_tile(t)
    nt = t // tm
    n_groups, gc = w.shape[1], w.shape[2]
    rev = lambda i: (nt - 1 - i, 0)
    rows = pl.BlockSpec((tm, d_model), rev)

    def body(dx_ref, x_ref, pre_ref, post_ref, d_ref, yu_ref, w_ref, sc_ref,
             dxi_ref, dyu_ref, dsc_ref, dpre_ref, dpost_ref, zbuf, sbuf):
        i = pl.program_id(0)

        @pl.when(i == 0)
        def _():
            zbuf[pl.ds(tm, POOL_HALO), :] = jnp.zeros((POOL_HALO, d_model), F32)
            dsc_ref[...] = jnp.zeros_like(dsc_ref)
            dpre_ref[...] = jnp.zeros_like(dpre_ref)
            dpost_ref[...] = jnp.zeros_like(dpost_ref)

        dxo = dx_ref[...]
        yuv = yu_ref[...]
        sc = sc_ref[...]
        dm, dpost = _rms_bwd(yuv * sc, post_ref[...], dxo)
        dpost_ref[...] += dpost
        dsc_ref[...] += jnp.sum(dm * yuv, axis=0, keepdims=True)
        dyu = dm * sc
        dyu_ref[...] = dyu.astype(dyu_ref.dtype)
        tok = (nt - 1 - i) * tm + lax.broadcasted_iota(jnp.int32, (tm, 1), 0)
        dds = []
        for gi, wnd in enumerate(POOL_WINDOWS):
            cols = pl.ds(gi * gc, gc)
            dd = _mm_tb(dyu[:, gi * gc:(gi + 1) * gc], w_ref[gi])
            cnt = jnp.minimum(tok + 1, wnd).astype(F32)
            zbuf[pl.ds(0, tm), cols] = dd / cnt
            dds.append(dd)
        dhs = []
        for gi, wnd in enumerate(POOL_WINDOWS):
            reach = wnd // 2
            n = tm + POOL_HALO - CONV_HALO * (gi + 1)
            cols = pl.ds(gi * gc, d_model - gi * gc)
            src = zbuf if gi == 0 else sbuf.at[gi - 1]
            level = src[pl.ds(0, n), cols] + src[pl.ds(reach, n), cols]
            if gi + 1 < len(POOL_WINDOWS):
                sbuf[gi, pl.ds(0, n), cols] = level
            dhs.append(level[:tm, :gc] - dds[gi])
        zbuf[pl.ds(tm, POOL_HALO), :] = zbuf[pl.ds(0, POOL_HALO), :]
        dh = jnp.concatenate(dhs, axis=1)
        dxp, dpre = _rms_bwd(x_ref[...], pre_ref[...], dh)
        dpre_ref[...] += dpre
        dxi_ref[...] = dxo + dxp

    vec = jax.ShapeDtypeStruct((1, d_model), F32)
    return pl.pallas_call(
        body, name=name, grid=(nt,),
        out_shape=(jax.ShapeDtypeStruct((t, d_model), F32), jax.ShapeDtypeStruct((t, d_model), _ACT_DTYPE), vec, vec, vec),
        in_specs=[rows, rows, _const((None, 1, d_model), (layer, 0, 0)), _const((None, 1, d_model), (layer, 0, 0)), rows, rows,
                  _const((None, n_groups, gc, gc), (layer, 0, 0, 0)), _const((None, 1, d_model), (layer, 0, 0))],
        out_specs=(rows, rows, _const((1, d_model)), _const((1, d_model)), _const((1, d_model))),
        scratch_shapes=[pltpu.VMEM((tm + POOL_HALO, d_model), F32),
                        pltpu.VMEM((len(POOL_WINDOWS) - 1, tm + POOL_HALO, d_model), F32)],
        compiler_params=_params(),
    )(dx, x, pre_g, post_g, d, yu, w, scale)


def _conv_taps(cw_ref, s):
    return [cw_ref[k, pl.ds(s, 1), :] for k in range(3)]


def _shift_down(v, k, before):
    rolled = pltpu.roll(v, k, 0)
    row = lax.broadcasted_iota(jnp.int32, before.shape, 0)
    head = jnp.where(row < k, pltpu.roll(before, k, 0), rolled[:CONV_HALO])
    return jnp.concatenate([head, rolled[CONV_HALO:]], axis=0)


def _shift_up(v, k, after):
    rows = v.shape[0]
    rolled = pltpu.roll(v, rows - k, 0)
    row = lax.broadcasted_iota(jnp.int32, after.shape, 0)
    tail = jnp.where(row >= CONV_HALO - k, pltpu.roll(after, CONV_HALO - k, 0), rolled[rows - CONV_HALO:])
    return jnp.concatenate([rolled[:rows - CONV_HALO], tail], axis=0)


def _ffn_fwd(x, pre_g, post_g, w_in, conv_w, conv_b, w_out, layer, w_layer, name):
    t, d_model = x.shape
    tm = _row_tile(t)
    fs = w_in.shape[2]
    half = N_DEV // 2

    def body(x_ref, pre_ref, post_ref, win_ref, cw_ref, cb_ref, wout_ref, xo_ref, u_ref, uc_ref, f_ref, carry):
        i = pl.program_id(0)

        @pl.when(i == 0)
        def _():
            carry[...] = jnp.zeros_like(carry)

        xv = x_ref[...]
        hf = _rms_fwd(xv, pre_ref[...]).astype(_MXU_DTYPE)
        f = jnp.zeros((tm, d_model), F32)
        project = lambda b: [_mm_tb(hf, win_ref[s]) for s in (b, b + half)]
        ahead = project(0)
        for b in range(half):
            us, ahead = ahead, project(b + 1) if b + 1 < half else None
            ucs = []
            for s, u in zip((b, b + half), us):
                u_ref[s] = u.astype(u_ref.dtype)
                before = carry[s]
                carry[s] = u[tm - CONV_HALO:]
                w0, w1, w2 = _conv_taps(cw_ref, s)
                uc = ((w0 * _shift_down(u, 2, before) + w1 * _shift_down(u, 1, before)) + w2 * u) + cb_ref[pl.ds(s, 1), :]
                uc_ref[s] = uc.astype(uc_ref.dtype)
                ucs.append(uc)
            gate, val = ucs
            cdf, _ = _gelu_parts(gate)
            f = f + _mm((gate * cdf) * val, wout_ref[pl.ds(b * fs, fs), :])
        f_ref[...] = f
        xo_ref[...] = xv + _rms_fwd(f, post_ref[...])

    tile3 = pl.BlockSpec((N_DEV, tm, fs), lambda i: (0, i, 0))
    saved = jax.ShapeDtypeStruct((N_DEV, t, fs), _SAVE_DTYPE)
    return pl.pallas_call(
        body, name=name, grid=(t // tm,),
        out_shape=(jax.ShapeDtypeStruct((t, d_model), F32), saved, saved, jax.ShapeDtypeStruct((t, d_model), F32)),
        in_specs=[_rows(tm, d_model), _const((None, 1, d_model), (layer, 0, 0)), _const((None, 1, d_model), (layer, 0, 0)),
                  _resident((None, N_DEV, fs, d_model), (w_layer, 0, 0, 0)), _const((None, 3, N_DEV, fs), (layer, 0, 0, 0)),
                  _const((None, N_DEV, fs), (layer, 0, 0)), _resident((None, half * fs, d_model), (w_layer, 0, 0))],
        out_specs=(_rows(tm, d_model), tile3, tile3, _rows(tm, d_model)),
        scratch_shapes=[pltpu.VMEM((N_DEV, CONV_HALO, fs), F32)],
        compiler_params=_params(),
    )(x, pre_g, post_g, w_in, conv_w, conv_b, w_out)


def _ffn_bwd_out(dx, f, post_g, uc, conv_b, w_out, layer, w_layer, name):
    t, d_model = dx.shape
    tm = _row_tile(t)
    nt = t // tm
    fs = uc.shape[2]
    half = N_DEV // 2

    def body(dx_ref, f_ref, post_ref, uc_ref, wout_ref, duc_ref, dwout_ref, dcb_ref, dpost_ref, acc):
        i = pl.program_id(0)

        @pl.when(i == 0)
        def _():
            acc[...] = jnp.zeros_like(acc)
            dcb_ref[...] = jnp.zeros_like(dcb_ref)
            dpost_ref[...] = jnp.zeros_like(dpost_ref)

        df, dpost = _rms_bwd(f_ref[...], post_ref[...], dx_ref[...])
        dpost_ref[...] += dpost
        dfm = df.astype(_MXU_DTYPE)
        project = lambda b: _mm_tb(dfm, wout_ref[pl.ds(b * fs, fs), :])
        ahead = project(0)
        for b in range(half):
            dg, ahead = ahead, project(b + 1) if b + 1 < half else None
            gate = uc_ref[b].astype(F32)
            val = uc_ref[b + half].astype(F32)
            cdf, dgelu = _gelu_parts(gate)
            ge = gate * cdf
            acc[pl.ds(b * fs, fs), :] += _mm_ta(ge * val, dfm)
            for s, dd in ((b, dg * val * dgelu), (b + half, dg * ge)):
                duc_ref[s] = dd.astype(duc_ref.dtype)
                dcb_ref[pl.ds(s, 1), :] += jnp.sum(dd, axis=0, keepdims=True)

        @pl.when(i == nt - 1)
        def _():
            dwout_ref[...] = acc[...].astype(dwout_ref.dtype)

    tile3 = pl.BlockSpec((N_DEV, tm, fs), lambda i: (0, i, 0))
    return pl.pallas_call(
        body, name=name, grid=(nt,),
        out_shape=(jax.ShapeDtypeStruct((N_DEV, t, fs), _SAVE_DTYPE), jax.ShapeDtypeStruct((half * fs, d_model), _WIRE_DTYPE),
                   jax.ShapeDtypeStruct((N_DEV, fs), F32), jax.ShapeDtypeStruct((1, d_model), F32)),
        in_specs=[_rows(tm, d_model), _rows(tm, d_model), _const((None, 1, d_model), (layer, 0, 0)), tile3,
                  _resident((None, half * fs, d_model), (w_layer, 0, 0))],
        out_specs=(tile3, _resident((half * fs, d_model), (0, 0)), _const((N_DEV, fs)), _const((1, d_model))),
        scratch_shapes=[pltpu.VMEM((half * fs, d_model), F32)],
        compiler_params=_params(),
    )(dx, f, post_g, uc, w_out)


def _ffn_bwd_in(dx, x, pre_g, duc, u, conv_w, w_in, layer, w_layer, name):
    t, d_model = dx.shape
    tm = _row_tile(t)
    nt = t // tm
    fs = duc.shape[2]
    hb = SAVE_HALO
    per_tile = tm // hb

    def body(dx_ref, x_ref, pre_ref, duc_ref, dn_ref, u_ref, cw_ref, win_ref, dxi_ref, du_ref, hf_ref, dcw_ref, dpre_ref):
        i = pl.program_id(0)

        @pl.when(i == 0)
        def _():
            dcw_ref[...] = jnp.zeros_like(dcw_ref)
            dpre_ref[...] = jnp.zeros_like(dpre_ref)

        xv = x_ref[...]
        pre = pre_ref[...]
        hf_ref[...] = _rms_fwd(xv, pre).astype(hf_ref.dtype)
        dhf = jnp.zeros((tm, d_model), F32)
        for s in range(N_DEV):
            d0 = duc_ref[s].astype(F32)
            after = jnp.where(i == nt - 1, 0.0, dn_ref[s].astype(F32)[:CONV_HALO])
            d1 = _shift_up(d0, 1, after)
            d2 = _shift_up(d0, 2, after)
            uv = u_ref[s].astype(F32)
            for k, dk in ((2, d0), (1, d1), (0, d2)):
                dcw_ref[k, pl.ds(s, 1), :] += jnp.sum(dk * uv, axis=0, keepdims=True)
            w0, w1, w2 = _conv_taps(cw_ref, s)
            du = (w2 * d0 + w1 * d1 + w0 * d2).astype(_MXU_DTYPE)
            du_ref[s] = du
            dhf = dhf + _mm(du, win_ref[s])
        dxp, dpre = _rms_bwd(xv, pre, dhf)
        dpre_ref[...] += dpre
        dxi_ref[...] = dx_ref[...] + dxp

    tile3 = pl.BlockSpec((N_DEV, tm, fs), lambda i: (0, i, 0))
    return pl.pallas_call(
        body, name=name, grid=(nt,),
        out_shape=(jax.ShapeDtypeStruct((t, d_model), F32), jax.ShapeDtypeStruct((N_DEV, t, fs), _ACT_DTYPE),
                   jax.ShapeDtypeStruct((t, d_model), _ACT_DTYPE), jax.ShapeDtypeStruct((3, N_DEV, fs), F32),
                   jax.ShapeDtypeStruct((1, d_model), F32)),
        in_specs=[_rows(tm, d_model), _rows(tm, d_model), _const((None, 1, d_model), (layer, 0, 0)), tile3,
                  pl.BlockSpec((N_DEV, hb, fs), lambda i: (0, jnp.minimum((i + 1) * per_tile, t // hb - 1), 0)), tile3,
                  _const((None, 3, N_DEV, fs), (layer, 0, 0, 0)), _resident((None, N_DEV, fs, d_model), (w_layer, 0, 0, 0))],
        out_specs=(_rows(tm, d_model), tile3, _rows(tm, d_model), _const((3, N_DEV, fs)), _const((1, d_model))),
        compiler_params=_params(),
    )(dx, x, pre_g, duc, duc, u, conv_w, w_in)


def _tn_matmul(a, b, a_spec, b_spec, n_out, m, n, name, out_dtype):
    def body(a_ref, b_ref, o_ref):
        o_ref[...] = _mm_ta(a_ref[...], b_ref[...]).astype(o_ref.dtype)

    return pl.pallas_call(
        body, name=name, grid=(n_out,),
        out_shape=jax.ShapeDtypeStruct((n_out, m, n), out_dtype),
        in_specs=[a_spec, b_spec],
        out_specs=pl.BlockSpec((None, m, n), lambda c: (c, 0, 0)),
        compiler_params=_params(),
    )(a, b)


def _kv_fwd(x, kv_g, w_kv, cos, ssin, name):
    t, d_model = x.shape
    tm = _row_tile(t, PROJ_TILE)
    kvd = w_kv.shape[1] // 2
    pairs = kvd // LANES

    def body(x_ref, g_ref, w_ref, cos_ref, sin_ref, k_ref, v_ref):
        kv = _mm(_rms_fwd(x_ref[...], g_ref[...]), w_ref[...])
        low = _low_half((tm, LANES))
        for j in range(pairs):
            kb = _rope_fwd(kv[:, j * LANES:(j + 1) * LANES], cos_ref[...], sin_ref[...])
            vb = kv[:, kvd + j * LANES:kvd + (j + 1) * LANES]
            for blk, ref in ((kb, k_ref), (vb, v_ref)):
                sw = pltpu.roll(blk, 64, 1)
                ref[2 * j] = jnp.where(low, blk, sw).astype(ref.dtype)
                ref[2 * j + 1] = jnp.where(low, sw, blk).astype(ref.dtype)

    heads = jax.ShapeDtypeStruct((N_KV_HEADS, t, LANES), _ACT_DTYPE)
    hspec = pl.BlockSpec((N_KV_HEADS, tm, LANES), lambda i: (0, i, 0))
    return pl.pallas_call(
        body, name=name, grid=(t // tm,), out_shape=(heads, heads),
        in_specs=[_rows(tm, d_model), _const((1, d_model)), _const(w_kv.shape), _rows(tm, LANES), _rows(tm, LANES)],
        out_specs=(hspec, hspec), compiler_params=_params(),
    )(x, kv_g, w_kv, cos, ssin)


def _kv_bwd(dx, x, kv_g, w_kv, cos, ssin, dks, dvs, name):
    t, d_model = x.shape
    tm = _row_tile(t, PROJ_TILE)
    nt = t // tm
    kvd = w_kv.shape[1] // 2
    pairs = kvd // LANES
    n_users = len(dks)

    def body(dx_ref, x_ref, g_ref, w_ref, cos_ref, sin_ref, *refs):
        dk_refs, dv_refs = refs[:n_users], refs[n_users:2 * n_users]
        dxi_ref, dw_ref, dg_ref, acc = refs[2 * n_users:]
        dk_ref = functools.reduce(lambda a, b: a + b, [r[...] for r in dk_refs])
        dv_ref = functools.reduce(lambda a, b: a + b, [r[...] for r in dv_refs])
        i = pl.program_id(0)

        @pl.when(i == 0)
        def _():
            dg_ref[...] = jnp.zeros_like(dg_ref)
            acc[...] = jnp.zeros_like(acc)

        xv = x_ref[...]
        g = g_ref[...]
        low = _low_half((tm, LANES))
        dks, dvs = [], []
        for j in range(pairs):
            dkb = jnp.where(low, dk_ref[2 * j], dk_ref[2 * j + 1])
            dks.append(_rope_bwd(dkb, cos_ref[...], sin_ref[...]))
            dvs.append(jnp.where(low, dv_ref[2 * j], dv_ref[2 * j + 1]))
        dkv = jnp.concatenate(dks + dvs, axis=1).astype(_MXU_DTYPE)
        acc[...] += _mm_ta(_rms_fwd(xv, g), dkv)
        dxp, dg = _rms_bwd(xv, g, _mm_tb(dkv, w_ref[...]))
        dg_ref[...] += dg
        dxi_ref[...] = dx_ref[...] + dxp

        @pl.when(i == nt - 1)
        def _():
            dw_ref[...] = acc[...].astype(dw_ref.dtype)

    hspec = pl.BlockSpec((N_KV_HEADS, tm, LANES), lambda i: (0, i, 0))
    return pl.pallas_call(
        body, name=name, grid=(nt,),
        out_shape=(jax.ShapeDtypeStruct((t, d_model), F32), jax.ShapeDtypeStruct(w_kv.shape, _WIRE_DTYPE),
                   jax.ShapeDtypeStruct((1, d_model), F32)),
        in_specs=[_rows(tm, d_model), _rows(tm, d_model), _const((1, d_model)), _const(w_kv.shape),
                  _rows(tm, LANES), _rows(tm, LANES)] + [hspec] * (2 * n_users),
        out_specs=(_rows(tm, d_model), _resident(w_kv.shape, (0, 0)), _const((1, d_model))),
        scratch_shapes=[pltpu.VMEM(w_kv.shape, F32)],
        compiler_params=_params(),
    )(dx, x, kv_g, w_kv, cos, ssin, *dks, *dvs)


def _q_fwd(x, pre_g, w_q, cos, ssin, layer, j, name):
    t, d_model = x.shape
    tm = _row_tile(t, PROJ_TILE)

    def body(x_ref, g_ref, w_ref, cos_ref, sin_ref, q_ref):
        q = _mm(_rms_fwd(x_ref[...], g_ref[...]), w_ref[...])
        for p in range(d_model // LANES):
            cols = slice(p * LANES, (p + 1) * LANES)
            q_ref[:, cols] = (_rope_fwd(q[:, cols], cos_ref[...], sin_ref[...]) * ATTN_SCALE).astype(q_ref.dtype)

    return pl.pallas_call(
        body, name=name, grid=(t // tm,), out_shape=jax.ShapeDtypeStruct((t, d_model), _ACT_DTYPE),
        in_specs=[_rows(tm, d_model), _const((None, 1, d_model), (layer, 0, 0)), _const((None, d_model, d_model), (j, 0, 0)),
                  _rows(tm, LANES), _rows(tm, LANES)],
        out_specs=_rows(tm, d_model), compiler_params=_params(),
    )(x, pre_g, w_q, cos, ssin)


def _q_bwd(dx, dqs, x, pre_g, w_q, cos, ssin, layer, j, name):
    t, d_model = x.shape
    tm = _row_tile(t, PROJ_TILE)
    nt = t // tm

    def body(dx_ref, dq_ref, x_ref, g_ref, w_ref, cos_ref, sin_ref, dxi_ref, dw_ref, dg_ref, acc):
        i = pl.program_id(0)

        @pl.when(i == 0)
        def _():
            dg_ref[...] = jnp.zeros_like(dg_ref)
            acc[...] = jnp.zeros_like(acc)

        xv = x_ref[...]
        g = g_ref[...]
        parts = []
        for p in range(d_model // LANES):
            cols = slice(p * LANES, (p + 1) * LANES)
            parts.append(_rope_bwd(dq_ref[:, cols] * ATTN_SCALE, cos_ref[...], sin_ref[...]))
        dq = jnp.concatenate(parts, axis=1).astype(_MXU_DTYPE)
        acc[...] += _mm_ta(_rms_fwd(xv, g), dq)
        dxp, dg = _rms_bwd(xv, g, _mm_tb(dq, w_ref[...]))
        dg_ref[...] += dg
        dxi_ref[...] = dx_ref[...] + dxp

        @pl.when(i == nt - 1)
        def _():
            dw_ref[...] = acc[...].astype(dw_ref.dtype)

    return pl.pallas_call(
        body, name=name, grid=(nt,),
        out_shape=(jax.ShapeDtypeStruct((t, d_model), F32), jax.ShapeDtypeStruct((d_model, d_model), _WIRE_DTYPE),
                   jax.ShapeDtypeStruct((1, d_model), F32)),
        in_specs=[_rows(tm, d_model), _rows(tm, d_model), _rows(tm, d_model), _const((None, 1, d_model), (layer, 0, 0)),
                  _const((None, d_model, d_model), (j, 0, 0)), _rows(tm, LANES), _rows(tm, LANES)],
        out_specs=(_rows(tm, d_model), _resident((d_model, d_model), (0, 0)), _const((1, d_model))),
        scratch_shapes=[pltpu.VMEM((d_model, d_model), F32)],
        compiler_params=_params(),
    )(dx, dqs, x, pre_g, w_q, cos, ssin)


def _stack_heads(pairs):
    low = _low_half(pairs[0].shape)
    zero = jnp.zeros_like(pairs[0])
    return jnp.concatenate([h for blk in pairs for h in (jnp.where(low, blk, zero), jnp.where(low, zero, blk))], axis=0)


def _unstack_heads(stacked, i):
    a, b = stacked[2 * i * BLOCK:(2 * i + 1) * BLOCK], stacked[(2 * i + 1) * BLOCK:(2 * i + 2) * BLOCK]
    return jnp.where(_low_half(a.shape), a, b)


def _attn_scores(q_pairs, k2, n, sinks):
    qst = _stack_heads(q_pairs)
    s = _mm_tb(qst, k2)
    row = lax.broadcasted_iota(jnp.int32, s.shape, 0)
    col = lax.broadcasted_iota(jnp.int32, s.shape, 1)
    rel = BLOCK + (row & (BLOCK - 1)) - col
    valid = (rel >= 0) & (rel < WINDOW) & (n * BLOCK + col - BLOCK >= 0)
    s = jnp.where(valid, s, NEG_INF)
    rows1 = lax.broadcasted_iota(jnp.int32, (s.shape[0], 1), 0)
    sink = jnp.full((s.shape[0], 1), sinks[-1], F32)
    for i in reversed(range(len(sinks) - 1)):
        sink = jnp.where(rows1 < (i + 1) * BLOCK, sinks[i], sink)
    return qst, s, sink


def _attn_fwd(qs, kdup, vdup, sinks, j, name):
    t, d_model = qs.shape
    nb = t // BLOCK
    n_pairs = d_model // LANES
    per_group = n_pairs // N_KV_HEADS

    def body(sink_ref, q_ref, kp_ref, ko_ref, vp_ref, vo_ref, o_ref, lse_ref):
        n = pl.program_id(0)
        lane = _lane_iota((BLOCK, LANES))
        lse = jnp.zeros((BLOCK, LANES), F32)
        for hk in range(N_KV_HEADS):
            pairs = range(hk * per_group, (hk + 1) * per_group)
            heads = range(2 * pairs[0], 2 * pairs[-1] + 2)
            k2 = jnp.concatenate([kp_ref[hk], ko_ref[hk]], axis=0)
            v2 = jnp.concatenate([vp_ref[hk], vo_ref[hk]], axis=0)
            _, s, sink = _attn_scores([q_ref[:, p * LANES:(p + 1) * LANES] for p in pairs], k2, n,
                                      [sink_ref[j, h] for h in heads])
            m = jnp.maximum(jnp.max(s, axis=-1, keepdims=True), sink)
            pe = jnp.exp(s - m)
            denom = jnp.sum(pe, axis=-1, keepdims=True) + jnp.exp(sink - m)
            o2 = _mm(pe, v2) / denom
            l2 = m + jnp.log(denom)
            for i, p in enumerate(pairs):
                o_ref[:, p * LANES:(p + 1) * LANES] = _unstack_heads(o2, i).astype(o_ref.dtype)
            for i, h in enumerate(heads):
                lse = jnp.where(lane == h, l2[i * BLOCK:(i + 1) * BLOCK], lse)
        lse_ref[...] = lse

    prev = pl.BlockSpec((N_KV_HEADS, BLOCK, LANES), lambda n: (0, jnp.maximum(n - 1, 0), 0))
    own = pl.BlockSpec((N_KV_HEADS, BLOCK, LANES), lambda n: (0, n, 0))
    return pl.pallas_call(
        body, name=name, grid=(nb,),
        out_shape=(jax.ShapeDtypeStruct((t, d_model), _ACT_DTYPE), jax.ShapeDtypeStruct((t, LANES), F32)),
        in_specs=[pl.BlockSpec(memory_space=pltpu.SMEM), _rows(BLOCK, d_model), prev, own, prev, own],
        out_specs=(_rows(BLOCK, d_model), _rows(BLOCK, LANES)), compiler_params=_params(),
    )(sinks, qs, kdup, kdup, vdup, vdup)


def _attn_bwd(qs, kdup, vdup, sinks, lse, do, j, name):
    t, d_model = qs.shape
    nb = t // BLOCK
    n_pairs = d_model // LANES
    per_group = n_pairs // N_KV_HEADS
    rev = lambda n: nb - 1 - n

    def body(sink_ref, q_ref, kp_ref, ko_ref, vp_ref, vo_ref, lse_ref, do_ref, dq_ref, dk_ref, dv_ref, ds_ref, ck, cv):
        i = pl.program_id(0)
        n = nb - 1 - i

        @pl.when(i == 0)
        def _():
            ck[...] = jnp.zeros_like(ck)
            cv[...] = jnp.zeros_like(cv)
            ds_ref[...] = jnp.zeros_like(ds_ref)

        lane = _lane_iota((BLOCK, LANES))
        lane1 = _lane_iota((1, LANES))
        lsev = lse_ref[...]
        dsink = jnp.zeros((1, LANES), F32)
        for hk in range(N_KV_HEADS):
            pairs = range(hk * per_group, (hk + 1) * per_group)
            heads = range(2 * pairs[0], 2 * pairs[-1] + 2)
            k2 = jnp.concatenate([kp_ref[hk], ko_ref[hk]], axis=0)
            v2 = jnp.concatenate([vp_ref[hk], vo_ref[hk]], axis=0)
            qst, s, sink = _attn_scores([q_ref[:, p * LANES:(p + 1) * LANES] for p in pairs], k2, n,
                                        [sink_ref[j, h] for h in heads])
            l2 = jnp.concatenate([jnp.sum(jnp.where(lane == h, lsev, 0.0), axis=-1, keepdims=True) for h in heads], axis=0)
            pn = jnp.exp(s - l2)
            dost = _stack_heads([do_ref[:, p * LANES:(p + 1) * LANES] for p in pairs])
            dp = _mm_tb(dost, v2)
            dr = jnp.sum(pn * dp, axis=-1, keepdims=True)
            dsm = (pn * (dp - dr)).astype(_MXU_DTYPE)
            dsk = -jnp.exp(sink - l2) * dr
            for i, h in enumerate(heads):
                dsink = dsink + jnp.where(lane1 == h, jnp.sum(dsk[i * BLOCK:(i + 1) * BLOCK]), 0.0)
            dq2 = _mm(dsm, k2)
            for i, p in enumerate(pairs):
                dq_ref[:, p * LANES:(p + 1) * LANES] = _unstack_heads(dq2, i)
            for acc, carry, ref in ((_mm_ta(dsm, qst), ck, dk_ref), (_mm_ta(pn, dost), cv, dv_ref)):
                folded = acc + pltpu.roll(acc, 64, 1)
                ref[hk] = folded[BLOCK:] + carry[hk]
                carry[hk] = folded[:BLOCK]
        ds_ref[...] += dsink

    prev = pl.BlockSpec((N_KV_HEADS, BLOCK, LANES), lambda n: (0, jnp.maximum(rev(n) - 1, 0), 0))
    own = pl.BlockSpec((N_KV_HEADS, BLOCK, LANES), lambda n: (0, rev(n), 0))
    rows = lambda cols: pl.BlockSpec((BLOCK, cols), lambda n: (rev(n), 0))
    heads = jax.ShapeDtypeStruct((N_KV_HEADS, t, LANES), F32)
    return pl.pallas_call(
        body, name=name, grid=(nb,),
        out_shape=(jax.ShapeDtypeStruct((t, d_model), F32), heads, heads, jax.ShapeDtypeStruct((1, LANES), F32)),
        in_specs=[pl.BlockSpec(memory_space=pltpu.SMEM), rows(d_model), prev, own, prev, own, rows(LANES), rows(d_model)],
        out_specs=(rows(d_model), own, own, _const((1, LANES))),
        scratch_shapes=[pltpu.VMEM((N_KV_HEADS, BLOCK, LANES), F32), pltpu.VMEM((N_KV_HEADS, BLOCK, LANES), F32)],
        compiler_params=_params(),
    )(sinks, qs, kdup, kdup, vdup, vdup, lse, do)


def _oproj_fwd(x, o, w_o, post_g, layer, j, name):
    t, d_model = x.shape
    tm = _row_tile(t, PROJ_TILE)

    def body(x_ref, o_ref, w_ref, g_ref, xo_ref, mo_ref):
        mo = _mm(o_ref[...], w_ref[...])
        mo_ref[...] = mo
        xo_ref[...] = x_ref[...] + _rms_fwd(mo, g_ref[...])

    full = jax.ShapeDtypeStruct((t, d_model), F32)
    return pl.pallas_call(
        body, name=name, grid=(t // tm,), out_shape=(full, full),
        in_specs=[_rows(tm, d_model), _rows(tm, d_model), _const((None, d_model, d_model), (j, 0, 0)),
                  _const((None, 1, d_model), (layer, 0, 0))],
        out_specs=(_rows(tm, d_model), _rows(tm, d_model)), compiler_params=_params(),
    )(x, o, w_o, post_g)


def _oproj_bwd(dx, mo, o, w_o, post_g, layer, j, name):
    t, d_model = dx.shape
    tm = _row_tile(t, PROJ_TILE)
    nt = t // tm

    def body(dx_ref, mo_ref, o_ref, w_ref, g_ref, do_ref, dw_ref, dg_ref, acc):
        i = pl.program_id(0)

        @pl.when(i == 0)
        def _():
            dg_ref[...] = jnp.zeros_like(dg_ref)
            acc[...] = jnp.zeros_like(acc)

        dmo, dg = _rms_bwd(mo_ref[...], g_ref[...], dx_ref[...])
        dg_ref[...] += dg
        dmo = dmo.astype(_MXU_DTYPE)
        acc[...] += _mm_ta(o_ref[...], dmo)
        do_ref[...] = _mm_tb(dmo, w_ref[...]).astype(do_ref.dtype)

        @pl.when(i == nt - 1)
        def _():
            dw_ref[...] = acc[...].astype(dw_ref.dtype)

    return pl.pallas_call(
        body, name=name, grid=(nt,),
        out_shape=(jax.ShapeDtypeStruct((t, d_model), _ACT_DTYPE), jax.ShapeDtypeStruct((d_model, d_model), _WIRE_DTYPE),
                   jax.ShapeDtypeStruct((1, d_model), F32)),
        in_specs=[_rows(tm, d_model), _rows(tm, d_model), _rows(tm, d_model), _const((None, d_model, d_model), (j, 0, 0)),
                  _const((None, 1, d_model), (layer, 0, 0))],
        out_specs=(_rows(tm, d_model), _resident((d_model, d_model), (0, 0)), _const((1, d_model))),
        scratch_shapes=[pltpu.VMEM((d_model, d_model), F32)],
        compiler_params=_params(),
    )(dx, mo, o, w_o, post_g)


def _loss_grad(y, target, name):
    t, d_model = y.shape
    tm = _row_tile(t, PROJ_TILE)

    def body(y_ref, t_ref, dy_ref, loss_ref):
        i = pl.program_id(0)

        @pl.when(i == 0)
        def _():
            loss_ref[...] = jnp.zeros_like(loss_ref)

        err = y_ref[...] - t_ref[...]
        dy_ref[...] = err / d_model
        loss_ref[...] += 0.5 * jnp.sum(jnp.mean(err * err, axis=-1, keepdims=True), axis=0, keepdims=True)

    return pl.pallas_call(
        body, name=name, grid=(t // tm,),
        out_shape=(jax.ShapeDtypeStruct((t, d_model), F32), jax.ShapeDtypeStruct((1, 1), F32)),
        in_specs=[_rows(tm, d_model), _rows(tm, d_model)], out_specs=(_rows(tm, d_model), _const((1, 1))),
        compiler_params=_params(),
    )(y, target)


def _mesh_position():
    return lax.axis_index("x"), lax.axis_index("y"), lax.axis_index("c")


def _block_of(px, py, pc):
    return 4 * px + 2 * py + pc


def _at_block(ref, axis, block):
    return ref.at[(slice(None),) * axis + (block,)]


GATHER, SCATTER, GATHER_CHIPS, GATHER_SIBLING = "gather", "scatter", "gather_chips", "gather_sibling"
COPIES = {GATHER: N_DEV - 1, SCATTER: N_DEV - 1, GATHER_CHIPS: 4, GATHER_SIBLING: 3}


def _land_shape(kind, s, axis):
    if kind == SCATTER:
        return (N_DEV,) + s.shape[:axis] + s.shape[axis + 1:]
    return s.shape[:axis] + (N_DEV,) + s.shape[axis:]


def _plan(kind, srcs, lands, axes):
    x, y, c = _mesh_position()
    my_block = _block_of(x, y, c)
    flips = {GATHER_CHIPS: (1, 4, 2, 6), GATHER_SIBLING: (4, 2, 6)}.get(kind, range(1, N_DEV))
    others = [(1 - x if k & 4 else x, 1 - y if k & 2 else y, 1 - c if k & 1 else c) for k in flips]
    remote = []
    for src, land, axis in zip(srcs, lands, axes):
        if kind == SCATTER:
            mine = land.at[my_block]
            remote += [(_at_block(src, axis, _block_of(*peer)), mine, peer, land.at[_block_of(*peer)]) for peer in others]
        elif kind == GATHER_SIBLING:
            for px, py, _ in others:
                mine, theirs = _at_block(land, axis, _block_of(px, py, c)), _at_block(land, axis, _block_of(px, py, 1 - c))
                remote.append((mine, mine, (x, y, 1 - c), theirs))
        else:
            mine = _at_block(land, axis, my_block)
            remote += [(src, mine, peer, _at_block(land, axis, _block_of(*peer))) for peer in others]
    return remote


def _remote(src, dst, send_sems, recv_sems, k, peer):
    return pltpu.make_async_remote_copy(src_ref=src, dst_ref=dst, send_sem=send_sems.at[k], recv_sem=recv_sems.at[k],
                                        device_id=peer, device_id_type=MESH)


_HBM = pl.BlockSpec(memory_space=pltpu.HBM)
_SEM = pl.BlockSpec(memory_space=pltpu.SEMAPHORE)
_SPLIT = dict(has_side_effects=pltpu.SideEffectType.DATAFLOW_SIDE_EFFECTING)


def _landing_zone(kind, s, axis, me):
    land = lax.empty(_land_shape(kind, s, axis), s.dtype)
    if kind == SCATTER:
        return lax.dynamic_update_slice_in_dim(land, lax.dynamic_slice_in_dim(s, me, 1, axis).reshape((1,) + land.shape[1:]), me, 0)
    return lax.dynamic_update_slice_in_dim(land, jnp.expand_dims(s, axis), me, axis)


def _exchange_start(kind, arrays, axes, after, name):
    n = len(arrays)
    if kind == GATHER_SIBLING:
        passed = list(arrays)
    else:
        me = _block_of(*_mesh_position())
        passed = list(arrays) + [_landing_zone(kind, s, a, me) for s, a in zip(arrays, axes)]
    n_sems = n * COPIES[kind]

    def body(*refs):
        land_refs = refs[len(passed) - n:len(passed)]
        send_sems, recv_sems = refs[len(passed) + 1], refs[len(passed) + 2]
        token = refs[-1]
        for k, (src, dst, peer, _) in enumerate(_plan(kind, refs[:n], land_refs, axes)):
            _remote(src, dst, send_sems, recv_sems, k, peer).start()
        token[...] = jnp.zeros_like(token)

    out = pl.pallas_call(
        body, name=name,
        out_shape=(pltpu.SemaphoreType.DMA((n_sems,)), pltpu.SemaphoreType.DMA((n_sems,)),
                   *[pltpu.HBM(a.shape, a.dtype) for a in passed], jax.ShapeDtypeStruct((8, LANES), F32)),
        in_specs=[_HBM] * len(passed) + [pl.BlockSpec(memory_space=pl.ANY)],
        out_specs=(_SEM, _SEM, *[_HBM] * len(passed), pl.BlockSpec(memory_space=pltpu.VMEM)),
        input_output_aliases={i: 2 + i for i in range(len(passed))},
        compiler_params=pltpu.CompilerParams(**_SPLIT),
    )(*[pltpu.with_memory_space_constraint(a, pltpu.HBM) for a in passed], after)
    return (kind, axes, n, out[:-1]), out[-1]


def _exchange_wait(handle, after, name):
    kind, axes, n, (send_sems, recv_sems, *thru) = handle

    def body(*refs):
        land_refs = refs[len(thru) - n:len(thru)]
        send_sems, recv_sems = refs[len(thru)], refs[len(thru) + 1]
        for k, (src, _, peer, arrives) in enumerate(_plan(kind, refs[:n], land_refs, axes)):
            cp = _remote(src, arrives, send_sems, recv_sems, k, peer)
            cp.wait_send()
            cp.wait_recv()

    out = pl.pallas_call(
        body, name=name,
        out_shape=tuple(pltpu.HBM(a.shape, a.dtype) for a in thru),
        in_specs=[_HBM] * len(thru) + [_SEM, _SEM, pl.BlockSpec(memory_space=pl.ANY)], out_specs=(_HBM,) * len(thru),
        input_output_aliases={i: i for i in range(len(thru))},
        compiler_params=pltpu.CompilerParams(**_SPLIT),
    )(*thru, send_sems, recv_sems, after)
    return out[len(thru) - n:]


def _adamw_math(w, g, m, v):
    m = ADAM_B1 * m + (1.0 - ADAM_B1) * g
    v = ADAM_B2 * v + (1.0 - ADAM_B2) * jnp.square(g)
    m_hat = m / (1.0 - ADAM_B1 ** ADAM_STEP)
    v_hat = v / (1.0 - ADAM_B2 ** ADAM_STEP)
    delta = -ADAM_LR * (m_hat / (jnp.sqrt(v_hat) + ADAM_EPS) + ADAM_WD * w)
    return delta, m, v


def _update_tile(rows):
    if rows <= 512:
        return rows
    for tr in (512, 384, 352, 256, 176, 128, 64, 32, 16):
        if rows % tr == 0:
            return tr
    raise ValueError(f"{rows} rows do not tile")


def _adamw(parts, w, m, v, slab, so_far, name):
    rows, c = w.shape
    r = parts.shape[1]
    tr = _update_tile(r)
    first = slab * (r // tr)
    if so_far is None:
        so_far = tuple(lax.empty((rows, c), F32) for _ in range(4))

    def body(p_ref, w_ref, m_ref, v_ref, *refs):
        g_ref, d_ref, mo_ref, vo_ref = refs[4:]
        g = p_ref[0].astype(F32)
        for s in range(1, N_DEV):
            g = g + p_ref[s].astype(F32)
        g_ref[...] = g
        d_ref[...], mo_ref[...], vo_ref[...] = _adamw_math(w_ref[...], g, m_ref[...], v_ref[...])

    out = jax.ShapeDtypeStruct((rows, c), F32)
    tile = pl.BlockSpec((tr, c), lambda i: (first + i, 0))
    return pl.pallas_call(
        body, name=name, grid=(r // tr,), out_shape=(out,) * 4,
        in_specs=[pl.BlockSpec((N_DEV, tr, c), lambda i: (0, i, 0))] + [tile] * 3 + [pl.BlockSpec(memory_space=pl.ANY)] * 4,
        out_specs=(tile,) * 4, input_output_aliases={4 + k: k for k in range(4)}, compiler_params=_params(),
    )(parts, w, m, v, *so_far)


def _adamw_small(parts, picks, weights, name):
    n = len(parts)

    def body(*refs):
        p_refs, wmv, outs = refs[:n], refs[n:4 * n], refs[4 * n:]
        me = _block_of(*_mesh_position())
        for i in range(n):
            g = picks[i](p_refs[i], 0, me)
            for s in range(1, N_DEV):
                g = g + picks[i](p_refs[i], s, me)
            w_ref, m_ref, v_ref = wmv[3 * i:3 * i + 3]
            g_ref, d_ref, mo_ref, vo_ref = outs[4 * i:4 * i + 4]
            g_ref[...] = g
            d_ref[...], mo_ref[...], vo_ref[...] = _adamw_math(w_ref[...], g, m_ref[...], v_ref[...])

    flat = [a for wmv in weights for a in wmv]
    out = pl.pallas_call(
        body, name=name,
        out_shape=tuple(jax.ShapeDtypeStruct(w.shape, F32) for w, _, _ in weights for _ in range(4)),
        compiler_params=pltpu.CompilerParams(vmem_limit_bytes=VMEM_LIMIT),
    )(*parts, *flat)
    return [tuple(out[4 * i:4 * i + 4]) for i in range(n)]


def kernel(x, positions, mix_pre_g, mix_post_g, pool_w, pool_scale, kv_norm_g, w_kv, w_q, w_o, sinks, ffn_pre_g, ffn_post_g, ffn_w_in, ffn_conv_w, ffn_conv_b, ffn_w_out, loss_target, m_mix_pre_g, m_mix_post_g, m_pool_w, m_pool_scale, m_kv_norm_g, m_w_kv, m_w_q, m_w_o, m_sinks, m_ffn_pre_g, m_ffn_post_g, m_ffn_w_in, m_ffn_conv_w, m_ffn_conv_b, m_ffn_w_out, v_mix_pre_g, v_mix_post_g, v_pool_w, v_pool_scale, v_kv_norm_g, v_w_kv, v_w_q, v_w_o, v_sinks, v_ffn_pre_g, v_ffn_post_g, v_ffn_w_in, v_ffn_conv_w, v_ffn_conv_b, v_ffn_w_out):
    depth, d_model = mix_pre_g.shape
    n_a = pool_w.shape[0]
    n_b = w_q.shape[0]
    t = x.shape[1]
    fs = ffn_w_in.shape[2]
    half = N_DEV // 2
    n_heads = d_model // HEAD_DIM
    x0 = x.reshape(t, d_model)
    target = loss_target.reshape(t, d_model)

    inv_freq = 1.0 / (ROPE_THETA ** (jnp.arange(0, HEAD_DIM, 2, dtype=F32) / HEAD_DIM))
    ang = positions.reshape(t).astype(F32)[:, None] * inv_freq
    cos, sin = jnp.cos(ang), jnp.sin(ang)
    cos = jnp.tile(cos, (1, 2 * LANES // HEAD_DIM))
    ssin = jnp.tile(jnp.concatenate([-sin, sin], axis=1), (1, LANES // HEAD_DIM))

    wire = lambda a: a.astype(_WIRE_DTYPE)
    by_hidden = lambda a: a.transpose(0, 2, 1)
    w_in_b, w_out_b = wire(by_hidden(ffn_w_in)), wire(ffn_w_out)
    first_flight, after = _exchange_start(GATHER, [wire(pool_w), pool_scale, ffn_conv_w], [2, 0, 0], pool_scale, "gather_first")
    pool_w_g, pool_scale_g, conv_w_g = _exchange_wait(first_flight, after, "gather_first_wait")
    groups = []
    for l in range(depth):
        if l == n_a:
            groups.append(("attn", [wire(w_kv), wire(w_q), wire(w_o)], [0, 1, 1], l))
        if l == 0:
            groups += [("0_in", [w_in_b[:1]], [1], 0), ("0_out", [w_out_b[:1]], [1], 0)]
        else:
            groups.append((l, [w_in_b[l:l + 1], w_out_b[l:l + 1]], [1, 1], l))
    over_ici, to_sibling, tokens, after = {}, {}, [], pool_w_g
    for key, shards, axes, _ in groups:
        over_ici[key], after = _exchange_start(GATHER_CHIPS, shards, axes, after, f"gather_chips_{key}")
        tokens.append(after)
    started = functools.reduce(lambda a, b: a + b, [tk[0, 0] for tk in tokens])

    def pass_on(layer, after):
        sent = jnp.zeros((), F32)
        for key, _, axes, first in groups:
            if first == layer:
                lands = _exchange_wait(over_ici[key], after, f"gather_chips_wait_{key}")
                to_sibling[key], after = _exchange_start(GATHER_SIBLING, lands, axes, after, f"gather_sibling_{key}")
                sent = sent + after[0, 0]
        return sent

    w_in_l, w_out_l = {}, {}
    pool_scale_f = pool_scale_g.transpose(1, 0, 2).reshape(n_a, 1, d_model)
    conv_w_f = conv_w_g.transpose(1, 2, 0, 3)
    pool_w_f = pool_w_g.reshape(n_a, len(POOL_WINDOWS), d_model // len(POOL_WINDOWS), -1)
    conv_b_f = ffn_conv_b.reshape(depth, N_DEV, fs)
    g3 = lambda a: a.reshape(a.shape[0], 1, a.shape[1])
    mix_pre, mix_post, ffn_pre, ffn_post = g3(mix_pre_g) + started, g3(mix_post_g), g3(ffn_pre_g), g3(ffn_post_g)
    kv_g = kv_norm_g.reshape(1, d_model)
    w_kv_f = w_q_f = w_o_f = None

    saved = []
    xc = x0
    kdup = vdup = x_kv = None
    for l in range(depth):
        x_in = xc
        if l < n_a:
            x_mid, dsave, yu = _pool_fwd(x_in, mix_pre, mix_post, pool_w_f, pool_scale_f, l, f"pool_fwd_{l}")
            mixer = (dsave, yu)
        else:
            j = l - n_a
            if j == 0:
                x_kv = x_in
                w_kv_g, w_q_g, w_o_g = _exchange_wait(to_sibling["attn"], x_in, "gather_sibling_wait_attn")
                w_kv_f = w_kv_g.reshape(d_model, -1)
                w_q_f = w_q_g.reshape(n_b, d_model, d_model)
                w_o_f = w_o_g.reshape(n_b, d_model, d_model)
                kdup, vdup = _kv_fwd(x_kv, kv_g, w_kv_f, cos, ssin, "kv_fwd")
            qs = _q_fwd(x_in, mix_pre, w_q_f, cos, ssin, l, j, f"q_fwd_{l}")
            o, lse = _attn_fwd(qs, kdup, vdup, sinks, j, f"attn_fwd_{l}")
            x_mid, mo = _oproj_fwd(x_in, o, w_o_f, mix_post, l, j, f"oproj_fwd_{l}")
            mixer = (qs, o, lse, mo)
        if l == 0:
            pass_on(0, x_mid)
            (w_in_l[l],), (w_out_g,) = (_exchange_wait(to_sibling[k], x_mid, f"gather_sibling_wait_{k}") for k in ("0_in", "0_out"))
        else:
            w_in_l[l], w_out_g = _exchange_wait(to_sibling[l], x_mid, f"gather_sibling_wait_{l}")
        w_out_l[l] = w_out_g.reshape(1, half * fs, d_model)
        xc, u, uc, f = _ffn_fwd(x_mid, ffn_pre, ffn_post, w_in_l[l], conv_w_f, conv_b_f, w_out_l[l], l, 0, f"ffn_fwd_{l}")
        saved.append((x_in, x_mid, u, uc, f, mixer))
        if l + 1 < depth:
            mix_pre = mix_pre + pass_on(l + 1, xc)

    dx, loss_part = _loss_grad(xc, target, "loss")

    gconv_w, gconv_b = [None] * depth, [None] * depth
    gmix_pre, gmix_post, gffn_pre, gffn_post = [None] * depth, [None] * depth, [None] * depth, [None] * depth
    gpool_scale, gsinks = [None] * n_a, [None] * n_b
    dks, dvs = [], []
    gkv_g = None
    whole = lambda cols: pl.BlockSpec((t, cols), lambda c: (0, 0), pipeline_mode=pl.Buffered(1))
    per_out = lambda cols: pl.BlockSpec((None, t, cols), lambda c: (c, 0, 0))
    by_dev = lambda g: g.reshape(N_DEV, -1, g.shape[-1])
    def small_grads():
        cat = lambda rows: jnp.concatenate(rows, axis=0)
        row = lambda a: a.reshape(1, -1)
        everything = lambda ref, s, me: ref[s]
        lanes = pool_scale.shape[1]
        return [("mix_pre_g", cat(gmix_pre), everything, (mix_pre_g, m_mix_pre_g, v_mix_pre_g)),
                ("mix_post_g", cat(gmix_post), everything, (mix_post_g, m_mix_post_g, v_mix_post_g)),
                ("kv_norm_g", gkv_g, everything, (row(kv_norm_g), row(m_kv_norm_g), row(v_kv_norm_g))),
                ("sinks", cat(gsinks), lambda ref, s, me: ref[s, :, pl.ds(0, n_heads)], (sinks, m_sinks, v_sinks)),
                ("ffn_pre_g", cat(gffn_pre), everything, (ffn_pre_g, m_ffn_pre_g, v_ffn_pre_g)),
                ("ffn_post_g", cat(gffn_post), everything, (ffn_post_g, m_ffn_post_g, v_ffn_post_g)),
                ("ffn_conv_b", jnp.stack(gconv_b).reshape(depth, N_DEV * fs), everything, (ffn_conv_b, m_ffn_conv_b, v_ffn_conv_b)),
                ("pool_scale", cat(gpool_scale), lambda ref, s, me: ref[s, :, pl.ds(pl.multiple_of(me * lanes, lanes), lanes)],
                 (pool_scale, m_pool_scale, v_pool_scale)),
                ("ffn_conv_w", jnp.stack(gconv_w).transpose(0, 2, 1, 3), lambda ref, s, me: ref[s, :, me],
                 (ffn_conv_w, m_ffn_conv_w, v_ffn_conv_w))]

    flying = []

    def launch(going, after, name):
        handle, token = _exchange_start(SCATTER, [g for _, _, g, _ in going], [a for _, _, _, a in going], after, name)
        flying.append(([(nm, slab) for nm, slab, _, _ in going], handle))
        return token

    post = mix_post
    ffn_post_b = ffn_post
    token = None
    for l in reversed(range(depth)):
        x_in, x_mid, u, uc, f, mixer = saved[l]
        duc, gout, gconv_b[l], gffn_post[l] = _ffn_bwd_out(dx, f, ffn_post_b, uc, conv_b_f, w_out_l[l], l, 0, f"ffn_bwd_out_{l}")
        going = [("ffn_w_out", l, by_dev(gout), 0)]
        pre = ffn_pre
        if l == 0:
            token = launch(going, gout, "scatter_start_0_out")
            going = []
            pre = pre + token[0, 0]
        dx, du, hf, gconv_w[l], gffn_pre[l] = _ffn_bwd_in(dx, x_mid, pre, duc, u, conv_w_f, w_in_l[l], l, 0, f"ffn_bwd_in_{l}")
        gin = _tn_matmul(du, hf, per_out(fs), whole(d_model), N_DEV, fs, d_model, f"grad_w_in_{l}", _WIRE_DTYPE)
        going.append(("ffn_w_in", l, gin, 0))
        if l == 0:
            token = launch(going, dx, "scatter_start_0_in")
            going = []
            post = post + token[0, 0]
        if l < n_a:
            dsave, yu = mixer
            dx, dyu, gpool_scale[l], gmix_pre[l], gmix_post[l] = _pool_bwd(
                dx, x_in, mix_pre, post, dsave, yu, pool_w_f, pool_scale_f, l, f"pool_bwd_{l}")
            gc = d_model // len(POOL_WINDOWS)
            by_group = pl.BlockSpec((t, gc), lambda c: (0, c))
            gpool = _tn_matmul(dsave, dyu, by_group, by_group, len(POOL_WINDOWS), gc, gc, f"grad_pool_w_{l}", _WIRE_DTYPE)
            going.append(("pool_w", l, gpool.reshape(len(POOL_WINDOWS), N_DEV, -1, gc), 1))
        else:
            j = l - n_a
            qs, o, lse, mo = mixer
            do, go, gmix_post[l] = _oproj_bwd(dx, mo, o, w_o_f, post, l, j, f"oproj_bwd_{l}")
            dqs, dk, dv, gsinks[j] = _attn_bwd(qs, kdup, vdup, sinks, lse, do, j, f"attn_bwd_{l}")
            dks.append(dk)
            dvs.append(dv)
            dx, gq, gmix_pre[l] = _q_bwd(dx, dqs, x_in, mix_pre, w_q_f, cos, ssin, l, j, f"q_bwd_{l}")
            if j == 0:
                dx, gkv, gkv_g = _kv_bwd(dx, x_kv, kv_g, w_kv_f, cos, ssin, dks, dvs, "kv_bwd")
                going.append(("w_kv", 0, by_dev(gkv), 0))
            going += [("w_o", j, by_dev(go), 0), ("w_q", j, by_dev(gq), 0)]
        after = dx
        if l == 0:
            small = small_grads()
            leaving = [g for _, g, _, _ in small] + [jnp.broadcast_to(loss_part, (1, LANES))]
            small_flight, after = _exchange_start(GATHER, leaving, [0] * len(leaving), dx, "gather_small_grads")
        token = launch(going, after, f"scatter_start_{l}")
        ffn_post_b = ffn_post_b + token[0, 0]

    grad_x = dx.reshape(x.shape)

    shard = {"pool_w": (pool_w, m_pool_w, v_pool_w), "w_kv": (w_kv, m_w_kv, v_w_kv), "w_q": (w_q, m_w_q, v_w_q),
             "w_o": (w_o, m_w_o, v_w_o), "ffn_w_in": tuple(by_hidden(a) for a in (ffn_w_in, m_ffn_w_in, v_ffn_w_in)),
             "ffn_w_out": (ffn_w_out, m_ffn_w_out, v_ffn_w_out)}
    big = {}

    def arrive(flights, after):
        for idx, (names, handle) in flights:
            parts = _exchange_wait(handle, after, f"scatter_wait_{idx}")
            for (nm, slab), p in zip(names, parts):
                cols = p.shape[-1]
                w2, m2, v2 = (a.reshape(-1, cols) for a in shard[nm])
                big[nm] = _adamw(p.reshape(N_DEV, -1, cols), w2, m2, v2, slab, big.get(nm), f"adamw_{nm}_{slab}")
                after = big[nm][0]
        return after

    done = arrive(list(enumerate(flying)), token)
    *small_parts, loss_parts = _exchange_wait(small_flight, done, "gather_small_grads_wait")
    loss = jnp.sum(loss_parts[:, 0, 0])
    upd = _adamw_small(small_parts, [pick for _, _, pick, _ in small], [wmv for _, _, _, wmv in small], "adamw_small")
    res = {nm: tuple(r.reshape(shard[nm][0].shape) for r in out) for nm, out in big.items()}
    res["ffn_w_in"] = tuple(by_hidden(r) for r in res["ffn_w_in"])
    for (nm, _, _, _), out in zip(small, upd):
        res[nm] = tuple(a.reshape(kv_norm_g.shape) for a in out) if nm == "kv_norm_g" else out

    order = ["mix_pre_g", "mix_post_g", "pool_w", "pool_scale", "kv_norm_g", "w_kv", "w_q", "w_o", "sinks", "ffn_pre_g",
             "ffn_post_g", "ffn_w_in", "ffn_conv_w", "ffn_conv_b", "ffn_w_out"]
    return (loss, grad_x, *[res[nm][0] for nm in order], *[res[nm][1] for nm in order],
            *[res[nm][2] for nm in order], *[res[nm][3] for nm in order])
```

```python
import functools
import math

import jax
import jax.numpy as jnp
from jax import lax
from jax.experimental import pallas as pl
from jax.experimental.pallas import tpu as pltpu

F32 = jnp.float32
_MXU_DTYPE = jnp.bfloat16
_ACT_DTYPE = jnp.bfloat16
_WIRE_DTYPE = jnp.bfloat16
_SAVE_DTYPE = jnp.bfloat16

N_DEV = 8
POOL_WINDOWS = (2, 4, 8, 16)
POOL_HALO = 32
assert POOL_WINDOWS == tuple(2 ** (g + 1) for g in range(len(POOL_WINDOWS))) and 8 * len(POOL_WINDOWS) <= POOL_HALO
HEAD_DIM = 64
N_KV_HEADS = 4
WINDOW = 128
BLOCK = 128
LANES = 128
ROPE_THETA = 10000.0
ATTN_SCALE = 1.0 / math.sqrt(HEAD_DIM)
NEG_INF = -1e30
RMS_EPS = 1e-6
CONV_HALO = 8
SAVE_HALO = 16
PROJ_TILE = 512
ADAM_LR = 0.001
ADAM_B1 = 0.9
ADAM_B2 = 0.999
ADAM_EPS = 1e-08
ADAM_WD = 0.01
ADAM_STEP = 10
VMEM_LIMIT = 56 * 1024 * 1024
MESH = pl.DeviceIdType.MESH


def _params(n_axes=1, vmem=VMEM_LIMIT):
    return pltpu.CompilerParams(dimension_semantics=("arbitrary",) * n_axes, vmem_limit_bytes=vmem)


def _resident(shape, index):
    return pl.BlockSpec(shape, lambda *_: index, pipeline_mode=pl.Buffered(1))


def _const(shape, index=None):
    index = (0,) * len(shape) if index is None else index
    return pl.BlockSpec(shape, lambda *_: index)


def _rows(tm, cols):
    return pl.BlockSpec((tm, cols), lambda i: (i, 0))


def _row_tile(t, most=256):
    for tm in (512, 256, 128, 64, 32, 16, 8):
        if tm <= most and t % tm == 0:
            return tm
    raise ValueError(f"sequence length {t} is not a multiple of 8")


def _mm(a, b):
    return jnp.dot(a.astype(_MXU_DTYPE), b.astype(_MXU_DTYPE), preferred_element_type=F32)


def _mm_tb(a, b):
    return lax.dot_general(a.astype(_MXU_DTYPE), b.astype(_MXU_DTYPE), (((1,), (1,)), ((), ())),
                           preferred_element_type=F32)


def _mm_ta(a, b):
    return lax.dot_general(a.astype(_MXU_DTYPE), b.astype(_MXU_DTYPE), (((0,), (0,)), ((), ())),
                           preferred_element_type=F32)


def _rms_r(x):
    return lax.rsqrt(jnp.mean(x * x, axis=-1, keepdims=True) + RMS_EPS)


def _rms_fwd(x, g):
    return (x * _rms_r(x)) * g


def _rms_bwd(x, g, dy):
    r = _rms_r(x)
    xh = x * r
    dg = jnp.sum(dy * xh, axis=0, keepdims=True)
    dxh = dy * g
    dx = r * (dxh - xh * jnp.mean(dxh * xh, axis=-1, keepdims=True))
    return dx, dg


_GELU_C = math.sqrt(2.0 / math.pi)


def _gelu_parts(z):
    z2 = z * z
    e = jnp.exp(z * (-2.0 * _GELU_C - (2.0 * _GELU_C * 0.044715) * z2))
    cdf = pl.reciprocal(1.0 + e, approx=False)
    dz = cdf + (z * (cdf * (1.0 - cdf))) * (2.0 * _GELU_C + (6.0 * _GELU_C * 0.044715) * z2)
    return cdf, dz


def _lane_iota(shape):
    return lax.broadcasted_iota(jnp.int32, shape, len(shape) - 1)


def _rope_partner(xb):
    first = (_lane_iota(xb.shape) & 32) == 0
    return jnp.where(first, pltpu.roll(xb, LANES - 32, 1), pltpu.roll(xb, 32, 1))


def _rope_fwd(xb, cos, ssin):
    return xb * cos + _rope_partner(xb) * ssin


def _rope_bwd(dyb, cos, ssin):
    return dyb * cos - _rope_partner(dyb) * ssin


def _low_half(shape):
    return (_lane_iota(shape) & 64) == 0


def _pool_fwd(x, pre_g, post_g, w, scale, layer, name):
    t, d_model = x.shape
    tm = _row_tile(t)
    n_groups, gc = w.shape[1], w.shape[2]

    def body(x_ref, pre_ref, post_ref, w_ref, sc_ref, xo_ref, d_ref, yu_ref, hbuf, sbuf):
        i = pl.program_id(0)

        @pl.when(i == 0)
        def _():
            hbuf[pl.ds(0, POOL_HALO), :] = jnp.zeros((POOL_HALO, d_model), F32)

        xv = x_ref[...]
        hbuf[pl.ds(POOL_HALO, tm), :] = _rms_fwd(xv, pre_ref[...])
        tok = i * tm + lax.broadcasted_iota(jnp.int32, (tm, 1), 0)
        yus = []
        for gi, wnd in enumerate(POOL_WINDOWS):
            first, reach = CONV_HALO * (gi + 1), wnd // 2
            n = POOL_HALO + tm - first
            cols = pl.ds(gi * gc, d_model - gi * gc)
            src = hbuf if gi == 0 else sbuf.at[gi - 1]
            level = src[pl.ds(first, n), cols] + src[pl.ds(first - reach, n), cols]
            if gi + 1 < len(POOL_WINDOWS):
                sbuf[gi, pl.ds(first, n), cols] = level
            h = hbuf[pl.ds(POOL_HALO, tm), pl.ds(gi * gc, gc)]
            cnt = jnp.minimum(tok + 1, wnd).astype(F32)
            dg = level[POOL_HALO - first:, :gc] / cnt - h
            d_ref[:, pl.ds(gi * gc, gc)] = dg.astype(d_ref.dtype)
            yus.append(_mm(dg, w_ref[gi]))
        hbuf[pl.ds(0, POOL_HALO), :] = hbuf[pl.ds(tm, POOL_HALO), :]
        yu = jnp.concatenate(yus, axis=1)
        yu_ref[...] = yu
        xo_ref[...] = xv + _rms_fwd(yu * sc_ref[...], post_ref[...])

    return pl.pallas_call(
        body, name=name, grid=(t // tm,),
        out_shape=(jax.ShapeDtypeStruct((t, d_model), F32), jax.ShapeDtypeStruct((t, d_model), _ACT_DTYPE),
                   jax.ShapeDtypeStruct((t, d_model), F32)),
        in_specs=[_rows(tm, d_model), _const((None, 1, d_model), (layer, 0, 0)), _const((None, 1, d_model), (layer, 0, 0)),
                  _const((None, n_groups, gc, gc), (layer, 0, 0, 0)), _const((None, 1, d_model), (layer, 0, 0))],
        out_specs=(_rows(tm, d_model), _rows(tm, d_model), _rows(tm, d_model)),
        scratch_shapes=[pltpu.VMEM((POOL_HALO + tm, d_model), F32),
                        pltpu.VMEM((len(POOL_WINDOWS) - 1, POOL_HALO + tm, d_model), F32)],
        compiler_params=_params(),
    )(x, pre_g, post_g, w, scale)


def _pool_bwd(dx, x, pre_g, post_g, d, yu, w, scale, layer, name):
    t, d_model = x.shape
    tm = _row_tile(t)
    nt = t // tm
    n_groups, gc = w.shape[1], w.shape[2]
    rev = lambda i: (nt - 1 - i, 0)
    rows = pl.BlockSpec((tm, d_model), rev)

    def body(dx_ref, x_ref, pre_ref, post_ref, d_ref, yu_ref, w_ref, sc_ref,
             dxi_ref, dyu_ref, dsc_ref, dpre_ref, dpost_ref, zbuf, sbuf):
        i = pl.program_id(0)

        @pl.when(i == 0)
        def _():
            zbuf[pl.ds(tm, POOL_HALO), :] = jnp.zeros((POOL_HALO, d_model), F32)
            dsc_ref[...] = jnp.zeros_like(dsc_ref)
            dpre_ref[...] = jnp.zeros_like(dpre_ref)
            dpost_ref[...] = jnp.zeros_like(dpost_ref)

        dxo = dx_ref[...]
        yuv = yu_ref[...]
        sc = sc_ref[...]
        dm, dpost = _rms_bwd(yuv * sc, post_ref[...], dxo)
        dpost_ref[...] += dpost
        dsc_ref[...] += jnp.sum(dm * yuv, axis=0, keepdims=True)
        dyu = dm * sc
        dyu_ref[...] = dyu.astype(dyu_ref.dtype)
        tok = (nt - 1 - i) * tm + lax.broadcasted_iota(jnp.int32, (tm, 1), 0)
        dds = []
        for gi, wnd in enumerate(POOL_WINDOWS):
            cols = pl.ds(gi * gc, gc)
            dd = _mm_tb(dyu[:, gi * gc:(gi + 1) * gc], w_ref[gi])
            cnt = jnp.minimum(tok + 1, wnd).astype(F32)
            zbuf[pl.ds(0, tm), cols] = dd / cnt
            dds.append(dd)
        dhs = []
        for gi, wnd in enumerate(POOL_WINDOWS):
            reach = wnd // 2
            n = tm + POOL_HALO - CONV_HALO * (gi + 1)
            cols = pl.ds(gi * gc, d_model - gi * gc)
            src = zbuf if gi == 0 else sbuf.at[gi - 1]
            level = src[pl.ds(0, n), cols] + src[pl.ds(reach, n), cols]
            if gi + 1 < len(POOL_WINDOWS):
                sbuf[gi, pl.ds(0, n), cols] = level
            dhs.append(level[:tm, :gc] - dds[gi])
        zbuf[pl.ds(tm, POOL_HALO), :] = zbuf[pl.ds(0, POOL_HALO), :]
        dh = jnp.concatenate(dhs, axis=1)
        dxp, dpre = _rms_bwd(x_ref[...], pre_ref[...], dh)
        dpre_ref[...] += dpre
        dxi_ref[...] = dxo + dxp

    vec = jax.ShapeDtypeStruct((1, d_model), F32)
    return pl.pallas_call(
        body, name=name, grid=(nt,),
        out_shape=(jax.ShapeDtypeStruct((t, d_model), F32), jax.ShapeDtypeStruct((t, d_model), _ACT_DTYPE), vec, vec, vec),
        in_specs=[rows, rows, _const((None, 1, d_model), (layer, 0, 0)), _const((None, 1, d_model), (layer, 0, 0)), rows, rows,
                  _const((None, n_groups, gc, gc), (layer, 0, 0, 0)), _const((None, 1, d_model), (layer, 0, 0))],
        out_specs=(rows, rows, _const((1, d_model)), _const((1, d_model)), _const((1, d_model))),
        scratch_shapes=[pltpu.VMEM((tm + POOL_HALO, d_model), F32),
                        pltpu.VMEM((len(POOL_WINDOWS) - 1, tm + POOL_HALO, d_model), F32)],
        compiler_params=_params(),
    )(dx, x, pre_g, post_g, d, yu, w, scale)


def _conv_taps(cw_ref, s):
    return [cw_ref[k, pl.ds(s, 1), :] for k in range(3)]


def _shift_down(v, k, before):
    rolled = pltpu.roll(v, k, 0)
    row = lax.broadcasted_iota(jnp.int32, before.shape, 0)
    head = jnp.where(row < k, pltpu.roll(before, k, 0), rolled[:CONV_HALO])
    return jnp.concatenate([head, rolled[CONV_HALO:]], axis=0)


def _shift_up(v, k, after):
    rows = v.shape[0]
    rolled = pltpu.roll(v, rows - k, 0)
    row = lax.broadcasted_iota(jnp.int32, after.shape, 0)
    tail = jnp.where(row >= CONV_HALO - k, pltpu.roll(after, CONV_HALO - k, 0), rolled[rows - CONV_HALO:])
    return jnp.concatenate([rolled[:rows - CONV_HALO], tail], axis=0)


def _ffn_fwd(x, pre_g, post_g, w_in, conv_w, conv_b, w_out, layer, w_layer, name):
    t, d_model = x.shape
    tm = _row_tile(t)
    fs = w_in.shape[2]
    half = N_DEV // 2

    def body(x_ref, pre_ref, post_ref, win_ref, cw_ref, cb_ref, wout_ref, xo_ref, u_ref, uc_ref, f_ref, carry):
        i = pl.program_id(0)

        @pl.when(i == 0)
        def _():
            carry[...] = jnp.zeros_like(carry)

        xv = x_ref[...]
        hf = _rms_fwd(xv, pre_ref[...]).astype(_MXU_DTYPE)
        f = jnp.zeros((tm, d_model), F32)
        project = lambda b: [_mm_tb(hf, win_ref[s]) for s in (b, b + half)]
        ahead = project(0)
        for b in range(half):
            us, ahead = ahead, project(b + 1) if b + 1 < half else None
            ucs = []
            for s, u in zip((b, b + half), us):
                u_ref[s] = u.astype(u_ref.dtype)
                before = carry[s]
                carry[s] = u[tm - CONV_HALO:]
                w0, w1, w2 = _conv_taps(cw_ref, s)
                uc = ((w0 * _shift_down(u, 2, before) + w1 * _shift_down(u, 1, before)) + w2 * u) + cb_ref[pl.ds(s, 1), :]
                uc_ref[s] = uc.astype(uc_ref.dtype)
                ucs.append(uc)
            gate, val = ucs
            cdf, _ = _gelu_parts(gate)
            f = f + _mm((gate * cdf) * val, wout_ref[pl.ds(b * fs, fs), :])
        f_ref[...] = f
        xo_ref[...] = xv + _rms_fwd(f, post_ref[...])

    tile3 = pl.BlockSpec((N_DEV, tm, fs), lambda i: (0, i, 0))
    saved = jax.ShapeDtypeStruct((N_DEV, t, fs), _SAVE_DTYPE)
    return pl.pallas_call(
        body, name=name, grid=(t // tm,),
        out_shape=(jax.ShapeDtypeStruct((t, d_model), F32), saved, saved, jax.ShapeDtypeStruct((t, d_model), F32)),
        in_specs=[_rows(tm, d_model), _const((None, 1, d_model), (layer, 0, 0)), _const((None, 1, d_model), (layer, 0, 0)),
                  _resident((None, N_DEV, fs, d_model), (w_layer, 0, 0, 0)), _const((None, 3, N_DEV, fs), (layer, 0, 0, 0)),
                  _const((None, N_DEV, fs), (layer, 0, 0)), _resident((None, half * fs, d_model), (w_layer, 0, 0))],
        out_specs=(_rows(tm, d_model), tile3, tile3, _rows(tm, d_model)),
        scratch_shapes=[pltpu.VMEM((N_DEV, CONV_HALO, fs), F32)],
        compiler_params=_params(),
    )(x, pre_g, post_g, w_in, conv_w, conv_b, w_out)


def _ffn_bwd_out(dx, f, post_g, uc, conv_b, w_out, layer, w_layer, name):
    t, d_model = dx.shape
    tm = _row_tile(t)
    nt = t // tm
    fs = uc.shape[2]
    half = N_DEV // 2

    def body(dx_ref, f_ref, post_ref, uc_ref, wout_ref, duc_ref, dwout_ref, dcb_ref, dpost_ref, acc):
        i = pl.program_id(0)

        @pl.when(i == 0)
        def _():
            acc[...] = jnp.zeros_like(acc)
            dcb_ref[...] = jnp.zeros_like(dcb_ref)
            dpost_ref[...] = jnp.zeros_like(dpost_ref)

        df, dpost = _rms_bwd(f_ref[...], post_ref[...], dx_ref[...])
        dpost_ref[...] += dpost
        dfm = df.astype(_MXU_DTYPE)
        project = lambda b: _mm_tb(dfm, wout_ref[pl.ds(b * fs, fs), :])
        ahead = project(0)
        for b in range(half):
            dg, ahead = ahead, project(b + 1) if b + 1 < half else None
            gate = uc_ref[b].astype(F32)
            val = uc_ref[b + half].astype(F32)
            cdf, dgelu = _gelu_parts(gate)
            ge = gate * cdf
            acc[pl.ds(b * fs, fs), :] += _mm_ta(ge * val, dfm)
            for s, dd in ((b, dg * val * dgelu), (b + half, dg * ge)):
                duc_ref[s] = dd.astype(duc_ref.dtype)
                dcb_ref[pl.ds(s, 1), :] += jnp.sum(dd, axis=0, keepdims=True)

        @pl.when(i == nt - 1)
        def _():
            dwout_ref[...] = acc[...].astype(dwout_ref.dtype)

    tile3 = pl.BlockSpec((N_DEV, tm, fs), lambda i: (0, i, 0))
    return pl.pallas_call(
        body, name=name, grid=(nt,),
        out_shape=(jax.ShapeDtypeStruct((N_DEV, t, fs), _SAVE_DTYPE), jax.ShapeDtypeStruct((half * fs, d_model), _WIRE_DTYPE),
                   jax.ShapeDtypeStruct((N_DEV, fs), F32), jax.ShapeDtypeStruct((1, d_model), F32)),
        in_specs=[_rows(tm, d_model), _rows(tm, d_model), _const((None, 1, d_model), (layer, 0, 0)), tile3,
                  _resident((None, half * fs, d_model), (w_layer, 0, 0))],
        out_specs=(tile3, _resident((half * fs, d_model), (0, 0)), _const((N_DEV, fs)), _const((1, d_model))),
        scratch_shapes=[pltpu.VMEM((half * fs, d_model), F32)],
        compiler_params=_params(),
    )(dx, f, post_g, uc, w_out)


def _ffn_bwd_in(dx, x, pre_g, duc, u, conv_w, w_in, layer, w_layer, name):
    t, d_model = dx.shape
    tm = _row_tile(t)
    nt = t // tm
    fs = duc.shape[2]
    hb = SAVE_HALO
    per_tile = tm // hb

    def body(dx_ref, x_ref, pre_ref, duc_ref, dn_ref, u_ref, cw_ref, win_ref, dxi_ref, du_ref, hf_ref, dcw_ref, dpre_ref):
        i = pl.program_id(0)

        @pl.when(i == 0)
        def _():
            dcw_ref[...] = jnp.zeros_like(dcw_ref)
            dpre_ref[...] = jnp.zeros_like(dpre_ref)

        xv = x_ref[...]
        pre = pre_ref[...]
        hf_ref[...] = _rms_fwd(xv, pre).astype(hf_ref.dtype)
        dhf = jnp.zeros((tm, d_model), F32)
        for s in range(N_DEV):
            d0 = duc_ref[s].astype(F32)
            after = jnp.where(i == nt - 1, 0.0, dn_ref[s].astype(F32)[:CONV_HALO])
            d1 = _shift_up(d0, 1, after)
            d2 = _shift_up(d0, 2, after)
            uv = u_ref[s].astype(F32)
            for k, dk in ((2, d0), (1, d1), (0, d2)):
                dcw_ref[k, pl.ds(s, 1), :] += jnp.sum(dk * uv, axis=0, keepdims=True)
            w0, w1, w2 = _conv_taps(cw_ref, s)
            du = (w2 * d0 + w1 * d1 + w0 * d2).astype(_MXU_DTYPE)
            du_ref[s] = du
            dhf = dhf + _mm(du, win_ref[s])
        dxp, dpre = _rms_bwd(xv, pre, dhf)
        dpre_ref[...] += dpre
        dxi_ref[...] = dx_ref[...] + dxp

    tile3 = pl.BlockSpec((N_DEV, tm, fs), lambda i: (0, i, 0))
    return pl.pallas_call(
        body, name=name, grid=(nt,),
        out_shape=(jax.ShapeDtypeStruct((t, d_model), F32), jax.ShapeDtypeStruct((N_DEV, t, fs), _ACT_DTYPE),
                   jax.ShapeDtypeStruct((t, d_model), _ACT_DTYPE), jax.ShapeDtypeStruct((3, N_DEV, fs), F32),
                   jax.ShapeDtypeStruct((1, d_model), F32)),
        in_specs=[_rows(tm, d_model), _rows(tm, d_model), _const((None, 1, d_model), (layer, 0, 0)), tile3,
                  pl.BlockSpec((N_DEV, hb, fs), lambda i: (0, jnp.minimum((i + 1) * per_tile, t // hb - 1), 0)), tile3,
                  _const((None, 3, N_DEV, fs), (layer, 0, 0, 0)), _resident((None, N_DEV, fs, d_model), (w_layer, 0, 0, 0))],
        out_specs=(_rows(tm, d_model), tile3, _rows(tm, d_model), _const((3, N_DEV, fs)), _const((1, d_model))),
        compiler_params=_params(),
    )(dx, x, pre_g, duc, duc, u, conv_w, w_in)


def _tn_matmul(a, b, a_spec, b_spec, n_out, m, n, name, out_dtype):
    def body(a_ref, b_ref, o_ref):
        o_ref[...] = _mm_ta(a_ref[...], b_ref[...]).astype(o_ref.dtype)

    return pl.pallas_call(
        body, name=name, grid=(n_out,),
        out_shape=jax.ShapeDtypeStruct((n_out, m, n), out_dtype),
        in_specs=[a_spec, b_spec],
        out_specs=pl.BlockSpec((None, m, n), lambda c: (c, 0, 0)),
        compiler_params=_params(),
    )(a, b)


def _kv_fwd(x, kv_g, w_kv, cos, ssin, name):
    t, d_model = x.shape
    tm = _row_tile(t, PROJ_TILE)
    kvd = w_kv.shape[1] // 2
    pairs = kvd // LANES

    def body(x_ref, g_ref, w_ref, cos_ref, sin_ref, k_ref, v_ref):
        kv = _mm(_rms_fwd(x_ref[...], g_ref[...]), w_ref[...])
        low = _low_half((tm, LANES))
        for j in range(pairs):
            kb = _rope_fwd(kv[:, j * LANES:(j + 1) * LANES], cos_ref[...], sin_ref[...])
            vb = kv[:, kvd + j * LANES:kvd + (j + 1) * LANES]
            for blk, ref in ((kb, k_ref), (vb, v_ref)):
                sw = pltpu.roll(blk, 64, 1)
                ref[2 * j] = jnp.where(low, blk, sw).astype(ref.dtype)
                ref[2 * j + 1] = jnp.where(low, sw, blk).astype(ref.dtype)

    heads = jax.ShapeDtypeStruct((N_KV_HEADS, t, LANES), _ACT_DTYPE)
    hspec = pl.BlockSpec((N_KV_HEADS, tm, LANES), lambda i: (0, i, 0))
    return pl.pallas_call(
        body, name=name, grid=(t // tm,), out_shape=(heads, heads),
        in_specs=[_rows(tm, d_model), _const((1, d_model)), _const(w_kv.shape), _rows(tm, LANES), _rows(tm, LANES)],
        out_specs=(hspec, hspec), compiler_params=_params(),
    )(x, kv_g, w_kv, cos, ssin)


def _kv_bwd(dx, x, kv_g, w_kv, cos, ssin, dks, dvs, name):
    t, d_model = x.shape
    tm = _row_tile(t, PROJ_TILE)
    nt = t // tm
    kvd = w_kv.shape[1] // 2
    pairs = kvd // LANES
    n_users = len(dks)

    def body(dx_ref, x_ref, g_ref, w_ref, cos_ref, sin_ref, *refs):
        dk_refs, dv_refs = refs[:n_users], refs[n_users:2 * n_users]
        dxi_ref, dw_ref, dg_ref, acc = refs[2 * n_users:]
        dk_ref = functools.reduce(lambda a, b: a + b, [r[...] for r in dk_refs])
        dv_ref = functools.reduce(lambda a, b: a + b, [r[...] for r in dv_refs])
        i = pl.program_id(0)

        @pl.when(i == 0)
        def _():
            dg_ref[...] = jnp.zeros_like(dg_ref)
            acc[...] = jnp.zeros_like(acc)

        xv = x_ref[...]
        g = g_ref[...]
        low = _low_half((tm, LANES))
        dks, dvs = [], []
        for j in range(pairs):
            dkb = jnp.where(low, dk_ref[2 * j], dk_ref[2 * j + 1])
            dks.append(_rope_bwd(dkb, cos_ref[...], sin_ref[...]))
            dvs.append(jnp.where(low, dv_ref[2 * j], dv_ref[2 * j + 1]))
        dkv = jnp.concatenate(dks + dvs, axis=1).astype(_MXU_DTYPE)
        acc[...] += _mm_ta(_rms_fwd(xv, g), dkv)
        dxp, dg = _rms_bwd(xv, g, _mm_tb(dkv, w_ref[...]))
        dg_ref[...] += dg
        dxi_ref[...] = dx_ref[...] + dxp

        @pl.when(i == nt - 1)
        def _():
            dw_ref[...] = acc[...].astype(dw_ref.dtype)

    hspec = pl.BlockSpec((N_KV_HEADS, tm, LANES), lambda i: (0, i, 0))
    return pl.pallas_call(
        body, name=name, grid=(nt,),
        out_shape=(jax.ShapeDtypeStruct((t, d_model), F32), jax.ShapeDtypeStruct(w_kv.shape, _WIRE_DTYPE),
                   jax.ShapeDtypeStruct((1, d_model), F32)),
        in_specs=[_rows(tm, d_model), _rows(tm, d_model), _const((1, d_model)), _const(w_kv.shape),
                  _rows(tm, LANES), _rows(tm, LANES)] + [hspec] * (2 * n_users),
        out_specs=(_rows(tm, d_model), _resident(w_kv.shape, (0, 0)), _const((1, d_model))),
        scratch_shapes=[pltpu.VMEM(w_kv.shape, F32)],
        compiler_params=_params(),
    )(dx, x, kv_g, w_kv, cos, ssin, *dks, *dvs)


def _q_fwd(x, pre_g, w_q, cos, ssin, layer, j, name):
    t, d_model = x.shape
    tm = _row_tile(t, PROJ_TILE)

    def body(x_ref, g_ref, w_ref, cos_ref, sin_ref, q_ref):
        q = _mm(_rms_fwd(x_ref[...], g_ref[...]), w_ref[...])
        for p in range(d_model // LANES):
            cols = slice(p * LANES, (p + 1) * LANES)
            q_ref[:, cols] = (_rope_fwd(q[:, cols], cos_ref[...], sin_ref[...]) * ATTN_SCALE).astype(q_ref.dtype)

    return pl.pallas_call(
        body, name=name, grid=(t // tm,), out_shape=jax.ShapeDtypeStruct((t, d_model), _ACT_DTYPE),
        in_specs=[_rows(tm, d_model), _const((None, 1, d_model), (layer, 0, 0)), _const((None, d_model, d_model), (j, 0, 0)),
                  _rows(tm, LANES), _rows(tm, LANES)],
        out_specs=_rows(tm, d_model), compiler_params=_params(),
    )(x, pre_g, w_q, cos, ssin)


def _q_bwd(dx, dqs, x, pre_g, w_q, cos, ssin, layer, j, name):
    t, d_model = x.shape
    tm = _row_tile(t, PROJ_TILE)
    nt = t // tm

    def body(dx_ref, dq_ref, x_ref, g_ref, w_ref, cos_ref, sin_ref, dxi_ref, dw_ref, dg_ref, acc):
        i = pl.program_id(0)

        @pl.when(i == 0)
        def _():
            dg_ref[...] = jnp.zeros_like(dg_ref)
            acc[...] = jnp.zeros_like(acc)

        xv = x_ref[...]
        g = g_ref[...]
        parts = []
        for p in range(d_model // LANES):
            cols = slice(p * LANES, (p + 1) * LANES)
            parts.append(_rope_bwd(dq_ref[:, cols] * ATTN_SCALE, cos_ref[...], sin_ref[...]))
        dq = jnp.concatenate(parts, axis=1).astype(_MXU_DTYPE)
        acc[...] += _mm_ta(_rms_fwd(xv, g), dq)
        dxp, dg = _rms_bwd(xv, g, _mm_tb(dq, w_ref[...]))
        dg_ref[...] += dg
        dxi_ref[...] = dx_ref[...] + dxp

        @pl.when(i == nt - 1)
        def _():
            dw_ref[...] = acc[...].astype(dw_ref.dtype)

    return pl.pallas_call(
        body, name=name, grid=(nt,),
        out_shape=(jax.ShapeDtypeStruct((t, d_model), F32), jax.ShapeDtypeStruct((d_model, d_model), _WIRE_DTYPE),
                   jax.ShapeDtypeStruct((1, d_model), F32)),
        in_specs=[_rows(tm, d_model), _rows(tm, d_model), _rows(tm, d_model), _const((None, 1, d_model), (layer, 0, 0)),
                  _const((None, d_model, d_model), (j, 0, 0)), _rows(tm, LANES), _rows(tm, LANES)],
        out_specs=(_rows(tm, d_model), _resident((d_model, d_model), (0, 0)), _const((1, d_model))),
        scratch_shapes=[pltpu.VMEM((d_model, d_model), F32)],
        compiler_params=_params(),
    )(dx, dqs, x, pre_g, w_q, cos, ssin)


def _stack_heads(pairs):
    low = _low_half(pairs[0].shape)
    zero = jnp.zeros_like(pairs[0])
    return jnp.concatenate([h for blk in pairs for h in (jnp.where(low, blk, zero), jnp.where(low, zero, blk))], axis=0)


def _unstack_heads(stacked, i):
    a, b = stacked[2 * i * BLOCK:(2 * i + 1) * BLOCK], stacked[(2 * i + 1) * BLOCK:(2 * i + 2) * BLOCK]
    return jnp.where(_low_half(a.shape), a, b)


def _attn_scores(q_pairs, k2, n, sinks):
    qst = _stack_heads(q_pairs)
    s = _mm_tb(qst, k2)
    row = lax.broadcasted_iota(jnp.int32, s.shape, 0)
    col = lax.broadcasted_iota(jnp.int32, s.shape, 1)
    rel = BLOCK + (row & (BLOCK - 1)) - col
    valid = (rel >= 0) & (rel < WINDOW) & (n * BLOCK + col - BLOCK >= 0)
    s = jnp.where(valid, s, NEG_INF)
    rows1 = lax.broadcasted_iota(jnp.int32, (s.shape[0], 1), 0)
    sink = jnp.full((s.shape[0], 1), sinks[-1], F32)
    for i in reversed(range(len(sinks) - 1)):
        sink = jnp.where(rows1 < (i + 1) * BLOCK, sinks[i], sink)
    return qst, s, sink


def _attn_fwd(qs, kdup, vdup, sinks, j, name):
    t, d_model = qs.shape
    nb = t // BLOCK
    n_pairs = d_model // LANES
    per_group = n_pairs // N_KV_HEADS

    def body(sink_ref, q_ref, kp_ref, ko_ref, vp_ref, vo_ref, o_ref, lse_ref):
        n = pl.program_id(0)
        lane = _lane_iota((BLOCK, LANES))
        lse = jnp.zeros((BLOCK, LANES), F32)
        for hk in range(N_KV_HEADS):
            pairs = range(hk * per_group, (hk + 1) * per_group)
            heads = range(2 * pairs[0], 2 * pairs[-1] + 2)
            k2 = jnp.concatenate([kp_ref[hk], ko_ref[hk]], axis=0)
            v2 = jnp.concatenate([vp_ref[hk], vo_ref[hk]], axis=0)
            _, s, sink = _attn_scores([q_ref[:, p * LANES:(p + 1) * LANES] for p in pairs], k2, n,
                                      [sink_ref[j, h] for h in heads])
            m = jnp.maximum(jnp.max(s, axis=-1, keepdims=True), sink)
            pe = jnp.exp(s - m)
            denom = jnp.sum(pe, axis=-1, keepdims=True) + jnp.exp(sink - m)
            o2 = _mm(pe, v2) / denom
            l2 = m + jnp.log(denom)
            for i, p in enumerate(pairs):
                o_ref[:, p * LANES:(p + 1) * LANES] = _unstack_heads(o2, i).astype(o_ref.dtype)
            for i, h in enumerate(heads):
                lse = jnp.where(lane == h, l2[i * BLOCK:(i + 1) * BLOCK], lse)
        lse_ref[...] = lse

    prev = pl.BlockSpec((N_KV_HEADS, BLOCK, LANES), lambda n: (0, jnp.maximum(n - 1, 0), 0))
    own = pl.BlockSpec((N_KV_HEADS, BLOCK, LANES), lambda n: (0, n, 0))
    return pl.pallas_call(
        body, name=name, grid=(nb,),
        out_shape=(jax.ShapeDtypeStruct((t, d_model), _ACT_DTYPE), jax.ShapeDtypeStruct((t, LANES), F32)),
        in_specs=[pl.BlockSpec(memory_space=pltpu.SMEM), _rows(BLOCK, d_model), prev, own, prev, own],
        out_specs=(_rows(BLOCK, d_model), _rows(BLOCK, LANES)), compiler_params=_params(),
    )(sinks, qs, kdup, kdup, vdup, vdup)


def _attn_bwd(qs, kdup, vdup, sinks, lse, do, j, name):
    t, d_model = qs.shape
    nb = t // BLOCK
    n_pairs = d_model // LANES
    per_group = n_pairs // N_KV_HEADS
    rev = lambda n: nb - 1 - n

    def body(sink_ref, q_ref, kp_ref, ko_ref, vp_ref, vo_ref, lse_ref, do_ref, dq_ref, dk_ref, dv_ref, ds_ref, ck, cv):
        i = pl.program_id(0)
        n = nb - 1 - i

        @pl.when(i == 0)
        def _():
            ck[...] = jnp.zeros_like(ck)
            cv[...] = jnp.zeros_like(cv)
            ds_ref[...] = jnp.zeros_like(ds_ref)

        lane = _lane_iota((BLOCK, LANES))
        lane1 = _lane_iota((1, LANES))
        lsev = lse_ref[...]
        dsink = jnp.zeros((1, LANES), F32)
        for hk in range(N_KV_HEADS):
            pairs = range(hk * per_group, (hk + 1) * per_group)
            heads = range(2 * pairs[0], 2 * pairs[-1] + 2)
            k2 = jnp.concatenate([kp_ref[hk], ko_ref[hk]], axis=0)
            v2 = jnp.concatenate([vp_ref[hk], vo_ref[hk]], axis=0)
            qst, s, sink = _attn_scores([q_ref[:, p * LANES:(p + 1) * LANES] for p in pairs], k2, n,
                                        [sink_ref[j, h] for h in heads])
            l2 = jnp.concatenate([jnp.sum(jnp.where(lane == h, lsev, 0.0), axis=-1, keepdims=True) for h in heads], axis=0)
            pn = jnp.exp(s - l2)
            dost = _stack_heads([do_ref[:, p * LANES:(p + 1) * LANES] for p in pairs])
            dp = _mm_tb(dost, v2)
            dr = jnp.sum(pn * dp, axis=-1, keepdims=True)
            dsm = (pn * (dp - dr)).astype(_MXU_DTYPE)
            dsk = -jnp.exp(sink - l2) * dr
            for i, h in enumerate(heads):
                dsink = dsink + jnp.where(lane1 == h, jnp.sum(dsk[i * BLOCK:(i + 1) * BLOCK]), 0.0)
            dq2 = _mm(dsm, k2)
            for i, p in enumerate(pairs):
                dq_ref[:, p * LANES:(p + 1) * LANES] = _unstack_heads(dq2, i)
            for acc, carry, ref in ((_mm_ta(dsm, qst), ck, dk_ref), (_mm_ta(pn, dost), cv, dv_ref)):
                folded = acc + pltpu.roll(acc, 64, 1)
                ref[hk] = folded[BLOCK:] + carry[hk]
                carry[hk] = folded[:BLOCK]
        ds_ref[...] += dsink

    prev = pl.BlockSpec((N_KV_HEADS, BLOCK, LANES), lambda n: (0, jnp.maximum(rev(n) - 1, 0), 0))
    own = pl.BlockSpec((N_KV_HEADS, BLOCK, LANES), lambda n: (0, rev(n), 0))
    rows = lambda cols: pl.BlockSpec((BLOCK, cols), lambda n: (rev(n), 0))
    heads = jax.ShapeDtypeStruct((N_KV_HEADS, t, LANES), F32)
    return pl.pallas_call(
        body, name=name, grid=(nb,),
        out_shape=(jax.ShapeDtypeStruct((t, d_model), F32), heads, heads, jax.ShapeDtypeStruct((1, LANES), F32)),
        in_specs=[pl.BlockSpec(memory_space=pltpu.SMEM), rows(d_model), prev, own, prev, own, rows(LANES), rows(d_model)],
        out_specs=(rows(d_model), own, own, _const((1, LANES))),
        scratch_shapes=[pltpu.VMEM((N_KV_HEADS, BLOCK, LANES), F32), pltpu.VMEM((N_KV_HEADS, BLOCK, LANES), F32)],
        compiler_params=_params(),
    )(sinks, qs, kdup, kdup, vdup, vdup, lse, do)


def _oproj_fwd(x, o, w_o, post_g, layer, j, name):
    t, d_model = x.shape
    tm = _row_tile(t, PROJ_TILE)

    def body(x_ref, o_ref, w_ref, g_ref, xo_ref, mo_ref):
        mo = _mm(o_ref[...], w_ref[...])
        mo_ref[...] = mo
        xo_ref[...] = x_ref[...] + _rms_fwd(mo, g_ref[...])

    full = jax.ShapeDtypeStruct((t, d_model), F32)
    return pl.pallas_call(
        body, name=name, grid=(t // tm,), out_shape=(full, full),
        in_specs=[_rows(tm, d_model), _rows(tm, d_model), _const((None, d_model, d_model), (j, 0, 0)),
                  _const((None, 1, d_model), (layer, 0, 0))],
        out_specs=(_rows(tm, d_model), _rows(tm, d_model)), compiler_params=_params(),
    )(x, o, w_o, post_g)


def _oproj_bwd(dx, mo, o, w_o, post_g, layer, j, name):
    t, d_model = dx.shape
    tm = _row_tile(t, PROJ_TILE)
    nt = t // tm

    def body(dx_ref, mo_ref, o_ref, w_ref, g_ref, do_ref, dw_ref, dg_ref, acc):
        i = pl.program_id(0)

        @pl.when(i == 0)
        def _():
            dg_ref[...] = jnp.zeros_like(dg_ref)
            acc[...] = jnp.zeros_like(acc)

        dmo, dg = _rms_bwd(mo_ref[...], g_ref[...], dx_ref[...])
        dg_ref[...] += dg
        dmo = dmo.astype(_MXU_DTYPE)
        acc[...] += _mm_ta(o_ref[...], dmo)
        do_ref[...] = _mm_tb(dmo, w_ref[...]).astype(do_ref.dtype)

        @pl.when(i == nt - 1)
        def _():
            dw_ref[...] = acc[...].astype(dw_ref.dtype)

    return pl.pallas_call(
        body, name=name, grid=(nt,),
        out_shape=(jax.ShapeDtypeStruct((t, d_model), _ACT_DTYPE), jax.ShapeDtypeStruct((d_model, d_model), _WIRE_DTYPE),
                   jax.ShapeDtypeStruct((1, d_model), F32)),
        in_specs=[_rows(tm, d_model), _rows(tm, d_model), _rows(tm, d_model), _const((None, d_model, d_model), (j, 0, 0)),
                  _const((None, 1, d_model), (layer, 0, 0))],
        out_specs=(_rows(tm, d_model), _resident((d_model, d_model), (0, 0)), _const((1, d_model))),
        scratch_shapes=[pltpu.VMEM((d_model, d_model), F32)],
        compiler_params=_params(),
    )(dx, mo, o, w_o, post_g)


def _loss_grad(y, target, name):
    t, d_model = y.shape
    tm = _row_tile(t, PROJ_TILE)

    def body(y_ref, t_ref, dy_ref, loss_ref):
        i = pl.program_id(0)

        @pl.when(i == 0)
        def _():
            loss_ref[...] = jnp.zeros_like(loss_ref)

        err = y_ref[...] - t_ref[...]
        dy_ref[...] = err / d_model
        loss_ref[...] += 0.5 * jnp.sum(jnp.mean(err * err, axis=-1, keepdims=True), axis=0, keepdims=True)

    return pl.pallas_call(
        body, name=name, grid=(t // tm,),
        out_shape=(jax.ShapeDtypeStruct((t, d_model), F32), jax.ShapeDtypeStruct((1, 1), F32)),
        in_specs=[_rows(tm, d_model), _rows(tm, d_model)], out_specs=(_rows(tm, d_model), _const((1, 1))),
        compiler_params=_params(),
    )(y, target)


def _mesh_position():
    return lax.axis_index("x"), lax.axis_index("y"), lax.axis_index("c")


def _block_of(px, py, pc):
    return 4 * px + 2 * py + pc


def _at_block(ref, axis, block):
    return ref.at[(slice(None),) * axis + (block,)]


def _all_gather(shards, axes, name):
    n = len(shards)

    def body(*refs):
        srcs, outs = refs[:n], refs[n:2 * n]
        send_sems, recv_sems, local_sems = refs[2 * n:]
        x, y, c = _mesh_position()
        me, sibling = (x, y, c), (x, y, 1 - c)
        chips = [(1 - x, y), (x, 1 - y), (1 - x, 1 - y)]

        def blk(i, pos):
            return _at_block(outs[i], axes[i], _block_of(*pos))

        def copy(i, k, block, to, src=None):
            return pltpu.make_async_remote_copy(
                src_ref=blk(i, block) if src is None else src, dst_ref=blk(i, block),
                send_sem=send_sems.at[i, k], recv_sem=recv_sems.at[i, k], device_id=to, device_id_type=MESH)

        mine = [pltpu.make_async_copy(srcs[i], blk(i, me), local_sems.at[i]) for i in range(n)]
        for cp in mine:
            cp.start()
        sent = []
        for i in range(n):
            sent += [copy(i, 1 + k, me, (*chip, c), src=srcs[i]) for k, chip in enumerate(chips)]
            sent.append(copy(i, 0, me, sibling, src=srcs[i]))
        for cp in sent:
            cp.start()
        for i in range(n):
            for k, chip in enumerate(chips):
                copy(i, 1 + k, (*chip, c), me).wait_recv()
                passed = copy(i, 4 + k, (*chip, c), sibling)
                passed.start()
                sent.append(passed)
        for i in range(n):
            copy(i, 0, sibling, me).wait_recv()
            for k, chip in enumerate(chips):
                copy(i, 4 + k, (*chip, 1 - c), me).wait_recv()
        for cp in sent:
            cp.wait_send()
        for cp in mine:
            cp.wait()

    hbm = pl.BlockSpec(memory_space=pl.ANY)
    return pl.pallas_call(
        body, name=name,
        out_shape=tuple(jax.ShapeDtypeStruct(s.shape[:a] + (N_DEV,) + s.shape[a:], s.dtype) for s, a in zip(shards, axes)),
        in_specs=[hbm] * n, out_specs=(hbm,) * n,
        scratch_shapes=[pltpu.SemaphoreType.DMA((n, 7)), pltpu.SemaphoreType.DMA((n, 7)), pltpu.SemaphoreType.DMA((n,))],
    )(*shards)


GATHER, SCATTER, GATHER_CHIPS, GATHER_SIBLING = "gather", "scatter", "gather_chips", "gather_sibling"
COPIES = {GATHER: N_DEV - 1, SCATTER: N_DEV - 1, GATHER_CHIPS: 4, GATHER_SIBLING: 3}


def _land_shape(kind, s, axis):
    if kind == SCATTER:
        return (N_DEV,) + s.shape[:axis] + s.shape[axis + 1:]
    return s.shape[:axis] + (N_DEV,) + s.shape[axis:]


def _plan(kind, srcs, lands, axes):
    x, y, c = _mesh_position()
    my_block = _block_of(x, y, c)
    flips = {GATHER_CHIPS: (1, 4, 2, 6), GATHER_SIBLING: (4, 2, 6)}.get(kind, range(1, N_DEV))
    others = [(1 - x if k & 4 else x, 1 - y if k & 2 else y, 1 - c if k & 1 else c) for k in flips]
    remote = []
    for src, land, axis in zip(srcs, lands, axes):
        if kind == SCATTER:
            mine = land.at[my_block]
            remote += [(_at_block(src, axis, _block_of(*peer)), mine, peer, land.at[_block_of(*peer)]) for peer in others]
        elif kind == GATHER_SIBLING:
            for px, py, _ in others:
                mine, theirs = _at_block(land, axis, _block_of(px, py, c)), _at_block(land, axis, _block_of(px, py, 1 - c))
                remote.append((mine, mine, (x, y, 1 - c), theirs))
        else:
            mine = _at_block(land, axis, my_block)
            remote += [(src, mine, peer, _at_block(land, axis, _block_of(*peer))) for peer in others]
    return remote


def _remote(src, dst, send_sems, recv_sems, k, peer):
    return pltpu.make_async_remote_copy(src_ref=src, dst_ref=dst, send_sem=send_sems.at[k], recv_sem=recv_sems.at[k],
                                        device_id=peer, device_id_type=MESH)


_HBM = pl.BlockSpec(memory_space=pltpu.HBM)
_SEM = pl.BlockSpec(memory_space=pltpu.SEMAPHORE)
_SPLIT = dict(has_side_effects=pltpu.SideEffectType.DATAFLOW_SIDE_EFFECTING)


def _landing_zone(kind, s, axis, me):
    land = lax.empty(_land_shape(kind, s, axis), s.dtype)
    if kind == SCATTER:
        return lax.dynamic_update_slice_in_dim(land, lax.dynamic_slice_in_dim(s, me, 1, axis).reshape((1,) + land.shape[1:]), me, 0)
    return lax.dynamic_update_slice_in_dim(land, jnp.expand_dims(s, axis), me, axis)


def _exchange_start(kind, arrays, axes, after, name):
    n = len(arrays)
    if kind == GATHER_SIBLING:
        passed = list(arrays)
    else:
        me = _block_of(*_mesh_position())
        passed = list(arrays) + [_landing_zone(kind, s, a, me) for s, a in zip(arrays, axes)]
    n_sems = n * COPIES[kind]

    def body(*refs):
        land_refs = refs[len(passed) - n:len(passed)]
        send_sems, recv_sems = refs[len(passed) + 1], refs[len(passed) + 2]
        token = refs[-1]
        for k, (src, dst, peer, _) in enumerate(_plan(kind, refs[:n], land_refs, axes)):
            _remote(src, dst, send_sems, recv_sems, k, peer).start()
        token[...] = jnp.zeros_like(token)

    out = pl.pallas_call(
        body, name=name,
        out_shape=(pltpu.SemaphoreType.DMA((n_sems,)), pltpu.SemaphoreType.DMA((n_sems,)),
                   *[pltpu.HBM(a.shape, a.dtype) for a in passed], jax.ShapeDtypeStruct((8, LANES), F32)),
        in_specs=[_HBM] * len(passed) + [pl.BlockSpec(memory_space=pl.ANY)],
        out_specs=(_SEM, _SEM, *[_HBM] * len(passed), pl.BlockSpec(memory_space=pltpu.VMEM)),
        input_output_aliases={i: 2 + i for i in range(len(passed))},
        compiler_params=pltpu.CompilerParams(**_SPLIT),
    )(*[pltpu.with_memory_space_constraint(a, pltpu.HBM) for a in passed], after)
    return (kind, axes, n, out[:-1]), out[-1]


def _exchange_wait(handle, after, name):
    kind, axes, n, (send_sems, recv_sems, *thru) = handle

    def body(*refs):
        land_refs = refs[len(thru) - n:len(thru)]
        send_sems, recv_sems = refs[len(thru)], refs[len(thru) + 1]
        for k, (src, _, peer, arrives) in enumerate(_plan(kind, refs[:n], land_refs, axes)):
            cp = _remote(src, arrives, send_sems, recv_sems, k, peer)
            cp.wait_send()
            cp.wait_recv()

    out = pl.pallas_call(
        body, name=name,
        out_shape=tuple(pltpu.HBM(a.shape, a.dtype) for a in thru),
        in_specs=[_HBM] * len(thru) + [_SEM, _SEM, pl.BlockSpec(memory_space=pl.ANY)], out_specs=(_HBM,) * len(thru),
        input_output_aliases={i: i for i in range(len(thru))},
        compiler_params=pltpu.CompilerParams(**_SPLIT),
    )(*thru, send_sems, recv_sems, after)
    return out[len(thru) - n:]


def _adamw_math(w, g, m, v):
    m = ADAM_B1 * m + (1.0 - ADAM_B1) * g
    v = ADAM_B2 * v + (1.0 - ADAM_B2) * jnp.square(g)
    m_hat = m / (1.0 - ADAM_B1 ** ADAM_STEP)
    v_hat = v / (1.0 - ADAM_B2 ** ADAM_STEP)
    delta = -ADAM_LR * (m_hat / (jnp.sqrt(v_hat) + ADAM_EPS) + ADAM_WD * w)
    return delta, m, v


def _update_tile(rows):
    if rows <= 512:
        return rows
    for tr in (512, 384, 352, 256, 176, 128, 64, 32, 16):
        if rows % tr == 0:
            return tr
    raise ValueError(f"{rows} rows do not tile")


def _adamw(parts, w, m, v, slab, so_far, name):
    rows, c = w.shape
    r = parts.shape[1]
    tr = _update_tile(r)
    first = slab * (r // tr)
    if so_far is None:
        so_far = tuple(lax.empty((rows, c), F32) for _ in range(4))

    def body(p_ref, w_ref, m_ref, v_ref, *refs):
        g_ref, d_ref, mo_ref, vo_ref = refs[4:]
        g = p_ref[0].astype(F32)
        for s in range(1, N_DEV):
            g = g + p_ref[s].astype(F32)
        g_ref[...] = g
        d_ref[...], mo_ref[...], vo_ref[...] = _adamw_math(w_ref[...], g, m_ref[...], v_ref[...])

    out = jax.ShapeDtypeStruct((rows, c), F32)
    tile = pl.BlockSpec((tr, c), lambda i: (first + i, 0))
    return pl.pallas_call(
        body, name=name, grid=(r // tr,), out_shape=(out,) * 4,
        in_specs=[pl.BlockSpec((N_DEV, tr, c), lambda i: (0, i, 0))] + [tile] * 3 + [pl.BlockSpec(memory_space=pl.ANY)] * 4,
        out_specs=(tile,) * 4, input_output_aliases={4 + k: k for k in range(4)}, compiler_params=_params(),
    )(parts, w, m, v, *so_far)


def _adamw_small(parts, picks, weights, name):
    n = len(parts)

    def body(*refs):
        p_refs, wmv, outs = refs[:n], refs[n:4 * n], refs[4 * n:]
        me = _block_of(*_mesh_position())
        for i in range(n):
            g = picks[i](p_refs[i], 0, me)
            for s in range(1, N_DEV):
                g = g + picks[i](p_refs[i], s, me)
            w_ref, m_ref, v_ref = wmv[3 * i:3 * i + 3]
            g_ref, d_ref, mo_ref, vo_ref = outs[4 * i:4 * i + 4]
            g_ref[...] = g
            d_ref[...], mo_ref[...], vo_ref[...] = _adamw_math(w_ref[...], g, m_ref[...], v_ref[...])

    flat = [a for wmv in weights for a in wmv]
    out = pl.pallas_call(
        body, name=name,
        out_shape=tuple(jax.ShapeDtypeStruct(w.shape, F32) for w, _, _ in weights for _ in range(4)),
        compiler_params=pltpu.CompilerParams(vmem_limit_bytes=VMEM_LIMIT),
    )(*parts, *flat)
    return [tuple(out[4 * i:4 * i + 4]) for i in range(n)]


def kernel(x, positions, mix_pre_g, mix_post_g, pool_w, pool_scale, kv_norm_g, w_kv, w_q, w_o, sinks, ffn_pre_g, ffn_post_g, ffn_w_in, ffn_conv_w, ffn_conv_b, ffn_w_out, loss_target, m_mix_pre_g, m_mix_post_g, m_pool_w, m_pool_scale, m_kv_norm_g, m_w_kv, m_w_q, m_w_o, m_sinks, m_ffn_pre_g, m_ffn_post_g, m_ffn_w_in, m_ffn_conv_w, m_ffn_conv_b, m_ffn_w_out, v_mix_pre_g, v_mix_post_g, v_pool_w, v_pool_scale, v_kv_norm_g, v_w_kv, v_w_q, v_w_o, v_sinks, v_ffn_pre_g, v_ffn_post_g, v_ffn_w_in, v_ffn_conv_w, v_ffn_conv_b, v_ffn_w_out):
    depth, d_model = mix_pre_g.shape
    n_a = pool_w.shape[0]
    n_b = w_q.shape[0]
    t = x.shape[1]
    fs = ffn_w_in.shape[2]
    half = N_DEV // 2
    n_heads = d_model // HEAD_DIM
    x0 = x.reshape(t, d_model)
    target = loss_target.reshape(t, d_model)

    inv_freq = 1.0 / (ROPE_THETA ** (jnp.arange(0, HEAD_DIM, 2, dtype=F32) / HEAD_DIM))
    ang = positions.reshape(t).astype(F32)[:, None] * inv_freq
    cos, sin = jnp.cos(ang), jnp.sin(ang)
    cos = jnp.tile(cos, (1, 2 * LANES // HEAD_DIM))
    ssin = jnp.tile(jnp.concatenate([-sin, sin], axis=1), (1, LANES // HEAD_DIM))

    wire = lambda a: a.astype(_WIRE_DTYPE)
    by_hidden = lambda a: a.transpose(0, 2, 1)
    w_in_b, w_out_b = wire(by_hidden(ffn_w_in)), wire(ffn_w_out)
    pool_w_g, pool_scale_g, conv_w_g = _all_gather([wire(pool_w), pool_scale, ffn_conv_w], [2, 0, 0], "gather_first")
    groups = []
    for l in range(depth):
        if l == n_a:
            groups.append(("attn", [wire(w_kv), wire(w_q), wire(w_o)], [0, 1, 1], l))
        if l == 0:
            groups += [("0_in", [w_in_b[:1]], [1], 0), ("0_out", [w_out_b[:1]], [1], 0)]
        else:
            groups.append((l, [w_in_b[l:l + 1], w_out_b[l:l + 1]], [1, 1], l))
    over_ici, to_sibling, tokens, after = {}, {}, [], pool_w_g
    for key, shards, axes, _ in groups:
        over_ici[key], after = _exchange_start(GATHER_CHIPS, shards, axes, after, f"gather_chips_{key}")
        tokens.append(after)
    started = functools.reduce(lambda a, b: a + b, [tk[0, 0] for tk in tokens])

    def pass_on(layer, after):
        sent = jnp.zeros((), F32)
        for key, _, axes, first in groups:
            if first == layer:
                lands = _exchange_wait(over_ici[key], after, f"gather_chips_wait_{key}")
                to_sibling[key], after = _exchange_start(GATHER_SIBLING, lands, axes, after, f"gather_sibling_{key}")
                sent = sent + after[0, 0]
        return sent

    w_in_l, w_out_l = {}, {}
    pool_scale_f = pool_scale_g.transpose(1, 0, 2).reshape(n_a, 1, d_model)
    conv_w_f = conv_w_g.transpose(1, 2, 0, 3)
    pool_w_f = pool_w_g.reshape(n_a, len(POOL_WINDOWS), d_model // len(POOL_WINDOWS), -1)
    conv_b_f = ffn_conv_b.reshape(depth, N_DEV, fs)
    g3 = lambda a: a.reshape(a.shape[0], 1, a.shape[1])
    mix_pre, mix_post, ffn_pre, ffn_post = g3(mix_pre_g) + started, g3(mix_post_g), g3(ffn_pre_g), g3(ffn_post_g)
    kv_g = kv_norm_g.reshape(1, d_model)
    w_kv_f = w_q_f = w_o_f = None

    saved = []
    xc = x0
    kdup = vdup = x_kv = None
    for l in range(depth):
        x_in = xc
        if l < n_a:
            x_mid, dsave, yu = _pool_fwd(x_in, mix_pre, mix_post, pool_w_f, pool_scale_f, l, f"pool_fwd_{l}")
            mixer = (dsave, yu)
        else:
            j = l - n_a
            if j == 0:
                x_kv = x_in
                w_kv_g, w_q_g, w_o_g = _exchange_wait(to_sibling["attn"], x_in, "gather_sibling_wait_attn")
                w_kv_f = w_kv_g.reshape(d_model, -1)
                w_q_f = w_q_g.reshape(n_b, d_model, d_model)
                w_o_f = w_o_g.reshape(n_b, d_model, d_model)
                kdup, vdup = _kv_fwd(x_kv, kv_g, w_kv_f, cos, ssin, "kv_fwd")
            qs = _q_fwd(x_in, mix_pre, w_q_f, cos, ssin, l, j, f"q_fwd_{l}")
            o, lse = _attn_fwd(qs, kdup, vdup, sinks, j, f"attn_fwd_{l}")
            x_mid, mo = _oproj_fwd(x_in, o, w_o_f, mix_post, l, j, f"oproj_fwd_{l}")
            mixer = (qs, o, lse, mo)
        if l == 0:
            pass_on(0, x_mid)
            (w_in_l[l],), (w_out_g,) = (_exchange_wait(to_sibling[k], x_mid, f"gather_sibling_wait_{k}") for k in ("0_in", "0_out"))
        else:
            w_in_l[l], w_out_g = _exchange_wait(to_sibling[l], x_mid, f"gather_sibling_wait_{l}")
        w_out_l[l] = w_out_g.reshape(1, half * fs, d_model)
        xc, u, uc, f = _ffn_fwd(x_mid, ffn_pre, ffn_post, w_in_l[l], conv_w_f, conv_b_f, w_out_l[l], l, 0, f"ffn_fwd_{l}")
        saved.append((x_in, x_mid, u, uc, f, mixer))
        if l + 1 < depth:
            mix_pre = mix_pre + pass_on(l + 1, xc)

    dx, loss_part = _loss_grad(xc, target, "loss")

    gconv_w, gconv_b = [None] * depth, [None] * depth
    gmix_pre, gmix_post, gffn_pre, gffn_post = [None] * depth, [None] * depth, [None] * depth, [None] * depth
    gpool_scale, gsinks = [None] * n_a, [None] * n_b
    dks, dvs = [], []
    gkv_g = None
    whole = lambda cols: pl.BlockSpec((t, cols), lambda c: (0, 0), pipeline_mode=pl.Buffered(1))
    per_out = lambda cols: pl.BlockSpec((None, t, cols), lambda c: (c, 0, 0))
    by_dev = lambda g: g.reshape(N_DEV, -1, g.shape[-1])
    def small_grads():
        cat = lambda rows: jnp.concatenate(rows, axis=0)
        row = lambda a: a.reshape(1, -1)
        everything = lambda ref, s, me: ref[s]
        lanes = pool_scale.shape[1]
        return [("mix_pre_g", cat(gmix_pre), everything, (mix_pre_g, m_mix_pre_g, v_mix_pre_g)),
                ("mix_post_g", cat(gmix_post), everything, (mix_post_g, m_mix_post_g, v_mix_post_g)),
                ("kv_norm_g", gkv_g, everything, (row(kv_norm_g), row(m_kv_norm_g), row(v_kv_norm_g))),
                ("sinks", cat(gsinks), lambda ref, s, me: ref[s, :, pl.ds(0, n_heads)], (sinks, m_sinks, v_sinks)),
                ("ffn_pre_g", cat(gffn_pre), everything, (ffn_pre_g, m_ffn_pre_g, v_ffn_pre_g)),
                ("ffn_post_g", cat(gffn_post), everything, (ffn_post_g, m_ffn_post_g, v_ffn_post_g)),
                ("ffn_conv_b", jnp.stack(gconv_b).reshape(depth, N_DEV * fs), everything, (ffn_conv_b, m_ffn_conv_b, v_ffn_conv_b)),
                ("pool_scale", cat(gpool_scale), lambda ref, s, me: ref[s, :, pl.ds(pl.multiple_of(me * lanes, lanes), lanes)],
                 (pool_scale, m_pool_scale, v_pool_scale)),
                ("ffn_conv_w", jnp.stack(gconv_w).transpose(0, 2, 1, 3), lambda ref, s, me: ref[s, :, me],
                 (ffn_conv_w, m_ffn_conv_w, v_ffn_conv_w))]

    flying = []

    def launch(going, after, name):
        handle, token = _exchange_start(SCATTER, [g for _, _, g, _ in going], [a for _, _, _, a in going], after, name)
        flying.append(([(nm, slab) for nm, slab, _, _ in going], handle))
        return token

    post = mix_post
    ffn_post_b = ffn_post
    token = None
    for l in reversed(range(depth)):
        x_in, x_mid, u, uc, f, mixer = saved[l]
        duc, gout, gconv_b[l], gffn_post[l] = _ffn_bwd_out(dx, f, ffn_post_b, uc, conv_b_f, w_out_l[l], l, 0, f"ffn_bwd_out_{l}")
        going = [("ffn_w_out", l, by_dev(gout), 0)]
        pre = ffn_pre
        if l == 0:
            token = launch(going, gout, "scatter_start_0_out")
            going = []
            pre = pre + token[0, 0]
        dx, du, hf, gconv_w[l], gffn_pre[l] = _ffn_bwd_in(dx, x_mid, pre, duc, u, conv_w_f, w_in_l[l], l, 0, f"ffn_bwd_in_{l}")
        gin = _tn_matmul(du, hf, per_out(fs), whole(d_model), N_DEV, fs, d_model, f"grad_w_in_{l}", _WIRE_DTYPE)
        going.append(("ffn_w_in", l, gin, 0))
        if l == 0:
            token = launch(going, dx, "scatter_start_0_in")
            going = []
            post = post + token[0, 0]
        if l < n_a:
            dsave, yu = mixer
            dx, dyu, gpool_scale[l], gmix_pre[l], gmix_post[l] = _pool_bwd(
                dx, x_in, mix_pre, post, dsave, yu, pool_w_f, pool_scale_f, l, f"pool_bwd_{l}")
            gc = d_model // len(POOL_WINDOWS)
            by_group = pl.BlockSpec((t, gc), lambda c: (0, c))
            gpool = _tn_matmul(dsave, dyu, by_group, by_group, len(POOL_WINDOWS), gc, gc, f"grad_pool_w_{l}", _WIRE_DTYPE)
            going.append(("pool_w", l, gpool.reshape(len(POOL_WINDOWS), N_DEV, -1, gc), 1))
        else:
            j = l - n_a
            qs, o, lse, mo = mixer
            do, go, gmix_post[l] = _oproj_bwd(dx, mo, o, w_o_f, post, l, j, f"oproj_bwd_{l}")
            dqs, dk, dv, gsinks[j] = _attn_bwd(qs, kdup, vdup, sinks, lse, do, j, f"attn_bwd_{l}")
            dks.append(dk)
            dvs.append(dv)
            dx, gq, gmix_pre[l] = _q_bwd(dx, dqs, x_in, mix_pre, w_q_f, cos, ssin, l, j, f"q_bwd_{l}")
            if j == 0:
                dx, gkv, gkv_g = _kv_bwd(dx, x_kv, kv_g, w_kv_f, cos, ssin, dks, dvs, "kv_bwd")
                going.append(("w_kv", 0, by_dev(gkv), 0))
            going += [("w_o", j, by_dev(go), 0), ("w_q", j, by_dev(gq), 0)]
        after = dx
        if l == 0:
            small = small_grads()
            leaving = [g for _, g, _, _ in small] + [jnp.broadcast_to(loss_part, (1, LANES))]
            small_flight, after = _exchange_start(GATHER, leaving, [0] * len(leaving), dx, "gather_small_grads")
        token = launch(going, after, f"scatter_start_{l}")
        ffn_post_b = ffn_post_b + token[0, 0]

    grad_x = dx.reshape(x.shape)

    shard = {"pool_w": (pool_w, m_pool_w, v_pool_w), "w_kv": (w_kv, m_w_kv, v_w_kv), "w_q": (w_q, m_w_q, v_w_q),
             "w_o": (w_o, m_w_o, v_w_o), "ffn_w_in": tuple(by_hidden(a) for a in (ffn_w_in, m_ffn_w_in, v_ffn_w_in)),
             "ffn_w_out": (ffn_w_out, m_ffn_w_out, v_ffn_w_out)}
    big = {}

    def arrive(flights, after):
        for idx, (names, handle) in flights:
            parts = _exchange_wait(handle, after, f"scatter_wait_{idx}")
            for (nm, slab), p in zip(names, parts):
                cols = p.shape[-1]
                w2, m2, v2 = (a.reshape(-1, cols) for a in shard[nm])
                big[nm] = _adamw(p.reshape(N_DEV, -1, cols), w2, m2, v2, slab, big.get(nm), f"adamw_{nm}_{slab}")
                after = big[nm][0]
        return after

    done = arrive(list(enumerate(flying)), token)
    *small_parts, loss_parts = _exchange_wait(small_flight, done, "gather_small_grads_wait")
    loss = jnp.sum(loss_parts[:, 0, 0])
    upd = _adamw_small(small_parts, [pick for _, _, pick, _ in small], [wmv for _, _, _, wmv in small], "adamw_small")
    res = {nm: tuple(r.reshape(shard[nm][0].shape) for r in out) for nm, out in big.items()}
    res["ffn_w_in"] = tuple(by_hidden(r) for r in res["ffn_w_in"])
    for (nm, _, _, _), out in zip(small, upd):
        res[nm] = tuple(a.reshape(kv_norm_g.shape) for a in out) if nm == "kv_norm_g" else out

    order = ["mix_pre_g", "mix_post_g", "pool_w", "pool_scale", "kv_norm_g", "w_kv", "w_q", "w_o", "sinks", "ffn_pre_g",
             "ffn_post_g", "ffn_w_in", "ffn_conv_w", "ffn_conv_b", "ffn_w_out"]
    return (loss, grad_x, *[res[nm][0] for nm in order], *[res[nm][1] for nm in order],
            *[res[nm][2] for nm in order], *[res[nm][3] for nm in order])
```
